```python
import math
import jax
import jax.numpy as jnp
from jax import lax
import numpy as np

D_MODEL = 1024
BATCH = 8
SEQ = 8192
DEPTH = 2

CTX_LEN = 256
GRID_W = 64
EPS = 1e-6

DN_HEADS = 4
DN_HEAD_DIM = 128
DN_WIDTH = DN_HEADS * DN_HEAD_DIM
DN_CHUNK = 64
POOL_WINDOWS = (2, 4, 8, 16)
POOL_GROUPS = len(POOL_WINDOWS)
POOL_WIDTH = D_MODEL // 4
POOL_GROUP_DIM = POOL_WIDTH // POOL_GROUPS
SC_WIDTH = D_MODEL // 4
N_BRANCH = 3
D_FF = ((8 * D_MODEL + 3 * 256 - 1) // (3 * 256)) * 256

OFF_Z = 3 * DN_WIDTH
OFF_A = OFF_Z + DN_WIDTH
OFF_BETA = OFF_A + 2 * DN_HEADS
OFF_POOL = OFF_BETA + 2 * DN_HEADS
OFF_SC = OFF_POOL + POOL_WIDTH
OFF_GATE = OFF_SC + 3 * SC_WIDTH
N_IN = OFF_GATE + N_BRANCH * D_MODEL
IN_SPLITS = (OFF_Z, OFF_A, OFF_BETA, OFF_POOL, OFF_SC, OFF_GATE)

kernel_name = "hybrid_parallel_deltanet_pool_shortconv_dit"


def rmsnorm(x, g):
    xf = x.astype(jnp.float32)
    y = xf * lax.rsqrt(jnp.mean(xf * xf, axis=-1, keepdims=True) + EPS)
    return (y * g.astype(jnp.float32)).astype(x.dtype)


def l2norm(x):
    return x * lax.rsqrt(jnp.sum(x * x, axis=-1, keepdims=True) + EPS)


def conv3(x, w):
    xp = jnp.pad(x, ((0, 0), (1, 1), (0, 0)))
    return xp[:, :-2] * w[0] + xp[:, 1:-1] * w[1] + xp[:, 2:] * w[2]


def decay_gate(p_a, p_b, a_log, dt_bias):
    bn, l, _ = p_a.shape
    a = p_a.astype(jnp.float32).reshape(bn, l, 2, DN_HEADS)
    g = -jnp.exp(a_log.astype(jnp.float32)) * jax.nn.softplus(a + dt_bias.astype(jnp.float32))
    beta = jax.nn.sigmoid(p_b.astype(jnp.float32).reshape(bn, l, 2, DN_HEADS))
    return g, beta


def gated_delta_chunked(q, k, v, g, beta, s0):
    bn, l, h, _ = k.shape
    dv = v.shape[-1]
    n = l // DN_CHUNK

    def chunks(t):
        t = t.reshape(bn, n, DN_CHUNK, h, *t.shape[3:])
        return jnp.moveaxis(t, (1, 3), (0, 2))

    kc, vc, bc = chunks(k), chunks(v), chunks(beta)
    gc = jnp.cumsum(chunks(g), axis=-1)
    idx = jnp.arange(DN_CHUNK)
    incl = idx[:, None] >= idx[None, :]
    strict = idx[:, None] > idx[None, :]
    decay = jnp.exp(jnp.where(incl, gc[..., :, None] - gc[..., None, :], -jnp.inf))
    kb = kc * bc[..., None]
    a = jnp.einsum('nbhid,nbhjd->nbhij', kb, kc) * jnp.where(strict, decay, 0.0)
    eye = jnp.eye(DN_CHUNK, dtype=jnp.float32)
    rhs = jnp.concatenate([vc * bc[..., None], kb * jnp.exp(gc)[..., None]], axis=-1)
    sol = lax.linalg.triangular_solve(eye + a, rhs, left_side=True, lower=True, unit_diagonal=True)
    u, w = sol[..., :dv], sol[..., dv:]
    g_last = gc[..., -1]
    k_state = kc * jnp.exp(g_last[..., None] - gc)[..., None]
    with_output = q is not None
    xs = (u, w, k_state, g_last)
    if with_output:
        qc = chunks(q)
        q_dec = qc * jnp.exp(gc)[..., None]
        a_qk = jnp.einsum('nbhid,nbhjd->nbhij', qc, kc) * decay
        xs = xs + (q_dec, a_qk)

    def step(s, xs_i):
        u_i, w_i, k_i, gl_i = xs_i[:4]
        v_new = u_i - jnp.einsum('bhck,bhkv->bhcv', w_i, s)
        s_next = s * jnp.exp(gl_i)[..., None, None] + jnp.einsum('bhck,bhcv->bhkv', k_i, v_new)
        if with_output:
            qd_i, aqk_i = xs_i[4:]
            o_i = jnp.einsum('bhck,bhkv->bhcv', qd_i, s) + jnp.einsum('bhij,bhjv->bhiv', aqk_i, v_new)
            return s_next, o_i
        return s_next, None

    s_fin, o = lax.scan(step, s0, xs)
    if with_output:
        o = jnp.moveaxis(o, (0, 2), (1, 3)).reshape(bn, l, h, dv)
    return o, s_fin


def dn_bidir(q, k, v, g, beta, s0_f, s0_b):
    rev = lambda t: jnp.flip(t, axis=1)
    o_f, s_f = gated_delta_chunked(q, k, v, g[:, :, 0], beta[:, :, 0], s0_f)
    o_b, s_b = gated_delta_chunked(None if q is None else rev(q), rev(k), rev(v),
                                   rev(g[:, :, 1]), rev(beta[:, :, 1]), s0_b)
    o = None if q is None else o_f + rev(o_b)
    return o, s_f, s_b


def box_mean(x, w, axis):
    l = x.shape[axis]
    lo = w // 2
    hi = w - 1 - lo
    cs = jnp.cumsum(x, axis=axis)
    cs = jnp.concatenate([jnp.zeros_like(lax.slice_in_dim(cs, 0, 1, axis=axis)), cs], axis=axis)
    pos = jnp.arange(l)
    start = jnp.clip(pos - lo, 0, l)
    end = jnp.clip(pos + hi + 1, 0, l)
    total = jnp.take(cs, end, axis=axis) - jnp.take(cs, start, axis=axis)
    shape = [1] * x.ndim
    shape[axis] = l
    return total / (end - start).astype(x.dtype).reshape(shape)


def pool_mixer(u, pool_w, pool_scale, rows):
    bn, l, _ = u.shape
    uf = u.astype(jnp.float32)
    outs = []
    for gi, w in enumerate(POOL_WINDOWS):
        ug = uf[..., gi * POOL_GROUP_DIM:(gi + 1) * POOL_GROUP_DIM]
        if rows is None:
            m = box_mean(ug, w, 1)
        else:
            ug2 = ug.reshape(bn, rows, GRID_W, POOL_GROUP_DIM)
            m = box_mean(box_mean(ug2, w, 1), w, 2).reshape(bn, l, POOL_GROUP_DIM)
        outs.append(m - ug)
    d = jnp.stack(outs, axis=2).astype(u.dtype)
    y = jnp.einsum('blgc,gcd->blgd', d, pool_w).reshape(bn, l, POOL_WIDTH)
    return y * pool_scale


def shortconv_mixer(p, conv_w):
    xin, gate_b, gate_c = jnp.split(p, 3, axis=-1)
    return gate_b * conv3(gate_c * xin, conv_w)


def mixer(h, lp, s0_f, s0_b, rows):
    bn, l, _ = h.shape
    p = h @ lp['w_in']
    p_qkv, p_z, p_a, p_b, p_pool, p_sc, p_gate = jnp.split(p, IN_SPLITS, axis=-1)
    qkv = jax.nn.silu(conv3(p_qkv, lp['dn_conv_w'])).astype(jnp.float32)
    qkv = qkv.reshape(bn, l, 3, DN_HEADS, DN_HEAD_DIM)
    q = l2norm(qkv[:, :, 0]) * (DN_HEAD_DIM ** -0.5)
    k = l2norm(qkv[:, :, 1])
    v = qkv[:, :, 2]
    g, beta = decay_gate(p_a, p_b, lp['dn_a_log'], lp['dn_dt_bias'])
    o, s_f, s_b = dn_bidir(q, k, v, g, beta, s0_f, s0_b)
    z = p_z.astype(jnp.float32).reshape(bn, l, DN_HEADS, DN_HEAD_DIM)
    o = rmsnorm(o, lp['dn_norm_g']) * jax.nn.silu(z)
    y_a = o.reshape(bn, l, DN_WIDTH).astype(h.dtype) @ lp['w_br_a']
    y_b = pool_mixer(p_pool, lp['pool_w'], lp['pool_scale'], rows) @ lp['w_br_b']
    y_c = shortconv_mixer(p_sc, lp['sc_conv_w']) @ lp['w_br_c']
    gates = jax.nn.sigmoid(p_gate.reshape(bn, l, N_BRANCH, D_MODEL))
    y = gates[:, :, 0] * y_a + gates[:, :, 1] * y_b + gates[:, :, 2] * y_c
    return y @ lp['w_o'], s_f, s_b


def context_states(h, lp, s0):
    bn, l, _ = h.shape
    w = lp['w_in']
    kv = jax.nn.silu(conv3(h @ w[:, DN_WIDTH:OFF_Z], lp['dn_conv_w'][:, DN_WIDTH:])).astype(jnp.float32)
    kv = kv.reshape(bn, l, 2, DN_HEADS, DN_HEAD_DIM)
    k = l2norm(kv[:, :, 0])
    v = kv[:, :, 1]
    p_a, p_b = jnp.split(h @ w[:, OFF_A:OFF_POOL], 2, axis=-1)
    g, beta = decay_gate(p_a, p_b, lp['dn_a_log'], lp['dn_dt_bias'])
    _, s_f, s_b = dn_bidir(None, k, v, g, beta, s0, s0)
    return s_f, s_b


def swiglu(h, w_gu, w_down):
    gate, up = jnp.split(h @ w_gu, 2, axis=-1)
    return (jax.nn.silu(gate) * up) @ w_down


def modulate(x, norm_g, shift, scale):
    return rmsnorm(x, norm_g) * (1 + scale) + shift


def _fwd_setup_inputs(seed: int = 0) -> dict:
    key = jax.random.key(seed)
    ks = jax.random.split(key, 24)
    f32 = jnp.float32
    nrm = lambda k, shape, s: jax.random.normal(k, shape, f32) * s
    L = DEPTH
    dt = jnp.exp(jax.random.uniform(ks[9], (L, 2, DN_HEADS), f32, math.log(1e-3), math.log(1e-1)))
    return {
        'x': nrm(ks[0], (BATCH, SEQ, D_MODEL), 1.0),
        'c': nrm(ks[1], (BATCH, D_MODEL), 1.0),
        'ctx': nrm(ks[2], (BATCH, CTX_LEN, D_MODEL), 1.0),
        'c_ctx': nrm(ks[3], (D_MODEL,), 1.0),
        'w_ada': nrm(ks[4], (L, D_MODEL, 6 * D_MODEL), 0.5 * D_MODEL ** -0.5),
        'b_ada': nrm(ks[5], (L, 6 * D_MODEL), 0.01),
        'norm1_g': 1.0 + nrm(ks[6], (L, D_MODEL), 0.1),
        'norm2_g': 1.0 + nrm(ks[7], (L, D_MODEL), 0.1),
        'w_in': nrm(ks[8], (L, D_MODEL, N_IN), D_MODEL ** -0.5),
        'dn_conv_w': nrm(ks[10], (L, 3, 3 * DN_WIDTH), 3 ** -0.5),
        'dn_a_log': jnp.log(jax.random.uniform(ks[11], (L, 2, DN_HEADS), f32, 1.0, 16.0)),
        'dn_dt_bias': dt + jnp.log(-jnp.expm1(-dt)),
        'dn_norm_g': 1.0 + nrm(ks[12], (L, DN_HEAD_DIM), 0.1),
        'pool_w': nrm(ks[13], (L, POOL_GROUPS, POOL_GROUP_DIM, POOL_GROUP_DIM), POOL_GROUP_DIM ** -0.5),
        'pool_scale': 1.0 + nrm(ks[14], (L, POOL_WIDTH), 0.1),
        'sc_conv_w': nrm(ks[15], (L, 3, SC_WIDTH), 3 ** -0.5),
        'w_br_a': nrm(ks[16], (L, DN_WIDTH, D_MODEL), DN_WIDTH ** -0.5),
        'w_br_b': nrm(ks[17], (L, POOL_WIDTH, D_MODEL), POOL_WIDTH ** -0.5),
        'w_br_c': nrm(ks[18], (L, SC_WIDTH, D_MODEL), SC_WIDTH ** -0.5),
        'w_o': nrm(ks[19], (L, D_MODEL, D_MODEL), D_MODEL ** -0.5),
        'w_gu': nrm(ks[20], (L, D_MODEL, 2 * D_FF), D_MODEL ** -0.5),
        'w_down': nrm(ks[21], (L, D_FF, D_MODEL), D_FF ** -0.5),
        'final_norm_g': 1.0 + nrm(ks[22], (D_MODEL,), 0.1),
    }


def _fwd_reference(x, c, ctx, c_ctx, w_ada, b_ada, norm1_g, norm2_g, w_in, dn_conv_w, dn_a_log,
              dn_dt_bias, dn_norm_g, pool_w, pool_scale, sc_conv_w, w_br_a, w_br_b, w_br_c,
              w_o, w_gu, w_down, final_norm_g):
    bn = x.shape[0]
    rows = x.shape[1] // GRID_W
    s0 = jnp.zeros((bn, DN_HEADS, DN_HEAD_DIM, DN_HEAD_DIM), jnp.float32)
    for l in range(DEPTH):
        lp = {'w_in': w_in[l], 'dn_conv_w': dn_conv_w[l], 'dn_a_log': dn_a_log[l],
              'dn_dt_bias': dn_dt_bias[l], 'dn_norm_g': dn_norm_g[l], 'pool_w': pool_w[l],
              'pool_scale': pool_scale[l], 'sc_conv_w': sc_conv_w[l], 'w_br_a': w_br_a[l],
              'w_br_b': w_br_b[l], 'w_br_c': w_br_c[l], 'w_o': w_o[l]}
        mod = jax.nn.silu(c) @ w_ada[l] + b_ada[l]
        sh1, sc1, g1, sh2, sc2, g2 = jnp.split(mod[:, None, :], 6, axis=-1)
        mod_c = jax.nn.silu(c_ctx) @ w_ada[l] + b_ada[l]
        sh1c, sc1c, g1c, sh2c, sc2c, g2c = jnp.split(mod_c, 6)
        hc = modulate(ctx, norm1_g[l], sh1c, sc1c)
        if l == DEPTH - 1:
            s_f, s_b = context_states(hc, lp, s0)
        else:
            mix_c, s_f, s_b = mixer(hc, lp, s0, s0, None)
            ctx = ctx + g1c * mix_c
            ctx = ctx + g2c * swiglu(modulate(ctx, norm2_g[l], sh2c, sc2c), w_gu[l], w_down[l])
        h = modulate(x, norm1_g[l], sh1, sc1)
        mix, _, _ = mixer(h, lp, s_f, s_b, rows)
        x = x + g1 * mix
        x = x + g2 * swiglu(modulate(x, norm2_g[l], sh2, sc2), w_gu[l], w_down[l])
    return rmsnorm(x, final_norm_g)


import jax as _jax
import jax.numpy as _jnp

TWIN_FORMAT = 'train_step'
FWD_PARAMS = ['x', 'c', 'ctx', 'c_ctx', 'w_ada', 'b_ada', 'norm1_g', 'norm2_g', 'w_in', 'dn_conv_w', 'dn_a_log', 'dn_dt_bias', 'dn_norm_g', 'pool_w', 'pool_scale', 'sc_conv_w', 'w_br_a', 'w_br_b', 'w_br_c', 'w_o', 'w_gu', 'w_down', 'final_norm_g']
TWIN_WEIGHTS = ['c_ctx', 'w_ada', 'b_ada', 'norm1_g', 'norm2_g', 'w_in', 'dn_conv_w', 'dn_a_log', 'dn_dt_bias', 'dn_norm_g', 'pool_w', 'pool_scale', 'sc_conv_w', 'w_br_a', 'w_br_b', 'w_br_c', 'w_o', 'w_gu', 'w_down', 'final_norm_g']
TWIN_DIFF_INPUT = 'x'
TWIN_INPUTS = ['x', 'c', 'ctx', 'c_ctx', 'w_ada', 'b_ada', 'norm1_g', 'norm2_g', 'w_in', 'dn_conv_w', 'dn_a_log', 'dn_dt_bias', 'dn_norm_g', 'pool_w', 'pool_scale', 'sc_conv_w', 'w_br_a', 'w_br_b', 'w_br_c', 'w_o', 'w_gu', 'w_down', 'final_norm_g', 'loss_target', 'm_c_ctx', 'm_w_ada', 'm_b_ada', 'm_norm1_g', 'm_norm2_g', 'm_w_in', 'm_dn_conv_w', 'm_dn_a_log', 'm_dn_dt_bias', 'm_dn_norm_g', 'm_pool_w', 'm_pool_scale', 'm_sc_conv_w', 'm_w_br_a', 'm_w_br_b', 'm_w_br_c', 'm_w_o', 'm_w_gu', 'm_w_down', 'm_final_norm_g', 'v_c_ctx', 'v_w_ada', 'v_b_ada', 'v_norm1_g', 'v_norm2_g', 'v_w_in', 'v_dn_conv_w', 'v_dn_a_log', 'v_dn_dt_bias', 'v_dn_norm_g', 'v_pool_w', 'v_pool_scale', 'v_sc_conv_w', 'v_w_br_a', 'v_w_br_b', 'v_w_br_c', 'v_w_o', 'v_w_gu', 'v_w_down', 'v_final_norm_g']
TWIN_OUTPUTS = ['loss', 'grad_x', 'grad_c_ctx', 'grad_w_ada', 'grad_b_ada', 'grad_norm1_g', 'grad_norm2_g', 'grad_w_in', 'grad_dn_conv_w', 'grad_dn_a_log', 'grad_dn_dt_bias', 'grad_dn_norm_g', 'grad_pool_w', 'grad_pool_scale', 'grad_sc_conv_w', 'grad_w_br_a', 'grad_w_br_b', 'grad_w_br_c', 'grad_w_o', 'grad_w_gu', 'grad_w_down', 'grad_final_norm_g', 'delta_c_ctx', 'delta_w_ada', 'delta_b_ada', 'delta_norm1_g', 'delta_norm2_g', 'delta_w_in', 'delta_dn_conv_w', 'delta_dn_a_log', 'delta_dn_dt_bias', 'delta_dn_norm_g', 'delta_pool_w', 'delta_pool_scale', 'delta_sc_conv_w', 'delta_w_br_a', 'delta_w_br_b', 'delta_w_br_c', 'delta_w_o', 'delta_w_gu', 'delta_w_down', 'delta_final_norm_g', 'new_m_c_ctx', 'new_m_w_ada', 'new_m_b_ada', 'new_m_norm1_g', 'new_m_norm2_g', 'new_m_w_in', 'new_m_dn_conv_w', 'new_m_dn_a_log', 'new_m_dn_dt_bias', 'new_m_dn_norm_g', 'new_m_pool_w', 'new_m_pool_scale', 'new_m_sc_conv_w', 'new_m_w_br_a', 'new_m_w_br_b', 'new_m_w_br_c', 'new_m_w_o', 'new_m_w_gu', 'new_m_w_down', 'new_m_final_norm_g', 'new_v_c_ctx', 'new_v_w_ada', 'new_v_b_ada', 'new_v_norm1_g', 'new_v_norm2_g', 'new_v_w_in', 'new_v_dn_conv_w', 'new_v_dn_a_log', 'new_v_dn_dt_bias', 'new_v_dn_norm_g', 'new_v_pool_w', 'new_v_pool_scale', 'new_v_sc_conv_w', 'new_v_w_br_a', 'new_v_w_br_b', 'new_v_w_br_c', 'new_v_w_o', 'new_v_w_gu', 'new_v_w_down', 'new_v_final_norm_g']
TWIN_LEAF_KINDS = {'loss': 'loss', 'grad_x': 'grad_x', 'grad_c_ctx': 'grad_w', 'grad_w_ada': 'grad_w', 'grad_b_ada': 'grad_w', 'grad_norm1_g': 'grad_w', 'grad_norm2_g': 'grad_w', 'grad_w_in': 'grad_w', 'grad_dn_conv_w': 'grad_w', 'grad_dn_a_log': 'grad_w', 'grad_dn_dt_bias': 'grad_w', 'grad_dn_norm_g': 'grad_w', 'grad_pool_w': 'grad_w', 'grad_pool_scale': 'grad_w', 'grad_sc_conv_w': 'grad_w', 'grad_w_br_a': 'grad_w', 'grad_w_br_b': 'grad_w', 'grad_w_br_c': 'grad_w', 'grad_w_o': 'grad_w', 'grad_w_gu': 'grad_w', 'grad_w_down': 'grad_w', 'grad_final_norm_g': 'grad_w', 'delta_c_ctx': 'delta_w', 'delta_w_ada': 'delta_w', 'delta_b_ada': 'delta_w', 'delta_norm1_g': 'delta_w', 'delta_norm2_g': 'delta_w', 'delta_w_in': 'delta_w', 'delta_dn_conv_w': 'delta_w', 'delta_dn_a_log': 'delta_w', 'delta_dn_dt_bias': 'delta_w', 'delta_dn_norm_g': 'delta_w', 'delta_pool_w': 'delta_w', 'delta_pool_scale': 'delta_w', 'delta_sc_conv_w': 'delta_w', 'delta_w_br_a': 'delta_w', 'delta_w_br_b': 'delta_w', 'delta_w_br_c': 'delta_w', 'delta_w_o': 'delta_w', 'delta_w_gu': 'delta_w', 'delta_w_down': 'delta_w', 'delta_final_norm_g': 'delta_w', 'new_m_c_ctx': 'new_m', 'new_m_w_ada': 'new_m', 'new_m_b_ada': 'new_m', 'new_m_norm1_g': 'new_m', 'new_m_norm2_g': 'new_m', 'new_m_w_in': 'new_m', 'new_m_dn_conv_w': 'new_m', 'new_m_dn_a_log': 'new_m', 'new_m_dn_dt_bias': 'new_m', 'new_m_dn_norm_g': 'new_m', 'new_m_pool_w': 'new_m', 'new_m_pool_scale': 'new_m', 'new_m_sc_conv_w': 'new_m', 'new_m_w_br_a': 'new_m', 'new_m_w_br_b': 'new_m', 'new_m_w_br_c': 'new_m', 'new_m_w_o': 'new_m', 'new_m_w_gu': 'new_m', 'new_m_w_down': 'new_m', 'new_m_final_norm_g': 'new_m', 'new_v_c_ctx': 'new_v', 'new_v_w_ada': 'new_v', 'new_v_b_ada': 'new_v', 'new_v_norm1_g': 'new_v', 'new_v_norm2_g': 'new_v', 'new_v_w_in': 'new_v', 'new_v_dn_conv_w': 'new_v', 'new_v_dn_a_log': 'new_v', 'new_v_dn_dt_bias': 'new_v', 'new_v_dn_norm_g': 'new_v', 'new_v_pool_w': 'new_v', 'new_v_pool_scale': 'new_v', 'new_v_sc_conv_w': 'new_v', 'new_v_w_br_a': 'new_v', 'new_v_w_br_b': 'new_v', 'new_v_w_br_c': 'new_v', 'new_v_w_o': 'new_v', 'new_v_w_gu': 'new_v', 'new_v_w_down': 'new_v', 'new_v_final_norm_g': 'new_v'}


def _forward(args):
    return _fwd_reference(*[args[k] for k in FWD_PARAMS])


def _output_shape():
    def fwd():
        inp = _fwd_setup_inputs(0)
        return _fwd_reference(*[inp[k] for k in FWD_PARAMS])
    out = _jax.eval_shape(fwd)
    return out.shape, out.dtype

N_MICROBATCH = 1
ADAM_LR = 0.001
ADAM_B1 = 0.9
ADAM_B2 = 0.999
ADAM_EPS = 1e-08
ADAM_WD = 0.01
ADAM_STEP = 10
PER_EXAMPLE_BATCH_AXIS = {'x': 0, 'c': 0, 'ctx': 0, 'loss_target': 0}
SHARED_INPUTS = []
_WEIGHT_DTYPES = {'c_ctx': _jnp.float32, 'w_ada': _jnp.float32, 'b_ada': _jnp.float32, 'norm1_g': _jnp.float32, 'norm2_g': _jnp.float32, 'w_in': _jnp.float32, 'dn_conv_w': _jnp.float32, 'dn_a_log': _jnp.float32, 'dn_dt_bias': _jnp.float32, 'dn_norm_g': _jnp.float32, 'pool_w': _jnp.float32, 'pool_scale': _jnp.float32, 'sc_conv_w': _jnp.float32, 'w_br_a': _jnp.float32, 'w_br_b': _jnp.float32, 'w_br_c': _jnp.float32, 'w_o': _jnp.float32, 'w_gu': _jnp.float32, 'w_down': _jnp.float32, 'final_norm_g': _jnp.float32}
MOMENT_SCALE = {'c_ctx': 3.235931e-03, 'w_ada': 2.236240e-01, 'b_ada': 4.996951e-01, 'norm1_g': 1.167797e-01, 'norm2_g': 7.442907e-02, 'w_in': 4.877687e-02, 'dn_conv_w': 2.889280e-02, 'dn_a_log': 7.503229e-02, 'dn_dt_bias': 7.264307e-02, 'dn_norm_g': 8.425008e-02, 'pool_w': 9.038502e-02, 'pool_scale': 9.550177e-02, 'sc_conv_w': 1.120951e-01, 'w_br_a': 2.897825e-02, 'w_br_b': 4.753504e-02, 'w_br_c': 5.880721e-02, 'w_o': 8.236849e-02, 'w_gu': 3.303961e-02, 'w_down': 5.422687e-02, 'final_norm_g': 6.445240e+01}


def _to_microbatches(a, axis):
    t = _jnp.moveaxis(a, axis, 0)
    t = t.reshape((N_MICROBATCH, t.shape[0] // N_MICROBATCH) + t.shape[1:])
    return _jnp.moveaxis(t, 1, axis + 1)


def setup_inputs(seed: int = 0) -> dict:
    inp = _fwd_setup_inputs(seed)
    key = _jax.random.fold_in(_jax.random.key(seed), 7919)
    shape, _ = _output_shape()
    out = dict(inp)
    out["loss_target"] = _jax.random.normal(_jax.random.fold_in(key, 0), shape, _jnp.float32)
    for i, name in enumerate(TWIN_WEIGHTS):
        w = inp[name].astype(_jnp.float32)
        if MOMENT_SCALE is None:
            s = _jnp.sqrt(_jnp.mean(_jnp.square(w)) + 1e-30)
        else:
            s = MOMENT_SCALE[name]
        km, kv = _jax.random.split(_jax.random.fold_in(key, i + 1))
        out[name] = w
        out["m_" + name] = s * _jax.random.normal(km, w.shape, _jnp.float32)
        out["v_" + name] = (s * s) * _jax.random.uniform(kv, w.shape, _jnp.float32, 0.5, 1.5)
    if N_MICROBATCH > 1:
        for name, axis in PER_EXAMPLE_BATCH_AXIS.items():
            out[name] = _to_microbatches(out[name], axis)
    return {'x': out['x'], 'c': out['c'], 'ctx': out['ctx'], 'c_ctx': out['c_ctx'], 'w_ada': out['w_ada'], 'b_ada': out['b_ada'], 'norm1_g': out['norm1_g'], 'norm2_g': out['norm2_g'], 'w_in': out['w_in'], 'dn_conv_w': out['dn_conv_w'], 'dn_a_log': out['dn_a_log'], 'dn_dt_bias': out['dn_dt_bias'], 'dn_norm_g': out['dn_norm_g'], 'pool_w': out['pool_w'], 'pool_scale': out['pool_scale'], 'sc_conv_w': out['sc_conv_w'], 'w_br_a': out['w_br_a'], 'w_br_b': out['w_br_b'], 'w_br_c': out['w_br_c'], 'w_o': out['w_o'], 'w_gu': out['w_gu'], 'w_down': out['w_down'], 'final_norm_g': out['final_norm_g'], 'loss_target': out['loss_target'], 'm_c_ctx': out['m_c_ctx'], 'm_w_ada': out['m_w_ada'], 'm_b_ada': out['m_b_ada'], 'm_norm1_g': out['m_norm1_g'], 'm_norm2_g': out['m_norm2_g'], 'm_w_in': out['m_w_in'], 'm_dn_conv_w': out['m_dn_conv_w'], 'm_dn_a_log': out['m_dn_a_log'], 'm_dn_dt_bias': out['m_dn_dt_bias'], 'm_dn_norm_g': out['m_dn_norm_g'], 'm_pool_w': out['m_pool_w'], 'm_pool_scale': out['m_pool_scale'], 'm_sc_conv_w': out['m_sc_conv_w'], 'm_w_br_a': out['m_w_br_a'], 'm_w_br_b': out['m_w_br_b'], 'm_w_br_c': out['m_w_br_c'], 'm_w_o': out['m_w_o'], 'm_w_gu': out['m_w_gu'], 'm_w_down': out['m_w_down'], 'm_final_norm_g': out['m_final_norm_g'], 'v_c_ctx': out['v_c_ctx'], 'v_w_ada': out['v_w_ada'], 'v_b_ada': out['v_b_ada'], 'v_norm1_g': out['v_norm1_g'], 'v_norm2_g': out['v_norm2_g'], 'v_w_in': out['v_w_in'], 'v_dn_conv_w': out['v_dn_conv_w'], 'v_dn_a_log': out['v_dn_a_log'], 'v_dn_dt_bias': out['v_dn_dt_bias'], 'v_dn_norm_g': out['v_dn_norm_g'], 'v_pool_w': out['v_pool_w'], 'v_pool_scale': out['v_pool_scale'], 'v_sc_conv_w': out['v_sc_conv_w'], 'v_w_br_a': out['v_w_br_a'], 'v_w_br_b': out['v_w_br_b'], 'v_w_br_c': out['v_w_br_c'], 'v_w_o': out['v_w_o'], 'v_w_gu': out['v_w_gu'], 'v_w_down': out['v_w_down'], 'v_final_norm_g': out['v_final_norm_g']}


def _loss(weights, diff, rest, loss_target):
    with _jax.named_scope("forward"):
        args = {**rest, TWIN_DIFF_INPUT: diff, **{k: w.astype(_WEIGHT_DTYPES[k]) for k, w in weights.items()}}
        y = _forward(args)
    with _jax.named_scope("loss_head"):
        err = _jnp.square(y.astype(_jnp.float32) - loss_target)
        return 0.5 * _jnp.sum(_jnp.mean(err, axis=-1)) if err.ndim else 0.5 * err


def _adamw(w, g, m, v):
    m = ADAM_B1 * m + (1.0 - ADAM_B1) * g
    v = ADAM_B2 * v + (1.0 - ADAM_B2) * _jnp.square(g)
    m_hat = m / (1.0 - ADAM_B1 ** ADAM_STEP)
    v_hat = v / (1.0 - ADAM_B2 ** ADAM_STEP)
    delta = -ADAM_LR * (m_hat / (_jnp.sqrt(v_hat) + ADAM_EPS) + ADAM_WD * w)
    return delta, m, v


def reference(x, c, ctx, c_ctx, w_ada, b_ada, norm1_g, norm2_g, w_in, dn_conv_w, dn_a_log, dn_dt_bias, dn_norm_g, pool_w, pool_scale, sc_conv_w, w_br_a, w_br_b, w_br_c, w_o, w_gu, w_down, final_norm_g, loss_target, m_c_ctx, m_w_ada, m_b_ada, m_norm1_g, m_norm2_g, m_w_in, m_dn_conv_w, m_dn_a_log, m_dn_dt_bias, m_dn_norm_g, m_pool_w, m_pool_scale, m_sc_conv_w, m_w_br_a, m_w_br_b, m_w_br_c, m_w_o, m_w_gu, m_w_down, m_final_norm_g, v_c_ctx, v_w_ada, v_b_ada, v_norm1_g, v_norm2_g, v_w_in, v_dn_conv_w, v_dn_a_log, v_dn_dt_bias, v_dn_norm_g, v_pool_w, v_pool_scale, v_sc_conv_w, v_w_br_a, v_w_br_b, v_w_br_c, v_w_o, v_w_gu, v_w_down, v_final_norm_g):
    given = dict(x=x, c=c, ctx=ctx, c_ctx=c_ctx, w_ada=w_ada, b_ada=b_ada, norm1_g=norm1_g, norm2_g=norm2_g, w_in=w_in, dn_conv_w=dn_conv_w, dn_a_log=dn_a_log, dn_dt_bias=dn_dt_bias, dn_norm_g=dn_norm_g, pool_w=pool_w, pool_scale=pool_scale, sc_conv_w=sc_conv_w, w_br_a=w_br_a, w_br_b=w_br_b, w_br_c=w_br_c, w_o=w_o, w_gu=w_gu, w_down=w_down, final_norm_g=final_norm_g, loss_target=loss_target, m_c_ctx=m_c_ctx, m_w_ada=m_w_ada, m_b_ada=m_b_ada, m_norm1_g=m_norm1_g, m_norm2_g=m_norm2_g, m_w_in=m_w_in, m_dn_conv_w=m_dn_conv_w, m_dn_a_log=m_dn_a_log, m_dn_dt_bias=m_dn_dt_bias, m_dn_norm_g=m_dn_norm_g, m_pool_w=m_pool_w, m_pool_scale=m_pool_scale, m_sc_conv_w=m_sc_conv_w, m_w_br_a=m_w_br_a, m_w_br_b=m_w_br_b, m_w_br_c=m_w_br_c, m_w_o=m_w_o, m_w_gu=m_w_gu, m_w_down=m_w_down, m_final_norm_g=m_final_norm_g, v_c_ctx=v_c_ctx, v_w_ada=v_w_ada, v_b_ada=v_b_ada, v_norm1_g=v_norm1_g, v_norm2_g=v_norm2_g, v_w_in=v_w_in, v_dn_conv_w=v_dn_conv_w, v_dn_a_log=v_dn_a_log, v_dn_dt_bias=v_dn_dt_bias, v_dn_norm_g=v_dn_norm_g, v_pool_w=v_pool_w, v_pool_scale=v_pool_scale, v_sc_conv_w=v_sc_conv_w, v_w_br_a=v_w_br_a, v_w_br_b=v_w_br_b, v_w_br_c=v_w_br_c, v_w_o=v_w_o, v_w_gu=v_w_gu, v_w_down=v_w_down, v_final_norm_g=v_final_norm_g)
    weights = {n: given[n] for n in TWIN_WEIGHTS}
    shared = {n: given[n] for n in SHARED_INPUTS}
    per_example = {n: given[n] for n in ['x', 'c', 'ctx']}
    grad_fn = _jax.value_and_grad(_loss, argnums=(0, 1))

    def one_microbatch(ex, loss_target):
        ex = dict(ex)
        diff = ex.pop(TWIN_DIFF_INPUT)
        return grad_fn(weights, diff, {**shared, **ex}, loss_target)

    if N_MICROBATCH == 1:
        loss, (grad_w, grad_x) = one_microbatch(per_example, given["loss_target"])
    else:
        def body(carry, xs):
            loss_sum, grad_sum = carry
            l_k, (gw_k, gx_k) = one_microbatch(xs[0], xs[1])
            with _jax.named_scope("update"):
                return (loss_sum + l_k, _jax.tree.map(_jnp.add, grad_sum, gw_k)), gx_k

        init = (_jnp.zeros((), _jnp.float32), _jax.tree.map(_jnp.zeros_like, weights))
        (loss, grad_w), grad_x = _jax.lax.scan(body, init, (per_example, given["loss_target"]))
    with _jax.named_scope("update"):
        delta_w, new_m, new_v = {}, {}, {}
        for n in TWIN_WEIGHTS:
            delta_w[n], new_m[n], new_v[n] = _adamw(weights[n], grad_w[n], given["m_" + n], given["v_" + n])
    return (loss, grad_x, *[grad_w[n] for n in TWIN_WEIGHTS], *[delta_w[n] for n in TWIN_WEIGHTS],
            *[new_m[n] for n in TWIN_WEIGHTS], *[new_v[n] for n in TWIN_WEIGHTS])
```

```python
import functools
import math

import numpy as np
import jax
import jax.numpy as jnp
from jax import lax
from jax.experimental import pallas as pl
from jax.experimental.pallas import tpu as pltpu

F32 = jnp.float32
MXU_DTYPE = jnp.bfloat16
HIGHEST = lax.Precision.HIGHEST

DN_HEADS = 4
HEAD_DIM = 128
DN_WIDTH = DN_HEADS * HEAD_DIM
DN_CHUNK = 64
GRID_W = 64
EPS = 1e-6
POOL_WINDOWS = (2, 4, 8, 16)
N_DIRHEAD = 2 * DN_HEADS
ADAM_LR, ADAM_B1, ADAM_B2, ADAM_EPS, ADAM_WD, ADAM_STEP = 0.001, 0.9, 0.999, 1e-08, 0.01, 10

LANES = 128
SUBLANES = 8
ROW_TILE = 256
VMEM_LIMIT = 56 * 1024 * 1024

MESH_ID = pl.DeviceIdType.MESH
NN, NT, TN = ((1,), (0,)), ((1,), (1,)), ((0,), (0,))


def _params(sem=None):
    return pltpu.CompilerParams(dimension_semantics=sem, vmem_limit_bytes=VMEM_LIMIT)


def _pick(n, cands):
    for c in cands:
        if c <= n and n % c == 0:
            return c
    return n


def _mm(name, a, b, dims, out_dtype=F32):
    if dims == NN:
        (m, kk), (_, n) = a.shape, b.shape
    elif dims == NT:
        (m, kk), (n, _) = a.shape, b.shape
    else:
        (kk, m), (_, n) = a.shape, b.shape
    tm = _pick(m, (768, 1024, 512, 256, 128, 64, 32, 16, 8))
    tn = _pick(n, (1024, 1408, 768, 896, 512, 256, 128))
    tk = _pick(kk, (1024, 1408, 768, 512, 256, 128))
    gi, gj, gl = m // tm, n // tn, kk // tk
    if dims == NN:
        a_spec = pl.BlockSpec((tm, tk), lambda i, j, l: (i, l))
        b_spec = pl.BlockSpec((tk, tn), lambda i, j, l: (l, j))
    elif dims == NT:
        a_spec = pl.BlockSpec((tm, tk), lambda i, j, l: (i, l))
        b_spec = pl.BlockSpec((tn, tk), lambda i, j, l: (j, l))
    else:
        a_spec = pl.BlockSpec((tk, tm), lambda i, j, l: (l, i))
        b_spec = pl.BlockSpec((tk, tn), lambda i, j, l: (l, j))
    direct = gl == 1
    use_acc = (not direct) and out_dtype != F32

    def body(a_ref, b_ref, o_ref, *scratch):
        part = lax.dot_general(a_ref[...].astype(MXU_DTYPE), b_ref[...].astype(MXU_DTYPE), (dims, ((), ())),
                               preferred_element_type=F32)
        if direct:
            o_ref[...] = part.astype(out_dtype)
            return
        acc = scratch[0] if use_acc else o_ref
        l = pl.program_id(2)

        @pl.when(l == 0)
        def _():
            acc[...] = part

        @pl.when(l > 0)
        def _():
            acc[...] += part

        if use_acc:
            @pl.when(l == gl - 1)
            def _():
                o_ref[...] = acc[...].astype(out_dtype)

    return pl.pallas_call(
        body, grid=(gi, gj, gl), in_specs=[a_spec, b_spec],
        out_specs=pl.BlockSpec((tm, tn), lambda i, j, l: (i, j)),
        out_shape=jax.ShapeDtypeStruct((m, n), out_dtype),
        scratch_shapes=[pltpu.VMEM((tm, tn), F32)] if use_acc else [],
        compiler_params=_params(("parallel", "parallel", "arbitrary")), name=name,
    )(a, b)


def mm(name, a, w):
    @jax.custom_vjp
    def f(a, w):
        return _mm(name + "_f", a, w.astype(MXU_DTYPE), NN)

    def fwd(a, w):
        return f(a, w), (a, w)

    def bwd(res, dc):
        a, w = res
        da = _mm(name + "_da", dc, w.astype(MXU_DTYPE), NT, out_dtype=a.dtype)
        dw = _mm(name + "_dw", a, dc, TN)
        return da, dw

    f.defvjp(fwd, bwd)
    return f(a, w)


def _split(x, parts):
    w = x.shape[-1] // parts
    return [x[:, k * w:(k + 1) * w] for k in range(parts)]


def _cat(xs):
    return xs[0] if len(xs) == 1 else jnp.concatenate(xs, axis=-1)


def _shift_rows(x, prev_ref, next_ref, i, ctx_tiles, nt):
    tr = x.shape[0]
    rid = lax.broadcasted_iota(jnp.int32, x.shape, 0)
    first = jnp.logical_or(i == 0, i == ctx_tiles)
    last = jnp.logical_or(i == ctx_tiles - 1, i == nt - 1)
    prow = jnp.where(first, 0.0, prev_ref[SUBLANES - 1:SUBLANES, :].astype(F32))
    nrow = jnp.where(last, 0.0, next_ref[0:1, :].astype(F32))
    xm = jnp.where(rid == 0, prow, pltpu.roll(x, 1, 0))
    xp = jnp.where(rid == tr - 1, nrow, pltpu.roll(x, tr - 1, 0))
    return xm, xp


def rowwise(name, fn, rows, vecs, *, halo=(), seg=(), parts_r=None, parts_v=None, outs=(), reds=(), ctx_tiles=1):
    nr, nv = len(rows), len(vecs)
    halo = tuple(halo) or (False,) * nr
    seg = tuple(seg) or (False,) * nv
    parts_r = tuple(parts_r or (1,) * nr)
    parts_v = tuple(parts_v or (1,) * nv)
    r_total = rows[0].shape[0]
    tr = ROW_TILE
    nt = r_total // tr
    assert r_total % tr == 0 and (ctx_tiles > 0 or not any(seg))

    def row_specs():
        sp = []
        for i, r in enumerate(rows):
            w = r.shape[1]
            sp.append(pl.BlockSpec((tr, w), lambda t: (t, 0)))
            if halo[i]:
                k = tr // SUBLANES
                sp.append(pl.BlockSpec((SUBLANES, w), lambda t: (jnp.maximum(t * k - 1, 0), 0)))
                sp.append(pl.BlockSpec((SUBLANES, w), lambda t: (jnp.minimum((t + 1) * k, nt * k - 1), 0)))
        return sp

    def vec_spec(j):
        w = vecs[j].shape[-1]
        if seg[j]:
            return pl.BlockSpec((None, 1, w), lambda t: (jnp.where(t >= ctx_tiles, 1, 0), 0, 0))
        return pl.BlockSpec((1, w), lambda t: (0, 0))

    def row_args(rv):
        a = []
        for i in range(nr):
            a += [rv[i]] * 3 if halo[i] else [rv[i]]
        return a

    def load(refs, t):
        pos, rp = 0, []
        for i in range(nr):
            x = refs[pos][...].astype(F32)
            if halo[i]:
                xm, xp = _shift_rows(x, refs[pos + 1], refs[pos + 2], t, ctx_tiles, nt)
                rp.append(list(zip(_split(x, parts_r[i]), _split(xm, parts_r[i]), _split(xp, parts_r[i]))))
                pos += 3
            else:
                rp.append(_split(x, parts_r[i]))
                pos += 1
        vp = []
        for j in range(nv):
            vp.append(_split(refs[pos][...].astype(F32), parts_v[j]))
            pos += 1
        return rp, vp, refs[pos:]

    n_out, n_red = len(outs), len(reds)

    def fwd_call(*rv):
        def body(*refs):
            t = pl.program_id(0)
            rp, vp, rest = load(refs, t)
            o_parts, r_parts = fn(rp, vp)
            for k in range(n_out):
                rest[k][...] = _cat(o_parts[k]).astype(outs[k][1])
            for k in range(n_red):
                ref = rest[n_out + k]

                @pl.when(t == 0)
                def _():
                    ref[...] = r_parts[k]

                @pl.when(t > 0)
                def _():
                    ref[...] += r_parts[k]

        res = pl.pallas_call(
            body, grid=(nt,),
            in_specs=row_specs() + [vec_spec(j) for j in range(nv)],
            out_specs=[pl.BlockSpec((tr, o[0]), lambda t: (t, 0)) for o in outs]
            + [pl.BlockSpec((1, w), lambda t: (0, 0)) for w in reds],
            out_shape=[jax.ShapeDtypeStruct((r_total, o[0]), o[1]) for o in outs]
            + [jax.ShapeDtypeStruct((1, w), F32) for w in reds],
            compiler_params=_params(("arbitrary",)), name=name + "_f",
        )(*row_args(rv), *rv[nr:])
        return tuple(res)

    def bwd_call(rv, cts):
        def body(*refs):
            t = pl.program_id(0)
            rp, vp, rest = load(refs, t)
            ct_o = [_split(rest[k][...].astype(F32), outs[k][2]) for k in range(n_out)]
            ct_r = [rest[n_out + k][...] for k in range(n_red)]
            rest = rest[n_out + n_red:]
            _, vjp = jax.vjp(fn, rp, vp)
            d_rp, d_vp = vjp((ct_o, ct_r))
            pos = 0
            for i in range(nr):
                if halo[i]:
                    for c in range(3):
                        rest[pos + c][...] = _cat([p[c] for p in d_rp[i]])
                    pos += 3
                else:
                    rest[pos][...] = _cat(d_rp[i]).astype(rows[i].dtype)
                    pos += 1
            for j in range(nv):
                ref, val = rest[pos + j], _cat(d_vp[j])
                start = jnp.logical_or(t == 0, t == ctx_tiles) if seg[j] else t == 0

                @pl.when(start)
                def _():
                    ref[...] = val

                @pl.when(jnp.logical_not(start))
                def _():
                    ref[...] += val

        d_row_specs, d_row_shapes = [], []
        for i, r in enumerate(rows):
            w = r.shape[1]
            for _ in range(3 if halo[i] else 1):
                d_row_specs.append(pl.BlockSpec((tr, w), lambda t: (t, 0)))
                d_row_shapes.append(jax.ShapeDtypeStruct(r.shape, F32 if halo[i] else r.dtype))
        res = pl.pallas_call(
            body, grid=(nt,),
            in_specs=row_specs() + [vec_spec(j) for j in range(nv)]
            + [pl.BlockSpec((tr, o[0]), lambda t: (t, 0)) for o in outs]
            + [pl.BlockSpec((1, w), lambda t: (0, 0)) for w in reds],
            out_specs=d_row_specs + [vec_spec(j) for j in range(nv)],
            out_shape=d_row_shapes + [jax.ShapeDtypeStruct(v.shape, F32) for v in vecs],
            compiler_params=_params(("arbitrary",)), name=name + "_b",
        )(*row_args(rv), *rv[nr:], *cts)
        d_rows, pos = [], 0
        for i in range(nr):
            if halo[i]:
                d_rows.append(_unshift(res[pos], res[pos + 1], res[pos + 2], ctx_tiles * tr).astype(rows[i].dtype))
                pos += 3
            else:
                d_rows.append(res[pos])
                pos += 1
        return tuple(d_rows) + tuple(res[pos:])

    @jax.custom_vjp
    def f(*rv):
        return fwd_call(*rv)

    f.defvjp(lambda *rv: (fwd_call(*rv), rv), lambda rv, cts: bwd_call(rv, cts))
    return f(*rows, *vecs)


def _unshift(d, dm, dp, ctx_rows):
    r = d.shape[0]
    t = lax.broadcasted_iota(jnp.int32, (r, 1), 0)
    zero = jnp.zeros((1, d.shape[1]), d.dtype)
    from_m = jnp.concatenate([dm[1:], zero], axis=0)
    from_p = jnp.concatenate([zero, dp[:-1]], axis=0)
    from_m = jnp.where(t == ctx_rows - 1, 0.0, from_m)
    from_p = jnp.where(t == ctx_rows, 0.0, from_p)
    return d + from_m + from_p


def _sigmoid(x):
    return 0.5 * (jnp.tanh(0.5 * x) + 1.0)


def _silu(x):
    return x * _sigmoid(x)


def _softplus(x):
    return jnp.maximum(x, 0.0) + jnp.log(1.0 + jnp.exp(-jnp.abs(x)))


def _rms(x, g):
    return x * lax.rsqrt(jnp.mean(x * x, axis=-1, keepdims=True) + EPS) * g


def _fn_modulate(r, v):
    (x,), (g,), (sh,), (sc,) = r[0], v[0], v[1], v[2]
    return [[_rms(x, g) * (1.0 + sc) + sh]], []


def _fn_resmod(r, v):
    (x,), (y,) = r
    (gate,), (g,), (sh,), (sc,) = v
    xn = x + gate * y
    return [[xn], [_rms(xn, g) * (1.0 + sc) + sh]], []


def _fn_final(r, v):
    (x,), (y,), (tgt,) = r
    (gate,), (g,) = v
    err = _rms(x + gate * y, g) - tgt
    row_loss = jnp.mean(err * err, axis=-1, keepdims=True)
    total = 0.5 * jnp.sum(row_loss, axis=0, keepdims=True)
    return [], [jnp.broadcast_to(total, (1, LANES))]


def _fn_dnprep(r, v):
    qkv, (ab,) = r
    w0, w1, w2, (alog,), (dtb,) = v
    out = [[], [], []]
    for n, (x, xm, xp) in enumerate(qkv):
        which = n // DN_HEADS
        y = _silu(xm * w0[n] + x * w1[n] + xp * w2[n])
        if which < 2:
            y = y * lax.rsqrt(jnp.sum(y * y, axis=-1, keepdims=True) + EPS)
        if which == 0:
            y = y * (HEAD_DIM ** -0.5)
        out[which].append(y)
    lane = lax.broadcasted_iota(jnp.int32, ab.shape, 1)
    g = -jnp.exp(alog) * _softplus(ab + dtb)
    gb = jnp.where(lane < N_DIRHEAD, g, jnp.where(lane < 2 * N_DIRHEAD, _sigmoid(ab), 0.0))
    return out + [[gb]], []


def _fn_dnpost(r, v):
    of, ob, z = r
    (g,) = v[0]
    return [[_rms(a + b, g) * _silu(c) for a, b, c in zip(of, ob, z)]], []


def _fn_sub(r, v):
    return [[r[0][0] - r[1][0]]], []


def _fn_scale(r, v):
    return [[r[0][0] * v[0][0]]], []


def _fn_shortconv(r, v):
    (xin, gb, gc), = r
    (w0,), (w1,), (w2,) = v
    u, um, up = (gc[k] * xin[k] for k in range(3))
    return [[gb[0] * (um * w0 + u * w1 + up * w2)]], []


def _fn_merge(r, v):
    gates, (ya,), (yb,), (yc,) = r
    return [[_sigmoid(gates[0]) * ya + _sigmoid(gates[1]) * yb + _sigmoid(gates[2]) * yc]], []


def _fn_swiglu(r, v):
    return [[_silu(r[0][0]) * r[1][0]]], []


def _dot(a, b, dims):
    return lax.dot_general(a.astype(MXU_DTYPE), b.astype(MXU_DTYPE), (dims, ((), ())), preferred_element_type=F32)


def _dot_hi(a, b):
    return lax.dot_general(a, b, (NN, ((), ())), precision=HIGHEST, preferred_element_type=F32)


def _dn_chunk(s, q, k, v, gb, j, reverse):
    c = q.shape[0]
    ii = lax.broadcasted_iota(jnp.int32, (c, c), 0)
    jj = lax.broadcasted_iota(jnp.int32, (c, c), 1)
    incl = (ii <= jj) if reverse else (ii >= jj)
    incl_t = (jj <= ii) if reverse else (jj >= ii)
    strict = (ii < jj) if reverse else (ii > jj)
    eye = ii == jj
    lane = lax.broadcasted_iota(jnp.int32, gb.shape, 1)
    g_col = jnp.sum(jnp.where(lane == j, gb, 0.0), axis=1, keepdims=True)
    b_col = jnp.sum(jnp.where(lane == N_DIRHEAD + j, gb, 0.0), axis=1, keepdims=True)
    g_row = jnp.sum(jnp.where(eye, g_col, 0.0), axis=0, keepdims=True)
    gc_col = jnp.sum(jnp.where(incl, g_row, 0.0), axis=1, keepdims=True)
    gc_row = jnp.sum(jnp.where(incl_t, g_col, 0.0), axis=0, keepdims=True)
    decay = jnp.where(incl, jnp.exp(jnp.where(incl, gc_col - gc_row, 0.0)), 0.0)
    kb = k * b_col
    a = _dot(kb, k, NT) * jnp.where(strict, decay, 0.0)
    t = jnp.where(eye, 1.0, 0.0) - a
    p = a
    for _ in range(int(math.log2(c)) - 1):
        p = _dot_hi(p, p)
        t = t + _dot_hi(t, p)
    u = _dot_hi(t, v * b_col)
    w = _dot_hi(t, kb * jnp.exp(gc_col))
    g_last = jnp.sum(g_col, axis=0, keepdims=True)
    k_state = k * jnp.exp(g_last - gc_col)
    v_new = u - _dot(w, s, NN)
    s_next = s * jnp.exp(g_last) + _dot(k_state, v_new, TN)
    o = _dot(q * jnp.exp(gc_col), s, NN) + _dot(_dot(q, k, NT) * decay, v_new, NN)
    return s_next, o


def _dn_rev(t, nc, n):
    return jnp.where(t < nc, nc - 1 - t, n - 1 - (t - nc))


def _dn_fwd(name, q, k, v, gb, nc):
    r = q.shape[0]
    n = r // DN_CHUNK
    c, hd, nh = DN_CHUNK, HEAD_DIM, DN_HEADS

    def body(qf, kf, vf, gf, qb, kb_, vb, gbb, of_ref, ob_ref, sf_ref, sb_ref, s_scr):
        t = pl.program_id(0)

        @pl.when(t == 0)
        def _():
            s_scr[...] = jnp.zeros_like(s_scr)

        for d, (qr, kr, vr, gr, o_ref, sall) in enumerate(((qf, kf, vf, gf, of_ref, sf_ref), (qb, kb_, vb, gbb, ob_ref, sb_ref))):
            gbv = gr[...]
            for h in range(nh):
                sl = slice(h * hd, (h + 1) * hd)
                s = s_scr[d, h]
                sall[0, h] = s
                s2, o = _dn_chunk(s, qr[:, sl], kr[:, sl], vr[:, sl], gbv, d * nh + h, d == 1)
                s_scr[d, h] = s2
                o_ref[:, sl] = o

    fw = lambda t: (t, 0)
    bw = lambda t: (_dn_rev(t, nc, n), 0)
    fw4 = lambda t: (t, 0, 0, 0)
    bw4 = lambda t: (_dn_rev(t, nc, n), 0, 0, 0)
    wide, narrow = (c, nh * hd), (c, LANES)
    return pl.pallas_call(
        body, grid=(n,),
        in_specs=[pl.BlockSpec(wide, fw)] * 3 + [pl.BlockSpec(narrow, fw)] + [pl.BlockSpec(wide, bw)] * 3 + [pl.BlockSpec(narrow, bw)],
        out_specs=[pl.BlockSpec(wide, fw), pl.BlockSpec(wide, bw),
                   pl.BlockSpec((1, nh, hd, hd), fw4), pl.BlockSpec((1, nh, hd, hd), bw4)],
        out_shape=[jax.ShapeDtypeStruct((r, nh * hd), F32)] * 2 + [jax.ShapeDtypeStruct((n, nh, hd, hd), F32)] * 2,
        scratch_shapes=[pltpu.VMEM((2, nh, hd, hd), F32)],
        compiler_params=_params(("arbitrary",)), name=name + "_f",
    )(q, k, v, gb, q, k, v, gb)


def _dn_bwd(name, q, k, v, gb, sall_f, sall_b, do_f, do_b, nc):
    r = q.shape[0]
    n = r // DN_CHUNK
    c, hd, nh = DN_CHUNK, HEAD_DIM, DN_HEADS

    def body(qf, kf, vf, gf, sf, dof, qb, kb_, vb, gbb, sb, dob,
             dqf, dkf, dvf, dgf, dqb, dkb, dvb, dgb_, ds_scr):
        t = pl.program_id(0)

        @pl.when(t == 0)
        def _():
            ds_scr[...] = jnp.zeros_like(ds_scr)

        for d, (qr, kr, vr, gr, sr, dor, dq, dk, dv, dg) in enumerate((
                (qf, kf, vf, gf, sf, dof, dqf, dkf, dvf, dgf), (qb, kb_, vb, gbb, sb, dob, dqb, dkb, dvb, dgb_))):
            gbv = gr[...]
            dg_acc = jnp.zeros_like(gbv)
            for h in range(nh):
                sl = slice(h * hd, (h + 1) * hd)
                chain = functools.partial(_dn_chunk, j=d * nh + h, reverse=d == 1)
                _, vjp = jax.vjp(chain, sr[0, h], qr[:, sl], kr[:, sl], vr[:, sl], gbv)
                ds, dq_h, dk_h, dv_h, dg_h = vjp((ds_scr[d, h], dor[:, sl]))
                ds_scr[d, h] = ds
                dq[:, sl] = dq_h
                dk[:, sl] = dk_h
                dv[:, sl] = dv_h
                dg_acc = dg_acc + dg_h
            dg[...] = dg_acc

    fw = lambda t: (n - 1 - t, 0)
    bw = lambda t: (_dn_rev(n - 1 - t, nc, n), 0)
    fw4 = lambda t: (n - 1 - t, 0, 0, 0)
    bw4 = lambda t: (_dn_rev(n - 1 - t, nc, n), 0, 0, 0)
    wide, narrow, st = (c, nh * hd), (c, LANES), (1, nh, hd, hd)
    one_dir = lambda m2, m4: [pl.BlockSpec(wide, m2)] * 3 + [pl.BlockSpec(narrow, m2), pl.BlockSpec(st, m4), pl.BlockSpec(wide, m2)]
    return pl.pallas_call(
        body, grid=(n,),
        in_specs=one_dir(fw, fw4) + one_dir(bw, bw4),
        out_specs=([pl.BlockSpec(wide, fw)] * 3 + [pl.BlockSpec(narrow, fw)]
                   + [pl.BlockSpec(wide, bw)] * 3 + [pl.BlockSpec(narrow, bw)]),
        out_shape=([jax.ShapeDtypeStruct((r, nh * hd), F32)] * 3 + [jax.ShapeDtypeStruct((r, LANES), F32)]) * 2,
        scratch_shapes=[pltpu.VMEM((2, nh, hd, hd), F32)],
        compiler_params=_params(("arbitrary",)), name=name + "_b",
    )(q, k, v, gb, sall_f, do_f, q, k, v, gb, sall_b, do_b)


def deltanet(name, q, k, v, gb, ctx_rows):
    nc = ctx_rows // DN_CHUNK

    @jax.custom_vjp
    def f(q, k, v, gb):
        return tuple(_dn_fwd(name, q, k, v, gb, nc)[:2])

    def fwd(q, k, v, gb):
        of, ob, sf, sb = _dn_fwd(name, q, k, v, gb, nc)
        return (of, ob), (q, k, v, gb, sf, sb)

    def bwd(res, cts):
        q, k, v, gb, sf, sb = res
        dqf, dkf, dvf, dgf, dqb, dkb, dvb, dgb_ = _dn_bwd(name, q, k, v, gb, sf, sb, cts[0], cts[1], nc)
        return dqf + dqb, dkf + dkb, dvf + dvb, dgf + dgb_

    f.defvjp(fwd, bwd)
    return f(q, k, v, gb)


def _box_matrix(l, w):
    lo, hi = w // 2, w - 1 - w // 2
    pos = np.arange(l)
    start, end = np.clip(pos - lo, 0, l), np.clip(pos + hi + 1, 0, l)
    col = np.arange(l)[None, :]
    return ((col >= start[:, None]) & (col < end[:, None])) / (end - start)[:, None].astype(np.float64)


def _pool_matrices(ctx_rows, grid_rows):
    assert ctx_rows == ROW_TILE and ROW_TILE % GRID_W == 0
    ctx = np.stack([_box_matrix(ctx_rows, w) for w in POOL_WINDOWS])
    cols = np.stack([np.kron(np.eye(ROW_TILE // GRID_W), _box_matrix(GRID_W, w)) for w in POOL_WINDOWS])
    rows = np.stack([_box_matrix(grid_rows, w) for w in POOL_WINDOWS])[None]
    return np.stack([ctx, cols]).astype(np.float32), rows.astype(np.float32)


def _pool_apply(name, x, mats, group_w, seg_tiles, lane_tile):
    r, w = x.shape
    b = mats.shape[-1]
    ng = mats.shape[1]
    period = ng * group_w

    def body(x_ref, m_ref, o_ref):
        xv = x_ref[...]
        lane = lax.broadcasted_iota(jnp.int32, xv.shape, 1)
        grp = (lane % period) // group_w
        acc = jnp.zeros_like(xv)
        for g in range(ng):
            acc = acc + jnp.where(grp == g, _dot_hi(m_ref[g], xv), 0.0)
        o_ref[...] = acc

    return pl.pallas_call(
        body, grid=(r // b, w // lane_tile),
        in_specs=[pl.BlockSpec((b, lane_tile), lambda i, j: (i, j)),
                  pl.BlockSpec((None, ng, b, b), lambda i, j: (jnp.where(i >= seg_tiles, 1, 0) if mats.shape[0] > 1 else 0, 0, 0, 0))],
        out_specs=pl.BlockSpec((b, lane_tile), lambda i, j: (i, j)),
        out_shape=jax.ShapeDtypeStruct(x.shape, F32),
        compiler_params=_params(("parallel", "parallel")), name=name,
    )(x, mats)


def pool_means(name, u, ctx_rows):
    r, pw = u.shape
    grid_rows = (r - ctx_rows) // GRID_W
    gw = pw // len(POOL_WINDOWS)
    m1, m2 = _pool_matrices(ctx_rows, grid_rows)
    lane_tile = min(2048, GRID_W * pw)

    def apply(x, a1, a2, tag):
        y = _pool_apply(name + tag + "1", x, jnp.asarray(a1), gw, ctx_rows // ROW_TILE, pw)
        lat = y[ctx_rows:].reshape(grid_rows, GRID_W * pw)
        lat = _pool_apply(name + tag + "2", lat, jnp.asarray(a2), gw, 0, lane_tile)
        return jnp.concatenate([y[:ctx_rows], lat.reshape(r - ctx_rows, pw)], axis=0)

    @jax.custom_vjp
    def f(u):
        return apply(u, m1, m2, "_f")

    tr = lambda m: np.ascontiguousarray(np.swapaxes(m, -1, -2))
    f.defvjp(lambda u: (apply(u, m1, m2, "_f"), None), lambda _, ct: (apply(ct, tr(m1), tr(m2), "_b"),))
    return f(u)


def _ew(name, fn, *xs, n_out=1):
    shapes = jax.eval_shape(lambda *a: fn(*a), *xs)
    shapes = shapes if isinstance(shapes, (tuple, list)) else (shapes,)

    def body(*refs):
        res = fn(*[r[...] for r in refs[:len(xs)]])
        res = res if isinstance(res, (tuple, list)) else (res,)
        for r, o in zip(res, refs[len(xs):]):
            o[...] = r

    out = pl.pallas_call(body, out_shape=[jax.ShapeDtypeStruct(s.shape, s.dtype) for s in shapes],
                         compiler_params=_params(), name=name)(*xs)
    return out[0] if len(shapes) == 1 else tuple(out)


def _adamw_math(w, g, m, v):
    m2 = ADAM_B1 * m + (1.0 - ADAM_B1) * g
    v2 = ADAM_B2 * v + (1.0 - ADAM_B2) * (g * g)
    m_hat = m2 / (1.0 - ADAM_B1 ** ADAM_STEP)
    v_hat = v2 / (1.0 - ADAM_B2 ** ADAM_STEP)
    delta = -ADAM_LR * (m_hat / (jnp.sqrt(v_hat) + ADAM_EPS) + ADAM_WD * w)
    return delta, m2, v2


def adamw(name, w, g, m, v):
    r, c = w.shape
    tr = _pick(r, (256, 128, 64, 32, 16, 8))

    def body(w_ref, g_ref, m_ref, v_ref, d_ref, m2_ref, v2_ref):
        d_ref[...], m2_ref[...], v2_ref[...] = _adamw_math(w_ref[...], g_ref[...], m_ref[...], v_ref[...])

    spec = pl.BlockSpec((tr, c), lambda i: (i, 0))
    return pl.pallas_call(body, grid=(r // tr,), in_specs=[spec] * 4, out_specs=[spec] * 3,
                          out_shape=[jax.ShapeDtypeStruct(w.shape, F32)] * 3,
                          compiler_params=_params(("parallel",)), name=name)(w, g, m, v)


def _sum_rows(name, xs):
    r, c = xs[0].shape
    tr = _pick(r, (512, 256, 128, 64, 32, 16, 8))

    def body(*refs):
        acc = refs[0][...]
        for ref in refs[1:-1]:
            acc = acc + ref[...]
        refs[-1][...] = acc

    spec = pl.BlockSpec((tr, c), lambda i: (i, 0))
    return pl.pallas_call(body, grid=(r // tr,), in_specs=[spec] * len(xs), out_specs=spec,
                          out_shape=jax.ShapeDtypeStruct((r, c), xs[0].dtype),
                          compiler_params=_params(("parallel",)), name=name)(*xs)


def _place():
    return lax.axis_index("x"), lax.axis_index("y"), lax.axis_index("c")


def _chip_peers(x, y, c):
    return [(1 - x, y, c), (x, 1 - y, c), (1 - x, 1 - y, c)]


def all_gather8(name, block):
    m_per, n = block.shape

    def body(x_ref, out_ref, send_sems, recv_sems, local_sem):
        x, y, c = _place()
        me, sibling = (x, y, c), (x, y, 1 - c)
        chips = [(1 - x, y), (x, 1 - y), (1 - x, 1 - y)]

        def rows(px, py, pc):
            return out_ref.at[pl.ds((4 * px + 2 * py + pc) * m_per, m_per), :]

        def copy(k, blk, to, src=None):
            return pltpu.make_async_remote_copy(
                src_ref=rows(*blk) if src is None else src, dst_ref=rows(*blk),
                send_sem=send_sems.at[k], recv_sem=recv_sems.at[k], device_id=to, device_id_type=MESH_ID)

        mine = pltpu.make_async_copy(x_ref, rows(*me), local_sem)
        mine.start()
        first = [copy(0, me, sibling, src=x_ref)]
        first += [copy(1 + j, me, (*chip, c), src=x_ref) for j, chip in enumerate(chips)]
        for cp in first:
            cp.start()
        passed = [copy(4 + j, (*chip, c), sibling) for j, chip in enumerate(chips)]
        for j, chip in enumerate(chips):
            copy(1 + j, (*chip, c), me).wait_recv()
            passed[j].start()
        copy(0, sibling, me).wait_recv()
        for j, chip in enumerate(chips):
            copy(4 + j, (*chip, 1 - c), me).wait_recv()
        for cp in first + passed:
            cp.wait_send()
        mine.wait()

    return pl.pallas_call(
        body, out_shape=jax.ShapeDtypeStruct((8 * m_per, n), block.dtype),
        in_specs=[pl.BlockSpec(memory_space=pltpu.VMEM)], out_specs=pl.BlockSpec(memory_space=pltpu.VMEM),
        scratch_shapes=[pltpu.SemaphoreType.DMA((7,)), pltpu.SemaphoreType.DMA((7,)), pltpu.SemaphoreType.DMA],
        compiler_params=_params(), name=name,
    )(block)


def gather_chips(name, shard):
    def body(x_ref, out_ref, send_sems, recv_sems, local_sem):
        x, y, c = _place()
        mine = pltpu.make_async_copy(x_ref, out_ref.at[2 * x + y], local_sem)
        mine.start()
        copies = []
        for p, peer in enumerate(_chip_peers(x, y, c)):
            cp = pltpu.make_async_remote_copy(
                src_ref=x_ref, dst_ref=out_ref.at[2 * x + y], send_sem=send_sems.at[p], recv_sem=recv_sems.at[p],
                device_id=peer, device_id_type=MESH_ID)
            cp.start()
            copies.append(cp)
        for p, (px, py, _) in enumerate(_chip_peers(x, y, c)):
            pltpu.make_async_remote_copy(
                src_ref=x_ref, dst_ref=out_ref.at[2 * px + py], send_sem=send_sems.at[p], recv_sem=recv_sems.at[p],
                device_id=(px, py, c), device_id_type=MESH_ID).wait_recv()
        for cp in copies:
            cp.wait_send()
        mine.wait()

    hbm = pl.BlockSpec(memory_space=pltpu.HBM)
    return pl.pallas_call(
        body, out_shape=jax.ShapeDtypeStruct((4,) + shard.shape, shard.dtype), in_specs=[hbm], out_specs=hbm,
        scratch_shapes=[pltpu.SemaphoreType.DMA((3,)), pltpu.SemaphoreType.DMA((3,)), pltpu.SemaphoreType.DMA],
        compiler_params=_params(), name=name,
    )(shard)


def swap_sibling(name, block):
    def body(x_ref, out_ref, send_sem, recv_sem):
        x, y, c = _place()
        cp = pltpu.make_async_remote_copy(src_ref=x_ref, dst_ref=out_ref, send_sem=send_sem, recv_sem=recv_sem,
                                          device_id=(x, y, 1 - c), device_id_type=MESH_ID)
        cp.start()
        cp.wait()

    hbm = pl.BlockSpec(memory_space=pltpu.HBM)
    return pl.pallas_call(
        body, out_shape=jax.ShapeDtypeStruct(block.shape, block.dtype), in_specs=[hbm], out_specs=hbm,
        scratch_shapes=[pltpu.SemaphoreType.DMA, pltpu.SemaphoreType.DMA],
        compiler_params=_params(), name=name,
    )(block)


def scatter_chips(name, pieces):
    def body(x_ref, out_ref, send_sems, recv_sems):
        x, y, c = _place()
        copies = []
        for p, (px, py, pc) in enumerate(_chip_peers(x, y, c)):
            cp = pltpu.make_async_remote_copy(
                src_ref=x_ref.at[2 * px + py], dst_ref=out_ref.at[p], send_sem=send_sems.at[p], recv_sem=recv_sems.at[p],
                device_id=(px, py, pc), device_id_type=MESH_ID)
            cp.start()
            copies.append(cp)
        for cp in copies:
            cp.wait()

    hbm = pl.BlockSpec(memory_space=pltpu.HBM)
    return pl.pallas_call(
        body, out_shape=jax.ShapeDtypeStruct((3,) + pieces.shape[1:], pieces.dtype), in_specs=[hbm], out_specs=hbm,
        scratch_shapes=[pltpu.SemaphoreType.DMA((3,)), pltpu.SemaphoreType.DMA((3,))],
        compiler_params=_params(), name=name,
    )(pieces)


def _sum_devices(name, got, fold=False):
    def body(a_ref, *o_refs):
        acc = a_ref[0]
        for i in range(1, N_DEV):
            acc = acc + a_ref[i]
        o_refs[0][...] = acc
        if fold:
            o_refs[1][...] = acc + pltpu.roll(acc, SUBLANES // 2, 0)

    shape = jax.ShapeDtypeStruct(got.shape[1:], F32)
    out = pl.pallas_call(body, out_shape=[shape] * (2 if fold else 1), compiler_params=_params(), name=name)(got)
    return out if fold else out[0]


def split_cols(w, bounds):
    edges = list(zip(bounds[:-1], bounds[1:]))

    def cut(w):
        return tuple(w[:, a:b] for a, b in edges)

    f = jax.custom_vjp(cut)
    f.defvjp(lambda w: (cut(w), None), lambda _, cts: (jnp.concatenate(cts, axis=1),))
    return f(w)


def _row(v):
    return v.reshape(1, -1)


def _pad_lanes(v, width=LANES):
    return jnp.pad(v, ((0, 0), (0, width - v.shape[1])))


def _block_diag(blocks):
    g, n, _ = blocks.shape
    out = jnp.zeros((g * n, g * n), blocks.dtype)
    for i in range(g):
        out = out.at[i * n:(i + 1) * n, i * n:(i + 1) * n].set(blocks[i])
    return out


def local_loss(p, x, mod_lat, mod_ctx, ctx, target):
    ctx_rows, d = ctx.shape
    depth = p["w_in"].shape[0]
    pw, scw = p["pool_scale"].shape[1], p["sc_conv_w"].shape[2]
    ff = p["w_down"].shape[1]
    ct = ctx_rows // ROW_TILE
    off_z = 3 * DN_WIDTH
    off_a = off_z + DN_WIDTH
    off_pool = off_a + 2 * N_DIRHEAD
    off_sc = off_pool + pw
    off_gate = off_sc + 3 * scw
    n_in = off_gate + 3 * d
    xs = jnp.concatenate([ctx, x], axis=0)
    seg = lambda l, k: jnp.stack([mod_ctx[l, k], mod_lat[l, k]]).reshape(2, 1, d)
    dn = gate2 = None
    for l in range(depth):
        sh1, sc1, g1, sh2, sc2, g2 = (seg(l, k) for k in range(6))
        tag = f"l{l}_"
        if l == 0:
            (h1,) = rowwise(tag + "mod", _fn_modulate, [xs], [_row(p["norm1_g"][l]), sh1, sc1], seg=(False, True, True),
                            outs=[(d, F32, 1)], ctx_tiles=ct)
        else:
            xs, h1 = rowwise(tag + "resmod1", _fn_resmod, [xs, dn], [gate2, _row(p["norm1_g"][l]), sh1, sc1],
                             seg=(True, False, True, True), outs=[(d, F32, 1), (d, F32, 1)], ctx_tiles=ct)
        w_qkv, w_z, w_ab, w_pool, w_sc, w_gate = split_cols(p["w_in"][l], (0, off_z, off_a, off_pool, off_sc, off_gate, n_in))
        p_qkv = mm(tag + "qkv", h1, w_qkv)
        p_z = mm(tag + "z", h1, w_z)
        p_ab = mm(tag + "ab", h1, _pad_lanes(w_ab))
        p_pool = mm(tag + "pool", h1, w_pool)
        p_sc = mm(tag + "sc", h1, w_sc)
        p_gate = mm(tag + "gate", h1, w_gate)
        cw = p["dn_conv_w"][l]
        q, k, v, gb = rowwise(
            tag + "dnprep", _fn_dnprep, [p_qkv, p_ab],
            [cw[0:1], cw[1:2], cw[2:3], _pad_lanes(p["dn_a_log"][l].reshape(1, -1)), _pad_lanes(p["dn_dt_bias"][l].reshape(1, -1))],
            halo=(True, False), parts_r=(3 * DN_HEADS, 1), parts_v=(3 * DN_HEADS,) * 3 + (1, 1),
            outs=[(DN_WIDTH, F32, DN_HEADS)] * 3 + [(LANES, F32, 1)], ctx_tiles=ct)
        o_f, o_b = deltanet(tag + "dn", q, k, v, gb, ctx_rows)
        (oz,) = rowwise(tag + "dnpost", _fn_dnpost, [o_f, o_b, p_z], [_row(p["dn_norm_g"][l])], parts_r=(DN_HEADS,) * 3,
                        outs=[(DN_WIDTH, MXU_DTYPE, DN_HEADS)], ctx_tiles=ct)
        y_a = mm(tag + "bra", oz, p["w_br_a"][l])
        means = pool_means(tag + "box", p_pool, ctx_rows)
        (dpool,) = rowwise(tag + "poolsub", _fn_sub, [means, p_pool], [], outs=[(pw, MXU_DTYPE, 1)], ctx_tiles=ct)
        mixed = mm(tag + "poolw", dpool, _block_diag(p["pool_w"][l]))
        (yb_in,) = rowwise(tag + "poolscale", _fn_scale, [mixed], [_row(p["pool_scale"][l])], outs=[(pw, MXU_DTYPE, 1)], ctx_tiles=ct)
        y_b = mm(tag + "brb", yb_in, p["w_br_b"][l])
        sw = p["sc_conv_w"][l]
        (yc_in,) = rowwise(tag + "sconv", _fn_shortconv, [p_sc], [sw[0:1], sw[1:2], sw[2:3]], halo=(True,), parts_r=(3,),
                           outs=[(scw, MXU_DTYPE, 1)], ctx_tiles=ct)
        y_c = mm(tag + "brc", yc_in, p["w_br_c"][l])
        (y,) = rowwise(tag + "merge", _fn_merge, [p_gate, y_a, y_b, y_c], [], parts_r=(3, 1, 1, 1),
                       outs=[(d, MXU_DTYPE, 1)], ctx_tiles=ct)
        mix = mm(tag + "wo", y, p["w_o"][l])
        xs, h2 = rowwise(tag + "resmod2", _fn_resmod, [xs, mix], [g1, _row(p["norm2_g"][l]), sh2, sc2],
                         seg=(True, False, True, True), outs=[(d, F32, 1), (d, F32, 1)], ctx_tiles=ct)
        w_g, w_u = split_cols(p["w_gu"][l], (0, ff, 2 * ff))
        (act,) = rowwise(tag + "swiglu", _fn_swiglu, [mm(tag + "ffg", h2, w_g), mm(tag + "ffu", h2, w_u)], [],
                         outs=[(ff, MXU_DTYPE, 1)], ctx_tiles=ct)
        dn = mm(tag + "down", act, p["w_down"][l])
        gate2 = g2
    (total,) = rowwise("final", _fn_final, [xs[ctx_rows:], dn[ctx_rows:], target],
                       [gate2[1], _row(p["final_norm_g"])], reds=[LANES], ctx_tiles=0)
    return total[0, 0]


BIG = ("w_in", "w_br_a", "w_br_b", "w_br_c", "w_o", "w_gu", "w_down")
ROW_SHARDED = ("w_o", "w_down")
SMALL_REPL = ("norm1_g", "norm2_g", "dn_a_log", "dn_dt_bias", "dn_norm_g", "pool_w", "pool_scale", "final_norm_g")
SMALL_SHARD = ("dn_conv_w", "sc_conv_w")
WEIGHTS = ("c_ctx", "w_ada", "b_ada", "norm1_g", "norm2_g", "w_in", "dn_conv_w", "dn_a_log", "dn_dt_bias", "dn_norm_g", "pool_w",
           "pool_scale", "sc_conv_w", "w_br_a", "w_br_b", "w_br_c", "w_o", "w_gu", "w_down", "final_norm_g")
N_CHIPS, N_DEV = 4, 8
COMM_COLS = 1024


def _pack(arrays, rows_multiple, cols=COMM_COLS, dtype=F32):
    flat = jnp.concatenate([a.astype(dtype).reshape(-1) for a in arrays])
    rows = -(-flat.shape[0] // (cols * rows_multiple)) * rows_multiple
    return jnp.pad(flat, (0, rows * cols - flat.shape[0])).reshape(rows, cols)


def _unpack(flat, shapes):
    out, pos = [], 0
    for s in shapes:
        n = int(np.prod(s))
        out.append(flat[pos:pos + n].reshape(s))
        pos += n
    return out


def _join_shards(parts, name):
    _, l, a, b = parts.shape
    if name in ROW_SHARDED:
        return jnp.moveaxis(parts, 0, 1).reshape(l, N_CHIPS * a, b)
    return jnp.moveaxis(parts, 0, 2).reshape(l, a, N_CHIPS * b)


def _cut_shard(full, name, j):
    if name in ROW_SHARDED:
        n = full.shape[1] // N_CHIPS
        return full[:, j * n:(j + 1) * n, :]
    n = full.shape[2] // N_CHIPS
    return full[:, :, j * n:(j + 1) * n]


def _dsilu(x):
    s = _sigmoid(x)
    return s + x * s * (1.0 - s)


def kernel(x, c, ctx, c_ctx, w_ada, b_ada, norm1_g, norm2_g, w_in, dn_conv_w, dn_a_log, dn_dt_bias, dn_norm_g, pool_w, pool_scale, sc_conv_w, w_br_a, w_br_b, w_br_c, w_o, w_gu, w_down, final_norm_g, loss_target, m_c_ctx, m_w_ada, m_b_ada, m_norm1_g, m_norm2_g, m_w_in, m_dn_conv_w, m_dn_a_log, m_dn_dt_bias, m_dn_norm_g, m_pool_w, m_pool_scale, m_sc_conv_w, m_w_br_a, m_w_br_b, m_w_br_c, m_w_o, m_w_gu, m_w_down, m_final_norm_g, v_c_ctx, v_w_ada, v_b_ada, v_norm1_g, v_norm2_g, v_w_in, v_dn_conv_w, v_dn_a_log, v_dn_dt_bias, v_dn_norm_g, v_pool_w, v_pool_scale, v_sc_conv_w, v_w_br_a, v_w_br_b, v_w_br_c, v_w_o, v_w_gu, v_w_down, v_final_norm_g):
    given = dict(locals())
    ix, iy, ic = _place()
    chip, dev = 2 * ix + iy, 4 * ix + 2 * iy + ic
    d = x.shape[-1]
    depth = w_in.shape[0]
    ada_cols = w_ada.shape[2]
    spare = 2 * SUBLANES - N_DEV - 1

    got0 = all_gather8("gather_cond", _pack([c, dn_conv_w, sc_conv_w], SUBLANES)).reshape(N_DEV, -1)
    c_all = got0[:, :d]
    taps = [_unpack(got0[2 * j, d:], [dn_conv_w.shape, sc_conv_w.shape]) for j in range(N_CHIPS)]
    full = {"dn_conv_w": jnp.concatenate([t[0] for t in taps], axis=-1),
            "sc_conv_w": jnp.concatenate([t[1] for t in taps], axis=-1)}
    big_shapes = [given[n].shape for n in BIG]
    shards = gather_chips("gather_weights", _pack([given[n] for n in BIG], 2 * SUBLANES, dtype=MXU_DTYPE))
    shards = shards.reshape(N_CHIPS, -1)
    per_chip = [_unpack(shards[j], big_shapes) for j in range(N_CHIPS)]
    for i, n in enumerate(BIG):
        full[n] = _join_shards(jnp.stack([per_chip[j][i] for j in range(N_CHIPS)]), n).astype(F32)
    for n in SMALL_REPL:
        full[n] = given[n]

    cond = jnp.concatenate([c_all, c_ctx[None], jnp.zeros((spare, d), F32)])
    s_cond = _ew("silu_cond", _silu, cond)
    mod_cols = jnp.concatenate([_mm(f"ada{l}_f", s_cond, w_ada[l], NN) for l in range(depth)], axis=1)
    mod_got = all_gather8("gather_mod", mod_cols).reshape(N_DEV, 2 * SUBLANES, depth, ada_cols)
    mod_all = jnp.concatenate([mod_got[2 * j] for j in range(N_CHIPS)], axis=-1) + b_ada[None]
    mod_lat = lax.dynamic_index_in_dim(mod_all, dev, 0, keepdims=False).reshape(depth, 6, d)
    mod_ctx = mod_all[N_DEV].reshape(depth, 6, d)

    loss_local, (g_full, grad_x, g_mod_lat, g_mod_ctx) = jax.value_and_grad(local_loss, argnums=(0, 1, 2, 3))(
        full, x[0], mod_lat, mod_ctx, ctx[0], loss_target[0])
    loss = lax.psum(loss_local, ("x", "y", "c"))

    dmod_cols = (2 * depth * 6 * d) // SUBLANES
    dmod = all_gather8("gather_dmod", _pack([g_mod_lat, g_mod_ctx], SUBLANES, cols=dmod_cols))
    dmod = dmod.reshape(N_DEV, SUBLANES, dmod_cols)
    dmod_sum, dmod_fold = _sum_devices("reduce_dmod", dmod, fold=True)
    half_rows = SUBLANES // 2
    grad_b_ada = dmod_fold[:half_rows].reshape(depth, 6 * d)
    dctx_sum = dmod_sum[half_rows:].reshape(1, depth, 6 * d)
    d9 = jnp.concatenate([dmod[:, :half_rows].reshape(N_DEV, depth, 6 * d), dctx_sum, jnp.zeros((spare, depth, 6 * d), F32)])
    d9 = lax.dynamic_slice_in_dim(d9, chip * ada_cols, ada_cols, axis=2)
    grad_w_ada = jnp.stack([_mm(f"ada{l}_dw", s_cond, d9[:, l], TN) for l in range(depth)])
    ds_cond = [_mm(f"ada{l}_da", d9[:, l], w_ada[l], NT) for l in range(depth)]
    dsilu_part = _sum_rows("sum_dcond", ds_cond)[N_DEV]

    small_names = SMALL_REPL + SMALL_SHARD
    small_grads = [dsilu_part] + [g_full[n] for n in small_names]
    small_shapes = [a.shape for a in small_grads]
    n_small = sum(int(np.prod(s)) for s in small_shapes)
    cols = -(-n_small // (SUBLANES * LANES)) * LANES
    got2 = all_gather8("gather_small", _pack(small_grads, SUBLANES, cols=cols)).reshape(N_DEV, SUBLANES, cols)
    small_sum = _unpack(_sum_devices("reduce_small", got2).reshape(-1), small_shapes)
    grads = dict(zip(small_names, small_sum[1:]))
    grads["c_ctx"] = _ew("dsilu", lambda g, z: 0.5 * g * _dsilu(z), _row(small_sum[0]), _row(c_ctx)).reshape(-1)
    grads["w_ada"], grads["b_ada"] = grad_w_ada, grad_b_ada
    for n in SMALL_SHARD:
        width = given[n].shape[-1]
        grads[n] = lax.dynamic_slice_in_dim(grads[n], chip * width, width, axis=-1)

    pieces = jnp.stack([_pack([_cut_shard(g_full[n], n, j) for n in BIG], 2 * SUBLANES) for j in range(N_CHIPS)])
    rows_h = pieces.shape[1] // 2
    halves = jnp.moveaxis(pieces.reshape(N_CHIPS, 2, rows_h, COMM_COLS), 1, 0)
    keep = lax.dynamic_index_in_dim(halves, ic, 0, keepdims=False).reshape(N_CHIPS * rows_h, COMM_COLS)
    give = lax.dynamic_index_in_dim(halves, 1 - ic, 0, keepdims=False).reshape(N_CHIPS * rows_h, COMM_COLS)
    pair = _sum_rows("sum_pair", [keep, swap_sibling("swap_halves", give)]).reshape(N_CHIPS, rows_h, COMM_COLS)
    got3 = scatter_chips("scatter_pieces", pair)
    own = lax.dynamic_index_in_dim(pair, chip, 0, keepdims=False)
    half = _sum_rows("sum_chips", [own, got3[0], got3[1], got3[2]])
    other = swap_sibling("swap_reduced", half)
    lo = jnp.where(ic == 0, half, other)
    hi = jnp.where(ic == 0, other, half)
    for n, g in zip(BIG, _unpack(jnp.concatenate([lo, hi]).reshape(-1), big_shapes)):
        grads[n] = g

    delta, new_m, new_v = {}, {}, {}
    large = BIG + ("w_ada",)
    for n in large:
        shape = given[n].shape
        flat = [a.reshape(-1, shape[-1]) for a in (given[n], grads[n], given["m_" + n], given["v_" + n])]
        res = adamw("adamw_" + n, *flat)
        delta[n], new_m[n], new_v[n] = (a.reshape(shape) for a in res)
    rest = [n for n in WEIGHTS if n not in large]
    rest_shapes = [given[n].shape for n in rest]
    packed = [_pack([src[pre + n] for n in rest], SUBLANES, cols=LANES)
              for src, pre in ((given, ""), (grads, ""), (given, "m_"), (given, "v_"))]
    for res, o in zip((delta, new_m, new_v), adamw("adamw_small", *packed)):
        for n, a in zip(rest, _unpack(o.reshape(-1), rest_shapes)):
            res[n] = a
    return (loss, grad_x[None], *[grads[n] for n in WEIGHTS], *[delta[n] for n in WEIGHTS],
            *[new_m[n] for n in WEIGHTS], *[new_v[n] for n in WEIGHTS])
```

```python
import functools
import math

import numpy as np
import jax
import jax.numpy as jnp
from jax import lax
from jax.experimental import pallas as pl
from jax.experimental.pallas import tpu as pltpu

F32 = jnp.float32
MXU_DTYPE = jnp.bfloat16
HIGHEST = lax.Precision.HIGHEST

DN_HEADS = 4
HEAD_DIM = 128
DN_WIDTH = DN_HEADS * HEAD_DIM
DN_CHUNK = 64
GRID_W = 64
EPS = 1e-6
POOL_WINDOWS = (2, 4, 8, 16)
N_DIRHEAD = 2 * DN_HEADS
ADAM_LR, ADAM_B1, ADAM_B2, ADAM_EPS, ADAM_WD, ADAM_STEP = 0.001, 0.9, 0.999, 1e-08, 0.01, 10

LANES = 128
SUBLANES = 8
ROW_TILE = 256
VMEM_LIMIT = 56 * 1024 * 1024

MESH_ID = pl.DeviceIdType.MESH
NN, NT, TN = ((1,), (0,)), ((1,), (1,)), ((0,), (0,))


def _params(sem=None):
    return pltpu.CompilerParams(dimension_semantics=sem, vmem_limit_bytes=VMEM_LIMIT)


def _pick(n, cands):
    for c in cands:
        if c <= n and n % c == 0:
            return c
    return n


def _mm(name, a, b, dims, out_dtype=F32):
    if dims == NN:
        (m, kk), (_, n) = a.shape, b.shape
    elif dims == NT:
        (m, kk), (n, _) = a.shape, b.shape
    else:
        (kk, m), (_, n) = a.shape, b.shape
    tm = _pick(m, (768, 1024, 512, 256, 128, 64, 32, 16, 8))
    tn = _pick(n, (1024, 1408, 768, 896, 512, 256, 128))
    tk = _pick(kk, (1024, 1408, 768, 512, 256, 128))
    gi, gj, gl = m // tm, n // tn, kk // tk
    if dims == NN:
        a_spec = pl.BlockSpec((tm, tk), lambda i, j, l: (i, l))
        b_spec = pl.BlockSpec((tk, tn), lambda i, j, l: (l, j))
    elif dims == NT:
        a_spec = pl.BlockSpec((tm, tk), lambda i, j, l: (i, l))
        b_spec = pl.BlockSpec((tn, tk), lambda i, j, l: (j, l))
    else:
        a_spec = pl.BlockSpec((tk, tm), lambda i, j, l: (l, i))
        b_spec = pl.BlockSpec((tk, tn), lambda i, j, l: (l, j))
    direct = gl == 1
    use_acc = (not direct) and out_dtype != F32

    def body(a_ref, b_ref, o_ref, *scratch):
        part = lax.dot_general(a_ref[...].astype(MXU_DTYPE), b_ref[...].astype(MXU_DTYPE), (dims, ((), ())),
                               preferred_element_type=F32)
        if direct:
            o_ref[...] = part.astype(out_dtype)
            return
        acc = scratch[0] if use_acc else o_ref
        l = pl.program_id(2)

        @pl.when(l == 0)
        def _():
            acc[...] = part

        @pl.when(l > 0)
        def _():
            acc[...] += part

        if use_acc:
            @pl.when(l == gl - 1)
            def _():
                o_ref[...] = acc[...].astype(out_dtype)

    return pl.pallas_call(
        body, grid=(gi, gj, gl), in_specs=[a_spec, b_spec],
        out_specs=pl.BlockSpec((tm, tn), lambda i, j, l: (i, j)),
        out_shape=jax.ShapeDtypeStruct((m, n), out_dtype),
        scratch_shapes=[pltpu.VMEM((tm, tn), F32)] if use_acc else [],
        compiler_params=_params(("parallel", "parallel", "arbitrary")), name=name,
    )(a, b)


def mm(name, a, w):
    @jax.custom_vjp
    def f(a, w):
        return _mm(name + "_f", a, w.astype(MXU_DTYPE), NN)

    def fwd(a, w):
        return f(a, w), (a, w)

    def bwd(res, dc):
        a, w = res
        da = _mm(name + "_da", dc, w.astype(MXU_DTYPE), NT, out_dtype=a.dtype)
        dw = _mm(name + "_dw", a, dc, TN)
        return da, dw

    f.defvjp(fwd, bwd)
    return f(a, w)


def _split(x, parts):
    w = x.shape[-1] // parts
    return [x[:, k * w:(k + 1) * w] for k in range(parts)]


def _cat(xs):
    return xs[0] if len(xs) == 1 else jnp.concatenate(xs, axis=-1)


def _shift_rows(x, prev_ref, next_ref, i, ctx_tiles, nt):
    tr = x.shape[0]
    rid = lax.broadcasted_iota(jnp.int32, x.shape, 0)
    first = jnp.logical_or(i == 0, i == ctx_tiles)
    last = jnp.logical_or(i == ctx_tiles - 1, i == nt - 1)
    prow = jnp.where(first, 0.0, prev_ref[SUBLANES - 1:SUBLANES, :].astype(F32))
    nrow = jnp.where(last, 0.0, next_ref[0:1, :].astype(F32))
    xm = jnp.where(rid == 0, prow, pltpu.roll(x, 1, 0))
    xp = jnp.where(rid == tr - 1, nrow, pltpu.roll(x, tr - 1, 0))
    return xm, xp


def rowwise(name, fn, rows, vecs, *, halo=(), seg=(), parts_r=None, parts_v=None, outs=(), reds=(), ctx_tiles=1):
    nr, nv = len(rows), len(vecs)
    halo = tuple(halo) or (False,) * nr
    seg = tuple(seg) or (False,) * nv
    parts_r = tuple(parts_r or (1,) * nr)
    parts_v = tuple(parts_v or (1,) * nv)
    r_total = rows[0].shape[0]
    tr = ROW_TILE
    nt = r_total // tr
    assert r_total % tr == 0 and (ctx_tiles > 0 or not any(seg))

    def row_specs():
        sp = []
        for i, r in enumerate(rows):
            w = r.shape[1]
            sp.append(pl.BlockSpec((tr, w), lambda t: (t, 0)))
            if halo[i]:
                k = tr // SUBLANES
                sp.append(pl.BlockSpec((SUBLANES, w), lambda t: (jnp.maximum(t * k - 1, 0), 0)))
                sp.append(pl.BlockSpec((SUBLANES, w), lambda t: (jnp.minimum((t + 1) * k, nt * k - 1), 0)))
        return sp

    def vec_spec(j):
        w = vecs[j].shape[-1]
        if seg[j]:
            return pl.BlockSpec((None, 1, w), lambda t: (jnp.where(t >= ctx_tiles, 1, 0), 0, 0))
        return pl.BlockSpec((1, w), lambda t: (0, 0))

    def row_args(rv):
        a = []
        for i in range(nr):
            a += [rv[i]] * 3 if halo[i] else [rv[i]]
        return a

    def load(refs, t):
        pos, rp = 0, []
        for i in range(nr):
            x = refs[pos][...].astype(F32)
            if halo[i]:
                xm, xp = _shift_rows(x, refs[pos + 1], refs[pos + 2], t, ctx_tiles, nt)
                rp.append(list(zip(_split(x, parts_r[i]), _split(xm, parts_r[i]), _split(xp, parts_r[i]))))
                pos += 3
            else:
                rp.append(_split(x, parts_r[i]))
                pos += 1
        vp = []
        for j in range(nv):
            vp.append(_split(refs[pos][...].astype(F32), parts_v[j]))
            pos += 1
        return rp, vp, refs[pos:]

    n_out, n_red = len(outs), len(reds)

    def fwd_call(*rv):
        def body(*refs):
            t = pl.program_id(0)
            rp, vp, rest = load(refs, t)
            o_parts, r_parts = fn(rp, vp)
            for k in range(n_out):
                rest[k][...] = _cat(o_parts[k]).astype(outs[k][1])
            for k in range(n_red):
                ref = rest[n_out + k]

                @pl.when(t == 0)
                def _():
                    ref[...] = r_parts[k]

                @pl.when(t > 0)
                def _():
                    ref[...] += r_parts[k]

        res = pl.pallas_call(
            body, grid=(nt,),
            in_specs=row_specs() + [vec_spec(j) for j in range(nv)],
            out_specs=[pl.BlockSpec((tr, o[0]), lambda t: (t, 0)) for o in outs]
            + [pl.BlockSpec((1, w), lambda t: (0, 0)) for w in reds],
            out_shape=[jax.ShapeDtypeStruct((r_total, o[0]), o[1]) for o in outs]
            + [jax.ShapeDtypeStruct((1, w), F32) for w in reds],
            compiler_params=_params(("arbitrary",)), name=name + "_f",
        )(*row_args(rv), *rv[nr:])
        return tuple(res)

    def bwd_call(rv, cts):
        def body(*refs):
            t = pl.program_id(0)
            rp, vp, rest = load(refs, t)
            ct_o = [_split(rest[k][...].astype(F32), outs[k][2]) for k in range(n_out)]
            ct_r = [rest[n_out + k][...] for k in range(n_red)]
            rest = rest[n_out + n_red:]
            _, vjp = jax.vjp(fn, rp, vp)
            d_rp, d_vp = vjp((ct_o, ct_r))
            pos = 0
            for i in range(nr):
                if halo[i]:
                    for c in range(3):
                        rest[pos + c][...] = _cat([p[c] for p in d_rp[i]])
                    pos += 3
                else:
                    rest[pos][...] = _cat(d_rp[i]).astype(rows[i].dtype)
                    pos += 1
            for j in range(nv):
                ref, val = rest[pos + j], _cat(d_vp[j])
                start = jnp.logical_or(t == 0, t == ctx_tiles) if seg[j] else t == 0

                @pl.when(start)
                def _():
                    ref[...] = val

                @pl.when(jnp.logical_not(start))
                def _():
                    ref[...] += val

        d_row_specs, d_row_shapes = [], []
        for i, r in enumerate(rows):
            w = r.shape[1]
            for _ in range(3 if halo[i] else 1):
                d_row_specs.append(pl.BlockSpec((tr, w), lambda t: (t, 0)))
                d_row_shapes.append(jax.ShapeDtypeStruct(r.shape, F32 if halo[i] else r.dtype))
        res = pl.pallas_call(
            body, grid=(nt,),
            in_specs=row_specs() + [vec_spec(j) for j in range(nv)]
            + [pl.BlockSpec((tr, o[0]), lambda t: (t, 0)) for o in outs]
            + [pl.BlockSpec((1, w), lambda t: (0, 0)) for w in reds],
            out_specs=d_row_specs + [vec_spec(j) for j in range(nv)],
            out_shape=d_row_shapes + [jax.ShapeDtypeStruct(v.shape, F32) for v in vecs],
            compiler_params=_params(("arbitrary",)), name=name + "_b",
        )(*row_args(rv), *rv[nr:], *cts)
        d_rows, pos = [], 0
        for i in range(nr):
            if halo[i]:
                d_rows.append(_unshift(res[pos], res[pos + 1], res[pos + 2], ctx_tiles * tr).astype(rows[i].dtype))
                pos += 3
            else:
                d_rows.append(res[pos])
                pos += 1
        return tuple(d_rows) + tuple(res[pos:])

    @jax.custom_vjp
    def f(*rv):
        return fwd_call(*rv)

    f.defvjp(lambda *rv: (fwd_call(*rv), rv), lambda rv, cts: bwd_call(rv, cts))
    return f(*rows, *vecs)


def _unshift(d, dm, dp, ctx_rows):
    r = d.shape[0]
    t = lax.broadcasted_iota(jnp.int32, (r, 1), 0)
    zero = jnp.zeros((1, d.shape[1]), d.dtype)
    from_m = jnp.concatenate([dm[1:], zero], axis=0)
    from_p = jnp.concatenate([zero, dp[:-1]], axis=0)
    from_m = jnp.where(t == ctx_rows - 1, 0.0, from_m)
    from_p = jnp.where(t == ctx_rows, 0.0, from_p)
    return d + from_m + from_p


def _sigmoid(x):
    return 0.5 * (jnp.tanh(0.5 * x) + 1.0)


def _silu(x):
    return x * _sigmoid(x)


def _softplus(x):
    return jnp.maximum(x, 0.0) + jnp.log(1.0 + jnp.exp(-jnp.abs(x)))


def _rms(x, g):
    return x * lax.rsqrt(jnp.mean(x * x, axis=-1, keepdims=True) + EPS) * g


def _fn_modulate(r, v):
    (x,), (g,), (sh,), (sc,) = r[0], v[0], v[1], v[2]
    return [[_rms(x, g) * (1.0 + sc) + sh]], []


def _fn_resmod(r, v):
    (x,), (y,) = r
    (gate,), (g,), (sh,), (sc,) = v
    xn = x + gate * y
    return [[xn], [_rms(xn, g) * (1.0 + sc) + sh]], []


def _fn_final(r, v):
    (x,), (y,), (tgt,) = r
    (gate,), (g,) = v
    err = _rms(x + gate * y, g) - tgt
    row_loss = jnp.mean(err * err, axis=-1, keepdims=True)
    total = 0.5 * jnp.sum(row_loss, axis=0, keepdims=True)
    return [], [jnp.broadcast_to(total, (1, LANES))]


def _fn_dnprep(r, v):
    qkv, (ab,) = r
    w0, w1, w2, (alog,), (dtb,) = v
    out = [[], [], []]
    for n, (x, xm, xp) in enumerate(qkv):
        which = n // DN_HEADS
        y = _silu(xm * w0[n] + x * w1[n] + xp * w2[n])
        if which < 2:
            y = y * lax.rsqrt(jnp.sum(y * y, axis=-1, keepdims=True) + EPS)
        if which == 0:
            y = y * (HEAD_DIM ** -0.5)
        out[which].append(y)
    lane = lax.broadcasted_iota(jnp.int32, ab.shape, 1)
    g = -jnp.exp(alog) * _softplus(ab + dtb)
    gb = jnp.where(lane < N_DIRHEAD, g, jnp.where(lane < 2 * N_DIRHEAD, _sigmoid(ab), 0.0))
    return out + [[gb]], []


def _fn_dnpost(r, v):
    of, ob, z = r
    (g,) = v[0]
    return [[_rms(a + b, g) * _silu(c) for a, b, c in zip(of, ob, z)]], []


def _fn_sub(r, v):
    return [[r[0][0] - r[1][0]]], []


def _fn_scale(r, v):
    return [[r[0][0] * v[0][0]]], []


def _fn_shortconv(r, v):
    (xin, gb, gc), = r
    (w0,), (w1,), (w2,) = v
    u, um, up = (gc[k] * xin[k] for k in range(3))
    return [[gb[0] * (um * w0 + u * w1 + up * w2)]], []


def _fn_merge(r, v):
    gates, (ya,), (yb,), (yc,) = r
    return [[_sigmoid(gates[0]) * ya + _sigmoid(gates[1]) * yb + _sigmoid(gates[2]) * yc]], []


def _fn_swiglu(r, v):
    return [[_silu(r[0][0]) * r[1][0]]], []


def _dot_hi(a, b):
    return lax.dot_general(a, b, (NN, ((), ())), precision=HIGHEST, preferred_element_type=F32)


def _bdot(a, b, dims):
    (ca,), (cb,) = dims
    return lax.dot_general(a.astype(MXU_DTYPE), b.astype(MXU_DTYPE), (((ca + 1,), (cb + 1,)), ((0,), (0,))),
                           preferred_element_type=F32)


def _hi_lo(a):
    hi = a.astype(MXU_DTYPE)
    return hi, (a - hi.astype(F32)).astype(MXU_DTYPE)


def _bdot3_raw(a, b, dims):
    (ah, al), (bh, bl) = _hi_lo(a), _hi_lo(b)
    return _bdot(ah, bh, dims) + (_bdot(ah, bl, dims) + _bdot(al, bh, dims))


@jax.custom_vjp
def _bdot3(a, b):
    return _bdot3_raw(a, b, NN)


_bdot3.defvjp(lambda a, b: (_bdot3_raw(a, b, NN), (a, b)),
              lambda res, ct: (_bdot3_raw(ct, res[1], NT), _bdot3_raw(res[0], ct, TN)))


def _dn_pre(q4, k4, v4, gb):
    nb, c = N_DIRHEAD, q4.shape[1]
    q, k, v = (jnp.concatenate([a, a], axis=0) for a in (q4, k4, v4))
    lane = lax.broadcasted_iota(jnp.int32, gb.shape, 1)
    col = lambda j: jnp.sum(jnp.where(lane == j, gb, 0.0), axis=1, keepdims=True)
    g_col = jnp.stack([col(j) for j in range(nb)])
    b_col = jnp.stack([col(nb + j) for j in range(nb)])
    bi, ii, jj = (lax.broadcasted_iota(jnp.int32, (nb, c, c), a) for a in range(3))
    ahead = jnp.where(bi >= DN_HEADS, jj - ii, ii - jj)
    incl, strict, eye = ahead >= 0, ahead > 0, ahead == 0
    g_row = jnp.sum(jnp.where(eye, g_col, 0.0), axis=1, keepdims=True)
    gc_col = jnp.sum(jnp.where(incl, g_row, 0.0), axis=2, keepdims=True)
    gc_row = jnp.sum(jnp.where(ahead <= 0, g_col, 0.0), axis=1, keepdims=True)
    decay = jnp.where(incl, jnp.exp(jnp.where(incl, gc_col - gc_row, 0.0)), 0.0)
    kb = k * b_col
    a = _bdot(kb, k, NT) * jnp.where(strict, decay, 0.0)
    t = jnp.where(eye, 1.0, 0.0) - a
    p = a
    for _ in range(int(math.log2(c)) - 1):
        p = _bdot3(p, p)
        t = t + _bdot3(t, p)
    e_gc = jnp.exp(gc_col)
    u = _bdot3(t, v * b_col)
    w = _bdot3(t, kb * e_gc)
    g_last = jnp.sum(g_col, axis=1, keepdims=True)
    k_state = k * jnp.exp(g_last - gc_col)
    a_qk = _bdot(q, k, NT) * decay
    return u, w, q * e_gc, k_state, a_qk, jnp.broadcast_to(jnp.exp(g_last), (nb, 1, LANES))


def _dn_step(s, u, w, qd, ks, aqk, gl):
    v_new = u - _bdot(w, s, NN)
    o = _bdot(qd, s, NN) + _bdot(aqk, v_new, NN)
    return s * gl[:, :, :1] + _bdot(ks, v_new, TN), o


def _heads(x):
    return jnp.stack([x[:, h * HEAD_DIM:(h + 1) * HEAD_DIM] for h in range(DN_HEADS)])


def _unheads(x):
    return jnp.concatenate([x[h] for h in range(x.shape[0])], axis=-1)


def _dn_rev(t, nc, n):
    return jnp.where(t < nc, nc - 1 - t, n - 1 - (t - nc))


def _pre_shapes(n):
    c, d, b = DN_CHUNK, HEAD_DIM, N_DIRHEAD
    shapes = [(n, b, c, d)] * 4 + [(n, b, c, c), (n, b, 1, LANES)]
    return shapes, [pl.BlockSpec((None,) + s[1:], lambda t: (t, 0, 0, 0)) for s in shapes]


def _dn_pre_fwd(name, q, k, v, gb):
    n = q.shape[0] // DN_CHUNK
    shapes, specs = _pre_shapes(n)

    def body(q_ref, k_ref, v_ref, g_ref, *o_refs):
        res = _dn_pre(_heads(q_ref[...]), _heads(k_ref[...]), _heads(v_ref[...]), g_ref[...])
        for r, o in zip(res, o_refs):
            o[...] = r

    wide = pl.BlockSpec((DN_CHUNK, DN_HEADS * HEAD_DIM), lambda t: (t, 0))
    narrow = pl.BlockSpec((DN_CHUNK, LANES), lambda t: (t, 0))
    return pl.pallas_call(body, grid=(n,), in_specs=[wide] * 3 + [narrow], out_specs=specs,
                          out_shape=[jax.ShapeDtypeStruct(s, F32) for s in shapes],
                          compiler_params=_params(("parallel",)), name=name + "_pre_f")(q, k, v, gb)


def _dn_pre_bwd(name, q, k, v, gb, cts):
    n = q.shape[0] // DN_CHUNK
    _, specs = _pre_shapes(n)
    n_ct = len(specs)

    def body(q_ref, k_ref, v_ref, g_ref, *refs):
        ct = tuple(r[...] for r in refs[:n_ct])
        _, vjp = jax.vjp(_dn_pre, _heads(q_ref[...]), _heads(k_ref[...]), _heads(v_ref[...]), g_ref[...])
        dq, dk, dv, dg = vjp(ct)
        for r, val in zip(refs[n_ct:], (_unheads(dq), _unheads(dk), _unheads(dv), dg)):
            r[...] = val

    wide = pl.BlockSpec((DN_CHUNK, DN_HEADS * HEAD_DIM), lambda t: (t, 0))
    narrow = pl.BlockSpec((DN_CHUNK, LANES), lambda t: (t, 0))
    return pl.pallas_call(body, grid=(n,), in_specs=[wide] * 3 + [narrow] + specs, out_specs=[wide] * 3 + [narrow],
                          out_shape=[jax.ShapeDtypeStruct(q.shape, F32)] * 3 + [jax.ShapeDtypeStruct(gb.shape, F32)],
                          compiler_params=_params(("parallel",)), name=name + "_pre_b")(q, k, v, gb, *cts)


def _scan_specs(pre_shapes, fw, bw):
    specs = []
    for s in pre_shapes:
        blk = (None, DN_HEADS) + s[2:]
        specs += [pl.BlockSpec(blk, lambda t: (fw(t), 0, 0, 0)), pl.BlockSpec(blk, lambda t: (bw(t), 1, 0, 0))]
    return specs


def _dn_scan_fwd(name, pre, nc):
    n = pre[0].shape[0]
    r = n * DN_CHUNK
    hd, nh, nb = HEAD_DIM, DN_HEADS, N_DIRHEAD
    shapes, _ = _pre_shapes(n)
    n_in = 2 * len(shapes)

    def body(*refs):
        ins, (of_ref, ob_ref, sf_ref, sb_ref, s_scr) = refs[:n_in], refs[n_in:]
        t = pl.program_id(0)

        @pl.when(t == 0)
        def _():
            s_scr[...] = jnp.zeros_like(s_scr)

        s = s_scr[...]
        sf_ref[...] = s[:nh]
        sb_ref[...] = s[nh:]
        args = [jnp.concatenate([ins[2 * i][...], ins[2 * i + 1][...]], axis=0) for i in range(n_in // 2)]
        s2, o = _dn_step(s, *args)
        s_scr[...] = s2
        of_ref[...] = _unheads(o[:nh])
        ob_ref[...] = _unheads(o[nh:])

    fw = lambda t: t
    bw = lambda t: _dn_rev(t, nc, n)
    wide, st = (DN_CHUNK, nh * hd), (None, nh, hd, hd)
    return pl.pallas_call(
        body, grid=(n,), in_specs=_scan_specs(shapes, fw, bw),
        out_specs=[pl.BlockSpec(wide, lambda t: (fw(t), 0)), pl.BlockSpec(wide, lambda t: (bw(t), 0)),
                   pl.BlockSpec(st, lambda t: (fw(t), 0, 0, 0)), pl.BlockSpec(st, lambda t: (bw(t), 0, 0, 0))],
        out_shape=[jax.ShapeDtypeStruct((r, nh * hd), F32)] * 2 + [jax.ShapeDtypeStruct((n, nh, hd, hd), F32)] * 2,
        scratch_shapes=[pltpu.VMEM((nb, hd, hd), F32)],
        compiler_params=_params(("arbitrary",)), name=name + "_scan_f",
    )(*[a for a in pre for _ in range(2)])


def _dn_scan_bwd(name, pre, sall_f, sall_b, do_f, do_b, nc):
    n = pre[0].shape[0]
    hd, nh, nb = HEAD_DIM, DN_HEADS, N_DIRHEAD
    shapes, _ = _pre_shapes(n)
    n_in = 2 * len(shapes)

    def body(*refs):
        ins, (sf_ref, sb_ref, dof_ref, dob_ref) = refs[:n_in], refs[n_in:n_in + 4]
        outs, ds_scr = refs[n_in + 4:2 * n_in + 4], refs[2 * n_in + 4]
        t = pl.program_id(0)

        @pl.when(t == 0)
        def _():
            ds_scr[...] = jnp.zeros_like(ds_scr)

        args = [jnp.concatenate([ins[2 * i][...], ins[2 * i + 1][...]], axis=0) for i in range(n_in // 2)]
        s = jnp.concatenate([sf_ref[...], sb_ref[...]], axis=0)
        do = jnp.concatenate([_heads(dof_ref[...]), _heads(dob_ref[...])], axis=0)
        _, vjp = jax.vjp(_dn_step, s, *args)
        cts = vjp((ds_scr[...], do))
        ds_scr[...] = cts[0]
        for i, ct in enumerate(cts[1:]):
            outs[2 * i][...] = ct[:nh]
            outs[2 * i + 1][...] = ct[nh:]

    fw = lambda t: n - 1 - t
    bw = lambda t: _dn_rev(n - 1 - t, nc, n)
    wide, st = (DN_CHUNK, nh * hd), (None, nh, hd, hd)
    half_shapes, half_specs = [], []
    for s in shapes:
        blk = (None, nh) + s[2:]
        half_shapes += [jax.ShapeDtypeStruct((n, nh) + s[2:], F32)] * 2
        half_specs += [pl.BlockSpec(blk, lambda t: (fw(t), 0, 0, 0)), pl.BlockSpec(blk, lambda t: (bw(t), 0, 0, 0))]
    res = pl.pallas_call(
        body, grid=(n,),
        in_specs=_scan_specs(shapes, fw, bw)
        + [pl.BlockSpec(st, lambda t: (fw(t), 0, 0, 0)), pl.BlockSpec(st, lambda t: (bw(t), 0, 0, 0)),
           pl.BlockSpec(wide, lambda t: (fw(t), 0)), pl.BlockSpec(wide, lambda t: (bw(t), 0))],
        out_specs=half_specs, out_shape=half_shapes,
        scratch_shapes=[pltpu.VMEM((nb, hd, hd), F32)],
        compiler_params=_params(("arbitrary",)), name=name + "_scan_b",
    )(*[a for a in pre for _ in range(2)], sall_f, sall_b, do_f, do_b)
    return tuple(jnp.concatenate([res[2 * i], res[2 * i + 1]], axis=1) for i in range(len(shapes)))


def deltanet(name, q, k, v, gb, ctx_rows):
    nc = ctx_rows // DN_CHUNK

    @jax.custom_vjp
    def pre(q, k, v, gb):
        return tuple(_dn_pre_fwd(name, q, k, v, gb))

    pre.defvjp(lambda *a: (tuple(_dn_pre_fwd(name, *a)), a), lambda a, cts: tuple(_dn_pre_bwd(name, *a, cts)))

    @jax.custom_vjp
    def scan(*ops):
        return tuple(_dn_scan_fwd(name, ops, nc)[:2])

    def scan_fwd(*ops):
        of, ob, sf, sb = _dn_scan_fwd(name, ops, nc)
        return (of, ob), (ops, sf, sb)

    scan.defvjp(scan_fwd, lambda res, cts: _dn_scan_bwd(name, res[0], res[1], res[2], cts[0], cts[1], nc))
    return scan(*pre(q, k, v, gb))


def _box_matrix(l, w):
    lo, hi = w // 2, w - 1 - w // 2
    pos = np.arange(l)
    start, end = np.clip(pos - lo, 0, l), np.clip(pos + hi + 1, 0, l)
    col = np.arange(l)[None, :]
    return ((col >= start[:, None]) & (col < end[:, None])) / (end - start)[:, None].astype(np.float64)


def _pool_matrices(ctx_rows, grid_rows):
    assert ctx_rows == ROW_TILE and ROW_TILE % GRID_W == 0
    ctx = np.stack([_box_matrix(ctx_rows, w) for w in POOL_WINDOWS])
    cols = np.stack([np.kron(np.eye(ROW_TILE // GRID_W), _box_matrix(GRID_W, w)) for w in POOL_WINDOWS])
    rows = np.stack([_box_matrix(grid_rows, w) for w in POOL_WINDOWS])[None]
    return np.stack([ctx, cols]).astype(np.float32), rows.astype(np.float32)


def _pool_apply(name, x, mats, group_w, seg_tiles, lane_tile):
    r, w = x.shape
    b = mats.shape[-1]
    ng = mats.shape[1]
    period = ng * group_w

    def body(x_ref, m_ref, o_ref):
        xv = x_ref[...]
        lane = lax.broadcasted_iota(jnp.int32, xv.shape, 1)
        grp = (lane % period) // group_w
        acc = jnp.zeros_like(xv)
        for g in range(ng):
            acc = acc + jnp.where(grp == g, _dot_hi(m_ref[g], xv), 0.0)
        o_ref[...] = acc

    return pl.pallas_call(
        body, grid=(r // b, w // lane_tile),
        in_specs=[pl.BlockSpec((b, lane_tile), lambda i, j: (i, j)),
                  pl.BlockSpec((None, ng, b, b), lambda i, j: (jnp.where(i >= seg_tiles, 1, 0) if mats.shape[0] > 1 else 0, 0, 0, 0))],
        out_specs=pl.BlockSpec((b, lane_tile), lambda i, j: (i, j)),
        out_shape=jax.ShapeDtypeStruct(x.shape, F32),
        compiler_params=_params(("parallel", "parallel")), name=name,
    )(x, mats)


def pool_means(name, u, ctx_rows):
    r, pw = u.shape
    grid_rows = (r - ctx_rows) // GRID_W
    gw = pw // len(POOL_WINDOWS)
    m1, m2 = _pool_matrices(ctx_rows, grid_rows)
    lane_tile = min(2048, GRID_W * pw)

    def apply(x, a1, a2, tag):
        y = _pool_apply(name + tag + "1", x, jnp.asarray(a1), gw, ctx_rows // ROW_TILE, pw)
        lat = y[ctx_rows:].reshape(grid_rows, GRID_W * pw)
        lat = _pool_apply(name + tag + "2", lat, jnp.asarray(a2), gw, 0, lane_tile)
        return jnp.concatenate([y[:ctx_rows], lat.reshape(r - ctx_rows, pw)], axis=0)

    @jax.custom_vjp
    def f(u):
        return apply(u, m1, m2, "_f")

    tr = lambda m: np.ascontiguousarray(np.swapaxes(m, -1, -2))
    f.defvjp(lambda u: (apply(u, m1, m2, "_f"), None), lambda _, ct: (apply(ct, tr(m1), tr(m2), "_b"),))
    return f(u)


def _ew(name, fn, *xs, n_out=1):
    shapes = jax.eval_shape(lambda *a: fn(*a), *xs)
    shapes = shapes if isinstance(shapes, (tuple, list)) else (shapes,)

    def body(*refs):
        res = fn(*[r[...] for r in refs[:len(xs)]])
        res = res if isinstance(res, (tuple, list)) else (res,)
        for r, o in zip(res, refs[len(xs):]):
            o[...] = r

    out = pl.pallas_call(body, out_shape=[jax.ShapeDtypeStruct(s.shape, s.dtype) for s in shapes],
                         compiler_params=_params(), name=name)(*xs)
    return out[0] if len(shapes) == 1 else tuple(out)


def _adamw_math(w, g, m, v):
    m2 = ADAM_B1 * m + (1.0 - ADAM_B1) * g
    v2 = ADAM_B2 * v + (1.0 - ADAM_B2) * (g * g)
    m_hat = m2 / (1.0 - ADAM_B1 ** ADAM_STEP)
    v_hat = v2 / (1.0 - ADAM_B2 ** ADAM_STEP)
    delta = -ADAM_LR * (m_hat / (jnp.sqrt(v_hat) + ADAM_EPS) + ADAM_WD * w)
    return delta, m2, v2


def adamw(name, w, g, m, v):
    r, c = w.shape
    tr = _pick(r, (256, 128, 64, 32, 16, 8))

    def body(w_ref, g_ref, m_ref, v_ref, d_ref, m2_ref, v2_ref):
        d_ref[...], m2_ref[...], v2_ref[...] = _adamw_math(w_ref[...], g_ref[...], m_ref[...], v_ref[...])

    spec = pl.BlockSpec((tr, c), lambda i: (i, 0))
    return pl.pallas_call(body, grid=(r // tr,), in_specs=[spec] * 4, out_specs=[spec] * 3,
                          out_shape=[jax.ShapeDtypeStruct(w.shape, F32)] * 3,
                          compiler_params=_params(("parallel",)), name=name)(w, g, m, v)


def _sum_rows(name, xs):
    r, c = xs[0].shape
    tr = _pick(r, (512, 256, 128, 64, 32, 16, 8))

    def body(*refs):
        acc = refs[0][...]
        for ref in refs[1:-1]:
            acc = acc + ref[...]
        refs[-1][...] = acc

    spec = pl.BlockSpec((tr, c), lambda i: (i, 0))
    return pl.pallas_call(body, grid=(r // tr,), in_specs=[spec] * len(xs), out_specs=spec,
                          out_shape=jax.ShapeDtypeStruct((r, c), xs[0].dtype),
                          compiler_params=_params(("parallel",)), name=name)(*xs)


def _place():
    return lax.axis_index("x"), lax.axis_index("y"), lax.axis_index("c")


def _chip_peers(x, y, c):
    return [(1 - x, y, c), (x, 1 - y, c), (1 - x, 1 - y, c)]


def all_gather8(name, block):
    m_per, n = block.shape

    def body(x_ref, out_ref, send_sems, recv_sems, local_sem):
        x, y, c = _place()
        me, sibling = (x, y, c), (x, y, 1 - c)
        chips = [(1 - x, y), (x, 1 - y), (1 - x, 1 - y)]

        def rows(px, py, pc):
            return out_ref.at[pl.ds((4 * px + 2 * py + pc) * m_per, m_per), :]

        def copy(k, blk, to, src=None):
            return pltpu.make_async_remote_copy(
                src_ref=rows(*blk) if src is None else src, dst_ref=rows(*blk),
                send_sem=send_sems.at[k], recv_sem=recv_sems.at[k], device_id=to, device_id_type=MESH_ID)

        mine = pltpu.make_async_copy(x_ref, rows(*me), local_sem)
        mine.start()
        first = [copy(0, me, sibling, src=x_ref)]
        first += [copy(1 + j, me, (*chip, c), src=x_ref) for j, chip in enumerate(chips)]
        for cp in first:
            cp.start()
        passed = [copy(4 + j, (*chip, c), sibling) for j, chip in enumerate(chips)]
        for j, chip in enumerate(chips):
            copy(1 + j, (*chip, c), me).wait_recv()
            passed[j].start()
        copy(0, sibling, me).wait_recv()
        for j, chip in enumerate(chips):
            copy(4 + j, (*chip, 1 - c), me).wait_recv()
        for cp in first + passed:
            cp.wait_send()
        mine.wait()

    return pl.pallas_call(
        body, out_shape=jax.ShapeDtypeStruct((8 * m_per, n), block.dtype),
        in_specs=[pl.BlockSpec(memory_space=pltpu.VMEM)], out_specs=pl.BlockSpec(memory_space=pltpu.VMEM),
        scratch_shapes=[pltpu.SemaphoreType.DMA((7,)), pltpu.SemaphoreType.DMA((7,)), pltpu.SemaphoreType.DMA],
        compiler_params=_params(), name=name,
    )(block)


def gather_chips(name, shard):
    def body(x_ref, out_ref, send_sems, recv_sems, local_sem):
        x, y, c = _place()
        mine = pltpu.make_async_copy(x_ref, out_ref.at[2 * x + y], local_sem)
        mine.start()
        copies = []
        for p, peer in enumerate(_chip_peers(x, y, c)):
            cp = pltpu.make_async_remote_copy(
                src_ref=x_ref, dst_ref=out_ref.at[2 * x + y], send_sem=send_sems.at[p], recv_sem=recv_sems.at[p],
                device_id=peer, device_id_type=MESH_ID)
            cp.start()
            copies.append(cp)
        for p, (px, py, _) in enumerate(_chip_peers(x, y, c)):
            pltpu.make_async_remote_copy(
                src_ref=x_ref, dst_ref=out_ref.at[2 * px + py], send_sem=send_sems.at[p], recv_sem=recv_sems.at[p],
                device_id=(px, py, c), device_id_type=MESH_ID).wait_recv()
        for cp in copies:
            cp.wait_send()
        mine.wait()

    hbm = pl.BlockSpec(memory_space=pltpu.HBM)
    return pl.pallas_call(
        body, out_shape=jax.ShapeDtypeStruct((4,) + shard.shape, shard.dtype), in_specs=[hbm], out_specs=hbm,
        scratch_shapes=[pltpu.SemaphoreType.DMA((3,)), pltpu.SemaphoreType.DMA((3,)), pltpu.SemaphoreType.DMA],
        compiler_params=_params(), name=name,
    )(shard)


def swap_sibling(name, block):
    def body(x_ref, out_ref, send_sem, recv_sem):
        x, y, c = _place()
        cp = pltpu.make_async_remote_copy(src_ref=x_ref, dst_ref=out_ref, send_sem=send_sem, recv_sem=recv_sem,
                                          device_id=(x, y, 1 - c), device_id_type=MESH_ID)
        cp.start()
        cp.wait()

    hbm = pl.BlockSpec(memory_space=pltpu.HBM)
    return pl.pallas_call(
        body, out_shape=jax.ShapeDtypeStruct(block.shape, block.dtype), in_specs=[hbm], out_specs=hbm,
        scratch_shapes=[pltpu.SemaphoreType.DMA, pltpu.SemaphoreType.DMA],
        compiler_params=_params(), name=name,
    )(block)


def scatter_chips(name, pieces):
    def body(x_ref, out_ref, send_sems, recv_sems):
        x, y, c = _place()
        copies = []
        for p, (px, py, pc) in enumerate(_chip_peers(x, y, c)):
            cp = pltpu.make_async_remote_copy(
                src_ref=x_ref.at[2 * px + py], dst_ref=out_ref.at[p], send_sem=send_sems.at[p], recv_sem=recv_sems.at[p],
                device_id=(px, py, pc), device_id_type=MESH_ID)
            cp.start()
            copies.append(cp)
        for cp in copies:
            cp.wait()

    hbm = pl.BlockSpec(memory_space=pltpu.HBM)
    return pl.pallas_call(
        body, out_shape=jax.ShapeDtypeStruct((3,) + pieces.shape[1:], pieces.dtype), in_specs=[hbm], out_specs=hbm,
        scratch_shapes=[pltpu.SemaphoreType.DMA((3,)), pltpu.SemaphoreType.DMA((3,))],
        compiler_params=_params(), name=name,
    )(pieces)


def _sum_devices(name, got, fold=False):
    def body(a_ref, *o_refs):
        acc = a_ref[0]
        for i in range(1, N_DEV):
            acc = acc + a_ref[i]
        o_refs[0][...] = acc
        if fold:
            o_refs[1][...] = acc + pltpu.roll(acc, SUBLANES // 2, 0)

    shape = jax.ShapeDtypeStruct(got.shape[1:], F32)
    out = pl.pallas_call(body, out_shape=[shape] * (2 if fold else 1), compiler_params=_params(), name=name)(got)
    return out if fold else out[0]


def split_cols(w, bounds):
    edges = list(zip(bounds[:-1], bounds[1:]))

    def cut(w):
        return tuple(w[:, a:b] for a, b in edges)

    f = jax.custom_vjp(cut)
    f.defvjp(lambda w: (cut(w), None), lambda _, cts: (jnp.concatenate(cts, axis=1),))
    return f(w)


def _row(v):
    return v.reshape(1, -1)


def _pad_lanes(v, width=LANES):
    return jnp.pad(v, ((0, 0), (0, width - v.shape[1])))


def _block_diag(blocks):
    g, n, _ = blocks.shape
    out = jnp.zeros((g * n, g * n), blocks.dtype)
    for i in range(g):
        out = out.at[i * n:(i + 1) * n, i * n:(i + 1) * n].set(blocks[i])
    return out


def local_loss(p, x, mod_lat, mod_ctx, ctx, target):
    ctx_rows, d = ctx.shape
    depth = p["w_in"].shape[0]
    pw, scw = p["pool_scale"].shape[1], p["sc_conv_w"].shape[2]
    ff = p["w_down"].shape[1]
    ct = ctx_rows // ROW_TILE
    off_z = 3 * DN_WIDTH
    off_a = off_z + DN_WIDTH
    off_pool = off_a + 2 * N_DIRHEAD
    off_sc = off_pool + pw
    off_gate = off_sc + 3 * scw
    n_in = off_gate + 3 * d
    xs = jnp.concatenate([ctx, x], axis=0)
    seg = lambda l, k: jnp.stack([mod_ctx[l, k], mod_lat[l, k]]).reshape(2, 1, d)
    dn = gate2 = None
    for l in range(depth):
        sh1, sc1, g1, sh2, sc2, g2 = (seg(l, k) for k in range(6))
        tag = f"l{l}_"
        if l == 0:
            (h1,) = rowwise(tag + "mod", _fn_modulate, [xs], [_row(p["norm1_g"][l]), sh1, sc1], seg=(False, True, True),
                            outs=[(d, F32, 1)], ctx_tiles=ct)
        else:
            xs, h1 = rowwise(tag + "resmod1", _fn_resmod, [xs, dn], [gate2, _row(p["norm1_g"][l]), sh1, sc1],
                             seg=(True, False, True, True), outs=[(d, F32, 1), (d, F32, 1)], ctx_tiles=ct)
        w_qkv, w_z, w_ab, w_pool, w_sc, w_gate = split_cols(p["w_in"][l], (0, off_z, off_a, off_pool, off_sc, off_gate, n_in))
        p_qkv = mm(tag + "qkv", h1, w_qkv)
        p_z = mm(tag + "z", h1, w_z)
        p_ab = mm(tag + "ab", h1, _pad_lanes(w_ab))
        p_pool = mm(tag + "pool", h1, w_pool)
        p_sc = mm(tag + "sc", h1, w_sc)
        p_gate = mm(tag + "gate", h1, w_gate)
        cw = p["dn_conv_w"][l]
        q, k, v, gb = rowwise(
            tag + "dnprep", _fn_dnprep, [p_qkv, p_ab],
            [cw[0:1], cw[1:2], cw[2:3], _pad_lanes(p["dn_a_log"][l].reshape(1, -1)), _pad_lanes(p["dn_dt_bias"][l].reshape(1, -1))],
            halo=(True, False), parts_r=(3 * DN_HEADS, 1), parts_v=(3 * DN_HEADS,) * 3 + (1, 1),
            outs=[(DN_WIDTH, F32, DN_HEADS)] * 3 + [(LANES, F32, 1)], ctx_tiles=ct)
        o_f, o_b = deltanet(tag + "dn", q, k, v, gb, ctx_rows)
        (oz,) = rowwise(tag + "dnpost", _fn_dnpost, [o_f, o_b, p_z], [_row(p["dn_norm_g"][l])], parts_r=(DN_HEADS,) * 3,
                        outs=[(DN_WIDTH, MXU_DTYPE, DN_HEADS)], ctx_tiles=ct)
        y_a = mm(tag + "bra", oz, p["w_br_a"][l])
        means = pool_means(tag + "box", p_pool, ctx_rows)
        (dpool,) = rowwise(tag + "poolsub", _fn_sub, [means, p_pool], [], outs=[(pw, MXU_DTYPE, 1)], ctx_tiles=ct)
        mixed = mm(tag + "poolw", dpool, _block_diag(p["pool_w"][l]))
        (yb_in,) = rowwise(tag + "poolscale", _fn_scale, [mixed], [_row(p["pool_scale"][l])], outs=[(pw, MXU_DTYPE, 1)], ctx_tiles=ct)
        y_b = mm(tag + "brb", yb_in, p["w_br_b"][l])
        sw = p["sc_conv_w"][l]
        (yc_in,) = rowwise(tag + "sconv", _fn_shortconv, [p_sc], [sw[0:1], sw[1:2], sw[2:3]], halo=(True,), parts_r=(3,),
                           outs=[(scw, MXU_DTYPE, 1)], ctx_tiles=ct)
        y_c = mm(tag + "brc", yc_in, p["w_br_c"][l])
        (y,) = rowwise(tag + "merge", _fn_merge, [p_gate, y_a, y_b, y_c], [], parts_r=(3, 1, 1, 1),
                       outs=[(d, MXU_DTYPE, 1)], ctx_tiles=ct)
        mix = mm(tag + "wo", y, p["w_o"][l])
        xs, h2 = rowwise(tag + "resmod2", _fn_resmod, [xs, mix], [g1, _row(p["norm2_g"][l]), sh2, sc2],
                         seg=(True, False, True, True), outs=[(d, F32, 1), (d, F32, 1)], ctx_tiles=ct)
        w_g, w_u = split_cols(p["w_gu"][l], (0, ff, 2 * ff))
        (act,) = rowwise(tag + "swiglu", _fn_swiglu, [mm(tag + "ffg", h2, w_g), mm(tag + "ffu", h2, w_u)], [],
                         outs=[(ff, MXU_DTYPE, 1)], ctx_tiles=ct)
        dn = mm(tag + "down", act, p["w_down"][l])
        gate2 = g2
    (total,) = rowwise("final", _fn_final, [xs[ctx_rows:], dn[ctx_rows:], target],
                       [gate2[1], _row(p["final_norm_g"])], reds=[LANES], ctx_tiles=0)
    return total[0, 0]


BIG = ("w_in", "w_br_a", "w_br_b", "w_br_c", "w_o", "w_gu", "w_down")
ROW_SHARDED = ("w_o", "w_down")
SMALL_REPL = ("norm1_g", "norm2_g", "dn_a_log", "dn_dt_bias", "dn_norm_g", "pool_w", "pool_scale", "final_norm_g")
SMALL_SHARD = ("dn_conv_w", "sc_conv_w")
WEIGHTS = ("c_ctx", "w_ada", "b_ada", "norm1_g", "norm2_g", "w_in", "dn_conv_w", "dn_a_log", "dn_dt_bias", "dn_norm_g", "pool_w",
           "pool_scale", "sc_conv_w", "w_br_a", "w_br_b", "w_br_c", "w_o", "w_gu", "w_down", "final_norm_g")
N_CHIPS, N_DEV = 4, 8
COMM_COLS = 1024


def _pack(arrays, rows_multiple, cols=COMM_COLS, dtype=F32):
    flat = jnp.concatenate([a.astype(dtype).reshape(-1) for a in arrays])
    rows = -(-flat.shape[0] // (cols * rows_multiple)) * rows_multiple
    return jnp.pad(flat, (0, rows * cols - flat.shape[0])).reshape(rows, cols)


def _unpack(flat, shapes):
    out, pos = [], 0
    for s in shapes:
        n = int(np.prod(s))
        out.append(flat[pos:pos + n].reshape(s))
        pos += n
    return out


def _join_shards(parts, name):
    _, l, a, b = parts.shape
    if name in ROW_SHARDED:
        return jnp.moveaxis(parts, 0, 1).reshape(l, N_CHIPS * a, b)
    return jnp.moveaxis(parts, 0, 2).reshape(l, a, N_CHIPS * b)


def _cut_shard(full, name, j):
    if name in ROW_SHARDED:
        n = full.shape[1] // N_CHIPS
        return full[:, j * n:(j + 1) * n, :]
    n = full.shape[2] // N_CHIPS
    return full[:, :, j * n:(j + 1) * n]


def _dsilu(x):
    s = _sigmoid(x)
    return s + x * s * (1.0 - s)


def kernel(x, c, ctx, c_ctx, w_ada, b_ada, norm1_g, norm2_g, w_in, dn_conv_w, dn_a_log, dn_dt_bias, dn_norm_g, pool_w, pool_scale, sc_conv_w, w_br_a, w_br_b, w_br_c, w_o, w_gu, w_down, final_norm_g, loss_target, m_c_ctx, m_w_ada, m_b_ada, m_norm1_g, m_norm2_g, m_w_in, m_dn_conv_w, m_dn_a_log, m_dn_dt_bias, m_dn_norm_g, m_pool_w, m_pool_scale, m_sc_conv_w, m_w_br_a, m_w_br_b, m_w_br_c, m_w_o, m_w_gu, m_w_down, m_final_norm_g, v_c_ctx, v_w_ada, v_b_ada, v_norm1_g, v_norm2_g, v_w_in, v_dn_conv_w, v_dn_a_log, v_dn_dt_bias, v_dn_norm_g, v_pool_w, v_pool_scale, v_sc_conv_w, v_w_br_a, v_w_br_b, v_w_br_c, v_w_o, v_w_gu, v_w_down, v_final_norm_g):
    given = dict(locals())
    ix, iy, ic = _place()
    chip, dev = 2 * ix + iy, 4 * ix + 2 * iy + ic
    d = x.shape[-1]
    depth = w_in.shape[0]
    ada_cols = w_ada.shape[2]
    spare = 2 * SUBLANES - N_DEV - 1

    got0 = all_gather8("gather_cond", _pack([c, dn_conv_w, sc_conv_w], SUBLANES)).reshape(N_DEV, -1)
    c_all = got0[:, :d]
    taps = [_unpack(got0[2 * j, d:], [dn_conv_w.shape, sc_conv_w.shape]) for j in range(N_CHIPS)]
    full = {"dn_conv_w": jnp.concatenate([t[0] for t in taps], axis=-1),
            "sc_conv_w": jnp.concatenate([t[1] for t in taps], axis=-1)}
    big_shapes = [given[n].shape for n in BIG]
    shards = gather_chips("gather_weights", _pack([given[n] for n in BIG], 2 * SUBLANES, dtype=MXU_DTYPE))
    shards = shards.reshape(N_CHIPS, -1)
    per_chip = [_unpack(shards[j], big_shapes) for j in range(N_CHIPS)]
    for i, n in enumerate(BIG):
        full[n] = _join_shards(jnp.stack([per_chip[j][i] for j in range(N_CHIPS)]), n).astype(F32)
    for n in SMALL_REPL:
        full[n] = given[n]

    cond = jnp.concatenate([c_all, c_ctx[None], jnp.zeros((spare, d), F32)])
    s_cond = _ew("silu_cond", _silu, cond)
    mod_cols = jnp.concatenate([_mm(f"ada{l}_f", s_cond, w_ada[l], NN) for l in range(depth)], axis=1)
    mod_got = all_gather8("gather_mod", mod_cols).reshape(N_DEV, 2 * SUBLANES, depth, ada_cols)
    mod_all = jnp.concatenate([mod_got[2 * j] for j in range(N_CHIPS)], axis=-1) + b_ada[None]
    mod_lat = lax.dynamic_index_in_dim(mod_all, dev, 0, keepdims=False).reshape(depth, 6, d)
    mod_ctx = mod_all[N_DEV].reshape(depth, 6, d)

    loss_local, (g_full, grad_x, g_mod_lat, g_mod_ctx) = jax.value_and_grad(local_loss, argnums=(0, 1, 2, 3))(
        full, x[0], mod_lat, mod_ctx, ctx[0], loss_target[0])
    loss = lax.psum(loss_local, ("x", "y", "c"))

    dmod_cols = (2 * depth * 6 * d) // SUBLANES
    dmod = all_gather8("gather_dmod", _pack([g_mod_lat, g_mod_ctx], SUBLANES, cols=dmod_cols))
    dmod = dmod.reshape(N_DEV, SUBLANES, dmod_cols)
    dmod_sum, dmod_fold = _sum_devices("reduce_dmod", dmod, fold=True)
    half_rows = SUBLANES // 2
    grad_b_ada = dmod_fold[:half_rows].reshape(depth, 6 * d)
    dctx_sum = dmod_sum[half_rows:].reshape(1, depth, 6 * d)
    d9 = jnp.concatenate([dmod[:, :half_rows].reshape(N_DEV, depth, 6 * d), dctx_sum, jnp.zeros((spare, depth, 6 * d), F32)])
    d9 = lax.dynamic_slice_in_dim(d9, chip * ada_cols, ada_cols, axis=2)
    grad_w_ada = jnp.stack([_mm(f"ada{l}_dw", s_cond, d9[:, l], TN) for l in range(depth)])
    ds_cond = [_mm(f"ada{l}_da", d9[:, l], w_ada[l], NT) for l in range(depth)]
    dsilu_part = _sum_rows("sum_dcond", ds_cond)[N_DEV]

    small_names = SMALL_REPL + SMALL_SHARD
    small_grads = [dsilu_part] + [g_full[n] for n in small_names]
    small_shapes = [a.shape for a in small_grads]
    n_small = sum(int(np.prod(s)) for s in small_shapes)
    cols = -(-n_small // (SUBLANES * LANES)) * LANES
    got2 = all_gather8("gather_small", _pack(small_grads, SUBLANES, cols=cols)).reshape(N_DEV, SUBLANES, cols)
    small_sum = _unpack(_sum_devices("reduce_small", got2).reshape(-1), small_shapes)
    grads = dict(zip(small_names, small_sum[1:]))
    grads["c_ctx"] = _ew("dsilu", lambda g, z: 0.5 * g * _dsilu(z), _row(small_sum[0]), _row(c_ctx)).reshape(-1)
    grads["w_ada"], grads["b_ada"] = grad_w_ada, grad_b_ada
    for n in SMALL_SHARD:
        width = given[n].shape[-1]
        grads[n] = lax.dynamic_slice_in_dim(grads[n], chip * width, width, axis=-1)

    pieces = jnp.stack([_pack([_cut_shard(g_full[n], n, j) for n in BIG], 2 * SUBLANES) for j in range(N_CHIPS)])
    rows_h = pieces.shape[1] // 2
    halves = jnp.moveaxis(pieces.reshape(N_CHIPS, 2, rows_h, COMM_COLS), 1, 0)
    keep = lax.dynamic_index_in_dim(halves, ic, 0, keepdims=False).reshape(N_CHIPS * rows_h, COMM_COLS)
    give = lax.dynamic_index_in_dim(halves, 1 - ic, 0, keepdims=False).reshape(N_CHIPS * rows_h, COMM_COLS)
    pair = _sum_rows("sum_pair", [keep, swap_sibling("swap_halves", give)]).reshape(N_CHIPS, rows_h, COMM_COLS)
    got3 = scatter_chips("scatter_pieces", pair)
    own = lax.dynamic_index_in_dim(pair, chip, 0, keepdims=False)
    half = _sum_rows("sum_chips", [own, got3[0], got3[1], got3[2]])
    other = swap_sibling("swap_reduced", half)
    lo = jnp.where(ic == 0, half, other)
    hi = jnp.where(ic == 0, other, half)
    for n, g in zip(BIG, _unpack(jnp.concatenate([lo, hi]).reshape(-1), big_shapes)):
        grads[n] = g

    delta, new_m, new_v = {}, {}, {}
    large = BIG + ("w_ada",)
    for n in large:
        shape = given[n].shape
        flat = [a.reshape(-1, shape[-1]) for a in (given[n], grads[n], given["m_" + n], given["v_" + n])]
        res = adamw("adamw_" + n, *flat)
        delta[n], new_m[n], new_v[n] = (a.reshape(shape) for a in res)
    rest = [n for n in WEIGHTS if n not in large]
    rest_shapes = [given[n].shape for n in rest]
    packed = [_pack([src[pre + n] for n in rest], SUBLANES, cols=LANES)
              for src, pre in ((given, ""), (grads, ""), (given, "m_"), (given, "v_"))]
    for res, o in zip((delta, new_m, new_v), adamw("adamw_small", *packed)):
        for n, a in zip(rest, _unpack(o.reshape(-1), rest_shapes)):
            res[n] = a
    return (loss, grad_x[None], *[grads[n] for n in WEIGHTS], *[delta[n] for n in WEIGHTS],
            *[new_m[n] for n in WEIGHTS], *[new_v[n] for n in WEIGHTS])
```

```python
import functools
import math

import numpy as np
import jax
import jax.numpy as jnp
from jax import lax
from jax.experimental import pallas as pl
from jax.experimental.pallas import tpu as pltpu

F32 = jnp.float32
MXU_DTYPE = jnp.bfloat16
HIGHEST = lax.Precision.HIGHEST

DN_HEADS = 4
HEAD_DIM = 128
DN_WIDTH = DN_HEADS * HEAD_DIM
DN_CHUNK = 64
GRID_W = 64
EPS = 1e-6
POOL_WINDOWS = (2, 4, 8, 16)
N_DIRHEAD = 2 * DN_HEADS
ADAM_LR, ADAM_B1, ADAM_B2, ADAM_EPS, ADAM_WD, ADAM_STEP = 0.001, 0.9, 0.999, 1e-08, 0.01, 10

LANES = 128
SUBLANES = 8
ROW_TILE = 256
VMEM_LIMIT = 56 * 1024 * 1024

MESH_ID = pl.DeviceIdType.MESH
NN, NT, TN = ((1,), (0,)), ((1,), (1,)), ((0,), (0,))


def _params(sem=None):
    return pltpu.CompilerParams(dimension_semantics=sem, vmem_limit_bytes=VMEM_LIMIT)


def _pick(n, cands):
    for c in cands:
        if c <= n and n % c == 0:
            return c
    return n


def _mm(name, a, b, dims, out_dtype=F32):
    if dims == NN:
        (m, kk), (_, n) = a.shape, b.shape
    elif dims == NT:
        (m, kk), (n, _) = a.shape, b.shape
    else:
        (kk, m), (_, n) = a.shape, b.shape
    tm = _pick(m, (768, 1024, 512, 256, 128, 64, 32, 16, 8))
    tn = _pick(n, (1024, 1408, 768, 896, 512, 256, 128))
    tk = _pick(kk, (1024, 1408, 768, 512, 256, 128))
    gi, gj, gl = m // tm, n // tn, kk // tk
    if dims == NN:
        a_spec = pl.BlockSpec((tm, tk), lambda i, j, l: (i, l))
        b_spec = pl.BlockSpec((tk, tn), lambda i, j, l: (l, j))
    elif dims == NT:
        a_spec = pl.BlockSpec((tm, tk), lambda i, j, l: (i, l))
        b_spec = pl.BlockSpec((tn, tk), lambda i, j, l: (j, l))
    else:
        a_spec = pl.BlockSpec((tk, tm), lambda i, j, l: (l, i))
        b_spec = pl.BlockSpec((tk, tn), lambda i, j, l: (l, j))
    direct = gl == 1
    use_acc = (not direct) and out_dtype != F32

    def body(a_ref, b_ref, o_ref, *scratch):
        part = lax.dot_general(a_ref[...].astype(MXU_DTYPE), b_ref[...].astype(MXU_DTYPE), (dims, ((), ())),
                               preferred_element_type=F32)
        if direct:
            o_ref[...] = part.astype(out_dtype)
            return
        acc = scratch[0] if use_acc else o_ref
        l = pl.program_id(2)

        @pl.when(l == 0)
        def _():
            acc[...] = part

        @pl.when(l > 0)
        def _():
            acc[...] += part

        if use_acc:
            @pl.when(l == gl - 1)
            def _():
                o_ref[...] = acc[...].astype(out_dtype)

    return pl.pallas_call(
        body, grid=(gi, gj, gl), in_specs=[a_spec, b_spec],
        out_specs=pl.BlockSpec((tm, tn), lambda i, j, l: (i, j)),
        out_shape=jax.ShapeDtypeStruct((m, n), out_dtype),
        scratch_shapes=[pltpu.VMEM((tm, tn), F32)] if use_acc else [],
        compiler_params=_params(("parallel", "parallel", "arbitrary")), name=name,
    )(a, b)


def mm(name, a, w):
    @jax.custom_vjp
    def f(a, w):
        return _mm(name + "_f", a, w.astype(MXU_DTYPE), NN)

    def fwd(a, w):
        return f(a, w), (a, w)

    def bwd(res, dc):
        a, w = res
        da = _mm(name + "_da", dc, w.astype(MXU_DTYPE), NT, out_dtype=a.dtype)
        dw = _mm(name + "_dw", a, dc, TN)
        return da, dw

    f.defvjp(fwd, bwd)
    return f(a, w)


def _split(x, parts):
    w = x.shape[-1] // parts
    return [x[:, k * w:(k + 1) * w] for k in range(parts)]


def _cat(xs):
    return xs[0] if len(xs) == 1 else jnp.concatenate(xs, axis=-1)


def _shift_rows(x, prev_ref, next_ref, i, ctx_tiles, nt):
    tr = x.shape[0]
    rid = lax.broadcasted_iota(jnp.int32, x.shape, 0)
    first = jnp.logical_or(i == 0, i == ctx_tiles)
    last = jnp.logical_or(i == ctx_tiles - 1, i == nt - 1)
    prow = jnp.where(first, 0.0, prev_ref[SUBLANES - 1:SUBLANES, :].astype(F32))
    nrow = jnp.where(last, 0.0, next_ref[0:1, :].astype(F32))
    xm = jnp.where(rid == 0, prow, pltpu.roll(x, 1, 0))
    xp = jnp.where(rid == tr - 1, nrow, pltpu.roll(x, tr - 1, 0))
    return xm, xp


def rowwise(name, fn, rows, vecs, *, halo=(), seg=(), parts_r=None, parts_v=None, outs=(), reds=(), ctx_tiles=1):
    nr, nv = len(rows), len(vecs)
    halo = tuple(halo) or (False,) * nr
    seg = tuple(seg) or (False,) * nv
    parts_r = tuple(parts_r or (1,) * nr)
    parts_v = tuple(parts_v or (1,) * nv)
    r_total = rows[0].shape[0]
    tr = ROW_TILE
    nt = r_total // tr
    assert r_total % tr == 0 and (ctx_tiles > 0 or not any(seg))

    def row_specs():
        sp = []
        for i, r in enumerate(rows):
            w = r.shape[1]
            sp.append(pl.BlockSpec((tr, w), lambda t: (t, 0)))
            if halo[i]:
                k = tr // SUBLANES
                sp.append(pl.BlockSpec((SUBLANES, w), lambda t: (jnp.maximum(t * k - 1, 0), 0)))
                sp.append(pl.BlockSpec((SUBLANES, w), lambda t: (jnp.minimum((t + 1) * k, nt * k - 1), 0)))
        return sp

    def vec_spec(j):
        w = vecs[j].shape[-1]
        if seg[j]:
            return pl.BlockSpec((None, 1, w), lambda t: (jnp.where(t >= ctx_tiles, 1, 0), 0, 0))
        return pl.BlockSpec((1, w), lambda t: (0, 0))

    def row_args(rv):
        a = []
        for i in range(nr):
            a += [rv[i]] * 3 if halo[i] else [rv[i]]
        return a

    def load(refs, t):
        pos, rp = 0, []
        for i in range(nr):
            x = refs[pos][...].astype(F32)
            if halo[i]:
                xm, xp = _shift_rows(x, refs[pos + 1], refs[pos + 2], t, ctx_tiles, nt)
                rp.append(list(zip(_split(x, parts_r[i]), _split(xm, parts_r[i]), _split(xp, parts_r[i]))))
                pos += 3
            else:
                rp.append(_split(x, parts_r[i]))
                pos += 1
        vp = []
        for j in range(nv):
            vp.append(_split(refs[pos][...].astype(F32), parts_v[j]))
            pos += 1
        return rp, vp, refs[pos:]

    n_out, n_red = len(outs), len(reds)

    def fwd_call(*rv):
        def body(*refs):
            t = pl.program_id(0)
            rp, vp, rest = load(refs, t)
            o_parts, r_parts = fn(rp, vp)
            for k in range(n_out):
                rest[k][...] = _cat(o_parts[k]).astype(outs[k][1])
            for k in range(n_red):
                ref = rest[n_out + k]

                @pl.when(t == 0)
                def _():
                    ref[...] = r_parts[k]

                @pl.when(t > 0)
                def _():
                    ref[...] += r_parts[k]

        res = pl.pallas_call(
            body, grid=(nt,),
            in_specs=row_specs() + [vec_spec(j) for j in range(nv)],
            out_specs=[pl.BlockSpec((tr, o[0]), lambda t: (t, 0)) for o in outs]
            + [pl.BlockSpec((1, w), lambda t: (0, 0)) for w in reds],
            out_shape=[jax.ShapeDtypeStruct((r_total, o[0]), o[1]) for o in outs]
            + [jax.ShapeDtypeStruct((1, w), F32) for w in reds],
            compiler_params=_params(("arbitrary",)), name=name + "_f",
        )(*row_args(rv), *rv[nr:])
        return tuple(res)

    def bwd_call(rv, cts):
        def body(*refs):
            t = pl.program_id(0)
            rp, vp, rest = load(refs, t)
            ct_o = [_split(rest[k][...].astype(F32), outs[k][2]) for k in range(n_out)]
            ct_r = [rest[n_out + k][...] for k in range(n_red)]
            rest = rest[n_out + n_red:]
            _, vjp = jax.vjp(fn, rp, vp)
            d_rp, d_vp = vjp((ct_o, ct_r))
            pos = 0
            for i in range(nr):
                if halo[i]:
                    for c in range(3):
                        rest[pos + c][...] = _cat([p[c] for p in d_rp[i]])
                    pos += 3
                else:
                    rest[pos][...] = _cat(d_rp[i]).astype(rows[i].dtype)
                    pos += 1
            for j in range(nv):
                ref, val = rest[pos + j], _cat(d_vp[j])
                start = jnp.logical_or(t == 0, t == ctx_tiles) if seg[j] else t == 0

                @pl.when(start)
                def _():
                    ref[...] = val

                @pl.when(jnp.logical_not(start))
                def _():
                    ref[...] += val

        d_row_specs, d_row_shapes = [], []
        for i, r in enumerate(rows):
            w = r.shape[1]
            for _ in range(3 if halo[i] else 1):
                d_row_specs.append(pl.BlockSpec((tr, w), lambda t: (t, 0)))
                d_row_shapes.append(jax.ShapeDtypeStruct(r.shape, F32 if halo[i] else r.dtype))
        res = pl.pallas_call(
            body, grid=(nt,),
            in_specs=row_specs() + [vec_spec(j) for j in range(nv)]
            + [pl.BlockSpec((tr, o[0]), lambda t: (t, 0)) for o in outs]
            + [pl.BlockSpec((1, w), lambda t: (0, 0)) for w in reds],
            out_specs=d_row_specs + [vec_spec(j) for j in range(nv)],
            out_shape=d_row_shapes + [jax.ShapeDtypeStruct(v.shape, F32) for v in vecs],
            compiler_params=_params(("arbitrary",)), name=name + "_b",
        )(*row_args(rv), *rv[nr:], *cts)
        d_rows, pos = [], 0
        for i in range(nr):
            if halo[i]:
                d_rows.append(_unshift(res[pos], res[pos + 1], res[pos + 2], ctx_tiles * tr).astype(rows[i].dtype))
                pos += 3
            else:
                d_rows.append(res[pos])
                pos += 1
        return tuple(d_rows) + tuple(res[pos:])

    @jax.custom_vjp
    def f(*rv):
        return fwd_call(*rv)

    f.defvjp(lambda *rv: (fwd_call(*rv), rv), lambda rv, cts: bwd_call(rv, cts))
    return f(*rows, *vecs)


def _unshift(d, dm, dp, ctx_rows):
    r = d.shape[0]
    t = lax.broadcasted_iota(jnp.int32, (r, 1), 0)
    zero = jnp.zeros((1, d.shape[1]), d.dtype)
    from_m = jnp.concatenate([dm[1:], zero], axis=0)
    from_p = jnp.concatenate([zero, dp[:-1]], axis=0)
    from_m = jnp.where(t == ctx_rows - 1, 0.0, from_m)
    from_p = jnp.where(t == ctx_rows, 0.0, from_p)
    return d + from_m + from_p


def _sigmoid(x):
    return 0.5 * (jnp.tanh(0.5 * x) + 1.0)


def _silu(x):
    return x * _sigmoid(x)


def _softplus(x):
    return jnp.maximum(x, 0.0) + jnp.log(1.0 + jnp.exp(-jnp.abs(x)))


def _rms(x, g):
    return x * lax.rsqrt(jnp.mean(x * x, axis=-1, keepdims=True) + EPS) * g


def _fn_modulate(r, v):
    (x,), (g,), (sh,), (sc,) = r[0], v[0], v[1], v[2]
    return [[_rms(x, g) * (1.0 + sc) + sh]], []


def _fn_resmod(r, v):
    (x,), (y,) = r
    (gate,), (g,), (sh,), (sc,) = v
    xn = x + gate * y
    return [[xn], [_rms(xn, g) * (1.0 + sc) + sh]], []


def _fn_final(r, v):
    (x,), (y,), (tgt,) = r
    (gate,), (g,) = v
    err = _rms(x + gate * y, g) - tgt
    row_loss = jnp.mean(err * err, axis=-1, keepdims=True)
    total = 0.5 * jnp.sum(row_loss, axis=0, keepdims=True)
    return [], [jnp.broadcast_to(total, (1, LANES))]


def _fn_dnprep(r, v):
    qkv, (ab,) = r
    w0, w1, w2, (alog,), (dtb,) = v
    out = [[], [], []]
    for n, (x, xm, xp) in enumerate(qkv):
        which = n // DN_HEADS
        y = _silu(xm * w0[n] + x * w1[n] + xp * w2[n])
        if which < 2:
            y = y * lax.rsqrt(jnp.sum(y * y, axis=-1, keepdims=True) + EPS)
        if which == 0:
            y = y * (HEAD_DIM ** -0.5)
        out[which].append(y)
    lane = lax.broadcasted_iota(jnp.int32, ab.shape, 1)
    g = -jnp.exp(alog) * _softplus(ab + dtb)
    gb = jnp.where(lane < N_DIRHEAD, g, jnp.where(lane < 2 * N_DIRHEAD, _sigmoid(ab), 0.0))
    return out + [[gb]], []


def _fn_dnpost(r, v):
    of, ob, z = r
    (g,) = v[0]
    return [[_rms(a + b, g) * _silu(c) for a, b, c in zip(of, ob, z)]], []


def _fn_sub(r, v):
    return [[r[0][0] - r[1][0]]], []


def _fn_scale(r, v):
    return [[r[0][0] * v[0][0]]], []


def _fn_shortconv(r, v):
    (xin, gb, gc), = r
    (w0,), (w1,), (w2,) = v
    u, um, up = (gc[k] * xin[k] for k in range(3))
    return [[gb[0] * (um * w0 + u * w1 + up * w2)]], []


def _fn_merge(r, v):
    gates, (ya,), (yb,), (yc,) = r
    return [[_sigmoid(gates[0]) * ya + _sigmoid(gates[1]) * yb + _sigmoid(gates[2]) * yc]], []


def _fn_swiglu(r, v):
    return [[_silu(r[0][0]) * r[1][0]]], []


def _dot_hi(a, b):
    return lax.dot_general(a, b, (NN, ((), ())), precision=HIGHEST, preferred_element_type=F32)


def _bdot(a, b, dims):
    (ca,), (cb,) = dims
    return lax.dot_general(a.astype(MXU_DTYPE), b.astype(MXU_DTYPE), (((ca + 1,), (cb + 1,)), ((0,), (0,))),
                           preferred_element_type=F32)


def _hi_lo(a):
    hi = a.astype(MXU_DTYPE)
    return hi, (a - hi.astype(F32)).astype(MXU_DTYPE)


def _bdot3_raw(a, b, dims):
    (ah, al), (bh, bl) = _hi_lo(a), _hi_lo(b)
    return _bdot(ah, bh, dims) + (_bdot(ah, bl, dims) + _bdot(al, bh, dims))


@jax.custom_vjp
def _bdot3(a, b):
    return _bdot3_raw(a, b, NN)


_bdot3.defvjp(lambda a, b: (_bdot3_raw(a, b, NN), (a, b)),
              lambda res, ct: (_bdot3_raw(ct, res[1], NT), _bdot3_raw(res[0], ct, TN)))


def _inv_doubling(a):
    c = a.shape[-1]
    ii, jj = (lax.broadcasted_iota(jnp.int32, a.shape, d) for d in (1, 2))
    t = jnp.where(ii == jj, 1.0, 0.0) - a
    p = a
    for _ in range(int(math.log2(c)) - 1):
        p = _bdot3(p, p)
        t = t + _bdot3(t, p)
    return t


@jax.custom_vjp
def _inv_unit_triangular(a):
    return _inv_doubling(a)


def _inv_fwd(a):
    t = _inv_doubling(a)
    return t, t


_inv_unit_triangular.defvjp(_inv_fwd, lambda t, ct: (-_bdot3_raw(_bdot3_raw(t, ct, TN), t, NT),))


def _dn_pre(q4, k4, v4, gb):
    nb, c = N_DIRHEAD, q4.shape[1]
    q, k, v = (jnp.concatenate([a, a], axis=0) for a in (q4, k4, v4))
    lane = lax.broadcasted_iota(jnp.int32, gb.shape, 1)
    col = lambda j: jnp.sum(jnp.where(lane == j, gb, 0.0), axis=1, keepdims=True)
    g_col = jnp.concatenate([col(j)[None] for j in range(nb)], axis=0)
    b_col = jnp.concatenate([col(nb + j)[None] for j in range(nb)], axis=0)
    bi, ii, jj = (lax.broadcasted_iota(jnp.int32, (nb, c, c), a) for a in range(3))
    ahead = jnp.where(bi >= DN_HEADS, jj - ii, ii - jj)
    incl, strict, eye = ahead >= 0, ahead > 0, ahead == 0
    g_row = jnp.sum(jnp.where(eye, g_col, 0.0), axis=1, keepdims=True)
    gc_col = jnp.sum(jnp.where(incl, g_row, 0.0), axis=2, keepdims=True)
    gc_row = jnp.sum(jnp.where(ahead <= 0, g_col, 0.0), axis=1, keepdims=True)
    decay = jnp.where(incl, jnp.exp(jnp.where(incl, gc_col - gc_row, 0.0)), 0.0)
    kb = k * b_col
    a = _bdot(kb, k, NT) * jnp.where(strict, decay, 0.0)
    t = _inv_unit_triangular(a)
    e_gc = jnp.exp(gc_col)
    u = _bdot3(t, v * b_col)
    w = _bdot3(t, kb * e_gc)
    g_last = jnp.sum(g_col, axis=1, keepdims=True)
    k_state = k * jnp.exp(g_last - gc_col)
    a_qk = _bdot(q, k, NT) * decay
    return u, w, q * e_gc, k_state, a_qk, jnp.broadcast_to(jnp.exp(g_last), (nb, 1, LANES))


def _dn_step(s, u, w, qd, ks, aqk, gl):
    v_new = u - _bdot(w, s, NN)
    o = _bdot(qd, s, NN) + _bdot(aqk, v_new, NN)
    return s * gl[:, :, :1] + _bdot(ks, v_new, TN), o


def _heads(x):
    return jnp.concatenate([x[None, :, h * HEAD_DIM:(h + 1) * HEAD_DIM] for h in range(DN_HEADS)], axis=0)


def _unheads(x):
    return jnp.concatenate([x[h] for h in range(x.shape[0])], axis=-1)


def _dn_rev(t, nc, n):
    return jnp.where(t < nc, nc - 1 - t, n - 1 - (t - nc))


def _pre_shapes(n):
    c, d, h = DN_CHUNK, HEAD_DIM, DN_HEADS
    shapes = [(n, h, c, d)] * 8 + [(n, h, c, c)] * 2 + [(n, h, 1, LANES)] * 2
    return shapes, [pl.BlockSpec((None,) + s[1:], lambda t: (t, 0, 0, 0)) for s in shapes]


def _dn_pre_fwd(name, q, k, v, gb):
    n = q.shape[0] // DN_CHUNK
    shapes, specs = _pre_shapes(n)

    def body(q_ref, k_ref, v_ref, g_ref, *o_refs):
        res = _dn_pre(_heads(q_ref[...]), _heads(k_ref[...]), _heads(v_ref[...]), g_ref[...])
        for i, r in enumerate(res):
            o_refs[2 * i][...] = r[:DN_HEADS]
            o_refs[2 * i + 1][...] = r[DN_HEADS:]

    wide = pl.BlockSpec((DN_CHUNK, DN_HEADS * HEAD_DIM), lambda t: (t, 0))
    narrow = pl.BlockSpec((DN_CHUNK, LANES), lambda t: (t, 0))
    return pl.pallas_call(body, grid=(n,), in_specs=[wide] * 3 + [narrow], out_specs=specs,
                          out_shape=[jax.ShapeDtypeStruct(s, F32) for s in shapes],
                          compiler_params=_params(("parallel",)), name=name + "_pre_f")(q, k, v, gb)


def _dn_pre_bwd(name, q, k, v, gb, cts):
    n = q.shape[0] // DN_CHUNK
    _, specs = _pre_shapes(n)
    n_ct = len(specs)

    def body(q_ref, k_ref, v_ref, g_ref, *refs):
        ct = tuple(jnp.concatenate([refs[i][...], refs[i + 1][...]], axis=0) for i in range(0, n_ct, 2))
        _, vjp = jax.vjp(_dn_pre, _heads(q_ref[...]), _heads(k_ref[...]), _heads(v_ref[...]), g_ref[...])
        dq, dk, dv, dg = vjp(ct)
        for r, val in zip(refs[n_ct:], (_unheads(dq), _unheads(dk), _unheads(dv), dg)):
            r[...] = val

    wide = pl.BlockSpec((DN_CHUNK, DN_HEADS * HEAD_DIM), lambda t: (t, 0))
    narrow = pl.BlockSpec((DN_CHUNK, LANES), lambda t: (t, 0))
    return pl.pallas_call(body, grid=(n,), in_specs=[wide] * 3 + [narrow] + specs, out_specs=[wide] * 3 + [narrow],
                          out_shape=[jax.ShapeDtypeStruct(q.shape, F32)] * 3 + [jax.ShapeDtypeStruct(gb.shape, F32)],
                          compiler_params=_params(("parallel",)), name=name + "_pre_b")(q, k, v, gb, *cts)


def _scan_specs(pre_shapes, fw, bw):
    maps = (lambda t: (fw(t), 0, 0, 0), lambda t: (bw(t), 0, 0, 0))
    return [pl.BlockSpec((None,) + s[1:], maps[i % 2]) for i, s in enumerate(pre_shapes)]


def _dn_scan_fwd(name, pre, nc):
    n = pre[0].shape[0]
    r = n * DN_CHUNK
    hd, nh, nb = HEAD_DIM, DN_HEADS, N_DIRHEAD
    shapes, _ = _pre_shapes(n)
    n_in = len(shapes)

    def body(*refs):
        ins, (of_ref, ob_ref, sf_ref, sb_ref, s_scr) = refs[:n_in], refs[n_in:]
        t = pl.program_id(0)

        @pl.when(t == 0)
        def _():
            s_scr[...] = jnp.zeros_like(s_scr)

        s = s_scr[...]
        sf_ref[...] = s[:nh]
        sb_ref[...] = s[nh:]
        args = [jnp.concatenate([ins[i][...], ins[i + 1][...]], axis=0) for i in range(0, n_in, 2)]
        s2, o = _dn_step(s, *args)
        s_scr[...] = s2
        of_ref[...] = _unheads(o[:nh])
        ob_ref[...] = _unheads(o[nh:])

    fw = lambda t: t
    bw = lambda t: _dn_rev(t, nc, n)
    wide, st = (DN_CHUNK, nh * hd), (None, nh, hd, hd)
    return pl.pallas_call(
        body, grid=(n,), in_specs=_scan_specs(shapes, fw, bw),
        out_specs=[pl.BlockSpec(wide, lambda t: (fw(t), 0)), pl.BlockSpec(wide, lambda t: (bw(t), 0)),
                   pl.BlockSpec(st, lambda t: (fw(t), 0, 0, 0)), pl.BlockSpec(st, lambda t: (bw(t), 0, 0, 0))],
        out_shape=[jax.ShapeDtypeStruct((r, nh * hd), F32)] * 2 + [jax.ShapeDtypeStruct((n, nh, hd, hd), F32)] * 2,
        scratch_shapes=[pltpu.VMEM((nb, hd, hd), F32)],
        compiler_params=_params(("arbitrary",)), name=name + "_scan_f",
    )(*pre)


def _dn_scan_bwd(name, pre, sall_f, sall_b, do_f, do_b, nc):
    n = pre[0].shape[0]
    hd, nh, nb = HEAD_DIM, DN_HEADS, N_DIRHEAD
    shapes, _ = _pre_shapes(n)
    n_in = len(shapes)

    def body(*refs):
        ins, (sf_ref, sb_ref, dof_ref, dob_ref) = refs[:n_in], refs[n_in:n_in + 4]
        outs, ds_scr = refs[n_in + 4:2 * n_in + 4], refs[2 * n_in + 4]
        t = pl.program_id(0)

        @pl.when(t == 0)
        def _():
            ds_scr[...] = jnp.zeros_like(ds_scr)

        args = [jnp.concatenate([ins[i][...], ins[i + 1][...]], axis=0) for i in range(0, n_in, 2)]
        s = jnp.concatenate([sf_ref[...], sb_ref[...]], axis=0)
        do = jnp.concatenate([_heads(dof_ref[...]), _heads(dob_ref[...])], axis=0)
        _, vjp = jax.vjp(_dn_step, s, *args)
        cts = vjp((ds_scr[...], do))
        ds_scr[...] = cts[0]
        for i, ct in enumerate(cts[1:]):
            outs[2 * i][...] = ct[:nh]
            outs[2 * i + 1][...] = ct[nh:]

    fw = lambda t: n - 1 - t
    bw = lambda t: _dn_rev(n - 1 - t, nc, n)
    wide, st = (DN_CHUNK, nh * hd), (None, nh, hd, hd)
    return tuple(pl.pallas_call(
        body, grid=(n,),
        in_specs=_scan_specs(shapes, fw, bw)
        + [pl.BlockSpec(st, lambda t: (fw(t), 0, 0, 0)), pl.BlockSpec(st, lambda t: (bw(t), 0, 0, 0)),
           pl.BlockSpec(wide, lambda t: (fw(t), 0)), pl.BlockSpec(wide, lambda t: (bw(t), 0))],
        out_specs=_scan_specs(shapes, fw, bw), out_shape=[jax.ShapeDtypeStruct(s, F32) for s in shapes],
        scratch_shapes=[pltpu.VMEM((nb, hd, hd), F32)],
        compiler_params=_params(("arbitrary",)), name=name + "_scan_b",
    )(*pre, sall_f, sall_b, do_f, do_b))


def deltanet(name, q, k, v, gb, ctx_rows):
    nc = ctx_rows // DN_CHUNK

    @jax.custom_vjp
    def pre(q, k, v, gb):
        return tuple(_dn_pre_fwd(name, q, k, v, gb))

    pre.defvjp(lambda *a: (tuple(_dn_pre_fwd(name, *a)), a), lambda a, cts: tuple(_dn_pre_bwd(name, *a, cts)))

    @jax.custom_vjp
    def scan(*ops):
        return tuple(_dn_scan_fwd(name, ops, nc)[:2])

    def scan_fwd(*ops):
        of, ob, sf, sb = _dn_scan_fwd(name, ops, nc)
        return (of, ob), (ops, sf, sb)

    scan.defvjp(scan_fwd, lambda res, cts: _dn_scan_bwd(name, res[0], res[1], res[2], cts[0], cts[1], nc))
    return scan(*pre(q, k, v, gb))


def _box_matrix(l, w):
    lo, hi = w // 2, w - 1 - w // 2
    pos = np.arange(l)
    start, end = np.clip(pos - lo, 0, l), np.clip(pos + hi + 1, 0, l)
    col = np.arange(l)[None, :]
    return ((col >= start[:, None]) & (col < end[:, None])) / (end - start)[:, None].astype(np.float64)


def _pool_matrices(ctx_rows, grid_rows):
    assert ctx_rows == ROW_TILE and ROW_TILE % GRID_W == 0
    ctx = np.stack([_box_matrix(ctx_rows, w) for w in POOL_WINDOWS])
    cols = np.stack([np.kron(np.eye(ROW_TILE // GRID_W), _box_matrix(GRID_W, w)) for w in POOL_WINDOWS])
    rows = np.stack([_box_matrix(grid_rows, w) for w in POOL_WINDOWS])[None]
    return np.stack([ctx, cols]).astype(np.float32), rows.astype(np.float32)


def _pool_apply(name, x, mats, group_w, seg_tiles, lane_tile):
    r, w = x.shape
    b = mats.shape[-1]
    ng = mats.shape[1]
    period = ng * group_w

    def body(x_ref, m_ref, o_ref):
        xv = x_ref[...]
        lane = lax.broadcasted_iota(jnp.int32, xv.shape, 1)
        grp = (lane % period) // group_w
        acc = jnp.zeros_like(xv)
        for g in range(ng):
            acc = acc + jnp.where(grp == g, _dot_hi(m_ref[g], xv), 0.0)
        o_ref[...] = acc

    return pl.pallas_call(
        body, grid=(r // b, w // lane_tile),
        in_specs=[pl.BlockSpec((b, lane_tile), lambda i, j: (i, j)),
                  pl.BlockSpec((None, ng, b, b), lambda i, j: (jnp.where(i >= seg_tiles, 1, 0) if mats.shape[0] > 1 else 0, 0, 0, 0))],
        out_specs=pl.BlockSpec((b, lane_tile), lambda i, j: (i, j)),
        out_shape=jax.ShapeDtypeStruct(x.shape, F32),
        compiler_params=_params(("parallel", "parallel")), name=name,
    )(x, mats)


def pool_means(name, u, ctx_rows):
    r, pw = u.shape
    grid_rows = (r - ctx_rows) // GRID_W
    gw = pw // len(POOL_WINDOWS)
    m1, m2 = _pool_matrices(ctx_rows, grid_rows)
    lane_tile = min(2048, GRID_W * pw)

    def apply(x, a1, a2, tag):
        y = _pool_apply(name + tag + "1", x, jnp.asarray(a1), gw, ctx_rows // ROW_TILE, pw)
        lat = y[ctx_rows:].reshape(grid_rows, GRID_W * pw)
        lat = _pool_apply(name + tag + "2", lat, jnp.asarray(a2), gw, 0, lane_tile)
        return jnp.concatenate([y[:ctx_rows], lat.reshape(r - ctx_rows, pw)], axis=0)

    @jax.custom_vjp
    def f(u):
        return apply(u, m1, m2, "_f")

    tr = lambda m: np.ascontiguousarray(np.swapaxes(m, -1, -2))
    f.defvjp(lambda u: (apply(u, m1, m2, "_f"), None), lambda _, ct: (apply(ct, tr(m1), tr(m2), "_b"),))
    return f(u)


def _ew(name, fn, *xs, n_out=1):
    shapes = jax.eval_shape(lambda *a: fn(*a), *xs)
    shapes = shapes if isinstance(shapes, (tuple, list)) else (shapes,)

    def body(*refs):
        res = fn(*[r[...] for r in refs[:len(xs)]])
        res = res if isinstance(res, (tuple, list)) else (res,)
        for r, o in zip(res, refs[len(xs):]):
            o[...] = r

    out = pl.pallas_call(body, out_shape=[jax.ShapeDtypeStruct(s.shape, s.dtype) for s in shapes],
                         compiler_params=_params(), name=name)(*xs)
    return out[0] if len(shapes) == 1 else tuple(out)


def _adamw_math(w, g, m, v):
    m2 = ADAM_B1 * m + (1.0 - ADAM_B1) * g
    v2 = ADAM_B2 * v + (1.0 - ADAM_B2) * (g * g)
    m_hat = m2 / (1.0 - ADAM_B1 ** ADAM_STEP)
    v_hat = v2 / (1.0 - ADAM_B2 ** ADAM_STEP)
    delta = -ADAM_LR * (m_hat / (jnp.sqrt(v_hat) + ADAM_EPS) + ADAM_WD * w)
    return delta, m2, v2


def adamw(name, w, g, m, v):
    r, c = w.shape
    tr = _pick(r, (256, 128, 64, 32, 16, 8))

    def body(w_ref, g_ref, m_ref, v_ref, d_ref, m2_ref, v2_ref):
        d_ref[...], m2_ref[...], v2_ref[...] = _adamw_math(w_ref[...], g_ref[...], m_ref[...], v_ref[...])

    spec = pl.BlockSpec((tr, c), lambda i: (i, 0))
    return pl.pallas_call(body, grid=(r // tr,), in_specs=[spec] * 4, out_specs=[spec] * 3,
                          out_shape=[jax.ShapeDtypeStruct(w.shape, F32)] * 3,
                          compiler_params=_params(("parallel",)), name=name)(w, g, m, v)


def _sum_rows(name, xs, out_dtype=F32):
    r, c = xs[0].shape
    tr = _pick(r, (512, 256, 128, 64, 32, 16, 8))

    def body(*refs):
        acc = refs[0][...].astype(F32)
        for ref in refs[1:-1]:
            acc = acc + ref[...].astype(F32)
        refs[-1][...] = acc.astype(out_dtype)

    spec = pl.BlockSpec((tr, c), lambda i: (i, 0))
    return pl.pallas_call(body, grid=(r // tr,), in_specs=[spec] * len(xs), out_specs=spec,
                          out_shape=jax.ShapeDtypeStruct((r, c), out_dtype),
                          compiler_params=_params(("parallel",)), name=name)(*xs)


def _place():
    return lax.axis_index("x"), lax.axis_index("y"), lax.axis_index("c")


def _chip_peers(x, y, c):
    return [(1 - x, y, c), (x, 1 - y, c), (1 - x, 1 - y, c)]


def all_gather8(name, block):
    m_per, n = block.shape

    def body(x_ref, out_ref, send_sems, recv_sems, local_sem):
        x, y, c = _place()
        me, sibling = (x, y, c), (x, y, 1 - c)
        chips = [(1 - x, y), (x, 1 - y), (1 - x, 1 - y)]

        def rows(px, py, pc):
            return out_ref.at[pl.ds((4 * px + 2 * py + pc) * m_per, m_per), :]

        def copy(k, blk, to, src=None):
            return pltpu.make_async_remote_copy(
                src_ref=rows(*blk) if src is None else src, dst_ref=rows(*blk),
                send_sem=send_sems.at[k], recv_sem=recv_sems.at[k], device_id=to, device_id_type=MESH_ID)

        mine = pltpu.make_async_copy(x_ref, rows(*me), local_sem)
        mine.start()
        first = [copy(0, me, sibling, src=x_ref)]
        first += [copy(1 + j, me, (*chip, c), src=x_ref) for j, chip in enumerate(chips)]
        for cp in first:
            cp.start()
        passed = [copy(4 + j, (*chip, c), sibling) for j, chip in enumerate(chips)]
        for j, chip in enumerate(chips):
            copy(1 + j, (*chip, c), me).wait_recv()
            passed[j].start()
        copy(0, sibling, me).wait_recv()
        for j, chip in enumerate(chips):
            copy(4 + j, (*chip, 1 - c), me).wait_recv()
        for cp in first + passed:
            cp.wait_send()
        mine.wait()

    return pl.pallas_call(
        body, out_shape=jax.ShapeDtypeStruct((8 * m_per, n), block.dtype),
        in_specs=[pl.BlockSpec(memory_space=pltpu.VMEM)], out_specs=pl.BlockSpec(memory_space=pltpu.VMEM),
        scratch_shapes=[pltpu.SemaphoreType.DMA((7,)), pltpu.SemaphoreType.DMA((7,)), pltpu.SemaphoreType.DMA],
        compiler_params=_params(), name=name,
    )(block)


def gather_chips(name, shard):
    def body(x_ref, out_ref, send_sems, recv_sems, local_sem):
        x, y, c = _place()
        mine = pltpu.make_async_copy(x_ref, out_ref.at[2 * x + y], local_sem)
        mine.start()
        copies = []
        for p, peer in enumerate(_chip_peers(x, y, c)):
            cp = pltpu.make_async_remote_copy(
                src_ref=x_ref, dst_ref=out_ref.at[2 * x + y], send_sem=send_sems.at[p], recv_sem=recv_sems.at[p],
                device_id=peer, device_id_type=MESH_ID)
            cp.start()
            copies.append(cp)
        for p, (px, py, _) in enumerate(_chip_peers(x, y, c)):
            pltpu.make_async_remote_copy(
                src_ref=x_ref, dst_ref=out_ref.at[2 * px + py], send_sem=send_sems.at[p], recv_sem=recv_sems.at[p],
                device_id=(px, py, c), device_id_type=MESH_ID).wait_recv()
        for cp in copies:
            cp.wait_send()
        mine.wait()

    hbm = pl.BlockSpec(memory_space=pltpu.HBM)
    return pl.pallas_call(
        body, out_shape=jax.ShapeDtypeStruct((4,) + shard.shape, shard.dtype), in_specs=[hbm], out_specs=hbm,
        scratch_shapes=[pltpu.SemaphoreType.DMA((3,)), pltpu.SemaphoreType.DMA((3,)), pltpu.SemaphoreType.DMA],
        compiler_params=_params(), name=name,
    )(shard)


def swap_sibling(name, block):
    def body(x_ref, out_ref, send_sem, recv_sem):
        x, y, c = _place()
        cp = pltpu.make_async_remote_copy(src_ref=x_ref, dst_ref=out_ref, send_sem=send_sem, recv_sem=recv_sem,
                                          device_id=(x, y, 1 - c), device_id_type=MESH_ID)
        cp.start()
        cp.wait()

    hbm = pl.BlockSpec(memory_space=pltpu.HBM)
    return pl.pallas_call(
        body, out_shape=jax.ShapeDtypeStruct(block.shape, block.dtype), in_specs=[hbm], out_specs=hbm,
        scratch_shapes=[pltpu.SemaphoreType.DMA, pltpu.SemaphoreType.DMA],
        compiler_params=_params(), name=name,
    )(block)


def scatter_chips(name, pieces):
    def body(x_ref, out_ref, send_sems, recv_sems):
        x, y, c = _place()
        copies = []
        for p, (px, py, pc) in enumerate(_chip_peers(x, y, c)):
            cp = pltpu.make_async_remote_copy(
                src_ref=x_ref.at[2 * px + py], dst_ref=out_ref.at[p], send_sem=send_sems.at[p], recv_sem=recv_sems.at[p],
                device_id=(px, py, pc), device_id_type=MESH_ID)
            cp.start()
            copies.append(cp)
        for cp in copies:
            cp.wait()

    hbm = pl.BlockSpec(memory_space=pltpu.HBM)
    return pl.pallas_call(
        body, out_shape=jax.ShapeDtypeStruct((3,) + pieces.shape[1:], pieces.dtype), in_specs=[hbm], out_specs=hbm,
        scratch_shapes=[pltpu.SemaphoreType.DMA((3,)), pltpu.SemaphoreType.DMA((3,))],
        compiler_params=_params(), name=name,
    )(pieces)


def _sum_devices(name, got, fold=False):
    def body(a_ref, *o_refs):
        acc = a_ref[0]
        for i in range(1, N_DEV):
            acc = acc + a_ref[i]
        o_refs[0][...] = acc
        if fold:
            o_refs[1][...] = acc + pltpu.roll(acc, SUBLANES // 2, 0)

    shape = jax.ShapeDtypeStruct(got.shape[1:], F32)
    out = pl.pallas_call(body, out_shape=[shape] * (2 if fold else 1), compiler_params=_params(), name=name)(got)
    return out if fold else out[0]


def split_cols(w, bounds):
    edges = list(zip(bounds[:-1], bounds[1:]))

    def cut(w):
        return tuple(w[:, a:b] for a, b in edges)

    f = jax.custom_vjp(cut)
    f.defvjp(lambda w: (cut(w), None), lambda _, cts: (jnp.concatenate(cts, axis=1),))
    return f(w)


def _row(v):
    return v.reshape(1, -1)


def _pad_lanes(v, width=LANES):
    return jnp.pad(v, ((0, 0), (0, width - v.shape[1])))


def _block_diag(blocks):
    g, n, _ = blocks.shape
    out = jnp.zeros((g * n, g * n), blocks.dtype)
    for i in range(g):
        out = out.at[i * n:(i + 1) * n, i * n:(i + 1) * n].set(blocks[i])
    return out


def local_loss(p, x, mod_lat, mod_ctx, ctx, target):
    ctx_rows, d = ctx.shape
    depth = p["w_in"].shape[0]
    pw, scw = p["pool_scale"].shape[1], p["sc_conv_w"].shape[2]
    ff = p["w_down"].shape[1]
    ct = ctx_rows // ROW_TILE
    off_z = 3 * DN_WIDTH
    off_a = off_z + DN_WIDTH
    off_pool = off_a + 2 * N_DIRHEAD
    off_sc = off_pool + pw
    off_gate = off_sc + 3 * scw
    n_in = off_gate + 3 * d
    xs = jnp.concatenate([ctx, x], axis=0)
    seg = lambda l, k: jnp.stack([mod_ctx[l, k], mod_lat[l, k]]).reshape(2, 1, d)
    dn = gate2 = None
    for l in range(depth):
        sh1, sc1, g1, sh2, sc2, g2 = (seg(l, k) for k in range(6))
        tag = f"l{l}_"
        if l == 0:
            (h1,) = rowwise(tag + "mod", _fn_modulate, [xs], [_row(p["norm1_g"][l]), sh1, sc1], seg=(False, True, True),
                            outs=[(d, F32, 1)], ctx_tiles=ct)
        else:
            xs, h1 = rowwise(tag + "resmod1", _fn_resmod, [xs, dn], [gate2, _row(p["norm1_g"][l]), sh1, sc1],
                             seg=(True, False, True, True), outs=[(d, F32, 1), (d, F32, 1)], ctx_tiles=ct)
        w_qkv, w_z, w_ab, w_pool, w_sc, w_gate = split_cols(p["w_in"][l], (0, off_z, off_a, off_pool, off_sc, off_gate, n_in))
        p_qkv = mm(tag + "qkv", h1, w_qkv)
        p_z = mm(tag + "z", h1, w_z)
        p_ab = mm(tag + "ab", h1, _pad_lanes(w_ab))
        p_pool = mm(tag + "pool", h1, w_pool)
        p_sc = mm(tag + "sc", h1, w_sc)
        p_gate = mm(tag + "gate", h1, w_gate)
        cw = p["dn_conv_w"][l]
        q, k, v, gb = rowwise(
            tag + "dnprep", _fn_dnprep, [p_qkv, p_ab],
            [cw[0:1], cw[1:2], cw[2:3], _pad_lanes(p["dn_a_log"][l].reshape(1, -1)), _pad_lanes(p["dn_dt_bias"][l].reshape(1, -1))],
            halo=(True, False), parts_r=(3 * DN_HEADS, 1), parts_v=(3 * DN_HEADS,) * 3 + (1, 1),
            outs=[(DN_WIDTH, F32, DN_HEADS)] * 3 + [(LANES, F32, 1)], ctx_tiles=ct)
        o_f, o_b = deltanet(tag + "dn", q, k, v, gb, ctx_rows)
        (oz,) = rowwise(tag + "dnpost", _fn_dnpost, [o_f, o_b, p_z], [_row(p["dn_norm_g"][l])], parts_r=(DN_HEADS,) * 3,
                        outs=[(DN_WIDTH, MXU_DTYPE, DN_HEADS)], ctx_tiles=ct)
        y_a = mm(tag + "bra", oz, p["w_br_a"][l])
        means = pool_means(tag + "box", p_pool, ctx_rows)
        (dpool,) = rowwise(tag + "poolsub", _fn_sub, [means, p_pool], [], outs=[(pw, MXU_DTYPE, 1)], ctx_tiles=ct)
        mixed = mm(tag + "poolw", dpool, _block_diag(p["pool_w"][l]))
        (yb_in,) = rowwise(tag + "poolscale", _fn_scale, [mixed], [_row(p["pool_scale"][l])], outs=[(pw, MXU_DTYPE, 1)], ctx_tiles=ct)
        y_b = mm(tag + "brb", yb_in, p["w_br_b"][l])
        sw = p["sc_conv_w"][l]
        (yc_in,) = rowwise(tag + "sconv", _fn_shortconv, [p_sc], [sw[0:1], sw[1:2], sw[2:3]], halo=(True,), parts_r=(3,),
                           outs=[(scw, MXU_DTYPE, 1)], ctx_tiles=ct)
        y_c = mm(tag + "brc", yc_in, p["w_br_c"][l])
        (y,) = rowwise(tag + "merge", _fn_merge, [p_gate, y_a, y_b, y_c], [], parts_r=(3, 1, 1, 1),
                       outs=[(d, MXU_DTYPE, 1)], ctx_tiles=ct)
        mix = mm(tag + "wo", y, p["w_o"][l])
        xs, h2 = rowwise(tag + "resmod2", _fn_resmod, [xs, mix], [g1, _row(p["norm2_g"][l]), sh2, sc2],
                         seg=(True, False, True, True), outs=[(d, F32, 1), (d, F32, 1)], ctx_tiles=ct)
        w_g, w_u = split_cols(p["w_gu"][l], (0, ff, 2 * ff))
        (act,) = rowwise(tag + "swiglu", _fn_swiglu, [mm(tag + "ffg", h2, w_g), mm(tag + "ffu", h2, w_u)], [],
                         outs=[(ff, MXU_DTYPE, 1)], ctx_tiles=ct)
        dn = mm(tag + "down", act, p["w_down"][l])
        gate2 = g2
    (total,) = rowwise("final", _fn_final, [xs[ctx_rows:], dn[ctx_rows:], target],
                       [gate2[1], _row(p["final_norm_g"])], reds=[LANES], ctx_tiles=0)
    return total[0, 0]


BIG = ("w_in", "w_br_a", "w_br_b", "w_br_c", "w_o", "w_gu", "w_down")
ROW_SHARDED = ("w_o", "w_down")
SMALL_REPL = ("norm1_g", "norm2_g", "dn_a_log", "dn_dt_bias", "dn_norm_g", "pool_w", "pool_scale", "final_norm_g")
SMALL_SHARD = ("dn_conv_w", "sc_conv_w")
WEIGHTS = ("c_ctx", "w_ada", "b_ada", "norm1_g", "norm2_g", "w_in", "dn_conv_w", "dn_a_log", "dn_dt_bias", "dn_norm_g", "pool_w",
           "pool_scale", "sc_conv_w", "w_br_a", "w_br_b", "w_br_c", "w_o", "w_gu", "w_down", "final_norm_g")
N_CHIPS, N_DEV = 4, 8
COMM_COLS = 1024


def _pack(arrays, rows_multiple, cols=COMM_COLS, dtype=F32):
    flat = jnp.concatenate([a.astype(dtype).reshape(-1) for a in arrays])
    rows = -(-flat.shape[0] // (cols * rows_multiple)) * rows_multiple
    return jnp.pad(flat, (0, rows * cols - flat.shape[0])).reshape(rows, cols)


def _unpack(flat, shapes):
    out, pos = [], 0
    for s in shapes:
        n = int(np.prod(s))
        out.append(flat[pos:pos + n].reshape(s))
        pos += n
    return out


def _join_shards(parts, name):
    _, l, a, b = parts.shape
    if name in ROW_SHARDED:
        return jnp.moveaxis(parts, 0, 1).reshape(l, N_CHIPS * a, b)
    return jnp.moveaxis(parts, 0, 2).reshape(l, a, N_CHIPS * b)


def _cut_shard(full, name, j):
    if name in ROW_SHARDED:
        n = full.shape[1] // N_CHIPS
        return full[:, j * n:(j + 1) * n, :]
    n = full.shape[2] // N_CHIPS
    return full[:, :, j * n:(j + 1) * n]


def _dsilu(x):
    s = _sigmoid(x)
    return s + x * s * (1.0 - s)


def kernel(x, c, ctx, c_ctx, w_ada, b_ada, norm1_g, norm2_g, w_in, dn_conv_w, dn_a_log, dn_dt_bias, dn_norm_g, pool_w, pool_scale, sc_conv_w, w_br_a, w_br_b, w_br_c, w_o, w_gu, w_down, final_norm_g, loss_target, m_c_ctx, m_w_ada, m_b_ada, m_norm1_g, m_norm2_g, m_w_in, m_dn_conv_w, m_dn_a_log, m_dn_dt_bias, m_dn_norm_g, m_pool_w, m_pool_scale, m_sc_conv_w, m_w_br_a, m_w_br_b, m_w_br_c, m_w_o, m_w_gu, m_w_down, m_final_norm_g, v_c_ctx, v_w_ada, v_b_ada, v_norm1_g, v_norm2_g, v_w_in, v_dn_conv_w, v_dn_a_log, v_dn_dt_bias, v_dn_norm_g, v_pool_w, v_pool_scale, v_sc_conv_w, v_w_br_a, v_w_br_b, v_w_br_c, v_w_o, v_w_gu, v_w_down, v_final_norm_g):
    given = dict(locals())
    ix, iy, ic = _place()
    chip, dev = 2 * ix + iy, 4 * ix + 2 * iy + ic
    d = x.shape[-1]
    depth = w_in.shape[0]
    ada_cols = w_ada.shape[2]
    spare = 2 * SUBLANES - N_DEV - 1

    got0 = all_gather8("gather_cond", _pack([c, dn_conv_w, sc_conv_w], SUBLANES)).reshape(N_DEV, -1)
    c_all = got0[:, :d]
    taps = [_unpack(got0[2 * j, d:], [dn_conv_w.shape, sc_conv_w.shape]) for j in range(N_CHIPS)]
    full = {"dn_conv_w": jnp.concatenate([t[0] for t in taps], axis=-1),
            "sc_conv_w": jnp.concatenate([t[1] for t in taps], axis=-1)}
    big_shapes = [given[n].shape for n in BIG]
    mine = _pack([given[n] for n in BIG], 4 * SUBLANES, dtype=MXU_DTYPE)
    mine = mine.reshape(2, mine.shape[0] // 2, COMM_COLS)
    got_half = gather_chips("gather_weights", lax.dynamic_index_in_dim(mine, ic, 0, keepdims=False))
    other_half = swap_sibling("swap_weights", got_half)
    shards = jnp.stack([jnp.where(ic == 0, got_half, other_half), jnp.where(ic == 0, other_half, got_half)], axis=1)
    shards = shards.reshape(N_CHIPS, -1)
    per_chip = [_unpack(shards[j], big_shapes) for j in range(N_CHIPS)]
    for i, n in enumerate(BIG):
        full[n] = _join_shards(jnp.stack([per_chip[j][i] for j in range(N_CHIPS)]), n).astype(F32)
    for n in SMALL_REPL:
        full[n] = given[n]

    cond = jnp.concatenate([c_all, c_ctx[None], jnp.zeros((spare, d), F32)])
    s_cond = _ew("silu_cond", _silu, cond)
    mod_cols = jnp.concatenate([_mm(f"ada{l}_f", s_cond, w_ada[l], NN) for l in range(depth)], axis=1)
    mod_got = all_gather8("gather_mod", mod_cols).reshape(N_DEV, 2 * SUBLANES, depth, ada_cols)
    mod_all = jnp.concatenate([mod_got[2 * j] for j in range(N_CHIPS)], axis=-1) + b_ada[None]
    mod_lat = lax.dynamic_index_in_dim(mod_all, dev, 0, keepdims=False).reshape(depth, 6, d)
    mod_ctx = mod_all[N_DEV].reshape(depth, 6, d)

    loss_local, (g_full, grad_x, g_mod_lat, g_mod_ctx) = jax.value_and_grad(local_loss, argnums=(0, 1, 2, 3))(
        full, x[0], mod_lat, mod_ctx, ctx[0], loss_target[0])
    loss = lax.psum(loss_local, ("x", "y", "c"))

    dmod_cols = (2 * depth * 6 * d) // SUBLANES
    dmod = all_gather8("gather_dmod", _pack([g_mod_lat, g_mod_ctx], SUBLANES, cols=dmod_cols))
    dmod = dmod.reshape(N_DEV, SUBLANES, dmod_cols)
    dmod_sum, dmod_fold = _sum_devices("reduce_dmod", dmod, fold=True)
    half_rows = SUBLANES // 2
    grad_b_ada = dmod_fold[:half_rows].reshape(depth, 6 * d)
    dctx_sum = dmod_sum[half_rows:].reshape(1, depth, 6 * d)
    d9 = jnp.concatenate([dmod[:, :half_rows].reshape(N_DEV, depth, 6 * d), dctx_sum, jnp.zeros((spare, depth, 6 * d), F32)])
    d9 = lax.dynamic_slice_in_dim(d9, chip * ada_cols, ada_cols, axis=2)
    grad_w_ada = jnp.stack([_mm(f"ada{l}_dw", s_cond, d9[:, l], TN) for l in range(depth)])
    ds_cond = [_mm(f"ada{l}_da", d9[:, l], w_ada[l], NT) for l in range(depth)]
    dsilu_part = _sum_rows("sum_dcond", ds_cond)[N_DEV]

    small_names = SMALL_REPL + SMALL_SHARD
    small_grads = [dsilu_part] + [g_full[n] for n in small_names]
    small_shapes = [a.shape for a in small_grads]
    n_small = sum(int(np.prod(s)) for s in small_shapes)
    cols = -(-n_small // (SUBLANES * LANES)) * LANES
    got2 = all_gather8("gather_small", _pack(small_grads, SUBLANES, cols=cols)).reshape(N_DEV, SUBLANES, cols)
    small_sum = _unpack(_sum_devices("reduce_small", got2).reshape(-1), small_shapes)
    grads = dict(zip(small_names, small_sum[1:]))
    grads["c_ctx"] = _ew("dsilu", lambda g, z: 0.5 * g * _dsilu(z), _row(small_sum[0]), _row(c_ctx)).reshape(-1)
    grads["w_ada"], grads["b_ada"] = grad_w_ada, grad_b_ada
    for n in SMALL_SHARD:
        width = given[n].shape[-1]
        grads[n] = lax.dynamic_slice_in_dim(grads[n], chip * width, width, axis=-1)

    pieces = jnp.stack([_pack([_cut_shard(g_full[n], n, j) for n in BIG], 2 * LANES) for j in range(N_CHIPS)])
    rows_h = pieces.shape[1] // 2
    halves = jnp.moveaxis(pieces.reshape(N_CHIPS, 2, rows_h, COMM_COLS), 1, 0)
    keep = lax.dynamic_index_in_dim(halves, ic, 0, keepdims=False).reshape(N_CHIPS * rows_h, COMM_COLS)
    give = lax.dynamic_index_in_dim(halves, 1 - ic, 0, keepdims=False).reshape(N_CHIPS * rows_h, COMM_COLS)
    pair = _sum_rows("sum_pair", [keep, swap_sibling("swap_halves", give)], out_dtype=MXU_DTYPE)
    pair = pair.reshape(N_CHIPS, rows_h, COMM_COLS)
    got3 = scatter_chips("scatter_pieces", pair)
    own = lax.dynamic_index_in_dim(pair, chip, 0, keepdims=False)
    half = _sum_rows("sum_chips", [own, got3[0], got3[1], got3[2]])
    other = swap_sibling("swap_reduced", half)
    lo = jnp.where(ic == 0, half, other)
    hi = jnp.where(ic == 0, other, half)
    for n, g in zip(BIG, _unpack(jnp.concatenate([lo, hi]).reshape(-1), big_shapes)):
        grads[n] = g

    delta, new_m, new_v = {}, {}, {}
    large = BIG + ("w_ada",)
    for n in large:
        shape = given[n].shape
        flat = [a.reshape(-1, shape[-1]) for a in (given[n], grads[n], given["m_" + n], given["v_" + n])]
        res = adamw("adamw_" + n, *flat)
        delta[n], new_m[n], new_v[n] = (a.reshape(shape) for a in res)
    rest = [n for n in WEIGHTS if n not in large]
    rest_shapes = [given[n].shape for n in rest]
    packed = [_pack([src[pre + n] for n in rest], SUBLANES, cols=LANES)
              for src, pre in ((given, ""), (grads, ""), (given, "m_"), (given, "v_"))]
    for res, o in zip((delta, new_m, new_v), adamw("adamw_small", *packed)):
        for n, a in zip(rest, _unpack(o.reshape(-1), rest_shapes)):
            res[n] = a
    return (loss, grad_x[None], *[grads[n] for n in WEIGHTS], *[delta[n] for n in WEIGHTS],
            *[new_m[n] for n in WEIGHTS], *[new_v[n] for n in WEIGHTS])
```

```python
import functools
import math

import numpy as np
import jax
import jax.numpy as jnp
from jax import lax
from jax.experimental import pallas as pl
from jax.experimental.pallas import tpu as pltpu

F32 = jnp.float32
MXU_DTYPE = jnp.bfloat16
HIGHEST = lax.Precision.HIGHEST

DN_HEADS = 4
HEAD_DIM = 128
DN_WIDTH = DN_HEADS * HEAD_DIM
DN_CHUNK = 64
GRID_W = 64
EPS = 1e-6
POOL_WINDOWS = (2, 4, 8, 16)
N_DIRHEAD = 2 * DN_HEADS
ADAM_LR, ADAM_B1, ADAM_B2, ADAM_EPS, ADAM_WD, ADAM_STEP = 0.001, 0.9, 0.999, 1e-08, 0.01, 10

LANES = 128
SUBLANES = 8
ROW_TILE = 256
VMEM_LIMIT = 56 * 1024 * 1024

MESH_ID = pl.DeviceIdType.MESH
NN, NT, TN = ((1,), (0,)), ((1,), (1,)), ((0,), (0,))


def _params(sem=None):
    return pltpu.CompilerParams(dimension_semantics=sem, vmem_limit_bytes=VMEM_LIMIT)


def _pick(n, cands):
    for c in cands:
        if c <= n and n % c == 0:
            return c
    return n


def _mm(name, a, b, dims, out_dtype=F32):
    if dims == NN:
        (m, kk), (_, n) = a.shape, b.shape
    elif dims == NT:
        (m, kk), (n, _) = a.shape, b.shape
    else:
        (kk, m), (_, n) = a.shape, b.shape
    tm = _pick(m, (768, 1024, 512, 256, 128, 64, 32, 16, 8))
    tn = _pick(n, (1024, 1408, 768, 896, 512, 256, 128))
    tk = _pick(kk, (1024, 1408, 768, 512, 256, 128))
    gi, gj, gl = m // tm, n // tn, kk // tk
    if dims == NN:
        a_spec = pl.BlockSpec((tm, tk), lambda i, j, l: (i, l))
        b_spec = pl.BlockSpec((tk, tn), lambda i, j, l: (l, j))
    elif dims == NT:
        a_spec = pl.BlockSpec((tm, tk), lambda i, j, l: (i, l))
        b_spec = pl.BlockSpec((tn, tk), lambda i, j, l: (j, l))
    else:
        a_spec = pl.BlockSpec((tk, tm), lambda i, j, l: (l, i))
        b_spec = pl.BlockSpec((tk, tn), lambda i, j, l: (l, j))
    direct = gl == 1
    use_acc = (not direct) and out_dtype != F32

    def body(a_ref, b_ref, o_ref, *scratch):
        part = lax.dot_general(a_ref[...].astype(MXU_DTYPE), b_ref[...].astype(MXU_DTYPE), (dims, ((), ())),
                               preferred_element_type=F32)
        if direct:
            o_ref[...] = part.astype(out_dtype)
            return
        acc = scratch[0] if use_acc else o_ref
        l = pl.program_id(2)

        @pl.when(l == 0)
        def _():
            acc[...] = part

        @pl.when(l > 0)
        def _():
            acc[...] += part

        if use_acc:
            @pl.when(l == gl - 1)
            def _():
                o_ref[...] = acc[...].astype(out_dtype)

    return pl.pallas_call(
        body, grid=(gi, gj, gl), in_specs=[a_spec, b_spec],
        out_specs=pl.BlockSpec((tm, tn), lambda i, j, l: (i, j)),
        out_shape=jax.ShapeDtypeStruct((m, n), out_dtype),
        scratch_shapes=[pltpu.VMEM((tm, tn), F32)] if use_acc else [],
        compiler_params=_params(("parallel", "parallel", "arbitrary")), name=name,
    )(a, b)


def mm(name, a, w, carrier):
    @jax.custom_vjp
    def f(a, w, carrier):
        return _mm(name + "_f", a, w, NN, out_dtype=MXU_DTYPE)

    def fwd(a, w, carrier):
        return f(a, w, carrier), (a, w)

    def bwd(res, dc):
        a, w = res
        da = _mm(name + "_da", dc, w, NT, out_dtype=a.dtype)
        dw = _mm(name + "_dw", a, dc, TN)
        return da, None, dw

    f.defvjp(fwd, bwd)
    return f(a, w, carrier)


def _split(x, parts):
    w = x.shape[-1] // parts
    return [x[:, k * w:(k + 1) * w] for k in range(parts)]


def _cat(xs):
    return xs[0] if len(xs) == 1 else jnp.concatenate(xs, axis=-1)


def _shift_rows(x, prev_ref, next_ref, i, ctx_tiles, nt):
    tr = x.shape[0]
    rid = lax.broadcasted_iota(jnp.int32, x.shape, 0)
    first = jnp.logical_or(i == 0, i == ctx_tiles)
    last = jnp.logical_or(i == ctx_tiles - 1, i == nt - 1)
    prow = jnp.where(first, 0.0, prev_ref[SUBLANES - 1:SUBLANES, :].astype(F32))
    nrow = jnp.where(last, 0.0, next_ref[0:1, :].astype(F32))
    xm = jnp.where(rid == 0, prow, pltpu.roll(x, 1, 0))
    xp = jnp.where(rid == tr - 1, nrow, pltpu.roll(x, tr - 1, 0))
    return xm, xp


def rowwise(name, fn, rows, vecs, *, halo=(), seg=(), parts_r=None, parts_v=None, outs=(), reds=(), ctx_tiles=1, skip=None):
    nr, nv = len(rows), len(vecs)
    halo = tuple(halo) or (False,) * nr
    seg = tuple(seg) or (False,) * nv
    skip = tuple(skip or (0,) * nr)
    parts_r = tuple(parts_r or (1,) * nr)
    parts_v = tuple(parts_v or (1,) * nv)
    r_total = rows[0].shape[0]
    tr = ROW_TILE
    nt = r_total // tr
    assert r_total % tr == 0 and (ctx_tiles > 0 or not any(seg)) and not any(h and s for h, s in zip(halo, skip))

    def tile_map(i):
        return lambda t: (jnp.maximum(t - skip[i], 0), 0)

    def row_specs():
        sp = []
        for i, r in enumerate(rows):
            w = r.shape[1]
            sp.append(pl.BlockSpec((tr, w), tile_map(i)))
            if halo[i]:
                k = tr // SUBLANES
                sp.append(pl.BlockSpec((SUBLANES, w), lambda t: (jnp.maximum(t * k - 1, 0), 0)))
                sp.append(pl.BlockSpec((SUBLANES, w), lambda t: (jnp.minimum((t + 1) * k, nt * k - 1), 0)))
        return sp

    def vec_spec(j):
        w = vecs[j].shape[-1]
        if seg[j]:
            return pl.BlockSpec((None, 1, w), lambda t: (jnp.where(t >= ctx_tiles, 1, 0), 0, 0))
        return pl.BlockSpec((1, w), lambda t: (0, 0))

    def row_args(rv):
        a = []
        for i in range(nr):
            a += [rv[i]] * 3 if halo[i] else [rv[i]]
        return a

    def load(refs, t):
        pos, rp = 0, []
        for i in range(nr):
            x = refs[pos][...].astype(F32)
            if halo[i]:
                xm, xp = _shift_rows(x, refs[pos + 1], refs[pos + 2], t, ctx_tiles, nt)
                rp.append(list(zip(_split(x, parts_r[i]), _split(xm, parts_r[i]), _split(xp, parts_r[i]))))
                pos += 3
            else:
                rp.append(_split(x, parts_r[i]))
                pos += 1
        vp = []
        for j in range(nv):
            vp.append(_split(refs[pos][...].astype(F32), parts_v[j]))
            pos += 1
        return rp, vp, refs[pos:]

    n_out, n_red = len(outs), len(reds)

    def fwd_call(*rv):
        def body(*refs):
            t = pl.program_id(0)
            rp, vp, rest = load(refs, t)
            o_parts, r_parts = fn(rp, vp)
            for k in range(n_out):
                rest[k][...] = _cat(o_parts[k]).astype(outs[k][1])
            for k in range(n_red):
                ref = rest[n_out + k]

                @pl.when(t == 0)
                def _():
                    ref[...] = r_parts[k]

                @pl.when(t > 0)
                def _():
                    ref[...] += r_parts[k]

        res = pl.pallas_call(
            body, grid=(nt,),
            in_specs=row_specs() + [vec_spec(j) for j in range(nv)],
            out_specs=[pl.BlockSpec((tr, o[0]), lambda t: (t, 0)) for o in outs]
            + [pl.BlockSpec((1, w), lambda t: (0, 0)) for w in reds],
            out_shape=[jax.ShapeDtypeStruct((r_total, o[0]), o[1]) for o in outs]
            + [jax.ShapeDtypeStruct((1, w), F32) for w in reds],
            compiler_params=_params(("arbitrary",)), name=name + "_f",
        )(*row_args(rv), *rv[nr:])
        return tuple(res)

    def bwd_call(rv, cts):
        def body(*refs):
            t = pl.program_id(0)
            rp, vp, rest = load(refs, t)
            ct_o = [_split(rest[k][...].astype(F32), outs[k][2]) for k in range(n_out)]
            ct_r = [rest[n_out + k][...] for k in range(n_red)]
            rest = rest[n_out + n_red:]
            _, vjp = jax.vjp(fn, rp, vp)
            d_rp, d_vp = vjp((ct_o, ct_r))
            pos = 0
            for i in range(nr):
                if halo[i]:
                    for c in range(3):
                        rest[pos + c][...] = _cat([p[c] for p in d_rp[i]])
                    pos += 3
                else:
                    rest[pos][...] = _cat(d_rp[i]).astype(rows[i].dtype)
                    pos += 1
            for j in range(nv):
                ref, val = rest[pos + j], _cat(d_vp[j])
                start = jnp.logical_or(t == 0, t == ctx_tiles) if seg[j] else t == 0

                @pl.when(start)
                def _():
                    ref[...] = val

                @pl.when(jnp.logical_not(start))
                def _():
                    ref[...] += val

        d_row_specs, d_row_shapes = [], []
        for i, r in enumerate(rows):
            w = r.shape[1]
            for _ in range(3 if halo[i] else 1):
                d_row_specs.append(pl.BlockSpec((tr, w), tile_map(i)))
                d_row_shapes.append(jax.ShapeDtypeStruct(r.shape, F32 if halo[i] else r.dtype))
        res = pl.pallas_call(
            body, grid=(nt,),
            in_specs=row_specs() + [vec_spec(j) for j in range(nv)]
            + [pl.BlockSpec((tr, o[0]), lambda t: (t, 0)) for o in outs]
            + [pl.BlockSpec((1, w), lambda t: (0, 0)) for w in reds],
            out_specs=d_row_specs + [vec_spec(j) for j in range(nv)],
            out_shape=d_row_shapes + [jax.ShapeDtypeStruct(v.shape, F32) for v in vecs],
            compiler_params=_params(("arbitrary",)), name=name + "_b",
        )(*row_args(rv), *rv[nr:], *cts)
        d_rows, pos = [], 0
        for i in range(nr):
            if halo[i]:
                d_rows.append(_unshift(res[pos], res[pos + 1], res[pos + 2], ctx_tiles * tr).astype(rows[i].dtype))
                pos += 3
            else:
                d_rows.append(res[pos])
                pos += 1
        return tuple(d_rows) + tuple(res[pos:])

    @jax.custom_vjp
    def f(*rv):
        return fwd_call(*rv)

    f.defvjp(lambda *rv: (fwd_call(*rv), rv), lambda rv, cts: bwd_call(rv, cts))
    return f(*rows, *vecs)


def _unshift(d, dm, dp, ctx_rows):
    r = d.shape[0]
    t = lax.broadcasted_iota(jnp.int32, (r, 1), 0)
    zero = jnp.zeros((1, d.shape[1]), d.dtype)
    from_m = jnp.concatenate([dm[1:], zero], axis=0)
    from_p = jnp.concatenate([zero, dp[:-1]], axis=0)
    from_m = jnp.where(t == ctx_rows - 1, 0.0, from_m)
    from_p = jnp.where(t == ctx_rows, 0.0, from_p)
    return d + from_m + from_p


def _sigmoid(x):
    return 0.5 * (jnp.tanh(0.5 * x) + 1.0)


def _silu(x):
    return x * _sigmoid(x)


def _softplus(x):
    return jnp.maximum(x, 0.0) + jnp.log(1.0 + jnp.exp(-jnp.abs(x)))


def _rms(x, g):
    return x * lax.rsqrt(jnp.mean(x * x, axis=-1, keepdims=True) + EPS) * g


def _fn_modulate(r, v):
    (x,), (g,), (sh,), (sc,) = r[0], v[0], v[1], v[2]
    return [[_rms(x, g) * (1.0 + sc) + sh]], []


def _fn_resmod(r, v):
    (x,), (y,) = r
    (gate,), (g,), (sh,), (sc,) = v
    xn = x + gate * y
    return [[xn], [_rms(xn, g) * (1.0 + sc) + sh]], []


def _fn_final(r, v):
    (x,), (y,), (tgt,) = r
    (gate,), (g,), (counts,) = v
    err = _rms(x + gate * y, g) - tgt
    row_loss = jnp.mean(err * err, axis=-1, keepdims=True)
    total = 0.5 * jnp.sum(row_loss, axis=0, keepdims=True)
    return [], [total * counts]


def _fn_dnprep(r, v):
    qkv, (ab,) = r
    w0, w1, w2, (alog,), (dtb,) = v
    out = [[], [], []]
    for n, (x, xm, xp) in enumerate(qkv):
        which = n // DN_HEADS
        y = _silu(xm * w0[n] + x * w1[n] + xp * w2[n])
        if which < 2:
            y = y * lax.rsqrt(jnp.sum(y * y, axis=-1, keepdims=True) + EPS)
        if which == 0:
            y = y * (HEAD_DIM ** -0.5)
        out[which].append(y)
    lane = lax.broadcasted_iota(jnp.int32, ab.shape, 1)
    g = -jnp.exp(alog) * _softplus(ab + dtb)
    gb = jnp.where(lane < N_DIRHEAD, g, jnp.where(lane < 2 * N_DIRHEAD, _sigmoid(ab), 0.0))
    return out + [[gb]], []


def _fn_dnpost(r, v):
    of, ob, z = r
    (g,) = v[0]
    return [[_rms(a + b, g) * _silu(c) for a, b, c in zip(of, ob, z)]], []


def _fn_sub(r, v):
    return [[r[0][0] - r[1][0]]], []


def _fn_scale(r, v):
    return [[r[0][0] * v[0][0]]], []


def _fn_shortconv(r, v):
    (xin, gb, gc), = r
    (w0,), (w1,), (w2,) = v
    u, um, up = (gc[k] * xin[k] for k in range(3))
    return [[gb[0] * (um * w0 + u * w1 + up * w2)]], []


def _fn_merge(r, v):
    gates, (ya,), (yb,), (yc,) = r
    return [[_sigmoid(gates[0]) * ya + _sigmoid(gates[1]) * yb + _sigmoid(gates[2]) * yc]], []


def _fn_swiglu(r, v):
    return [[_silu(r[0][0]) * r[1][0]]], []


def _dot_hi(a, b):
    return lax.dot_general(a, b, (NN, ((), ())), precision=HIGHEST, preferred_element_type=F32)


def _bdot(a, b, dims):
    (ca,), (cb,) = dims
    return lax.dot_general(a.astype(MXU_DTYPE), b.astype(MXU_DTYPE), (((ca + 1,), (cb + 1,)), ((0,), (0,))),
                           preferred_element_type=F32)


def _hi_lo(a):
    hi = a.astype(MXU_DTYPE)
    return hi, (a - hi.astype(F32)).astype(MXU_DTYPE)


def _bdot3_raw(a, b, dims):
    (ah, al), (bh, bl) = _hi_lo(a), _hi_lo(b)
    return _bdot(ah, bh, dims) + (_bdot(ah, bl, dims) + _bdot(al, bh, dims))


@jax.custom_vjp
def _bdot3(a, b):
    return _bdot3_raw(a, b, NN)


_bdot3.defvjp(lambda a, b: (_bdot3_raw(a, b, NN), (a, b)),
              lambda res, ct: (_bdot3_raw(ct, res[1], NT), _bdot3_raw(res[0], ct, TN)))


def _inv_doubling(a):
    c = a.shape[-1]
    ii, jj = (lax.broadcasted_iota(jnp.int32, a.shape, d) for d in (1, 2))
    t = jnp.where(ii == jj, 1.0, 0.0) - a
    p = a
    for _ in range(int(math.log2(c)) - 1):
        p = _bdot3(p, p)
        t = t + _bdot3(t, p)
    return t


@jax.custom_vjp
def _inv_unit_triangular(a):
    return _inv_doubling(a)


def _inv_fwd(a):
    t = _inv_doubling(a)
    return t, t


_inv_unit_triangular.defvjp(_inv_fwd, lambda t, ct: (-_bdot3_raw(_bdot3_raw(t, ct, TN), t, NT),))


def _dn_pre(q4, k4, v4, gb):
    nb, c = N_DIRHEAD, q4.shape[1]
    q, k, v = (jnp.concatenate([a, a], axis=0) for a in (q4, k4, v4))
    lane = lax.broadcasted_iota(jnp.int32, gb.shape, 1)
    col = lambda j: jnp.sum(jnp.where(lane == j, gb, 0.0), axis=1, keepdims=True)
    g_col = jnp.concatenate([col(j)[None] for j in range(nb)], axis=0)
    b_col = jnp.concatenate([col(nb + j)[None] for j in range(nb)], axis=0)
    bi, ii, jj = (lax.broadcasted_iota(jnp.int32, (nb, c, c), a) for a in range(3))
    ahead = jnp.where(bi >= DN_HEADS, jj - ii, ii - jj)
    incl, strict, eye = ahead >= 0, ahead > 0, ahead == 0
    g_row = jnp.sum(jnp.where(eye, g_col, 0.0), axis=1, keepdims=True)
    gc_col = jnp.sum(jnp.where(incl, g_row, 0.0), axis=2, keepdims=True)
    gc_row = jnp.sum(jnp.where(ahead <= 0, g_col, 0.0), axis=1, keepdims=True)
    decay = jnp.where(incl, jnp.exp(jnp.where(incl, gc_col - gc_row, 0.0)), 0.0)
    kb = k * b_col
    a = _bdot(kb, k, NT) * jnp.where(strict, decay, 0.0)
    t = _inv_unit_triangular(a)
    e_gc = jnp.exp(gc_col)
    u = _bdot3(t, v * b_col)
    w = _bdot3(t, kb * e_gc)
    g_last = jnp.sum(g_col, axis=1, keepdims=True)
    k_state = k * jnp.exp(g_last - gc_col)
    a_qk = _bdot(q, k, NT) * decay
    return u, w, q * e_gc, k_state, a_qk, jnp.broadcast_to(jnp.exp(g_last), (nb, 1, LANES))


def _dn_step(s, u, w, qd, ks, aqk, gl):
    v_new = u - _bdot(w, s, NN)
    o = _bdot(qd, s, NN) + _bdot(aqk, v_new, NN)
    return s * gl[:, :, :1] + _bdot(ks, v_new, TN), o


def _heads(x):
    return jnp.concatenate([x[None, :, h * HEAD_DIM:(h + 1) * HEAD_DIM] for h in range(DN_HEADS)], axis=0)


def _unheads(x):
    return jnp.concatenate([x[h] for h in range(x.shape[0])], axis=-1)


def _dn_rev(t, nc, n):
    return jnp.where(t < nc, nc - 1 - t, n - 1 - (t - nc))


def _pre_shapes(n):
    c, d, h = DN_CHUNK, HEAD_DIM, DN_HEADS
    shapes = [(n, h, c, d)] * 8 + [(n, h, c, c)] * 2 + [(n, h, 1, LANES)] * 2
    return shapes, [pl.BlockSpec((None,) + s[1:], lambda t: (t, 0, 0, 0)) for s in shapes]


def _dn_pre_fwd(name, q, k, v, gb):
    n = q.shape[0] // DN_CHUNK
    shapes, specs = _pre_shapes(n)

    def body(q_ref, k_ref, v_ref, g_ref, *o_refs):
        res = _dn_pre(_heads(q_ref[...]), _heads(k_ref[...]), _heads(v_ref[...]), g_ref[...])
        for i, r in enumerate(res):
            o_refs[2 * i][...] = r[:DN_HEADS]
            o_refs[2 * i + 1][...] = r[DN_HEADS:]

    wide = pl.BlockSpec((DN_CHUNK, DN_HEADS * HEAD_DIM), lambda t: (t, 0))
    narrow = pl.BlockSpec((DN_CHUNK, LANES), lambda t: (t, 0))
    return pl.pallas_call(body, grid=(n,), in_specs=[wide] * 3 + [narrow], out_specs=specs,
                          out_shape=[jax.ShapeDtypeStruct(s, F32) for s in shapes],
                          compiler_params=_params(("parallel",)), name=name + "_pre_f")(q, k, v, gb)


def _dn_pre_bwd(name, q, k, v, gb, cts):
    n = q.shape[0] // DN_CHUNK
    _, specs = _pre_shapes(n)
    n_ct = len(specs)

    def body(q_ref, k_ref, v_ref, g_ref, *refs):
        ct = tuple(jnp.concatenate([refs[i][...], refs[i + 1][...]], axis=0) for i in range(0, n_ct, 2))
        _, vjp = jax.vjp(_dn_pre, _heads(q_ref[...]), _heads(k_ref[...]), _heads(v_ref[...]), g_ref[...])
        dq, dk, dv, dg = vjp(ct)
        for r, val in zip(refs[n_ct:], (_unheads(dq), _unheads(dk), _unheads(dv), dg)):
            r[...] = val

    wide = pl.BlockSpec((DN_CHUNK, DN_HEADS * HEAD_DIM), lambda t: (t, 0))
    narrow = pl.BlockSpec((DN_CHUNK, LANES), lambda t: (t, 0))
    return pl.pallas_call(body, grid=(n,), in_specs=[wide] * 3 + [narrow] + specs, out_specs=[wide] * 3 + [narrow],
                          out_shape=[jax.ShapeDtypeStruct(q.shape, F32)] * 3 + [jax.ShapeDtypeStruct(gb.shape, F32)],
                          compiler_params=_params(("parallel",)), name=name + "_pre_b")(q, k, v, gb, *cts)


def _scan_specs(pre_shapes, fw, bw):
    maps = (lambda t: (fw(t), 0, 0, 0), lambda t: (bw(t), 0, 0, 0))
    return [pl.BlockSpec((None,) + s[1:], maps[i % 2]) for i, s in enumerate(pre_shapes)]


def _dn_scan_fwd(name, pre, nc):
    n = pre[0].shape[0]
    r = n * DN_CHUNK
    hd, nh, nb = HEAD_DIM, DN_HEADS, N_DIRHEAD
    shapes, _ = _pre_shapes(n)
    n_in = len(shapes)

    def body(*refs):
        ins, (of_ref, ob_ref, sf_ref, sb_ref, s_scr) = refs[:n_in], refs[n_in:]
        t = pl.program_id(0)

        @pl.when(t == 0)
        def _():
            s_scr[...] = jnp.zeros_like(s_scr)

        s = s_scr[...]
        sf_ref[...] = s[:nh]
        sb_ref[...] = s[nh:]
        args = [jnp.concatenate([ins[i][...], ins[i + 1][...]], axis=0) for i in range(0, n_in, 2)]
        s2, o = _dn_step(s, *args)
        s_scr[...] = s2
        of_ref[...] = _unheads(o[:nh])
        ob_ref[...] = _unheads(o[nh:])

    fw = lambda t: t
    bw = lambda t: _dn_rev(t, nc, n)
    wide, st = (DN_CHUNK, nh * hd), (None, nh, hd, hd)
    return pl.pallas_call(
        body, grid=(n,), in_specs=_scan_specs(shapes, fw, bw),
        out_specs=[pl.BlockSpec(wide, lambda t: (fw(t), 0)), pl.BlockSpec(wide, lambda t: (bw(t), 0)),
                   pl.BlockSpec(st, lambda t: (fw(t), 0, 0, 0)), pl.BlockSpec(st, lambda t: (bw(t), 0, 0, 0))],
        out_shape=[jax.ShapeDtypeStruct((r, nh * hd), F32)] * 2 + [jax.ShapeDtypeStruct((n, nh, hd, hd), F32)] * 2,
        scratch_shapes=[pltpu.VMEM((nb, hd, hd), F32)],
        compiler_params=_params(("arbitrary",)), name=name + "_scan_f",
    )(*pre)


def _dn_scan_bwd(name, pre, sall_f, sall_b, do_f, do_b, nc):
    n = pre[0].shape[0]
    hd, nh, nb = HEAD_DIM, DN_HEADS, N_DIRHEAD
    shapes, _ = _pre_shapes(n)
    n_in = len(shapes)

    def body(*refs):
        ins, (sf_ref, sb_ref, dof_ref, dob_ref) = refs[:n_in], refs[n_in:n_in + 4]
        outs, ds_scr = refs[n_in + 4:2 * n_in + 4], refs[2 * n_in + 4]
        t = pl.program_id(0)

        @pl.when(t == 0)
        def _():
            ds_scr[...] = jnp.zeros_like(ds_scr)

        args = [jnp.concatenate([ins[i][...], ins[i + 1][...]], axis=0) for i in range(0, n_in, 2)]
        s = jnp.concatenate([sf_ref[...], sb_ref[...]], axis=0)
        do = jnp.concatenate([_heads(dof_ref[...]), _heads(dob_ref[...])], axis=0)
        _, vjp = jax.vjp(_dn_step, s, *args)
        cts = vjp((ds_scr[...], do))
        ds_scr[...] = cts[0]
        for i, ct in enumerate(cts[1:]):
            outs[2 * i][...] = ct[:nh]
            outs[2 * i + 1][...] = ct[nh:]

    fw = lambda t: n - 1 - t
    bw = lambda t: _dn_rev(n - 1 - t, nc, n)
    wide, st = (DN_CHUNK, nh * hd), (None, nh, hd, hd)
    return tuple(pl.pallas_call(
        body, grid=(n,),
        in_specs=_scan_specs(shapes, fw, bw)
        + [pl.BlockSpec(st, lambda t: (fw(t), 0, 0, 0)), pl.BlockSpec(st, lambda t: (bw(t), 0, 0, 0)),
           pl.BlockSpec(wide, lambda t: (fw(t), 0)), pl.BlockSpec(wide, lambda t: (bw(t), 0))],
        out_specs=_scan_specs(shapes, fw, bw), out_shape=[jax.ShapeDtypeStruct(s, F32) for s in shapes],
        scratch_shapes=[pltpu.VMEM((nb, hd, hd), F32)],
        compiler_params=_params(("arbitrary",)), name=name + "_scan_b",
    )(*pre, sall_f, sall_b, do_f, do_b))


def deltanet(name, q, k, v, gb, ctx_rows):
    nc = ctx_rows // DN_CHUNK

    @jax.custom_vjp
    def pre(q, k, v, gb):
        return tuple(_dn_pre_fwd(name, q, k, v, gb))

    pre.defvjp(lambda *a: (tuple(_dn_pre_fwd(name, *a)), a), lambda a, cts: tuple(_dn_pre_bwd(name, *a, cts)))

    @jax.custom_vjp
    def scan(*ops):
        return tuple(_dn_scan_fwd(name, ops, nc)[:2])

    def scan_fwd(*ops):
        of, ob, sf, sb = _dn_scan_fwd(name, ops, nc)
        return (of, ob), (ops, sf, sb)

    scan.defvjp(scan_fwd, lambda res, cts: _dn_scan_bwd(name, res[0], res[1], res[2], cts[0], cts[1], nc))
    return scan(*pre(q, k, v, gb))


def _box_matrix(l, w):
    lo, hi = w // 2, w - 1 - w // 2
    pos = np.arange(l)
    start, end = np.clip(pos - lo, 0, l), np.clip(pos + hi + 1, 0, l)
    col = np.arange(l)[None, :]
    return ((col >= start[:, None]) & (col < end[:, None])) / (end - start)[:, None].astype(np.float64)


def _pool_matrices(ctx_rows, grid_rows):
    assert ctx_rows == ROW_TILE and ROW_TILE % GRID_W == 0
    ctx = np.stack([_box_matrix(ctx_rows, w) for w in POOL_WINDOWS])
    cols = np.stack([np.kron(np.eye(ROW_TILE // GRID_W), _box_matrix(GRID_W, w)) for w in POOL_WINDOWS])
    rows = np.stack([_box_matrix(grid_rows, w) for w in POOL_WINDOWS])[None]
    return np.stack([ctx, cols]).astype(np.float32), rows.astype(np.float32)


def _pool_apply(name, x, mats, group_w, seg_tiles, lane_tile):
    r, w = x.shape
    b = mats.shape[-1]
    ng = mats.shape[1]
    period = ng * group_w

    def body(x_ref, m_ref, o_ref):
        xv = x_ref[...].astype(F32)
        lane = lax.broadcasted_iota(jnp.int32, xv.shape, 1)
        grp = (lane % period) // group_w
        acc = jnp.zeros_like(xv)
        for g in range(ng):
            acc = acc + jnp.where(grp == g, _dot_hi(m_ref[g], xv), 0.0)
        o_ref[...] = acc

    return pl.pallas_call(
        body, grid=(r // b, w // lane_tile),
        in_specs=[pl.BlockSpec((b, lane_tile), lambda i, j: (i, j)),
                  pl.BlockSpec((None, ng, b, b), lambda i, j: (jnp.where(i >= seg_tiles, 1, 0) if mats.shape[0] > 1 else 0, 0, 0, 0))],
        out_specs=pl.BlockSpec((b, lane_tile), lambda i, j: (i, j)),
        out_shape=jax.ShapeDtypeStruct(x.shape, F32),
        compiler_params=_params(("parallel", "parallel")), name=name,
    )(x, mats)


def pool_means(name, u, ctx_rows):
    r, pw = u.shape
    grid_rows = (r - ctx_rows) // GRID_W
    gw = pw // len(POOL_WINDOWS)
    m1, m2 = _pool_matrices(ctx_rows, grid_rows)
    lane_tile = min(2048, GRID_W * pw)

    def apply(x, a1, a2, tag):
        y = _pool_apply(name + tag + "1", x, jnp.asarray(a1), gw, ctx_rows // ROW_TILE, pw)
        lat = y[ctx_rows:].reshape(grid_rows, GRID_W * pw)
        lat = _pool_apply(name + tag + "2", lat, jnp.asarray(a2), gw, 0, lane_tile)
        return jnp.concatenate([y[:ctx_rows], lat.reshape(r - ctx_rows, pw)], axis=0)

    @jax.custom_vjp
    def f(u):
        return apply(u, m1, m2, "_f")

    tr = lambda m: np.ascontiguousarray(np.swapaxes(m, -1, -2))
    f.defvjp(lambda u: (apply(u, m1, m2, "_f"), None), lambda _, ct: (apply(ct, tr(m1), tr(m2), "_b").astype(u.dtype),))
    return f(u)


def _ew(name, fn, *xs, n_out=1):
    shapes = jax.eval_shape(lambda *a: fn(*a), *xs)
    shapes = shapes if isinstance(shapes, (tuple, list)) else (shapes,)

    def body(*refs):
        res = fn(*[r[...] for r in refs[:len(xs)]])
        res = res if isinstance(res, (tuple, list)) else (res,)
        for r, o in zip(res, refs[len(xs):]):
            o[...] = r

    out = pl.pallas_call(body, out_shape=[jax.ShapeDtypeStruct(s.shape, s.dtype) for s in shapes],
                         compiler_params=_params(), name=name)(*xs)
    return out[0] if len(shapes) == 1 else tuple(out)


def _adamw_math(w, g, m, v):
    m2 = ADAM_B1 * m + (1.0 - ADAM_B1) * g
    v2 = ADAM_B2 * v + (1.0 - ADAM_B2) * (g * g)
    m_hat = m2 / (1.0 - ADAM_B1 ** ADAM_STEP)
    v_hat = v2 / (1.0 - ADAM_B2 ** ADAM_STEP)
    delta = -ADAM_LR * (m_hat / (jnp.sqrt(v_hat) + ADAM_EPS) + ADAM_WD * w)
    return delta, m2, v2


def adamw(name, w, g, m, v):
    r, c = w.shape
    tr = _pick(r, (256, 128, 64, 32, 16, 8))

    def body(w_ref, g_ref, m_ref, v_ref, d_ref, m2_ref, v2_ref):
        d_ref[...], m2_ref[...], v2_ref[...] = _adamw_math(w_ref[...], g_ref[...], m_ref[...], v_ref[...])

    spec = pl.BlockSpec((tr, c), lambda i: (i, 0))
    return pl.pallas_call(body, grid=(r // tr,), in_specs=[spec] * 4, out_specs=[spec] * 3,
                          out_shape=[jax.ShapeDtypeStruct(w.shape, F32)] * 3,
                          compiler_params=_params(("parallel",)), name=name)(w, g, m, v)


def _sum_rows(name, xs, out_dtype=F32):
    r, c = xs[0].shape
    tr = _pick(r, (512, 256, 128, 64, 32, 16, 8))

    def body(*refs):
        acc = refs[0][...].astype(F32)
        for ref in refs[1:-1]:
            acc = acc + ref[...].astype(F32)
        refs[-1][...] = acc.astype(out_dtype)

    spec = pl.BlockSpec((tr, c), lambda i: (i, 0))
    return pl.pallas_call(body, grid=(r // tr,), in_specs=[spec] * len(xs), out_specs=spec,
                          out_shape=jax.ShapeDtypeStruct((r, c), out_dtype),
                          compiler_params=_params(("parallel",)), name=name)(*xs)


def _place():
    return lax.axis_index("x"), lax.axis_index("y"), lax.axis_index("c")


def _chip_peers(x, y, c):
    return [(1 - x, y, c), (x, 1 - y, c), (1 - x, 1 - y, c)]


def all_gather8(name, block):
    m_per, n = block.shape

    def body(x_ref, out_ref, send_sems, recv_sems, local_sem):
        x, y, c = _place()
        me, sibling = (x, y, c), (x, y, 1 - c)
        chips = [(1 - x, y), (x, 1 - y), (1 - x, 1 - y)]

        def rows(px, py, pc):
            return out_ref.at[pl.ds((4 * px + 2 * py + pc) * m_per, m_per), :]

        def copy(k, blk, to, src=None):
            return pltpu.make_async_remote_copy(
                src_ref=rows(*blk) if src is None else src, dst_ref=rows(*blk),
                send_sem=send_sems.at[k], recv_sem=recv_sems.at[k], device_id=to, device_id_type=MESH_ID)

        mine = pltpu.make_async_copy(x_ref, rows(*me), local_sem)
        mine.start()
        first = [copy(0, me, sibling, src=x_ref)]
        first += [copy(1 + j, me, (*chip, c), src=x_ref) for j, chip in enumerate(chips)]
        for cp in first:
            cp.start()
        passed = [copy(4 + j, (*chip, c), sibling) for j, chip in enumerate(chips)]
        for j, chip in enumerate(chips):
            copy(1 + j, (*chip, c), me).wait_recv()
            passed[j].start()
        copy(0, sibling, me).wait_recv()
        for j, chip in enumerate(chips):
            copy(4 + j, (*chip, 1 - c), me).wait_recv()
        for cp in first + passed:
            cp.wait_send()
        mine.wait()

    return pl.pallas_call(
        body, out_shape=jax.ShapeDtypeStruct((8 * m_per, n), block.dtype),
        in_specs=[pl.BlockSpec(memory_space=pltpu.VMEM)], out_specs=pl.BlockSpec(memory_space=pltpu.VMEM),
        scratch_shapes=[pltpu.SemaphoreType.DMA((7,)), pltpu.SemaphoreType.DMA((7,)), pltpu.SemaphoreType.DMA],
        compiler_params=_params(), name=name,
    )(block)


def gather_chips(name, shard):
    def body(x_ref, out_ref, send_sems, recv_sems, local_sem):
        x, y, c = _place()
        mine = pltpu.make_async_copy(x_ref, out_ref.at[2 * x + y], local_sem)
        mine.start()
        copies = []
        for p, peer in enumerate(_chip_peers(x, y, c)):
            cp = pltpu.make_async_remote_copy(
                src_ref=x_ref, dst_ref=out_ref.at[2 * x + y], send_sem=send_sems.at[p], recv_sem=recv_sems.at[p],
                device_id=peer, device_id_type=MESH_ID)
            cp.start()
            copies.append(cp)
        for p, (px, py, _) in enumerate(_chip_peers(x, y, c)):
            pltpu.make_async_remote_copy(
                src_ref=x_ref, dst_ref=out_ref.at[2 * px + py], send_sem=send_sems.at[p], recv_sem=recv_sems.at[p],
                device_id=(px, py, c), device_id_type=MESH_ID).wait_recv()
        for cp in copies:
            cp.wait_send()
        mine.wait()

    hbm = pl.BlockSpec(memory_space=pltpu.HBM)
    return pl.pallas_call(
        body, out_shape=jax.ShapeDtypeStruct((4,) + shard.shape, shard.dtype), in_specs=[hbm], out_specs=hbm,
        scratch_shapes=[pltpu.SemaphoreType.DMA((3,)), pltpu.SemaphoreType.DMA((3,)), pltpu.SemaphoreType.DMA],
        compiler_params=_params(), name=name,
    )(shard)


def swap_sibling(name, block):
    def body(x_ref, out_ref, send_sem, recv_sem):
        x, y, c = _place()
        cp = pltpu.make_async_remote_copy(src_ref=x_ref, dst_ref=out_ref, send_sem=send_sem, recv_sem=recv_sem,
                                          device_id=(x, y, 1 - c), device_id_type=MESH_ID)
        cp.start()
        cp.wait()

    hbm = pl.BlockSpec(memory_space=pltpu.HBM)
    return pl.pallas_call(
        body, out_shape=jax.ShapeDtypeStruct(block.shape, block.dtype), in_specs=[hbm], out_specs=hbm,
        scratch_shapes=[pltpu.SemaphoreType.DMA, pltpu.SemaphoreType.DMA],
        compiler_params=_params(), name=name,
    )(block)


def scatter_chips(name, pieces):
    def body(x_ref, out_ref, send_sems, recv_sems):
        x, y, c = _place()
        copies = []
        for p, (px, py, pc) in enumerate(_chip_peers(x, y, c)):
            cp = pltpu.make_async_remote_copy(
                src_ref=x_ref.at[2 * px + py], dst_ref=out_ref.at[p], send_sem=send_sems.at[p], recv_sem=recv_sems.at[p],
                device_id=(px, py, pc), device_id_type=MESH_ID)
            cp.start()
            copies.append(cp)
        for cp in copies:
            cp.wait()

    hbm = pl.BlockSpec(memory_space=pltpu.HBM)
    return pl.pallas_call(
        body, out_shape=jax.ShapeDtypeStruct((3,) + pieces.shape[1:], pieces.dtype), in_specs=[hbm], out_specs=hbm,
        scratch_shapes=[pltpu.SemaphoreType.DMA((3,)), pltpu.SemaphoreType.DMA((3,))],
        compiler_params=_params(), name=name,
    )(pieces)


def _sum_devices(name, got, fold=False):
    def body(a_ref, *o_refs):
        acc = a_ref[0]
        for i in range(1, N_DEV):
            acc = acc + a_ref[i]
        o_refs[0][...] = acc
        if fold:
            o_refs[1][...] = acc + pltpu.roll(acc, SUBLANES // 2, 0)

    shape = jax.ShapeDtypeStruct(got.shape[1:], F32)
    out = pl.pallas_call(body, out_shape=[shape] * (2 if fold else 1), compiler_params=_params(), name=name)(got)
    return out if fold else out[0]


W_IN_GROUPS = ("qkv", "z", "ab", "pool", "sc", "gate")


def _w_in_bounds(d, pw, scw):
    off_z = 3 * DN_WIDTH
    off_a = off_z + DN_WIDTH
    off_pool = off_a + 2 * N_DIRHEAD
    off_sc = off_pool + pw
    off_gate = off_sc + 3 * scw
    return (0, off_z, off_a, off_pool, off_sc, off_gate, off_gate + 3 * d)


def split_matrices(full, pw, scw):
    depth, d = full["w_in"].shape[:2]
    ff = full["w_down"].shape[1]
    bounds = _w_in_bounds(d, pw, scw)
    out = {k: [] for k in W_IN_GROUPS + ("g", "u", "br_a", "br_b", "br_c", "o", "down")}
    for l in range(depth):
        for k, a, b in zip(W_IN_GROUPS, bounds[:-1], bounds[1:]):
            w = full["w_in"][l][:, a:b]
            out[k].append(_pad_lanes(w) if k == "ab" else w)
        out["g"].append(full["w_gu"][l][:, :ff])
        out["u"].append(full["w_gu"][l][:, ff:])
        for k in ("br_a", "br_b", "br_c", "o", "down"):
            out[k].append(full["w_" + k][l])
    return out


def join_matrix_grads(g, pw, scw):
    depth = len(g["qkv"])
    n_ab = 2 * N_DIRHEAD
    w_in = [jnp.concatenate([g[k][l][:, :n_ab] if k == "ab" else g[k][l] for k in W_IN_GROUPS], axis=1) for l in range(depth)]
    out = {"w_in": jnp.stack(w_in), "w_gu": jnp.stack([jnp.concatenate([g["g"][l], g["u"][l]], axis=1) for l in range(depth)])}
    for k in ("br_a", "br_b", "br_c", "o", "down"):
        out["w_" + k] = jnp.stack(g[k])
    return out


def _row(v):
    return v.reshape(1, -1)


def _pad_lanes(v, width=LANES):
    return jnp.pad(v, ((0, 0), (0, width - v.shape[1])))


def _block_diag(blocks):
    g, n, _ = blocks.shape
    out = jnp.zeros((g * n, g * n), blocks.dtype)
    for i in range(g):
        out = out.at[i * n:(i + 1) * n, i * n:(i + 1) * n].set(blocks[i])
    return out


def local_loss(p, carriers, x, mod_lat, mod_ctx, wts, ctx, target):
    ctx_rows, d = ctx.shape
    depth = len(wts["qkv"])
    pw, scw = p["pool_scale"].shape[1], p["sc_conv_w"].shape[2]
    ff = wts["down"][0].shape[0]
    ct = ctx_rows // ROW_TILE
    xs = jnp.concatenate([ctx, x], axis=0)
    seg = lambda l, k: jnp.stack([mod_ctx[l, k], mod_lat[l, k]]).reshape(2, 1, d)
    dn = gate2 = None
    for l in range(depth):
        sh1, sc1, g1, sh2, sc2, g2 = (seg(l, k) for k in range(6))
        tag = f"l{l}_"
        lin = lambda key, a: mm(tag + key, a, wts[key][l], carriers[key][l])
        if l == 0:
            (h1,) = rowwise(tag + "mod", _fn_modulate, [xs], [_row(p["norm1_g"][l]), sh1, sc1], seg=(False, True, True),
                            outs=[(d, F32, 1)], ctx_tiles=ct)
        else:
            xs, h1 = rowwise(tag + "resmod1", _fn_resmod, [xs, dn], [gate2, _row(p["norm1_g"][l]), sh1, sc1],
                             seg=(True, False, True, True), outs=[(d, F32, 1), (d, F32, 1)], ctx_tiles=ct)
        p_qkv, p_z, p_ab, p_pool, p_sc, p_gate = (lin(key, h1) for key in W_IN_GROUPS)
        cw = p["dn_conv_w"][l]
        q, k, v, gb = rowwise(
            tag + "dnprep", _fn_dnprep, [p_qkv, p_ab],
            [cw[0:1], cw[1:2], cw[2:3], _pad_lanes(p["dn_a_log"][l].reshape(1, -1)), _pad_lanes(p["dn_dt_bias"][l].reshape(1, -1))],
            halo=(True, False), parts_r=(3 * DN_HEADS, 1), parts_v=(3 * DN_HEADS,) * 3 + (1, 1),
            outs=[(DN_WIDTH, F32, DN_HEADS)] * 3 + [(LANES, F32, 1)], ctx_tiles=ct)
        o_f, o_b = deltanet(tag + "dn", q, k, v, gb, ctx_rows)
        (oz,) = rowwise(tag + "dnpost", _fn_dnpost, [o_f, o_b, p_z], [_row(p["dn_norm_g"][l])], parts_r=(DN_HEADS,) * 3,
                        outs=[(DN_WIDTH, MXU_DTYPE, DN_HEADS)], ctx_tiles=ct)
        y_a = lin("br_a", oz)
        means = pool_means(tag + "box", p_pool, ctx_rows)
        (dpool,) = rowwise(tag + "poolsub", _fn_sub, [means, p_pool], [], outs=[(pw, MXU_DTYPE, 1)], ctx_tiles=ct)
        pool_mat = _block_diag(p["pool_w"][l])
        mixed = mm(tag + "poolw", dpool, pool_mat.astype(MXU_DTYPE), pool_mat)
        (yb_in,) = rowwise(tag + "poolscale", _fn_scale, [mixed], [_row(p["pool_scale"][l])], outs=[(pw, MXU_DTYPE, 1)], ctx_tiles=ct)
        y_b = lin("br_b", yb_in)
        sw = p["sc_conv_w"][l]
        (yc_in,) = rowwise(tag + "sconv", _fn_shortconv, [p_sc], [sw[0:1], sw[1:2], sw[2:3]], halo=(True,), parts_r=(3,),
                           outs=[(scw, MXU_DTYPE, 1)], ctx_tiles=ct)
        y_c = lin("br_c", yc_in)
        (y,) = rowwise(tag + "merge", _fn_merge, [p_gate, y_a, y_b, y_c], [], parts_r=(3, 1, 1, 1),
                       outs=[(d, MXU_DTYPE, 1)], ctx_tiles=ct)
        mix = lin("o", y)
        xs, h2 = rowwise(tag + "resmod2", _fn_resmod, [xs, mix], [g1, _row(p["norm2_g"][l]), sh2, sc2],
                         seg=(True, False, True, True), outs=[(d, F32, 1), (d, F32, 1)], ctx_tiles=ct)
        (act,) = rowwise(tag + "swiglu", _fn_swiglu, [lin("g", h2), lin("u", h2)], [],
                         outs=[(ff, MXU_DTYPE, 1)], ctx_tiles=ct)
        dn = lin("down", act)
        gate2 = g2
    counts = jnp.concatenate([jnp.zeros((1, 1, LANES), F32), jnp.ones((1, 1, LANES), F32)])
    (total,) = rowwise("final", _fn_final, [xs, dn, target], [gate2, _row(p["final_norm_g"]), counts],
                       seg=(True, False, True), reds=[LANES], ctx_tiles=ct, skip=(0, 0, ct))
    return total[0, 0]


BIG = ("w_in", "w_br_a", "w_br_b", "w_br_c", "w_o", "w_gu", "w_down")
ROW_SHARDED = ("w_o", "w_down")
SMALL_REPL = ("norm1_g", "norm2_g", "dn_a_log", "dn_dt_bias", "dn_norm_g", "pool_w", "pool_scale", "final_norm_g")
SMALL_SHARD = ("dn_conv_w", "sc_conv_w")
WEIGHTS = ("c_ctx", "w_ada", "b_ada", "norm1_g", "norm2_g", "w_in", "dn_conv_w", "dn_a_log", "dn_dt_bias", "dn_norm_g", "pool_w",
           "pool_scale", "sc_conv_w", "w_br_a", "w_br_b", "w_br_c", "w_o", "w_gu", "w_down", "final_norm_g")
N_CHIPS, N_DEV = 4, 8
COMM_COLS = 1024


def _pack(arrays, rows_multiple, cols=COMM_COLS, dtype=F32):
    flat = jnp.concatenate([a.astype(dtype).reshape(-1) for a in arrays])
    rows = -(-flat.shape[0] // (cols * rows_multiple)) * rows_multiple
    return jnp.pad(flat, (0, rows * cols - flat.shape[0])).reshape(rows, cols)


def _unpack(flat, shapes):
    out, pos = [], 0
    for s in shapes:
        n = int(np.prod(s))
        out.append(flat[pos:pos + n].reshape(s))
        pos += n
    return out


def _join_shards(parts, name):
    _, l, a, b = parts.shape
    if name in ROW_SHARDED:
        return jnp.moveaxis(parts, 0, 1).reshape(l, N_CHIPS * a, b)
    return jnp.moveaxis(parts, 0, 2).reshape(l, a, N_CHIPS * b)


def _cut_shard(full, name, j):
    if name in ROW_SHARDED:
        n = full.shape[1] // N_CHIPS
        return full[:, j * n:(j + 1) * n, :]
    n = full.shape[2] // N_CHIPS
    return full[:, :, j * n:(j + 1) * n]


def _dsilu(x):
    s = _sigmoid(x)
    return s + x * s * (1.0 - s)


def kernel(x, c, ctx, c_ctx, w_ada, b_ada, norm1_g, norm2_g, w_in, dn_conv_w, dn_a_log, dn_dt_bias, dn_norm_g, pool_w, pool_scale, sc_conv_w, w_br_a, w_br_b, w_br_c, w_o, w_gu, w_down, final_norm_g, loss_target, m_c_ctx, m_w_ada, m_b_ada, m_norm1_g, m_norm2_g, m_w_in, m_dn_conv_w, m_dn_a_log, m_dn_dt_bias, m_dn_norm_g, m_pool_w, m_pool_scale, m_sc_conv_w, m_w_br_a, m_w_br_b, m_w_br_c, m_w_o, m_w_gu, m_w_down, m_final_norm_g, v_c_ctx, v_w_ada, v_b_ada, v_norm1_g, v_norm2_g, v_w_in, v_dn_conv_w, v_dn_a_log, v_dn_dt_bias, v_dn_norm_g, v_pool_w, v_pool_scale, v_sc_conv_w, v_w_br_a, v_w_br_b, v_w_br_c, v_w_o, v_w_gu, v_w_down, v_final_norm_g):
    given = dict(locals())
    ix, iy, ic = _place()
    chip, dev = 2 * ix + iy, 4 * ix + 2 * iy + ic
    d = x.shape[-1]
    depth = w_in.shape[0]
    ada_cols = w_ada.shape[2]
    spare = 2 * SUBLANES - N_DEV - 1

    got0 = all_gather8("gather_cond", _pack([c, dn_conv_w, sc_conv_w], SUBLANES)).reshape(N_DEV, -1)
    c_all = got0[:, :d]
    taps = [_unpack(got0[2 * j, d:], [dn_conv_w.shape, sc_conv_w.shape]) for j in range(N_CHIPS)]
    full = {"dn_conv_w": jnp.concatenate([t[0] for t in taps], axis=-1),
            "sc_conv_w": jnp.concatenate([t[1] for t in taps], axis=-1)}
    big_shapes = [given[n].shape for n in BIG]
    mine = _pack([given[n] for n in BIG], 4 * SUBLANES, dtype=MXU_DTYPE)
    mine = mine.reshape(2, mine.shape[0] // 2, COMM_COLS)
    got_half = gather_chips("gather_weights", lax.dynamic_index_in_dim(mine, ic, 0, keepdims=False))
    other_half = swap_sibling("swap_weights", got_half)
    shards = jnp.stack([jnp.where(ic == 0, got_half, other_half), jnp.where(ic == 0, other_half, got_half)], axis=1)
    shards = shards.reshape(N_CHIPS, -1)
    per_chip = [_unpack(shards[j], big_shapes) for j in range(N_CHIPS)]
    mats = {n: _join_shards(jnp.stack([per_chip[j][i] for j in range(N_CHIPS)]), n) for i, n in enumerate(BIG)}
    for n in SMALL_REPL:
        full[n] = given[n]
    pw, scw = pool_scale.shape[1], full["sc_conv_w"].shape[2]
    wts = split_matrices(mats, pw, scw)
    carriers = {k: [jnp.zeros(w.shape, F32) for w in ws] for k, ws in wts.items()}

    cond = jnp.concatenate([c_all, c_ctx[None], jnp.zeros((spare, d), F32)])
    s_cond = _ew("silu_cond", _silu, cond)
    mod_cols = jnp.concatenate([_mm(f"ada{l}_f", s_cond, w_ada[l], NN) for l in range(depth)], axis=1)
    mod_got = all_gather8("gather_mod", mod_cols).reshape(N_DEV, 2 * SUBLANES, depth, ada_cols)
    mod_all = jnp.concatenate([mod_got[2 * j] for j in range(N_CHIPS)], axis=-1) + b_ada[None]
    mod_lat = lax.dynamic_index_in_dim(mod_all, dev, 0, keepdims=False).reshape(depth, 6, d)
    mod_ctx = mod_all[N_DEV].reshape(depth, 6, d)

    loss_local, (g_full, g_mats, grad_x, g_mod_lat, g_mod_ctx) = jax.value_and_grad(local_loss, argnums=(0, 1, 2, 3, 4))(
        full, carriers, x[0], mod_lat, mod_ctx, wts, ctx[0], loss_target[0])
    g_full.update(join_matrix_grads(g_mats, pw, scw))
    loss = lax.psum(loss_local, ("x", "y", "c"))

    dmod_cols = (2 * depth * 6 * d) // SUBLANES
    dmod = all_gather8("gather_dmod", _pack([g_mod_lat, g_mod_ctx], SUBLANES, cols=dmod_cols))
    dmod = dmod.reshape(N_DEV, SUBLANES, dmod_cols)
    dmod_sum, dmod_fold = _sum_devices("reduce_dmod", dmod, fold=True)
    half_rows = SUBLANES // 2
    grad_b_ada = dmod_fold[:half_rows].reshape(depth, 6 * d)
    dctx_sum = dmod_sum[half_rows:].reshape(1, depth, 6 * d)
    d9 = jnp.concatenate([dmod[:, :half_rows].reshape(N_DEV, depth, 6 * d), dctx_sum, jnp.zeros((spare, depth, 6 * d), F32)])
    d9 = lax.dynamic_slice_in_dim(d9, chip * ada_cols, ada_cols, axis=2)
    grad_w_ada = jnp.stack([_mm(f"ada{l}_dw", s_cond, d9[:, l], TN) for l in range(depth)])
    ds_cond = [_mm(f"ada{l}_da", d9[:, l], w_ada[l], NT) for l in range(depth)]
    dsilu_part = _sum_rows("sum_dcond", ds_cond)[N_DEV]

    small_names = SMALL_REPL + SMALL_SHARD
    small_grads = [dsilu_part] + [g_full[n] for n in small_names]
    small_shapes = [a.shape for a in small_grads]
    n_small = sum(int(np.prod(s)) for s in small_shapes)
    cols = -(-n_small // (SUBLANES * LANES)) * LANES
    got2 = all_gather8("gather_small", _pack(small_grads, SUBLANES, cols=cols)).reshape(N_DEV, SUBLANES, cols)
    small_sum = _unpack(_sum_devices("reduce_small", got2).reshape(-1), small_shapes)
    grads = dict(zip(small_names, small_sum[1:]))
    grads["c_ctx"] = _ew("dsilu", lambda g, z: 0.5 * g * _dsilu(z), _row(small_sum[0]), _row(c_ctx)).reshape(-1)
    grads["w_ada"], grads["b_ada"] = grad_w_ada, grad_b_ada
    for n in SMALL_SHARD:
        width = given[n].shape[-1]
        grads[n] = lax.dynamic_slice_in_dim(grads[n], chip * width, width, axis=-1)

    pieces = jnp.stack([_pack([_cut_shard(g_full[n], n, j) for n in BIG], 2 * LANES) for j in range(N_CHIPS)])
    rows_h = pieces.shape[1] // 2
    halves = jnp.moveaxis(pieces.reshape(N_CHIPS, 2, rows_h, COMM_COLS), 1, 0)
    keep = lax.dynamic_index_in_dim(halves, ic, 0, keepdims=False).reshape(N_CHIPS * rows_h, COMM_COLS)
    give = lax.dynamic_index_in_dim(halves, 1 - ic, 0, keepdims=False).reshape(N_CHIPS * rows_h, COMM_COLS)
    pair = _sum_rows("sum_pair", [keep, swap_sibling("swap_halves", give)], out_dtype=MXU_DTYPE)
    pair = pair.reshape(N_CHIPS, rows_h, COMM_COLS)
    got3 = scatter_chips("scatter_pieces", pair)
    own = lax.dynamic_index_in_dim(pair, chip, 0, keepdims=False)
    half = _sum_rows("sum_chips", [own, got3[0], got3[1], got3[2]])
    other = swap_sibling("swap_reduced", half)
    lo = jnp.where(ic == 0, half, other)
    hi = jnp.where(ic == 0, other, half)
    for n, g in zip(BIG, _unpack(jnp.concatenate([lo, hi]).reshape(-1), big_shapes)):
        grads[n] = g

    delta, new_m, new_v = {}, {}, {}
    large = BIG + ("w_ada",)
    for n in large:
        shape = given[n].shape
        flat = [a.reshape(-1, shape[-1]) for a in (given[n], grads[n], given["m_" + n], given["v_" + n])]
        res = adamw("adamw_" + n, *flat)
        delta[n], new_m[n], new_v[n] = (a.reshape(shape) for a in res)
    rest = [n for n in WEIGHTS if n not in large]
    rest_shapes = [given[n].shape for n in rest]
    packed = [_pack([src[pre + n] for n in rest], SUBLANES, cols=LANES)
              for src, pre in ((given, ""), (grads, ""), (given, "m_"), (given, "v_"))]
    for res, o in zip((delta, new_m, new_v), adamw("adamw_small", *packed)):
        for n, a in zip(rest, _unpack(o.reshape(-1), rest_shapes)):
            res[n] = a
    return (loss, grad_x[None], *[grads[n] for n in WEIGHTS], *[delta[n] for n in WEIGHTS],
            *[new_m[n] for n in WEIGHTS], *[new_v[n] for n in WEIGHTS])
```

```python
import functools
import math

import numpy as np
import jax
import jax.numpy as jnp
from jax import lax
from jax.experimental import pallas as pl
from jax.experimental.pallas import tpu as pltpu

F32 = jnp.float32
MXU_DTYPE = jnp.bfloat16
HIGHEST = lax.Precision.HIGHEST

DN_HEADS = 4
HEAD_DIM = 128
DN_WIDTH = DN_HEADS * HEAD_DIM
DN_CHUNK = 64
GRID_W = 64
EPS = 1e-6
POOL_WINDOWS = (2, 4, 8, 16)
N_DIRHEAD = 2 * DN_HEADS
ADAM_LR, ADAM_B1, ADAM_B2, ADAM_EPS, ADAM_WD, ADAM_STEP = 0.001, 0.9, 0.999, 1e-08, 0.01, 10

LANES = 128
SUBLANES = 8
ROW_TILE = 256
VMEM_LIMIT = 56 * 1024 * 1024

MESH_ID = pl.DeviceIdType.MESH
NN, NT, TN = ((1,), (0,)), ((1,), (1,)), ((0,), (0,))


def _params(sem=None):
    return pltpu.CompilerParams(dimension_semantics=sem, vmem_limit_bytes=VMEM_LIMIT)


def _pick(n, cands):
    for c in cands:
        if c <= n and n % c == 0:
            return c
    return n


def _mm(name, a, b, dims, out_dtype=F32):
    if dims == NN:
        (m, kk), (_, n) = a.shape, b.shape
    elif dims == NT:
        (m, kk), (n, _) = a.shape, b.shape
    else:
        (kk, m), (_, n) = a.shape, b.shape
    tm = _pick(m, (768, 1024, 512, 256, 128, 64, 32, 16, 8))
    tn = _pick(n, (1024, 1408, 1664, 768, 896, 512, 256, 128))
    tk = _pick(kk, (1024, 1408, 768, 512, 256, 128))
    gi, gj, gl = m // tm, n // tn, kk // tk
    if dims == NN:
        a_spec = pl.BlockSpec((tm, tk), lambda i, j, l: (i, l))
        b_spec = pl.BlockSpec((tk, tn), lambda i, j, l: (l, j))
    elif dims == NT:
        a_spec = pl.BlockSpec((tm, tk), lambda i, j, l: (i, l))
        b_spec = pl.BlockSpec((tn, tk), lambda i, j, l: (j, l))
    else:
        a_spec = pl.BlockSpec((tk, tm), lambda i, j, l: (l, i))
        b_spec = pl.BlockSpec((tk, tn), lambda i, j, l: (l, j))
    direct = gl == 1
    use_acc = (not direct) and out_dtype != F32

    def body(a_ref, b_ref, o_ref, *scratch):
        part = lax.dot_general(a_ref[...].astype(MXU_DTYPE), b_ref[...].astype(MXU_DTYPE), (dims, ((), ())),
                               preferred_element_type=F32)
        if direct:
            o_ref[...] = part.astype(out_dtype)
            return
        acc = scratch[0] if use_acc else o_ref
        l = pl.program_id(2)

        @pl.when(l == 0)
        def _():
            acc[...] = part

        @pl.when(l > 0)
        def _():
            acc[...] += part

        if use_acc:
            @pl.when(l == gl - 1)
            def _():
                o_ref[...] = acc[...].astype(out_dtype)

    return pl.pallas_call(
        body, grid=(gi, gj, gl), in_specs=[a_spec, b_spec],
        out_specs=pl.BlockSpec((tm, tn), lambda i, j, l: (i, j)),
        out_shape=jax.ShapeDtypeStruct((m, n), out_dtype),
        scratch_shapes=[pltpu.VMEM((tm, tn), F32)] if use_acc else [],
        compiler_params=_params(("parallel", "parallel", "arbitrary")), name=name,
    )(a, b)


def mm(name, a, w, carrier):
    @jax.custom_vjp
    def f(a, w, carrier):
        return _mm(name + "_f", a, w, NN, out_dtype=MXU_DTYPE)

    def fwd(a, w, carrier):
        return f(a, w, carrier), (a, w)

    def bwd(res, dc):
        a, w = res
        da = _mm(name + "_da", dc, w, NT, out_dtype=a.dtype)
        dw = _mm(name + "_dw", a, dc, TN)
        return da, None, dw

    f.defvjp(fwd, bwd)
    return f(a, w, carrier)


def _mm_nt_sum(name, dcs, ws, out_dtype):
    m, kk = dcs[0].shape[0], ws[0].shape[0]
    tm = _pick(m, (768, 1024, 512, 256, 128, 64, 32, 16, 8))
    tns = [_pick(w.shape[1], (1664, 1536, 1024, 768, 896, 512, 256, 128)) for w in ws]
    counts = [w.shape[1] // tn for w, tn in zip(ws, tns)]
    starts = [sum(counts[:g]) for g in range(len(ws))]
    steps = sum(counts)

    def col(g):
        return lambda i, t: jnp.clip(t - starts[g], 0, counts[g] - 1)

    def body(*refs):
        dc_refs, w_refs, o_ref, acc = refs[:len(ws)], refs[len(ws):2 * len(ws)], refs[-2], refs[-1]
        t = pl.program_id(1)

        @pl.when(t == 0)
        def _():
            acc[...] = jnp.zeros_like(acc)

        for g in range(len(ws)):
            @pl.when(jnp.logical_and(t >= starts[g], t < starts[g] + counts[g]))
            def _():
                acc[...] += lax.dot_general(dc_refs[g][...].astype(MXU_DTYPE), w_refs[g][...].astype(MXU_DTYPE),
                                            (NT, ((), ())), preferred_element_type=F32)

        @pl.when(t == steps - 1)
        def _():
            o_ref[...] = acc[...].astype(out_dtype)

    dc_specs = [pl.BlockSpec((tm, tns[g]), (lambda c: lambda i, t: (i, c(i, t)))(col(g))) for g in range(len(ws))]
    w_specs = [pl.BlockSpec((kk, tns[g]), (lambda c: lambda i, t: (0, c(i, t)))(col(g))) for g in range(len(ws))]
    return pl.pallas_call(
        body, grid=(m // tm, steps), in_specs=dc_specs + w_specs,
        out_specs=pl.BlockSpec((tm, kk), lambda i, t: (i, 0)),
        out_shape=jax.ShapeDtypeStruct((m, kk), out_dtype),
        scratch_shapes=[pltpu.VMEM((tm, kk), F32)],
        compiler_params=_params(("parallel", "arbitrary")), name=name,
    )(*dcs, *ws)


def mm_fanout(name, a, ws, carriers):
    n = len(ws)

    @jax.custom_vjp
    def f(a, ws, carriers):
        return tuple(_mm(f"{name}{g}_f", a, ws[g], NN, out_dtype=MXU_DTYPE) for g in range(n))

    def fwd(a, ws, carriers):
        return f(a, ws, carriers), (a, ws)

    def bwd(res, dcs):
        a, ws = res
        da = _mm_nt_sum(name + "_da", list(dcs), list(ws), a.dtype)
        dws = tuple(_mm(f"{name}{g}_dw", a, dcs[g], TN) for g in range(n))
        return da, None, dws

    f.defvjp(fwd, bwd)
    return f(a, tuple(ws), tuple(carriers))


def _split(x, parts):
    w = x.shape[-1] // parts
    return [x[:, k * w:(k + 1) * w] for k in range(parts)]


def _cat(xs):
    return xs[0] if len(xs) == 1 else jnp.concatenate(xs, axis=-1)


def _shift_rows(x, prev_ref, next_ref, i, ctx_tiles, nt):
    tr = x.shape[0]
    rid = lax.broadcasted_iota(jnp.int32, x.shape, 0)
    first = jnp.logical_or(i == 0, i == ctx_tiles)
    last = jnp.logical_or(i == ctx_tiles - 1, i == nt - 1)
    prow = jnp.where(first, 0.0, prev_ref[SUBLANES - 1:SUBLANES, :].astype(F32))
    nrow = jnp.where(last, 0.0, next_ref[0:1, :].astype(F32))
    xm = jnp.where(rid == 0, prow, pltpu.roll(x, 1, 0))
    xp = jnp.where(rid == tr - 1, nrow, pltpu.roll(x, tr - 1, 0))
    return xm, xp


def rowwise(name, fn, rows, vecs, *, halo=(), seg=(), parts_r=None, parts_v=None, outs=(), reds=(), ctx_tiles=1, skip=None):
    nr, nv = len(rows), len(vecs)
    halo = tuple(halo) or (False,) * nr
    seg = tuple(seg) or (False,) * nv
    skip = tuple(skip or (0,) * nr)
    parts_r = tuple(parts_r or (1,) * nr)
    parts_v = tuple(parts_v or (1,) * nv)
    r_total = rows[0].shape[0]
    tr = ROW_TILE
    nt = r_total // tr
    assert r_total % tr == 0 and (ctx_tiles > 0 or not any(seg)) and not any(h and s for h, s in zip(halo, skip))

    def tile_map(i):
        return lambda t: (jnp.maximum(t - skip[i], 0), 0)

    def row_specs():
        sp = []
        for i, r in enumerate(rows):
            w = r.shape[1]
            sp.append(pl.BlockSpec((tr, w), tile_map(i)))
            if halo[i]:
                k = tr // SUBLANES
                sp.append(pl.BlockSpec((SUBLANES, w), lambda t: (jnp.maximum(t * k - 1, 0), 0)))
                sp.append(pl.BlockSpec((SUBLANES, w), lambda t: (jnp.minimum((t + 1) * k, nt * k - 1), 0)))
        return sp

    def vec_spec(j):
        w = vecs[j].shape[-1]
        if seg[j]:
            return pl.BlockSpec((None, 1, w), lambda t: (jnp.where(t >= ctx_tiles, 1, 0), 0, 0))
        return pl.BlockSpec((1, w), lambda t: (0, 0))

    def row_args(rv):
        a = []
        for i in range(nr):
            a += [rv[i]] * 3 if halo[i] else [rv[i]]
        return a

    def load(refs, t):
        pos, rp = 0, []
        for i in range(nr):
            x = refs[pos][...].astype(F32)
            if halo[i]:
                xm, xp = _shift_rows(x, refs[pos + 1], refs[pos + 2], t, ctx_tiles, nt)
                rp.append(list(zip(_split(x, parts_r[i]), _split(xm, parts_r[i]), _split(xp, parts_r[i]))))
                pos += 3
            else:
                rp.append(_split(x, parts_r[i]))
                pos += 1
        vp = []
        for j in range(nv):
            vp.append(_split(refs[pos][...].astype(F32), parts_v[j]))
            pos += 1
        return rp, vp, refs[pos:]

    n_out, n_red = len(outs), len(reds)

    def fwd_call(*rv):
        def body(*refs):
            t = pl.program_id(0)
            rp, vp, rest = load(refs, t)
            o_parts, r_parts = fn(rp, vp)
            for k in range(n_out):
                rest[k][...] = _cat(o_parts[k]).astype(outs[k][1])
            for k in range(n_red):
                ref = rest[n_out + k]

                @pl.when(t == 0)
                def _():
                    ref[...] = r_parts[k]

                @pl.when(t > 0)
                def _():
                    ref[...] += r_parts[k]

        res = pl.pallas_call(
            body, grid=(nt,),
            in_specs=row_specs() + [vec_spec(j) for j in range(nv)],
            out_specs=[pl.BlockSpec((tr, o[0]), lambda t: (t, 0)) for o in outs]
            + [pl.BlockSpec((1, w), lambda t: (0, 0)) for w in reds],
            out_shape=[jax.ShapeDtypeStruct((r_total, o[0]), o[1]) for o in outs]
            + [jax.ShapeDtypeStruct((1, w), F32) for w in reds],
            compiler_params=_params(("arbitrary",)), name=name + "_f",
        )(*row_args(rv), *rv[nr:])
        return tuple(res)

    def bwd_call(rv, cts):
        def body(*refs):
            t = pl.program_id(0)
            rp, vp, rest = load(refs, t)
            ct_o = [_split(rest[k][...].astype(F32), outs[k][2]) for k in range(n_out)]
            ct_r = [rest[n_out + k][...] for k in range(n_red)]
            rest = rest[n_out + n_red:]
            _, vjp = jax.vjp(fn, rp, vp)
            d_rp, d_vp = vjp((ct_o, ct_r))
            pos = 0
            for i in range(nr):
                if halo[i]:
                    for c in range(3):
                        rest[pos + c][...] = _cat([p[c] for p in d_rp[i]])
                    pos += 3
                else:
                    rest[pos][...] = _cat(d_rp[i]).astype(rows[i].dtype)
                    pos += 1
            for j in range(nv):
                ref, val = rest[pos + j], _cat(d_vp[j])
                start = jnp.logical_or(t == 0, t == ctx_tiles) if seg[j] else t == 0

                @pl.when(start)
                def _():
                    ref[...] = val

                @pl.when(jnp.logical_not(start))
                def _():
                    ref[...] += val

        d_row_specs, d_row_shapes = [], []
        for i, r in enumerate(rows):
            w = r.shape[1]
            for _ in range(3 if halo[i] else 1):
                d_row_specs.append(pl.BlockSpec((tr, w), tile_map(i)))
                d_row_shapes.append(jax.ShapeDtypeStruct(r.shape, F32 if halo[i] else r.dtype))
        res = pl.pallas_call(
            body, grid=(nt,),
            in_specs=row_specs() + [vec_spec(j) for j in range(nv)]
            + [pl.BlockSpec((tr, o[0]), lambda t: (t, 0)) for o in outs]
            + [pl.BlockSpec((1, w), lambda t: (0, 0)) for w in reds],
            out_specs=d_row_specs + [vec_spec(j) for j in range(nv)],
            out_shape=d_row_shapes + [jax.ShapeDtypeStruct(v.shape, F32) for v in vecs],
            compiler_params=_params(("arbitrary",)), name=name + "_b",
        )(*row_args(rv), *rv[nr:], *cts)
        d_rows, pos = [], 0
        for i in range(nr):
            if halo[i]:
                d_rows.append(_unshift(res[pos], res[pos + 1], res[pos + 2], ctx_tiles * tr).astype(rows[i].dtype))
                pos += 3
            else:
                d_rows.append(res[pos])
                pos += 1
        return tuple(d_rows) + tuple(res[pos:])

    @jax.custom_vjp
    def f(*rv):
        return fwd_call(*rv)

    f.defvjp(lambda *rv: (fwd_call(*rv), rv), lambda rv, cts: bwd_call(rv, cts))
    return f(*rows, *vecs)


def _unshift(d, dm, dp, ctx_rows):
    r = d.shape[0]
    t = lax.broadcasted_iota(jnp.int32, (r, 1), 0)
    zero = jnp.zeros((1, d.shape[1]), d.dtype)
    from_m = jnp.concatenate([dm[1:], zero], axis=0)
    from_p = jnp.concatenate([zero, dp[:-1]], axis=0)
    from_m = jnp.where(t == ctx_rows - 1, 0.0, from_m)
    from_p = jnp.where(t == ctx_rows, 0.0, from_p)
    return d + from_m + from_p


def _sigmoid(x):
    return 0.5 * (jnp.tanh(0.5 * x) + 1.0)


def _silu(x):
    return x * _sigmoid(x)


def _softplus(x):
    return jnp.maximum(x, 0.0) + jnp.log(1.0 + jnp.exp(-jnp.abs(x)))


def _rms(x, g):
    return x * lax.rsqrt(jnp.mean(x * x, axis=-1, keepdims=True) + EPS) * g


def _fn_modulate(r, v):
    (x,), (g,), (sh,), (sc,) = r[0], v[0], v[1], v[2]
    return [[_rms(x, g) * (1.0 + sc) + sh]], []


def _fn_resmod(r, v):
    (x,), (y,) = r
    (gate,), (g,), (sh,), (sc,) = v
    xn = x + gate * y
    return [[xn], [_rms(xn, g) * (1.0 + sc) + sh]], []


def _fn_final(r, v):
    (x,), (y,), (tgt,) = r
    (gate,), (g,), (counts,) = v
    err = _rms(x + gate * y, g) - tgt
    row_loss = jnp.mean(err * err, axis=-1, keepdims=True)
    total = 0.5 * jnp.sum(row_loss, axis=0, keepdims=True)
    return [], [total * counts]


def _fn_dnprep(r, v):
    qkv, (ab,) = r
    w0, w1, w2, (alog,), (dtb,) = v
    out = [[], [], []]
    for n, (x, xm, xp) in enumerate(qkv):
        which = n // DN_HEADS
        y = _silu(xm * w0[n] + x * w1[n] + xp * w2[n])
        if which < 2:
            y = y * lax.rsqrt(jnp.sum(y * y, axis=-1, keepdims=True) + EPS)
        if which == 0:
            y = y * (HEAD_DIM ** -0.5)
        out[which].append(y)
    lane = lax.broadcasted_iota(jnp.int32, ab.shape, 1)
    g = -jnp.exp(alog) * _softplus(ab + dtb)
    gb = jnp.where(lane < N_DIRHEAD, g, jnp.where(lane < 2 * N_DIRHEAD, _sigmoid(ab), 0.0))
    return out + [[gb]], []


def _fn_dnpost(r, v):
    of, ob, z = r
    (g,) = v[0]
    return [[_rms(a + b, g) * _silu(c) for a, b, c in zip(of, ob, z)]], []


def _fn_sub(r, v):
    return [[r[0][0] - r[1][0]]], []


def _fn_scale(r, v):
    return [[r[0][0] * v[0][0]]], []


def _fn_shortconv(r, v):
    (xin, gb, gc), = r
    (w0,), (w1,), (w2,) = v
    u, um, up = (gc[k] * xin[k] for k in range(3))
    return [[gb[0] * (um * w0 + u * w1 + up * w2)]], []


def _fn_merge(r, v):
    gates, (ya,), (yb,), (yc,) = r
    return [[_sigmoid(gates[0]) * ya + _sigmoid(gates[1]) * yb + _sigmoid(gates[2]) * yc]], []


def _fn_swiglu(r, v):
    gate, up = r[0]
    return [[_silu(gate) * up]], []


def _dot_hi(a, b):
    return lax.dot_general(a, b, (NN, ((), ())), precision=HIGHEST, preferred_element_type=F32)


def _bdot(a, b, dims):
    (ca,), (cb,) = dims
    return lax.dot_general(a.astype(MXU_DTYPE), b.astype(MXU_DTYPE), (((ca + 1,), (cb + 1,)), ((0,), (0,))),
                           preferred_element_type=F32)


def _hi_lo(a):
    hi = a.astype(MXU_DTYPE)
    return hi, (a - hi.astype(F32)).astype(MXU_DTYPE)


def _bdot3_raw(a, b, dims):
    (ah, al), (bh, bl) = _hi_lo(a), _hi_lo(b)
    return _bdot(ah, bh, dims) + (_bdot(ah, bl, dims) + _bdot(al, bh, dims))


@jax.custom_vjp
def _bdot3(a, b):
    return _bdot3_raw(a, b, NN)


_bdot3.defvjp(lambda a, b: (_bdot3_raw(a, b, NN), (a, b)),
              lambda res, ct: (_bdot3_raw(ct, res[1], NT), _bdot3_raw(res[0], ct, TN)))


def _inv_doubling(a):
    c = a.shape[-1]
    ii, jj = (lax.broadcasted_iota(jnp.int32, a.shape, d) for d in (1, 2))
    t = jnp.where(ii == jj, 1.0, 0.0) - a
    p = a
    for _ in range(int(math.log2(c)) - 1):
        p = _bdot3(p, p)
        t = t + _bdot3(t, p)
    return t


@jax.custom_vjp
def _inv_unit_triangular(a):
    return _inv_doubling(a)


def _inv_fwd(a):
    t = _inv_doubling(a)
    return t, t


_inv_unit_triangular.defvjp(_inv_fwd, lambda t, ct: (-_bdot3_raw(_bdot3_raw(t, ct, TN), t, NT),))


def _dn_pre(q4, k4, v4, gb):
    nb, c = N_DIRHEAD, q4.shape[1]
    q, k, v = (jnp.concatenate([a, a], axis=0) for a in (q4, k4, v4))
    lane = lax.broadcasted_iota(jnp.int32, gb.shape, 1)
    col = lambda j: jnp.sum(jnp.where(lane == j, gb, 0.0), axis=1, keepdims=True)
    g_col = jnp.concatenate([col(j)[None] for j in range(nb)], axis=0)
    b_col = jnp.concatenate([col(nb + j)[None] for j in range(nb)], axis=0)
    bi, ii, jj = (lax.broadcasted_iota(jnp.int32, (nb, c, c), a) for a in range(3))
    ahead = jnp.where(bi >= DN_HEADS, jj - ii, ii - jj)
    incl, strict, eye = ahead >= 0, ahead > 0, ahead == 0
    g_row = jnp.sum(jnp.where(eye, g_col, 0.0), axis=1, keepdims=True)
    gc_col = jnp.sum(jnp.where(incl, g_row, 0.0), axis=2, keepdims=True)
    gc_row = jnp.sum(jnp.where(ahead <= 0, g_col, 0.0), axis=1, keepdims=True)
    decay = jnp.where(incl, jnp.exp(jnp.where(incl, gc_col - gc_row, 0.0)), 0.0)
    kb = k * b_col
    a = _bdot(kb, k, NT) * jnp.where(strict, decay, 0.0)
    t = _inv_unit_triangular(a)
    e_gc = jnp.exp(gc_col)
    u = _bdot3(t, v * b_col)
    w = _bdot3(t, kb * e_gc)
    g_last = jnp.sum(g_col, axis=1, keepdims=True)
    k_state = k * jnp.exp(g_last - gc_col)
    a_qk = _bdot(q, k, NT) * decay
    return u, w, q * e_gc, k_state, a_qk, jnp.broadcast_to(jnp.exp(g_last), (nb, 1, LANES))


def _dn_step(s, u, w, qd, ks, aqk, gl):
    v_new = u - _bdot(w, s, NN)
    o = _bdot(qd, s, NN) + _bdot(aqk, v_new, NN)
    return s * gl[:, :, :1] + _bdot(ks, v_new, TN), o


def _heads(x):
    return jnp.concatenate([x[None, :, h * HEAD_DIM:(h + 1) * HEAD_DIM] for h in range(DN_HEADS)], axis=0)


def _unheads(x):
    return jnp.concatenate([x[h] for h in range(x.shape[0])], axis=-1)


def _dn_rev(t, nc, n):
    return jnp.where(t < nc, nc - 1 - t, n - 1 - (t - nc))


def _pre_shapes(n):
    c, d, h = DN_CHUNK, HEAD_DIM, DN_HEADS
    shapes = [(n, h, c, d)] * 8 + [(n, h, c, c)] * 2 + [(n, h, 1, LANES)] * 2
    return shapes, [pl.BlockSpec((None,) + s[1:], lambda t: (t, 0, 0, 0)) for s in shapes]


def _dn_pre_fwd(name, q, k, v, gb):
    n = q.shape[0] // DN_CHUNK
    shapes, specs = _pre_shapes(n)

    def body(q_ref, k_ref, v_ref, g_ref, *o_refs):
        res = _dn_pre(_heads(q_ref[...]), _heads(k_ref[...]), _heads(v_ref[...]), g_ref[...])
        for i, r in enumerate(res):
            o_refs[2 * i][...] = r[:DN_HEADS]
            o_refs[2 * i + 1][...] = r[DN_HEADS:]

    wide = pl.BlockSpec((DN_CHUNK, DN_HEADS * HEAD_DIM), lambda t: (t, 0))
    narrow = pl.BlockSpec((DN_CHUNK, LANES), lambda t: (t, 0))
    return pl.pallas_call(body, grid=(n,), in_specs=[wide] * 3 + [narrow], out_specs=specs,
                          out_shape=[jax.ShapeDtypeStruct(s, F32) for s in shapes],
                          compiler_params=_params(("parallel",)), name=name + "_pre_f")(q, k, v, gb)


def _dn_pre_bwd(name, q, k, v, gb, cts):
    n = q.shape[0] // DN_CHUNK
    _, specs = _pre_shapes(n)
    n_ct = len(specs)

    def body(q_ref, k_ref, v_ref, g_ref, *refs):
        ct = tuple(jnp.concatenate([refs[i][...], refs[i + 1][...]], axis=0) for i in range(0, n_ct, 2))
        _, vjp = jax.vjp(_dn_pre, _heads(q_ref[...]), _heads(k_ref[...]), _heads(v_ref[...]), g_ref[...])
        dq, dk, dv, dg = vjp(ct)
        for r, val in zip(refs[n_ct:], (_unheads(dq), _unheads(dk), _unheads(dv), dg)):
            r[...] = val

    wide = pl.BlockSpec((DN_CHUNK, DN_HEADS * HEAD_DIM), lambda t: (t, 0))
    narrow = pl.BlockSpec((DN_CHUNK, LANES), lambda t: (t, 0))
    return pl.pallas_call(body, grid=(n,), in_specs=[wide] * 3 + [narrow] + specs, out_specs=[wide] * 3 + [narrow],
                          out_shape=[jax.ShapeDtypeStruct(q.shape, F32)] * 3 + [jax.ShapeDtypeStruct(gb.shape, F32)],
                          compiler_params=_params(("parallel",)), name=name + "_pre_b")(q, k, v, gb, *cts)


def _scan_specs(pre_shapes, fw, bw):
    maps = (lambda t: (fw(t), 0, 0, 0), lambda t: (bw(t), 0, 0, 0))
    return [pl.BlockSpec((None,) + s[1:], maps[i % 2]) for i, s in enumerate(pre_shapes)]


def _dn_scan_fwd(name, pre, nc):
    n = pre[0].shape[0]
    r = n * DN_CHUNK
    hd, nh, nb = HEAD_DIM, DN_HEADS, N_DIRHEAD
    shapes, _ = _pre_shapes(n)
    n_in = len(shapes)

    def body(*refs):
        ins, (of_ref, ob_ref, sf_ref, sb_ref, s_scr) = refs[:n_in], refs[n_in:]
        t = pl.program_id(0)

        @pl.when(t == 0)
        def _():
            s_scr[...] = jnp.zeros_like(s_scr)

        s = s_scr[...]
        sf_ref[...] = s[:nh]
        sb_ref[...] = s[nh:]
        args = [jnp.concatenate([ins[i][...], ins[i + 1][...]], axis=0) for i in range(0, n_in, 2)]
        s2, o = _dn_step(s, *args)
        s_scr[...] = s2
        of_ref[...] = _unheads(o[:nh])
        ob_ref[...] = _unheads(o[nh:])

    fw = lambda t: t
    bw = lambda t: _dn_rev(t, nc, n)
    wide, st = (DN_CHUNK, nh * hd), (None, nh, hd, hd)
    return pl.pallas_call(
        body, grid=(n,), in_specs=_scan_specs(shapes, fw, bw),
        out_specs=[pl.BlockSpec(wide, lambda t: (fw(t), 0)), pl.BlockSpec(wide, lambda t: (bw(t), 0)),
                   pl.BlockSpec(st, lambda t: (fw(t), 0, 0, 0)), pl.BlockSpec(st, lambda t: (bw(t), 0, 0, 0))],
        out_shape=[jax.ShapeDtypeStruct((r, nh * hd), F32)] * 2 + [jax.ShapeDtypeStruct((n, nh, hd, hd), F32)] * 2,
        scratch_shapes=[pltpu.VMEM((nb, hd, hd), F32)],
        compiler_params=_params(("arbitrary",)), name=name + "_scan_f",
    )(*pre)


def _dn_scan_bwd(name, pre, sall_f, sall_b, do_f, do_b, nc):
    n = pre[0].shape[0]
    hd, nh, nb = HEAD_DIM, DN_HEADS, N_DIRHEAD
    shapes, _ = _pre_shapes(n)
    n_in = len(shapes)

    def body(*refs):
        ins, (sf_ref, sb_ref, dof_ref, dob_ref) = refs[:n_in], refs[n_in:n_in + 4]
        outs, ds_scr = refs[n_in + 4:2 * n_in + 4], refs[2 * n_in + 4]
        t = pl.program_id(0)

        @pl.when(t == 0)
        def _():
            ds_scr[...] = jnp.zeros_like(ds_scr)

        args = [jnp.concatenate([ins[i][...], ins[i + 1][...]], axis=0) for i in range(0, n_in, 2)]
        s = jnp.concatenate([sf_ref[...], sb_ref[...]], axis=0)
        do = jnp.concatenate([_heads(dof_ref[...]), _heads(dob_ref[...])], axis=0)
        _, vjp = jax.vjp(_dn_step, s, *args)
        cts = vjp((ds_scr[...], do))
        ds_scr[...] = cts[0]
        for i, ct in enumerate(cts[1:]):
            outs[2 * i][...] = ct[:nh]
            outs[2 * i + 1][...] = ct[nh:]

    fw = lambda t: n - 1 - t
    bw = lambda t: _dn_rev(n - 1 - t, nc, n)
    wide, st = (DN_CHUNK, nh * hd), (None, nh, hd, hd)
    return tuple(pl.pallas_call(
        body, grid=(n,),
        in_specs=_scan_specs(shapes, fw, bw)
        + [pl.BlockSpec(st, lambda t: (fw(t), 0, 0, 0)), pl.BlockSpec(st, lambda t: (bw(t), 0, 0, 0)),
           pl.BlockSpec(wide, lambda t: (fw(t), 0)), pl.BlockSpec(wide, lambda t: (bw(t), 0))],
        out_specs=_scan_specs(shapes, fw, bw), out_shape=[jax.ShapeDtypeStruct(s, F32) for s in shapes],
        scratch_shapes=[pltpu.VMEM((nb, hd, hd), F32)],
        compiler_params=_params(("arbitrary",)), name=name + "_scan_b",
    )(*pre, sall_f, sall_b, do_f, do_b))


def deltanet(name, q, k, v, gb, ctx_rows):
    nc = ctx_rows // DN_CHUNK

    @jax.custom_vjp
    def pre(q, k, v, gb):
        return tuple(_dn_pre_fwd(name, q, k, v, gb))

    pre.defvjp(lambda *a: (tuple(_dn_pre_fwd(name, *a)), a), lambda a, cts: tuple(_dn_pre_bwd(name, *a, cts)))

    @jax.custom_vjp
    def scan(*ops):
        return tuple(_dn_scan_fwd(name, ops, nc)[:2])

    def scan_fwd(*ops):
        of, ob, sf, sb = _dn_scan_fwd(name, ops, nc)
        return (of, ob), (ops, sf, sb)

    scan.defvjp(scan_fwd, lambda res, cts: _dn_scan_bwd(name, res[0], res[1], res[2], cts[0], cts[1], nc))
    return scan(*pre(q, k, v, gb))


def _box_matrix(l, w):
    lo, hi = w // 2, w - 1 - w // 2
    pos = np.arange(l)
    start, end = np.clip(pos - lo, 0, l), np.clip(pos + hi + 1, 0, l)
    col = np.arange(l)[None, :]
    return ((col >= start[:, None]) & (col < end[:, None])) / (end - start)[:, None].astype(np.float64)


def _pool_matrices(ctx_rows, grid_rows):
    assert ctx_rows == ROW_TILE and ROW_TILE % GRID_W == 0
    ctx = np.stack([_box_matrix(ctx_rows, w) for w in POOL_WINDOWS])
    cols = np.stack([np.kron(np.eye(ROW_TILE // GRID_W), _box_matrix(GRID_W, w)) for w in POOL_WINDOWS])
    rows = np.stack([_box_matrix(grid_rows, w) for w in POOL_WINDOWS])[None]
    return np.stack([ctx, cols]).astype(np.float32), rows.astype(np.float32)


def _pool_apply(name, x, mats, group_w, seg_tiles, lane_tile):
    r, w = x.shape
    b = mats.shape[-1]
    ng = mats.shape[1]
    period = ng * group_w

    def body(x_ref, m_ref, o_ref):
        xv = x_ref[...].astype(F32)
        lane = lax.broadcasted_iota(jnp.int32, xv.shape, 1)
        grp = (lane % period) // group_w
        acc = jnp.zeros_like(xv)
        for g in range(ng):
            acc = acc + jnp.where(grp == g, _dot_hi(m_ref[g], xv), 0.0)
        o_ref[...] = acc

    return pl.pallas_call(
        body, grid=(r // b, w // lane_tile),
        in_specs=[pl.BlockSpec((b, lane_tile), lambda i, j: (i, j)),
                  pl.BlockSpec((None, ng, b, b), lambda i, j: (jnp.where(i >= seg_tiles, 1, 0) if mats.shape[0] > 1 else 0, 0, 0, 0))],
        out_specs=pl.BlockSpec((b, lane_tile), lambda i, j: (i, j)),
        out_shape=jax.ShapeDtypeStruct(x.shape, F32),
        compiler_params=_params(("parallel", "parallel")), name=name,
    )(x, mats)


def pool_means(name, u, ctx_rows):
    r, pw = u.shape
    grid_rows = (r - ctx_rows) // GRID_W
    gw = pw // len(POOL_WINDOWS)
    m1, m2 = _pool_matrices(ctx_rows, grid_rows)
    lane_tile = min(2048, GRID_W * pw)

    def apply(x, a1, a2, tag):
        y = _pool_apply(name + tag + "1", x, jnp.asarray(a1), gw, ctx_rows // ROW_TILE, pw)
        lat = y[ctx_rows:].reshape(grid_rows, GRID_W * pw)
        lat = _pool_apply(name + tag + "2", lat, jnp.asarray(a2), gw, 0, lane_tile)
        return jnp.concatenate([y[:ctx_rows], lat.reshape(r - ctx_rows, pw)], axis=0)

    @jax.custom_vjp
    def f(u):
        return apply(u, m1, m2, "_f")

    tr = lambda m: np.ascontiguousarray(np.swapaxes(m, -1, -2))
    f.defvjp(lambda u: (apply(u, m1, m2, "_f"), None), lambda _, ct: (apply(ct, tr(m1), tr(m2), "_b").astype(u.dtype),))
    return f(u)


def _ew(name, fn, *xs, n_out=1):
    shapes = jax.eval_shape(lambda *a: fn(*a), *xs)
    shapes = shapes if isinstance(shapes, (tuple, list)) else (shapes,)

    def body(*refs):
        res = fn(*[r[...] for r in refs[:len(xs)]])
        res = res if isinstance(res, (tuple, list)) else (res,)
        for r, o in zip(res, refs[len(xs):]):
            o[...] = r

    out = pl.pallas_call(body, out_shape=[jax.ShapeDtypeStruct(s.shape, s.dtype) for s in shapes],
                         compiler_params=_params(), name=name)(*xs)
    return out[0] if len(shapes) == 1 else tuple(out)


def _adamw_math(w, g, m, v):
    m2 = ADAM_B1 * m + (1.0 - ADAM_B1) * g
    v2 = ADAM_B2 * v + (1.0 - ADAM_B2) * (g * g)
    m_hat = m2 / (1.0 - ADAM_B1 ** ADAM_STEP)
    v_hat = v2 / (1.0 - ADAM_B2 ** ADAM_STEP)
    delta = -ADAM_LR * (m_hat / (jnp.sqrt(v_hat) + ADAM_EPS) + ADAM_WD * w)
    return delta, m2, v2


def adamw(name, w, g, m, v):
    r, c = w.shape
    tr = _pick(r, (256, 128, 64, 32, 16, 8))

    def body(w_ref, g_ref, m_ref, v_ref, d_ref, m2_ref, v2_ref):
        d_ref[...], m2_ref[...], v2_ref[...] = _adamw_math(w_ref[...], g_ref[...], m_ref[...], v_ref[...])

    spec = pl.BlockSpec((tr, c), lambda i: (i, 0))
    return pl.pallas_call(body, grid=(r // tr,), in_specs=[spec] * 4, out_specs=[spec] * 3,
                          out_shape=[jax.ShapeDtypeStruct(w.shape, F32)] * 3,
                          compiler_params=_params(("parallel",)), name=name)(w, g, m, v)


def _sum_rows(name, xs, out_dtype=F32):
    r, c = xs[0].shape
    tr = _pick(r, (512, 256, 128, 64, 32, 16, 8))

    def body(*refs):
        acc = refs[0][...].astype(F32)
        for ref in refs[1:-1]:
            acc = acc + ref[...].astype(F32)
        refs[-1][...] = acc.astype(out_dtype)

    spec = pl.BlockSpec((tr, c), lambda i: (i, 0))
    return pl.pallas_call(body, grid=(r // tr,), in_specs=[spec] * len(xs), out_specs=spec,
                          out_shape=jax.ShapeDtypeStruct((r, c), out_dtype),
                          compiler_params=_params(("parallel",)), name=name)(*xs)


def _place():
    return lax.axis_index("x"), lax.axis_index("y"), lax.axis_index("c")


def _chip_peers(x, y, c):
    return [(1 - x, y, c), (x, 1 - y, c), (1 - x, 1 - y, c)]


def all_gather8(name, block):
    m_per, n = block.shape

    def body(x_ref, out_ref, send_sems, recv_sems, local_sem):
        x, y, c = _place()
        me, sibling = (x, y, c), (x, y, 1 - c)
        chips = [(1 - x, y), (x, 1 - y), (1 - x, 1 - y)]

        def rows(px, py, pc):
            return out_ref.at[pl.ds((4 * px + 2 * py + pc) * m_per, m_per), :]

        def copy(k, blk, to, src=None):
            return pltpu.make_async_remote_copy(
                src_ref=rows(*blk) if src is None else src, dst_ref=rows(*blk),
                send_sem=send_sems.at[k], recv_sem=recv_sems.at[k], device_id=to, device_id_type=MESH_ID)

        mine = pltpu.make_async_copy(x_ref, rows(*me), local_sem)
        mine.start()
        first = [copy(0, me, sibling, src=x_ref)]
        first += [copy(1 + j, me, (*chip, c), src=x_ref) for j, chip in enumerate(chips)]
        for cp in first:
            cp.start()
        passed = [copy(4 + j, (*chip, c), sibling) for j, chip in enumerate(chips)]
        for j, chip in enumerate(chips):
            copy(1 + j, (*chip, c), me).wait_recv()
            passed[j].start()
        copy(0, sibling, me).wait_recv()
        for j, chip in enumerate(chips):
            copy(4 + j, (*chip, 1 - c), me).wait_recv()
        for cp in first + passed:
            cp.wait_send()
        mine.wait()

    return pl.pallas_call(
        body, out_shape=jax.ShapeDtypeStruct((8 * m_per, n), block.dtype),
        in_specs=[pl.BlockSpec(memory_space=pltpu.VMEM)], out_specs=pl.BlockSpec(memory_space=pltpu.VMEM),
        scratch_shapes=[pltpu.SemaphoreType.DMA((7,)), pltpu.SemaphoreType.DMA((7,)), pltpu.SemaphoreType.DMA],
        compiler_params=_params(), name=name,
    )(block)


def gather_chips(name, shard):
    def body(x_ref, out_ref, send_sems, recv_sems, local_sem):
        x, y, c = _place()
        mine = pltpu.make_async_copy(x_ref, out_ref.at[2 * x + y], local_sem)
        mine.start()
        copies = []
        for p, peer in enumerate(_chip_peers(x, y, c)):
            cp = pltpu.make_async_remote_copy(
                src_ref=x_ref, dst_ref=out_ref.at[2 * x + y], send_sem=send_sems.at[p], recv_sem=recv_sems.at[p],
                device_id=peer, device_id_type=MESH_ID)
            cp.start()
            copies.append(cp)
        for p, (px, py, _) in enumerate(_chip_peers(x, y, c)):
            pltpu.make_async_remote_copy(
                src_ref=x_ref, dst_ref=out_ref.at[2 * px + py], send_sem=send_sems.at[p], recv_sem=recv_sems.at[p],
                device_id=(px, py, c), device_id_type=MESH_ID).wait_recv()
        for cp in copies:
            cp.wait_send()
        mine.wait()

    hbm = pl.BlockSpec(memory_space=pltpu.HBM)
    return pl.pallas_call(
        body, out_shape=jax.ShapeDtypeStruct((4,) + shard.shape, shard.dtype), in_specs=[hbm], out_specs=hbm,
        scratch_shapes=[pltpu.SemaphoreType.DMA((3,)), pltpu.SemaphoreType.DMA((3,)), pltpu.SemaphoreType.DMA],
        compiler_params=_params(), name=name,
    )(shard)


def swap_sibling(name, block):
    def body(x_ref, out_ref, send_sem, recv_sem):
        x, y, c = _place()
        cp = pltpu.make_async_remote_copy(src_ref=x_ref, dst_ref=out_ref, send_sem=send_sem, recv_sem=recv_sem,
                                          device_id=(x, y, 1 - c), device_id_type=MESH_ID)
        cp.start()
        cp.wait()

    hbm = pl.BlockSpec(memory_space=pltpu.HBM)
    return pl.pallas_call(
        body, out_shape=jax.ShapeDtypeStruct(block.shape, block.dtype), in_specs=[hbm], out_specs=hbm,
        scratch_shapes=[pltpu.SemaphoreType.DMA, pltpu.SemaphoreType.DMA],
        compiler_params=_params(), name=name,
    )(block)


def scatter_chips(name, pieces):
    def body(x_ref, out_ref, send_sems, recv_sems):
        x, y, c = _place()
        copies = []
        for p, (px, py, pc) in enumerate(_chip_peers(x, y, c)):
            cp = pltpu.make_async_remote_copy(
                src_ref=x_ref.at[2 * px + py], dst_ref=out_ref.at[p], send_sem=send_sems.at[p], recv_sem=recv_sems.at[p],
                device_id=(px, py, pc), device_id_type=MESH_ID)
            cp.start()
            copies.append(cp)
        for cp in copies:
            cp.wait()

    hbm = pl.BlockSpec(memory_space=pltpu.HBM)
    return pl.pallas_call(
        body, out_shape=jax.ShapeDtypeStruct((3,) + pieces.shape[1:], pieces.dtype), in_specs=[hbm], out_specs=hbm,
        scratch_shapes=[pltpu.SemaphoreType.DMA((3,)), pltpu.SemaphoreType.DMA((3,))],
        compiler_params=_params(), name=name,
    )(pieces)


def _sum_devices(name, got, fold=False):
    def body(a_ref, *o_refs):
        acc = a_ref[0]
        for i in range(1, N_DEV):
            acc = acc + a_ref[i]
        o_refs[0][...] = acc
        if fold:
            o_refs[1][...] = acc + pltpu.roll(acc, SUBLANES // 2, 0)

    shape = jax.ShapeDtypeStruct(got.shape[1:], F32)
    out = pl.pallas_call(body, out_shape=[shape] * (2 if fold else 1), compiler_params=_params(), name=name)(got)
    return out if fold else out[0]


def _w_in_bounds(d, pw, scw):
    off_z = 3 * DN_WIDTH
    off_a = off_z + DN_WIDTH
    off_pool = off_a + 2 * N_DIRHEAD
    off_sc = off_pool + pw
    off_gate = off_sc + 3 * scw
    return (0, off_z, off_a, off_pool, off_sc, off_gate, off_gate + 3 * d)


def _misc_widths(pw, scw):
    return (DN_WIDTH, pw, 3 * scw, LANES)


def split_matrices(full, pw, scw):
    depth, d = full["w_in"].shape[:2]
    b = _w_in_bounds(d, pw, scw)
    out = {k: [] for k in ("qkv", "gate", "misc", "gu", "br_a", "br_b", "br_c", "o", "down")}
    for l in range(depth):
        w = full["w_in"][l]
        out["qkv"].append(w[:, b[0]:b[1]])
        out["gate"].append(w[:, b[5]:b[6]])
        out["misc"].append(jnp.concatenate([w[:, b[1]:b[2]], w[:, b[3]:b[5]], _pad_lanes(w[:, b[2]:b[3]])], axis=1))
        out["gu"].append(full["w_gu"][l])
        for k in ("br_a", "br_b", "br_c", "o", "down"):
            out[k].append(full["w_" + k][l])
    return out


def join_matrix_grads(g, pw, scw):
    depth = len(g["qkv"])
    n_z, n_rest, n_ab = DN_WIDTH, pw + 3 * scw, 2 * N_DIRHEAD
    w_in = []
    for l in range(depth):
        m = g["misc"][l]
        w_in.append(jnp.concatenate([g["qkv"][l], m[:, :n_z], m[:, n_z + n_rest:n_z + n_rest + n_ab],
                                     m[:, n_z:n_z + n_rest], g["gate"][l]], axis=1))
    out = {"w_in": jnp.stack(w_in), "w_gu": jnp.stack(g["gu"])}
    for k in ("br_a", "br_b", "br_c", "o", "down"):
        out["w_" + k] = jnp.stack(g[k])
    return out


def split_cols(x, widths):
    edges = np.cumsum((0,) + tuple(widths))

    def cut(x):
        return tuple(x[:, a:b] for a, b in zip(edges[:-1], edges[1:]))

    f = jax.custom_vjp(cut)
    f.defvjp(lambda x: (cut(x), None), lambda _, cts: (jnp.concatenate(cts, axis=1),))
    return f(x)


def _row(v):
    return v.reshape(1, -1)


def _pad_lanes(v, width=LANES):
    return jnp.pad(v, ((0, 0), (0, width - v.shape[1])))


def _block_diag(blocks):
    g, n, _ = blocks.shape
    out = jnp.zeros((g * n, g * n), blocks.dtype)
    for i in range(g):
        out = out.at[i * n:(i + 1) * n, i * n:(i + 1) * n].set(blocks[i])
    return out


def local_loss(p, carriers, x, mod_lat, mod_ctx, wts, ctx, target):
    ctx_rows, d = ctx.shape
    depth = len(wts["qkv"])
    pw, scw = p["pool_scale"].shape[1], p["sc_conv_w"].shape[2]
    ff = wts["down"][0].shape[0]
    ct = ctx_rows // ROW_TILE
    xs = jnp.concatenate([ctx, x], axis=0)
    seg = lambda l, k: jnp.stack([mod_ctx[l, k], mod_lat[l, k]]).reshape(2, 1, d)
    dn = gate2 = None
    for l in range(depth):
        sh1, sc1, g1, sh2, sc2, g2 = (seg(l, k) for k in range(6))
        tag = f"l{l}_"
        lin = lambda key, a: mm(tag + key, a, wts[key][l], carriers[key][l])
        if l == 0:
            (h1,) = rowwise(tag + "mod", _fn_modulate, [xs], [_row(p["norm1_g"][l]), sh1, sc1], seg=(False, True, True),
                            outs=[(d, MXU_DTYPE, 1)], ctx_tiles=ct)
        else:
            xs, h1 = rowwise(tag + "resmod1", _fn_resmod, [xs, dn], [gate2, _row(p["norm1_g"][l]), sh1, sc1],
                             seg=(True, False, True, True), outs=[(d, F32, 1), (d, MXU_DTYPE, 1)], ctx_tiles=ct)
        fan = ("qkv", "gate", "misc")
        p_qkv, p_gate, p_misc = mm_fanout(tag + "in", h1, [wts[k][l] for k in fan], [carriers[k][l] for k in fan])
        p_z, p_pool, p_sc, p_ab = split_cols(p_misc, _misc_widths(pw, scw))
        cw = p["dn_conv_w"][l]
        q, k, v, gb = rowwise(
            tag + "dnprep", _fn_dnprep, [p_qkv, p_ab],
            [cw[0:1], cw[1:2], cw[2:3], _pad_lanes(p["dn_a_log"][l].reshape(1, -1)), _pad_lanes(p["dn_dt_bias"][l].reshape(1, -1))],
            halo=(True, False), parts_r=(3 * DN_HEADS, 1), parts_v=(3 * DN_HEADS,) * 3 + (1, 1),
            outs=[(DN_WIDTH, F32, DN_HEADS)] * 3 + [(LANES, F32, 1)], ctx_tiles=ct)
        o_f, o_b = deltanet(tag + "dn", q, k, v, gb, ctx_rows)
        (oz,) = rowwise(tag + "dnpost", _fn_dnpost, [o_f, o_b, p_z], [_row(p["dn_norm_g"][l])], parts_r=(DN_HEADS,) * 3,
                        outs=[(DN_WIDTH, MXU_DTYPE, DN_HEADS)], ctx_tiles=ct)
        y_a = lin("br_a", oz)
        means = pool_means(tag + "box", p_pool, ctx_rows)
        (dpool,) = rowwise(tag + "poolsub", _fn_sub, [means, p_pool], [], outs=[(pw, MXU_DTYPE, 1)], ctx_tiles=ct)
        pool_mat = _block_diag(p["pool_w"][l])
        mixed = mm(tag + "poolw", dpool, pool_mat.astype(MXU_DTYPE), pool_mat)
        (yb_in,) = rowwise(tag + "poolscale", _fn_scale, [mixed], [_row(p["pool_scale"][l])], outs=[(pw, MXU_DTYPE, 1)], ctx_tiles=ct)
        y_b = lin("br_b", yb_in)
        sw = p["sc_conv_w"][l]
        (yc_in,) = rowwise(tag + "sconv", _fn_shortconv, [p_sc], [sw[0:1], sw[1:2], sw[2:3]], halo=(True,), parts_r=(3,),
                           outs=[(scw, MXU_DTYPE, 1)], ctx_tiles=ct)
        y_c = lin("br_c", yc_in)
        (y,) = rowwise(tag + "merge", _fn_merge, [p_gate, y_a, y_b, y_c], [], parts_r=(3, 1, 1, 1),
                       outs=[(d, MXU_DTYPE, 1)], ctx_tiles=ct)
        mix = lin("o", y)
        xs, h2 = rowwise(tag + "resmod2", _fn_resmod, [xs, mix], [g1, _row(p["norm2_g"][l]), sh2, sc2],
                         seg=(True, False, True, True), outs=[(d, F32, 1), (d, MXU_DTYPE, 1)], ctx_tiles=ct)
        (act,) = rowwise(tag + "swiglu", _fn_swiglu, [lin("gu", h2)], [], parts_r=(2,),
                         outs=[(ff, MXU_DTYPE, 1)], ctx_tiles=ct)
        dn = lin("down", act)
        gate2 = g2
    counts = jnp.concatenate([jnp.zeros((1, 1, LANES), F32), jnp.ones((1, 1, LANES), F32)])
    (total,) = rowwise("final", _fn_final, [xs, dn, target], [gate2, _row(p["final_norm_g"]), counts],
                       seg=(True, False, True), reds=[LANES], ctx_tiles=ct, skip=(0, 0, ct))
    return total[0, 0]


BIG = ("w_in", "w_br_a", "w_br_b", "w_br_c", "w_o", "w_gu", "w_down")
ROW_SHARDED = ("w_o", "w_down")
SMALL_REPL = ("norm1_g", "norm2_g", "dn_a_log", "dn_dt_bias", "dn_norm_g", "pool_w", "pool_scale", "final_norm_g")
SMALL_SHARD = ("dn_conv_w", "sc_conv_w")
WEIGHTS = ("c_ctx", "w_ada", "b_ada", "norm1_g", "norm2_g", "w_in", "dn_conv_w", "dn_a_log", "dn_dt_bias", "dn_norm_g", "pool_w",
           "pool_scale", "sc_conv_w", "w_br_a", "w_br_b", "w_br_c", "w_o", "w_gu", "w_down", "final_norm_g")
N_CHIPS, N_DEV = 4, 8
COMM_COLS = 1024


def _pack(arrays, rows_multiple, cols=COMM_COLS, dtype=F32):
    flat = jnp.concatenate([a.astype(dtype).reshape(-1) for a in arrays])
    rows = -(-flat.shape[0] // (cols * rows_multiple)) * rows_multiple
    return jnp.pad(flat, (0, rows * cols - flat.shape[0])).reshape(rows, cols)


def _unpack(flat, shapes):
    out, pos = [], 0
    for s in shapes:
        n = int(np.prod(s))
        out.append(flat[pos:pos + n].reshape(s))
        pos += n
    return out


def _join_shards(parts, name):
    _, l, a, b = parts.shape
    if name in ROW_SHARDED:
        return jnp.moveaxis(parts, 0, 1).reshape(l, N_CHIPS * a, b)
    return jnp.moveaxis(parts, 0, 2).reshape(l, a, N_CHIPS * b)


def _cut_shard(full, name, j):
    if name in ROW_SHARDED:
        n = full.shape[1] // N_CHIPS
        return full[:, j * n:(j + 1) * n, :]
    n = full.shape[2] // N_CHIPS
    return full[:, :, j * n:(j + 1) * n]


def _dsilu(x):
    s = _sigmoid(x)
    return s + x * s * (1.0 - s)


def kernel(x, c, ctx, c_ctx, w_ada, b_ada, norm1_g, norm2_g, w_in, dn_conv_w, dn_a_log, dn_dt_bias, dn_norm_g, pool_w, pool_scale, sc_conv_w, w_br_a, w_br_b, w_br_c, w_o, w_gu, w_down, final_norm_g, loss_target, m_c_ctx, m_w_ada, m_b_ada, m_norm1_g, m_norm2_g, m_w_in, m_dn_conv_w, m_dn_a_log, m_dn_dt_bias, m_dn_norm_g, m_pool_w, m_pool_scale, m_sc_conv_w, m_w_br_a, m_w_br_b, m_w_br_c, m_w_o, m_w_gu, m_w_down, m_final_norm_g, v_c_ctx, v_w_ada, v_b_ada, v_norm1_g, v_norm2_g, v_w_in, v_dn_conv_w, v_dn_a_log, v_dn_dt_bias, v_dn_norm_g, v_pool_w, v_pool_scale, v_sc_conv_w, v_w_br_a, v_w_br_b, v_w_br_c, v_w_o, v_w_gu, v_w_down, v_final_norm_g):
    given = dict(locals())
    ix, iy, ic = _place()
    chip, dev = 2 * ix + iy, 4 * ix + 2 * iy + ic
    d = x.shape[-1]
    depth = w_in.shape[0]
    ada_cols = w_ada.shape[2]
    spare = 2 * SUBLANES - N_DEV - 1

    got0 = all_gather8("gather_cond", _pack([c, dn_conv_w, sc_conv_w], SUBLANES)).reshape(N_DEV, -1)
    c_all = got0[:, :d]
    taps = [_unpack(got0[2 * j, d:], [dn_conv_w.shape, sc_conv_w.shape]) for j in range(N_CHIPS)]
    full = {"dn_conv_w": jnp.concatenate([t[0] for t in taps], axis=-1),
            "sc_conv_w": jnp.concatenate([t[1] for t in taps], axis=-1)}
    big_shapes = [given[n].shape for n in BIG]
    mine = _pack([given[n] for n in BIG], 4 * SUBLANES, dtype=MXU_DTYPE)
    mine = mine.reshape(2, mine.shape[0] // 2, COMM_COLS)
    got_half = gather_chips("gather_weights", lax.dynamic_index_in_dim(mine, ic, 0, keepdims=False))
    other_half = swap_sibling("swap_weights", got_half)
    shards = jnp.stack([jnp.where(ic == 0, got_half, other_half), jnp.where(ic == 0, other_half, got_half)], axis=1)
    shards = shards.reshape(N_CHIPS, -1)
    per_chip = [_unpack(shards[j], big_shapes) for j in range(N_CHIPS)]
    mats = {n: _join_shards(jnp.stack([per_chip[j][i] for j in range(N_CHIPS)]), n) for i, n in enumerate(BIG)}
    for n in SMALL_REPL:
        full[n] = given[n]
    pw, scw = pool_scale.shape[1], full["sc_conv_w"].shape[2]
    wts = split_matrices(mats, pw, scw)
    carriers = {k: [jnp.zeros(w.shape, F32) for w in ws] for k, ws in wts.items()}

    cond = jnp.concatenate([c_all, c_ctx[None], jnp.zeros((spare, d), F32)])
    s_cond = _ew("silu_cond", _silu, cond)
    mod_cols = jnp.concatenate([_mm(f"ada{l}_f", s_cond, w_ada[l], NN) for l in range(depth)], axis=1)
    mod_got = all_gather8("gather_mod", mod_cols).reshape(N_DEV, 2 * SUBLANES, depth, ada_cols)
    mod_all = jnp.concatenate([mod_got[2 * j] for j in range(N_CHIPS)], axis=-1) + b_ada[None]
    mod_lat = lax.dynamic_index_in_dim(mod_all, dev, 0, keepdims=False).reshape(depth, 6, d)
    mod_ctx = mod_all[N_DEV].reshape(depth, 6, d)

    loss_local, (g_full, g_mats, grad_x, g_mod_lat, g_mod_ctx) = jax.value_and_grad(local_loss, argnums=(0, 1, 2, 3, 4))(
        full, carriers, x[0], mod_lat, mod_ctx, wts, ctx[0], loss_target[0])
    g_full.update(join_matrix_grads(g_mats, pw, scw))
    loss = lax.psum(loss_local, ("x", "y", "c"))

    dmod_cols = (2 * depth * 6 * d) // SUBLANES
    dmod = all_gather8("gather_dmod", _pack([g_mod_lat, g_mod_ctx], SUBLANES, cols=dmod_cols))
    dmod = dmod.reshape(N_DEV, SUBLANES, dmod_cols)
    dmod_sum, dmod_fold = _sum_devices("reduce_dmod", dmod, fold=True)
    half_rows = SUBLANES // 2
    grad_b_ada = dmod_fold[:half_rows].reshape(depth, 6 * d)
    dctx_sum = dmod_sum[half_rows:].reshape(1, depth, 6 * d)
    d9 = jnp.concatenate([dmod[:, :half_rows].reshape(N_DEV, depth, 6 * d), dctx_sum, jnp.zeros((spare, depth, 6 * d), F32)])
    d9 = lax.dynamic_slice_in_dim(d9, chip * ada_cols, ada_cols, axis=2)
    grad_w_ada = jnp.stack([_mm(f"ada{l}_dw", s_cond, d9[:, l], TN) for l in range(depth)])
    ds_cond = [_mm(f"ada{l}_da", d9[:, l], w_ada[l], NT) for l in range(depth)]
    dsilu_part = _sum_rows("sum_dcond", ds_cond)[N_DEV]

    small_names = SMALL_REPL + SMALL_SHARD
    small_grads = [dsilu_part] + [g_full[n] for n in small_names]
    small_shapes = [a.shape for a in small_grads]
    n_small = sum(int(np.prod(s)) for s in small_shapes)
    cols = -(-n_small // (SUBLANES * LANES)) * LANES
    got2 = all_gather8("gather_small", _pack(small_grads, SUBLANES, cols=cols)).reshape(N_DEV, SUBLANES, cols)
    small_sum = _unpack(_sum_devices("reduce_small", got2).reshape(-1), small_shapes)
    grads = dict(zip(small_names, small_sum[1:]))
    grads["c_ctx"] = _ew("dsilu", lambda g, z: 0.5 * g * _dsilu(z), _row(small_sum[0]), _row(c_ctx)).reshape(-1)
    grads["w_ada"], grads["b_ada"] = grad_w_ada, grad_b_ada
    for n in SMALL_SHARD:
        width = given[n].shape[-1]
        grads[n] = lax.dynamic_slice_in_dim(grads[n], chip * width, width, axis=-1)

    pieces = jnp.stack([_pack([_cut_shard(g_full[n], n, j) for n in BIG], 2 * LANES) for j in range(N_CHIPS)])
    rows_h = pieces.shape[1] // 2
    halves = jnp.moveaxis(pieces.reshape(N_CHIPS, 2, rows_h, COMM_COLS), 1, 0)
    keep = lax.dynamic_index_in_dim(halves, ic, 0, keepdims=False).reshape(N_CHIPS * rows_h, COMM_COLS)
    give = lax.dynamic_index_in_dim(halves, 1 - ic, 0, keepdims=False).reshape(N_CHIPS * rows_h, COMM_COLS)
    pair = _sum_rows("sum_pair", [keep, swap_sibling("swap_halves", give)], out_dtype=MXU_DTYPE)
    pair = pair.reshape(N_CHIPS, rows_h, COMM_COLS)
    got3 = scatter_chips("scatter_pieces", pair)
    own = lax.dynamic_index_in_dim(pair, chip, 0, keepdims=False)
    half = _sum_rows("sum_chips", [own, got3[0], got3[1], got3[2]])
    other = swap_sibling("swap_reduced", half)
    lo = jnp.where(ic == 0, half, other)
    hi = jnp.where(ic == 0, other, half)
    for n, g in zip(BIG, _unpack(jnp.concatenate([lo, hi]).reshape(-1), big_shapes)):
        grads[n] = g

    delta, new_m, new_v = {}, {}, {}
    large = BIG + ("w_ada",)
    for n in large:
        shape = given[n].shape
        flat = [a.reshape(-1, shape[-1]) for a in (given[n], grads[n], given["m_" + n], given["v_" + n])]
        res = adamw("adamw_" + n, *flat)
        delta[n], new_m[n], new_v[n] = (a.reshape(shape) for a in res)
    rest = [n for n in WEIGHTS if n not in large]
    rest_shapes = [given[n].shape for n in rest]
    packed = [_pack([src[pre + n] for n in rest], SUBLANES, cols=LANES)
              for src, pre in ((given, ""), (grads, ""), (given, "m_"), (given, "v_"))]
    for res, o in zip((delta, new_m, new_v), adamw("adamw_small", *packed)):
        for n, a in zip(rest, _unpack(o.reshape(-1), rest_shapes)):
            res[n] = a
    return (loss, grad_x[None], *[grads[n] for n in WEIGHTS], *[delta[n] for n in WEIGHTS],
            *[new_m[n] for n in WEIGHTS], *[new_v[n] for n in WEIGHTS])
```

```python
import functools
import math

import numpy as np
import jax
import jax.numpy as jnp
from jax import lax
from jax.experimental import pallas as pl
from jax.experimental.pallas import tpu as pltpu

F32 = jnp.float32
MXU_DTYPE = jnp.bfloat16
HIGHEST = lax.Precision.HIGHEST

DN_HEADS = 4
HEAD_DIM = 128
DN_WIDTH = DN_HEADS * HEAD_DIM
DN_CHUNK = 64
GRID_W = 64
EPS = 1e-6
POOL_WINDOWS = (2, 4, 8, 16)
N_DIRHEAD = 2 * DN_HEADS
ADAM_LR, ADAM_B1, ADAM_B2, ADAM_EPS, ADAM_WD, ADAM_STEP = 0.001, 0.9, 0.999, 1e-08, 0.01, 10

LANES = 128
SUBLANES = 8
ROW_TILE = 256
VMEM_LIMIT = 56 * 1024 * 1024

MESH_ID = pl.DeviceIdType.MESH
NN, NT, TN = ((1,), (0,)), ((1,), (1,)), ((0,), (0,))


def _params(sem=None):
    return pltpu.CompilerParams(dimension_semantics=sem, vmem_limit_bytes=VMEM_LIMIT)


def _pick(n, cands):
    for c in cands:
        if c <= n and n % c == 0:
            return c
    return n


def _mm(name, a, b, dims, out_dtype=F32):
    if dims == NN:
        (m, kk), (_, n) = a.shape, b.shape
    elif dims == NT:
        (m, kk), (n, _) = a.shape, b.shape
    else:
        (kk, m), (_, n) = a.shape, b.shape
    tm = _pick(m, (768, 1024, 1408, 512, 256, 128, 64, 32, 16, 8))
    tn = _pick(n, (1024, 1408, 1664, 768, 896, 512, 256, 128))
    tk = kk if dims == NN and kk <= 2816 else _pick(kk, (1024, 1408, 768, 512, 256, 128))
    gi, gj, gl = m // tm, n // tn, kk // tk
    if dims == NN:
        a_spec = pl.BlockSpec((tm, tk), lambda i, j, l: (i, l))
        b_spec = pl.BlockSpec((tk, tn), lambda i, j, l: (l, j))
    elif dims == NT:
        a_spec = pl.BlockSpec((tm, tk), lambda i, j, l: (i, l))
        b_spec = pl.BlockSpec((tn, tk), lambda i, j, l: (j, l))
    else:
        a_spec = pl.BlockSpec((tk, tm), lambda i, j, l: (l, i))
        b_spec = pl.BlockSpec((tk, tn), lambda i, j, l: (l, j))
    direct = gl == 1
    use_acc = (not direct) and out_dtype != F32

    def body(a_ref, b_ref, o_ref, *scratch):
        part = lax.dot_general(a_ref[...].astype(MXU_DTYPE), b_ref[...].astype(MXU_DTYPE), (dims, ((), ())),
                               preferred_element_type=F32)
        if direct:
            o_ref[...] = part.astype(out_dtype)
            return
        acc = scratch[0] if use_acc else o_ref
        l = pl.program_id(2)

        @pl.when(l == 0)
        def _():
            acc[...] = part

        @pl.when(l > 0)
        def _():
            acc[...] += part

        if use_acc:
            @pl.when(l == gl - 1)
            def _():
                o_ref[...] = acc[...].astype(out_dtype)

    return pl.pallas_call(
        body, grid=(gi, gj, gl), in_specs=[a_spec, b_spec],
        out_specs=pl.BlockSpec((tm, tn), lambda i, j, l: (i, j)),
        out_shape=jax.ShapeDtypeStruct((m, n), out_dtype),
        scratch_shapes=[pltpu.VMEM((tm, tn), F32)] if use_acc else [],
        compiler_params=_params(("parallel", "parallel", "arbitrary")), name=name,
    )(a, b)


def mm(name, a, w, carrier):
    @jax.custom_vjp
    def f(a, w, carrier):
        return _mm(name + "_f", a, w, NN, out_dtype=MXU_DTYPE)

    def fwd(a, w, carrier):
        return f(a, w, carrier), (a, w)

    def bwd(res, dc):
        a, w = res
        da = _mm(name + "_da", dc, w, NT, out_dtype=a.dtype)
        dw = _mm(name + "_dw", a, dc, TN)
        return da, None, dw

    f.defvjp(fwd, bwd)
    return f(a, w, carrier)


def _mm_nt_sum(name, dcs, ws, out_dtype):
    m, kk = dcs[0].shape[0], ws[0].shape[0]
    tm = _pick(m, (768, 1024, 512, 256, 128, 64, 32, 16, 8))
    tns = [_pick(w.shape[1], (1664, 1536, 1024, 768, 896, 512, 256, 128)) for w in ws]
    counts = [w.shape[1] // tn for w, tn in zip(ws, tns)]
    starts = [sum(counts[:g]) for g in range(len(ws))]
    steps = sum(counts)

    def col(g):
        return lambda i, t: jnp.clip(t - starts[g], 0, counts[g] - 1)

    def body(*refs):
        dc_refs, w_refs, o_ref, acc = refs[:len(ws)], refs[len(ws):2 * len(ws)], refs[-2], refs[-1]
        t = pl.program_id(1)

        @pl.when(t == 0)
        def _():
            acc[...] = jnp.zeros_like(acc)

        for g in range(len(ws)):
            @pl.when(jnp.logical_and(t >= starts[g], t < starts[g] + counts[g]))
            def _():
                acc[...] += lax.dot_general(dc_refs[g][...].astype(MXU_DTYPE), w_refs[g][...].astype(MXU_DTYPE),
                                            (NT, ((), ())), preferred_element_type=F32)

        @pl.when(t == steps - 1)
        def _():
            o_ref[...] = acc[...].astype(out_dtype)

    dc_specs = [pl.BlockSpec((tm, tns[g]), (lambda c: lambda i, t: (i, c(i, t)))(col(g))) for g in range(len(ws))]
    w_specs = [pl.BlockSpec((kk, tns[g]), (lambda c: lambda i, t: (0, c(i, t)))(col(g))) for g in range(len(ws))]
    return pl.pallas_call(
        body, grid=(m // tm, steps), in_specs=dc_specs + w_specs,
        out_specs=pl.BlockSpec((tm, kk), lambda i, t: (i, 0)),
        out_shape=jax.ShapeDtypeStruct((m, kk), out_dtype),
        scratch_shapes=[pltpu.VMEM((tm, kk), F32)],
        compiler_params=_params(("parallel", "arbitrary")), name=name,
    )(*dcs, *ws)


def mm_fanout(name, a, ws, carriers):
    n = len(ws)

    @jax.custom_vjp
    def f(a, ws, carriers):
        return tuple(_mm(f"{name}{g}_f", a, ws[g], NN, out_dtype=MXU_DTYPE) for g in range(n))

    def fwd(a, ws, carriers):
        return f(a, ws, carriers), (a, ws)

    def bwd(res, dcs):
        a, ws = res
        da = _mm_nt_sum(name + "_da", list(dcs), list(ws), a.dtype)
        dws = tuple(_mm(f"{name}{g}_dw", a, dcs[g], TN) for g in range(n))
        return da, None, dws

    f.defvjp(fwd, bwd)
    return f(a, tuple(ws), tuple(carriers))


def _split(x, parts):
    w = x.shape[-1] // parts
    return [x[:, k * w:(k + 1) * w] for k in range(parts)]


def _cat(xs):
    return xs[0] if len(xs) == 1 else jnp.concatenate(xs, axis=-1)


def _shift_rows(x, prev_ref, next_ref, i, ctx_tiles, nt):
    tr = x.shape[0]
    rid = lax.broadcasted_iota(jnp.int32, x.shape, 0)
    first = jnp.logical_or(i == 0, i == ctx_tiles)
    last = jnp.logical_or(i == ctx_tiles - 1, i == nt - 1)
    prow = jnp.where(first, 0.0, prev_ref[SUBLANES - 1:SUBLANES, :].astype(F32))
    nrow = jnp.where(last, 0.0, next_ref[0:1, :].astype(F32))
    xm = jnp.where(rid == 0, prow, pltpu.roll(x, 1, 0))
    xp = jnp.where(rid == tr - 1, nrow, pltpu.roll(x, tr - 1, 0))
    return xm, xp


def rowwise(name, fn, rows, vecs, *, halo=(), seg=(), parts_r=None, parts_v=None, outs=(), reds=(), ctx_tiles=1, skip=None):
    nr, nv = len(rows), len(vecs)
    halo = tuple(halo) or (False,) * nr
    seg = tuple(seg) or (False,) * nv
    skip = tuple(skip or (0,) * nr)
    parts_r = tuple(parts_r or (1,) * nr)
    parts_v = tuple(parts_v or (1,) * nv)
    r_total = rows[0].shape[0]
    tr = ROW_TILE
    nt = r_total // tr
    assert r_total % tr == 0 and (ctx_tiles > 0 or not any(seg)) and not any(h and s for h, s in zip(halo, skip))

    def tile_map(i):
        return lambda t: (jnp.maximum(t - skip[i], 0), 0)

    def row_specs():
        sp = []
        for i, r in enumerate(rows):
            w = r.shape[1]
            sp.append(pl.BlockSpec((tr, w), tile_map(i)))
            if halo[i]:
                k = tr // SUBLANES
                sp.append(pl.BlockSpec((SUBLANES, w), lambda t: (jnp.maximum(t * k - 1, 0), 0)))
                sp.append(pl.BlockSpec((SUBLANES, w), lambda t: (jnp.minimum((t + 1) * k, nt * k - 1), 0)))
        return sp

    def vec_spec(j):
        w = vecs[j].shape[-1]
        if seg[j]:
            return pl.BlockSpec((None, 1, w), lambda t: (jnp.where(t >= ctx_tiles, 1, 0), 0, 0))
        return pl.BlockSpec((1, w), lambda t: (0, 0))

    def row_args(rv):
        a = []
        for i in range(nr):
            a += [rv[i]] * 3 if halo[i] else [rv[i]]
        return a

    def load(refs, t):
        pos, rp = 0, []
        for i in range(nr):
            x = refs[pos][...].astype(F32)
            if halo[i]:
                xm, xp = _shift_rows(x, refs[pos + 1], refs[pos + 2], t, ctx_tiles, nt)
                rp.append(list(zip(_split(x, parts_r[i]), _split(xm, parts_r[i]), _split(xp, parts_r[i]))))
                pos += 3
            else:
                rp.append(_split(x, parts_r[i]))
                pos += 1
        vp = []
        for j in range(nv):
            vp.append(_split(refs[pos][...].astype(F32), parts_v[j]))
            pos += 1
        return rp, vp, refs[pos:]

    n_out, n_red = len(outs), len(reds)

    def fwd_call(*rv):
        def body(*refs):
            t = pl.program_id(0)
            rp, vp, rest = load(refs, t)
            o_parts, r_parts = fn(rp, vp)
            for k in range(n_out):
                rest[k][...] = _cat(o_parts[k]).astype(outs[k][1])
            for k in range(n_red):
                ref = rest[n_out + k]

                @pl.when(t == 0)
                def _():
                    ref[...] = r_parts[k]

                @pl.when(t > 0)
                def _():
                    ref[...] += r_parts[k]

        res = pl.pallas_call(
            body, grid=(nt,),
            in_specs=row_specs() + [vec_spec(j) for j in range(nv)],
            out_specs=[pl.BlockSpec((tr, o[0]), lambda t: (t, 0)) for o in outs]
            + [pl.BlockSpec((1, w), lambda t: (0, 0)) for w in reds],
            out_shape=[jax.ShapeDtypeStruct((r_total, o[0]), o[1]) for o in outs]
            + [jax.ShapeDtypeStruct((1, w), F32) for w in reds],
            compiler_params=_params(("arbitrary",)), name=name + "_f",
        )(*row_args(rv), *rv[nr:])
        return tuple(res)

    def bwd_call(rv, cts):
        def body(*refs):
            t = pl.program_id(0)
            rp, vp, rest = load(refs, t)
            ct_o = [_split(rest[k][...].astype(F32), outs[k][2]) for k in range(n_out)]
            ct_r = [rest[n_out + k][...] for k in range(n_red)]
            rest = rest[n_out + n_red:]
            _, vjp = jax.vjp(fn, rp, vp)
            d_rp, d_vp = vjp((ct_o, ct_r))
            pos = 0
            for i in range(nr):
                if halo[i]:
                    for c in range(3):
                        rest[pos + c][...] = _cat([p[c] for p in d_rp[i]])
                    pos += 3
                else:
                    rest[pos][...] = _cat(d_rp[i]).astype(rows[i].dtype)
                    pos += 1
            for j in range(nv):
                ref, val = rest[pos + j], _cat(d_vp[j])
                start = jnp.logical_or(t == 0, t == ctx_tiles) if seg[j] else t == 0

                @pl.when(start)
                def _():
                    ref[...] = val

                @pl.when(jnp.logical_not(start))
                def _():
                    ref[...] += val

        d_row_specs, d_row_shapes = [], []
        for i, r in enumerate(rows):
            w = r.shape[1]
            for _ in range(3 if halo[i] else 1):
                d_row_specs.append(pl.BlockSpec((tr, w), tile_map(i)))
                d_row_shapes.append(jax.ShapeDtypeStruct(r.shape, F32 if halo[i] else r.dtype))
        res = pl.pallas_call(
            body, grid=(nt,),
            in_specs=row_specs() + [vec_spec(j) for j in range(nv)]
            + [pl.BlockSpec((tr, o[0]), lambda t: (t, 0)) for o in outs]
            + [pl.BlockSpec((1, w), lambda t: (0, 0)) for w in reds],
            out_specs=d_row_specs + [vec_spec(j) for j in range(nv)],
            out_shape=d_row_shapes + [jax.ShapeDtypeStruct(v.shape, F32) for v in vecs],
            compiler_params=_params(("arbitrary",)), name=name + "_b",
        )(*row_args(rv), *rv[nr:], *cts)
        d_rows, pos = [], 0
        for i in range(nr):
            if halo[i]:
                d_rows.append(_unshift(res[pos], res[pos + 1], res[pos + 2], ctx_tiles * tr).astype(rows[i].dtype))
                pos += 3
            else:
                d_rows.append(res[pos])
                pos += 1
        return tuple(d_rows) + tuple(res[pos:])

    @jax.custom_vjp
    def f(*rv):
        return fwd_call(*rv)

    f.defvjp(lambda *rv: (fwd_call(*rv), rv), lambda rv, cts: bwd_call(rv, cts))
    return f(*rows, *vecs)


def _unshift(d, dm, dp, ctx_rows):
    r = d.shape[0]
    t = lax.broadcasted_iota(jnp.int32, (r, 1), 0)
    zero = jnp.zeros((1, d.shape[1]), d.dtype)
    from_m = jnp.concatenate([dm[1:], zero], axis=0)
    from_p = jnp.concatenate([zero, dp[:-1]], axis=0)
    from_m = jnp.where(t == ctx_rows - 1, 0.0, from_m)
    from_p = jnp.where(t == ctx_rows, 0.0, from_p)
    return d + from_m + from_p


def _sigmoid(x):
    return 0.5 * (jnp.tanh(0.5 * x) + 1.0)


def _silu(x):
    return x * _sigmoid(x)


def _softplus(x):
    return jnp.maximum(x, 0.0) + jnp.log(1.0 + jnp.exp(-jnp.abs(x)))


def _rms(x, g):
    return x * lax.rsqrt(jnp.mean(x * x, axis=-1, keepdims=True) + EPS) * g


def _fn_modulate(r, v):
    (x,), (g,), (sh,), (sc,) = r[0], v[0], v[1], v[2]
    return [[_rms(x, g) * (1.0 + sc) + sh]], []


def _fn_resmod(r, v):
    (x,), (y,) = r
    (gate,), (g,), (sh,), (sc,) = v
    xn = x + gate * y
    return [[xn], [_rms(xn, g) * (1.0 + sc) + sh]], []


def _fn_final(r, v):
    (x,), (y,), (tgt,) = r
    (gate,), (g,), (counts,) = v
    err = _rms(x + gate * y, g) - tgt
    row_loss = jnp.mean(err * err, axis=-1, keepdims=True)
    total = 0.5 * jnp.sum(row_loss, axis=0, keepdims=True)
    return [], [total * counts]


def _fn_dnprep(r, v):
    qkv, (ab,) = r
    w0, w1, w2, (alog,), (dtb,) = v
    out = [[], [], []]
    for n, (x, xm, xp) in enumerate(qkv):
        which = n // DN_HEADS
        y = _silu(xm * w0[n] + x * w1[n] + xp * w2[n])
        if which < 2:
            y = y * lax.rsqrt(jnp.sum(y * y, axis=-1, keepdims=True) + EPS)
        if which == 0:
            y = y * (HEAD_DIM ** -0.5)
        out[which].append(y)
    lane = lax.broadcasted_iota(jnp.int32, ab.shape, 1)
    g = -jnp.exp(alog) * _softplus(ab + dtb)
    gb = jnp.where(lane < N_DIRHEAD, g, jnp.where(lane < 2 * N_DIRHEAD, _sigmoid(ab), 0.0))
    return out + [[gb]], []


def _fn_dnpost(r, v):
    of, ob, z = r
    (g,) = v[0]
    return [[_rms(a + b, g) * _silu(c) for a, b, c in zip(of, ob, z)]], []


def _fn_sub(r, v):
    return [[r[0][0] - r[1][0]]], []


def _fn_scale(r, v):
    return [[r[0][0] * v[0][0]]], []


def _fn_shortconv(r, v):
    (xin, gb, gc), = r
    (w0,), (w1,), (w2,) = v
    u, um, up = (gc[k] * xin[k] for k in range(3))
    return [[gb[0] * (um * w0 + u * w1 + up * w2)]], []


def _fn_merge(r, v):
    gates, (ya,), (yb,), (yc,) = r
    return [[_sigmoid(gates[0]) * ya + _sigmoid(gates[1]) * yb + _sigmoid(gates[2]) * yc]], []


def _fn_swiglu(r, v):
    gate, up = r[0]
    return [[_silu(gate) * up]], []


def _dot_hi(a, b):
    return lax.dot_general(a, b, (NN, ((), ())), precision=HIGHEST, preferred_element_type=F32)


def _bdot(a, b, dims):
    (ca,), (cb,) = dims
    return lax.dot_general(a.astype(MXU_DTYPE), b.astype(MXU_DTYPE), (((ca + 1,), (cb + 1,)), ((0,), (0,))),
                           preferred_element_type=F32)


def _hi_lo(a):
    hi = a.astype(MXU_DTYPE)
    return hi, (a - hi.astype(F32)).astype(MXU_DTYPE)


def _bdot3_raw(a, b, dims):
    (ah, al), (bh, bl) = _hi_lo(a), _hi_lo(b)
    return _bdot(ah, bh, dims) + (_bdot(ah, bl, dims) + _bdot(al, bh, dims))


@jax.custom_vjp
def _bdot3(a, b):
    return _bdot3_raw(a, b, NN)


_bdot3.defvjp(lambda a, b: (_bdot3_raw(a, b, NN), (a, b)),
              lambda res, ct: (_bdot3_raw(ct, res[1], NT), _bdot3_raw(res[0], ct, TN)))


def _inv_doubling(a):
    c = a.shape[-1]
    ii, jj = (lax.broadcasted_iota(jnp.int32, a.shape, d) for d in (1, 2))
    t = jnp.where(ii == jj, 1.0, 0.0) - a
    p = a
    for _ in range(int(math.log2(c)) - 1):
        p = _bdot3(p, p)
        t = t + _bdot3(t, p)
    return t


def _dn_gates(k4, gb):
    nb, c = N_DIRHEAD, k4.shape[1]
    k = jnp.concatenate([k4, k4], axis=0)
    lane = lax.broadcasted_iota(jnp.int32, gb.shape, 1)
    col = lambda j: jnp.sum(jnp.where(lane == j, gb, 0.0), axis=1, keepdims=True)
    g_col = jnp.concatenate([col(j)[None] for j in range(nb)], axis=0)
    b_col = jnp.concatenate([col(nb + j)[None] for j in range(nb)], axis=0)
    bi, ii, jj = (lax.broadcasted_iota(jnp.int32, (nb, c, c), a) for a in range(3))
    ahead = jnp.where(bi >= DN_HEADS, jj - ii, ii - jj)
    incl = ahead >= 0
    g_row = jnp.sum(jnp.where(ahead == 0, g_col, 0.0), axis=1, keepdims=True)
    gc_col = jnp.sum(jnp.where(incl, g_row, 0.0), axis=2, keepdims=True)
    gc_row = jnp.sum(jnp.where(ahead <= 0, g_col, 0.0), axis=1, keepdims=True)
    decay = jnp.where(incl, jnp.exp(jnp.where(incl, gc_col - gc_row, 0.0)), 0.0)
    return k, g_col, b_col, gc_col, decay, ahead > 0


def _dn_a(k4, gb):
    k, _, b_col, _, decay, strict = _dn_gates(k4, gb)
    return _bdot(k * b_col, k, NT) * jnp.where(strict, decay, 0.0)


def _dn_operands(q4, k4, v4, gb, t):
    k, g_col, b_col, gc_col, decay, _ = _dn_gates(k4, gb)
    q, v = (jnp.concatenate([a, a], axis=0) for a in (q4, v4))
    e_gc = jnp.exp(gc_col)
    u = _bdot3(t, v * b_col)
    w = _bdot3(t, k * b_col * e_gc)
    g_last = jnp.sum(g_col, axis=1, keepdims=True)
    k_state = k * jnp.exp(g_last - gc_col)
    a_qk = _bdot(q, k, NT) * decay
    return u, w, q * e_gc, k_state, a_qk, jnp.broadcast_to(jnp.exp(g_last), (N_DIRHEAD, 1, LANES))


def _dn_step(s, u, w, qd, ks, aqk, gl):
    v_new = u - _bdot(w, s, NN)
    o = _bdot(qd, s, NN) + _bdot(aqk, v_new, NN)
    return s * gl[:, :, :1] + _bdot(ks, v_new, TN), o


def _heads(x):
    return jnp.concatenate([x[None, :, h * HEAD_DIM:(h + 1) * HEAD_DIM] for h in range(DN_HEADS)], axis=0)


def _unheads(x):
    return jnp.concatenate([x[h] for h in range(x.shape[0])], axis=-1)


def _dn_rev(t, nc, n):
    return jnp.where(t < nc, nc - 1 - t, n - 1 - (t - nc))


def _pre_shapes(n):
    c, d, h = DN_CHUNK, HEAD_DIM, DN_HEADS
    shapes = [(n, h, c, d)] * 8 + [(n, h, c, c)] * 2 + [(n, h, 1, LANES)] * 2
    return shapes, [pl.BlockSpec((None,) + s[1:], lambda t: (t, 0, 0, 0)) for s in shapes]


def _inverse_spec(n):
    shape = (n, N_DIRHEAD, DN_CHUNK, DN_CHUNK)
    return shape, pl.BlockSpec((None,) + shape[1:], lambda t: (t, 0, 0, 0))


def _dn_pre_fwd(name, q, k, v, gb):
    n = q.shape[0] // DN_CHUNK
    shapes, specs = _pre_shapes(n)
    t_shape, t_spec = _inverse_spec(n)

    def body(q_ref, k_ref, v_ref, g_ref, *o_refs):
        q4, k4, v4, gb_ = _heads(q_ref[...]), _heads(k_ref[...]), _heads(v_ref[...]), g_ref[...]
        t = _inv_doubling(_dn_a(k4, gb_))
        for i, r in enumerate(_dn_operands(q4, k4, v4, gb_, t)):
            o_refs[2 * i][...] = r[:DN_HEADS]
            o_refs[2 * i + 1][...] = r[DN_HEADS:]
        o_refs[-1][...] = t

    wide = pl.BlockSpec((DN_CHUNK, DN_HEADS * HEAD_DIM), lambda t: (t, 0))
    narrow = pl.BlockSpec((DN_CHUNK, LANES), lambda t: (t, 0))
    return pl.pallas_call(body, grid=(n,), in_specs=[wide] * 3 + [narrow], out_specs=specs + [t_spec],
                          out_shape=[jax.ShapeDtypeStruct(s, F32) for s in shapes + [t_shape]],
                          compiler_params=_params(("parallel",)), name=name + "_pre_f")(q, k, v, gb)


def _dn_pre_bwd(name, q, k, v, gb, inv, cts):
    n = q.shape[0] // DN_CHUNK
    _, specs = _pre_shapes(n)
    _, t_spec = _inverse_spec(n)
    n_ct = len(specs)

    def body(q_ref, k_ref, v_ref, g_ref, t_ref, *refs):
        ct = tuple(jnp.concatenate([refs[i][...], refs[i + 1][...]], axis=0) for i in range(0, n_ct, 2))
        q4, k4, v4, gb_, t = _heads(q_ref[...]), _heads(k_ref[...]), _heads(v_ref[...]), g_ref[...], t_ref[...]
        _, vjp = jax.vjp(_dn_operands, q4, k4, v4, gb_, t)
        dq, dk, dv, dg, dt = vjp(ct)
        da = -_bdot3_raw(_bdot3_raw(t, dt, TN), t, NT)
        _, vjp_a = jax.vjp(_dn_a, k4, gb_)
        dk_a, dg_a = vjp_a(da)
        for r, val in zip(refs[n_ct:], (_unheads(dq), _unheads(dk + dk_a), _unheads(dv), dg + dg_a)):
            r[...] = val

    wide = pl.BlockSpec((DN_CHUNK, DN_HEADS * HEAD_DIM), lambda t: (t, 0))
    narrow = pl.BlockSpec((DN_CHUNK, LANES), lambda t: (t, 0))
    return pl.pallas_call(body, grid=(n,), in_specs=[wide] * 3 + [narrow, t_spec] + specs, out_specs=[wide] * 3 + [narrow],
                          out_shape=[jax.ShapeDtypeStruct(q.shape, F32)] * 3 + [jax.ShapeDtypeStruct(gb.shape, F32)],
                          compiler_params=_params(("parallel",)), name=name + "_pre_b")(q, k, v, gb, inv, *cts)


def _scan_specs(pre_shapes, fw, bw):
    maps = (lambda t: (fw(t), 0, 0, 0), lambda t: (bw(t), 0, 0, 0))
    return [pl.BlockSpec((None,) + s[1:], maps[i % 2]) for i, s in enumerate(pre_shapes)]


def _dn_scan_fwd(name, pre, nc):
    n = pre[0].shape[0]
    r = n * DN_CHUNK
    hd, nh, nb = HEAD_DIM, DN_HEADS, N_DIRHEAD
    shapes, _ = _pre_shapes(n)
    n_in = len(shapes)

    def body(*refs):
        ins, (of_ref, ob_ref, sf_ref, sb_ref, s_scr) = refs[:n_in], refs[n_in:]
        t = pl.program_id(0)

        @pl.when(t == 0)
        def _():
            s_scr[...] = jnp.zeros_like(s_scr)

        s = s_scr[...]
        sf_ref[...] = s[:nh]
        sb_ref[...] = s[nh:]
        args = [jnp.concatenate([ins[i][...], ins[i + 1][...]], axis=0) for i in range(0, n_in, 2)]
        s2, o = _dn_step(s, *args)
        s_scr[...] = s2
        of_ref[...] = _unheads(o[:nh])
        ob_ref[...] = _unheads(o[nh:])

    fw = lambda t: t
    bw = lambda t: _dn_rev(t, nc, n)
    wide, st = (DN_CHUNK, nh * hd), (None, nh, hd, hd)
    return pl.pallas_call(
        body, grid=(n,), in_specs=_scan_specs(shapes, fw, bw),
        out_specs=[pl.BlockSpec(wide, lambda t: (fw(t), 0)), pl.BlockSpec(wide, lambda t: (bw(t), 0)),
                   pl.BlockSpec(st, lambda t: (fw(t), 0, 0, 0)), pl.BlockSpec(st, lambda t: (bw(t), 0, 0, 0))],
        out_shape=[jax.ShapeDtypeStruct((r, nh * hd), F32)] * 2 + [jax.ShapeDtypeStruct((n, nh, hd, hd), F32)] * 2,
        scratch_shapes=[pltpu.VMEM((nb, hd, hd), F32)],
        compiler_params=_params(("arbitrary",)), name=name + "_scan_f",
    )(*pre)


def _dn_scan_bwd(name, pre, sall_f, sall_b, do_f, do_b, nc):
    n = pre[0].shape[0]
    hd, nh, nb = HEAD_DIM, DN_HEADS, N_DIRHEAD
    shapes, _ = _pre_shapes(n)
    n_in = len(shapes)

    def body(*refs):
        ins, (sf_ref, sb_ref, dof_ref, dob_ref) = refs[:n_in], refs[n_in:n_in + 4]
        outs, ds_scr = refs[n_in + 4:2 * n_in + 4], refs[2 * n_in + 4]
        t = pl.program_id(0)

        @pl.when(t == 0)
        def _():
            ds_scr[...] = jnp.zeros_like(ds_scr)

        args = [jnp.concatenate([ins[i][...], ins[i + 1][...]], axis=0) for i in range(0, n_in, 2)]
        s = jnp.concatenate([sf_ref[...], sb_ref[...]], axis=0)
        do = jnp.concatenate([_heads(dof_ref[...]), _heads(dob_ref[...])], axis=0)
        _, vjp = jax.vjp(_dn_step, s, *args)
        cts = vjp((ds_scr[...], do))
        ds_scr[...] = cts[0]
        for i, ct in enumerate(cts[1:]):
            outs[2 * i][...] = ct[:nh]
            outs[2 * i + 1][...] = ct[nh:]

    fw = lambda t: n - 1 - t
    bw = lambda t: _dn_rev(n - 1 - t, nc, n)
    wide, st = (DN_CHUNK, nh * hd), (None, nh, hd, hd)
    return tuple(pl.pallas_call(
        body, grid=(n,),
        in_specs=_scan_specs(shapes, fw, bw)
        + [pl.BlockSpec(st, lambda t: (fw(t), 0, 0, 0)), pl.BlockSpec(st, lambda t: (bw(t), 0, 0, 0)),
           pl.BlockSpec(wide, lambda t: (fw(t), 0)), pl.BlockSpec(wide, lambda t: (bw(t), 0))],
        out_specs=_scan_specs(shapes, fw, bw), out_shape=[jax.ShapeDtypeStruct(s, F32) for s in shapes],
        scratch_shapes=[pltpu.VMEM((nb, hd, hd), F32)],
        compiler_params=_params(("arbitrary",)), name=name + "_scan_b",
    )(*pre, sall_f, sall_b, do_f, do_b))


def deltanet(name, q, k, v, gb, ctx_rows):
    nc = ctx_rows // DN_CHUNK

    @jax.custom_vjp
    def pre(q, k, v, gb):
        return tuple(_dn_pre_fwd(name, q, k, v, gb)[:-1])

    def pre_fwd(*a):
        *ops, inv = _dn_pre_fwd(name, *a)
        return tuple(ops), (a, inv)

    pre.defvjp(pre_fwd, lambda res, cts: tuple(_dn_pre_bwd(name, *res[0], res[1], cts)))

    @jax.custom_vjp
    def scan(*ops):
        return tuple(_dn_scan_fwd(name, ops, nc)[:2])

    def scan_fwd(*ops):
        of, ob, sf, sb = _dn_scan_fwd(name, ops, nc)
        return (of, ob), (ops, sf, sb)

    scan.defvjp(scan_fwd, lambda res, cts: _dn_scan_bwd(name, res[0], res[1], res[2], cts[0], cts[1], nc))
    return scan(*pre(q, k, v, gb))


def _box_matrix(l, w):
    lo, hi = w // 2, w - 1 - w // 2
    pos = np.arange(l)
    start, end = np.clip(pos - lo, 0, l), np.clip(pos + hi + 1, 0, l)
    col = np.arange(l)[None, :]
    return ((col >= start[:, None]) & (col < end[:, None])) / (end - start)[:, None].astype(np.float64)


def _pool_matrices(ctx_rows, grid_rows):
    assert ctx_rows == ROW_TILE and ROW_TILE % GRID_W == 0
    ctx = np.stack([_box_matrix(ctx_rows, w) for w in POOL_WINDOWS])
    cols = np.stack([np.kron(np.eye(ROW_TILE // GRID_W), _box_matrix(GRID_W, w)) for w in POOL_WINDOWS])
    rows = np.stack([_box_matrix(grid_rows, w) for w in POOL_WINDOWS])[None]
    return np.stack([ctx, cols]).astype(np.float32), rows.astype(np.float32)


def _pool_apply(name, x, mats, group_w, seg_tiles, lane_tile):
    r, w = x.shape
    b = mats.shape[-1]
    ng = mats.shape[1]
    period = ng * group_w

    def body(x_ref, m_ref, o_ref):
        xv = x_ref[...].astype(F32)
        lane = lax.broadcasted_iota(jnp.int32, xv.shape, 1)
        grp = (lane % period) // group_w
        acc = jnp.zeros_like(xv)
        for g in range(ng):
            acc = acc + jnp.where(grp == g, _dot_hi(m_ref[g], xv), 0.0)
        o_ref[...] = acc

    return pl.pallas_call(
        body, grid=(r // b, w // lane_tile),
        in_specs=[pl.BlockSpec((b, lane_tile), lambda i, j: (i, j)),
                  pl.BlockSpec((None, ng, b, b), lambda i, j: (jnp.where(i >= seg_tiles, 1, 0) if mats.shape[0] > 1 else 0, 0, 0, 0))],
        out_specs=pl.BlockSpec((b, lane_tile), lambda i, j: (i, j)),
        out_shape=jax.ShapeDtypeStruct(x.shape, F32),
        compiler_params=_params(("parallel", "parallel")), name=name,
    )(x, mats)


def pool_means(name, u, ctx_rows):
    r, pw = u.shape
    grid_rows = (r - ctx_rows) // GRID_W
    gw = pw // len(POOL_WINDOWS)
    m1, m2 = _pool_matrices(ctx_rows, grid_rows)
    lane_tile = min(2048, GRID_W * pw)

    def apply(x, a1, a2, tag):
        y = _pool_apply(name + tag + "1", x, jnp.asarray(a1), gw, ctx_rows // ROW_TILE, pw)
        lat = y[ctx_rows:].reshape(grid_rows, GRID_W * pw)
        lat = _pool_apply(name + tag + "2", lat, jnp.asarray(a2), gw, 0, lane_tile)
        return jnp.concatenate([y[:ctx_rows], lat.reshape(r - ctx_rows, pw)], axis=0)

    @jax.custom_vjp
    def f(u):
        return apply(u, m1, m2, "_f")

    tr = lambda m: np.ascontiguousarray(np.swapaxes(m, -1, -2))
    f.defvjp(lambda u: (apply(u, m1, m2, "_f"), None), lambda _, ct: (apply(ct, tr(m1), tr(m2), "_b").astype(u.dtype),))
    return f(u)


def _ew(name, fn, *xs, n_out=1):
    shapes = jax.eval_shape(lambda *a: fn(*a), *xs)
    shapes = shapes if isinstance(shapes, (tuple, list)) else (shapes,)

    def body(*refs):
        res = fn(*[r[...] for r in refs[:len(xs)]])
        res = res if isinstance(res, (tuple, list)) else (res,)
        for r, o in zip(res, refs[len(xs):]):
            o[...] = r

    out = pl.pallas_call(body, out_shape=[jax.ShapeDtypeStruct(s.shape, s.dtype) for s in shapes],
                         compiler_params=_params(), name=name)(*xs)
    return out[0] if len(shapes) == 1 else tuple(out)


def _adamw_math(w, g, m, v):
    m2 = ADAM_B1 * m + (1.0 - ADAM_B1) * g
    v2 = ADAM_B2 * v + (1.0 - ADAM_B2) * (g * g)
    m_hat = m2 / (1.0 - ADAM_B1 ** ADAM_STEP)
    v_hat = v2 / (1.0 - ADAM_B2 ** ADAM_STEP)
    delta = -ADAM_LR * (m_hat / (jnp.sqrt(v_hat) + ADAM_EPS) + ADAM_WD * w)
    return delta, m2, v2


def adamw(name, w, g, m, v):
    r, c = w.shape
    tr = _pick(r, (256, 128, 64, 32, 16, 8))

    def body(w_ref, g_ref, m_ref, v_ref, d_ref, m2_ref, v2_ref):
        d_ref[...], m2_ref[...], v2_ref[...] = _adamw_math(w_ref[...], g_ref[...], m_ref[...], v_ref[...])

    spec = pl.BlockSpec((tr, c), lambda i: (i, 0))
    return pl.pallas_call(body, grid=(r // tr,), in_specs=[spec] * 4, out_specs=[spec] * 3,
                          out_shape=[jax.ShapeDtypeStruct(w.shape, F32)] * 3,
                          compiler_params=_params(("parallel",)), name=name)(w, g, m, v)


def _sum_rows(name, xs, out_dtype=F32):
    r, c = xs[0].shape
    tr = _pick(r, (512, 256, 128, 64, 32, 16, 8))

    def body(*refs):
        acc = refs[0][...].astype(F32)
        for ref in refs[1:-1]:
            acc = acc + ref[...].astype(F32)
        refs[-1][...] = acc.astype(out_dtype)

    spec = pl.BlockSpec((tr, c), lambda i: (i, 0))
    return pl.pallas_call(body, grid=(r // tr,), in_specs=[spec] * len(xs), out_specs=spec,
                          out_shape=jax.ShapeDtypeStruct((r, c), out_dtype),
                          compiler_params=_params(("parallel",)), name=name)(*xs)


def _place():
    return lax.axis_index("x"), lax.axis_index("y"), lax.axis_index("c")


def _chip_peers(x, y, c):
    return [(1 - x, y, c), (x, 1 - y, c), (1 - x, 1 - y, c)]


def all_gather8(name, block):
    m_per, n = block.shape

    def body(x_ref, out_ref, send_sems, recv_sems, local_sem):
        x, y, c = _place()
        me, sibling = (x, y, c), (x, y, 1 - c)
        chips = [(1 - x, y), (x, 1 - y), (1 - x, 1 - y)]

        def rows(px, py, pc):
            return out_ref.at[pl.ds((4 * px + 2 * py + pc) * m_per, m_per), :]

        def copy(k, blk, to, src=None):
            return pltpu.make_async_remote_copy(
                src_ref=rows(*blk) if src is None else src, dst_ref=rows(*blk),
                send_sem=send_sems.at[k], recv_sem=recv_sems.at[k], device_id=to, device_id_type=MESH_ID)

        mine = pltpu.make_async_copy(x_ref, rows(*me), local_sem)
        mine.start()
        first = [copy(0, me, sibling, src=x_ref)]
        first += [copy(1 + j, me, (*chip, c), src=x_ref) for j, chip in enumerate(chips)]
        for cp in first:
            cp.start()
        passed = [copy(4 + j, (*chip, c), sibling) for j, chip in enumerate(chips)]
        for j, chip in enumerate(chips):
            copy(1 + j, (*chip, c), me).wait_recv()
            passed[j].start()
        copy(0, sibling, me).wait_recv()
        for j, chip in enumerate(chips):
            copy(4 + j, (*chip, 1 - c), me).wait_recv()
        for cp in first + passed:
            cp.wait_send()
        mine.wait()

    return pl.pallas_call(
        body, out_shape=jax.ShapeDtypeStruct((8 * m_per, n), block.dtype),
        in_specs=[pl.BlockSpec(memory_space=pltpu.VMEM)], out_specs=pl.BlockSpec(memory_space=pltpu.VMEM),
        scratch_shapes=[pltpu.SemaphoreType.DMA((7,)), pltpu.SemaphoreType.DMA((7,)), pltpu.SemaphoreType.DMA],
        compiler_params=_params(), name=name,
    )(block)


def gather_chips(name, shard):
    def body(x_ref, out_ref, send_sems, recv_sems, local_sem):
        x, y, c = _place()
        mine = pltpu.make_async_copy(x_ref, out_ref.at[2 * x + y], local_sem)
        mine.start()
        copies = []
        for p, peer in enumerate(_chip_peers(x, y, c)):
            cp = pltpu.make_async_remote_copy(
                src_ref=x_ref, dst_ref=out_ref.at[2 * x + y], send_sem=send_sems.at[p], recv_sem=recv_sems.at[p],
                device_id=peer, device_id_type=MESH_ID)
            cp.start()
            copies.append(cp)
        for p, (px, py, _) in enumerate(_chip_peers(x, y, c)):
            pltpu.make_async_remote_copy(
                src_ref=x_ref, dst_ref=out_ref.at[2 * px + py], send_sem=send_sems.at[p], recv_sem=recv_sems.at[p],
                device_id=(px, py, c), device_id_type=MESH_ID).wait_recv()
        for cp in copies:
            cp.wait_send()
        mine.wait()

    hbm = pl.BlockSpec(memory_space=pltpu.HBM)
    return pl.pallas_call(
        body, out_shape=jax.ShapeDtypeStruct((4,) + shard.shape, shard.dtype), in_specs=[hbm], out_specs=hbm,
        scratch_shapes=[pltpu.SemaphoreType.DMA((3,)), pltpu.SemaphoreType.DMA((3,)), pltpu.SemaphoreType.DMA],
        compiler_params=_params(), name=name,
    )(shard)


def swap_sibling(name, block):
    def body(x_ref, out_ref, send_sem, recv_sem):
        x, y, c = _place()
        cp = pltpu.make_async_remote_copy(src_ref=x_ref, dst_ref=out_ref, send_sem=send_sem, recv_sem=recv_sem,
                                          device_id=(x, y, 1 - c), device_id_type=MESH_ID)
        cp.start()
        cp.wait()

    hbm = pl.BlockSpec(memory_space=pltpu.HBM)
    return pl.pallas_call(
        body, out_shape=jax.ShapeDtypeStruct(block.shape, block.dtype), in_specs=[hbm], out_specs=hbm,
        scratch_shapes=[pltpu.SemaphoreType.DMA, pltpu.SemaphoreType.DMA],
        compiler_params=_params(), name=name,
    )(block)


def scatter_chips(name, pieces):
    def body(x_ref, out_ref, send_sems, recv_sems):
        x, y, c = _place()
        copies = []
        for p, (px, py, pc) in enumerate(_chip_peers(x, y, c)):
            cp = pltpu.make_async_remote_copy(
                src_ref=x_ref.at[2 * px + py], dst_ref=out_ref.at[p], send_sem=send_sems.at[p], recv_sem=recv_sems.at[p],
                device_id=(px, py, pc), device_id_type=MESH_ID)
            cp.start()
            copies.append(cp)
        for cp in copies:
            cp.wait()

    hbm = pl.BlockSpec(memory_space=pltpu.HBM)
    return pl.pallas_call(
        body, out_shape=jax.ShapeDtypeStruct((3,) + pieces.shape[1:], pieces.dtype), in_specs=[hbm], out_specs=hbm,
        scratch_shapes=[pltpu.SemaphoreType.DMA((3,)), pltpu.SemaphoreType.DMA((3,))],
        compiler_params=_params(), name=name,
    )(pieces)


def _sum_devices(name, got, fold=False):
    def body(a_ref, *o_refs):
        acc = a_ref[0]
        for i in range(1, N_DEV):
            acc = acc + a_ref[i]
        o_refs[0][...] = acc
        if fold:
            o_refs[1][...] = acc + pltpu.roll(acc, SUBLANES // 2, 0)

    shape = jax.ShapeDtypeStruct(got.shape[1:], F32)
    out = pl.pallas_call(body, out_shape=[shape] * (2 if fold else 1), compiler_params=_params(), name=name)(got)
    return out if fold else out[0]


def _w_in_bounds(d, pw, scw):
    off_z = 3 * DN_WIDTH
    off_a = off_z + DN_WIDTH
    off_pool = off_a + 2 * N_DIRHEAD
    off_sc = off_pool + pw
    off_gate = off_sc + 3 * scw
    return (0, off_z, off_a, off_pool, off_sc, off_gate, off_gate + 3 * d)


def _misc_widths(pw, scw):
    return (DN_WIDTH, pw, 3 * scw, LANES)


def split_matrices(full, pw, scw):
    depth, d = full["w_in"].shape[:2]
    b = _w_in_bounds(d, pw, scw)
    out = {k: [] for k in ("qkv", "gate", "misc", "gu", "br_a", "br_b", "br_c", "o", "down")}
    for l in range(depth):
        w = full["w_in"][l]
        out["qkv"].append(w[:, b[0]:b[1]])
        out["gate"].append(w[:, b[5]:b[6]])
        out["misc"].append(jnp.concatenate([w[:, b[1]:b[2]], w[:, b[3]:b[5]], _pad_lanes(w[:, b[2]:b[3]])], axis=1))
        out["gu"].append(full["w_gu"][l])
        for k in ("br_a", "br_b", "br_c", "o", "down"):
            out[k].append(full["w_" + k][l])
    return out


def join_matrix_grads(g, pw, scw):
    depth = len(g["qkv"])
    n_z, n_rest, n_ab = DN_WIDTH, pw + 3 * scw, 2 * N_DIRHEAD
    w_in = []
    for l in range(depth):
        m = g["misc"][l]
        w_in.append(jnp.concatenate([g["qkv"][l], m[:, :n_z], m[:, n_z + n_rest:n_z + n_rest + n_ab],
                                     m[:, n_z:n_z + n_rest], g["gate"][l]], axis=1))
    out = {"w_in": jnp.stack(w_in), "w_gu": jnp.stack(g["gu"])}
    for k in ("br_a", "br_b", "br_c", "o", "down"):
        out["w_" + k] = jnp.stack(g[k])
    return {k: v.astype(MXU_DTYPE) for k, v in out.items()}


def split_cols(x, widths):
    edges = np.cumsum((0,) + tuple(widths))

    def cut(x):
        return tuple(x[:, a:b] for a, b in zip(edges[:-1], edges[1:]))

    f = jax.custom_vjp(cut)
    f.defvjp(lambda x: (cut(x), None), lambda _, cts: (jnp.concatenate(cts, axis=1),))
    return f(x)


def _row(v):
    return v.reshape(1, -1)


def _pad_lanes(v, width=LANES):
    return jnp.pad(v, ((0, 0), (0, width - v.shape[1])))


def _block_diag(blocks):
    g, n, _ = blocks.shape
    out = jnp.zeros((g * n, g * n), blocks.dtype)
    for i in range(g):
        out = out.at[i * n:(i + 1) * n, i * n:(i + 1) * n].set(blocks[i])
    return out


def local_loss(p, carriers, x, mod_lat, mod_ctx, wts, ctx, target):
    ctx_rows, d = ctx.shape
    depth = len(wts["qkv"])
    pw, scw = p["pool_scale"].shape[1], p["sc_conv_w"].shape[2]
    ff = wts["down"][0].shape[0]
    ct = ctx_rows // ROW_TILE
    xs = jnp.concatenate([ctx, x], axis=0)
    seg = lambda l, k: jnp.stack([mod_ctx[l, k], mod_lat[l, k]]).reshape(2, 1, d)
    dn = gate2 = None
    for l in range(depth):
        sh1, sc1, g1, sh2, sc2, g2 = (seg(l, k) for k in range(6))
        tag = f"l{l}_"
        lin = lambda key, a: mm(tag + key, a, wts[key][l], carriers[key][l])
        if l == 0:
            (h1,) = rowwise(tag + "mod", _fn_modulate, [xs], [_row(p["norm1_g"][l]), sh1, sc1], seg=(False, True, True),
                            outs=[(d, MXU_DTYPE, 1)], ctx_tiles=ct)
        else:
            xs, h1 = rowwise(tag + "resmod1", _fn_resmod, [xs, dn], [gate2, _row(p["norm1_g"][l]), sh1, sc1],
                             seg=(True, False, True, True), outs=[(d, F32, 1), (d, MXU_DTYPE, 1)], ctx_tiles=ct)
        fan = ("qkv", "gate", "misc")
        p_qkv, p_gate, p_misc = mm_fanout(tag + "in", h1, [wts[k][l] for k in fan], [carriers[k][l] for k in fan])
        p_z, p_pool, p_sc, p_ab = split_cols(p_misc, _misc_widths(pw, scw))
        cw = p["dn_conv_w"][l]
        q, k, v, gb = rowwise(
            tag + "dnprep", _fn_dnprep, [p_qkv, p_ab],
            [cw[0:1], cw[1:2], cw[2:3], _pad_lanes(p["dn_a_log"][l].reshape(1, -1)), _pad_lanes(p["dn_dt_bias"][l].reshape(1, -1))],
            halo=(True, False), parts_r=(3 * DN_HEADS, 1), parts_v=(3 * DN_HEADS,) * 3 + (1, 1),
            outs=[(DN_WIDTH, F32, DN_HEADS)] * 3 + [(LANES, F32, 1)], ctx_tiles=ct)
        o_f, o_b = deltanet(tag + "dn", q, k, v, gb, ctx_rows)
        (oz,) = rowwise(tag + "dnpost", _fn_dnpost, [o_f, o_b, p_z], [_row(p["dn_norm_g"][l])], parts_r=(DN_HEADS,) * 3,
                        outs=[(DN_WIDTH, MXU_DTYPE, DN_HEADS)], ctx_tiles=ct)
        y_a = lin("br_a", oz)
        means = pool_means(tag + "box", p_pool, ctx_rows)
        (dpool,) = rowwise(tag + "poolsub", _fn_sub, [means, p_pool], [], outs=[(pw, MXU_DTYPE, 1)], ctx_tiles=ct)
        pool_mat = _block_diag(p["pool_w"][l])
        mixed = mm(tag + "poolw", dpool, pool_mat.astype(MXU_DTYPE), pool_mat)
        (yb_in,) = rowwise(tag + "poolscale", _fn_scale, [mixed], [_row(p["pool_scale"][l])], outs=[(pw, MXU_DTYPE, 1)], ctx_tiles=ct)
        y_b = lin("br_b", yb_in)
        sw = p["sc_conv_w"][l]
        (yc_in,) = rowwise(tag + "sconv", _fn_shortconv, [p_sc], [sw[0:1], sw[1:2], sw[2:3]], halo=(True,), parts_r=(3,),
                           outs=[(scw, MXU_DTYPE, 1)], ctx_tiles=ct)
        y_c = lin("br_c", yc_in)
        (y,) = rowwise(tag + "merge", _fn_merge, [p_gate, y_a, y_b, y_c], [], parts_r=(3, 1, 1, 1),
                       outs=[(d, MXU_DTYPE, 1)], ctx_tiles=ct)
        mix = lin("o", y)
        xs, h2 = rowwise(tag + "resmod2", _fn_resmod, [xs, mix], [g1, _row(p["norm2_g"][l]), sh2, sc2],
                         seg=(True, False, True, True), outs=[(d, F32, 1), (d, MXU_DTYPE, 1)], ctx_tiles=ct)
        (act,) = rowwise(tag + "swiglu", _fn_swiglu, [lin("gu", h2)], [], parts_r=(2,),
                         outs=[(ff, MXU_DTYPE, 1)], ctx_tiles=ct)
        dn = lin("down", act)
        gate2 = g2
    counts = jnp.concatenate([jnp.zeros((1, 1, LANES), F32), jnp.ones((1, 1, LANES), F32)])
    (total,) = rowwise("final", _fn_final, [xs, dn, target], [gate2, _row(p["final_norm_g"]), counts],
                       seg=(True, False, True), reds=[LANES], ctx_tiles=ct, skip=(0, 0, ct))
    return total[0, 0]


BIG = ("w_in", "w_br_a", "w_br_b", "w_br_c", "w_o", "w_gu", "w_down")
ROW_SHARDED = ("w_o", "w_down")
SMALL_REPL = ("norm1_g", "norm2_g", "dn_a_log", "dn_dt_bias", "dn_norm_g", "pool_w", "pool_scale", "final_norm_g")
SMALL_SHARD = ("dn_conv_w", "sc_conv_w")
WEIGHTS = ("c_ctx", "w_ada", "b_ada", "norm1_g", "norm2_g", "w_in", "dn_conv_w", "dn_a_log", "dn_dt_bias", "dn_norm_g", "pool_w",
           "pool_scale", "sc_conv_w", "w_br_a", "w_br_b", "w_br_c", "w_o", "w_gu", "w_down", "final_norm_g")
N_CHIPS, N_DEV = 4, 8
COMM_COLS = 1024


def _pack(arrays, rows_multiple, cols=COMM_COLS, dtype=F32):
    flat = jnp.concatenate([a.astype(dtype).reshape(-1) for a in arrays])
    rows = -(-flat.shape[0] // (cols * rows_multiple)) * rows_multiple
    return jnp.pad(flat, (0, rows * cols - flat.shape[0])).reshape(rows, cols)


def _unpack(flat, shapes):
    out, pos = [], 0
    for s in shapes:
        n = int(np.prod(s))
        out.append(flat[pos:pos + n].reshape(s))
        pos += n
    return out


def _join_shards(parts, name):
    _, l, a, b = parts.shape
    if name in ROW_SHARDED:
        return jnp.moveaxis(parts, 0, 1).reshape(l, N_CHIPS * a, b)
    return jnp.moveaxis(parts, 0, 2).reshape(l, a, N_CHIPS * b)


def _cut_shard(full, name, j):
    if name in ROW_SHARDED:
        n = full.shape[1] // N_CHIPS
        return full[:, j * n:(j + 1) * n, :]
    n = full.shape[2] // N_CHIPS
    return full[:, :, j * n:(j + 1) * n]


def _dsilu(x):
    s = _sigmoid(x)
    return s + x * s * (1.0 - s)


def kernel(x, c, ctx, c_ctx, w_ada, b_ada, norm1_g, norm2_g, w_in, dn_conv_w, dn_a_log, dn_dt_bias, dn_norm_g, pool_w, pool_scale, sc_conv_w, w_br_a, w_br_b, w_br_c, w_o, w_gu, w_down, final_norm_g, loss_target, m_c_ctx, m_w_ada, m_b_ada, m_norm1_g, m_norm2_g, m_w_in, m_dn_conv_w, m_dn_a_log, m_dn_dt_bias, m_dn_norm_g, m_pool_w, m_pool_scale, m_sc_conv_w, m_w_br_a, m_w_br_b, m_w_br_c, m_w_o, m_w_gu, m_w_down, m_final_norm_g, v_c_ctx, v_w_ada, v_b_ada, v_norm1_g, v_norm2_g, v_w_in, v_dn_conv_w, v_dn_a_log, v_dn_dt_bias, v_dn_norm_g, v_pool_w, v_pool_scale, v_sc_conv_w, v_w_br_a, v_w_br_b, v_w_br_c, v_w_o, v_w_gu, v_w_down, v_final_norm_g):
    given = dict(locals())
    ix, iy, ic = _place()
    chip, dev = 2 * ix + iy, 4 * ix + 2 * iy + ic
    d = x.shape[-1]
    depth = w_in.shape[0]
    ada_cols = w_ada.shape[2]
    spare = 2 * SUBLANES - N_DEV - 1

    got0 = all_gather8("gather_cond", _pack([c, dn_conv_w, sc_conv_w], SUBLANES)).reshape(N_DEV, -1)
    c_all = got0[:, :d]
    taps = [_unpack(got0[2 * j, d:], [dn_conv_w.shape, sc_conv_w.shape]) for j in range(N_CHIPS)]
    full = {"dn_conv_w": jnp.concatenate([t[0] for t in taps], axis=-1),
            "sc_conv_w": jnp.concatenate([t[1] for t in taps], axis=-1)}
    big_shapes = [given[n].shape for n in BIG]
    mine = _pack([given[n] for n in BIG], 4 * SUBLANES, dtype=MXU_DTYPE)
    mine = mine.reshape(2, mine.shape[0] // 2, COMM_COLS)
    got_half = gather_chips("gather_weights", lax.dynamic_index_in_dim(mine, ic, 0, keepdims=False))
    other_half = swap_sibling("swap_weights", got_half)
    shards = jnp.stack([jnp.where(ic == 0, got_half, other_half), jnp.where(ic == 0, other_half, got_half)], axis=1)
    shards = shards.reshape(N_CHIPS, -1)
    per_chip = [_unpack(shards[j], big_shapes) for j in range(N_CHIPS)]
    mats = {n: _join_shards(jnp.stack([per_chip[j][i] for j in range(N_CHIPS)]), n) for i, n in enumerate(BIG)}
    for n in SMALL_REPL:
        full[n] = given[n]
    pw, scw = pool_scale.shape[1], full["sc_conv_w"].shape[2]
    wts = split_matrices(mats, pw, scw)
    carriers = {k: [jnp.zeros(w.shape, F32) for w in ws] for k, ws in wts.items()}

    cond = jnp.concatenate([c_all, c_ctx[None], jnp.zeros((spare, d), F32)])
    s_cond = _ew("silu_cond", _silu, cond)
    mod_cols = jnp.concatenate([_mm(f"ada{l}_f", s_cond, w_ada[l], NN) for l in range(depth)], axis=1)
    mod_got = all_gather8("gather_mod", mod_cols).reshape(N_DEV, 2 * SUBLANES, depth, ada_cols)
    mod_all = jnp.concatenate([mod_got[2 * j] for j in range(N_CHIPS)], axis=-1) + b_ada[None]
    mod_lat = lax.dynamic_index_in_dim(mod_all, dev, 0, keepdims=False).reshape(depth, 6, d)
    mod_ctx = mod_all[N_DEV].reshape(depth, 6, d)

    loss_local, (g_full, g_mats, grad_x, g_mod_lat, g_mod_ctx) = jax.value_and_grad(local_loss, argnums=(0, 1, 2, 3, 4))(
        full, carriers, x[0], mod_lat, mod_ctx, wts, ctx[0], loss_target[0])
    g_full.update(join_matrix_grads(g_mats, pw, scw))
    loss = lax.psum(loss_local, ("x", "y", "c"))

    dmod_cols = (2 * depth * 6 * d) // SUBLANES
    dmod = all_gather8("gather_dmod", _pack([g_mod_lat, g_mod_ctx], SUBLANES, cols=dmod_cols))
    dmod = dmod.reshape(N_DEV, SUBLANES, dmod_cols)
    dmod_sum, dmod_fold = _sum_devices("reduce_dmod", dmod, fold=True)
    half_rows = SUBLANES // 2
    grad_b_ada = dmod_fold[:half_rows].reshape(depth, 6 * d)
    dctx_sum = dmod_sum[half_rows:].reshape(1, depth, 6 * d)
    d9 = jnp.concatenate([dmod[:, :half_rows].reshape(N_DEV, depth, 6 * d), dctx_sum, jnp.zeros((spare, depth, 6 * d), F32)])
    d9 = lax.dynamic_slice_in_dim(d9, chip * ada_cols, ada_cols, axis=2)
    grad_w_ada = jnp.stack([_mm(f"ada{l}_dw", s_cond, d9[:, l], TN) for l in range(depth)])
    ds_cond = [_mm(f"ada{l}_da", d9[:, l], w_ada[l], NT) for l in range(depth)]
    dsilu_part = _sum_rows("sum_dcond", ds_cond)[N_DEV]

    small_names = SMALL_REPL + SMALL_SHARD
    small_grads = [dsilu_part] + [g_full[n] for n in small_names]
    small_shapes = [a.shape for a in small_grads]
    n_small = sum(int(np.prod(s)) for s in small_shapes)
    cols = -(-n_small // (SUBLANES * LANES)) * LANES
    got2 = all_gather8("gather_small", _pack(small_grads, SUBLANES, cols=cols)).reshape(N_DEV, SUBLANES, cols)
    small_sum = _unpack(_sum_devices("reduce_small", got2).reshape(-1), small_shapes)
    grads = dict(zip(small_names, small_sum[1:]))
    grads["c_ctx"] = _ew("dsilu", lambda g, z: 0.5 * g * _dsilu(z), _row(small_sum[0]), _row(c_ctx)).reshape(-1)
    grads["w_ada"], grads["b_ada"] = grad_w_ada, grad_b_ada
    for n in SMALL_SHARD:
        width = given[n].shape[-1]
        grads[n] = lax.dynamic_slice_in_dim(grads[n], chip * width, width, axis=-1)

    pieces = jnp.stack([_pack([_cut_shard(g_full[n], n, j) for n in BIG], 2 * LANES, dtype=MXU_DTYPE) for j in range(N_CHIPS)])
    rows_h = pieces.shape[1] // 2
    halves = jnp.moveaxis(pieces.reshape(N_CHIPS, 2, rows_h, COMM_COLS), 1, 0)
    keep = lax.dynamic_index_in_dim(halves, ic, 0, keepdims=False).reshape(N_CHIPS * rows_h, COMM_COLS)
    give = lax.dynamic_index_in_dim(halves, 1 - ic, 0, keepdims=False).reshape(N_CHIPS * rows_h, COMM_COLS)
    pair = _sum_rows("sum_pair", [keep, swap_sibling("swap_halves", give)], out_dtype=MXU_DTYPE)
    pair = pair.reshape(N_CHIPS, rows_h, COMM_COLS)
    got3 = scatter_chips("scatter_pieces", pair)
    own = lax.dynamic_index_in_dim(pair, chip, 0, keepdims=False)
    half = _sum_rows("sum_chips", [own, got3[0], got3[1], got3[2]])
    other = swap_sibling("swap_reduced", half)
    lo = jnp.where(ic == 0, half, other)
    hi = jnp.where(ic == 0, other, half)
    for n, g in zip(BIG, _unpack(jnp.concatenate([lo, hi]).reshape(-1), big_shapes)):
        grads[n] = g

    delta, new_m, new_v = {}, {}, {}
    large = BIG + ("w_ada",)
    for n in large:
        shape = given[n].shape
        flat = [a.reshape(-1, shape[-1]) for a in (given[n], grads[n], given["m_" + n], given["v_" + n])]
        res = adamw("adamw_" + n, *flat)
        delta[n], new_m[n], new_v[n] = (a.reshape(shape) for a in res)
    rest = [n for n in WEIGHTS if n not in large]
    rest_shapes = [given[n].shape for n in rest]
    packed = [_pack([src[pre + n] for n in rest], SUBLANES, cols=LANES)
              for src, pre in ((given, ""), (grads, ""), (given, "m_"), (given, "v_"))]
    for res, o in zip((delta, new_m, new_v), adamw("adamw_small", *packed)):
        for n, a in zip(rest, _unpack(o.reshape(-1), rest_shapes)):
            res[n] = a
    return (loss, grad_x[None], *[grads[n] for n in WEIGHTS], *[delta[n] for n in WEIGHTS],
            *[new_m[n] for n in WEIGHTS], *[new_v[n] for n in WEIGHTS])
```

```python
import functools
import math

import numpy as np
import jax
import jax.numpy as jnp
from jax import lax
from jax.experimental import pallas as pl
from jax.experimental.pallas import tpu as pltpu

F32 = jnp.float32
MXU_DTYPE = jnp.bfloat16
HIGHEST = lax.Precision.HIGHEST

DN_HEADS = 4
HEAD_DIM = 128
DN_WIDTH = DN_HEADS * HEAD_DIM
DN_CHUNK = 64
GRID_W = 64
EPS = 1e-6
POOL_WINDOWS = (2, 4, 8, 16)
N_DIRHEAD = 2 * DN_HEADS
ADAM_LR, ADAM_B1, ADAM_B2, ADAM_EPS, ADAM_WD, ADAM_STEP = 0.001, 0.9, 0.999, 1e-08, 0.01, 10

LANES = 128
SUBLANES = 8
ROW_TILE = 256
VMEM_LIMIT = 56 * 1024 * 1024

MESH_ID = pl.DeviceIdType.MESH
NN, NT, TN = ((1,), (0,)), ((1,), (1,)), ((0,), (0,))


def _params(sem=None):
    return pltpu.CompilerParams(dimension_semantics=sem, vmem_limit_bytes=VMEM_LIMIT)


def _pick(n, cands):
    for c in cands:
        if c <= n and n % c == 0:
            return c
    return n


def _mm(name, a, b, dims, out_dtype=F32):
    if dims == NN:
        (m, kk), (_, n) = a.shape, b.shape
    elif dims == NT:
        (m, kk), (n, _) = a.shape, b.shape
    else:
        (kk, m), (_, n) = a.shape, b.shape
    tm = _pick(m, (768, 1024, 1408, 512, 256, 128, 64, 32, 16, 8))
    tn = _pick(n, (1536, 1024, 1408, 1664, 768, 896, 512, 256, 128))
    tk = kk if dims == NN and kk <= 2816 else _pick(kk, (1024, 1408, 768, 512, 256, 128))
    gi, gj, gl = m // tm, n // tn, kk // tk
    if dims == NN:
        a_spec = pl.BlockSpec((tm, tk), lambda i, j, l: (i, l))
        b_spec = pl.BlockSpec((tk, tn), lambda i, j, l: (l, j))
    elif dims == NT:
        a_spec = pl.BlockSpec((tm, tk), lambda i, j, l: (i, l))
        b_spec = pl.BlockSpec((tn, tk), lambda i, j, l: (j, l))
    else:
        a_spec = pl.BlockSpec((tk, tm), lambda i, j, l: (l, i))
        b_spec = pl.BlockSpec((tk, tn), lambda i, j, l: (l, j))
    direct = gl == 1
    use_acc = (not direct) and out_dtype != F32

    def body(a_ref, b_ref, o_ref, *scratch):
        part = lax.dot_general(a_ref[...].astype(MXU_DTYPE), b_ref[...].astype(MXU_DTYPE), (dims, ((), ())),
                               preferred_element_type=F32)
        if direct:
            o_ref[...] = part.astype(out_dtype)
            return
        acc = scratch[0] if use_acc else o_ref
        l = pl.program_id(2)

        @pl.when(l == 0)
        def _():
            acc[...] = part

        @pl.when(l > 0)
        def _():
            acc[...] += part

        if use_acc:
            @pl.when(l == gl - 1)
            def _():
                o_ref[...] = acc[...].astype(out_dtype)

    return pl.pallas_call(
        body, grid=(gi, gj, gl), in_specs=[a_spec, b_spec],
        out_specs=pl.BlockSpec((tm, tn), lambda i, j, l: (i, j)),
        out_shape=jax.ShapeDtypeStruct((m, n), out_dtype),
        scratch_shapes=[pltpu.VMEM((tm, tn), F32)] if use_acc else [],
        compiler_params=_params(("parallel", "parallel", "arbitrary")), name=name,
    )(a, b)


def mm(name, a, w, carrier):
    @jax.custom_vjp
    def f(a, w, carrier):
        return _mm(name + "_f", a, w, NN, out_dtype=MXU_DTYPE)

    def fwd(a, w, carrier):
        return f(a, w, carrier), (a, w)

    def bwd(res, dc):
        a, w = res
        da = _mm(name + "_da", dc, w, NT, out_dtype=a.dtype)
        dw = _mm(name + "_dw", a, dc, TN)
        return da, None, dw

    f.defvjp(fwd, bwd)
    return f(a, w, carrier)


def _mm_nt_sum(name, dcs, ws, out_dtype):
    m, kk = dcs[0].shape[0], ws[0].shape[0]
    tm = _pick(m, (768, 1024, 512, 256, 128, 64, 32, 16, 8))
    tns = [_pick(w.shape[1], (1664, 1536, 1024, 768, 896, 512, 256, 128)) for w in ws]
    counts = [w.shape[1] // tn for w, tn in zip(ws, tns)]
    starts = [sum(counts[:g]) for g in range(len(ws))]
    steps = sum(counts)

    def col(g):
        return lambda i, t: jnp.clip(t - starts[g], 0, counts[g] - 1)

    def body(*refs):
        dc_refs, w_refs, o_ref, acc = refs[:len(ws)], refs[len(ws):2 * len(ws)], refs[-2], refs[-1]
        t = pl.program_id(1)

        @pl.when(t == 0)
        def _():
            acc[...] = jnp.zeros_like(acc)

        for g in range(len(ws)):
            @pl.when(jnp.logical_and(t >= starts[g], t < starts[g] + counts[g]))
            def _():
                acc[...] += lax.dot_general(dc_refs[g][...].astype(MXU_DTYPE), w_refs[g][...].astype(MXU_DTYPE),
                                            (NT, ((), ())), preferred_element_type=F32)

        @pl.when(t == steps - 1)
        def _():
            o_ref[...] = acc[...].astype(out_dtype)

    dc_specs = [pl.BlockSpec((tm, tns[g]), (lambda c: lambda i, t: (i, c(i, t)))(col(g))) for g in range(len(ws))]
    w_specs = [pl.BlockSpec((kk, tns[g]), (lambda c: lambda i, t: (0, c(i, t)))(col(g))) for g in range(len(ws))]
    return pl.pallas_call(
        body, grid=(m // tm, steps), in_specs=dc_specs + w_specs,
        out_specs=pl.BlockSpec((tm, kk), lambda i, t: (i, 0)),
        out_shape=jax.ShapeDtypeStruct((m, kk), out_dtype),
        scratch_shapes=[pltpu.VMEM((tm, kk), F32)],
        compiler_params=_params(("parallel", "arbitrary")), name=name,
    )(*dcs, *ws)


def mm_fanout(name, a, ws, carriers):
    n = len(ws)

    @jax.custom_vjp
    def f(a, ws, carriers):
        return tuple(_mm(f"{name}{g}_f", a, ws[g], NN, out_dtype=MXU_DTYPE) for g in range(n))

    def fwd(a, ws, carriers):
        return f(a, ws, carriers), (a, ws)

    def bwd(res, dcs):
        a, ws = res
        da = _mm_nt_sum(name + "_da", list(dcs), list(ws), a.dtype)
        dws = tuple(_mm(f"{name}{g}_dw", a, dcs[g], TN) for g in range(n))
        return da, None, dws

    f.defvjp(fwd, bwd)
    return f(a, tuple(ws), tuple(carriers))


def _split(x, parts):
    w = x.shape[-1] // parts
    return [x[:, k * w:(k + 1) * w] for k in range(parts)]


def _cat(xs):
    return xs[0] if len(xs) == 1 else jnp.concatenate(xs, axis=-1)


def _shift_rows(x, prev_ref, next_ref, i, ctx_tiles, nt):
    tr = x.shape[0]
    rid = lax.broadcasted_iota(jnp.int32, x.shape, 0)
    first = jnp.logical_or(i == 0, i == ctx_tiles)
    last = jnp.logical_or(i == ctx_tiles - 1, i == nt - 1)
    prow = jnp.where(first, 0.0, prev_ref[SUBLANES - 1:SUBLANES, :].astype(F32))
    nrow = jnp.where(last, 0.0, next_ref[0:1, :].astype(F32))
    xm = jnp.where(rid == 0, prow, pltpu.roll(x, 1, 0))
    xp = jnp.where(rid == tr - 1, nrow, pltpu.roll(x, tr - 1, 0))
    return xm, xp


def rowwise(name, fn, rows, vecs, *, halo=(), seg=(), parts_r=None, parts_v=None, outs=(), reds=(), ctx_tiles=1, skip=None):
    nr, nv = len(rows), len(vecs)
    halo = tuple(halo) or (False,) * nr
    seg = tuple(seg) or (False,) * nv
    skip = tuple(skip or (0,) * nr)
    parts_r = tuple(parts_r or (1,) * nr)
    parts_v = tuple(parts_v or (1,) * nv)
    r_total = rows[0].shape[0]
    tr = ROW_TILE
    nt = r_total // tr
    assert r_total % tr == 0 and (ctx_tiles > 0 or not any(seg)) and not any(h and s for h, s in zip(halo, skip))

    def tile_map(i):
        return lambda t: (jnp.maximum(t - skip[i], 0), 0)

    def row_specs():
        sp = []
        for i, r in enumerate(rows):
            w = r.shape[1]
            sp.append(pl.BlockSpec((tr, w), tile_map(i)))
            if halo[i]:
                k = tr // SUBLANES
                sp.append(pl.BlockSpec((SUBLANES, w), lambda t: (jnp.maximum(t * k - 1, 0), 0)))
                sp.append(pl.BlockSpec((SUBLANES, w), lambda t: (jnp.minimum((t + 1) * k, nt * k - 1), 0)))
        return sp

    def vec_spec(j):
        w = vecs[j].shape[-1]
        if seg[j]:
            return pl.BlockSpec((None, 1, w), lambda t: (jnp.where(t >= ctx_tiles, 1, 0), 0, 0))
        return pl.BlockSpec((1, w), lambda t: (0, 0))

    def row_args(rv):
        a = []
        for i in range(nr):
            a += [rv[i]] * 3 if halo[i] else [rv[i]]
        return a

    def load(refs, t):
        pos, rp = 0, []
        for i in range(nr):
            x = refs[pos][...].astype(F32)
            if halo[i]:
                xm, xp = _shift_rows(x, refs[pos + 1], refs[pos + 2], t, ctx_tiles, nt)
                rp.append(list(zip(_split(x, parts_r[i]), _split(xm, parts_r[i]), _split(xp, parts_r[i]))))
                pos += 3
            else:
                rp.append(_split(x, parts_r[i]))
                pos += 1
        vp = []
        for j in range(nv):
            vp.append(_split(refs[pos][...].astype(F32), parts_v[j]))
            pos += 1
        return rp, vp, refs[pos:]

    n_out, n_red = len(outs), len(reds)

    def fwd_call(*rv):
        def body(*refs):
            t = pl.program_id(0)
            rp, vp, rest = load(refs, t)
            o_parts, r_parts = fn(rp, vp)
            for k in range(n_out):
                rest[k][...] = _cat(o_parts[k]).astype(outs[k][1])
            for k in range(n_red):
                ref = rest[n_out + k]

                @pl.when(t == 0)
                def _():
                    ref[...] = r_parts[k]

                @pl.when(t > 0)
                def _():
                    ref[...] += r_parts[k]

        res = pl.pallas_call(
            body, grid=(nt,),
            in_specs=row_specs() + [vec_spec(j) for j in range(nv)],
            out_specs=[pl.BlockSpec((tr, o[0]), lambda t: (t, 0)) for o in outs]
            + [pl.BlockSpec((1, w), lambda t: (0, 0)) for w in reds],
            out_shape=[jax.ShapeDtypeStruct((r_total, o[0]), o[1]) for o in outs]
            + [jax.ShapeDtypeStruct((1, w), F32) for w in reds],
            compiler_params=_params(("arbitrary",)), name=name + "_f",
        )(*row_args(rv), *rv[nr:])
        return tuple(res)

    def bwd_call(rv, cts):
        def body(*refs):
            t = pl.program_id(0)
            rp, vp, rest = load(refs, t)
            ct_o = [_split(rest[k][...].astype(F32), outs[k][2]) for k in range(n_out)]
            ct_r = [rest[n_out + k][...] for k in range(n_red)]
            rest = rest[n_out + n_red:]
            _, vjp = jax.vjp(fn, rp, vp)
            d_rp, d_vp = vjp((ct_o, ct_r))
            pos = 0
            for i in range(nr):
                if halo[i]:
                    for c in range(3):
                        rest[pos + c][...] = _cat([p[c] for p in d_rp[i]])
                    pos += 3
                else:
                    rest[pos][...] = _cat(d_rp[i]).astype(rows[i].dtype)
                    pos += 1
            for j in range(nv):
                ref, val = rest[pos + j], _cat(d_vp[j])
                start = jnp.logical_or(t == 0, t == ctx_tiles) if seg[j] else t == 0

                @pl.when(start)
                def _():
                    ref[...] = val

                @pl.when(jnp.logical_not(start))
                def _():
                    ref[...] += val

        d_row_specs, d_row_shapes = [], []
        for i, r in enumerate(rows):
            w = r.shape[1]
            for _ in range(3 if halo[i] else 1):
                d_row_specs.append(pl.BlockSpec((tr, w), tile_map(i)))
                d_row_shapes.append(jax.ShapeDtypeStruct(r.shape, F32 if halo[i] else r.dtype))
        res = pl.pallas_call(
            body, grid=(nt,),
            in_specs=row_specs() + [vec_spec(j) for j in range(nv)]
            + [pl.BlockSpec((tr, o[0]), lambda t: (t, 0)) for o in outs]
            + [pl.BlockSpec((1, w), lambda t: (0, 0)) for w in reds],
            out_specs=d_row_specs + [vec_spec(j) for j in range(nv)],
            out_shape=d_row_shapes + [jax.ShapeDtypeStruct(v.shape, F32) for v in vecs],
            compiler_params=_params(("arbitrary",)), name=name + "_b",
        )(*row_args(rv), *rv[nr:], *cts)
        d_rows, pos = [], 0
        for i in range(nr):
            if halo[i]:
                d_rows.append(_unshift(res[pos], res[pos + 1], res[pos + 2], ctx_tiles * tr).astype(rows[i].dtype))
                pos += 3
            else:
                d_rows.append(res[pos])
                pos += 1
        return tuple(d_rows) + tuple(res[pos:])

    @jax.custom_vjp
    def f(*rv):
        return fwd_call(*rv)

    f.defvjp(lambda *rv: (fwd_call(*rv), rv), lambda rv, cts: bwd_call(rv, cts))
    return f(*rows, *vecs)


def _unshift(d, dm, dp, ctx_rows):
    r = d.shape[0]
    t = lax.broadcasted_iota(jnp.int32, (r, 1), 0)
    zero = jnp.zeros((1, d.shape[1]), d.dtype)
    from_m = jnp.concatenate([dm[1:], zero], axis=0)
    from_p = jnp.concatenate([zero, dp[:-1]], axis=0)
    from_m = jnp.where(t == ctx_rows - 1, 0.0, from_m)
    from_p = jnp.where(t == ctx_rows, 0.0, from_p)
    return d + from_m + from_p


def _sigmoid(x):
    return 0.5 * (jnp.tanh(0.5 * x) + 1.0)


def _silu(x):
    return x * _sigmoid(x)


def _softplus(x):
    return jnp.maximum(x, 0.0) + jnp.log(1.0 + jnp.exp(-jnp.abs(x)))


def _rms(x, g):
    return x * lax.rsqrt(jnp.mean(x * x, axis=-1, keepdims=True) + EPS) * g


def _fn_modulate(r, v):
    (x,), (g,), (sh,), (sc,) = r[0], v[0], v[1], v[2]
    return [[_rms(x, g) * (1.0 + sc) + sh]], []


def _fn_resmod(r, v):
    (x,), (y,) = r
    (gate,), (g,), (sh,), (sc,) = v
    xn = x + gate * y
    return [[xn], [_rms(xn, g) * (1.0 + sc) + sh]], []


def _fn_final(r, v):
    (x,), (y,), (tgt,) = r
    (gate,), (g,), (counts,) = v
    err = _rms(x + gate * y, g) - tgt
    row_loss = jnp.mean(err * err, axis=-1, keepdims=True)
    total = 0.5 * jnp.sum(row_loss, axis=0, keepdims=True)
    return [], [total * counts]


def _fn_dnprep(r, v):
    qkv, (ab,) = r
    w0, w1, w2, (alog,), (dtb,) = v
    out = [[], [], []]
    for n, (x, xm, xp) in enumerate(qkv):
        which = n // DN_HEADS
        y = _silu(xm * w0[n] + x * w1[n] + xp * w2[n])
        if which < 2:
            y = y * lax.rsqrt(jnp.sum(y * y, axis=-1, keepdims=True) + EPS)
        if which == 0:
            y = y * (HEAD_DIM ** -0.5)
        out[which].append(y)
    lane = lax.broadcasted_iota(jnp.int32, ab.shape, 1)
    g = -jnp.exp(alog) * _softplus(ab + dtb)
    gb = jnp.where(lane < N_DIRHEAD, g, jnp.where(lane < 2 * N_DIRHEAD, _sigmoid(ab), 0.0))
    return out + [[gb]], []


def _fn_dnpost(r, v):
    of, ob, z = r
    (g,) = v[0]
    return [[_rms(a + b, g) * _silu(c) for a, b, c in zip(of, ob, z)]], []


def _fn_sub(r, v):
    return [[r[0][0] - r[1][0]]], []


def _fn_scale(r, v):
    return [[r[0][0] * v[0][0]]], []


def _fn_shortconv(r, v):
    (xin, gb, gc), = r
    (w0,), (w1,), (w2,) = v
    u, um, up = (gc[k] * xin[k] for k in range(3))
    return [[gb[0] * (um * w0 + u * w1 + up * w2)]], []


def _fn_merge(r, v):
    gates, (ya,), (yb,), (yc,) = r
    return [[_sigmoid(gates[0]) * ya + _sigmoid(gates[1]) * yb + _sigmoid(gates[2]) * yc]], []


def _fn_swiglu(r, v):
    gate, up = r[0]
    return [[_silu(gate) * up]], []


def _dot3(a, b):
    (ah, al), (bh, bl) = _hi_lo(a), _hi_lo(b)
    dot = lambda x, y: lax.dot_general(x, y, (NN, ((), ())), preferred_element_type=F32)
    return dot(ah, bh) + (dot(ah, bl) + dot(al, bh))


def _bdot(a, b, dims):
    (ca,), (cb,) = dims
    return lax.dot_general(a.astype(MXU_DTYPE), b.astype(MXU_DTYPE), (((ca + 1,), (cb + 1,)), ((0,), (0,))),
                           preferred_element_type=F32)


def _hi_lo(a):
    hi = a.astype(MXU_DTYPE)
    return hi, (a - hi.astype(F32)).astype(MXU_DTYPE)


def _bdot3_raw(a, b, dims):
    (ah, al), (bh, bl) = _hi_lo(a), _hi_lo(b)
    return _bdot(ah, bh, dims) + (_bdot(ah, bl, dims) + _bdot(al, bh, dims))


@jax.custom_vjp
def _bdot3(a, b):
    return _bdot3_raw(a, b, NN)


_bdot3.defvjp(lambda a, b: (_bdot3_raw(a, b, NN), (a, b)),
              lambda res, ct: (_bdot3_raw(ct, res[1], NT), _bdot3_raw(res[0], ct, TN)))


def _inv_doubling(a):
    c = a.shape[-1]
    ii, jj = (lax.broadcasted_iota(jnp.int32, a.shape, d) for d in (1, 2))
    t = jnp.where(ii == jj, 1.0, 0.0) - a
    p = a
    for _ in range(int(math.log2(c)) - 1):
        p = _bdot3(p, p)
        t = t + _bdot3(t, p)
    return t


def _dn_gates(k4, gb):
    nb, c = N_DIRHEAD, k4.shape[1]
    k = jnp.concatenate([k4, k4], axis=0)
    lane = lax.broadcasted_iota(jnp.int32, gb.shape, 1)
    col = lambda j: jnp.sum(jnp.where(lane == j, gb, 0.0), axis=1, keepdims=True)
    g_col = jnp.concatenate([col(j)[None] for j in range(nb)], axis=0)
    b_col = jnp.concatenate([col(nb + j)[None] for j in range(nb)], axis=0)
    bi, ii, jj = (lax.broadcasted_iota(jnp.int32, (nb, c, c), a) for a in range(3))
    ahead = jnp.where(bi >= DN_HEADS, jj - ii, ii - jj)
    incl = ahead >= 0
    g_row = jnp.sum(jnp.where(ahead == 0, g_col, 0.0), axis=1, keepdims=True)
    gc_col = jnp.sum(jnp.where(incl, g_row, 0.0), axis=2, keepdims=True)
    gc_row = jnp.sum(jnp.where(ahead <= 0, g_col, 0.0), axis=1, keepdims=True)
    decay = jnp.where(incl, jnp.exp(jnp.where(incl, gc_col - gc_row, 0.0)), 0.0)
    return k, g_col, b_col, gc_col, decay, ahead > 0


def _dn_a(k4, gb):
    k, _, b_col, _, decay, strict = _dn_gates(k4, gb)
    return _bdot(k * b_col, k, NT) * jnp.where(strict, decay, 0.0)


def _dn_operands(q4, k4, v4, gb, t):
    k, g_col, b_col, gc_col, decay, _ = _dn_gates(k4, gb)
    q, v = (jnp.concatenate([a, a], axis=0) for a in (q4, v4))
    e_gc = jnp.exp(gc_col)
    u = _bdot3(t, v * b_col)
    w = _bdot3(t, k * b_col * e_gc)
    g_last = jnp.sum(g_col, axis=1, keepdims=True)
    k_state = k * jnp.exp(g_last - gc_col)
    a_qk = _bdot(q, k, NT) * decay
    return u, w, q * e_gc, k_state, a_qk, jnp.broadcast_to(jnp.exp(g_last), (N_DIRHEAD, 1, LANES))


def _dn_step(s, u, w, qd, ks, aqk, gl):
    v_new = u - _bdot(w, s, NN)
    o = _bdot(qd, s, NN) + _bdot(aqk, v_new, NN)
    return s * gl[:, :, :1] + _bdot(ks, v_new, TN), o


def _heads(x):
    return jnp.concatenate([x[None, :, h * HEAD_DIM:(h + 1) * HEAD_DIM] for h in range(DN_HEADS)], axis=0)


def _unheads(x):
    return jnp.concatenate([x[h] for h in range(x.shape[0])], axis=-1)


def _dn_rev(t, nc, n):
    return jnp.where(t < nc, nc - 1 - t, n - 1 - (t - nc))


PRE_CHUNKS = 2


def _pre_shapes(n):
    c, d, h = DN_CHUNK, HEAD_DIM, DN_HEADS
    shapes = [(n, h, c, d)] * 8 + [(n, h, c, c)] * 2 + [(n, h, 1, LANES)] * 2
    return shapes, [pl.BlockSpec((PRE_CHUNKS,) + s[1:], lambda t: (t, 0, 0, 0)) for s in shapes]


def _inverse_spec(n):
    shape = (n, N_DIRHEAD, DN_CHUNK, DN_CHUNK)
    return shape, pl.BlockSpec((PRE_CHUNKS,) + shape[1:], lambda t: (t, 0, 0, 0))


def _pre_row_specs():
    rows = PRE_CHUNKS * DN_CHUNK
    return pl.BlockSpec((rows, DN_HEADS * HEAD_DIM), lambda t: (t, 0)), pl.BlockSpec((rows, LANES), lambda t: (t, 0))


def _dn_pre_fwd(name, q, k, v, gb):
    n = q.shape[0] // DN_CHUNK
    shapes, specs = _pre_shapes(n)
    t_shape, t_spec = _inverse_spec(n)

    def body(q_ref, k_ref, v_ref, g_ref, *o_refs):
        for s in range(PRE_CHUNKS):
            rows = pl.ds(s * DN_CHUNK, DN_CHUNK)
            q4, k4, v4, gb_ = _heads(q_ref[rows, :]), _heads(k_ref[rows, :]), _heads(v_ref[rows, :]), g_ref[rows, :]
            t = _inv_doubling(_dn_a(k4, gb_))
            for i, r in enumerate(_dn_operands(q4, k4, v4, gb_, t)):
                o_refs[2 * i][s] = r[:DN_HEADS]
                o_refs[2 * i + 1][s] = r[DN_HEADS:]
            o_refs[-1][s] = t

    wide, narrow = _pre_row_specs()
    return pl.pallas_call(body, grid=(n // PRE_CHUNKS,), in_specs=[wide] * 3 + [narrow], out_specs=specs + [t_spec],
                          out_shape=[jax.ShapeDtypeStruct(s, F32) for s in shapes + [t_shape]],
                          compiler_params=_params(("parallel",)), name=name + "_pre_f")(q, k, v, gb)


def _dn_pre_bwd(name, q, k, v, gb, inv, cts):
    n = q.shape[0] // DN_CHUNK
    _, specs = _pre_shapes(n)
    _, t_spec = _inverse_spec(n)
    n_ct = len(specs)

    def body(q_ref, k_ref, v_ref, g_ref, t_ref, *refs):
        for s in range(PRE_CHUNKS):
            rows = pl.ds(s * DN_CHUNK, DN_CHUNK)
            ct = tuple(jnp.concatenate([refs[i][s], refs[i + 1][s]], axis=0) for i in range(0, n_ct, 2))
            q4, k4, v4, gb_, t = _heads(q_ref[rows, :]), _heads(k_ref[rows, :]), _heads(v_ref[rows, :]), g_ref[rows, :], t_ref[s]
            _, vjp = jax.vjp(_dn_operands, q4, k4, v4, gb_, t)
            dq, dk, dv, dg, dt = vjp(ct)
            da = -_bdot3_raw(_bdot3_raw(t, dt, TN), t, NT)
            _, vjp_a = jax.vjp(_dn_a, k4, gb_)
            dk_a, dg_a = vjp_a(da)
            for r, val in zip(refs[n_ct:], (_unheads(dq), _unheads(dk + dk_a), _unheads(dv), dg + dg_a)):
                r[rows, :] = val

    wide, narrow = _pre_row_specs()
    return pl.pallas_call(body, grid=(n // PRE_CHUNKS,), in_specs=[wide] * 3 + [narrow, t_spec] + specs,
                          out_specs=[wide] * 3 + [narrow],
                          out_shape=[jax.ShapeDtypeStruct(q.shape, F32)] * 3 + [jax.ShapeDtypeStruct(gb.shape, F32)],
                          compiler_params=_params(("parallel",)), name=name + "_pre_b")(q, k, v, gb, inv, *cts)


def _scan_specs(pre_shapes, fw, bw):
    maps = (lambda t: (fw(t), 0, 0, 0), lambda t: (bw(t), 0, 0, 0))
    return [pl.BlockSpec((None,) + s[1:], maps[i % 2]) for i, s in enumerate(pre_shapes)]


def _dn_scan_fwd(name, pre, nc):
    n = pre[0].shape[0]
    r = n * DN_CHUNK
    hd, nh, nb = HEAD_DIM, DN_HEADS, N_DIRHEAD
    shapes, _ = _pre_shapes(n)
    n_in = len(shapes)

    def body(*refs):
        ins, (of_ref, ob_ref, sf_ref, sb_ref, s_scr) = refs[:n_in], refs[n_in:]
        t = pl.program_id(0)

        @pl.when(t == 0)
        def _():
            s_scr[...] = jnp.zeros_like(s_scr)

        s = s_scr[...]
        sf_ref[...] = s[:nh]
        sb_ref[...] = s[nh:]
        args = [jnp.concatenate([ins[i][...], ins[i + 1][...]], axis=0) for i in range(0, n_in, 2)]
        s2, o = _dn_step(s, *args)
        s_scr[...] = s2
        of_ref[...] = _unheads(o[:nh])
        ob_ref[...] = _unheads(o[nh:])

    fw = lambda t: t
    bw = lambda t: _dn_rev(t, nc, n)
    wide, st = (DN_CHUNK, nh * hd), (None, nh, hd, hd)
    return pl.pallas_call(
        body, grid=(n,), in_specs=_scan_specs(shapes, fw, bw),
        out_specs=[pl.BlockSpec(wide, lambda t: (fw(t), 0)), pl.BlockSpec(wide, lambda t: (bw(t), 0)),
                   pl.BlockSpec(st, lambda t: (fw(t), 0, 0, 0)), pl.BlockSpec(st, lambda t: (bw(t), 0, 0, 0))],
        out_shape=[jax.ShapeDtypeStruct((r, nh * hd), F32)] * 2 + [jax.ShapeDtypeStruct((n, nh, hd, hd), F32)] * 2,
        scratch_shapes=[pltpu.VMEM((nb, hd, hd), F32)],
        compiler_params=_params(("arbitrary",)), name=name + "_scan_f",
    )(*pre)


def _dn_scan_bwd(name, pre, sall_f, sall_b, do_f, do_b, nc):
    n = pre[0].shape[0]
    hd, nh, nb = HEAD_DIM, DN_HEADS, N_DIRHEAD
    shapes, _ = _pre_shapes(n)
    n_in = len(shapes)

    def body(*refs):
        ins, (sf_ref, sb_ref, dof_ref, dob_ref) = refs[:n_in], refs[n_in:n_in + 4]
        outs, ds_scr = refs[n_in + 4:2 * n_in + 4], refs[2 * n_in + 4]
        t = pl.program_id(0)

        @pl.when(t == 0)
        def _():
            ds_scr[...] = jnp.zeros_like(ds_scr)

        args = [jnp.concatenate([ins[i][...], ins[i + 1][...]], axis=0) for i in range(0, n_in, 2)]
        s = jnp.concatenate([sf_ref[...], sb_ref[...]], axis=0)
        do = jnp.concatenate([_heads(dof_ref[...]), _heads(dob_ref[...])], axis=0)
        _, vjp = jax.vjp(_dn_step, s, *args)
        cts = vjp((ds_scr[...], do))
        ds_scr[...] = cts[0]
        for i, ct in enumerate(cts[1:]):
            outs[2 * i][...] = ct[:nh]
            outs[2 * i + 1][...] = ct[nh:]

    fw = lambda t: n - 1 - t
    bw = lambda t: _dn_rev(n - 1 - t, nc, n)
    wide, st = (DN_CHUNK, nh * hd), (None, nh, hd, hd)
    return tuple(pl.pallas_call(
        body, grid=(n,),
        in_specs=_scan_specs(shapes, fw, bw)
        + [pl.BlockSpec(st, lambda t: (fw(t), 0, 0, 0)), pl.BlockSpec(st, lambda t: (bw(t), 0, 0, 0)),
           pl.BlockSpec(wide, lambda t: (fw(t), 0)), pl.BlockSpec(wide, lambda t: (bw(t), 0))],
        out_specs=_scan_specs(shapes, fw, bw), out_shape=[jax.ShapeDtypeStruct(s, F32) for s in shapes],
        scratch_shapes=[pltpu.VMEM((nb, hd, hd), F32)],
        compiler_params=_params(("arbitrary",)), name=name + "_scan_b",
    )(*pre, sall_f, sall_b, do_f, do_b))


def deltanet(name, q, k, v, gb, ctx_rows):
    nc = ctx_rows // DN_CHUNK

    @jax.custom_vjp
    def pre(q, k, v, gb):
        return tuple(_dn_pre_fwd(name, q, k, v, gb)[:-1])

    def pre_fwd(*a):
        *ops, inv = _dn_pre_fwd(name, *a)
        return tuple(ops), (a, inv)

    pre.defvjp(pre_fwd, lambda res, cts: tuple(_dn_pre_bwd(name, *res[0], res[1], cts)))

    @jax.custom_vjp
    def scan(*ops):
        return tuple(_dn_scan_fwd(name, ops, nc)[:2])

    def scan_fwd(*ops):
        of, ob, sf, sb = _dn_scan_fwd(name, ops, nc)
        return (of, ob), (ops, sf, sb)

    scan.defvjp(scan_fwd, lambda res, cts: _dn_scan_bwd(name, res[0], res[1], res[2], cts[0], cts[1], nc))
    return scan(*pre(q, k, v, gb))


def _box_matrix(l, w):
    lo, hi = w // 2, w - 1 - w // 2
    pos = np.arange(l)
    start, end = np.clip(pos - lo, 0, l), np.clip(pos + hi + 1, 0, l)
    col = np.arange(l)[None, :]
    return ((col >= start[:, None]) & (col < end[:, None])) / (end - start)[:, None].astype(np.float64)


def _pool_matrices(ctx_rows, grid_rows):
    assert ctx_rows == ROW_TILE and ROW_TILE % GRID_W == 0
    ctx = np.stack([_box_matrix(ctx_rows, w) for w in POOL_WINDOWS])
    cols = np.stack([np.kron(np.eye(ROW_TILE // GRID_W), _box_matrix(GRID_W, w)) for w in POOL_WINDOWS])
    rows = np.stack([_box_matrix(grid_rows, w) for w in POOL_WINDOWS])[None]
    return np.stack([ctx, cols]).astype(np.float32), rows.astype(np.float32)


def _pool_apply(name, x, mats, group_w, seg_tiles, lane_tile):
    r, w = x.shape
    b = mats.shape[-1]
    ng = mats.shape[1]
    period = ng * group_w

    def body(x_ref, m_ref, o_ref):
        xv = x_ref[...].astype(F32)
        lane = lax.broadcasted_iota(jnp.int32, xv.shape, 1)
        grp = (lane % period) // group_w
        acc = jnp.zeros_like(xv)
        for g in range(ng):
            acc = acc + jnp.where(grp == g, _dot3(m_ref[g], xv), 0.0)
        o_ref[...] = acc

    return pl.pallas_call(
        body, grid=(r // b, w // lane_tile),
        in_specs=[pl.BlockSpec((b, lane_tile), lambda i, j: (i, j)),
                  pl.BlockSpec((None, ng, b, b), lambda i, j: (jnp.where(i >= seg_tiles, 1, 0) if mats.shape[0] > 1 else 0, 0, 0, 0))],
        out_specs=pl.BlockSpec((b, lane_tile), lambda i, j: (i, j)),
        out_shape=jax.ShapeDtypeStruct(x.shape, F32),
        compiler_params=_params(("parallel", "parallel")), name=name,
    )(x, mats)


def pool_means(name, u, ctx_rows):
    r, pw = u.shape
    grid_rows = (r - ctx_rows) // GRID_W
    gw = pw // len(POOL_WINDOWS)
    m1, m2 = _pool_matrices(ctx_rows, grid_rows)
    lane_tile = min(2048, GRID_W * pw)

    def apply(x, a1, a2, tag):
        y = _pool_apply(name + tag + "1", x, jnp.asarray(a1), gw, ctx_rows // ROW_TILE, pw)
        lat = y[ctx_rows:].reshape(grid_rows, GRID_W * pw)
        lat = _pool_apply(name + tag + "2", lat, jnp.asarray(a2), gw, 0, lane_tile)
        return jnp.concatenate([y[:ctx_rows], lat.reshape(r - ctx_rows, pw)], axis=0)

    @jax.custom_vjp
    def f(u):
        return apply(u, m1, m2, "_f")

    tr = lambda m: np.ascontiguousarray(np.swapaxes(m, -1, -2))
    f.defvjp(lambda u: (apply(u, m1, m2, "_f"), None), lambda _, ct: (apply(ct, tr(m1), tr(m2), "_b").astype(u.dtype),))
    return f(u)


def _ew(name, fn, *xs, n_out=1):
    shapes = jax.eval_shape(lambda *a: fn(*a), *xs)
    shapes = shapes if isinstance(shapes, (tuple, list)) else (shapes,)

    def body(*refs):
        res = fn(*[r[...] for r in refs[:len(xs)]])
        res = res if isinstance(res, (tuple, list)) else (res,)
        for r, o in zip(res, refs[len(xs):]):
            o[...] = r

    out = pl.pallas_call(body, out_shape=[jax.ShapeDtypeStruct(s.shape, s.dtype) for s in shapes],
                         compiler_params=_params(), name=name)(*xs)
    return out[0] if len(shapes) == 1 else tuple(out)


def _adamw_math(w, g, m, v):
    m2 = ADAM_B1 * m + (1.0 - ADAM_B1) * g
    v2 = ADAM_B2 * v + (1.0 - ADAM_B2) * (g * g)
    m_hat = m2 / (1.0 - ADAM_B1 ** ADAM_STEP)
    v_hat = v2 / (1.0 - ADAM_B2 ** ADAM_STEP)
    delta = -ADAM_LR * (m_hat / (jnp.sqrt(v_hat) + ADAM_EPS) + ADAM_WD * w)
    return delta, m2, v2


def adamw(name, w, g, m, v):
    r, c = w.shape
    tr = _pick(r, (256, 128, 64, 32, 16, 8))

    def body(w_ref, g_ref, m_ref, v_ref, d_ref, m2_ref, v2_ref):
        d_ref[...], m2_ref[...], v2_ref[...] = _adamw_math(w_ref[...], g_ref[...], m_ref[...], v_ref[...])

    spec = pl.BlockSpec((tr, c), lambda i: (i, 0))
    return pl.pallas_call(body, grid=(r // tr,), in_specs=[spec] * 4, out_specs=[spec] * 3,
                          out_shape=[jax.ShapeDtypeStruct(w.shape, F32)] * 3,
                          compiler_params=_params(("parallel",)), name=name)(w, g, m, v)


def _sum_rows(name, xs, out_dtype=F32):
    r, c = xs[0].shape
    tr = _pick(r, (512, 256, 128, 64, 32, 16, 8))

    def body(*refs):
        acc = refs[0][...].astype(F32)
        for ref in refs[1:-1]:
            acc = acc + ref[...].astype(F32)
        refs[-1][...] = acc.astype(out_dtype)

    spec = pl.BlockSpec((tr, c), lambda i: (i, 0))
    return pl.pallas_call(body, grid=(r // tr,), in_specs=[spec] * len(xs), out_specs=spec,
                          out_shape=jax.ShapeDtypeStruct((r, c), out_dtype),
                          compiler_params=_params(("parallel",)), name=name)(*xs)


def _place():
    return lax.axis_index("x"), lax.axis_index("y"), lax.axis_index("c")


def _chip_peers(x, y, c):
    return [(1 - x, y, c), (x, 1 - y, c), (1 - x, 1 - y, c)]


def all_gather8(name, block):
    m_per, n = block.shape

    def body(x_ref, out_ref, send_sems, recv_sems, local_sem):
        x, y, c = _place()
        me, sibling = (x, y, c), (x, y, 1 - c)
        chips = [(1 - x, y), (x, 1 - y), (1 - x, 1 - y)]

        def rows(px, py, pc):
            return out_ref.at[pl.ds((4 * px + 2 * py + pc) * m_per, m_per), :]

        def copy(k, blk, to, src=None):
            return pltpu.make_async_remote_copy(
                src_ref=rows(*blk) if src is None else src, dst_ref=rows(*blk),
                send_sem=send_sems.at[k], recv_sem=recv_sems.at[k], device_id=to, device_id_type=MESH_ID)

        mine = pltpu.make_async_copy(x_ref, rows(*me), local_sem)
        mine.start()
        first = [copy(0, me, sibling, src=x_ref)]
        first += [copy(1 + j, me, (*chip, c), src=x_ref) for j, chip in enumerate(chips)]
        for cp in first:
            cp.start()
        passed = [copy(4 + j, (*chip, c), sibling) for j, chip in enumerate(chips)]
        for j, chip in enumerate(chips):
            copy(1 + j, (*chip, c), me).wait_recv()
            passed[j].start()
        copy(0, sibling, me).wait_recv()
        for j, chip in enumerate(chips):
            copy(4 + j, (*chip, 1 - c), me).wait_recv()
        for cp in first + passed:
            cp.wait_send()
        mine.wait()

    return pl.pallas_call(
        body, out_shape=jax.ShapeDtypeStruct((8 * m_per, n), block.dtype),
        in_specs=[pl.BlockSpec(memory_space=pltpu.VMEM)], out_specs=pl.BlockSpec(memory_space=pltpu.VMEM),
        scratch_shapes=[pltpu.SemaphoreType.DMA((7,)), pltpu.SemaphoreType.DMA((7,)), pltpu.SemaphoreType.DMA],
        compiler_params=_params(), name=name,
    )(block)


def gather_chips(name, shard):
    def body(x_ref, out_ref, send_sems, recv_sems, local_sem):
        x, y, c = _place()
        mine = pltpu.make_async_copy(x_ref, out_ref.at[2 * x + y], local_sem)
        mine.start()
        copies = []
        for p, peer in enumerate(_chip_peers(x, y, c)):
            cp = pltpu.make_async_remote_copy(
                src_ref=x_ref, dst_ref=out_ref.at[2 * x + y], send_sem=send_sems.at[p], recv_sem=recv_sems.at[p],
                device_id=peer, device_id_type=MESH_ID)
            cp.start()
            copies.append(cp)
        for p, (px, py, _) in enumerate(_chip_peers(x, y, c)):
            pltpu.make_async_remote_copy(
                src_ref=x_ref, dst_ref=out_ref.at[2 * px + py], send_sem=send_sems.at[p], recv_sem=recv_sems.at[p],
                device_id=(px, py, c), device_id_type=MESH_ID).wait_recv()
        for cp in copies:
            cp.wait_send()
        mine.wait()

    hbm = pl.BlockSpec(memory_space=pltpu.HBM)
    return pl.pallas_call(
        body, out_shape=jax.ShapeDtypeStruct((4,) + shard.shape, shard.dtype), in_specs=[hbm], out_specs=hbm,
        scratch_shapes=[pltpu.SemaphoreType.DMA((3,)), pltpu.SemaphoreType.DMA((3,)), pltpu.SemaphoreType.DMA],
        compiler_params=_params(), name=name,
    )(shard)


def swap_sibling(name, block):
    def body(x_ref, out_ref, send_sem, recv_sem):
        x, y, c = _place()
        cp = pltpu.make_async_remote_copy(src_ref=x_ref, dst_ref=out_ref, send_sem=send_sem, recv_sem=recv_sem,
                                          device_id=(x, y, 1 - c), device_id_type=MESH_ID)
        cp.start()
        cp.wait()

    hbm = pl.BlockSpec(memory_space=pltpu.HBM)
    return pl.pallas_call(
        body, out_shape=jax.ShapeDtypeStruct(block.shape, block.dtype), in_specs=[hbm], out_specs=hbm,
        scratch_shapes=[pltpu.SemaphoreType.DMA, pltpu.SemaphoreType.DMA],
        compiler_params=_params(), name=name,
    )(block)


def scatter_chips(name, pieces):
    def body(x_ref, out_ref, send_sems, recv_sems):
        x, y, c = _place()
        copies = []
        for p, (px, py, pc) in enumerate(_chip_peers(x, y, c)):
            cp = pltpu.make_async_remote_copy(
                src_ref=x_ref.at[2 * px + py], dst_ref=out_ref.at[p], send_sem=send_sems.at[p], recv_sem=recv_sems.at[p],
                device_id=(px, py, pc), device_id_type=MESH_ID)
            cp.start()
            copies.append(cp)
        for cp in copies:
            cp.wait()

    hbm = pl.BlockSpec(memory_space=pltpu.HBM)
    return pl.pallas_call(
        body, out_shape=jax.ShapeDtypeStruct((3,) + pieces.shape[1:], pieces.dtype), in_specs=[hbm], out_specs=hbm,
        scratch_shapes=[pltpu.SemaphoreType.DMA((3,)), pltpu.SemaphoreType.DMA((3,))],
        compiler_params=_params(), name=name,
    )(pieces)


def _sum_devices(name, got, fold=False):
    def body(a_ref, *o_refs):
        acc = a_ref[0]
        for i in range(1, N_DEV):
            acc = acc + a_ref[i]
        o_refs[0][...] = acc
        if fold:
            o_refs[1][...] = acc + pltpu.roll(acc, SUBLANES // 2, 0)

    shape = jax.ShapeDtypeStruct(got.shape[1:], F32)
    out = pl.pallas_call(body, out_shape=[shape] * (2 if fold else 1), compiler_params=_params(), name=name)(got)
    return out if fold else out[0]


def _w_in_bounds(d, pw, scw):
    off_z = 3 * DN_WIDTH
    off_a = off_z + DN_WIDTH
    off_pool = off_a + 2 * N_DIRHEAD
    off_sc = off_pool + pw
    off_gate = off_sc + 3 * scw
    return (0, off_z, off_a, off_pool, off_sc, off_gate, off_gate + 3 * d)


def _misc_widths(pw, scw):
    return (DN_WIDTH, pw, 3 * scw, LANES)


def split_matrices(full, pw, scw):
    depth, d = full["w_in"].shape[:2]
    b = _w_in_bounds(d, pw, scw)
    out = {k: [] for k in ("qkv", "gate", "misc", "gu", "br_a", "br_b", "br_c", "o", "down")}
    for l in range(depth):
        w = full["w_in"][l]
        out["qkv"].append(w[:, b[0]:b[1]])
        out["gate"].append(w[:, b[5]:b[6]])
        out["misc"].append(jnp.concatenate([w[:, b[1]:b[2]], w[:, b[3]:b[5]], _pad_lanes(w[:, b[2]:b[3]])], axis=1))
        out["gu"].append(full["w_gu"][l])
        for k in ("br_a", "br_b", "br_c", "o", "down"):
            out[k].append(full["w_" + k][l])
    return out


def join_matrix_grads(g, pw, scw):
    depth = len(g["qkv"])
    n_z, n_rest, n_ab = DN_WIDTH, pw + 3 * scw, 2 * N_DIRHEAD
    w_in = []
    for l in range(depth):
        m = g["misc"][l]
        w_in.append(jnp.concatenate([g["qkv"][l], m[:, :n_z], m[:, n_z + n_rest:n_z + n_rest + n_ab],
                                     m[:, n_z:n_z + n_rest], g["gate"][l]], axis=1))
    out = {"w_in": jnp.stack(w_in), "w_gu": jnp.stack(g["gu"])}
    for k in ("br_a", "br_b", "br_c", "o", "down"):
        out["w_" + k] = jnp.stack(g[k])
    return {k: v.astype(MXU_DTYPE) for k, v in out.items()}


def split_cols(x, widths):
    edges = np.cumsum((0,) + tuple(widths))

    def cut(x):
        return tuple(x[:, a:b] for a, b in zip(edges[:-1], edges[1:]))

    f = jax.custom_vjp(cut)
    f.defvjp(lambda x: (cut(x), None), lambda _, cts: (jnp.concatenate(cts, axis=1),))
    return f(x)


def _row(v):
    return v.reshape(1, -1)


def _pad_lanes(v, width=LANES):
    return jnp.pad(v, ((0, 0), (0, width - v.shape[1])))


def _block_diag(blocks):
    g, n, _ = blocks.shape
    out = jnp.zeros((g * n, g * n), blocks.dtype)
    for i in range(g):
        out = out.at[i * n:(i + 1) * n, i * n:(i + 1) * n].set(blocks[i])
    return out


def local_loss(p, carriers, x, mod_lat, mod_ctx, wts, ctx, target):
    ctx_rows, d = ctx.shape
    depth = len(wts["qkv"])
    pw, scw = p["pool_scale"].shape[1], p["sc_conv_w"].shape[2]
    ff = wts["down"][0].shape[0]
    ct = ctx_rows // ROW_TILE
    xs = jnp.concatenate([ctx, x], axis=0)
    seg = lambda l, k: jnp.stack([mod_ctx[l, k], mod_lat[l, k]]).reshape(2, 1, d)
    dn = gate2 = None
    for l in range(depth):
        sh1, sc1, g1, sh2, sc2, g2 = (seg(l, k) for k in range(6))
        tag = f"l{l}_"
        lin = lambda key, a: mm(tag + key, a, wts[key][l], carriers[key][l])
        if l == 0:
            (h1,) = rowwise(tag + "mod", _fn_modulate, [xs], [_row(p["norm1_g"][l]), sh1, sc1], seg=(False, True, True),
                            outs=[(d, MXU_DTYPE, 1)], ctx_tiles=ct)
        else:
            xs, h1 = rowwise(tag + "resmod1", _fn_resmod, [xs, dn], [gate2, _row(p["norm1_g"][l]), sh1, sc1],
                             seg=(True, False, True, True), outs=[(d, F32, 1), (d, MXU_DTYPE, 1)], ctx_tiles=ct)
        fan = ("qkv", "gate", "misc")
        p_qkv, p_gate, p_misc = mm_fanout(tag + "in", h1, [wts[k][l] for k in fan], [carriers[k][l] for k in fan])
        p_z, p_pool, p_sc, p_ab = split_cols(p_misc, _misc_widths(pw, scw))
        cw = p["dn_conv_w"][l]
        q, k, v, gb = rowwise(
            tag + "dnprep", _fn_dnprep, [p_qkv, p_ab],
            [cw[0:1], cw[1:2], cw[2:3], _pad_lanes(p["dn_a_log"][l].reshape(1, -1)), _pad_lanes(p["dn_dt_bias"][l].reshape(1, -1))],
            halo=(True, False), parts_r=(3 * DN_HEADS, 1), parts_v=(3 * DN_HEADS,) * 3 + (1, 1),
            outs=[(DN_WIDTH, F32, DN_HEADS)] * 3 + [(LANES, F32, 1)], ctx_tiles=ct)
        o_f, o_b = deltanet(tag + "dn", q, k, v, gb, ctx_rows)
        (oz,) = rowwise(tag + "dnpost", _fn_dnpost, [o_f, o_b, p_z], [_row(p["dn_norm_g"][l])], parts_r=(DN_HEADS,) * 3,
                        outs=[(DN_WIDTH, MXU_DTYPE, DN_HEADS)], ctx_tiles=ct)
        y_a = lin("br_a", oz)
        means = pool_means(tag + "box", p_pool, ctx_rows)
        (dpool,) = rowwise(tag + "poolsub", _fn_sub, [means, p_pool], [], outs=[(pw, MXU_DTYPE, 1)], ctx_tiles=ct)
        pool_mat = _block_diag(p["pool_w"][l])
        mixed = mm(tag + "poolw", dpool, pool_mat.astype(MXU_DTYPE), pool_mat)
        (yb_in,) = rowwise(tag + "poolscale", _fn_scale, [mixed], [_row(p["pool_scale"][l])], outs=[(pw, MXU_DTYPE, 1)], ctx_tiles=ct)
        y_b = lin("br_b", yb_in)
        sw = p["sc_conv_w"][l]
        (yc_in,) = rowwise(tag + "sconv", _fn_shortconv, [p_sc], [sw[0:1], sw[1:2], sw[2:3]], halo=(True,), parts_r=(3,),
                           outs=[(scw, MXU_DTYPE, 1)], ctx_tiles=ct)
        y_c = lin("br_c", yc_in)
        (y,) = rowwise(tag + "merge", _fn_merge, [p_gate, y_a, y_b, y_c], [], parts_r=(3, 1, 1, 1),
                       outs=[(d, MXU_DTYPE, 1)], ctx_tiles=ct)
        mix = lin("o", y)
        xs, h2 = rowwise(tag + "resmod2", _fn_resmod, [xs, mix], [g1, _row(p["norm2_g"][l]), sh2, sc2],
                         seg=(True, False, True, True), outs=[(d, F32, 1), (d, MXU_DTYPE, 1)], ctx_tiles=ct)
        (act,) = rowwise(tag + "swiglu", _fn_swiglu, [lin("gu", h2)], [], parts_r=(2,),
                         outs=[(ff, MXU_DTYPE, 1)], ctx_tiles=ct)
        dn = lin("down", act)
        gate2 = g2
    counts = jnp.concatenate([jnp.zeros((1, 1, LANES), F32), jnp.ones((1, 1, LANES), F32)])
    (total,) = rowwise("final", _fn_final, [xs, dn, target], [gate2, _row(p["final_norm_g"]), counts],
                       seg=(True, False, True), reds=[LANES], ctx_tiles=ct, skip=(0, 0, ct))
    return total[0, 0]


BIG = ("w_in", "w_br_a", "w_br_b", "w_br_c", "w_o", "w_gu", "w_down")
ROW_SHARDED = ("w_o", "w_down")
SMALL_REPL = ("norm1_g", "norm2_g", "dn_a_log", "dn_dt_bias", "dn_norm_g", "pool_w", "pool_scale", "final_norm_g")
SMALL_SHARD = ("dn_conv_w", "sc_conv_w")
WEIGHTS = ("c_ctx", "w_ada", "b_ada", "norm1_g", "norm2_g", "w_in", "dn_conv_w", "dn_a_log", "dn_dt_bias", "dn_norm_g", "pool_w",
           "pool_scale", "sc_conv_w", "w_br_a", "w_br_b", "w_br_c", "w_o", "w_gu", "w_down", "final_norm_g")
N_CHIPS, N_DEV = 4, 8
COMM_COLS = 1024


def _pack(arrays, rows_multiple, cols=COMM_COLS, dtype=F32):
    size = sum(int(np.prod(a.shape)) for a in arrays)
    rows = -(-size // (cols * rows_multiple)) * rows_multiple
    tail = [jnp.zeros((rows * cols - size,), dtype)] if rows * cols > size else []
    return jnp.concatenate([a.astype(dtype).reshape(-1) for a in arrays] + tail).reshape(rows, cols)


def _unpack(flat, shapes):
    out, pos = [], 0
    for s in shapes:
        n = int(np.prod(s))
        out.append(flat[pos:pos + n].reshape(s))
        pos += n
    return out


def _join_shards(parts, name):
    _, l, a, b = parts.shape
    if name in ROW_SHARDED:
        return jnp.moveaxis(parts, 0, 1).reshape(l, N_CHIPS * a, b)
    return jnp.moveaxis(parts, 0, 2).reshape(l, a, N_CHIPS * b)


def _cut_shard(full, name, j):
    if name in ROW_SHARDED:
        n = full.shape[1] // N_CHIPS
        return full[:, j * n:(j + 1) * n, :]
    n = full.shape[2] // N_CHIPS
    return full[:, :, j * n:(j + 1) * n]


def _dsilu(x):
    s = _sigmoid(x)
    return s + x * s * (1.0 - s)


def kernel(x, c, ctx, c_ctx, w_ada, b_ada, norm1_g, norm2_g, w_in, dn_conv_w, dn_a_log, dn_dt_bias, dn_norm_g, pool_w, pool_scale, sc_conv_w, w_br_a, w_br_b, w_br_c, w_o, w_gu, w_down, final_norm_g, loss_target, m_c_ctx, m_w_ada, m_b_ada, m_norm1_g, m_norm2_g, m_w_in, m_dn_conv_w, m_dn_a_log, m_dn_dt_bias, m_dn_norm_g, m_pool_w, m_pool_scale, m_sc_conv_w, m_w_br_a, m_w_br_b, m_w_br_c, m_w_o, m_w_gu, m_w_down, m_final_norm_g, v_c_ctx, v_w_ada, v_b_ada, v_norm1_g, v_norm2_g, v_w_in, v_dn_conv_w, v_dn_a_log, v_dn_dt_bias, v_dn_norm_g, v_pool_w, v_pool_scale, v_sc_conv_w, v_w_br_a, v_w_br_b, v_w_br_c, v_w_o, v_w_gu, v_w_down, v_final_norm_g):
    given = dict(locals())
    ix, iy, ic = _place()
    chip, dev = 2 * ix + iy, 4 * ix + 2 * iy + ic
    d = x.shape[-1]
    depth = w_in.shape[0]
    ada_cols = w_ada.shape[2]
    spare = 2 * SUBLANES - N_DEV - 1

    got0 = all_gather8("gather_cond", _pack([c, dn_conv_w, sc_conv_w], SUBLANES)).reshape(N_DEV, -1)
    c_all = got0[:, :d]
    taps = [_unpack(got0[2 * j, d:], [dn_conv_w.shape, sc_conv_w.shape]) for j in range(N_CHIPS)]
    full = {"dn_conv_w": jnp.concatenate([t[0] for t in taps], axis=-1),
            "sc_conv_w": jnp.concatenate([t[1] for t in taps], axis=-1)}
    big_shapes = [given[n].shape for n in BIG]
    mine = _pack([given[n] for n in BIG], 4 * SUBLANES, dtype=MXU_DTYPE)
    mine = mine.reshape(2, mine.shape[0] // 2, COMM_COLS)
    got_half = gather_chips("gather_weights", lax.dynamic_index_in_dim(mine, ic, 0, keepdims=False))
    other_half = swap_sibling("swap_weights", got_half)
    shards = jnp.stack([jnp.where(ic == 0, got_half, other_half), jnp.where(ic == 0, other_half, got_half)], axis=1)
    shards = shards.reshape(N_CHIPS, -1)
    per_chip = [_unpack(shards[j], big_shapes) for j in range(N_CHIPS)]
    mats = {n: _join_shards(jnp.stack([per_chip[j][i] for j in range(N_CHIPS)]), n) for i, n in enumerate(BIG)}
    for n in SMALL_REPL:
        full[n] = given[n]
    pw, scw = pool_scale.shape[1], full["sc_conv_w"].shape[2]
    wts = split_matrices(mats, pw, scw)
    carriers = {k: [jnp.zeros(w.shape, F32) for w in ws] for k, ws in wts.items()}

    cond = jnp.concatenate([c_all, c_ctx[None], jnp.zeros((spare, d), F32)])
    s_cond = _ew("silu_cond", _silu, cond)
    mod_cols = jnp.concatenate([_mm(f"ada{l}_f", s_cond, w_ada[l], NN) for l in range(depth)], axis=1)
    mod_got = all_gather8("gather_mod", mod_cols).reshape(N_DEV, 2 * SUBLANES, depth, ada_cols)
    mod_all = jnp.concatenate([mod_got[2 * j] for j in range(N_CHIPS)], axis=-1) + b_ada[None]
    mod_lat = lax.dynamic_index_in_dim(mod_all, dev, 0, keepdims=False).reshape(depth, 6, d)
    mod_ctx = mod_all[N_DEV].reshape(depth, 6, d)

    loss_local, (g_full, g_mats, grad_x, g_mod_lat, g_mod_ctx) = jax.value_and_grad(local_loss, argnums=(0, 1, 2, 3, 4))(
        full, carriers, x[0], mod_lat, mod_ctx, wts, ctx[0], loss_target[0])
    g_full.update(join_matrix_grads(g_mats, pw, scw))
    loss = lax.psum(loss_local, ("x", "y", "c"))

    dmod_cols = (2 * depth * 6 * d) // SUBLANES
    dmod = all_gather8("gather_dmod", _pack([g_mod_lat, g_mod_ctx], SUBLANES, cols=dmod_cols))
    dmod = dmod.reshape(N_DEV, SUBLANES, dmod_cols)
    dmod_sum, dmod_fold = _sum_devices("reduce_dmod", dmod, fold=True)
    half_rows = SUBLANES // 2
    grad_b_ada = dmod_fold[:half_rows].reshape(depth, 6 * d)
    dctx_sum = dmod_sum[half_rows:].reshape(1, depth, 6 * d)
    d9 = jnp.concatenate([dmod[:, :half_rows].reshape(N_DEV, depth, 6 * d), dctx_sum, jnp.zeros((spare, depth, 6 * d), F32)])
    d9 = lax.dynamic_slice_in_dim(d9, chip * ada_cols, ada_cols, axis=2)
    grad_w_ada = jnp.stack([_mm(f"ada{l}_dw", s_cond, d9[:, l], TN) for l in range(depth)])
    ds_cond = [_mm(f"ada{l}_da", d9[:, l], w_ada[l], NT) for l in range(depth)]
    dsilu_part = _sum_rows("sum_dcond", ds_cond)[N_DEV]

    small_names = SMALL_REPL + SMALL_SHARD
    small_grads = [dsilu_part] + [g_full[n] for n in small_names]
    small_shapes = [a.shape for a in small_grads]
    n_small = sum(int(np.prod(s)) for s in small_shapes)
    cols = -(-n_small // (SUBLANES * LANES)) * LANES
    got2 = all_gather8("gather_small", _pack(small_grads, SUBLANES, cols=cols)).reshape(N_DEV, SUBLANES, cols)
    small_sum = _unpack(_sum_devices("reduce_small", got2).reshape(-1), small_shapes)
    grads = dict(zip(small_names, small_sum[1:]))
    grads["c_ctx"] = _ew("dsilu", lambda g, z: 0.5 * g * _dsilu(z), _row(small_sum[0]), _row(c_ctx)).reshape(-1)
    grads["w_ada"], grads["b_ada"] = grad_w_ada, grad_b_ada
    for n in SMALL_SHARD:
        width = given[n].shape[-1]
        grads[n] = lax.dynamic_slice_in_dim(grads[n], chip * width, width, axis=-1)

    pieces = jnp.stack([_pack([_cut_shard(g_full[n], n, j) for n in BIG], 2 * LANES, dtype=MXU_DTYPE) for j in range(N_CHIPS)])
    rows_h = pieces.shape[1] // 2
    halves = pieces.reshape(N_CHIPS, 2, rows_h, COMM_COLS)
    keep = lax.dynamic_index_in_dim(halves, ic, 1, keepdims=False).reshape(N_CHIPS * rows_h, COMM_COLS)
    give = lax.dynamic_index_in_dim(halves, 1 - ic, 1, keepdims=False).reshape(N_CHIPS * rows_h, COMM_COLS)
    pair = _sum_rows("sum_pair", [keep, swap_sibling("swap_halves", give)], out_dtype=MXU_DTYPE)
    pair = pair.reshape(N_CHIPS, rows_h, COMM_COLS)
    got3 = scatter_chips("scatter_pieces", pair)
    own = lax.dynamic_index_in_dim(pair, chip, 0, keepdims=False)
    half = _sum_rows("sum_chips", [own, got3[0], got3[1], got3[2]])
    other = swap_sibling("swap_reduced", half)
    lo = jnp.where(ic == 0, half, other)
    hi = jnp.where(ic == 0, other, half)
    for n, g in zip(BIG, _unpack(jnp.concatenate([lo, hi]).reshape(-1), big_shapes)):
        grads[n] = g

    delta, new_m, new_v = {}, {}, {}
    large = BIG + ("w_ada",)
    for n in large:
        shape = given[n].shape
        flat = [a.reshape(-1, shape[-1]) for a in (given[n], grads[n], given["m_" + n], given["v_" + n])]
        res = adamw("adamw_" + n, *flat)
        delta[n], new_m[n], new_v[n] = (a.reshape(shape) for a in res)
    rest = [n for n in WEIGHTS if n not in large]
    rest_shapes = [given[n].shape for n in rest]
    packed = [_pack([src[pre + n] for n in rest], SUBLANES, cols=LANES)
              for src, pre in ((given, ""), (grads, ""), (given, "m_"), (given, "v_"))]
    for res, o in zip((delta, new_m, new_v), adamw("adamw_small", *packed)):
        for n, a in zip(rest, _unpack(o.reshape(-1), rest_shapes)):
            res[n] = a
    return (loss, grad_x[None], *[grads[n] for n in WEIGHTS], *[delta[n] for n in WEIGHTS],
            *[new_m[n] for n in WEIGHTS], *[new_v[n] for n in WEIGHTS])
```

```python
import functools
import math

import numpy as np
import jax
import jax.numpy as jnp
from jax import lax
from jax.experimental import pallas as pl
from jax.experimental.pallas import tpu as pltpu

F32 = jnp.float32
MXU_DTYPE = jnp.bfloat16
HIGHEST = lax.Precision.HIGHEST

DN_HEADS = 4
HEAD_DIM = 128
DN_WIDTH = DN_HEADS * HEAD_DIM
DN_CHUNK = 64
GRID_W = 64
EPS = 1e-6
POOL_WINDOWS = (2, 4, 8, 16)
N_DIRHEAD = 2 * DN_HEADS
ADAM_LR, ADAM_B1, ADAM_B2, ADAM_EPS, ADAM_WD, ADAM_STEP = 0.001, 0.9, 0.999, 1e-08, 0.01, 10

LANES = 128
SUBLANES = 8
ROW_TILE = 256
VMEM_LIMIT = 56 * 1024 * 1024

MESH_ID = pl.DeviceIdType.MESH
NN, NT, TN = ((1,), (0,)), ((1,), (1,)), ((0,), (0,))


def _params(sem=None):
    return pltpu.CompilerParams(dimension_semantics=sem, vmem_limit_bytes=VMEM_LIMIT)


def _pick(n, cands):
    for c in cands:
        if c <= n and n % c == 0:
            return c
    return n


def _mm(name, a, b, dims, out_dtype=F32):
    if dims == NN:
        (m, kk), (_, n) = a.shape, b.shape
    elif dims == NT:
        (m, kk), (n, _) = a.shape, b.shape
    else:
        (kk, m), (_, n) = a.shape, b.shape
    tm = _pick(m, (768, 1024, 1408, 512, 256, 128, 64, 32, 16, 8))
    tn = _pick(n, (1536, 1024, 1408, 1664, 768, 896, 512, 256, 128))
    tk = kk if dims == NN and kk <= 2816 else _pick(kk, (1024, 1408, 768, 512, 256, 128))
    gi, gj, gl = m // tm, n // tn, kk // tk
    if dims == NN:
        a_spec = pl.BlockSpec((tm, tk), lambda i, j, l: (i, l))
        b_spec = pl.BlockSpec((tk, tn), lambda i, j, l: (l, j))
    elif dims == NT:
        a_spec = pl.BlockSpec((tm, tk), lambda i, j, l: (i, l))
        b_spec = pl.BlockSpec((tn, tk), lambda i, j, l: (j, l))
    else:
        a_spec = pl.BlockSpec((tk, tm), lambda i, j, l: (l, i))
        b_spec = pl.BlockSpec((tk, tn), lambda i, j, l: (l, j))
    direct = gl == 1
    use_acc = (not direct) and out_dtype != F32

    def body(a_ref, b_ref, o_ref, *scratch):
        part = lax.dot_general(a_ref[...].astype(MXU_DTYPE), b_ref[...].astype(MXU_DTYPE), (dims, ((), ())),
                               preferred_element_type=F32)
        if direct:
            o_ref[...] = part.astype(out_dtype)
            return
        acc = scratch[0] if use_acc else o_ref
        l = pl.program_id(2)

        @pl.when(l == 0)
        def _():
            acc[...] = part

        @pl.when(l > 0)
        def _():
            acc[...] += part

        if use_acc:
            @pl.when(l == gl - 1)
            def _():
                o_ref[...] = acc[...].astype(out_dtype)

    return pl.pallas_call(
        body, grid=(gi, gj, gl), in_specs=[a_spec, b_spec],
        out_specs=pl.BlockSpec((tm, tn), lambda i, j, l: (i, j)),
        out_shape=jax.ShapeDtypeStruct((m, n), out_dtype),
        scratch_shapes=[pltpu.VMEM((tm, tn), F32)] if use_acc else [],
        compiler_params=_params(("parallel", "parallel", "arbitrary")), name=name,
    )(a, b)


def mm(name, a, w, carrier):
    @jax.custom_vjp
    def f(a, w, carrier):
        return _mm(name + "_f", a, w, NN, out_dtype=MXU_DTYPE)

    def fwd(a, w, carrier):
        return f(a, w, carrier), (a, w)

    def bwd(res, dc):
        a, w = res
        da = _mm(name + "_da", dc, w, NT, out_dtype=a.dtype)
        dw = _mm(name + "_dw", a, dc, TN)
        return da, None, dw

    f.defvjp(fwd, bwd)
    return f(a, w, carrier)


def _mm_nt_sum(name, dcs, ws, out_dtype):
    m, kk = dcs[0].shape[0], ws[0].shape[0]
    tm = _pick(m, (768, 1024, 512, 256, 128, 64, 32, 16, 8))
    tns = [_pick(w.shape[1], (1664, 1536, 1024, 768, 896, 512, 256, 128)) for w in ws]
    counts = [w.shape[1] // tn for w, tn in zip(ws, tns)]
    starts = [sum(counts[:g]) for g in range(len(ws))]
    steps = sum(counts)

    def col(g):
        return lambda i, t: jnp.clip(t - starts[g], 0, counts[g] - 1)

    def body(*refs):
        dc_refs, w_refs, o_ref, acc = refs[:len(ws)], refs[len(ws):2 * len(ws)], refs[-2], refs[-1]
        t = pl.program_id(1)

        @pl.when(t == 0)
        def _():
            acc[...] = jnp.zeros_like(acc)

        for g in range(len(ws)):
            @pl.when(jnp.logical_and(t >= starts[g], t < starts[g] + counts[g]))
            def _():
                acc[...] += lax.dot_general(dc_refs[g][...].astype(MXU_DTYPE), w_refs[g][...].astype(MXU_DTYPE),
                                            (NT, ((), ())), preferred_element_type=F32)

        @pl.when(t == steps - 1)
        def _():
            o_ref[...] = acc[...].astype(out_dtype)

    dc_specs = [pl.BlockSpec((tm, tns[g]), (lambda c: lambda i, t: (i, c(i, t)))(col(g))) for g in range(len(ws))]
    w_specs = [pl.BlockSpec((kk, tns[g]), (lambda c: lambda i, t: (0, c(i, t)))(col(g))) for g in range(len(ws))]
    return pl.pallas_call(
        body, grid=(m // tm, steps), in_specs=dc_specs + w_specs,
        out_specs=pl.BlockSpec((tm, kk), lambda i, t: (i, 0)),
        out_shape=jax.ShapeDtypeStruct((m, kk), out_dtype),
        scratch_shapes=[pltpu.VMEM((tm, kk), F32)],
        compiler_params=_params(("parallel", "arbitrary")), name=name,
    )(*dcs, *ws)


def mm_fanout(name, a, ws, carriers):
    n = len(ws)

    @jax.custom_vjp
    def f(a, ws, carriers):
        return tuple(_mm(f"{name}{g}_f", a, ws[g], NN, out_dtype=MXU_DTYPE) for g in range(n))

    def fwd(a, ws, carriers):
        return f(a, ws, carriers), (a, ws)

    def bwd(res, dcs):
        a, ws = res
        da = _mm_nt_sum(name + "_da", list(dcs), list(ws), a.dtype)
        dws = tuple(_mm(f"{name}{g}_dw", a, dcs[g], TN) for g in range(n))
        return da, None, dws

    f.defvjp(fwd, bwd)
    return f(a, tuple(ws), tuple(carriers))


def _split(x, parts):
    w = x.shape[-1] // parts
    return [x[:, k * w:(k + 1) * w] for k in range(parts)]


def _cat(xs):
    return xs[0] if len(xs) == 1 else jnp.concatenate(xs, axis=-1)


def _shift_rows(x, prev_ref, next_ref, i, ctx_tiles, nt):
    tr = x.shape[0]
    rid = lax.broadcasted_iota(jnp.int32, x.shape, 0)
    first = jnp.logical_or(i == 0, i == ctx_tiles)
    last = jnp.logical_or(i == ctx_tiles - 1, i == nt - 1)
    prow = jnp.where(first, 0.0, prev_ref[SUBLANES - 1:SUBLANES, :].astype(F32))
    nrow = jnp.where(last, 0.0, next_ref[0:1, :].astype(F32))
    xm = jnp.where(rid == 0, prow, pltpu.roll(x, 1, 0))
    xp = jnp.where(rid == tr - 1, nrow, pltpu.roll(x, tr - 1, 0))
    return xm, xp


def rowwise(name, fn, rows, vecs, *, halo=(), seg=(), parts_r=None, parts_v=None, outs=(), reds=(), ctx_tiles=1, skip=None):
    nr, nv = len(rows), len(vecs)
    halo = tuple(halo) or (False,) * nr
    seg = tuple(seg) or (False,) * nv
    skip = tuple(skip or (0,) * nr)
    parts_r = tuple(parts_r or (1,) * nr)
    parts_v = tuple(parts_v or (1,) * nv)
    r_total = rows[0].shape[0]
    tr = ROW_TILE
    nt = r_total // tr
    assert r_total % tr == 0 and (ctx_tiles > 0 or not any(seg)) and not any(h and s for h, s in zip(halo, skip))

    def tile_map(i):
        return lambda t: (jnp.maximum(t - skip[i], 0), 0)

    def row_specs():
        sp = []
        for i, r in enumerate(rows):
            w = r.shape[1]
            sp.append(pl.BlockSpec((tr, w), tile_map(i)))
            if halo[i]:
                k = tr // SUBLANES
                sp.append(pl.BlockSpec((SUBLANES, w), lambda t: (jnp.maximum(t * k - 1, 0), 0)))
                sp.append(pl.BlockSpec((SUBLANES, w), lambda t: (jnp.minimum((t + 1) * k, nt * k - 1), 0)))
        return sp

    def vec_spec(j):
        w = vecs[j].shape[-1]
        if seg[j]:
            return pl.BlockSpec((None, 1, w), lambda t: (jnp.where(t >= ctx_tiles, 1, 0), 0, 0))
        return pl.BlockSpec((1, w), lambda t: (0, 0))

    def row_args(rv):
        a = []
        for i in range(nr):
            a += [rv[i]] * 3 if halo[i] else [rv[i]]
        return a

    def load(refs, t):
        pos, rp = 0, []
        for i in range(nr):
            x = refs[pos][...].astype(F32)
            if halo[i]:
                xm, xp = _shift_rows(x, refs[pos + 1], refs[pos + 2], t, ctx_tiles, nt)
                rp.append(list(zip(_split(x, parts_r[i]), _split(xm, parts_r[i]), _split(xp, parts_r[i]))))
                pos += 3
            else:
                rp.append(_split(x, parts_r[i]))
                pos += 1
        vp = []
        for j in range(nv):
            vp.append(_split(refs[pos][...].astype(F32), parts_v[j]))
            pos += 1
        return rp, vp, refs[pos:]

    n_out, n_red = len(outs), len(reds)

    def fwd_call(*rv):
        def body(*refs):
            t = pl.program_id(0)
            rp, vp, rest = load(refs, t)
            o_parts, r_parts = fn(rp, vp)
            for k in range(n_out):
                rest[k][...] = _cat(o_parts[k]).astype(outs[k][1])
            for k in range(n_red):
                ref = rest[n_out + k]

                @pl.when(t == 0)
                def _():
                    ref[...] = r_parts[k]

                @pl.when(t > 0)
                def _():
                    ref[...] += r_parts[k]

        res = pl.pallas_call(
            body, grid=(nt,),
            in_specs=row_specs() + [vec_spec(j) for j in range(nv)],
            out_specs=[pl.BlockSpec((tr, o[0]), lambda t: (t, 0)) for o in outs]
            + [pl.BlockSpec((1, w), lambda t: (0, 0)) for w in reds],
            out_shape=[jax.ShapeDtypeStruct((r_total, o[0]), o[1]) for o in outs]
            + [jax.ShapeDtypeStruct((1, w), F32) for w in reds],
            compiler_params=_params(("arbitrary",)), name=name + "_f",
        )(*row_args(rv), *rv[nr:])
        return tuple(res)

    def bwd_call(rv, cts):
        def body(*refs):
            t = pl.program_id(0)
            rp, vp, rest = load(refs, t)
            ct_o = [_split(rest[k][...].astype(F32), outs[k][2]) for k in range(n_out)]
            ct_r = [rest[n_out + k][...] for k in range(n_red)]
            rest = rest[n_out + n_red:]
            _, vjp = jax.vjp(fn, rp, vp)
            d_rp, d_vp = vjp((ct_o, ct_r))
            pos = 0
            for i in range(nr):
                if halo[i]:
                    for c in range(3):
                        rest[pos + c][...] = _cat([p[c] for p in d_rp[i]])
                    pos += 3
                else:
                    rest[pos][...] = _cat(d_rp[i]).astype(rows[i].dtype)
                    pos += 1
            for j in range(nv):
                ref, val = rest[pos + j], _cat(d_vp[j])
                start = jnp.logical_or(t == 0, t == ctx_tiles) if seg[j] else t == 0

                @pl.when(start)
                def _():
                    ref[...] = val

                @pl.when(jnp.logical_not(start))
                def _():
                    ref[...] += val

        d_row_specs, d_row_shapes = [], []
        for i, r in enumerate(rows):
            w = r.shape[1]
            for _ in range(3 if halo[i] else 1):
                d_row_specs.append(pl.BlockSpec((tr, w), tile_map(i)))
                d_row_shapes.append(jax.ShapeDtypeStruct(r.shape, F32 if halo[i] else r.dtype))
        res = pl.pallas_call(
            body, grid=(nt,),
            in_specs=row_specs() + [vec_spec(j) for j in range(nv)]
            + [pl.BlockSpec((tr, o[0]), lambda t: (t, 0)) for o in outs]
            + [pl.BlockSpec((1, w), lambda t: (0, 0)) for w in reds],
            out_specs=d_row_specs + [vec_spec(j) for j in range(nv)],
            out_shape=d_row_shapes + [jax.ShapeDtypeStruct(v.shape, F32) for v in vecs],
            compiler_params=_params(("arbitrary",)), name=name + "_b",
        )(*row_args(rv), *rv[nr:], *cts)
        d_rows, pos = [], 0
        for i in range(nr):
            if halo[i]:
                d_rows.append(_unshift(res[pos], res[pos + 1], res[pos + 2], ctx_tiles * tr).astype(rows[i].dtype))
                pos += 3
            else:
                d_rows.append(res[pos])
                pos += 1
        return tuple(d_rows) + tuple(res[pos:])

    @jax.custom_vjp
    def f(*rv):
        return fwd_call(*rv)

    f.defvjp(lambda *rv: (fwd_call(*rv), rv), lambda rv, cts: bwd_call(rv, cts))
    return f(*rows, *vecs)


def _unshift(d, dm, dp, ctx_rows):
    r = d.shape[0]
    t = lax.broadcasted_iota(jnp.int32, (r, 1), 0)
    zero = jnp.zeros((1, d.shape[1]), d.dtype)
    from_m = jnp.concatenate([dm[1:], zero], axis=0)
    from_p = jnp.concatenate([zero, dp[:-1]], axis=0)
    from_m = jnp.where(t == ctx_rows - 1, 0.0, from_m)
    from_p = jnp.where(t == ctx_rows, 0.0, from_p)
    return d + from_m + from_p


def _sigmoid(x):
    return 0.5 * (jnp.tanh(0.5 * x) + 1.0)


def _silu(x):
    return x * _sigmoid(x)


def _softplus(x):
    return jnp.maximum(x, 0.0) + jnp.log(1.0 + jnp.exp(-jnp.abs(x)))


def _rms(x, g):
    return x * lax.rsqrt(jnp.mean(x * x, axis=-1, keepdims=True) + EPS) * g


def _fn_modulate(r, v):
    (x,), (g,), (sh,), (sc,) = r[0], v[0], v[1], v[2]
    return [[_rms(x, g) * (1.0 + sc) + sh]], []


def _fn_resmod(r, v):
    (x,), (y,) = r
    (gate,), (g,), (sh,), (sc,) = v
    xn = x + gate * y
    return [[xn], [_rms(xn, g) * (1.0 + sc) + sh]], []


def _fn_final(r, v):
    (x,), (y,), (tgt,) = r
    (gate,), (g,), (counts,) = v
    err = _rms(x + gate * y, g) - tgt
    row_loss = jnp.mean(err * err, axis=-1, keepdims=True)
    total = 0.5 * jnp.sum(row_loss, axis=0, keepdims=True)
    return [], [total * counts]


def _fn_dnprep(r, v):
    qkv, (ab,) = r
    w0, w1, w2, (alog,), (dtb,) = v
    out = [[], [], []]
    for n, (x, xm, xp) in enumerate(qkv):
        which = n // DN_HEADS
        y = _silu(xm * w0[n] + x * w1[n] + xp * w2[n])
        if which < 2:
            y = y * lax.rsqrt(jnp.sum(y * y, axis=-1, keepdims=True) + EPS)
        if which == 0:
            y = y * (HEAD_DIM ** -0.5)
        out[which].append(y)
    lane = lax.broadcasted_iota(jnp.int32, ab.shape, 1)
    g = -jnp.exp(alog) * _softplus(ab + dtb)
    gb = jnp.where(lane < N_DIRHEAD, g, jnp.where(lane < 2 * N_DIRHEAD, _sigmoid(ab), 0.0))
    return out + [[gb]], []


def _fn_dnpost(r, v):
    of, ob, z = r
    (g,) = v[0]
    return [[_rms(a + b, g) * _silu(c) for a, b, c in zip(of, ob, z)]], []


def _fn_sub(r, v):
    return [[r[0][0] - r[1][0]]], []


def _fn_scale(r, v):
    return [[r[0][0] * v[0][0]]], []


def _fn_shortconv(r, v):
    (xin, gb, gc), = r
    (w0,), (w1,), (w2,) = v
    u, um, up = (gc[k] * xin[k] for k in range(3))
    return [[gb[0] * (um * w0 + u * w1 + up * w2)]], []


def _fn_merge(r, v):
    gates, (ya,), (yb,), (yc,) = r
    return [[_sigmoid(gates[0]) * ya + _sigmoid(gates[1]) * yb + _sigmoid(gates[2]) * yc]], []


def _fn_swiglu(r, v):
    gate, up = r[0]
    return [[_silu(gate) * up]], []


def _dot3(a, b):
    (ah, al), (bh, bl) = _hi_lo(a), _hi_lo(b)
    dot = lambda x, y: lax.dot_general(x, y, (NN, ((), ())), preferred_element_type=F32)
    return dot(ah, bh) + (dot(ah, bl) + dot(al, bh))


def _bdot(a, b, dims):
    (ca,), (cb,) = dims
    return lax.dot_general(a.astype(MXU_DTYPE), b.astype(MXU_DTYPE), (((ca + 1,), (cb + 1,)), ((0,), (0,))),
                           preferred_element_type=F32)


def _hi_lo(a):
    hi = a.astype(MXU_DTYPE)
    return hi, (a - hi.astype(F32)).astype(MXU_DTYPE)


def _bdot3_raw(a, b, dims):
    (ah, al), (bh, bl) = _hi_lo(a), _hi_lo(b)
    return _bdot(ah, bh, dims) + (_bdot(ah, bl, dims) + _bdot(al, bh, dims))


@jax.custom_vjp
def _bdot3(a, b):
    return _bdot3_raw(a, b, NN)


_bdot3.defvjp(lambda a, b: (_bdot3_raw(a, b, NN), (a, b)),
              lambda res, ct: (_bdot3_raw(ct, res[1], NT), _bdot3_raw(res[0], ct, TN)))


def _inv_doubling(a):
    c = a.shape[-1]
    ii, jj = (lax.broadcasted_iota(jnp.int32, a.shape, d) for d in (1, 2))
    t = jnp.where(ii == jj, 1.0, 0.0) - a
    p = a
    for _ in range(int(math.log2(c)) - 1):
        p = _bdot3(p, p)
        t = t + _bdot3(t, p)
    return t


def _dn_gates(k4, gb):
    nb, c = N_DIRHEAD, k4.shape[1]
    k = jnp.concatenate([k4, k4], axis=0)
    lane = lax.broadcasted_iota(jnp.int32, gb.shape, 1)
    col = lambda j: jnp.sum(jnp.where(lane == j, gb, 0.0), axis=1, keepdims=True)
    g_col = jnp.concatenate([col(j)[None] for j in range(nb)], axis=0)
    b_col = jnp.concatenate([col(nb + j)[None] for j in range(nb)], axis=0)
    bi, ii, jj = (lax.broadcasted_iota(jnp.int32, (nb, c, c), a) for a in range(3))
    ahead = jnp.where(bi >= DN_HEADS, jj - ii, ii - jj)
    incl = ahead >= 0
    g_row = jnp.sum(jnp.where(ahead == 0, g_col, 0.0), axis=1, keepdims=True)
    gc_col = jnp.sum(jnp.where(incl, g_row, 0.0), axis=2, keepdims=True)
    gc_row = jnp.sum(jnp.where(ahead <= 0, g_col, 0.0), axis=1, keepdims=True)
    decay = jnp.where(incl, jnp.exp(jnp.where(incl, gc_col - gc_row, 0.0)), 0.0)
    return k, g_col, b_col, gc_col, decay, ahead > 0


def _dn_a(k4, gb):
    k, _, b_col, _, decay, strict = _dn_gates(k4, gb)
    return _bdot(k * b_col, k, NT) * jnp.where(strict, decay, 0.0)


def _dn_operands(q4, k4, v4, gb, t):
    k, g_col, b_col, gc_col, decay, _ = _dn_gates(k4, gb)
    q, v = (jnp.concatenate([a, a], axis=0) for a in (q4, v4))
    e_gc = jnp.exp(gc_col)
    u = _bdot3(t, v * b_col)
    w = _bdot3(t, k * b_col * e_gc)
    g_last = jnp.sum(g_col, axis=1, keepdims=True)
    k_state = k * jnp.exp(g_last - gc_col)
    a_qk = _bdot(q, k, NT) * decay
    return u, w, q * e_gc, k_state, a_qk, jnp.broadcast_to(jnp.exp(g_last), (N_DIRHEAD, 1, LANES))


def _dn_step(s, u, w, qd, ks, aqk, gl):
    v_new = u - _bdot(w, s, NN)
    o = _bdot(qd, s, NN) + _bdot(aqk, v_new, NN)
    return s * gl[:, :, :1] + _bdot(ks, v_new, TN), o


def _heads(x):
    return jnp.concatenate([x[None, :, h * HEAD_DIM:(h + 1) * HEAD_DIM] for h in range(DN_HEADS)], axis=0)


def _unheads(x):
    return jnp.concatenate([x[h] for h in range(x.shape[0])], axis=-1)


def _dn_rev(t, nc, n):
    return jnp.where(t < nc, nc - 1 - t, n - 1 - (t - nc))


PRE_CHUNKS = 2


def _pre_shapes(n):
    c, d, h = DN_CHUNK, HEAD_DIM, DN_HEADS
    shapes = [(n, h, c, d)] * 8 + [(n, h, c, c)] * 2 + [(n, h, 1, LANES)] * 2
    return shapes, [pl.BlockSpec((PRE_CHUNKS,) + s[1:], lambda t: (t, 0, 0, 0)) for s in shapes]


def _inverse_spec(n):
    shape = (n, N_DIRHEAD, DN_CHUNK, DN_CHUNK)
    return shape, pl.BlockSpec((PRE_CHUNKS,) + shape[1:], lambda t: (t, 0, 0, 0))


def _pre_row_specs():
    rows = PRE_CHUNKS * DN_CHUNK
    return pl.BlockSpec((rows, DN_HEADS * HEAD_DIM), lambda t: (t, 0)), pl.BlockSpec((rows, LANES), lambda t: (t, 0))


def _dn_pre_fwd(name, q, k, v, gb):
    n = q.shape[0] // DN_CHUNK
    shapes, specs = _pre_shapes(n)
    t_shape, t_spec = _inverse_spec(n)

    def body(q_ref, k_ref, v_ref, g_ref, *o_refs):
        for s in range(PRE_CHUNKS):
            rows = pl.ds(s * DN_CHUNK, DN_CHUNK)
            q4, k4, v4, gb_ = _heads(q_ref[rows, :]), _heads(k_ref[rows, :]), _heads(v_ref[rows, :]), g_ref[rows, :]
            t = _inv_doubling(_dn_a(k4, gb_))
            for i, r in enumerate(_dn_operands(q4, k4, v4, gb_, t)):
                o_refs[2 * i][s] = r[:DN_HEADS]
                o_refs[2 * i + 1][s] = r[DN_HEADS:]
            o_refs[-1][s] = t

    wide, narrow = _pre_row_specs()
    return pl.pallas_call(body, grid=(n // PRE_CHUNKS,), in_specs=[wide] * 3 + [narrow], out_specs=specs + [t_spec],
                          out_shape=[jax.ShapeDtypeStruct(s, F32) for s in shapes + [t_shape]],
                          compiler_params=_params(("parallel",)), name=name + "_pre_f")(q, k, v, gb)


def _dn_pre_bwd(name, q, k, v, gb, inv, cts):
    n = q.shape[0] // DN_CHUNK
    _, specs = _pre_shapes(n)
    _, t_spec = _inverse_spec(n)
    n_ct = len(specs)

    def body(q_ref, k_ref, v_ref, g_ref, t_ref, *refs):
        for s in range(PRE_CHUNKS):
            rows = pl.ds(s * DN_CHUNK, DN_CHUNK)
            ct = tuple(jnp.concatenate([refs[i][s], refs[i + 1][s]], axis=0) for i in range(0, n_ct, 2))
            q4, k4, v4, gb_, t = _heads(q_ref[rows, :]), _heads(k_ref[rows, :]), _heads(v_ref[rows, :]), g_ref[rows, :], t_ref[s]
            _, vjp = jax.vjp(_dn_operands, q4, k4, v4, gb_, t)
            dq, dk, dv, dg, dt = vjp(ct)
            da = -_bdot3_raw(_bdot3_raw(t, dt, TN), t, NT)
            _, vjp_a = jax.vjp(_dn_a, k4, gb_)
            dk_a, dg_a = vjp_a(da)
            for r, val in zip(refs[n_ct:], (_unheads(dq), _unheads(dk + dk_a), _unheads(dv), dg + dg_a)):
                r[rows, :] = val

    wide, narrow = _pre_row_specs()
    return pl.pallas_call(body, grid=(n // PRE_CHUNKS,), in_specs=[wide] * 3 + [narrow, t_spec] + specs,
                          out_specs=[wide] * 3 + [narrow],
                          out_shape=[jax.ShapeDtypeStruct(q.shape, F32)] * 3 + [jax.ShapeDtypeStruct(gb.shape, F32)],
                          compiler_params=_params(("parallel",)), name=name + "_pre_b")(q, k, v, gb, inv, *cts)


def _scan_specs(pre_shapes, fw, bw):
    maps = (lambda t: (fw(t), 0, 0, 0), lambda t: (bw(t), 0, 0, 0))
    return [pl.BlockSpec((None,) + s[1:], maps[i % 2]) for i, s in enumerate(pre_shapes)]


def _dn_scan_fwd(name, pre, nc):
    n = pre[0].shape[0]
    r = n * DN_CHUNK
    hd, nh, nb = HEAD_DIM, DN_HEADS, N_DIRHEAD
    shapes, _ = _pre_shapes(n)
    n_in = len(shapes)

    def body(*refs):
        ins, (of_ref, ob_ref, sf_ref, sb_ref, s_scr) = refs[:n_in], refs[n_in:]
        t = pl.program_id(0)

        @pl.when(t == 0)
        def _():
            s_scr[...] = jnp.zeros_like(s_scr)

        s = s_scr[...]
        sf_ref[...] = s[:nh]
        sb_ref[...] = s[nh:]
        args = [jnp.concatenate([ins[i][...], ins[i + 1][...]], axis=0) for i in range(0, n_in, 2)]
        s2, o = _dn_step(s, *args)
        s_scr[...] = s2
        of_ref[...] = _unheads(o[:nh])
        ob_ref[...] = _unheads(o[nh:])

    fw = lambda t: t
    bw = lambda t: _dn_rev(t, nc, n)
    wide, st = (DN_CHUNK, nh * hd), (None, nh, hd, hd)
    return pl.pallas_call(
        body, grid=(n,), in_specs=_scan_specs(shapes, fw, bw),
        out_specs=[pl.BlockSpec(wide, lambda t: (fw(t), 0)), pl.BlockSpec(wide, lambda t: (bw(t), 0)),
                   pl.BlockSpec(st, lambda t: (fw(t), 0, 0, 0)), pl.BlockSpec(st, lambda t: (bw(t), 0, 0, 0))],
        out_shape=[jax.ShapeDtypeStruct((r, nh * hd), F32)] * 2 + [jax.ShapeDtypeStruct((n, nh, hd, hd), F32)] * 2,
        scratch_shapes=[pltpu.VMEM((nb, hd, hd), F32)],
        compiler_params=_params(("arbitrary",)), name=name + "_scan_f",
    )(*pre)


def _dn_scan_bwd(name, pre, sall_f, sall_b, do_f, do_b, nc):
    n = pre[0].shape[0]
    hd, nh, nb = HEAD_DIM, DN_HEADS, N_DIRHEAD
    shapes, _ = _pre_shapes(n)
    n_in = len(shapes)

    def body(*refs):
        ins, (sf_ref, sb_ref, dof_ref, dob_ref) = refs[:n_in], refs[n_in:n_in + 4]
        outs, ds_scr = refs[n_in + 4:2 * n_in + 4], refs[2 * n_in + 4]
        t = pl.program_id(0)

        @pl.when(t == 0)
        def _():
            ds_scr[...] = jnp.zeros_like(ds_scr)

        args = [jnp.concatenate([ins[i][...], ins[i + 1][...]], axis=0) for i in range(0, n_in, 2)]
        s = jnp.concatenate([sf_ref[...], sb_ref[...]], axis=0)
        do = jnp.concatenate([_heads(dof_ref[...]), _heads(dob_ref[...])], axis=0)
        _, vjp = jax.vjp(_dn_step, s, *args)
        cts = vjp((ds_scr[...], do))
        ds_scr[...] = cts[0]
        for i, ct in enumerate(cts[1:]):
            outs[2 * i][...] = ct[:nh]
            outs[2 * i + 1][...] = ct[nh:]

    fw = lambda t: n - 1 - t
    bw = lambda t: _dn_rev(n - 1 - t, nc, n)
    wide, st = (DN_CHUNK, nh * hd), (None, nh, hd, hd)
    return tuple(pl.pallas_call(
        body, grid=(n,),
        in_specs=_scan_specs(shapes, fw, bw)
        + [pl.BlockSpec(st, lambda t: (fw(t), 0, 0, 0)), pl.BlockSpec(st, lambda t: (bw(t), 0, 0, 0)),
           pl.BlockSpec(wide, lambda t: (fw(t), 0)), pl.BlockSpec(wide, lambda t: (bw(t), 0))],
        out_specs=_scan_specs(shapes, fw, bw), out_shape=[jax.ShapeDtypeStruct(s, F32) for s in shapes],
        scratch_shapes=[pltpu.VMEM((nb, hd, hd), F32)],
        compiler_params=_params(("arbitrary",)), name=name + "_scan_b",
    )(*pre, sall_f, sall_b, do_f, do_b))


def deltanet(name, q, k, v, gb, ctx_rows):
    nc = ctx_rows // DN_CHUNK

    @jax.custom_vjp
    def pre(q, k, v, gb):
        return tuple(_dn_pre_fwd(name, q, k, v, gb)[:-1])

    def pre_fwd(*a):
        *ops, inv = _dn_pre_fwd(name, *a)
        return tuple(ops), (a, inv)

    pre.defvjp(pre_fwd, lambda res, cts: tuple(_dn_pre_bwd(name, *res[0], res[1], cts)))

    @jax.custom_vjp
    def scan(*ops):
        return tuple(_dn_scan_fwd(name, ops, nc)[:2])

    def scan_fwd(*ops):
        of, ob, sf, sb = _dn_scan_fwd(name, ops, nc)
        return (of, ob), (ops, sf, sb)

    scan.defvjp(scan_fwd, lambda res, cts: _dn_scan_bwd(name, res[0], res[1], res[2], cts[0], cts[1], nc))
    return scan(*pre(q, k, v, gb))


def _box_matrix(l, w):
    lo, hi = w // 2, w - 1 - w // 2
    pos = np.arange(l)
    start, end = np.clip(pos - lo, 0, l), np.clip(pos + hi + 1, 0, l)
    col = np.arange(l)[None, :]
    return ((col >= start[:, None]) & (col < end[:, None])) / (end - start)[:, None].astype(np.float64)


def _pool_matrices(ctx_rows, grid_rows):
    assert ctx_rows == ROW_TILE and ROW_TILE % GRID_W == 0
    ctx = np.stack([_box_matrix(ctx_rows, w) for w in POOL_WINDOWS])
    cols = np.stack([np.kron(np.eye(ROW_TILE // GRID_W), _box_matrix(GRID_W, w)) for w in POOL_WINDOWS])
    rows = np.stack([_box_matrix(grid_rows, w) for w in POOL_WINDOWS])[None]
    return np.stack([ctx, cols]).astype(np.float32), rows.astype(np.float32)


def _pool_apply(name, x, mats, group_w, seg_tiles, lane_tile):
    r, w = x.shape
    b = mats.shape[-1]
    ng = mats.shape[1]
    period = ng * group_w

    def body(x_ref, m_ref, o_ref):
        xv = x_ref[...].astype(F32)
        lane = lax.broadcasted_iota(jnp.int32, xv.shape, 1)
        grp = (lane % period) // group_w
        acc = jnp.zeros_like(xv)
        for g in range(ng):
            acc = acc + jnp.where(grp == g, _dot3(m_ref[g], xv), 0.0)
        o_ref[...] = acc

    return pl.pallas_call(
        body, grid=(r // b, w // lane_tile),
        in_specs=[pl.BlockSpec((b, lane_tile), lambda i, j: (i, j)),
                  pl.BlockSpec((None, ng, b, b), lambda i, j: (jnp.where(i >= seg_tiles, 1, 0) if mats.shape[0] > 1 else 0, 0, 0, 0))],
        out_specs=pl.BlockSpec((b, lane_tile), lambda i, j: (i, j)),
        out_shape=jax.ShapeDtypeStruct(x.shape, F32),
        compiler_params=_params(("parallel", "parallel")), name=name,
    )(x, mats)


def pool_means(name, u, ctx_rows):
    r, pw = u.shape
    grid_rows = (r - ctx_rows) // GRID_W
    gw = pw // len(POOL_WINDOWS)
    m1, m2 = _pool_matrices(ctx_rows, grid_rows)
    lane_tile = min(2048, GRID_W * pw)

    def apply(x, a1, a2, tag):
        y = _pool_apply(name + tag + "1", x, jnp.asarray(a1), gw, ctx_rows // ROW_TILE, pw)
        lat = y[ctx_rows:].reshape(grid_rows, GRID_W * pw)
        lat = _pool_apply(name + tag + "2", lat, jnp.asarray(a2), gw, 0, lane_tile)
        return jnp.concatenate([y[:ctx_rows], lat.reshape(r - ctx_rows, pw)], axis=0)

    @jax.custom_vjp
    def f(u):
        return apply(u, m1, m2, "_f")

    tr = lambda m: np.ascontiguousarray(np.swapaxes(m, -1, -2))
    f.defvjp(lambda u: (apply(u, m1, m2, "_f"), None), lambda _, ct: (apply(ct, tr(m1), tr(m2), "_b").astype(u.dtype),))
    return f(u)


def _ew(name, fn, *xs, n_out=1):
    shapes = jax.eval_shape(lambda *a: fn(*a), *xs)
    shapes = shapes if isinstance(shapes, (tuple, list)) else (shapes,)

    def body(*refs):
        res = fn(*[r[...] for r in refs[:len(xs)]])
        res = res if isinstance(res, (tuple, list)) else (res,)
        for r, o in zip(res, refs[len(xs):]):
            o[...] = r

    out = pl.pallas_call(body, out_shape=[jax.ShapeDtypeStruct(s.shape, s.dtype) for s in shapes],
                         compiler_params=_params(), name=name)(*xs)
    return out[0] if len(shapes) == 1 else tuple(out)


def _adamw_math(w, g, m, v):
    m2 = ADAM_B1 * m + (1.0 - ADAM_B1) * g
    v2 = ADAM_B2 * v + (1.0 - ADAM_B2) * (g * g)
    m_hat = m2 / (1.0 - ADAM_B1 ** ADAM_STEP)
    v_hat = v2 / (1.0 - ADAM_B2 ** ADAM_STEP)
    delta = -ADAM_LR * (m_hat / (jnp.sqrt(v_hat) + ADAM_EPS) + ADAM_WD * w)
    return delta, m2, v2


def adamw(name, w, g, m, v):
    r, c = w.shape
    tr = _pick(r, (256, 128, 64, 32, 16, 8))

    def body(w_ref, g_ref, m_ref, v_ref, d_ref, m2_ref, v2_ref):
        d_ref[...], m2_ref[...], v2_ref[...] = _adamw_math(w_ref[...], g_ref[...], m_ref[...], v_ref[...])

    spec = pl.BlockSpec((tr, c), lambda i: (i, 0))
    return pl.pallas_call(body, grid=(r // tr,), in_specs=[spec] * 4, out_specs=[spec] * 3,
                          out_shape=[jax.ShapeDtypeStruct(w.shape, F32)] * 3,
                          compiler_params=_params(("parallel",)), name=name)(w, g, m, v)


def _sum_rows(name, xs, out_dtype=F32):
    r, c = xs[0].shape
    tr = _pick(r, tuple(t for t in (512, 256, 128, 64, 32, 16, 8) if t * c * 4 <= 2 << 20))

    def body(*refs):
        acc = refs[0][...].astype(F32)
        for ref in refs[1:-1]:
            acc = acc + ref[...].astype(F32)
        refs[-1][...] = acc.astype(out_dtype)

    spec = pl.BlockSpec((tr, c), lambda i: (i, 0))
    return pl.pallas_call(body, grid=(r // tr,), in_specs=[spec] * len(xs), out_specs=spec,
                          out_shape=jax.ShapeDtypeStruct((r, c), out_dtype),
                          compiler_params=_params(("parallel",)), name=name)(*xs)


def _place():
    return lax.axis_index("x"), lax.axis_index("y"), lax.axis_index("c")


def _chip_peers(x, y, c):
    return [(1 - x, y, c), (x, 1 - y, c), (1 - x, 1 - y, c)]


def all_gather8(name, block):
    m_per, n = block.shape

    def body(x_ref, out_ref, send_sems, recv_sems, local_sem):
        x, y, c = _place()
        me, sibling = (x, y, c), (x, y, 1 - c)
        chips = [(1 - x, y), (x, 1 - y), (1 - x, 1 - y)]

        def rows(px, py, pc):
            return out_ref.at[pl.ds((4 * px + 2 * py + pc) * m_per, m_per), :]

        def copy(k, blk, to, src=None):
            return pltpu.make_async_remote_copy(
                src_ref=rows(*blk) if src is None else src, dst_ref=rows(*blk),
                send_sem=send_sems.at[k], recv_sem=recv_sems.at[k], device_id=to, device_id_type=MESH_ID)

        mine = pltpu.make_async_copy(x_ref, rows(*me), local_sem)
        mine.start()
        first = [copy(0, me, sibling, src=x_ref)]
        first += [copy(1 + j, me, (*chip, c), src=x_ref) for j, chip in enumerate(chips)]
        for cp in first:
            cp.start()
        passed = [copy(4 + j, (*chip, c), sibling) for j, chip in enumerate(chips)]
        for j, chip in enumerate(chips):
            copy(1 + j, (*chip, c), me).wait_recv()
            passed[j].start()
        copy(0, sibling, me).wait_recv()
        for j, chip in enumerate(chips):
            copy(4 + j, (*chip, 1 - c), me).wait_recv()
        for cp in first + passed:
            cp.wait_send()
        mine.wait()

    return pl.pallas_call(
        body, out_shape=jax.ShapeDtypeStruct((8 * m_per, n), block.dtype),
        in_specs=[pl.BlockSpec(memory_space=pltpu.VMEM)], out_specs=pl.BlockSpec(memory_space=pltpu.VMEM),
        scratch_shapes=[pltpu.SemaphoreType.DMA((7,)), pltpu.SemaphoreType.DMA((7,)), pltpu.SemaphoreType.DMA],
        compiler_params=_params(), name=name,
    )(block)


N_PEERS = 3


def gather_chips(name, shards):
    n = len(shards)

    def body(*refs):
        xs, outs, (send_sems, recv_sems, local_sems) = refs[:n], refs[n:2 * n], refs[2 * n:]
        x, y, c = _place()
        local, remote = [], []
        for i in range(n):
            local.append(pltpu.make_async_copy(xs[i].at[c], outs[i].at[2 * x + y], local_sems.at[i]))
            local[-1].start()
            for p, peer in enumerate(_chip_peers(x, y, c)):
                remote.append(pltpu.make_async_remote_copy(
                    src_ref=xs[i].at[c], dst_ref=outs[i].at[2 * x + y], send_sem=send_sems.at[i * N_PEERS + p],
                    recv_sem=recv_sems.at[i * N_PEERS + p], device_id=peer, device_id_type=MESH_ID))
                remote[-1].start()
        for i in range(n):
            for p, (px, py, _) in enumerate(_chip_peers(x, y, c)):
                pltpu.make_async_remote_copy(
                    src_ref=xs[i].at[c], dst_ref=outs[i].at[2 * px + py], send_sem=send_sems.at[i * N_PEERS + p],
                    recv_sem=recv_sems.at[i * N_PEERS + p], device_id=(px, py, c), device_id_type=MESH_ID).wait_recv()
        for cp in remote:
            cp.wait_send()
        for cp in local:
            cp.wait()

    hbm = pl.BlockSpec(memory_space=pltpu.HBM)
    return pl.pallas_call(
        body, out_shape=[jax.ShapeDtypeStruct((N_CHIPS,) + s.shape[1:], s.dtype) for s in shards],
        in_specs=[hbm] * n, out_specs=[hbm] * n,
        scratch_shapes=[pltpu.SemaphoreType.DMA((n * N_PEERS,)), pltpu.SemaphoreType.DMA((n * N_PEERS,)),
                        pltpu.SemaphoreType.DMA((n,))],
        compiler_params=_params(), name=name,
    )(*shards)


def swap_sibling(name, blocks):
    n = len(blocks)
    pairs = [tuple(b) if isinstance(b, (tuple, list)) else (b,) for b in blocks]
    flat = [a for p in pairs for a in p]
    first = [sum(len(q) for q in pairs[:i]) for i in range(n)]

    def body(*refs):
        xs, outs, (send_sems, recv_sems) = refs[:len(flat)], refs[len(flat):len(flat) + n], refs[len(flat) + n:]
        x, y, c = _place()

        def copy(i, src):
            return pltpu.make_async_remote_copy(src_ref=src, dst_ref=outs[i], send_sem=send_sems.at[i], recv_sem=recv_sems.at[i],
                                                device_id=(x, y, 1 - c), device_id_type=MESH_ID)

        for i, p in enumerate(pairs):
            if len(p) == 1:
                copy(i, xs[first[i]]).start()
            else:
                for half in range(2):
                    pl.when(c == 1 - half)(copy(i, xs[first[i] + half]).start)
        for i in range(n):
            copy(i, xs[first[i]]).wait()

    hbm = pl.BlockSpec(memory_space=pltpu.HBM)
    return pl.pallas_call(
        body, out_shape=[jax.ShapeDtypeStruct(p[0].shape, p[0].dtype) for p in pairs],
        in_specs=[hbm] * len(flat), out_specs=[hbm] * n,
        scratch_shapes=[pltpu.SemaphoreType.DMA((n,)), pltpu.SemaphoreType.DMA((n,))],
        compiler_params=_params(), name=name,
    )(*flat)


def scatter_chips(name, pieces):
    n = len(pieces)

    def body(*refs):
        xs, outs, (send_sems, recv_sems) = refs[:n], refs[n:2 * n], refs[2 * n:]
        x, y, c = _place()
        copies = []
        for i in range(n):
            for p, (px, py, pc) in enumerate(_chip_peers(x, y, c)):
                copies.append(pltpu.make_async_remote_copy(
                    src_ref=xs[i].at[2 * px + py], dst_ref=outs[i].at[p], send_sem=send_sems.at[i * N_PEERS + p],
                    recv_sem=recv_sems.at[i * N_PEERS + p], device_id=(px, py, pc), device_id_type=MESH_ID))
                copies[-1].start()
        for cp in copies:
            cp.wait()

    hbm = pl.BlockSpec(memory_space=pltpu.HBM)
    return pl.pallas_call(
        body, out_shape=[jax.ShapeDtypeStruct((N_PEERS,) + p.shape[1:], p.dtype) for p in pieces],
        in_specs=[hbm] * n, out_specs=[hbm] * n,
        scratch_shapes=[pltpu.SemaphoreType.DMA((n * N_PEERS,)), pltpu.SemaphoreType.DMA((n * N_PEERS,))],
        compiler_params=_params(), name=name,
    )(*pieces)


def pair_sum(name, half0, half1, got, out_dtype):
    r, w = got.shape
    tr = _pick(r, tuple(t for t in (512, 256, 128, 64, 32, 16, 8) if t * w * 4 <= 2 << 20))

    def body(a0, a1, g, o):
        mine = jnp.where(lax.axis_index("c") == 0, a0[...], a1[...])
        o[...] = (mine.astype(F32) + g[...].astype(F32)).astype(out_dtype)

    spec = pl.BlockSpec((tr, w), lambda i: (i, 0))
    return pl.pallas_call(body, grid=(r // tr,), in_specs=[spec] * 3, out_specs=spec,
                          out_shape=jax.ShapeDtypeStruct((r, w), out_dtype),
                          compiler_params=_params(("parallel",)), name=name)(half0, half1, got)


def _sum_devices(name, got, fold=False):
    def body(a_ref, *o_refs):
        acc = a_ref[0]
        for i in range(1, N_DEV):
            acc = acc + a_ref[i]
        o_refs[0][...] = acc
        if fold:
            o_refs[1][...] = acc + pltpu.roll(acc, SUBLANES // 2, 0)

    shape = jax.ShapeDtypeStruct(got.shape[1:], F32)
    out = pl.pallas_call(body, out_shape=[shape] * (2 if fold else 1), compiler_params=_params(), name=name)(got)
    return out if fold else out[0]


def _w_in_bounds(d, pw, scw):
    off_z = 3 * DN_WIDTH
    off_a = off_z + DN_WIDTH
    off_pool = off_a + 2 * N_DIRHEAD
    off_sc = off_pool + pw
    off_gate = off_sc + 3 * scw
    return (0, off_z, off_a, off_pool, off_sc, off_gate, off_gate + 3 * d)


def _misc_widths(pw, scw):
    return (DN_WIDTH, pw, 3 * scw, LANES)


def split_matrices(full, pw, scw):
    depth, d = len(full["w_in"]), full["w_in"][0].shape[0]
    b = _w_in_bounds(d, pw, scw)
    out = {k: [] for k in ("qkv", "gate", "misc", "gu", "br_a", "br_b", "br_c", "o", "down")}
    for l in range(depth):
        w = full["w_in"][l]
        out["qkv"].append(w[:, b[0]:b[1]])
        out["gate"].append(w[:, b[5]:b[6]])
        out["misc"].append(jnp.concatenate([w[:, b[1]:b[2]], w[:, b[3]:b[5]], _pad_lanes(w[:, b[2]:b[3]])], axis=1))
        out["gu"].append(full["w_gu"][l])
        for k in ("br_a", "br_b", "br_c", "o", "down"):
            out[k].append(full["w_" + k][l])
    return out


def join_matrix_grads(g, pw, scw):
    depth = len(g["qkv"])
    n_z, n_rest, n_ab = DN_WIDTH, pw + 3 * scw, 2 * N_DIRHEAD
    w_in = []
    for l in range(depth):
        m = g["misc"][l]
        w_in.append(jnp.concatenate([g["qkv"][l], m[:, :n_z], m[:, n_z + n_rest:n_z + n_rest + n_ab],
                                     m[:, n_z:n_z + n_rest], g["gate"][l]], axis=1))
    out = {"w_in": w_in, "w_gu": g["gu"]}
    for k in ("br_a", "br_b", "br_c", "o", "down"):
        out["w_" + k] = g[k]
    return {k: [a.astype(MXU_DTYPE) for a in v] for k, v in out.items()}


def split_cols(x, widths):
    edges = np.cumsum((0,) + tuple(widths))

    def cut(x):
        return tuple(x[:, a:b] for a, b in zip(edges[:-1], edges[1:]))

    f = jax.custom_vjp(cut)
    f.defvjp(lambda x: (cut(x), None), lambda _, cts: (jnp.concatenate(cts, axis=1),))
    return f(x)


def _row(v):
    return v.reshape(1, -1)


def _pad_lanes(v, width=LANES):
    return jnp.pad(v, ((0, 0), (0, width - v.shape[1])))


def _block_diag(blocks):
    g, n, _ = blocks.shape
    out = jnp.zeros((g * n, g * n), blocks.dtype)
    for i in range(g):
        out = out.at[i * n:(i + 1) * n, i * n:(i + 1) * n].set(blocks[i])
    return out


def local_loss(p, carriers, x, mod_lat, mod_ctx, wts, ctx, target):
    ctx_rows, d = ctx.shape
    depth = len(wts["qkv"])
    pw, scw = p["pool_scale"].shape[1], p["sc_conv_w"].shape[2]
    ff = wts["down"][0].shape[0]
    ct = ctx_rows // ROW_TILE
    xs = jnp.concatenate([ctx, x], axis=0)
    seg = lambda l, k: jnp.stack([mod_ctx[l, k], mod_lat[l, k]]).reshape(2, 1, d)
    dn = gate2 = None
    for l in range(depth):
        sh1, sc1, g1, sh2, sc2, g2 = (seg(l, k) for k in range(6))
        tag = f"l{l}_"
        lin = lambda key, a: mm(tag + key, a, wts[key][l], carriers[key][l])
        if l == 0:
            (h1,) = rowwise(tag + "mod", _fn_modulate, [xs], [_row(p["norm1_g"][l]), sh1, sc1], seg=(False, True, True),
                            outs=[(d, MXU_DTYPE, 1)], ctx_tiles=ct)
        else:
            xs, h1 = rowwise(tag + "resmod1", _fn_resmod, [xs, dn], [gate2, _row(p["norm1_g"][l]), sh1, sc1],
                             seg=(True, False, True, True), outs=[(d, F32, 1), (d, MXU_DTYPE, 1)], ctx_tiles=ct)
        fan = ("qkv", "gate", "misc")
        p_qkv, p_gate, p_misc = mm_fanout(tag + "in", h1, [wts[k][l] for k in fan], [carriers[k][l] for k in fan])
        p_z, p_pool, p_sc, p_ab = split_cols(p_misc, _misc_widths(pw, scw))
        cw = p["dn_conv_w"][l]
        q, k, v, gb = rowwise(
            tag + "dnprep", _fn_dnprep, [p_qkv, p_ab],
            [cw[0:1], cw[1:2], cw[2:3], _pad_lanes(p["dn_a_log"][l].reshape(1, -1)), _pad_lanes(p["dn_dt_bias"][l].reshape(1, -1))],
            halo=(True, False), parts_r=(3 * DN_HEADS, 1), parts_v=(3 * DN_HEADS,) * 3 + (1, 1),
            outs=[(DN_WIDTH, F32, DN_HEADS)] * 3 + [(LANES, F32, 1)], ctx_tiles=ct)
        o_f, o_b = deltanet(tag + "dn", q, k, v, gb, ctx_rows)
        (oz,) = rowwise(tag + "dnpost", _fn_dnpost, [o_f, o_b, p_z], [_row(p["dn_norm_g"][l])], parts_r=(DN_HEADS,) * 3,
                        outs=[(DN_WIDTH, MXU_DTYPE, DN_HEADS)], ctx_tiles=ct)
        y_a = lin("br_a", oz)
        means = pool_means(tag + "box", p_pool, ctx_rows)
        (dpool,) = rowwise(tag + "poolsub", _fn_sub, [means, p_pool], [], outs=[(pw, MXU_DTYPE, 1)], ctx_tiles=ct)
        pool_mat = _block_diag(p["pool_w"][l])
        mixed = mm(tag + "poolw", dpool, pool_mat.astype(MXU_DTYPE), pool_mat)
        (yb_in,) = rowwise(tag + "poolscale", _fn_scale, [mixed], [_row(p["pool_scale"][l])], outs=[(pw, MXU_DTYPE, 1)], ctx_tiles=ct)
        y_b = lin("br_b", yb_in)
        sw = p["sc_conv_w"][l]
        (yc_in,) = rowwise(tag + "sconv", _fn_shortconv, [p_sc], [sw[0:1], sw[1:2], sw[2:3]], halo=(True,), parts_r=(3,),
                           outs=[(scw, MXU_DTYPE, 1)], ctx_tiles=ct)
        y_c = lin("br_c", yc_in)
        (y,) = rowwise(tag + "merge", _fn_merge, [p_gate, y_a, y_b, y_c], [], parts_r=(3, 1, 1, 1),
                       outs=[(d, MXU_DTYPE, 1)], ctx_tiles=ct)
        mix = lin("o", y)
        xs, h2 = rowwise(tag + "resmod2", _fn_resmod, [xs, mix], [g1, _row(p["norm2_g"][l]), sh2, sc2],
                         seg=(True, False, True, True), outs=[(d, F32, 1), (d, MXU_DTYPE, 1)], ctx_tiles=ct)
        (act,) = rowwise(tag + "swiglu", _fn_swiglu, [lin("gu", h2)], [], parts_r=(2,),
                         outs=[(ff, MXU_DTYPE, 1)], ctx_tiles=ct)
        dn = lin("down", act)
        gate2 = g2
    counts = jnp.concatenate([jnp.zeros((1, 1, LANES), F32), jnp.ones((1, 1, LANES), F32)])
    (total,) = rowwise("final", _fn_final, [xs, dn, target], [gate2, _row(p["final_norm_g"]), counts],
                       seg=(True, False, True), reds=[LANES], ctx_tiles=ct, skip=(0, 0, ct))
    return total[0, 0]


BIG = ("w_in", "w_br_a", "w_br_b", "w_br_c", "w_o", "w_gu", "w_down")
ROW_SHARDED = ("w_o", "w_down")
SMALL_REPL = ("norm1_g", "norm2_g", "dn_a_log", "dn_dt_bias", "dn_norm_g", "pool_w", "pool_scale", "final_norm_g")
SMALL_SHARD = ("dn_conv_w", "sc_conv_w")
WEIGHTS = ("c_ctx", "w_ada", "b_ada", "norm1_g", "norm2_g", "w_in", "dn_conv_w", "dn_a_log", "dn_dt_bias", "dn_norm_g", "pool_w",
           "pool_scale", "sc_conv_w", "w_br_a", "w_br_b", "w_br_c", "w_o", "w_gu", "w_down", "final_norm_g")
N_CHIPS, N_DEV = 4, 8
COMM_COLS = 1024


def _pack(arrays, rows_multiple, cols=COMM_COLS, dtype=F32):
    size = sum(int(np.prod(a.shape)) for a in arrays)
    rows = -(-size // (cols * rows_multiple)) * rows_multiple
    tail = [jnp.zeros((rows * cols - size,), dtype)] if rows * cols > size else []
    return jnp.concatenate([a.astype(dtype).reshape(-1) for a in arrays] + tail).reshape(rows, cols)


def _unpack(flat, shapes):
    out, pos = [], 0
    for s in shapes:
        n = int(np.prod(s))
        out.append(flat[pos:pos + n].reshape(s))
        pos += n
    return out


def _join_shards(parts, name):
    _, a, b = parts.shape
    if name in ROW_SHARDED:
        return parts.reshape(N_CHIPS * a, b)
    return jnp.moveaxis(parts, 0, 1).reshape(a, N_CHIPS * b)


def _cut_shards(full, name):
    k, n = full.shape
    if name in ROW_SHARDED:
        return full.reshape(N_CHIPS, k // N_CHIPS, n)
    return jnp.moveaxis(full.reshape(k, N_CHIPS, n // N_CHIPS), 1, 0)


def _dsilu(x):
    s = _sigmoid(x)
    return s + x * s * (1.0 - s)


def kernel(x, c, ctx, c_ctx, w_ada, b_ada, norm1_g, norm2_g, w_in, dn_conv_w, dn_a_log, dn_dt_bias, dn_norm_g, pool_w, pool_scale, sc_conv_w, w_br_a, w_br_b, w_br_c, w_o, w_gu, w_down, final_norm_g, loss_target, m_c_ctx, m_w_ada, m_b_ada, m_norm1_g, m_norm2_g, m_w_in, m_dn_conv_w, m_dn_a_log, m_dn_dt_bias, m_dn_norm_g, m_pool_w, m_pool_scale, m_sc_conv_w, m_w_br_a, m_w_br_b, m_w_br_c, m_w_o, m_w_gu, m_w_down, m_final_norm_g, v_c_ctx, v_w_ada, v_b_ada, v_norm1_g, v_norm2_g, v_w_in, v_dn_conv_w, v_dn_a_log, v_dn_dt_bias, v_dn_norm_g, v_pool_w, v_pool_scale, v_sc_conv_w, v_w_br_a, v_w_br_b, v_w_br_c, v_w_o, v_w_gu, v_w_down, v_final_norm_g):
    given = dict(locals())
    ix, iy, ic = _place()
    chip, dev = 2 * ix + iy, 4 * ix + 2 * iy + ic
    d = x.shape[-1]
    depth = w_in.shape[0]
    ada_cols = w_ada.shape[2]
    spare = 2 * SUBLANES - N_DEV - 1

    got0 = all_gather8("gather_cond", _pack([c, dn_conv_w, sc_conv_w], SUBLANES)).reshape(N_DEV, -1)
    c_all = got0[:, :d]
    taps = [_unpack(got0[2 * j, d:], [dn_conv_w.shape, sc_conv_w.shape]) for j in range(N_CHIPS)]
    full = {"dn_conv_w": jnp.concatenate([t[0] for t in taps], axis=-1),
            "sc_conv_w": jnp.concatenate([t[1] for t in taps], axis=-1)}
    assert depth == 2, "the two cores of a chip split the layers between them"
    got_layer = gather_chips("gather_weights", [given[n].astype(MXU_DTYPE) for n in BIG])
    other_layer = swap_sibling("swap_weights", got_layer)
    mats = {n: [_join_shards(jnp.where(ic == l, got_layer[i], other_layer[i]), n) for l in range(depth)]
            for i, n in enumerate(BIG)}
    for n in SMALL_REPL:
        full[n] = given[n]
    pw, scw = pool_scale.shape[1], full["sc_conv_w"].shape[2]
    wts = split_matrices(mats, pw, scw)
    carriers = {k: [jnp.zeros(w.shape, F32) for w in ws] for k, ws in wts.items()}

    cond = jnp.concatenate([c_all, c_ctx[None], jnp.zeros((spare, d), F32)])
    s_cond = _ew("silu_cond", _silu, cond)
    mod_cols = jnp.concatenate([_mm(f"ada{l}_f", s_cond, w_ada[l], NN) for l in range(depth)], axis=1)
    mod_got = all_gather8("gather_mod", mod_cols).reshape(N_DEV, 2 * SUBLANES, depth, ada_cols)
    mod_all = jnp.concatenate([mod_got[2 * j] for j in range(N_CHIPS)], axis=-1) + b_ada[None]
    mod_lat = lax.dynamic_index_in_dim(mod_all, dev, 0, keepdims=False).reshape(depth, 6, d)
    mod_ctx = mod_all[N_DEV].reshape(depth, 6, d)

    loss_local, (g_full, g_mats, grad_x, g_mod_lat, g_mod_ctx) = jax.value_and_grad(local_loss, argnums=(0, 1, 2, 3, 4))(
        full, carriers, x[0], mod_lat, mod_ctx, wts, ctx[0], loss_target[0])
    g_mats_full = join_matrix_grads(g_mats, pw, scw)
    loss = lax.psum(loss_local, ("x", "y", "c"))

    dmod_cols = (2 * depth * 6 * d) // SUBLANES
    dmod = all_gather8("gather_dmod", _pack([g_mod_lat, g_mod_ctx], SUBLANES, cols=dmod_cols))
    dmod = dmod.reshape(N_DEV, SUBLANES, dmod_cols)
    dmod_sum, dmod_fold = _sum_devices("reduce_dmod", dmod, fold=True)
    half_rows = SUBLANES // 2
    grad_b_ada = dmod_fold[:half_rows].reshape(depth, 6 * d)
    dctx_sum = dmod_sum[half_rows:].reshape(1, depth, 6 * d)
    d9 = jnp.concatenate([dmod[:, :half_rows].reshape(N_DEV, depth, 6 * d), dctx_sum, jnp.zeros((spare, depth, 6 * d), F32)])
    d9 = lax.dynamic_slice_in_dim(d9, chip * ada_cols, ada_cols, axis=2)
    grad_w_ada = jnp.stack([_mm(f"ada{l}_dw", s_cond, d9[:, l], TN) for l in range(depth)])
    ds_cond = [_mm(f"ada{l}_da", d9[:, l], w_ada[l], NT) for l in range(depth)]
    dsilu_part = _sum_rows("sum_dcond", ds_cond)[N_DEV]

    small_names = SMALL_REPL + SMALL_SHARD
    small_grads = [dsilu_part] + [g_full[n] for n in small_names]
    small_shapes = [a.shape for a in small_grads]
    n_small = sum(int(np.prod(s)) for s in small_shapes)
    cols = -(-n_small // (SUBLANES * LANES)) * LANES
    got2 = all_gather8("gather_small", _pack(small_grads, SUBLANES, cols=cols)).reshape(N_DEV, SUBLANES, cols)
    small_sum = _unpack(_sum_devices("reduce_small", got2).reshape(-1), small_shapes)
    grads = dict(zip(small_names, small_sum[1:]))
    grads["c_ctx"] = _ew("dsilu", lambda g, z: 0.5 * g * _dsilu(z), _row(small_sum[0]), _row(c_ctx)).reshape(-1)
    grads["w_ada"], grads["b_ada"] = grad_w_ada, grad_b_ada
    for n in SMALL_SHARD:
        width = given[n].shape[-1]
        grads[n] = lax.dynamic_slice_in_dim(grads[n], chip * width, width, axis=-1)

    from_sibling = swap_sibling("swap_layers", [tuple(g_mats_full[n]) for n in BIG])
    pieces = [_cut_shards(pair_sum("sum_pair_" + n, *g_mats_full[n], from_sibling[i], MXU_DTYPE), n) for i, n in enumerate(BIG)]
    from_chips = scatter_chips("scatter_pieces", pieces)
    reduced = []
    for i, n in enumerate(BIG):
        own = lax.dynamic_index_in_dim(pieces[i], chip, 0, keepdims=False)
        a, b = own.shape
        view = lambda t: t.reshape(-1, b)
        reduced.append(_sum_rows("sum_chips_" + n, [view(own)] + [view(from_chips[i][p]) for p in range(N_PEERS)]).reshape(a, b))
    other_reduced = swap_sibling("swap_reduced", reduced)
    for i, n in enumerate(BIG):
        grads[n] = jnp.stack([jnp.where(ic == l, reduced[i], other_reduced[i]) for l in range(depth)])

    delta, new_m, new_v = {}, {}, {}
    large = BIG + ("w_ada",)
    for n in large:
        shape = given[n].shape
        flat = [a.reshape(-1, shape[-1]) for a in (given[n], grads[n], given["m_" + n], given["v_" + n])]
        res = adamw("adamw_" + n, *flat)
        delta[n], new_m[n], new_v[n] = (a.reshape(shape) for a in res)
    rest = [n for n in WEIGHTS if n not in large]
    rest_shapes = [given[n].shape for n in rest]
    packed = [_pack([src[pre + n] for n in rest], SUBLANES, cols=LANES)
              for src, pre in ((given, ""), (grads, ""), (given, "m_"), (given, "v_"))]
    for res, o in zip((delta, new_m, new_v), adamw("adamw_small", *packed)):
        for n, a in zip(rest, _unpack(o.reshape(-1), rest_shapes)):
            res[n] = a
    return (loss, grad_x[None], *[grads[n] for n in WEIGHTS], *[delta[n] for n in WEIGHTS],
            *[new_m[n] for n in WEIGHTS], *[new_v[n] for n in WEIGHTS])
```

```python
import functools
import math

import numpy as np
import jax
import jax.numpy as jnp
from jax import lax
from jax.experimental import pallas as pl
from jax.experimental.pallas import tpu as pltpu

F32 = jnp.float32
MXU_DTYPE = jnp.bfloat16
HIGHEST = lax.Precision.HIGHEST

DN_HEADS = 4
HEAD_DIM = 128
DN_WIDTH = DN_HEADS * HEAD_DIM
DN_CHUNK = 64
GRID_W = 64
EPS = 1e-6
POOL_WINDOWS = (2, 4, 8, 16)
N_DIRHEAD = 2 * DN_HEADS
ADAM_LR, ADAM_B1, ADAM_B2, ADAM_EPS, ADAM_WD, ADAM_STEP = 0.001, 0.9, 0.999, 1e-08, 0.01, 10

LANES = 128
SUBLANES = 8
ROW_TILE = 256
VMEM_LIMIT = 56 * 1024 * 1024

MESH_ID = pl.DeviceIdType.MESH
NN, NT, TN = ((1,), (0,)), ((1,), (1,)), ((0,), (0,))


def _params(sem=None):
    return pltpu.CompilerParams(dimension_semantics=sem, vmem_limit_bytes=VMEM_LIMIT)


def _pick(n, cands):
    for c in cands:
        if c <= n and n % c == 0:
            return c
    return n


def _mm(name, a, b, dims, out_dtype=F32):
    if dims == NN:
        (m, kk), (_, n) = a.shape, b.shape
    elif dims == NT:
        (m, kk), (n, _) = a.shape, b.shape
    else:
        (kk, m), (_, n) = a.shape, b.shape
    tm = _pick(m, (768, 1024, 1408, 512, 256, 128, 64, 32, 16, 8))
    tn = _pick(n, (1536, 1024, 1408, 1664, 768, 896, 512, 256, 128))
    tk = kk if dims == NN and kk <= 2816 else _pick(kk, (1024, 1408, 768, 512, 256, 128))
    gi, gj, gl = m // tm, n // tn, kk // tk
    if dims == NN:
        a_spec = pl.BlockSpec((tm, tk), lambda i, j, l: (i, l))
        b_spec = pl.BlockSpec((tk, tn), lambda i, j, l: (l, j))
    elif dims == NT:
        a_spec = pl.BlockSpec((tm, tk), lambda i, j, l: (i, l))
        b_spec = pl.BlockSpec((tn, tk), lambda i, j, l: (j, l))
    else:
        a_spec = pl.BlockSpec((tk, tm), lambda i, j, l: (l, i))
        b_spec = pl.BlockSpec((tk, tn), lambda i, j, l: (l, j))
    direct = gl == 1
    use_acc = (not direct) and out_dtype != F32

    def body(a_ref, b_ref, o_ref, *scratch):
        part = lax.dot_general(a_ref[...].astype(MXU_DTYPE), b_ref[...].astype(MXU_DTYPE), (dims, ((), ())),
                               preferred_element_type=F32)
        if direct:
            o_ref[...] = part.astype(out_dtype)
            return
        acc = scratch[0] if use_acc else o_ref
        l = pl.program_id(2)

        @pl.when(l == 0)
        def _():
            acc[...] = part

        @pl.when(l > 0)
        def _():
            acc[...] += part

        if use_acc:
            @pl.when(l == gl - 1)
            def _():
                o_ref[...] = acc[...].astype(out_dtype)

    return pl.pallas_call(
        body, grid=(gi, gj, gl), in_specs=[a_spec, b_spec],
        out_specs=pl.BlockSpec((tm, tn), lambda i, j, l: (i, j)),
        out_shape=jax.ShapeDtypeStruct((m, n), out_dtype),
        scratch_shapes=[pltpu.VMEM((tm, tn), F32)] if use_acc else [],
        compiler_params=_params(("parallel", "parallel", "arbitrary")), name=name,
    )(a, b)


def mm(name, a, w, carrier):
    @jax.custom_vjp
    def f(a, w, carrier):
        return _mm(name + "_f", a, w, NN, out_dtype=MXU_DTYPE)

    def fwd(a, w, carrier):
        return f(a, w, carrier), (a, w)

    def bwd(res, dc):
        a, w = res
        da = _mm(name + "_da", dc, w, NT, out_dtype=a.dtype)
        dw = _mm(name + "_dw", a, dc, TN)
        return da, None, dw

    f.defvjp(fwd, bwd)
    return f(a, w, carrier)


def _mm_nt_sum(name, dcs, ws, out_dtype):
    m, kk = dcs[0].shape[0], ws[0].shape[0]
    tm = _pick(m, (768, 1024, 512, 256, 128, 64, 32, 16, 8))
    tns = [_pick(w.shape[1], (1664, 1536, 1024, 768, 896, 512, 256, 128)) for w in ws]
    counts = [w.shape[1] // tn for w, tn in zip(ws, tns)]
    starts = [sum(counts[:g]) for g in range(len(ws))]
    steps = sum(counts)

    def col(g):
        return lambda i, t: jnp.clip(t - starts[g], 0, counts[g] - 1)

    def body(*refs):
        dc_refs, w_refs, o_ref, acc = refs[:len(ws)], refs[len(ws):2 * len(ws)], refs[-2], refs[-1]
        t = pl.program_id(1)

        @pl.when(t == 0)
        def _():
            acc[...] = jnp.zeros_like(acc)

        for g in range(len(ws)):
            @pl.when(jnp.logical_and(t >= starts[g], t < starts[g] + counts[g]))
            def _():
                acc[...] += lax.dot_general(dc_refs[g][...].astype(MXU_DTYPE), w_refs[g][...].astype(MXU_DTYPE),
                                            (NT, ((), ())), preferred_element_type=F32)

        @pl.when(t == steps - 1)
        def _():
            o_ref[...] = acc[...].astype(out_dtype)

    dc_specs = [pl.BlockSpec((tm, tns[g]), (lambda c: lambda i, t: (i, c(i, t)))(col(g))) for g in range(len(ws))]
    w_specs = [pl.BlockSpec((kk, tns[g]), (lambda c: lambda i, t: (0, c(i, t)))(col(g))) for g in range(len(ws))]
    return pl.pallas_call(
        body, grid=(m // tm, steps), in_specs=dc_specs + w_specs,
        out_specs=pl.BlockSpec((tm, kk), lambda i, t: (i, 0)),
        out_shape=jax.ShapeDtypeStruct((m, kk), out_dtype),
        scratch_shapes=[pltpu.VMEM((tm, kk), F32)],
        compiler_params=_params(("parallel", "arbitrary")), name=name,
    )(*dcs, *ws)


def mm_fanout(name, a, ws, carriers):
    n = len(ws)

    @jax.custom_vjp
    def f(a, ws, carriers):
        return tuple(_mm(f"{name}{g}_f", a, ws[g], NN, out_dtype=MXU_DTYPE) for g in range(n))

    def fwd(a, ws, carriers):
        return f(a, ws, carriers), (a, ws)

    def bwd(res, dcs):
        a, ws = res
        da = _mm_nt_sum(name + "_da", list(dcs), list(ws), a.dtype)
        dws = tuple(_mm(f"{name}{g}_dw", a, dcs[g], TN) for g in range(n))
        return da, None, dws

    f.defvjp(fwd, bwd)
    return f(a, tuple(ws), tuple(carriers))


def _split(x, parts):
    w = x.shape[-1] // parts
    return [x[:, k * w:(k + 1) * w] for k in range(parts)]


def _cat(xs):
    return xs[0] if len(xs) == 1 else jnp.concatenate(xs, axis=-1)


def _shift_rows(x, prev_ref, next_ref, i, ctx_tiles, nt):
    tr = x.shape[0]
    rid = lax.broadcasted_iota(jnp.int32, x.shape, 0)
    first = jnp.logical_or(i == 0, i == ctx_tiles)
    last = jnp.logical_or(i == ctx_tiles - 1, i == nt - 1)
    prow = jnp.where(first, 0.0, prev_ref[SUBLANES - 1:SUBLANES, :].astype(F32))
    nrow = jnp.where(last, 0.0, next_ref[0:1, :].astype(F32))
    xm = jnp.where(rid == 0, prow, pltpu.roll(x, 1, 0))
    xp = jnp.where(rid == tr - 1, nrow, pltpu.roll(x, tr - 1, 0))
    return xm, xp


def rowwise(name, fn, rows, vecs, *, halo=(), seg=(), parts_r=None, parts_v=None, outs=(), reds=(), ctx_tiles=1, skip=None):
    nr, nv = len(rows), len(vecs)
    halo = tuple(halo) or (False,) * nr
    seg = tuple(seg) or (False,) * nv
    skip = tuple(skip or (0,) * nr)
    parts_r = tuple(parts_r or (1,) * nr)
    parts_v = tuple(parts_v or (1,) * nv)
    r_total = rows[0].shape[0]
    tr = ROW_TILE
    nt = r_total // tr
    assert r_total % tr == 0 and (ctx_tiles > 0 or not any(seg)) and not any(h and s for h, s in zip(halo, skip))

    def tile_map(i):
        return lambda t: (jnp.maximum(t - skip[i], 0), 0)

    def row_specs():
        sp = []
        for i, r in enumerate(rows):
            w = r.shape[1]
            sp.append(pl.BlockSpec((tr, w), tile_map(i)))
            if halo[i]:
                k = tr // SUBLANES
                sp.append(pl.BlockSpec((SUBLANES, w), lambda t: (jnp.maximum(t * k - 1, 0), 0)))
                sp.append(pl.BlockSpec((SUBLANES, w), lambda t: (jnp.minimum((t + 1) * k, nt * k - 1), 0)))
        return sp

    def vec_spec(j):
        w = vecs[j].shape[-1]
        if seg[j]:
            return pl.BlockSpec((None, 1, w), lambda t: (jnp.where(t >= ctx_tiles, 1, 0), 0, 0))
        return pl.BlockSpec((1, w), lambda t: (0, 0))

    def row_args(rv):
        a = []
        for i in range(nr):
            a += [rv[i]] * 3 if halo[i] else [rv[i]]
        return a

    def load(refs, t):
        pos, rp = 0, []
        for i in range(nr):
            x = refs[pos][...].astype(F32)
            if halo[i]:
                xm, xp = _shift_rows(x, refs[pos + 1], refs[pos + 2], t, ctx_tiles, nt)
                rp.append(list(zip(_split(x, parts_r[i]), _split(xm, parts_r[i]), _split(xp, parts_r[i]))))
                pos += 3
            else:
                rp.append(_split(x, parts_r[i]))
                pos += 1
        vp = []
        for j in range(nv):
            vp.append(_split(refs[pos][...].astype(F32), parts_v[j]))
            pos += 1
        return rp, vp, refs[pos:]

    n_out, n_red = len(outs), len(reds)

    def fwd_call(*rv):
        def body(*refs):
            t = pl.program_id(0)
            rp, vp, rest = load(refs, t)
            o_parts, r_parts = fn(rp, vp)
            for k in range(n_out):
                rest[k][...] = _cat(o_parts[k]).astype(outs[k][1])
            for k in range(n_red):
                ref = rest[n_out + k]

                @pl.when(t == 0)
                def _():
                    ref[...] = r_parts[k]

                @pl.when(t > 0)
                def _():
                    ref[...] += r_parts[k]

        res = pl.pallas_call(
            body, grid=(nt,),
            in_specs=row_specs() + [vec_spec(j) for j in range(nv)],
            out_specs=[pl.BlockSpec((tr, o[0]), lambda t: (t, 0)) for o in outs]
            + [pl.BlockSpec((1, w), lambda t: (0, 0)) for w in reds],
            out_shape=[jax.ShapeDtypeStruct((r_total, o[0]), o[1]) for o in outs]
            + [jax.ShapeDtypeStruct((1, w), F32) for w in reds],
            compiler_params=_params(("arbitrary",)), name=name + "_f",
        )(*row_args(rv), *rv[nr:])
        return tuple(res)

    def bwd_call(rv, cts):
        def body(*refs):
            t = pl.program_id(0)
            rp, vp, rest = load(refs, t)
            ct_o = [_split(rest[k][...].astype(F32), outs[k][2]) for k in range(n_out)]
            ct_r = [rest[n_out + k][...] for k in range(n_red)]
            rest = rest[n_out + n_red:]
            _, vjp = jax.vjp(fn, rp, vp)
            d_rp, d_vp = vjp((ct_o, ct_r))
            pos = 0
            for i in range(nr):
                if halo[i]:
                    for c in range(3):
                        rest[pos + c][...] = _cat([p[c] for p in d_rp[i]])
                    pos += 3
                else:
                    rest[pos][...] = _cat(d_rp[i]).astype(rows[i].dtype)
                    pos += 1
            for j in range(nv):
                ref, val = rest[pos + j], _cat(d_vp[j])
                start = jnp.logical_or(t == 0, t == ctx_tiles) if seg[j] else t == 0

                @pl.when(start)
                def _():
                    ref[...] = val

                @pl.when(jnp.logical_not(start))
                def _():
                    ref[...] += val

        d_row_specs, d_row_shapes = [], []
        for i, r in enumerate(rows):
            w = r.shape[1]
            for _ in range(3 if halo[i] else 1):
                d_row_specs.append(pl.BlockSpec((tr, w), tile_map(i)))
                d_row_shapes.append(jax.ShapeDtypeStruct(r.shape, F32 if halo[i] else r.dtype))
        res = pl.pallas_call(
            body, grid=(nt,),
            in_specs=row_specs() + [vec_spec(j) for j in range(nv)]
            + [pl.BlockSpec((tr, o[0]), lambda t: (t, 0)) for o in outs]
            + [pl.BlockSpec((1, w), lambda t: (0, 0)) for w in reds],
            out_specs=d_row_specs + [vec_spec(j) for j in range(nv)],
            out_shape=d_row_shapes + [jax.ShapeDtypeStruct(v.shape, F32) for v in vecs],
            compiler_params=_params(("arbitrary",)), name=name + "_b",
        )(*row_args(rv), *rv[nr:], *cts)
        d_rows, pos = [], 0
        for i in range(nr):
            if halo[i]:
                d_rows.append(_unshift(res[pos], res[pos + 1], res[pos + 2], ctx_tiles * tr).astype(rows[i].dtype))
                pos += 3
            else:
                d_rows.append(res[pos])
                pos += 1
        return tuple(d_rows) + tuple(res[pos:])

    @jax.custom_vjp
    def f(*rv):
        return fwd_call(*rv)

    f.defvjp(lambda *rv: (fwd_call(*rv), rv), lambda rv, cts: bwd_call(rv, cts))
    return f(*rows, *vecs)


def _unshift(d, dm, dp, ctx_rows):
    r = d.shape[0]
    t = lax.broadcasted_iota(jnp.int32, (r, 1), 0)
    zero = jnp.zeros((1, d.shape[1]), d.dtype)
    from_m = jnp.concatenate([dm[1:], zero], axis=0)
    from_p = jnp.concatenate([zero, dp[:-1]], axis=0)
    from_m = jnp.where(t == ctx_rows - 1, 0.0, from_m)
    from_p = jnp.where(t == ctx_rows, 0.0, from_p)
    return d + from_m + from_p


def _sigmoid(x):
    return 0.5 * (jnp.tanh(0.5 * x) + 1.0)


def _silu(x):
    return x * _sigmoid(x)


def _softplus(x):
    return jnp.maximum(x, 0.0) + jnp.log(1.0 + jnp.exp(-jnp.abs(x)))


def _rms(x, g):
    return x * lax.rsqrt(jnp.mean(x * x, axis=-1, keepdims=True) + EPS) * g


def _fn_modulate(r, v):
    (x,), (g,), (sh,), (sc,) = r[0], v[0], v[1], v[2]
    return [[_rms(x, g) * (1.0 + sc) + sh]], []


def _fn_resmod(r, v):
    (x,), (y,) = r
    (gate,), (g,), (sh,), (sc,) = v
    xn = x + gate * y
    return [[xn], [_rms(xn, g) * (1.0 + sc) + sh]], []


def _fn_final(r, v):
    (x,), (y,), (tgt,) = r
    (gate,), (g,), (counts,) = v
    err = _rms(x + gate * y, g) - tgt
    row_loss = jnp.mean(err * err, axis=-1, keepdims=True)
    total = 0.5 * jnp.sum(row_loss, axis=0, keepdims=True)
    return [], [total * counts]


def _fn_dnprep(r, v):
    qkv, (ab,) = r
    w0, w1, w2, (alog,), (dtb,) = v
    out = [[], [], []]
    for n, (x, xm, xp) in enumerate(qkv):
        which = n // DN_HEADS
        y = _silu(xm * w0[n] + x * w1[n] + xp * w2[n])
        if which < 2:
            y = y * lax.rsqrt(jnp.sum(y * y, axis=-1, keepdims=True) + EPS)
        if which == 0:
            y = y * (HEAD_DIM ** -0.5)
        out[which].append(y)
    lane = lax.broadcasted_iota(jnp.int32, ab.shape, 1)
    g = -jnp.exp(alog) * _softplus(ab + dtb)
    gb = jnp.where(lane < N_DIRHEAD, g, jnp.where(lane < 2 * N_DIRHEAD, _sigmoid(ab), 0.0))
    return out + [[gb]], []


def _fn_dnpost(r, v):
    of, ob, z = r
    (g,) = v[0]
    return [[_rms(a + b, g) * _silu(c) for a, b, c in zip(of, ob, z)]], []


def _fn_sub(r, v):
    return [[r[0][0] - r[1][0]]], []


def _fn_scale(r, v):
    return [[r[0][0] * v[0][0]]], []


def _fn_shortconv(r, v):
    (xin, gb, gc), = r
    (w0,), (w1,), (w2,) = v
    u, um, up = (gc[k] * xin[k] for k in range(3))
    return [[gb[0] * (um * w0 + u * w1 + up * w2)]], []


def _fn_merge(r, v):
    gates, (ya,), (yb,), (yc,) = r
    return [[_sigmoid(gates[0]) * ya + _sigmoid(gates[1]) * yb + _sigmoid(gates[2]) * yc]], []


def _fn_swiglu(r, v):
    gate, up = r[0]
    return [[_silu(gate) * up]], []


def _dot3(a, b):
    (ah, al), (bh, bl) = _hi_lo(a), _hi_lo(b)
    dot = lambda x, y: lax.dot_general(x, y, (NN, ((), ())), preferred_element_type=F32)
    return dot(ah, bh) + (dot(ah, bl) + dot(al, bh))


def _bdot(a, b, dims):
    (ca,), (cb,) = dims
    return lax.dot_general(a.astype(MXU_DTYPE), b.astype(MXU_DTYPE), (((ca + 1,), (cb + 1,)), ((0,), (0,))),
                           preferred_element_type=F32)


def _hi_lo(a):
    hi = a.astype(MXU_DTYPE)
    return hi, (a - hi.astype(F32)).astype(MXU_DTYPE)


def _bdot3_raw(a, b, dims):
    (ah, al), (bh, bl) = _hi_lo(a), _hi_lo(b)
    return _bdot(ah, bh, dims) + (_bdot(ah, bl, dims) + _bdot(al, bh, dims))


@jax.custom_vjp
def _bdot3(a, b):
    return _bdot3_raw(a, b, NN)


_bdot3.defvjp(lambda a, b: (_bdot3_raw(a, b, NN), (a, b)),
              lambda res, ct: (_bdot3_raw(ct, res[1], NT), _bdot3_raw(res[0], ct, TN)))


def _inv_doubling(a):
    c = a.shape[-1]
    ii, jj = (lax.broadcasted_iota(jnp.int32, a.shape, d) for d in (1, 2))
    t = jnp.where(ii == jj, 1.0, 0.0) - a
    p = _bdot3_raw(a, a, NN)
    for _ in range(int(math.log2(c)) - 2):
        both = _bdot3_raw(jnp.concatenate([p, t], axis=1), p, NN)
        t = t + both[:, c:]
        p = both[:, :c]
    return t + _bdot3_raw(t, p, NN)


def _dn_gates(k4, gb):
    nb, c = N_DIRHEAD, k4.shape[1]
    k = jnp.concatenate([k4, k4], axis=0)
    lane = lax.broadcasted_iota(jnp.int32, gb.shape, 1)
    col = lambda j: jnp.sum(jnp.where(lane == j, gb, 0.0), axis=1, keepdims=True)
    g_col = jnp.concatenate([col(j)[None] for j in range(nb)], axis=0)
    b_col = jnp.concatenate([col(nb + j)[None] for j in range(nb)], axis=0)
    bi, ii, jj = (lax.broadcasted_iota(jnp.int32, (nb, c, c), a) for a in range(3))
    ahead = jnp.where(bi >= DN_HEADS, jj - ii, ii - jj)
    incl = ahead >= 0
    g_row = jnp.sum(jnp.where(ahead == 0, g_col, 0.0), axis=1, keepdims=True)
    gc_col = jnp.sum(jnp.where(incl, g_row, 0.0), axis=2, keepdims=True)
    gc_row = jnp.sum(jnp.where(ahead <= 0, g_col, 0.0), axis=1, keepdims=True)
    decay = jnp.where(incl, jnp.exp(jnp.where(incl, gc_col - gc_row, 0.0)), 0.0)
    return k, g_col, b_col, gc_col, decay, ahead > 0


def _dn_a(k4, gb):
    k, _, b_col, _, decay, strict = _dn_gates(k4, gb)
    return _bdot(k * b_col, k, NT) * jnp.where(strict, decay, 0.0)


def _dn_operands(q4, k4, v4, gb, t):
    k, g_col, b_col, gc_col, decay, _ = _dn_gates(k4, gb)
    q, v = (jnp.concatenate([a, a], axis=0) for a in (q4, v4))
    e_gc = jnp.exp(gc_col)
    d = k.shape[-1]
    uw = _bdot3(t, jnp.concatenate([v * b_col, k * b_col * e_gc], axis=2))
    u, w = uw[:, :, :d], uw[:, :, d:]
    g_last = jnp.sum(g_col, axis=1, keepdims=True)
    k_state = k * jnp.exp(g_last - gc_col)
    a_qk = _bdot(q, k, NT) * decay
    return u, w, q * e_gc, k_state, a_qk, jnp.broadcast_to(jnp.exp(g_last), (N_DIRHEAD, 1, LANES))


def _dn_step(s, u, w, qd, ks, aqk, gl):
    c = u.shape[1]
    on_state = _bdot(jnp.concatenate([w, qd], axis=1), s, NN)
    v_new = u - on_state[:, :c]
    o = on_state[:, c:] + _bdot(aqk, v_new, NN)
    return s * gl[:, :, :1] + _bdot(ks, v_new, TN), o


def _heads(x):
    return jnp.concatenate([x[None, :, h * HEAD_DIM:(h + 1) * HEAD_DIM] for h in range(DN_HEADS)], axis=0)


def _unheads(x):
    return jnp.concatenate([x[h] for h in range(x.shape[0])], axis=-1)


def _dn_rev(t, nc, n):
    return jnp.where(t < nc, nc - 1 - t, n - 1 - (t - nc))


PRE_CHUNKS = 2


def _pre_shapes(n):
    c, d, h = DN_CHUNK, HEAD_DIM, DN_HEADS
    shapes = [(n, h, c, d)] * 8 + [(n, h, c, c)] * 2 + [(n, h, 1, LANES)] * 2
    return shapes, [pl.BlockSpec((PRE_CHUNKS,) + s[1:], lambda t: (t, 0, 0, 0)) for s in shapes]


def _inverse_spec(n):
    shape = (n, N_DIRHEAD, DN_CHUNK, DN_CHUNK)
    return shape, pl.BlockSpec((PRE_CHUNKS,) + shape[1:], lambda t: (t, 0, 0, 0))


def _pre_row_specs():
    rows = PRE_CHUNKS * DN_CHUNK
    return pl.BlockSpec((rows, DN_HEADS * HEAD_DIM), lambda t: (t, 0)), pl.BlockSpec((rows, LANES), lambda t: (t, 0))


def _dn_pre_fwd(name, q, k, v, gb):
    n = q.shape[0] // DN_CHUNK
    shapes, specs = _pre_shapes(n)
    t_shape, t_spec = _inverse_spec(n)

    def body(q_ref, k_ref, v_ref, g_ref, *o_refs):
        for s in range(PRE_CHUNKS):
            rows = pl.ds(s * DN_CHUNK, DN_CHUNK)
            q4, k4, v4, gb_ = _heads(q_ref[rows, :]), _heads(k_ref[rows, :]), _heads(v_ref[rows, :]), g_ref[rows, :]
            t = _inv_doubling(_dn_a(k4, gb_))
            for i, r in enumerate(_dn_operands(q4, k4, v4, gb_, t)):
                o_refs[2 * i][s] = r[:DN_HEADS]
                o_refs[2 * i + 1][s] = r[DN_HEADS:]
            o_refs[-1][s] = t

    wide, narrow = _pre_row_specs()
    return pl.pallas_call(body, grid=(n // PRE_CHUNKS,), in_specs=[wide] * 3 + [narrow], out_specs=specs + [t_spec],
                          out_shape=[jax.ShapeDtypeStruct(s, F32) for s in shapes + [t_shape]],
                          compiler_params=_params(("parallel",)), name=name + "_pre_f")(q, k, v, gb)


def _dn_pre_bwd(name, q, k, v, gb, inv, cts):
    n = q.shape[0] // DN_CHUNK
    _, specs = _pre_shapes(n)
    _, t_spec = _inverse_spec(n)
    n_ct = len(specs)

    def body(q_ref, k_ref, v_ref, g_ref, t_ref, *refs):
        for s in range(PRE_CHUNKS):
            rows = pl.ds(s * DN_CHUNK, DN_CHUNK)
            ct = tuple(jnp.concatenate([refs[i][s], refs[i + 1][s]], axis=0) for i in range(0, n_ct, 2))
            q4, k4, v4, gb_, t = _heads(q_ref[rows, :]), _heads(k_ref[rows, :]), _heads(v_ref[rows, :]), g_ref[rows, :], t_ref[s]
            _, vjp = jax.vjp(_dn_operands, q4, k4, v4, gb_, t)
            dq, dk, dv, dg, dt = vjp(ct)
            da = -_bdot3_raw(_bdot3_raw(t, dt, TN), t, NT)
            _, vjp_a = jax.vjp(_dn_a, k4, gb_)
            dk_a, dg_a = vjp_a(da)
            for r, val in zip(refs[n_ct:], (_unheads(dq), _unheads(dk + dk_a), _unheads(dv), dg + dg_a)):
                r[rows, :] = val

    wide, narrow = _pre_row_specs()
    return pl.pallas_call(body, grid=(n // PRE_CHUNKS,), in_specs=[wide] * 3 + [narrow, t_spec] + specs,
                          out_specs=[wide] * 3 + [narrow],
                          out_shape=[jax.ShapeDtypeStruct(q.shape, F32)] * 3 + [jax.ShapeDtypeStruct(gb.shape, F32)],
                          compiler_params=_params(("parallel",)), name=name + "_pre_b")(q, k, v, gb, inv, *cts)


def _scan_specs(pre_shapes, fw, bw):
    maps = (lambda t: (fw(t), 0, 0, 0), lambda t: (bw(t), 0, 0, 0))
    return [pl.BlockSpec((None,) + s[1:], maps[i % 2]) for i, s in enumerate(pre_shapes)]


def _dn_scan_fwd(name, pre, nc):
    n = pre[0].shape[0]
    r = n * DN_CHUNK
    hd, nh, nb = HEAD_DIM, DN_HEADS, N_DIRHEAD
    shapes, _ = _pre_shapes(n)
    n_in = len(shapes)

    def body(*refs):
        ins, (of_ref, ob_ref, sf_ref, sb_ref, s_scr) = refs[:n_in], refs[n_in:]
        t = pl.program_id(0)

        @pl.when(t == 0)
        def _():
            s_scr[...] = jnp.zeros_like(s_scr)

        s = s_scr[...]
        sf_ref[...] = s[:nh]
        sb_ref[...] = s[nh:]
        args = [jnp.concatenate([ins[i][...], ins[i + 1][...]], axis=0) for i in range(0, n_in, 2)]
        s2, o = _dn_step(s, *args)
        s_scr[...] = s2
        of_ref[...] = _unheads(o[:nh])
        ob_ref[...] = _unheads(o[nh:])

    fw = lambda t: t
    bw = lambda t: _dn_rev(t, nc, n)
    wide, st = (DN_CHUNK, nh * hd), (None, nh, hd, hd)
    return pl.pallas_call(
        body, grid=(n,), in_specs=_scan_specs(shapes, fw, bw),
        out_specs=[pl.BlockSpec(wide, lambda t: (fw(t), 0)), pl.BlockSpec(wide, lambda t: (bw(t), 0)),
                   pl.BlockSpec(st, lambda t: (fw(t), 0, 0, 0)), pl.BlockSpec(st, lambda t: (bw(t), 0, 0, 0))],
        out_shape=[jax.ShapeDtypeStruct((r, nh * hd), F32)] * 2 + [jax.ShapeDtypeStruct((n, nh, hd, hd), F32)] * 2,
        scratch_shapes=[pltpu.VMEM((nb, hd, hd), F32)],
        compiler_params=_params(("arbitrary",)), name=name + "_scan_f",
    )(*pre)


def _dn_scan_bwd(name, pre, sall_f, sall_b, do_f, do_b, nc):
    n = pre[0].shape[0]
    hd, nh, nb = HEAD_DIM, DN_HEADS, N_DIRHEAD
    shapes, _ = _pre_shapes(n)
    n_in = len(shapes)

    def body(*refs):
        ins, (sf_ref, sb_ref, dof_ref, dob_ref) = refs[:n_in], refs[n_in:n_in + 4]
        outs, ds_scr = refs[n_in + 4:2 * n_in + 4], refs[2 * n_in + 4]
        t = pl.program_id(0)

        @pl.when(t == 0)
        def _():
            ds_scr[...] = jnp.zeros_like(ds_scr)

        args = [jnp.concatenate([ins[i][...], ins[i + 1][...]], axis=0) for i in range(0, n_in, 2)]
        s = jnp.concatenate([sf_ref[...], sb_ref[...]], axis=0)
        do = jnp.concatenate([_heads(dof_ref[...]), _heads(dob_ref[...])], axis=0)
        _, vjp = jax.vjp(_dn_step, s, *args)
        cts = vjp((ds_scr[...], do))
        ds_scr[...] = cts[0]
        for i, ct in enumerate(cts[1:]):
            outs[2 * i][...] = ct[:nh]
            outs[2 * i + 1][...] = ct[nh:]

    fw = lambda t: n - 1 - t
    bw = lambda t: _dn_rev(n - 1 - t, nc, n)
    wide, st = (DN_CHUNK, nh * hd), (None, nh, hd, hd)
    return tuple(pl.pallas_call(
        body, grid=(n,),
        in_specs=_scan_specs(shapes, fw, bw)
        + [pl.BlockSpec(st, lambda t: (fw(t), 0, 0, 0)), pl.BlockSpec(st, lambda t: (bw(t), 0, 0, 0)),
           pl.BlockSpec(wide, lambda t: (fw(t), 0)), pl.BlockSpec(wide, lambda t: (bw(t), 0))],
        out_specs=_scan_specs(shapes, fw, bw), out_shape=[jax.ShapeDtypeStruct(s, F32) for s in shapes],
        scratch_shapes=[pltpu.VMEM((nb, hd, hd), F32)],
        compiler_params=_params(("arbitrary",)), name=name + "_scan_b",
    )(*pre, sall_f, sall_b, do_f, do_b))


def deltanet(name, q, k, v, gb, ctx_rows):
    nc = ctx_rows // DN_CHUNK

    @jax.custom_vjp
    def pre(q, k, v, gb):
        return tuple(_dn_pre_fwd(name, q, k, v, gb)[:-1])

    def pre_fwd(*a):
        *ops, inv = _dn_pre_fwd(name, *a)
        return tuple(ops), (a, inv)

    pre.defvjp(pre_fwd, lambda res, cts: tuple(_dn_pre_bwd(name, *res[0], res[1], cts)))

    @jax.custom_vjp
    def scan(*ops):
        return tuple(_dn_scan_fwd(name, ops, nc)[:2])

    def scan_fwd(*ops):
        of, ob, sf, sb = _dn_scan_fwd(name, ops, nc)
        return (of, ob), (ops, sf, sb)

    scan.defvjp(scan_fwd, lambda res, cts: _dn_scan_bwd(name, res[0], res[1], res[2], cts[0], cts[1], nc))
    return scan(*pre(q, k, v, gb))


def _box_matrix(l, w):
    lo, hi = w // 2, w - 1 - w // 2
    pos = np.arange(l)
    start, end = np.clip(pos - lo, 0, l), np.clip(pos + hi + 1, 0, l)
    col = np.arange(l)[None, :]
    return ((col >= start[:, None]) & (col < end[:, None])) / (end - start)[:, None].astype(np.float64)


def _pool_matrices(ctx_rows, grid_rows):
    assert ctx_rows == ROW_TILE and ROW_TILE % GRID_W == 0
    ctx = np.stack([_box_matrix(ctx_rows, w) for w in POOL_WINDOWS])
    cols = np.stack([np.kron(np.eye(ROW_TILE // GRID_W), _box_matrix(GRID_W, w)) for w in POOL_WINDOWS])
    rows = np.stack([_box_matrix(grid_rows, w) for w in POOL_WINDOWS])[None]
    return np.stack([ctx, cols]).astype(np.float32), rows.astype(np.float32)


def _pool_apply(name, x, mats, group_w, seg_tiles, lane_tile):
    r, w = x.shape
    b = mats.shape[-1]
    ng = mats.shape[1]
    period = ng * group_w

    def body(x_ref, m_ref, o_ref):
        xv = x_ref[...].astype(F32)
        lane = lax.broadcasted_iota(jnp.int32, xv.shape, 1)
        grp = (lane % period) // group_w
        acc = jnp.zeros_like(xv)
        for g in range(ng):
            acc = acc + jnp.where(grp == g, _dot3(m_ref[g], xv), 0.0)
        o_ref[...] = acc

    return pl.pallas_call(
        body, grid=(r // b, w // lane_tile),
        in_specs=[pl.BlockSpec((b, lane_tile), lambda i, j: (i, j)),
                  pl.BlockSpec((None, ng, b, b), lambda i, j: (jnp.where(i >= seg_tiles, 1, 0) if mats.shape[0] > 1 else 0, 0, 0, 0))],
        out_specs=pl.BlockSpec((b, lane_tile), lambda i, j: (i, j)),
        out_shape=jax.ShapeDtypeStruct(x.shape, F32),
        compiler_params=_params(("parallel", "parallel")), name=name,
    )(x, mats)


def pool_means(name, u, ctx_rows):
    r, pw = u.shape
    grid_rows = (r - ctx_rows) // GRID_W
    gw = pw // len(POOL_WINDOWS)
    m1, m2 = _pool_matrices(ctx_rows, grid_rows)
    lane_tile = min(2048, GRID_W * pw)

    def apply(x, a1, a2, tag):
        y = _pool_apply(name + tag + "1", x, jnp.asarray(a1), gw, ctx_rows // ROW_TILE, pw)
        lat = y[ctx_rows:].reshape(grid_rows, GRID_W * pw)
        lat = _pool_apply(name + tag + "2", lat, jnp.asarray(a2), gw, 0, lane_tile)
        return jnp.concatenate([y[:ctx_rows], lat.reshape(r - ctx_rows, pw)], axis=0)

    @jax.custom_vjp
    def f(u):
        return apply(u, m1, m2, "_f")

    tr = lambda m: np.ascontiguousarray(np.swapaxes(m, -1, -2))
    f.defvjp(lambda u: (apply(u, m1, m2, "_f"), None), lambda _, ct: (apply(ct, tr(m1), tr(m2), "_b").astype(u.dtype),))
    return f(u)


def _ew(name, fn, *xs, n_out=1):
    shapes = jax.eval_shape(lambda *a: fn(*a), *xs)
    shapes = shapes if isinstance(shapes, (tuple, list)) else (shapes,)

    def body(*refs):
        res = fn(*[r[...] for r in refs[:len(xs)]])
        res = res if isinstance(res, (tuple, list)) else (res,)
        for r, o in zip(res, refs[len(xs):]):
            o[...] = r

    out = pl.pallas_call(body, out_shape=[jax.ShapeDtypeStruct(s.shape, s.dtype) for s in shapes],
                         compiler_params=_params(), name=name)(*xs)
    return out[0] if len(shapes) == 1 else tuple(out)


def _adamw_math(w, g, m, v):
    m2 = ADAM_B1 * m + (1.0 - ADAM_B1) * g
    v2 = ADAM_B2 * v + (1.0 - ADAM_B2) * (g * g)
    m_hat = m2 / (1.0 - ADAM_B1 ** ADAM_STEP)
    v_hat = v2 / (1.0 - ADAM_B2 ** ADAM_STEP)
    delta = -ADAM_LR * (m_hat / (jnp.sqrt(v_hat) + ADAM_EPS) + ADAM_WD * w)
    return delta, m2, v2


def adamw(name, w, g, m, v):
    *lead, r, c = w.shape
    tr = _pick(r, (256, 128, 64, 32, 16, 8))

    def body(w_ref, g_ref, m_ref, v_ref, d_ref, m2_ref, v2_ref):
        d_ref[...], m2_ref[...], v2_ref[...] = _adamw_math(w_ref[...], g_ref[...], m_ref[...], v_ref[...])

    if lead:
        grid, spec = (lead[0], r // tr), pl.BlockSpec((None, tr, c), lambda l, i: (l, i, 0))
    else:
        grid, spec = (r // tr,), pl.BlockSpec((tr, c), lambda i: (i, 0))
    return pl.pallas_call(body, grid=grid, in_specs=[spec] * 4, out_specs=[spec] * 3,
                          out_shape=[jax.ShapeDtypeStruct(w.shape, F32)] * 3,
                          compiler_params=_params(("parallel",) * len(grid)), name=name)(w, g, m, v)


def _sum_rows(name, xs, out_dtype=F32):
    r, c = xs[0].shape
    tr = _pick(r, tuple(t for t in (512, 256, 128, 64, 32, 16, 8) if t * c * 4 <= 2 << 20))

    def body(*refs):
        acc = refs[0][...].astype(F32)
        for ref in refs[1:-1]:
            acc = acc + ref[...].astype(F32)
        refs[-1][...] = acc.astype(out_dtype)

    spec = pl.BlockSpec((tr, c), lambda i: (i, 0))
    return pl.pallas_call(body, grid=(r // tr,), in_specs=[spec] * len(xs), out_specs=spec,
                          out_shape=jax.ShapeDtypeStruct((r, c), out_dtype),
                          compiler_params=_params(("parallel",)), name=name)(*xs)


def _place():
    return lax.axis_index("x"), lax.axis_index("y"), lax.axis_index("c")


def _chip_peers(x, y, c):
    return [(1 - x, y, c), (x, 1 - y, c), (1 - x, 1 - y, c)]


def all_gather8(name, block):
    m_per, n = block.shape

    def body(x_ref, out_ref, send_sems, recv_sems, local_sem):
        x, y, c = _place()
        me, sibling = (x, y, c), (x, y, 1 - c)
        chips = [(1 - x, y), (x, 1 - y), (1 - x, 1 - y)]

        def rows(px, py, pc):
            return out_ref.at[pl.ds((4 * px + 2 * py + pc) * m_per, m_per), :]

        def copy(k, blk, to, src=None):
            return pltpu.make_async_remote_copy(
                src_ref=rows(*blk) if src is None else src, dst_ref=rows(*blk),
                send_sem=send_sems.at[k], recv_sem=recv_sems.at[k], device_id=to, device_id_type=MESH_ID)

        mine = pltpu.make_async_copy(x_ref, rows(*me), local_sem)
        mine.start()
        first = [copy(0, me, sibling, src=x_ref)]
        first += [copy(1 + j, me, (*chip, c), src=x_ref) for j, chip in enumerate(chips)]
        for cp in first:
            cp.start()
        passed = [copy(4 + j, (*chip, c), sibling) for j, chip in enumerate(chips)]
        for j, chip in enumerate(chips):
            copy(1 + j, (*chip, c), me).wait_recv()
            passed[j].start()
        copy(0, sibling, me).wait_recv()
        for j, chip in enumerate(chips):
            copy(4 + j, (*chip, 1 - c), me).wait_recv()
        for cp in first + passed:
            cp.wait_send()
        mine.wait()

    return pl.pallas_call(
        body, out_shape=jax.ShapeDtypeStruct((8 * m_per, n), block.dtype),
        in_specs=[pl.BlockSpec(memory_space=pltpu.VMEM)], out_specs=pl.BlockSpec(memory_space=pltpu.VMEM),
        scratch_shapes=[pltpu.SemaphoreType.DMA((7,)), pltpu.SemaphoreType.DMA((7,)), pltpu.SemaphoreType.DMA],
        compiler_params=_params(), name=name,
    )(block)


N_PEERS = 3


def gather_chips(name, shards):
    n = len(shards)

    def body(*refs):
        xs, outs, (send_sems, recv_sems, local_sems) = refs[:n], refs[n:2 * n], refs[2 * n:]
        x, y, c = _place()
        local, remote = [], []
        for i in range(n):
            local.append(pltpu.make_async_copy(xs[i].at[c], outs[i].at[2 * x + y], local_sems.at[i]))
            local[-1].start()
            for p, peer in enumerate(_chip_peers(x, y, c)):
                remote.append(pltpu.make_async_remote_copy(
                    src_ref=xs[i].at[c], dst_ref=outs[i].at[2 * x + y], send_sem=send_sems.at[i * N_PEERS + p],
                    recv_sem=recv_sems.at[i * N_PEERS + p], device_id=peer, device_id_type=MESH_ID))
                remote[-1].start()
        for i in range(n):
            for p, (px, py, _) in enumerate(_chip_peers(x, y, c)):
                pltpu.make_async_remote_copy(
                    src_ref=xs[i].at[c], dst_ref=outs[i].at[2 * px + py], send_sem=send_sems.at[i * N_PEERS + p],
                    recv_sem=recv_sems.at[i * N_PEERS + p], device_id=(px, py, c), device_id_type=MESH_ID).wait_recv()
        for cp in remote:
            cp.wait_send()
        for cp in local:
            cp.wait()

    hbm = pl.BlockSpec(memory_space=pltpu.HBM)
    return pl.pallas_call(
        body, out_shape=[jax.ShapeDtypeStruct((N_CHIPS,) + s.shape[1:], s.dtype) for s in shards],
        in_specs=[hbm] * n, out_specs=[hbm] * n,
        scratch_shapes=[pltpu.SemaphoreType.DMA((n * N_PEERS,)), pltpu.SemaphoreType.DMA((n * N_PEERS,)),
                        pltpu.SemaphoreType.DMA((n,))],
        compiler_params=_params(), name=name,
    )(*shards)


def swap_sibling(name, blocks):
    n = len(blocks)
    pairs = [tuple(b) if isinstance(b, (tuple, list)) else (b,) for b in blocks]
    flat = [a for p in pairs for a in p]
    first = [sum(len(q) for q in pairs[:i]) for i in range(n)]

    def body(*refs):
        xs, outs, (send_sems, recv_sems) = refs[:len(flat)], refs[len(flat):len(flat) + n], refs[len(flat) + n:]
        x, y, c = _place()

        def copy(i, src):
            return pltpu.make_async_remote_copy(src_ref=src, dst_ref=outs[i], send_sem=send_sems.at[i], recv_sem=recv_sems.at[i],
                                                device_id=(x, y, 1 - c), device_id_type=MESH_ID)

        for i, p in enumerate(pairs):
            if len(p) == 1:
                copy(i, xs[first[i]]).start()
            else:
                for half in range(2):
                    pl.when(c == 1 - half)(copy(i, xs[first[i] + half]).start)
        for i in range(n):
            copy(i, xs[first[i]]).wait()

    hbm = pl.BlockSpec(memory_space=pltpu.HBM)
    return pl.pallas_call(
        body, out_shape=[jax.ShapeDtypeStruct(p[0].shape, p[0].dtype) for p in pairs],
        in_specs=[hbm] * len(flat), out_specs=[hbm] * n,
        scratch_shapes=[pltpu.SemaphoreType.DMA((n,)), pltpu.SemaphoreType.DMA((n,))],
        compiler_params=_params(), name=name,
    )(*flat)


def scatter_chips(name, pieces):
    n = len(pieces)

    def body(*refs):
        xs, outs, (send_sems, recv_sems) = refs[:n], refs[n:2 * n], refs[2 * n:]
        x, y, c = _place()
        copies = []
        for i in range(n):
            for p, (px, py, pc) in enumerate(_chip_peers(x, y, c)):
                copies.append(pltpu.make_async_remote_copy(
                    src_ref=xs[i].at[2 * px + py], dst_ref=outs[i].at[p], send_sem=send_sems.at[i * N_PEERS + p],
                    recv_sem=recv_sems.at[i * N_PEERS + p], device_id=(px, py, pc), device_id_type=MESH_ID))
                copies[-1].start()
        for cp in copies:
            cp.wait()

    hbm = pl.BlockSpec(memory_space=pltpu.HBM)
    return pl.pallas_call(
        body, out_shape=[jax.ShapeDtypeStruct((N_PEERS,) + p.shape[1:], p.dtype) for p in pieces],
        in_specs=[hbm] * n, out_specs=[hbm] * n,
        scratch_shapes=[pltpu.SemaphoreType.DMA((n * N_PEERS,)), pltpu.SemaphoreType.DMA((n * N_PEERS,))],
        compiler_params=_params(), name=name,
    )(*pieces)


def pair_sum(name, half0, half1, got, out_dtype):
    r, w = got.shape
    tr = _pick(r, tuple(t for t in (512, 256, 128, 64, 32, 16, 8) if t * w * 4 <= 2 << 20))

    def body(a0, a1, g, o):
        mine = jnp.where(lax.axis_index("c") == 0, a0[...], a1[...])
        o[...] = (mine.astype(F32) + g[...].astype(F32)).astype(out_dtype)

    spec = pl.BlockSpec((tr, w), lambda i: (i, 0))
    return pl.pallas_call(body, grid=(r // tr,), in_specs=[spec] * 3, out_specs=spec,
                          out_shape=jax.ShapeDtypeStruct((r, w), out_dtype),
                          compiler_params=_params(("parallel",)), name=name)(half0, half1, got)


def _sum_devices(name, got, fold=False):
    def body(a_ref, *o_refs):
        acc = a_ref[0]
        for i in range(1, N_DEV):
            acc = acc + a_ref[i]
        o_refs[0][...] = acc
        if fold:
            o_refs[1][...] = acc + pltpu.roll(acc, SUBLANES // 2, 0)

    shape = jax.ShapeDtypeStruct(got.shape[1:], F32)
    out = pl.pallas_call(body, out_shape=[shape] * (2 if fold else 1), compiler_params=_params(), name=name)(got)
    return out if fold else out[0]


def _w_in_bounds(d, pw, scw):
    off_z = 3 * DN_WIDTH
    off_a = off_z + DN_WIDTH
    off_pool = off_a + 2 * N_DIRHEAD
    off_sc = off_pool + pw
    off_gate = off_sc + 3 * scw
    return (0, off_z, off_a, off_pool, off_sc, off_gate, off_gate + 3 * d)


def _misc_widths(pw, scw):
    return (DN_WIDTH, pw, 3 * scw, LANES)


def split_matrices(full, pw, scw):
    depth, d = len(full["w_in"]), full["w_in"][0].shape[0]
    b = _w_in_bounds(d, pw, scw)
    out = {k: [] for k in ("qkv", "gate", "misc", "gu", "br_a", "br_b", "br_c", "o", "down")}
    for l in range(depth):
        w = full["w_in"][l]
        out["qkv"].append(w[:, b[0]:b[1]])
        out["gate"].append(w[:, b[5]:b[6]])
        out["misc"].append(jnp.concatenate([w[:, b[1]:b[2]], w[:, b[3]:b[5]], _pad_lanes(w[:, b[2]:b[3]])], axis=1))
        out["gu"].append(full["w_gu"][l])
        for k in ("br_a", "br_b", "br_c", "o", "down"):
            out[k].append(full["w_" + k][l])
    return out


def join_matrix_grads(g, pw, scw):
    depth = len(g["qkv"])
    n_z, n_rest, n_ab = DN_WIDTH, pw + 3 * scw, 2 * N_DIRHEAD
    w_in = []
    for l in range(depth):
        m = g["misc"][l]
        w_in.append(jnp.concatenate([g["qkv"][l], m[:, :n_z], m[:, n_z + n_rest:n_z + n_rest + n_ab],
                                     m[:, n_z:n_z + n_rest], g["gate"][l]], axis=1))
    out = {"w_in": w_in, "w_gu": g["gu"]}
    for k in ("br_a", "br_b", "br_c", "o", "down"):
        out["w_" + k] = g[k]
    return {k: [a.astype(MXU_DTYPE) for a in v] for k, v in out.items()}


def split_cols(x, widths):
    edges = np.cumsum((0,) + tuple(widths))

    def cut(x):
        return tuple(x[:, a:b] for a, b in zip(edges[:-1], edges[1:]))

    f = jax.custom_vjp(cut)
    f.defvjp(lambda x: (cut(x), None), lambda _, cts: (jnp.concatenate(cts, axis=1),))
    return f(x)


def _row(v):
    return v.reshape(1, -1)


def _pad_lanes(v, width=LANES):
    return jnp.pad(v, ((0, 0), (0, width - v.shape[1])))


def _block_diag(blocks):
    g, n, _ = blocks.shape
    out = jnp.zeros((g * n, g * n), blocks.dtype)
    for i in range(g):
        out = out.at[i * n:(i + 1) * n, i * n:(i + 1) * n].set(blocks[i])
    return out


def local_loss(p, carriers, x, mod_lat, mod_ctx, wts, ctx, target):
    ctx_rows, d = ctx.shape
    depth = len(wts["qkv"])
    pw, scw = p["pool_scale"].shape[1], p["sc_conv_w"].shape[2]
    ff = wts["down"][0].shape[0]
    ct = ctx_rows // ROW_TILE
    xs = jnp.concatenate([ctx, x], axis=0)
    seg = lambda l, k: jnp.stack([mod_ctx[l, k], mod_lat[l, k]]).reshape(2, 1, d)
    dn = gate2 = None
    for l in range(depth):
        sh1, sc1, g1, sh2, sc2, g2 = (seg(l, k) for k in range(6))
        tag = f"l{l}_"
        lin = lambda key, a: mm(tag + key, a, wts[key][l], carriers[key][l])
        if l == 0:
            (h1,) = rowwise(tag + "mod", _fn_modulate, [xs], [_row(p["norm1_g"][l]), sh1, sc1], seg=(False, True, True),
                            outs=[(d, MXU_DTYPE, 1)], ctx_tiles=ct)
        else:
            xs, h1 = rowwise(tag + "resmod1", _fn_resmod, [xs, dn], [gate2, _row(p["norm1_g"][l]), sh1, sc1],
                             seg=(True, False, True, True), outs=[(d, F32, 1), (d, MXU_DTYPE, 1)], ctx_tiles=ct)
        fan = ("qkv", "gate", "misc")
        p_qkv, p_gate, p_misc = mm_fanout(tag + "in", h1, [wts[k][l] for k in fan], [carriers[k][l] for k in fan])
        p_z, p_pool, p_sc, p_ab = split_cols(p_misc, _misc_widths(pw, scw))
        cw = p["dn_conv_w"][l]
        q, k, v, gb = rowwise(
            tag + "dnprep", _fn_dnprep, [p_qkv, p_ab],
            [cw[0:1], cw[1:2], cw[2:3], _pad_lanes(p["dn_a_log"][l].reshape(1, -1)), _pad_lanes(p["dn_dt_bias"][l].reshape(1, -1))],
            halo=(True, False), parts_r=(3 * DN_HEADS, 1), parts_v=(3 * DN_HEADS,) * 3 + (1, 1),
            outs=[(DN_WIDTH, F32, DN_HEADS)] * 3 + [(LANES, F32, 1)], ctx_tiles=ct)
        o_f, o_b = deltanet(tag + "dn", q, k, v, gb, ctx_rows)
        (oz,) = rowwise(tag + "dnpost", _fn_dnpost, [o_f, o_b, p_z], [_row(p["dn_norm_g"][l])], parts_r=(DN_HEADS,) * 3,
                        outs=[(DN_WIDTH, MXU_DTYPE, DN_HEADS)], ctx_tiles=ct)
        y_a = lin("br_a", oz)
        means = pool_means(tag + "box", p_pool, ctx_rows)
        (dpool,) = rowwise(tag + "poolsub", _fn_sub, [means, p_pool], [], outs=[(pw, MXU_DTYPE, 1)], ctx_tiles=ct)
        pool_mat = _block_diag(p["pool_w"][l])
        mixed = mm(tag + "poolw", dpool, pool_mat.astype(MXU_DTYPE), pool_mat)
        (yb_in,) = rowwise(tag + "poolscale", _fn_scale, [mixed], [_row(p["pool_scale"][l])], outs=[(pw, MXU_DTYPE, 1)], ctx_tiles=ct)
        y_b = lin("br_b", yb_in)
        sw = p["sc_conv_w"][l]
        (yc_in,) = rowwise(tag + "sconv", _fn_shortconv, [p_sc], [sw[0:1], sw[1:2], sw[2:3]], halo=(True,), parts_r=(3,),
                           outs=[(scw, MXU_DTYPE, 1)], ctx_tiles=ct)
        y_c = lin("br_c", yc_in)
        (y,) = rowwise(tag + "merge", _fn_merge, [p_gate, y_a, y_b, y_c], [], parts_r=(3, 1, 1, 1),
                       outs=[(d, MXU_DTYPE, 1)], ctx_tiles=ct)
        mix = lin("o", y)
        xs, h2 = rowwise(tag + "resmod2", _fn_resmod, [xs, mix], [g1, _row(p["norm2_g"][l]), sh2, sc2],
                         seg=(True, False, True, True), outs=[(d, F32, 1), (d, MXU_DTYPE, 1)], ctx_tiles=ct)
        (act,) = rowwise(tag + "swiglu", _fn_swiglu, [lin("gu", h2)], [], parts_r=(2,),
                         outs=[(ff, MXU_DTYPE, 1)], ctx_tiles=ct)
        dn = lin("down", act)
        gate2 = g2
    counts = jnp.concatenate([jnp.zeros((1, 1, LANES), F32), jnp.ones((1, 1, LANES), F32)])
    (total,) = rowwise("final", _fn_final, [xs, dn, target], [gate2, _row(p["final_norm_g"]), counts],
                       seg=(True, False, True), reds=[LANES], ctx_tiles=ct, skip=(0, 0, ct))
    return total[0, 0]


BIG = ("w_in", "w_br_a", "w_br_b", "w_br_c", "w_o", "w_gu", "w_down")
ROW_SHARDED = ("w_o", "w_down")
SMALL_REPL = ("norm1_g", "norm2_g", "dn_a_log", "dn_dt_bias", "dn_norm_g", "pool_w", "pool_scale", "final_norm_g")
SMALL_SHARD = ("dn_conv_w", "sc_conv_w")
WEIGHTS = ("c_ctx", "w_ada", "b_ada", "norm1_g", "norm2_g", "w_in", "dn_conv_w", "dn_a_log", "dn_dt_bias", "dn_norm_g", "pool_w",
           "pool_scale", "sc_conv_w", "w_br_a", "w_br_b", "w_br_c", "w_o", "w_gu", "w_down", "final_norm_g")
N_CHIPS, N_DEV = 4, 8
COMM_COLS = 1024


def _pack(arrays, rows_multiple, cols=COMM_COLS, dtype=F32):
    size = sum(int(np.prod(a.shape)) for a in arrays)
    rows = -(-size // (cols * rows_multiple)) * rows_multiple
    tail = [jnp.zeros((rows * cols - size,), dtype)] if rows * cols > size else []
    return jnp.concatenate([a.astype(dtype).reshape(-1) for a in arrays] + tail).reshape(rows, cols)


def _unpack(flat, shapes):
    out, pos = [], 0
    for s in shapes:
        n = int(np.prod(s))
        out.append(flat[pos:pos + n].reshape(s))
        pos += n
    return out


def _join_shards(parts, name):
    _, a, b = parts.shape
    if name in ROW_SHARDED:
        return parts.reshape(N_CHIPS * a, b)
    return jnp.moveaxis(parts, 0, 1).reshape(a, N_CHIPS * b)


def _cut_shards(full, name):
    k, n = full.shape
    if name in ROW_SHARDED:
        return full.reshape(N_CHIPS, k // N_CHIPS, n)
    return jnp.moveaxis(full.reshape(k, N_CHIPS, n // N_CHIPS), 1, 0)


def _dsilu(x):
    s = _sigmoid(x)
    return s + x * s * (1.0 - s)


def kernel(x, c, ctx, c_ctx, w_ada, b_ada, norm1_g, norm2_g, w_in, dn_conv_w, dn_a_log, dn_dt_bias, dn_norm_g, pool_w, pool_scale, sc_conv_w, w_br_a, w_br_b, w_br_c, w_o, w_gu, w_down, final_norm_g, loss_target, m_c_ctx, m_w_ada, m_b_ada, m_norm1_g, m_norm2_g, m_w_in, m_dn_conv_w, m_dn_a_log, m_dn_dt_bias, m_dn_norm_g, m_pool_w, m_pool_scale, m_sc_conv_w, m_w_br_a, m_w_br_b, m_w_br_c, m_w_o, m_w_gu, m_w_down, m_final_norm_g, v_c_ctx, v_w_ada, v_b_ada, v_norm1_g, v_norm2_g, v_w_in, v_dn_conv_w, v_dn_a_log, v_dn_dt_bias, v_dn_norm_g, v_pool_w, v_pool_scale, v_sc_conv_w, v_w_br_a, v_w_br_b, v_w_br_c, v_w_o, v_w_gu, v_w_down, v_final_norm_g):
    given = dict(locals())
    ix, iy, ic = _place()
    chip, dev = 2 * ix + iy, 4 * ix + 2 * iy + ic
    d = x.shape[-1]
    depth = w_in.shape[0]
    ada_cols = w_ada.shape[2]
    spare = 2 * SUBLANES - N_DEV - 1

    got0 = all_gather8("gather_cond", _pack([c, dn_conv_w, sc_conv_w], SUBLANES)).reshape(N_DEV, -1)
    c_all = got0[:, :d]
    taps = [_unpack(got0[2 * j, d:], [dn_conv_w.shape, sc_conv_w.shape]) for j in range(N_CHIPS)]
    full = {"dn_conv_w": jnp.concatenate([t[0] for t in taps], axis=-1),
            "sc_conv_w": jnp.concatenate([t[1] for t in taps], axis=-1)}
    assert depth == 2, "the two cores of a chip split the layers between them"
    got_layer = gather_chips("gather_weights", [given[n].astype(MXU_DTYPE) for n in BIG])
    other_layer = swap_sibling("swap_weights", got_layer)
    mats = {n: [_join_shards(jnp.where(ic == l, got_layer[i], other_layer[i]), n) for l in range(depth)]
            for i, n in enumerate(BIG)}
    for n in SMALL_REPL:
        full[n] = given[n]
    pw, scw = pool_scale.shape[1], full["sc_conv_w"].shape[2]
    wts = split_matrices(mats, pw, scw)
    carriers = {k: [jnp.zeros(w.shape, F32) for w in ws] for k, ws in wts.items()}

    cond = jnp.concatenate([c_all, c_ctx[None], jnp.zeros((spare, d), F32)])
    s_cond = _ew("silu_cond", _silu, cond)
    mod_cols = jnp.concatenate([_mm(f"ada{l}_f", s_cond, w_ada[l], NN) for l in range(depth)], axis=1)
    mod_got = all_gather8("gather_mod", mod_cols).reshape(N_DEV, 2 * SUBLANES, depth, ada_cols)
    mod_all = jnp.concatenate([mod_got[2 * j] for j in range(N_CHIPS)], axis=-1) + b_ada[None]
    mod_lat = lax.dynamic_index_in_dim(mod_all, dev, 0, keepdims=False).reshape(depth, 6, d)
    mod_ctx = mod_all[N_DEV].reshape(depth, 6, d)

    loss_local, (g_full, g_mats, grad_x, g_mod_lat, g_mod_ctx) = jax.value_and_grad(local_loss, argnums=(0, 1, 2, 3, 4))(
        full, carriers, x[0], mod_lat, mod_ctx, wts, ctx[0], loss_target[0])
    g_mats_full = join_matrix_grads(g_mats, pw, scw)
    loss = lax.psum(loss_local, ("x", "y", "c"))

    dmod_cols = (2 * depth * 6 * d) // SUBLANES
    dmod = all_gather8("gather_dmod", _pack([g_mod_lat, g_mod_ctx], SUBLANES, cols=dmod_cols))
    dmod = dmod.reshape(N_DEV, SUBLANES, dmod_cols)
    dmod_sum, dmod_fold = _sum_devices("reduce_dmod", dmod, fold=True)
    half_rows = SUBLANES // 2
    grad_b_ada = dmod_fold[:half_rows].reshape(depth, 6 * d)
    dctx_sum = dmod_sum[half_rows:].reshape(1, depth, 6 * d)
    d9 = jnp.concatenate([dmod[:, :half_rows].reshape(N_DEV, depth, 6 * d), dctx_sum, jnp.zeros((spare, depth, 6 * d), F32)])
    d9 = lax.dynamic_slice_in_dim(d9, chip * ada_cols, ada_cols, axis=2)
    grad_w_ada = jnp.stack([_mm(f"ada{l}_dw", s_cond, d9[:, l], TN) for l in range(depth)])
    ds_cond = [_mm(f"ada{l}_da", d9[:, l], w_ada[l], NT) for l in range(depth)]
    dsilu_part = _sum_rows("sum_dcond", ds_cond)[N_DEV]

    small_names = SMALL_REPL + SMALL_SHARD
    small_grads = [dsilu_part] + [g_full[n] for n in small_names]
    small_shapes = [a.shape for a in small_grads]
    n_small = sum(int(np.prod(s)) for s in small_shapes)
    cols = -(-n_small // (SUBLANES * LANES)) * LANES
    got2 = all_gather8("gather_small", _pack(small_grads, SUBLANES, cols=cols)).reshape(N_DEV, SUBLANES, cols)
    small_sum = _unpack(_sum_devices("reduce_small", got2).reshape(-1), small_shapes)
    grads = dict(zip(small_names, small_sum[1:]))
    grads["c_ctx"] = _ew("dsilu", lambda g, z: 0.5 * g * _dsilu(z), _row(small_sum[0]), _row(c_ctx)).reshape(-1)
    grads["w_ada"], grads["b_ada"] = grad_w_ada, grad_b_ada
    for n in SMALL_SHARD:
        width = given[n].shape[-1]
        grads[n] = lax.dynamic_slice_in_dim(grads[n], chip * width, width, axis=-1)

    from_sibling = swap_sibling("swap_layers", [tuple(g_mats_full[n]) for n in BIG])
    pieces = [_cut_shards(pair_sum("sum_pair_" + n, *g_mats_full[n], from_sibling[i], MXU_DTYPE), n) for i, n in enumerate(BIG)]
    from_chips = scatter_chips("scatter_pieces", pieces)
    reduced = []
    for i, n in enumerate(BIG):
        own = lax.dynamic_index_in_dim(pieces[i], chip, 0, keepdims=False)
        a, b = own.shape
        view = lambda t: t.reshape(-1, b)
        reduced.append(_sum_rows("sum_chips_" + n, [view(own)] + [view(from_chips[i][p]) for p in range(N_PEERS)]).reshape(a, b))
    other_reduced = swap_sibling("swap_reduced", reduced)
    for i, n in enumerate(BIG):
        grads[n] = jnp.stack([jnp.where(ic == l, reduced[i], other_reduced[i]) for l in range(depth)])

    delta, new_m, new_v = {}, {}, {}
    large = BIG + ("w_ada",)
    for n in large:
        delta[n], new_m[n], new_v[n] = adamw("adamw_" + n, given[n], grads[n], given["m_" + n], given["v_" + n])
    rest = [n for n in WEIGHTS if n not in large]
    rest_shapes = [given[n].shape for n in rest]
    packed = [_pack([src[pre + n] for n in rest], SUBLANES, cols=LANES)
              for src, pre in ((given, ""), (grads, ""), (given, "m_"), (given, "v_"))]
    for res, o in zip((delta, new_m, new_v), adamw("adamw_small", *packed)):
        for n, a in zip(rest, _unpack(o.reshape(-1), rest_shapes)):
            res[n] = a
    return (loss, grad_x[None], *[grads[n] for n in WEIGHTS], *[delta[n] for n in WEIGHTS],
            *[new_m[n] for n in WEIGHTS], *[new_v[n] for n in WEIGHTS])
```

```python
import functools
import math

import numpy as np
import jax
import jax.numpy as jnp
from jax import lax
from jax.experimental import pallas as pl
from jax.experimental.pallas import tpu as pltpu

F32 = jnp.float32
MXU_DTYPE = jnp.bfloat16
HIGHEST = lax.Precision.HIGHEST

DN_HEADS = 4
HEAD_DIM = 128
DN_WIDTH = DN_HEADS * HEAD_DIM
DN_CHUNK = 64
GRID_W = 64
EPS = 1e-6
POOL_WINDOWS = (2, 4, 8, 16)
N_DIRHEAD = 2 * DN_HEADS
ADAM_LR, ADAM_B1, ADAM_B2, ADAM_EPS, ADAM_WD, ADAM_STEP = 0.001, 0.9, 0.999, 1e-08, 0.01, 10

LANES = 128
SUBLANES = 8
ROW_TILE = 256
VMEM_LIMIT = 56 * 1024 * 1024

MESH_ID = pl.DeviceIdType.MESH
NN, NT, TN = ((1,), (0,)), ((1,), (1,)), ((0,), (0,))


def _params(sem=None):
    return pltpu.CompilerParams(dimension_semantics=sem, vmem_limit_bytes=VMEM_LIMIT)


def _pick(n, cands):
    for c in cands:
        if c <= n and n % c == 0:
            return c
    return n


def _mm(name, a, b, dims, out_dtype=F32):
    if dims == NN:
        (m, kk), (_, n) = a.shape, b.shape
    elif dims == NT:
        (m, kk), (n, _) = a.shape, b.shape
    else:
        (kk, m), (_, n) = a.shape, b.shape
    tm = _pick(m, (768, 1024, 1408, 512, 256, 128, 64, 32, 16, 8))
    tn = _pick(n, ((3072, 2816) if dims == NN else ()) + (1536, 1024, 1408, 1664, 768, 896, 512, 256, 128))
    tk = kk if dims == NN and kk <= 2816 else _pick(kk, (1024, 1408, 768, 512, 256, 128))
    gi, gj, gl = m // tm, n // tn, kk // tk
    if dims == NN:
        a_spec = pl.BlockSpec((tm, tk), lambda i, j, l: (i, l))
        b_spec = pl.BlockSpec((tk, tn), lambda i, j, l: (l, j))
    elif dims == NT:
        a_spec = pl.BlockSpec((tm, tk), lambda i, j, l: (i, l))
        b_spec = pl.BlockSpec((tn, tk), lambda i, j, l: (j, l))
    else:
        a_spec = pl.BlockSpec((tk, tm), lambda i, j, l: (l, i))
        b_spec = pl.BlockSpec((tk, tn), lambda i, j, l: (l, j))
    direct = gl == 1
    use_acc = (not direct) and out_dtype != F32

    def body(a_ref, b_ref, o_ref, *scratch):
        part = lax.dot_general(a_ref[...].astype(MXU_DTYPE), b_ref[...].astype(MXU_DTYPE), (dims, ((), ())),
                               preferred_element_type=F32)
        if direct:
            o_ref[...] = part.astype(out_dtype)
            return
        acc = scratch[0] if use_acc else o_ref
        l = pl.program_id(2)

        @pl.when(l == 0)
        def _():
            acc[...] = part

        @pl.when(l > 0)
        def _():
            acc[...] += part

        if use_acc:
            @pl.when(l == gl - 1)
            def _():
                o_ref[...] = acc[...].astype(out_dtype)

    return pl.pallas_call(
        body, grid=(gi, gj, gl), in_specs=[a_spec, b_spec],
        out_specs=pl.BlockSpec((tm, tn), lambda i, j, l: (i, j)),
        out_shape=jax.ShapeDtypeStruct((m, n), out_dtype),
        scratch_shapes=[pltpu.VMEM((tm, tn), F32)] if use_acc else [],
        compiler_params=_params(("parallel", "parallel", "arbitrary")), name=name,
    )(a, b)


def mm(name, a, w, carrier):
    @jax.custom_vjp
    def f(a, w, carrier):
        return _mm(name + "_f", a, w, NN, out_dtype=MXU_DTYPE)

    def fwd(a, w, carrier):
        return f(a, w, carrier), (a, w)

    def bwd(res, dc):
        a, w = res
        da = _mm(name + "_da", dc, w, NT, out_dtype=a.dtype)
        dw = _mm(name + "_dw", a, dc, TN)
        return da, None, dw

    f.defvjp(fwd, bwd)
    return f(a, w, carrier)


def _mm_nt_sum(name, dcs, ws, out_dtype):
    m, kk = dcs[0].shape[0], ws[0].shape[0]
    tm = _pick(m, (768, 1024, 512, 256, 128, 64, 32, 16, 8))
    tns = [_pick(w.shape[1], (1664, 1536, 1024, 768, 896, 512, 256, 128)) for w in ws]
    counts = [w.shape[1] // tn for w, tn in zip(ws, tns)]
    starts = [sum(counts[:g]) for g in range(len(ws))]
    steps = sum(counts)

    def col(g):
        return lambda i, t: jnp.clip(t - starts[g], 0, counts[g] - 1)

    def body(*refs):
        dc_refs, w_refs, o_ref, acc = refs[:len(ws)], refs[len(ws):2 * len(ws)], refs[-2], refs[-1]
        t = pl.program_id(1)

        @pl.when(t == 0)
        def _():
            acc[...] = jnp.zeros_like(acc)

        for g in range(len(ws)):
            @pl.when(jnp.logical_and(t >= starts[g], t < starts[g] + counts[g]))
            def _():
                acc[...] += lax.dot_general(dc_refs[g][...].astype(MXU_DTYPE), w_refs[g][...].astype(MXU_DTYPE),
                                            (NT, ((), ())), preferred_element_type=F32)

        @pl.when(t == steps - 1)
        def _():
            o_ref[...] = acc[...].astype(out_dtype)

    dc_specs = [pl.BlockSpec((tm, tns[g]), (lambda c: lambda i, t: (i, c(i, t)))(col(g))) for g in range(len(ws))]
    w_specs = [pl.BlockSpec((kk, tns[g]), (lambda c: lambda i, t: (0, c(i, t)))(col(g))) for g in range(len(ws))]
    return pl.pallas_call(
        body, grid=(m // tm, steps), in_specs=dc_specs + w_specs,
        out_specs=pl.BlockSpec((tm, kk), lambda i, t: (i, 0)),
        out_shape=jax.ShapeDtypeStruct((m, kk), out_dtype),
        scratch_shapes=[pltpu.VMEM((tm, kk), F32)],
        compiler_params=_params(("parallel", "arbitrary")), name=name,
    )(*dcs, *ws)


def mm_fanout(name, a, ws, carriers):
    n = len(ws)

    @jax.custom_vjp
    def f(a, ws, carriers):
        return tuple(_mm(f"{name}{g}_f", a, ws[g], NN, out_dtype=MXU_DTYPE) for g in range(n))

    def fwd(a, ws, carriers):
        return f(a, ws, carriers), (a, ws)

    def bwd(res, dcs):
        a, ws = res
        da = _mm_nt_sum(name + "_da", list(dcs), list(ws), a.dtype)
        dws = tuple(_mm(f"{name}{g}_dw", a, dcs[g], TN) for g in range(n))
        return da, None, dws

    f.defvjp(fwd, bwd)
    return f(a, tuple(ws), tuple(carriers))


def _split(x, parts):
    w = x.shape[-1] // parts
    return [x[:, k * w:(k + 1) * w] for k in range(parts)]


def _cat(xs):
    return xs[0] if len(xs) == 1 else jnp.concatenate(xs, axis=-1)


def _shift_rows(x, prev_ref, next_ref, i, ctx_tiles, nt):
    tr = x.shape[0]
    rid = lax.broadcasted_iota(jnp.int32, x.shape, 0)
    first = jnp.logical_or(i == 0, i == ctx_tiles)
    last = jnp.logical_or(i == ctx_tiles - 1, i == nt - 1)
    prow = jnp.where(first, 0.0, prev_ref[SUBLANES - 1:SUBLANES, :].astype(F32))
    nrow = jnp.where(last, 0.0, next_ref[0:1, :].astype(F32))
    xm = jnp.where(rid == 0, prow, pltpu.roll(x, 1, 0))
    xp = jnp.where(rid == tr - 1, nrow, pltpu.roll(x, tr - 1, 0))
    return xm, xp


def rowwise(name, fn, rows, vecs, *, halo=(), seg=(), parts_r=None, parts_v=None, outs=(), reds=(), ctx_tiles=1, skip=None):
    nr, nv = len(rows), len(vecs)
    halo = tuple(halo) or (False,) * nr
    seg = tuple(seg) or (False,) * nv
    skip = tuple(skip or (0,) * nr)
    parts_r = tuple(parts_r or (1,) * nr)
    parts_v = tuple(parts_v or (1,) * nv)
    r_total = rows[0].shape[0]
    tr = ROW_TILE
    nt = r_total // tr
    assert r_total % tr == 0 and (ctx_tiles > 0 or not any(seg)) and not any(h and s for h, s in zip(halo, skip))

    def tile_map(i):
        return lambda t: (jnp.maximum(t - skip[i], 0), 0)

    def row_specs():
        sp = []
        for i, r in enumerate(rows):
            w = r.shape[1]
            sp.append(pl.BlockSpec((tr, w), tile_map(i)))
            if halo[i]:
                k = tr // SUBLANES
                sp.append(pl.BlockSpec((SUBLANES, w), lambda t: (jnp.maximum(t * k - 1, 0), 0)))
                sp.append(pl.BlockSpec((SUBLANES, w), lambda t: (jnp.minimum((t + 1) * k, nt * k - 1), 0)))
        return sp

    def vec_spec(j):
        w = vecs[j].shape[-1]
        if seg[j]:
            return pl.BlockSpec((None, 1, w), lambda t: (jnp.where(t >= ctx_tiles, 1, 0), 0, 0))
        return pl.BlockSpec((1, w), lambda t: (0, 0))

    def row_args(rv):
        a = []
        for i in range(nr):
            a += [rv[i]] * 3 if halo[i] else [rv[i]]
        return a

    def load(refs, t):
        pos, rp = 0, []
        for i in range(nr):
            x = refs[pos][...].astype(F32)
            if halo[i]:
                xm, xp = _shift_rows(x, refs[pos + 1], refs[pos + 2], t, ctx_tiles, nt)
                rp.append(list(zip(_split(x, parts_r[i]), _split(xm, parts_r[i]), _split(xp, parts_r[i]))))
                pos += 3
            else:
                rp.append(_split(x, parts_r[i]))
                pos += 1
        vp = []
        for j in range(nv):
            vp.append(_split(refs[pos][...].astype(F32), parts_v[j]))
            pos += 1
        return rp, vp, refs[pos:]

    n_out, n_red = len(outs), len(reds)

    def fwd_call(*rv):
        def body(*refs):
            t = pl.program_id(0)
            rp, vp, rest = load(refs, t)
            o_parts, r_parts = fn(rp, vp)
            for k in range(n_out):
                rest[k][...] = _cat(o_parts[k]).astype(outs[k][1])
            for k in range(n_red):
                ref = rest[n_out + k]

                @pl.when(t == 0)
                def _():
                    ref[...] = r_parts[k]

                @pl.when(t > 0)
                def _():
                    ref[...] += r_parts[k]

        res = pl.pallas_call(
            body, grid=(nt,),
            in_specs=row_specs() + [vec_spec(j) for j in range(nv)],
            out_specs=[pl.BlockSpec((tr, o[0]), lambda t: (t, 0)) for o in outs]
            + [pl.BlockSpec((1, w), lambda t: (0, 0)) for w in reds],
            out_shape=[jax.ShapeDtypeStruct((r_total, o[0]), o[1]) for o in outs]
            + [jax.ShapeDtypeStruct((1, w), F32) for w in reds],
            compiler_params=_params(("arbitrary",)), name=name + "_f",
        )(*row_args(rv), *rv[nr:])
        return tuple(res)

    def bwd_call(rv, cts):
        def body(*refs):
            t = pl.program_id(0)
            rp, vp, rest = load(refs, t)
            ct_o = [_split(rest[k][...].astype(F32), outs[k][2]) for k in range(n_out)]
            ct_r = [rest[n_out + k][...] for k in range(n_red)]
            rest = rest[n_out + n_red:]
            _, vjp = jax.vjp(fn, rp, vp)
            d_rp, d_vp = vjp((ct_o, ct_r))
            pos = 0
            for i in range(nr):
                if halo[i]:
                    for c in range(3):
                        rest[pos + c][...] = _cat([p[c] for p in d_rp[i]])
                    pos += 3
                else:
                    rest[pos][...] = _cat(d_rp[i]).astype(rows[i].dtype)
                    pos += 1
            for j in range(nv):
                ref, val = rest[pos + j], _cat(d_vp[j])
                start = jnp.logical_or(t == 0, t == ctx_tiles) if seg[j] else t == 0

                @pl.when(start)
                def _():
                    ref[...] = val

                @pl.when(jnp.logical_not(start))
                def _():
                    ref[...] += val

        d_row_specs, d_row_shapes = [], []
        for i, r in enumerate(rows):
            w = r.shape[1]
            for _ in range(3 if halo[i] else 1):
                d_row_specs.append(pl.BlockSpec((tr, w), tile_map(i)))
                d_row_shapes.append(jax.ShapeDtypeStruct(r.shape, F32 if halo[i] else r.dtype))
        res = pl.pallas_call(
            body, grid=(nt,),
            in_specs=row_specs() + [vec_spec(j) for j in range(nv)]
            + [pl.BlockSpec((tr, o[0]), lambda t: (t, 0)) for o in outs]
            + [pl.BlockSpec((1, w), lambda t: (0, 0)) for w in reds],
            out_specs=d_row_specs + [vec_spec(j) for j in range(nv)],
            out_shape=d_row_shapes + [jax.ShapeDtypeStruct(v.shape, F32) for v in vecs],
            compiler_params=_params(("arbitrary",)), name=name + "_b",
        )(*row_args(rv), *rv[nr:], *cts)
        d_rows, pos = [], 0
        for i in range(nr):
            if halo[i]:
                d_rows.append(_unshift(res[pos], res[pos + 1], res[pos + 2], ctx_tiles * tr).astype(rows[i].dtype))
                pos += 3
            else:
                d_rows.append(res[pos])
                pos += 1
        return tuple(d_rows) + tuple(res[pos:])

    @jax.custom_vjp
    def f(*rv):
        return fwd_call(*rv)

    f.defvjp(lambda *rv: (fwd_call(*rv), rv), lambda rv, cts: bwd_call(rv, cts))
    return f(*rows, *vecs)


def _unshift(d, dm, dp, ctx_rows):
    r = d.shape[0]
    t = lax.broadcasted_iota(jnp.int32, (r, 1), 0)
    zero = jnp.zeros((1, d.shape[1]), d.dtype)
    from_m = jnp.concatenate([dm[1:], zero], axis=0)
    from_p = jnp.concatenate([zero, dp[:-1]], axis=0)
    from_m = jnp.where(t == ctx_rows - 1, 0.0, from_m)
    from_p = jnp.where(t == ctx_rows, 0.0, from_p)
    return d + from_m + from_p


def _sigmoid(x):
    return 0.5 * (jnp.tanh(0.5 * x) + 1.0)


def _silu(x):
    return x * _sigmoid(x)


def _softplus(x):
    return jnp.maximum(x, 0.0) + jnp.log(1.0 + jnp.exp(-jnp.abs(x)))


def _rms(x, g):
    return x * lax.rsqrt(jnp.mean(x * x, axis=-1, keepdims=True) + EPS) * g


def _fn_modulate(r, v):
    (x,), (g,), (sh,), (sc,) = r[0], v[0], v[1], v[2]
    return [[_rms(x, g) * (1.0 + sc) + sh]], []


def _fn_resmod(r, v):
    (x,), (y,) = r
    (gate,), (g,), (sh,), (sc,) = v
    xn = x + gate * y
    return [[xn], [_rms(xn, g) * (1.0 + sc) + sh]], []


def _fn_final(r, v):
    (x,), (y,), (tgt,) = r
    (gate,), (g,), (counts,) = v
    err = _rms(x + gate * y, g) - tgt
    row_loss = jnp.mean(err * err, axis=-1, keepdims=True)
    total = 0.5 * jnp.sum(row_loss, axis=0, keepdims=True)
    return [], [total * counts]


def _fn_conv(r, v):
    (x, xm, xp), = r[0]
    (w0,), (w1,), (w2,) = v
    return [[xm * w0 + x * w1 + xp * w2]], []


def _fn_conv_taps(r, v):
    (dy,), ((x, xm, xp),) = r
    col = lambda a: jnp.sum(dy * a, axis=0, keepdims=True)
    return [], [col(xm), col(x), col(xp)]


def conv3(name, x, w0, w1, w2, ctx_tiles, out_dtype):
    width = x.shape[1]

    def conv(tag, x, taps, dtype):
        return rowwise(name + tag, _fn_conv, [x], list(taps), halo=(True,), outs=[(width, dtype, 1)], ctx_tiles=ctx_tiles)[0]

    @jax.custom_vjp
    def f(x, w0, w1, w2):
        return conv("_y", x, (w0, w1, w2), out_dtype)

    def bwd(res, dy):
        x, w0, w1, w2 = res
        taps = rowwise(name + "_dw", _fn_conv_taps, [dy, x], [], halo=(False, True), reds=[width] * 3, ctx_tiles=ctx_tiles)
        return (conv("_dx", dy, (w2, w1, w0), x.dtype),) + tuple(taps)

    f.defvjp(lambda x, w0, w1, w2: (conv("_y", x, (w0, w1, w2), out_dtype), (x, w0, w1, w2)), bwd)
    return f(x, w0, w1, w2)


def _fn_dnprep(r, v):
    qkv, (ab,) = r
    (alog,), (dtb,) = v
    out = [[], [], []]
    for n, x in enumerate(qkv):
        which = n // DN_HEADS
        y = _silu(x)
        if which < 2:
            y = y * lax.rsqrt(jnp.sum(y * y, axis=-1, keepdims=True) + EPS)
        if which == 0:
            y = y * (HEAD_DIM ** -0.5)
        out[which].append(y)
    lane = lax.broadcasted_iota(jnp.int32, ab.shape, 1)
    g = -jnp.exp(alog) * _softplus(ab + dtb)
    gb = jnp.where(lane < N_DIRHEAD, g, jnp.where(lane < 2 * N_DIRHEAD, _sigmoid(ab), 0.0))
    return out + [[gb]], []


def _fn_dnpost(r, v):
    of, ob, z = r
    (g,) = v[0]
    return [[_rms(a + b, g) * _silu(c) for a, b, c in zip(of, ob, z)]], []


def _fn_sub(r, v):
    return [[r[0][0] - r[1][0]]], []


def _fn_scale(r, v):
    return [[r[0][0] * v[0][0]]], []


def _fn_shortconv(r, v):
    (xin, gb, gc), = r
    (w0,), (w1,), (w2,) = v
    u, um, up = (gc[k] * xin[k] for k in range(3))
    return [[gb[0] * (um * w0 + u * w1 + up * w2)]], []


def _fn_merge(r, v):
    gates, (ya,), (yb,), (yc,) = r
    return [[_sigmoid(gates[0]) * ya + _sigmoid(gates[1]) * yb + _sigmoid(gates[2]) * yc]], []


def _fn_swiglu(r, v):
    gate, up = r[0]
    return [[_silu(gate) * up]], []


def _dot3(a, b):
    (ah, al), (bh, bl) = _hi_lo(a), _hi_lo(b)
    dot = lambda x, y: lax.dot_general(x, y, (NN, ((), ())), preferred_element_type=F32)
    return dot(ah, bh) + (dot(ah, bl) + dot(al, bh))


def _bdot(a, b, dims):
    (ca,), (cb,) = dims
    return lax.dot_general(a.astype(MXU_DTYPE), b.astype(MXU_DTYPE), (((ca + 1,), (cb + 1,)), ((0,), (0,))),
                           preferred_element_type=F32)


def _hi_lo(a):
    hi = a.astype(MXU_DTYPE)
    return hi, (a - hi.astype(F32)).astype(MXU_DTYPE)


def _bdot3_raw(a, b, dims):
    (ah, al), (bh, bl) = _hi_lo(a), _hi_lo(b)
    return _bdot(ah, bh, dims) + (_bdot(ah, bl, dims) + _bdot(al, bh, dims))


@jax.custom_vjp
def _bdot3(a, b):
    return _bdot3_raw(a, b, NN)


_bdot3.defvjp(lambda a, b: (_bdot3_raw(a, b, NN), (a, b)),
              lambda res, ct: (_bdot3_raw(ct, res[1], NT), _bdot3_raw(res[0], ct, TN)))


def _inv_doubling(a):
    c = a.shape[-1]
    ii, jj = (lax.broadcasted_iota(jnp.int32, a.shape, d) for d in (1, 2))
    t = jnp.where(ii == jj, 1.0, 0.0) - a
    p = _bdot3_raw(a, a, NN)
    for _ in range(int(math.log2(c)) - 2):
        both = _bdot3_raw(jnp.concatenate([p, t], axis=1), p, NN)
        t = t + both[:, c:]
        p = both[:, :c]
    return t + _bdot3_raw(t, p, NN)


def _dn_gates(k4, gb):
    nb, c = N_DIRHEAD, k4.shape[1]
    k = jnp.concatenate([k4, k4], axis=0)
    lane = lax.broadcasted_iota(jnp.int32, gb.shape, 1)
    col = lambda j: jnp.sum(jnp.where(lane == j, gb, 0.0), axis=1, keepdims=True)
    g_col = jnp.concatenate([col(j)[None] for j in range(nb)], axis=0)
    b_col = jnp.concatenate([col(nb + j)[None] for j in range(nb)], axis=0)
    bi, ii, jj = (lax.broadcasted_iota(jnp.int32, (nb, c, c), a) for a in range(3))
    ahead = jnp.where(bi >= DN_HEADS, jj - ii, ii - jj)
    incl = ahead >= 0
    g_row = jnp.sum(jnp.where(ahead == 0, g_col, 0.0), axis=1, keepdims=True)
    gc_col = jnp.sum(jnp.where(incl, g_row, 0.0), axis=2, keepdims=True)
    gc_row = jnp.sum(jnp.where(ahead <= 0, g_col, 0.0), axis=1, keepdims=True)
    decay = jnp.where(incl, jnp.exp(jnp.where(incl, gc_col - gc_row, 0.0)), 0.0)
    return k, g_col, b_col, gc_col, decay, ahead > 0


def _dn_a(k4, gb):
    k, _, b_col, _, decay, strict = _dn_gates(k4, gb)
    return _bdot(k * b_col, k, NT) * jnp.where(strict, decay, 0.0)


def _dn_operands(q4, k4, v4, gb, t):
    k, g_col, b_col, gc_col, decay, _ = _dn_gates(k4, gb)
    q, v = (jnp.concatenate([a, a], axis=0) for a in (q4, v4))
    e_gc = jnp.exp(gc_col)
    d = k.shape[-1]
    uw = _bdot3(t, jnp.concatenate([v * b_col, k * b_col * e_gc], axis=2))
    u, w = uw[:, :, :d], uw[:, :, d:]
    g_last = jnp.sum(g_col, axis=1, keepdims=True)
    k_state = k * jnp.exp(g_last - gc_col)
    a_qk = _bdot(q, k, NT) * decay
    return u, w, q * e_gc, k_state, a_qk, jnp.broadcast_to(jnp.exp(g_last), (N_DIRHEAD, 1, LANES))


def _dn_step(s, u, w, qd, ks, aqk, gl):
    c = u.shape[1]
    on_state = _bdot(jnp.concatenate([w, qd], axis=1), s, NN)
    v_new = u - on_state[:, :c]
    o = on_state[:, c:] + _bdot(aqk, v_new, NN)
    return s * gl[:, :, :1] + _bdot(ks, v_new, TN), o


def _heads(x):
    return jnp.concatenate([x[None, :, h * HEAD_DIM:(h + 1) * HEAD_DIM] for h in range(DN_HEADS)], axis=0)


def _unheads(x):
    return jnp.concatenate([x[h] for h in range(x.shape[0])], axis=-1)


def _dn_rev(t, nc, n):
    return jnp.where(t < nc, nc - 1 - t, n - 1 - (t - nc))


PRE_CHUNKS = 2


def _pre_shapes(n):
    c, d, h = DN_CHUNK, HEAD_DIM, DN_HEADS
    shapes = [(n, h, c, d)] * 8 + [(n, h, c, c)] * 2 + [(n, h, 1, LANES)] * 2
    return shapes, [pl.BlockSpec((PRE_CHUNKS,) + s[1:], lambda t: (t, 0, 0, 0)) for s in shapes]


def _inverse_spec(n):
    shape = (n, N_DIRHEAD, DN_CHUNK, DN_CHUNK)
    return shape, pl.BlockSpec((PRE_CHUNKS,) + shape[1:], lambda t: (t, 0, 0, 0))


def _pre_row_specs():
    rows = PRE_CHUNKS * DN_CHUNK
    return pl.BlockSpec((rows, DN_HEADS * HEAD_DIM), lambda t: (t, 0)), pl.BlockSpec((rows, LANES), lambda t: (t, 0))


def _dn_pre_fwd(name, q, k, v, gb):
    n = q.shape[0] // DN_CHUNK
    shapes, specs = _pre_shapes(n)
    t_shape, t_spec = _inverse_spec(n)

    def body(q_ref, k_ref, v_ref, g_ref, *o_refs):
        for s in range(PRE_CHUNKS):
            rows = pl.ds(s * DN_CHUNK, DN_CHUNK)
            q4, k4, v4, gb_ = _heads(q_ref[rows, :]), _heads(k_ref[rows, :]), _heads(v_ref[rows, :]), g_ref[rows, :]
            t = _inv_doubling(_dn_a(k4, gb_))
            for i, r in enumerate(_dn_operands(q4, k4, v4, gb_, t)):
                o_refs[2 * i][s] = r[:DN_HEADS]
                o_refs[2 * i + 1][s] = r[DN_HEADS:]
            o_refs[-1][s] = t

    wide, narrow = _pre_row_specs()
    return pl.pallas_call(body, grid=(n // PRE_CHUNKS,), in_specs=[wide] * 3 + [narrow], out_specs=specs + [t_spec],
                          out_shape=[jax.ShapeDtypeStruct(s, F32) for s in shapes + [t_shape]],
                          compiler_params=_params(("parallel",)), name=name + "_pre_f")(q, k, v, gb)


def _dn_pre_bwd(name, q, k, v, gb, inv, cts):
    n = q.shape[0] // DN_CHUNK
    _, specs = _pre_shapes(n)
    _, t_spec = _inverse_spec(n)
    n_ct = len(specs)

    def body(q_ref, k_ref, v_ref, g_ref, t_ref, *refs):
        for s in range(PRE_CHUNKS):
            rows = pl.ds(s * DN_CHUNK, DN_CHUNK)
            ct = tuple(jnp.concatenate([refs[i][s], refs[i + 1][s]], axis=0) for i in range(0, n_ct, 2))
            q4, k4, v4, gb_, t = _heads(q_ref[rows, :]), _heads(k_ref[rows, :]), _heads(v_ref[rows, :]), g_ref[rows, :], t_ref[s]
            _, vjp = jax.vjp(_dn_operands, q4, k4, v4, gb_, t)
            dq, dk, dv, dg, dt = vjp(ct)
            da = -_bdot3_raw(_bdot3_raw(t, dt, TN), t, NT)
            _, vjp_a = jax.vjp(_dn_a, k4, gb_)
            dk_a, dg_a = vjp_a(da)
            for r, val in zip(refs[n_ct:], (_unheads(dq), _unheads(dk + dk_a), _unheads(dv), dg + dg_a)):
                r[rows, :] = val

    wide, narrow = _pre_row_specs()
    return pl.pallas_call(body, grid=(n // PRE_CHUNKS,), in_specs=[wide] * 3 + [narrow, t_spec] + specs,
                          out_specs=[wide] * 3 + [narrow],
                          out_shape=[jax.ShapeDtypeStruct(q.shape, F32)] * 3 + [jax.ShapeDtypeStruct(gb.shape, F32)],
                          compiler_params=_params(("parallel",)), name=name + "_pre_b")(q, k, v, gb, inv, *cts)


def _scan_specs(pre_shapes, fw, bw):
    maps = (lambda t: (fw(t), 0, 0, 0), lambda t: (bw(t), 0, 0, 0))
    return [pl.BlockSpec((None,) + s[1:], maps[i % 2]) for i, s in enumerate(pre_shapes)]


def _dn_scan_fwd(name, pre, nc):
    n = pre[0].shape[0]
    r = n * DN_CHUNK
    hd, nh, nb = HEAD_DIM, DN_HEADS, N_DIRHEAD
    shapes, _ = _pre_shapes(n)
    n_in = len(shapes)

    def body(*refs):
        ins, (of_ref, ob_ref, sf_ref, sb_ref, s_scr) = refs[:n_in], refs[n_in:]
        t = pl.program_id(0)

        @pl.when(t == 0)
        def _():
            s_scr[...] = jnp.zeros_like(s_scr)

        s = s_scr[...]
        sf_ref[...] = s[:nh]
        sb_ref[...] = s[nh:]
        args = [jnp.concatenate([ins[i][...], ins[i + 1][...]], axis=0) for i in range(0, n_in, 2)]
        s2, o = _dn_step(s, *args)
        s_scr[...] = s2
        of_ref[...] = _unheads(o[:nh])
        ob_ref[...] = _unheads(o[nh:])

    fw = lambda t: t
    bw = lambda t: _dn_rev(t, nc, n)
    wide, st = (DN_CHUNK, nh * hd), (None, nh, hd, hd)
    return pl.pallas_call(
        body, grid=(n,), in_specs=_scan_specs(shapes, fw, bw),
        out_specs=[pl.BlockSpec(wide, lambda t: (fw(t), 0)), pl.BlockSpec(wide, lambda t: (bw(t), 0)),
                   pl.BlockSpec(st, lambda t: (fw(t), 0, 0, 0)), pl.BlockSpec(st, lambda t: (bw(t), 0, 0, 0))],
        out_shape=[jax.ShapeDtypeStruct((r, nh * hd), F32)] * 2 + [jax.ShapeDtypeStruct((n, nh, hd, hd), F32)] * 2,
        scratch_shapes=[pltpu.VMEM((nb, hd, hd), F32)],
        compiler_params=_params(("arbitrary",)), name=name + "_scan_f",
    )(*pre)


def _dn_scan_bwd(name, pre, sall_f, sall_b, do_f, do_b, nc):
    n = pre[0].shape[0]
    hd, nh, nb = HEAD_DIM, DN_HEADS, N_DIRHEAD
    shapes, _ = _pre_shapes(n)
    n_in = len(shapes)

    def body(*refs):
        ins, (sf_ref, sb_ref, dof_ref, dob_ref) = refs[:n_in], refs[n_in:n_in + 4]
        outs, ds_scr = refs[n_in + 4:2 * n_in + 4], refs[2 * n_in + 4]
        t = pl.program_id(0)

        @pl.when(t == 0)
        def _():
            ds_scr[...] = jnp.zeros_like(ds_scr)

        args = [jnp.concatenate([ins[i][...], ins[i + 1][...]], axis=0) for i in range(0, n_in, 2)]
        s = jnp.concatenate([sf_ref[...], sb_ref[...]], axis=0)
        do = jnp.concatenate([_heads(dof_ref[...]), _heads(dob_ref[...])], axis=0)
        _, vjp = jax.vjp(_dn_step, s, *args)
        cts = vjp((ds_scr[...], do))
        ds_scr[...] = cts[0]
        for i, ct in enumerate(cts[1:]):
            outs[2 * i][...] = ct[:nh]
            outs[2 * i + 1][...] = ct[nh:]

    fw = lambda t: n - 1 - t
    bw = lambda t: _dn_rev(n - 1 - t, nc, n)
    wide, st = (DN_CHUNK, nh * hd), (None, nh, hd, hd)
    return tuple(pl.pallas_call(
        body, grid=(n,),
        in_specs=_scan_specs(shapes, fw, bw)
        + [pl.BlockSpec(st, lambda t: (fw(t), 0, 0, 0)), pl.BlockSpec(st, lambda t: (bw(t), 0, 0, 0)),
           pl.BlockSpec(wide, lambda t: (fw(t), 0)), pl.BlockSpec(wide, lambda t: (bw(t), 0))],
        out_specs=_scan_specs(shapes, fw, bw), out_shape=[jax.ShapeDtypeStruct(s, F32) for s in shapes],
        scratch_shapes=[pltpu.VMEM((nb, hd, hd), F32)],
        compiler_params=_params(("arbitrary",)), name=name + "_scan_b",
    )(*pre, sall_f, sall_b, do_f, do_b))


def deltanet(name, q, k, v, gb, ctx_rows):
    nc = ctx_rows // DN_CHUNK

    @jax.custom_vjp
    def pre(q, k, v, gb):
        return tuple(_dn_pre_fwd(name, q, k, v, gb)[:-1])

    def pre_fwd(*a):
        *ops, inv = _dn_pre_fwd(name, *a)
        return tuple(ops), (a, inv)

    pre.defvjp(pre_fwd, lambda res, cts: tuple(_dn_pre_bwd(name, *res[0], res[1], cts)))

    @jax.custom_vjp
    def scan(*ops):
        return tuple(_dn_scan_fwd(name, ops, nc)[:2])

    def scan_fwd(*ops):
        of, ob, sf, sb = _dn_scan_fwd(name, ops, nc)
        return (of, ob), (ops, sf, sb)

    scan.defvjp(scan_fwd, lambda res, cts: _dn_scan_bwd(name, res[0], res[1], res[2], cts[0], cts[1], nc))
    return scan(*pre(q, k, v, gb))


def _box_matrix(l, w):
    lo, hi = w // 2, w - 1 - w // 2
    pos = np.arange(l)
    start, end = np.clip(pos - lo, 0, l), np.clip(pos + hi + 1, 0, l)
    col = np.arange(l)[None, :]
    return ((col >= start[:, None]) & (col < end[:, None])) / (end - start)[:, None].astype(np.float64)


def _pool_matrices(ctx_rows, grid_rows):
    assert ctx_rows == ROW_TILE and ROW_TILE % GRID_W == 0
    ctx = np.stack([_box_matrix(ctx_rows, w) for w in POOL_WINDOWS])
    cols = np.stack([np.kron(np.eye(ROW_TILE // GRID_W), _box_matrix(GRID_W, w)) for w in POOL_WINDOWS])
    rows = np.stack([_box_matrix(grid_rows, w) for w in POOL_WINDOWS])[None]
    return np.stack([ctx, cols]).astype(np.float32), rows.astype(np.float32)


def _pool_apply(name, x, mats, group_w, seg_tiles, lane_tile):
    r, w = x.shape
    b = mats.shape[-1]
    ng = mats.shape[1]
    period = ng * group_w

    def body(x_ref, m_ref, o_ref):
        xv = x_ref[...].astype(F32)
        lane = lax.broadcasted_iota(jnp.int32, xv.shape, 1)
        grp = (lane % period) // group_w
        acc = jnp.zeros_like(xv)
        for g in range(ng):
            acc = acc + jnp.where(grp == g, _dot3(m_ref[g], xv), 0.0)
        o_ref[...] = acc

    return pl.pallas_call(
        body, grid=(r // b, w // lane_tile),
        in_specs=[pl.BlockSpec((b, lane_tile), lambda i, j: (i, j)),
                  pl.BlockSpec((None, ng, b, b), lambda i, j: (jnp.where(i >= seg_tiles, 1, 0) if mats.shape[0] > 1 else 0, 0, 0, 0))],
        out_specs=pl.BlockSpec((b, lane_tile), lambda i, j: (i, j)),
        out_shape=jax.ShapeDtypeStruct(x.shape, F32),
        compiler_params=_params(("parallel", "parallel")), name=name,
    )(x, mats)


def pool_means(name, u, ctx_rows):
    r, pw = u.shape
    grid_rows = (r - ctx_rows) // GRID_W
    gw = pw // len(POOL_WINDOWS)
    m1, m2 = _pool_matrices(ctx_rows, grid_rows)
    lane_tile = min(2048, GRID_W * pw)

    def apply(x, a1, a2, tag):
        y = _pool_apply(name + tag + "1", x, jnp.asarray(a1), gw, ctx_rows // ROW_TILE, pw)
        lat = y[ctx_rows:].reshape(grid_rows, GRID_W * pw)
        lat = _pool_apply(name + tag + "2", lat, jnp.asarray(a2), gw, 0, lane_tile)
        return jnp.concatenate([y[:ctx_rows], lat.reshape(r - ctx_rows, pw)], axis=0)

    @jax.custom_vjp
    def f(u):
        return apply(u, m1, m2, "_f")

    tr = lambda m: np.ascontiguousarray(np.swapaxes(m, -1, -2))
    f.defvjp(lambda u: (apply(u, m1, m2, "_f"), None), lambda _, ct: (apply(ct, tr(m1), tr(m2), "_b").astype(u.dtype),))
    return f(u)


def _ew(name, fn, *xs, n_out=1):
    shapes = jax.eval_shape(lambda *a: fn(*a), *xs)
    shapes = shapes if isinstance(shapes, (tuple, list)) else (shapes,)

    def body(*refs):
        res = fn(*[r[...] for r in refs[:len(xs)]])
        res = res if isinstance(res, (tuple, list)) else (res,)
        for r, o in zip(res, refs[len(xs):]):
            o[...] = r

    out = pl.pallas_call(body, out_shape=[jax.ShapeDtypeStruct(s.shape, s.dtype) for s in shapes],
                         compiler_params=_params(), name=name)(*xs)
    return out[0] if len(shapes) == 1 else tuple(out)


def _adamw_math(w, g, m, v):
    m2 = ADAM_B1 * m + (1.0 - ADAM_B1) * g
    v2 = ADAM_B2 * v + (1.0 - ADAM_B2) * (g * g)
    m_hat = m2 / (1.0 - ADAM_B1 ** ADAM_STEP)
    v_hat = v2 / (1.0 - ADAM_B2 ** ADAM_STEP)
    delta = -ADAM_LR * (m_hat / (jnp.sqrt(v_hat) + ADAM_EPS) + ADAM_WD * w)
    return delta, m2, v2


def adamw(name, w, g, m, v):
    *lead, r, c = w.shape
    tr = _pick(r, (256, 128, 64, 32, 16, 8))

    def body(w_ref, g_ref, m_ref, v_ref, d_ref, m2_ref, v2_ref):
        d_ref[...], m2_ref[...], v2_ref[...] = _adamw_math(w_ref[...], g_ref[...], m_ref[...], v_ref[...])

    if lead:
        grid, spec = (lead[0], r // tr), pl.BlockSpec((None, tr, c), lambda l, i: (l, i, 0))
    else:
        grid, spec = (r // tr,), pl.BlockSpec((tr, c), lambda i: (i, 0))
    return pl.pallas_call(body, grid=grid, in_specs=[spec] * 4, out_specs=[spec] * 3,
                          out_shape=[jax.ShapeDtypeStruct(w.shape, F32)] * 3,
                          compiler_params=_params(("parallel",) * len(grid)), name=name)(w, g, m, v)


def _sum_rows(name, xs, out_dtype=F32):
    r, c = xs[0].shape
    tr = _pick(r, tuple(t for t in (512, 256, 128, 64, 32, 16, 8) if t * c * 4 <= 2 << 20))

    def body(*refs):
        acc = refs[0][...].astype(F32)
        for ref in refs[1:-1]:
            acc = acc + ref[...].astype(F32)
        refs[-1][...] = acc.astype(out_dtype)

    spec = pl.BlockSpec((tr, c), lambda i: (i, 0))
    return pl.pallas_call(body, grid=(r // tr,), in_specs=[spec] * len(xs), out_specs=spec,
                          out_shape=jax.ShapeDtypeStruct((r, c), out_dtype),
                          compiler_params=_params(("parallel",)), name=name)(*xs)


def _place():
    return lax.axis_index("x"), lax.axis_index("y"), lax.axis_index("c")


def _chip_peers(x, y, c):
    return [(1 - x, y, c), (x, 1 - y, c), (1 - x, 1 - y, c)]


def all_gather8(name, block):
    m_per, n = block.shape

    def body(x_ref, out_ref, send_sems, recv_sems, local_sem):
        x, y, c = _place()
        me, sibling = (x, y, c), (x, y, 1 - c)
        chips = [(1 - x, y), (x, 1 - y), (1 - x, 1 - y)]

        def rows(px, py, pc):
            return out_ref.at[pl.ds((4 * px + 2 * py + pc) * m_per, m_per), :]

        def copy(k, blk, to, src=None):
            return pltpu.make_async_remote_copy(
                src_ref=rows(*blk) if src is None else src, dst_ref=rows(*blk),
                send_sem=send_sems.at[k], recv_sem=recv_sems.at[k], device_id=to, device_id_type=MESH_ID)

        mine = pltpu.make_async_copy(x_ref, rows(*me), local_sem)
        mine.start()
        first = [copy(0, me, sibling, src=x_ref)]
        first += [copy(1 + j, me, (*chip, c), src=x_ref) for j, chip in enumerate(chips)]
        for cp in first:
            cp.start()
        passed = [copy(4 + j, (*chip, c), sibling) for j, chip in enumerate(chips)]
        for j, chip in enumerate(chips):
            copy(1 + j, (*chip, c), me).wait_recv()
            passed[j].start()
        copy(0, sibling, me).wait_recv()
        for j, chip in enumerate(chips):
            copy(4 + j, (*chip, 1 - c), me).wait_recv()
        for cp in first + passed:
            cp.wait_send()
        mine.wait()

    return pl.pallas_call(
        body, out_shape=jax.ShapeDtypeStruct((8 * m_per, n), block.dtype),
        in_specs=[pl.BlockSpec(memory_space=pltpu.VMEM)], out_specs=pl.BlockSpec(memory_space=pltpu.VMEM),
        scratch_shapes=[pltpu.SemaphoreType.DMA((7,)), pltpu.SemaphoreType.DMA((7,)), pltpu.SemaphoreType.DMA],
        compiler_params=_params(), name=name,
    )(block)


N_PEERS = 3


def gather_chips(name, shards):
    n = len(shards)

    def body(*refs):
        xs, outs, (send_sems, recv_sems, local_sems) = refs[:n], refs[n:2 * n], refs[2 * n:]
        x, y, c = _place()
        local, remote = [], []
        for i in range(n):
            local.append(pltpu.make_async_copy(xs[i].at[c], outs[i].at[2 * x + y], local_sems.at[i]))
            local[-1].start()
            for p, peer in enumerate(_chip_peers(x, y, c)):
                remote.append(pltpu.make_async_remote_copy(
                    src_ref=xs[i].at[c], dst_ref=outs[i].at[2 * x + y], send_sem=send_sems.at[i * N_PEERS + p],
                    recv_sem=recv_sems.at[i * N_PEERS + p], device_id=peer, device_id_type=MESH_ID))
                remote[-1].start()
        for i in range(n):
            for p, (px, py, _) in enumerate(_chip_peers(x, y, c)):
                pltpu.make_async_remote_copy(
                    src_ref=xs[i].at[c], dst_ref=outs[i].at[2 * px + py], send_sem=send_sems.at[i * N_PEERS + p],
                    recv_sem=recv_sems.at[i * N_PEERS + p], device_id=(px, py, c), device_id_type=MESH_ID).wait_recv()
        for cp in remote:
            cp.wait_send()
        for cp in local:
            cp.wait()

    hbm = pl.BlockSpec(memory_space=pltpu.HBM)
    return pl.pallas_call(
        body, out_shape=[jax.ShapeDtypeStruct((N_CHIPS,) + s.shape[1:], s.dtype) for s in shards],
        in_specs=[hbm] * n, out_specs=[hbm] * n,
        scratch_shapes=[pltpu.SemaphoreType.DMA((n * N_PEERS,)), pltpu.SemaphoreType.DMA((n * N_PEERS,)),
                        pltpu.SemaphoreType.DMA((n,))],
        compiler_params=_params(), name=name,
    )(*shards)


def swap_sibling(name, blocks):
    n = len(blocks)
    pairs = [tuple(b) if isinstance(b, (tuple, list)) else (b,) for b in blocks]
    flat = [a for p in pairs for a in p]
    first = [sum(len(q) for q in pairs[:i]) for i in range(n)]

    def body(*refs):
        xs, outs, (send_sems, recv_sems) = refs[:len(flat)], refs[len(flat):len(flat) + n], refs[len(flat) + n:]
        x, y, c = _place()

        def copy(i, src):
            return pltpu.make_async_remote_copy(src_ref=src, dst_ref=outs[i], send_sem=send_sems.at[i], recv_sem=recv_sems.at[i],
                                                device_id=(x, y, 1 - c), device_id_type=MESH_ID)

        for i, p in enumerate(pairs):
            if len(p) == 1:
                copy(i, xs[first[i]]).start()
            else:
                for half in range(2):
                    pl.when(c == 1 - half)(copy(i, xs[first[i] + half]).start)
        for i in range(n):
            copy(i, xs[first[i]]).wait()

    hbm = pl.BlockSpec(memory_space=pltpu.HBM)
    return pl.pallas_call(
        body, out_shape=[jax.ShapeDtypeStruct(p[0].shape, p[0].dtype) for p in pairs],
        in_specs=[hbm] * len(flat), out_specs=[hbm] * n,
        scratch_shapes=[pltpu.SemaphoreType.DMA((n,)), pltpu.SemaphoreType.DMA((n,))],
        compiler_params=_params(), name=name,
    )(*flat)


def scatter_chips(name, pieces):
    n = len(pieces)

    def body(*refs):
        xs, outs, (send_sems, recv_sems) = refs[:n], refs[n:2 * n], refs[2 * n:]
        x, y, c = _place()
        copies = []
        for i in range(n):
            for p, (px, py, pc) in enumerate(_chip_peers(x, y, c)):
                copies.append(pltpu.make_async_remote_copy(
                    src_ref=xs[i].at[2 * px + py], dst_ref=outs[i].at[p], send_sem=send_sems.at[i * N_PEERS + p],
                    recv_sem=recv_sems.at[i * N_PEERS + p], device_id=(px, py, pc), device_id_type=MESH_ID))
                copies[-1].start()
        for cp in copies:
            cp.wait()

    hbm = pl.BlockSpec(memory_space=pltpu.HBM)
    return pl.pallas_call(
        body, out_shape=[jax.ShapeDtypeStruct((N_PEERS,) + p.shape[1:], p.dtype) for p in pieces],
        in_specs=[hbm] * n, out_specs=[hbm] * n,
        scratch_shapes=[pltpu.SemaphoreType.DMA((n * N_PEERS,)), pltpu.SemaphoreType.DMA((n * N_PEERS,))],
        compiler_params=_params(), name=name,
    )(*pieces)


def pair_sum(name, half0, half1, got, out_dtype):
    r, w = got.shape
    tr = _pick(r, tuple(t for t in (512, 256, 128, 64, 32, 16, 8) if t * w * 4 <= 2 << 20))

    def body(a0, a1, g, o):
        mine = jnp.where(lax.axis_index("c") == 0, a0[...], a1[...])
        o[...] = (mine.astype(F32) + g[...].astype(F32)).astype(out_dtype)

    spec = pl.BlockSpec((tr, w), lambda i: (i, 0))
    return pl.pallas_call(body, grid=(r // tr,), in_specs=[spec] * 3, out_specs=spec,
                          out_shape=jax.ShapeDtypeStruct((r, w), out_dtype),
                          compiler_params=_params(("parallel",)), name=name)(half0, half1, got)


def _sum_devices(name, got, fold=False):
    def body(a_ref, *o_refs):
        acc = a_ref[0]
        for i in range(1, N_DEV):
            acc = acc + a_ref[i]
        o_refs[0][...] = acc
        if fold:
            o_refs[1][...] = acc + pltpu.roll(acc, SUBLANES // 2, 0)

    shape = jax.ShapeDtypeStruct(got.shape[1:], F32)
    out = pl.pallas_call(body, out_shape=[shape] * (2 if fold else 1), compiler_params=_params(), name=name)(got)
    return out if fold else out[0]


def _w_in_bounds(d, pw, scw):
    off_z = 3 * DN_WIDTH
    off_a = off_z + DN_WIDTH
    off_pool = off_a + 2 * N_DIRHEAD
    off_sc = off_pool + pw
    off_gate = off_sc + 3 * scw
    return (0, off_z, off_a, off_pool, off_sc, off_gate, off_gate + 3 * d)


def _misc_widths(pw, scw):
    return (DN_WIDTH, pw, 3 * scw, LANES)


def split_matrices(full, pw, scw):
    depth, d = len(full["w_in"]), full["w_in"][0].shape[0]
    b = _w_in_bounds(d, pw, scw)
    out = {k: [] for k in ("qkv", "gate", "misc", "gu", "br_a", "br_b", "br_c", "o", "down")}
    for l in range(depth):
        w = full["w_in"][l]
        out["qkv"].append(w[:, b[0]:b[1]])
        out["gate"].append(w[:, b[5]:b[6]])
        out["misc"].append(jnp.concatenate([w[:, b[1]:b[2]], w[:, b[3]:b[5]], _pad_lanes(w[:, b[2]:b[3]])], axis=1))
        out["gu"].append(full["w_gu"][l])
        for k in ("br_a", "br_b", "br_c", "o", "down"):
            out[k].append(full["w_" + k][l])
    return out


def join_matrix_grads(g, pw, scw):
    depth = len(g["qkv"])
    n_z, n_rest, n_ab = DN_WIDTH, pw + 3 * scw, 2 * N_DIRHEAD
    w_in = []
    for l in range(depth):
        m = g["misc"][l]
        w_in.append(jnp.concatenate([g["qkv"][l], m[:, :n_z], m[:, n_z + n_rest:n_z + n_rest + n_ab],
                                     m[:, n_z:n_z + n_rest], g["gate"][l]], axis=1))
    out = {"w_in": w_in, "w_gu": g["gu"]}
    for k in ("br_a", "br_b", "br_c", "o", "down"):
        out["w_" + k] = g[k]
    return {k: [a.astype(MXU_DTYPE) for a in v] for k, v in out.items()}


def split_cols(x, widths):
    edges = np.cumsum((0,) + tuple(widths))

    def cut(x):
        return tuple(x[:, a:b] for a, b in zip(edges[:-1], edges[1:]))

    f = jax.custom_vjp(cut)
    f.defvjp(lambda x: (cut(x), None), lambda _, cts: (jnp.concatenate(cts, axis=1),))
    return f(x)


def _row(v):
    return v.reshape(1, -1)


def _pad_lanes(v, width=LANES):
    return jnp.pad(v, ((0, 0), (0, width - v.shape[1])))


def _block_diag(blocks):
    g, n, _ = blocks.shape
    out = jnp.zeros((g * n, g * n), blocks.dtype)
    for i in range(g):
        out = out.at[i * n:(i + 1) * n, i * n:(i + 1) * n].set(blocks[i])
    return out


def local_loss(p, carriers, x, mod_lat, mod_ctx, wts, ctx, target):
    ctx_rows, d = ctx.shape
    depth = len(wts["qkv"])
    pw, scw = p["pool_scale"].shape[1], p["sc_conv_w"].shape[2]
    ff = wts["down"][0].shape[0]
    ct = ctx_rows // ROW_TILE
    xs = jnp.concatenate([ctx, x], axis=0)
    seg = lambda l, k: jnp.stack([mod_ctx[l, k], mod_lat[l, k]]).reshape(2, 1, d)
    dn = gate2 = None
    for l in range(depth):
        sh1, sc1, g1, sh2, sc2, g2 = (seg(l, k) for k in range(6))
        tag = f"l{l}_"
        lin = lambda key, a: mm(tag + key, a, wts[key][l], carriers[key][l])
        if l == 0:
            (h1,) = rowwise(tag + "mod", _fn_modulate, [xs], [_row(p["norm1_g"][l]), sh1, sc1], seg=(False, True, True),
                            outs=[(d, MXU_DTYPE, 1)], ctx_tiles=ct)
        else:
            xs, h1 = rowwise(tag + "resmod1", _fn_resmod, [xs, dn], [gate2, _row(p["norm1_g"][l]), sh1, sc1],
                             seg=(True, False, True, True), outs=[(d, F32, 1), (d, MXU_DTYPE, 1)], ctx_tiles=ct)
        fan = ("qkv", "gate", "misc")
        p_qkv, p_gate, p_misc = mm_fanout(tag + "in", h1, [wts[k][l] for k in fan], [carriers[k][l] for k in fan])
        p_z, p_pool, p_sc, p_ab = split_cols(p_misc, _misc_widths(pw, scw))
        cw = p["dn_conv_w"][l]
        qkv_conv = conv3(tag + "dnconv", p_qkv, cw[0:1], cw[1:2], cw[2:3], ct, F32)
        q, k, v, gb = rowwise(
            tag + "dnprep", _fn_dnprep, [qkv_conv, p_ab],
            [_pad_lanes(p["dn_a_log"][l].reshape(1, -1)), _pad_lanes(p["dn_dt_bias"][l].reshape(1, -1))],
            parts_r=(3 * DN_HEADS, 1), outs=[(DN_WIDTH, F32, DN_HEADS)] * 3 + [(LANES, F32, 1)], ctx_tiles=ct)
        o_f, o_b = deltanet(tag + "dn", q, k, v, gb, ctx_rows)
        (oz,) = rowwise(tag + "dnpost", _fn_dnpost, [o_f, o_b, p_z], [_row(p["dn_norm_g"][l])], parts_r=(DN_HEADS,) * 3,
                        outs=[(DN_WIDTH, MXU_DTYPE, DN_HEADS)], ctx_tiles=ct)
        y_a = lin("br_a", oz)
        means = pool_means(tag + "box", p_pool, ctx_rows)
        (dpool,) = rowwise(tag + "poolsub", _fn_sub, [means, p_pool], [], outs=[(pw, MXU_DTYPE, 1)], ctx_tiles=ct)
        pool_mat = _block_diag(p["pool_w"][l])
        mixed = mm(tag + "poolw", dpool, pool_mat.astype(MXU_DTYPE), pool_mat)
        (yb_in,) = rowwise(tag + "poolscale", _fn_scale, [mixed], [_row(p["pool_scale"][l])], outs=[(pw, MXU_DTYPE, 1)], ctx_tiles=ct)
        y_b = lin("br_b", yb_in)
        sw = p["sc_conv_w"][l]
        (yc_in,) = rowwise(tag + "sconv", _fn_shortconv, [p_sc], [sw[0:1], sw[1:2], sw[2:3]], halo=(True,), parts_r=(3,),
                           outs=[(scw, MXU_DTYPE, 1)], ctx_tiles=ct)
        y_c = lin("br_c", yc_in)
        (y,) = rowwise(tag + "merge", _fn_merge, [p_gate, y_a, y_b, y_c], [], parts_r=(3, 1, 1, 1),
                       outs=[(d, MXU_DTYPE, 1)], ctx_tiles=ct)
        mix = lin("o", y)
        xs, h2 = rowwise(tag + "resmod2", _fn_resmod, [xs, mix], [g1, _row(p["norm2_g"][l]), sh2, sc2],
                         seg=(True, False, True, True), outs=[(d, F32, 1), (d, MXU_DTYPE, 1)], ctx_tiles=ct)
        (act,) = rowwise(tag + "swiglu", _fn_swiglu, [lin("gu", h2)], [], parts_r=(2,),
                         outs=[(ff, MXU_DTYPE, 1)], ctx_tiles=ct)
        dn = lin("down", act)
        gate2 = g2
    counts = jnp.concatenate([jnp.zeros((1, 1, LANES), F32), jnp.ones((1, 1, LANES), F32)])
    (total,) = rowwise("final", _fn_final, [xs, dn, target], [gate2, _row(p["final_norm_g"]), counts],
                       seg=(True, False, True), reds=[LANES], ctx_tiles=ct, skip=(0, 0, ct))
    return total[0, 0]


BIG = ("w_in", "w_br_a", "w_br_b", "w_br_c", "w_o", "w_gu", "w_down")
ROW_SHARDED = ("w_o", "w_down")
SMALL_REPL = ("norm1_g", "norm2_g", "dn_a_log", "dn_dt_bias", "dn_norm_g", "pool_w", "pool_scale", "final_norm_g")
SMALL_SHARD = ("dn_conv_w", "sc_conv_w")
WEIGHTS = ("c_ctx", "w_ada", "b_ada", "norm1_g", "norm2_g", "w_in", "dn_conv_w", "dn_a_log", "dn_dt_bias", "dn_norm_g", "pool_w",
           "pool_scale", "sc_conv_w", "w_br_a", "w_br_b", "w_br_c", "w_o", "w_gu", "w_down", "final_norm_g")
N_CHIPS, N_DEV = 4, 8
COMM_COLS = 1024


def _pack(arrays, rows_multiple, cols=COMM_COLS, dtype=F32):
    size = sum(int(np.prod(a.shape)) for a in arrays)
    rows = -(-size // (cols * rows_multiple)) * rows_multiple
    tail = [jnp.zeros((rows * cols - size,), dtype)] if rows * cols > size else []
    return jnp.concatenate([a.astype(dtype).reshape(-1) for a in arrays] + tail).reshape(rows, cols)


def _unpack(flat, shapes):
    out, pos = [], 0
    for s in shapes:
        n = int(np.prod(s))
        out.append(flat[pos:pos + n].reshape(s))
        pos += n
    return out


def _join_shards(parts, name):
    _, a, b = parts.shape
    if name in ROW_SHARDED:
        return parts.reshape(N_CHIPS * a, b)
    return jnp.moveaxis(parts, 0, 1).reshape(a, N_CHIPS * b)


def _cut_shards(full, name):
    k, n = full.shape
    if name in ROW_SHARDED:
        return full.reshape(N_CHIPS, k // N_CHIPS, n)
    return jnp.moveaxis(full.reshape(k, N_CHIPS, n // N_CHIPS), 1, 0)


def _dsilu(x):
    s = _sigmoid(x)
    return s + x * s * (1.0 - s)


def kernel(x, c, ctx, c_ctx, w_ada, b_ada, norm1_g, norm2_g, w_in, dn_conv_w, dn_a_log, dn_dt_bias, dn_norm_g, pool_w, pool_scale, sc_conv_w, w_br_a, w_br_b, w_br_c, w_o, w_gu, w_down, final_norm_g, loss_target, m_c_ctx, m_w_ada, m_b_ada, m_norm1_g, m_norm2_g, m_w_in, m_dn_conv_w, m_dn_a_log, m_dn_dt_bias, m_dn_norm_g, m_pool_w, m_pool_scale, m_sc_conv_w, m_w_br_a, m_w_br_b, m_w_br_c, m_w_o, m_w_gu, m_w_down, m_final_norm_g, v_c_ctx, v_w_ada, v_b_ada, v_norm1_g, v_norm2_g, v_w_in, v_dn_conv_w, v_dn_a_log, v_dn_dt_bias, v_dn_norm_g, v_pool_w, v_pool_scale, v_sc_conv_w, v_w_br_a, v_w_br_b, v_w_br_c, v_w_o, v_w_gu, v_w_down, v_final_norm_g):
    given = dict(locals())
    ix, iy, ic = _place()
    chip, dev = 2 * ix + iy, 4 * ix + 2 * iy + ic
    d = x.shape[-1]
    depth = w_in.shape[0]
    ada_cols = w_ada.shape[2]
    spare = 2 * SUBLANES - N_DEV - 1

    got0 = all_gather8("gather_cond", _pack([c, dn_conv_w, sc_conv_w], SUBLANES)).reshape(N_DEV, -1)
    c_all = got0[:, :d]
    taps = [_unpack(got0[2 * j, d:], [dn_conv_w.shape, sc_conv_w.shape]) for j in range(N_CHIPS)]
    full = {"dn_conv_w": jnp.concatenate([t[0] for t in taps], axis=-1),
            "sc_conv_w": jnp.concatenate([t[1] for t in taps], axis=-1)}
    assert depth == 2, "the two cores of a chip split the layers between them"
    got_layer = gather_chips("gather_weights", [given[n].astype(MXU_DTYPE) for n in BIG])
    other_layer = swap_sibling("swap_weights", got_layer)
    mats = {n: [_join_shards(jnp.where(ic == l, got_layer[i], other_layer[i]), n) for l in range(depth)]
            for i, n in enumerate(BIG)}
    for n in SMALL_REPL:
        full[n] = given[n]
    pw, scw = pool_scale.shape[1], full["sc_conv_w"].shape[2]
    wts = split_matrices(mats, pw, scw)
    carriers = {k: [jnp.zeros(w.shape, F32) for w in ws] for k, ws in wts.items()}

    cond = jnp.concatenate([c_all, c_ctx[None], jnp.zeros((spare, d), F32)])
    s_cond = _ew("silu_cond", _silu, cond)
    mod_cols = jnp.concatenate([_mm(f"ada{l}_f", s_cond, w_ada[l], NN) for l in range(depth)], axis=1)
    mod_got = all_gather8("gather_mod", mod_cols).reshape(N_DEV, 2 * SUBLANES, depth, ada_cols)
    mod_all = jnp.concatenate([mod_got[2 * j] for j in range(N_CHIPS)], axis=-1) + b_ada[None]
    mod_lat = lax.dynamic_index_in_dim(mod_all, dev, 0, keepdims=False).reshape(depth, 6, d)
    mod_ctx = mod_all[N_DEV].reshape(depth, 6, d)

    loss_local, (g_full, g_mats, grad_x, g_mod_lat, g_mod_ctx) = jax.value_and_grad(local_loss, argnums=(0, 1, 2, 3, 4))(
        full, carriers, x[0], mod_lat, mod_ctx, wts, ctx[0], loss_target[0])
    g_mats_full = join_matrix_grads(g_mats, pw, scw)
    loss = lax.psum(loss_local, ("x", "y", "c"))

    dmod_cols = (2 * depth * 6 * d) // SUBLANES
    dmod = all_gather8("gather_dmod", _pack([g_mod_lat, g_mod_ctx], SUBLANES, cols=dmod_cols))
    dmod = dmod.reshape(N_DEV, SUBLANES, dmod_cols)
    dmod_sum, dmod_fold = _sum_devices("reduce_dmod", dmod, fold=True)
    half_rows = SUBLANES // 2
    grad_b_ada = dmod_fold[:half_rows].reshape(depth, 6 * d)
    dctx_sum = dmod_sum[half_rows:].reshape(1, depth, 6 * d)
    d9 = jnp.concatenate([dmod[:, :half_rows].reshape(N_DEV, depth, 6 * d), dctx_sum, jnp.zeros((spare, depth, 6 * d), F32)])
    d9 = lax.dynamic_slice_in_dim(d9, chip * ada_cols, ada_cols, axis=2)
    grad_w_ada = jnp.stack([_mm(f"ada{l}_dw", s_cond, d9[:, l], TN) for l in range(depth)])
    ds_cond = [_mm(f"ada{l}_da", d9[:, l], w_ada[l], NT) for l in range(depth)]
    dsilu_part = _sum_rows("sum_dcond", ds_cond)[N_DEV]

    small_names = SMALL_REPL + SMALL_SHARD
    small_grads = [dsilu_part] + [g_full[n] for n in small_names]
    small_shapes = [a.shape for a in small_grads]
    n_small = sum(int(np.prod(s)) for s in small_shapes)
    cols = -(-n_small // (SUBLANES * LANES)) * LANES
    got2 = all_gather8("gather_small", _pack(small_grads, SUBLANES, cols=cols)).reshape(N_DEV, SUBLANES, cols)
    small_sum = _unpack(_sum_devices("reduce_small", got2).reshape(-1), small_shapes)
    grads = dict(zip(small_names, small_sum[1:]))
    grads["c_ctx"] = _ew("dsilu", lambda g, z: 0.5 * g * _dsilu(z), _row(small_sum[0]), _row(c_ctx)).reshape(-1)
    grads["w_ada"], grads["b_ada"] = grad_w_ada, grad_b_ada
    for n in SMALL_SHARD:
        width = given[n].shape[-1]
        grads[n] = lax.dynamic_slice_in_dim(grads[n], chip * width, width, axis=-1)

    from_sibling = swap_sibling("swap_layers", [tuple(g_mats_full[n]) for n in BIG])
    pieces = [_cut_shards(pair_sum("sum_pair_" + n, *g_mats_full[n], from_sibling[i], MXU_DTYPE), n) for i, n in enumerate(BIG)]
    from_chips = scatter_chips("scatter_pieces", pieces)
    reduced = []
    for i, n in enumerate(BIG):
        own = lax.dynamic_index_in_dim(pieces[i], chip, 0, keepdims=False)
        a, b = own.shape
        view = lambda t: t.reshape(-1, b)
        reduced.append(_sum_rows("sum_chips_" + n, [view(own)] + [view(from_chips[i][p]) for p in range(N_PEERS)]).reshape(a, b))
    other_reduced = swap_sibling("swap_reduced", reduced)
    for i, n in enumerate(BIG):
        grads[n] = jnp.stack([jnp.where(ic == l, reduced[i], other_reduced[i]) for l in range(depth)])

    delta, new_m, new_v = {}, {}, {}
    large = BIG + ("w_ada",)
    for n in large:
        delta[n], new_m[n], new_v[n] = adamw("adamw_" + n, given[n], grads[n], given["m_" + n], given["v_" + n])
    rest = [n for n in WEIGHTS if n not in large]
    rest_shapes = [given[n].shape for n in rest]
    packed = [_pack([src[pre + n] for n in rest], SUBLANES, cols=LANES)
              for src, pre in ((given, ""), (grads, ""), (given, "m_"), (given, "v_"))]
    for res, o in zip((delta, new_m, new_v), adamw("adamw_small", *packed)):
        for n, a in zip(rest, _unpack(o.reshape(-1), rest_shapes)):
            res[n] = a
    return (loss, grad_x[None], *[grads[n] for n in WEIGHTS], *[delta[n] for n in WEIGHTS],
            *[new_m[n] for n in WEIGHTS], *[new_v[n] for n in WEIGHTS])
```

```python
import functools
import math

import numpy as np
import jax
import jax.numpy as jnp
from jax import lax
from jax.experimental import pallas as pl
from jax.experimental.pallas import tpu as pltpu

F32 = jnp.float32
MXU_DTYPE = jnp.bfloat16
HIGHEST = lax.Precision.HIGHEST

DN_HEADS = 4
HEAD_DIM = 128
DN_WIDTH = DN_HEADS * HEAD_DIM
DN_CHUNK = 64
GRID_W = 64
EPS = 1e-6
POOL_WINDOWS = (2, 4, 8, 16)
N_DIRHEAD = 2 * DN_HEADS
ADAM_LR, ADAM_B1, ADAM_B2, ADAM_EPS, ADAM_WD, ADAM_STEP = 0.001, 0.9, 0.999, 1e-08, 0.01, 10

LANES = 128
SUBLANES = 8
ROW_TILE = 256
VMEM_LIMIT = 56 * 1024 * 1024

MESH_ID = pl.DeviceIdType.MESH
NN, NT, TN = ((1,), (0,)), ((1,), (1,)), ((0,), (0,))


def _params(sem=None):
    return pltpu.CompilerParams(dimension_semantics=sem, vmem_limit_bytes=VMEM_LIMIT)


def _pick(n, cands):
    for c in cands:
        if c <= n and n % c == 0:
            return c
    return n


def _mm(name, a, b, dims, out_dtype=F32):
    if dims == NN:
        (m, kk), (_, n) = a.shape, b.shape
    elif dims == NT:
        (m, kk), (n, _) = a.shape, b.shape
    else:
        (kk, m), (_, n) = a.shape, b.shape
    tm = _pick(m, (768, 1024, 1408, 512, 256, 128, 64, 32, 16, 8))
    tn = _pick(n, ((3072, 2816) if dims == NN else ()) + (1536, 1024, 1408, 1664, 768, 896, 512, 256, 128))
    tk = kk if dims == NN and kk <= 2816 else _pick(kk, (1024, 1408, 768, 512, 256, 128))
    gi, gj, gl = m // tm, n // tn, kk // tk
    if dims == NN:
        a_spec = pl.BlockSpec((tm, tk), lambda i, j, l: (i, l))
        b_spec = pl.BlockSpec((tk, tn), lambda i, j, l: (l, j))
    elif dims == NT:
        a_spec = pl.BlockSpec((tm, tk), lambda i, j, l: (i, l))
        b_spec = pl.BlockSpec((tn, tk), lambda i, j, l: (j, l))
    else:
        a_spec = pl.BlockSpec((tk, tm), lambda i, j, l: (l, i))
        b_spec = pl.BlockSpec((tk, tn), lambda i, j, l: (l, j))
    direct = gl == 1
    use_acc = (not direct) and out_dtype != F32

    def body(a_ref, b_ref, o_ref, *scratch):
        part = lax.dot_general(a_ref[...].astype(MXU_DTYPE), b_ref[...].astype(MXU_DTYPE), (dims, ((), ())),
                               preferred_element_type=F32)
        if direct:
            o_ref[...] = part.astype(out_dtype)
            return
        acc = scratch[0] if use_acc else o_ref
        l = pl.program_id(2)

        @pl.when(l == 0)
        def _():
            acc[...] = part

        @pl.when(l > 0)
        def _():
            acc[...] += part

        if use_acc:
            @pl.when(l == gl - 1)
            def _():
                o_ref[...] = acc[...].astype(out_dtype)

    return pl.pallas_call(
        body, grid=(gi, gj, gl), in_specs=[a_spec, b_spec],
        out_specs=pl.BlockSpec((tm, tn), lambda i, j, l: (i, j)),
        out_shape=jax.ShapeDtypeStruct((m, n), out_dtype),
        scratch_shapes=[pltpu.VMEM((tm, tn), F32)] if use_acc else [],
        compiler_params=_params(("parallel", "parallel", "arbitrary")), name=name,
    )(a, b)


def mm(name, a, w, carrier):
    @jax.custom_vjp
    def f(a, w, carrier):
        return _mm(name + "_f", a, w, NN, out_dtype=MXU_DTYPE)

    def fwd(a, w, carrier):
        return f(a, w, carrier), (a, w)

    def bwd(res, dc):
        a, w = res
        da = _mm(name + "_da", dc, w, NT, out_dtype=a.dtype)
        dw = _mm(name + "_dw", a, dc, TN)
        return da, None, dw

    f.defvjp(fwd, bwd)
    return f(a, w, carrier)


def _mm_nt_sum(name, dcs, ws, out_dtype):
    m, kk = dcs[0].shape[0], ws[0].shape[0]
    tm = _pick(m, (768, 1024, 512, 256, 128, 64, 32, 16, 8))
    tns = [_pick(w.shape[1], (1664, 1536, 1024, 768, 896, 512, 256, 128)) for w in ws]
    counts = [w.shape[1] // tn for w, tn in zip(ws, tns)]
    starts = [sum(counts[:g]) for g in range(len(ws))]
    steps = sum(counts)

    def col(g):
        return lambda i, t: jnp.clip(t - starts[g], 0, counts[g] - 1)

    def body(*refs):
        dc_refs, w_refs, o_ref, acc = refs[:len(ws)], refs[len(ws):2 * len(ws)], refs[-2], refs[-1]
        t = pl.program_id(1)

        @pl.when(t == 0)
        def _():
            acc[...] = jnp.zeros_like(acc)

        for g in range(len(ws)):
            @pl.when(jnp.logical_and(t >= starts[g], t < starts[g] + counts[g]))
            def _():
                acc[...] += lax.dot_general(dc_refs[g][...].astype(MXU_DTYPE), w_refs[g][...].astype(MXU_DTYPE),
                                            (NT, ((), ())), preferred_element_type=F32)

        @pl.when(t == steps - 1)
        def _():
            o_ref[...] = acc[...].astype(out_dtype)

    dc_specs = [pl.BlockSpec((tm, tns[g]), (lambda c: lambda i, t: (i, c(i, t)))(col(g))) for g in range(len(ws))]
    w_specs = [pl.BlockSpec((kk, tns[g]), (lambda c: lambda i, t: (0, c(i, t)))(col(g))) for g in range(len(ws))]
    return pl.pallas_call(
        body, grid=(m // tm, steps), in_specs=dc_specs + w_specs,
        out_specs=pl.BlockSpec((tm, kk), lambda i, t: (i, 0)),
        out_shape=jax.ShapeDtypeStruct((m, kk), out_dtype),
        scratch_shapes=[pltpu.VMEM((tm, kk), F32)],
        compiler_params=_params(("parallel", "arbitrary")), name=name,
    )(*dcs, *ws)


def mm_fanout(name, a, ws, carriers):
    n = len(ws)

    @jax.custom_vjp
    def f(a, ws, carriers):
        return tuple(_mm(f"{name}{g}_f", a, ws[g], NN, out_dtype=MXU_DTYPE) for g in range(n))

    def fwd(a, ws, carriers):
        return f(a, ws, carriers), (a, ws)

    def bwd(res, dcs):
        a, ws = res
        da = _mm_nt_sum(name + "_da", list(dcs), list(ws), a.dtype)
        dws = tuple(_mm(f"{name}{g}_dw", a, dcs[g], TN) for g in range(n))
        return da, None, dws

    f.defvjp(fwd, bwd)
    return f(a, tuple(ws), tuple(carriers))


def _split(x, parts):
    w = x.shape[-1] // parts
    return [x[:, k * w:(k + 1) * w] for k in range(parts)]


def _cat(xs):
    return xs[0] if len(xs) == 1 else jnp.concatenate(xs, axis=-1)


def _shift_rows(x, prev_ref, next_ref, i, ctx_tiles, nt):
    tr = x.shape[0]
    rid = lax.broadcasted_iota(jnp.int32, x.shape, 0)
    first = jnp.logical_or(i == 0, i == ctx_tiles)
    last = jnp.logical_or(i == ctx_tiles - 1, i == nt - 1)
    prow = jnp.where(first, 0.0, prev_ref[SUBLANES - 1:SUBLANES, :].astype(F32))
    nrow = jnp.where(last, 0.0, next_ref[0:1, :].astype(F32))
    xm = jnp.where(rid == 0, prow, pltpu.roll(x, 1, 0))
    xp = jnp.where(rid == tr - 1, nrow, pltpu.roll(x, tr - 1, 0))
    return xm, xp


def rowwise(name, fn, rows, vecs, *, halo=(), seg=(), parts_r=None, parts_v=None, outs=(), reds=(), ctx_tiles=1, skip=None,
            long_tiles=False):
    nr, nv = len(rows), len(vecs)
    halo = tuple(halo) or (False,) * nr
    seg = tuple(seg) or (False,) * nv
    skip = tuple(skip or (0,) * nr)
    parts_r = tuple(parts_r or (1,) * nr)
    parts_v = tuple(parts_v or (1,) * nv)
    r_total = rows[0].shape[0]
    tr = _pick(r_total, (2816, 2112, 1408, 768)) if long_tiles else ROW_TILE
    assert not long_tiles or not (any(halo) or any(seg) or any(skip))
    nt = r_total // tr
    assert r_total % tr == 0 and (ctx_tiles > 0 or not any(seg)) and not any(h and s for h, s in zip(halo, skip))

    def tile_map(i):
        return lambda t: (jnp.maximum(t - skip[i], 0), 0)

    def row_specs():
        sp = []
        for i, r in enumerate(rows):
            w = r.shape[1]
            sp.append(pl.BlockSpec((tr, w), tile_map(i)))
            if halo[i]:
                k = tr // SUBLANES
                sp.append(pl.BlockSpec((SUBLANES, w), lambda t: (jnp.maximum(t * k - 1, 0), 0)))
                sp.append(pl.BlockSpec((SUBLANES, w), lambda t: (jnp.minimum((t + 1) * k, nt * k - 1), 0)))
        return sp

    def vec_spec(j):
        w = vecs[j].shape[-1]
        if seg[j]:
            return pl.BlockSpec((None, 1, w), lambda t: (jnp.where(t >= ctx_tiles, 1, 0), 0, 0))
        return pl.BlockSpec((1, w), lambda t: (0, 0))

    def row_args(rv):
        a = []
        for i in range(nr):
            a += [rv[i]] * 3 if halo[i] else [rv[i]]
        return a

    def load(refs, t):
        pos, rp = 0, []
        for i in range(nr):
            x = refs[pos][...].astype(F32)
            if halo[i]:
                xm, xp = _shift_rows(x, refs[pos + 1], refs[pos + 2], t, ctx_tiles, nt)
                rp.append(list(zip(_split(x, parts_r[i]), _split(xm, parts_r[i]), _split(xp, parts_r[i]))))
                pos += 3
            else:
                rp.append(_split(x, parts_r[i]))
                pos += 1
        vp = []
        for j in range(nv):
            vp.append(_split(refs[pos][...].astype(F32), parts_v[j]))
            pos += 1
        return rp, vp, refs[pos:]

    n_out, n_red = len(outs), len(reds)

    def fwd_call(*rv):
        def body(*refs):
            t = pl.program_id(0)
            rp, vp, rest = load(refs, t)
            o_parts, r_parts = fn(rp, vp)
            for k in range(n_out):
                rest[k][...] = _cat(o_parts[k]).astype(outs[k][1])
            for k in range(n_red):
                ref = rest[n_out + k]

                @pl.when(t == 0)
                def _():
                    ref[...] = r_parts[k]

                @pl.when(t > 0)
                def _():
                    ref[...] += r_parts[k]

        res = pl.pallas_call(
            body, grid=(nt,),
            in_specs=row_specs() + [vec_spec(j) for j in range(nv)],
            out_specs=[pl.BlockSpec((tr, o[0]), lambda t: (t, 0)) for o in outs]
            + [pl.BlockSpec((1, w), lambda t: (0, 0)) for w in reds],
            out_shape=[jax.ShapeDtypeStruct((r_total, o[0]), o[1]) for o in outs]
            + [jax.ShapeDtypeStruct((1, w), F32) for w in reds],
            compiler_params=_params(("arbitrary",)), name=name + "_f",
        )(*row_args(rv), *rv[nr:])
        return tuple(res)

    def bwd_call(rv, cts):
        def body(*refs):
            t = pl.program_id(0)
            rp, vp, rest = load(refs, t)
            ct_o = [_split(rest[k][...].astype(F32), outs[k][2]) for k in range(n_out)]
            ct_r = [rest[n_out + k][...] for k in range(n_red)]
            rest = rest[n_out + n_red:]
            _, vjp = jax.vjp(fn, rp, vp)
            d_rp, d_vp = vjp((ct_o, ct_r))
            pos = 0
            for i in range(nr):
                if halo[i]:
                    for c in range(3):
                        rest[pos + c][...] = _cat([p[c] for p in d_rp[i]])
                    pos += 3
                else:
                    rest[pos][...] = _cat(d_rp[i]).astype(rows[i].dtype)
                    pos += 1
            for j in range(nv):
                ref, val = rest[pos + j], _cat(d_vp[j])
                start = jnp.logical_or(t == 0, t == ctx_tiles) if seg[j] else t == 0

                @pl.when(start)
                def _():
                    ref[...] = val

                @pl.when(jnp.logical_not(start))
                def _():
                    ref[...] += val

        d_row_specs, d_row_shapes = [], []
        for i, r in enumerate(rows):
            w = r.shape[1]
            for _ in range(3 if halo[i] else 1):
                d_row_specs.append(pl.BlockSpec((tr, w), tile_map(i)))
                d_row_shapes.append(jax.ShapeDtypeStruct(r.shape, F32 if halo[i] else r.dtype))
        res = pl.pallas_call(
            body, grid=(nt,),
            in_specs=row_specs() + [vec_spec(j) for j in range(nv)]
            + [pl.BlockSpec((tr, o[0]), lambda t: (t, 0)) for o in outs]
            + [pl.BlockSpec((1, w), lambda t: (0, 0)) for w in reds],
            out_specs=d_row_specs + [vec_spec(j) for j in range(nv)],
            out_shape=d_row_shapes + [jax.ShapeDtypeStruct(v.shape, F32) for v in vecs],
            compiler_params=_params(("arbitrary",)), name=name + "_b",
        )(*row_args(rv), *rv[nr:], *cts)
        d_rows, pos = [], 0
        for i in range(nr):
            if halo[i]:
                d_rows.append(_unshift(res[pos], res[pos + 1], res[pos + 2], ctx_tiles * tr).astype(rows[i].dtype))
                pos += 3
            else:
                d_rows.append(res[pos])
                pos += 1
        return tuple(d_rows) + tuple(res[pos:])

    @jax.custom_vjp
    def f(*rv):
        return fwd_call(*rv)

    f.defvjp(lambda *rv: (fwd_call(*rv), rv), lambda rv, cts: bwd_call(rv, cts))
    return f(*rows, *vecs)


def _unshift(d, dm, dp, ctx_rows):
    r = d.shape[0]
    t = lax.broadcasted_iota(jnp.int32, (r, 1), 0)
    zero = jnp.zeros((1, d.shape[1]), d.dtype)
    from_m = jnp.concatenate([dm[1:], zero], axis=0)
    from_p = jnp.concatenate([zero, dp[:-1]], axis=0)
    from_m = jnp.where(t == ctx_rows - 1, 0.0, from_m)
    from_p = jnp.where(t == ctx_rows, 0.0, from_p)
    return d + from_m + from_p


def _sigmoid(x):
    return 0.5 * (jnp.tanh(0.5 * x) + 1.0)


def _silu(x):
    return x * _sigmoid(x)


def _softplus(x):
    return jnp.maximum(x, 0.0) + jnp.log(1.0 + jnp.exp(-jnp.abs(x)))


def _rms(x, g):
    return x * lax.rsqrt(jnp.mean(x * x, axis=-1, keepdims=True) + EPS) * g


def _fn_modulate(r, v):
    (x,), (g,), (sh,), (sc,) = r[0], v[0], v[1], v[2]
    return [[_rms(x, g) * (1.0 + sc) + sh]], []


def _fn_resmod(r, v):
    (x,), (y,) = r
    (gate,), (g,), (sh,), (sc,) = v
    xn = x + gate * y
    return [[xn], [_rms(xn, g) * (1.0 + sc) + sh]], []


def _fn_final(r, v):
    (x,), (y,), (tgt,) = r
    (gate,), (g,), (counts,) = v
    err = _rms(x + gate * y, g) - tgt
    row_loss = jnp.mean(err * err, axis=-1, keepdims=True)
    total = 0.5 * jnp.sum(row_loss, axis=0, keepdims=True)
    return [], [total * counts]


def _fn_conv(r, v):
    (x, xm, xp), = r[0]
    (w0,), (w1,), (w2,) = v
    return [[xm * w0 + x * w1 + xp * w2]], []


def _fn_conv_taps(r, v):
    (dy,), ((x, xm, xp),) = r
    col = lambda a: jnp.sum(dy * a, axis=0, keepdims=True)
    return [], [col(xm), col(x), col(xp)]


def conv3(name, x, w0, w1, w2, ctx_tiles, out_dtype):
    width = x.shape[1]

    def conv(tag, x, taps, dtype):
        return rowwise(name + tag, _fn_conv, [x], list(taps), halo=(True,), outs=[(width, dtype, 1)], ctx_tiles=ctx_tiles)[0]

    @jax.custom_vjp
    def f(x, w0, w1, w2):
        return conv("_y", x, (w0, w1, w2), out_dtype)

    def bwd(res, dy):
        x, w0, w1, w2 = res
        taps = rowwise(name + "_dw", _fn_conv_taps, [dy, x], [], halo=(False, True), reds=[width] * 3, ctx_tiles=ctx_tiles)
        return (conv("_dx", dy, (w2, w1, w0), x.dtype),) + tuple(taps)

    f.defvjp(lambda x, w0, w1, w2: (conv("_y", x, (w0, w1, w2), out_dtype), (x, w0, w1, w2)), bwd)
    return f(x, w0, w1, w2)


def _fn_dnprep(r, v):
    qkv, (ab,) = r
    (alog,), (dtb,) = v
    out = [[], [], []]
    for n, x in enumerate(qkv):
        which = n // DN_HEADS
        y = _silu(x)
        if which < 2:
            y = y * lax.rsqrt(jnp.sum(y * y, axis=-1, keepdims=True) + EPS)
        if which == 0:
            y = y * (HEAD_DIM ** -0.5)
        out[which].append(y)
    lane = lax.broadcasted_iota(jnp.int32, ab.shape, 1)
    g = -jnp.exp(alog) * _softplus(ab + dtb)
    gb = jnp.where(lane < N_DIRHEAD, g, jnp.where(lane < 2 * N_DIRHEAD, _sigmoid(ab), 0.0))
    return out + [[gb]], []


def _fn_dnpost(r, v):
    of, ob, z = r
    (g,) = v[0]
    return [[_rms(a + b, g) * _silu(c) for a, b, c in zip(of, ob, z)]], []


def _fn_sub(r, v):
    return [[r[0][0] - r[1][0]]], []


def _fn_scale(r, v):
    return [[r[0][0] * v[0][0]]], []


def _fn_shortconv(r, v):
    (xin, gb, gc), = r
    (w0,), (w1,), (w2,) = v
    u, um, up = (gc[k] * xin[k] for k in range(3))
    return [[gb[0] * (um * w0 + u * w1 + up * w2)]], []


def _fn_merge(r, v):
    gates, (ya,), (yb,), (yc,) = r
    return [[_sigmoid(gates[0]) * ya + _sigmoid(gates[1]) * yb + _sigmoid(gates[2]) * yc]], []


def _fn_swiglu(r, v):
    gate, up = r[0]
    return [[_silu(gate) * up]], []


def _dot3(a, b):
    (ah, al), (bh, bl) = _hi_lo(a), _hi_lo(b)
    dot = lambda x, y: lax.dot_general(x, y, (NN, ((), ())), preferred_element_type=F32)
    return dot(ah, bh) + (dot(ah, bl) + dot(al, bh))


def _bdot(a, b, dims):
    (ca,), (cb,) = dims
    return lax.dot_general(a.astype(MXU_DTYPE), b.astype(MXU_DTYPE), (((ca + 1,), (cb + 1,)), ((0,), (0,))),
                           preferred_element_type=F32)


def _hi_lo(a):
    hi = a.astype(MXU_DTYPE)
    return hi, (a - hi.astype(F32)).astype(MXU_DTYPE)


def _bdot3_raw(a, b, dims):
    (ah, al), (bh, bl) = _hi_lo(a), _hi_lo(b)
    return _bdot(ah, bh, dims) + (_bdot(ah, bl, dims) + _bdot(al, bh, dims))


@jax.custom_vjp
def _bdot3(a, b):
    return _bdot3_raw(a, b, NN)


_bdot3.defvjp(lambda a, b: (_bdot3_raw(a, b, NN), (a, b)),
              lambda res, ct: (_bdot3_raw(ct, res[1], NT), _bdot3_raw(res[0], ct, TN)))


def _inv_doubling(a):
    c = a.shape[-1]
    ii, jj = (lax.broadcasted_iota(jnp.int32, a.shape, d) for d in (1, 2))
    t = jnp.where(ii == jj, 1.0, 0.0) - a
    p = _bdot3_raw(a, a, NN)
    for _ in range(int(math.log2(c)) - 2):
        both = _bdot3_raw(jnp.concatenate([p, t], axis=1), p, NN)
        t = t + both[:, c:]
        p = both[:, :c]
    return t + _bdot3_raw(t, p, NN)


def _dn_gates(k4, gb):
    nb, c = N_DIRHEAD, k4.shape[1]
    k = jnp.concatenate([k4, k4], axis=0)
    lane = lax.broadcasted_iota(jnp.int32, gb.shape, 1)
    col = lambda j: jnp.sum(jnp.where(lane == j, gb, 0.0), axis=1, keepdims=True)
    g_col = jnp.concatenate([col(j)[None] for j in range(nb)], axis=0)
    b_col = jnp.concatenate([col(nb + j)[None] for j in range(nb)], axis=0)
    bi, ii, jj = (lax.broadcasted_iota(jnp.int32, (nb, c, c), a) for a in range(3))
    ahead = jnp.where(bi >= DN_HEADS, jj - ii, ii - jj)
    incl = ahead >= 0
    g_row = jnp.sum(jnp.where(ahead == 0, g_col, 0.0), axis=1, keepdims=True)
    gc_col = jnp.sum(jnp.where(incl, g_row, 0.0), axis=2, keepdims=True)
    gc_row = jnp.sum(jnp.where(ahead <= 0, g_col, 0.0), axis=1, keepdims=True)
    decay = jnp.where(incl, jnp.exp(jnp.where(incl, gc_col - gc_row, 0.0)), 0.0)
    return k, g_col, b_col, gc_col, decay, ahead > 0


def _dn_a(k4, gb):
    k, _, b_col, _, decay, strict = _dn_gates(k4, gb)
    return _bdot(k * b_col, k, NT) * jnp.where(strict, decay, 0.0)


def _dn_operands(q4, k4, v4, gb, t):
    k, g_col, b_col, gc_col, decay, _ = _dn_gates(k4, gb)
    q, v = (jnp.concatenate([a, a], axis=0) for a in (q4, v4))
    e_gc = jnp.exp(gc_col)
    d = k.shape[-1]
    uw = _bdot3(t, jnp.concatenate([v * b_col, k * b_col * e_gc], axis=2))
    u, w = uw[:, :, :d], uw[:, :, d:]
    g_last = jnp.sum(g_col, axis=1, keepdims=True)
    k_state = k * jnp.exp(g_last - gc_col)
    a_qk = _bdot(q, k, NT) * decay
    return u, w, q * e_gc, k_state, a_qk, jnp.broadcast_to(jnp.exp(g_last), (N_DIRHEAD, 1, LANES))


def _dn_step(s, u, w, qd, ks, aqk, gl):
    c = u.shape[1]
    on_state = _bdot(jnp.concatenate([w, qd], axis=1), s, NN)
    v_new = u - on_state[:, :c]
    o = on_state[:, c:] + _bdot(aqk, v_new, NN)
    return s * gl[:, :, :1] + _bdot(ks, v_new, TN), o


def _heads(x):
    return jnp.concatenate([x[None, :, h * HEAD_DIM:(h + 1) * HEAD_DIM] for h in range(DN_HEADS)], axis=0)


def _unheads(x):
    return jnp.concatenate([x[h] for h in range(x.shape[0])], axis=-1)


def _dn_rev(t, nc, n):
    return jnp.where(t < nc, nc - 1 - t, n - 1 - (t - nc))


PRE_CHUNKS = 2


def _pre_shapes(n):
    c, d, h = DN_CHUNK, HEAD_DIM, DN_HEADS
    shapes = [(n, h, c, d)] * 8 + [(n, h, c, c)] * 2 + [(n, h, 1, LANES)] * 2
    return shapes, [pl.BlockSpec((PRE_CHUNKS,) + s[1:], lambda t: (t, 0, 0, 0)) for s in shapes]


def _inverse_spec(n):
    shape = (n, N_DIRHEAD, DN_CHUNK, DN_CHUNK)
    return shape, pl.BlockSpec((PRE_CHUNKS,) + shape[1:], lambda t: (t, 0, 0, 0))


def _pre_row_specs():
    rows = PRE_CHUNKS * DN_CHUNK
    return pl.BlockSpec((rows, DN_HEADS * HEAD_DIM), lambda t: (t, 0)), pl.BlockSpec((rows, LANES), lambda t: (t, 0))


def _dn_pre_fwd(name, q, k, v, gb):
    n = q.shape[0] // DN_CHUNK
    shapes, specs = _pre_shapes(n)
    t_shape, t_spec = _inverse_spec(n)

    def body(q_ref, k_ref, v_ref, g_ref, *o_refs):
        for s in range(PRE_CHUNKS):
            rows = pl.ds(s * DN_CHUNK, DN_CHUNK)
            q4, k4, v4, gb_ = _heads(q_ref[rows, :]), _heads(k_ref[rows, :]), _heads(v_ref[rows, :]), g_ref[rows, :]
            t = _inv_doubling(_dn_a(k4, gb_))
            for i, r in enumerate(_dn_operands(q4, k4, v4, gb_, t)):
                o_refs[2 * i][s] = r[:DN_HEADS]
                o_refs[2 * i + 1][s] = r[DN_HEADS:]
            o_refs[-1][s] = t

    wide, narrow = _pre_row_specs()
    return pl.pallas_call(body, grid=(n // PRE_CHUNKS,), in_specs=[wide] * 3 + [narrow], out_specs=specs + [t_spec],
                          out_shape=[jax.ShapeDtypeStruct(s, F32) for s in shapes + [t_shape]],
                          compiler_params=_params(("parallel",)), name=name + "_pre_f")(q, k, v, gb)


def _dn_pre_bwd(name, q, k, v, gb, inv, cts):
    n = q.shape[0] // DN_CHUNK
    _, specs = _pre_shapes(n)
    _, t_spec = _inverse_spec(n)
    n_ct = len(specs)

    def body(q_ref, k_ref, v_ref, g_ref, t_ref, *refs):
        for s in range(PRE_CHUNKS):
            rows = pl.ds(s * DN_CHUNK, DN_CHUNK)
            ct = tuple(jnp.concatenate([refs[i][s], refs[i + 1][s]], axis=0) for i in range(0, n_ct, 2))
            q4, k4, v4, gb_, t = _heads(q_ref[rows, :]), _heads(k_ref[rows, :]), _heads(v_ref[rows, :]), g_ref[rows, :], t_ref[s]
            _, vjp = jax.vjp(_dn_operands, q4, k4, v4, gb_, t)
            dq, dk, dv, dg, dt = vjp(ct)
            da = -_bdot3_raw(_bdot3_raw(t, dt, TN), t, NT)
            _, vjp_a = jax.vjp(_dn_a, k4, gb_)
            dk_a, dg_a = vjp_a(da)
            for r, val in zip(refs[n_ct:], (_unheads(dq), _unheads(dk + dk_a), _unheads(dv), dg + dg_a)):
                r[rows, :] = val

    wide, narrow = _pre_row_specs()
    return pl.pallas_call(body, grid=(n // PRE_CHUNKS,), in_specs=[wide] * 3 + [narrow, t_spec] + specs,
                          out_specs=[wide] * 3 + [narrow],
                          out_shape=[jax.ShapeDtypeStruct(q.shape, F32)] * 3 + [jax.ShapeDtypeStruct(gb.shape, F32)],
                          compiler_params=_params(("parallel",)), name=name + "_pre_b")(q, k, v, gb, inv, *cts)


SCAN_CHUNKS = 2


def _scan_specs(pre_shapes, fw, bw):
    maps = (lambda t: (fw(t), 0, 0, 0), lambda t: (bw(t), 0, 0, 0))
    return [pl.BlockSpec((SCAN_CHUNKS,) + s[1:], maps[i % 2]) for i, s in enumerate(pre_shapes)]


def _dn_scan_fwd(name, pre, nc):
    n = pre[0].shape[0]
    r = n * DN_CHUNK
    hd, nh, nb, sc = HEAD_DIM, DN_HEADS, N_DIRHEAD, SCAN_CHUNKS
    assert n % sc == 0 and nc % sc == 0
    shapes, _ = _pre_shapes(n)
    n_in = len(shapes)

    def body(*refs):
        ins, (of_ref, ob_ref, sf_ref, sb_ref, s_scr) = refs[:n_in], refs[n_in:]
        t = pl.program_id(0)

        @pl.when(t == 0)
        def _():
            s_scr[...] = jnp.zeros_like(s_scr)

        s = s_scr[...]
        for j in range(sc):
            jf, jb = j, sc - 1 - j
            sf_ref[jf] = s[:nh]
            sb_ref[jb] = s[nh:]
            args = [jnp.concatenate([ins[i][jf], ins[i + 1][jb]], axis=0) for i in range(0, n_in, 2)]
            s, o = _dn_step(s, *args)
            of_ref[pl.ds(jf * DN_CHUNK, DN_CHUNK), :] = _unheads(o[:nh])
            ob_ref[pl.ds(jb * DN_CHUNK, DN_CHUNK), :] = _unheads(o[nh:])
        s_scr[...] = s

    fw = lambda t: t
    bw = lambda t: _dn_rev(sc * t + sc - 1, nc, n) // sc
    wide, st = (sc * DN_CHUNK, nh * hd), (sc, nh, hd, hd)
    return pl.pallas_call(
        body, grid=(n // sc,), in_specs=_scan_specs(shapes, fw, bw),
        out_specs=[pl.BlockSpec(wide, lambda t: (fw(t), 0)), pl.BlockSpec(wide, lambda t: (bw(t), 0)),
                   pl.BlockSpec(st, lambda t: (fw(t), 0, 0, 0)), pl.BlockSpec(st, lambda t: (bw(t), 0, 0, 0))],
        out_shape=[jax.ShapeDtypeStruct((r, nh * hd), F32)] * 2 + [jax.ShapeDtypeStruct((n, nh, hd, hd), F32)] * 2,
        scratch_shapes=[pltpu.VMEM((nb, hd, hd), F32)],
        compiler_params=_params(("arbitrary",)), name=name + "_scan_f",
    )(*pre)


def _dn_scan_bwd(name, pre, sall_f, sall_b, do_f, do_b, nc):
    n = pre[0].shape[0]
    hd, nh, nb, sc = HEAD_DIM, DN_HEADS, N_DIRHEAD, SCAN_CHUNKS
    shapes, _ = _pre_shapes(n)
    n_in = len(shapes)

    def body(*refs):
        ins, (sf_ref, sb_ref, dof_ref, dob_ref) = refs[:n_in], refs[n_in:n_in + 4]
        outs, ds_scr = refs[n_in + 4:2 * n_in + 4], refs[2 * n_in + 4]
        t = pl.program_id(0)

        @pl.when(t == 0)
        def _():
            ds_scr[...] = jnp.zeros_like(ds_scr)

        ds = ds_scr[...]
        for j in range(sc):
            jf, jb = sc - 1 - j, j
            args = [jnp.concatenate([ins[i][jf], ins[i + 1][jb]], axis=0) for i in range(0, n_in, 2)]
            s = jnp.concatenate([sf_ref[jf], sb_ref[jb]], axis=0)
            do = jnp.concatenate([_heads(dof_ref[pl.ds(jf * DN_CHUNK, DN_CHUNK), :]),
                                  _heads(dob_ref[pl.ds(jb * DN_CHUNK, DN_CHUNK), :])], axis=0)
            _, vjp = jax.vjp(_dn_step, s, *args)
            cts = vjp((ds, do))
            ds = cts[0]
            for i, ct in enumerate(cts[1:]):
                outs[2 * i][jf] = ct[:nh]
                outs[2 * i + 1][jb] = ct[nh:]
        ds_scr[...] = ds

    fw = lambda t: n // sc - 1 - t
    bw = lambda t: _dn_rev(n - 1 - sc * t, nc, n) // sc
    wide, st = (sc * DN_CHUNK, nh * hd), (sc, nh, hd, hd)
    return tuple(pl.pallas_call(
        body, grid=(n // sc,),
        in_specs=_scan_specs(shapes, fw, bw)
        + [pl.BlockSpec(st, lambda t: (fw(t), 0, 0, 0)), pl.BlockSpec(st, lambda t: (bw(t), 0, 0, 0)),
           pl.BlockSpec(wide, lambda t: (fw(t), 0)), pl.BlockSpec(wide, lambda t: (bw(t), 0))],
        out_specs=_scan_specs(shapes, fw, bw), out_shape=[jax.ShapeDtypeStruct(s, F32) for s in shapes],
        scratch_shapes=[pltpu.VMEM((nb, hd, hd), F32)],
        compiler_params=_params(("arbitrary",)), name=name + "_scan_b",
    )(*pre, sall_f, sall_b, do_f, do_b))


def deltanet(name, q, k, v, gb, ctx_rows):
    nc = ctx_rows // DN_CHUNK

    @jax.custom_vjp
    def pre(q, k, v, gb):
        return tuple(_dn_pre_fwd(name, q, k, v, gb)[:-1])

    def pre_fwd(*a):
        *ops, inv = _dn_pre_fwd(name, *a)
        return tuple(ops), (a, inv)

    pre.defvjp(pre_fwd, lambda res, cts: tuple(_dn_pre_bwd(name, *res[0], res[1], cts)))

    @jax.custom_vjp
    def scan(*ops):
        return tuple(_dn_scan_fwd(name, ops, nc)[:2])

    def scan_fwd(*ops):
        of, ob, sf, sb = _dn_scan_fwd(name, ops, nc)
        return (of, ob), (ops, sf, sb)

    scan.defvjp(scan_fwd, lambda res, cts: _dn_scan_bwd(name, res[0], res[1], res[2], cts[0], cts[1], nc))
    return scan(*pre(q, k, v, gb))


def _box_matrix(l, w):
    lo, hi = w // 2, w - 1 - w // 2
    pos = np.arange(l)
    start, end = np.clip(pos - lo, 0, l), np.clip(pos + hi + 1, 0, l)
    col = np.arange(l)[None, :]
    return ((col >= start[:, None]) & (col < end[:, None])) / (end - start)[:, None].astype(np.float64)


def _pool_matrices(ctx_rows, grid_rows):
    assert ctx_rows == ROW_TILE and ROW_TILE % GRID_W == 0
    ctx = np.stack([_box_matrix(ctx_rows, w) for w in POOL_WINDOWS])
    cols = np.stack([np.kron(np.eye(ROW_TILE // GRID_W), _box_matrix(GRID_W, w)) for w in POOL_WINDOWS])
    rows = np.stack([_box_matrix(grid_rows, w) for w in POOL_WINDOWS])[None]
    return np.stack([ctx, cols]).astype(np.float32), rows.astype(np.float32)


def _pool_apply(name, x, mats, group_w, seg_tiles, lane_tile):
    r, w = x.shape
    b = mats.shape[-1]
    ng = mats.shape[1]
    period = ng * group_w

    def body(x_ref, m_ref, o_ref):
        xv = x_ref[...].astype(F32)
        lane = lax.broadcasted_iota(jnp.int32, xv.shape, 1)
        grp = (lane % period) // group_w
        acc = jnp.zeros_like(xv)
        for g in range(ng):
            acc = acc + jnp.where(grp == g, _dot3(m_ref[g], xv), 0.0)
        o_ref[...] = acc

    return pl.pallas_call(
        body, grid=(r // b, w // lane_tile),
        in_specs=[pl.BlockSpec((b, lane_tile), lambda i, j: (i, j)),
                  pl.BlockSpec((None, ng, b, b), lambda i, j: (jnp.where(i >= seg_tiles, 1, 0) if mats.shape[0] > 1 else 0, 0, 0, 0))],
        out_specs=pl.BlockSpec((b, lane_tile), lambda i, j: (i, j)),
        out_shape=jax.ShapeDtypeStruct(x.shape, F32),
        compiler_params=_params(("parallel", "parallel")), name=name,
    )(x, mats)


def pool_means(name, u, ctx_rows):
    r, pw = u.shape
    grid_rows = (r - ctx_rows) // GRID_W
    gw = pw // len(POOL_WINDOWS)
    m1, m2 = _pool_matrices(ctx_rows, grid_rows)
    lane_tile = min(2048, GRID_W * pw)

    def apply(x, a1, a2, tag):
        y = _pool_apply(name + tag + "1", x, jnp.asarray(a1), gw, ctx_rows // ROW_TILE, pw)
        lat = y[ctx_rows:].reshape(grid_rows, GRID_W * pw)
        lat = _pool_apply(name + tag + "2", lat, jnp.asarray(a2), gw, 0, lane_tile)
        return jnp.concatenate([y[:ctx_rows], lat.reshape(r - ctx_rows, pw)], axis=0)

    @jax.custom_vjp
    def f(u):
        return apply(u, m1, m2, "_f")

    tr = lambda m: np.ascontiguousarray(np.swapaxes(m, -1, -2))
    f.defvjp(lambda u: (apply(u, m1, m2, "_f"), None), lambda _, ct: (apply(ct, tr(m1), tr(m2), "_b").astype(u.dtype),))
    return f(u)


def _ew(name, fn, *xs, n_out=1):
    shapes = jax.eval_shape(lambda *a: fn(*a), *xs)
    shapes = shapes if isinstance(shapes, (tuple, list)) else (shapes,)

    def body(*refs):
        res = fn(*[r[...] for r in refs[:len(xs)]])
        res = res if isinstance(res, (tuple, list)) else (res,)
        for r, o in zip(res, refs[len(xs):]):
            o[...] = r

    out = pl.pallas_call(body, out_shape=[jax.ShapeDtypeStruct(s.shape, s.dtype) for s in shapes],
                         compiler_params=_params(), name=name)(*xs)
    return out[0] if len(shapes) == 1 else tuple(out)


def _adamw_math(w, g, m, v):
    m2 = ADAM_B1 * m + (1.0 - ADAM_B1) * g
    v2 = ADAM_B2 * v + (1.0 - ADAM_B2) * (g * g)
    m_hat = m2 / (1.0 - ADAM_B1 ** ADAM_STEP)
    v_hat = v2 / (1.0 - ADAM_B2 ** ADAM_STEP)
    delta = -ADAM_LR * (m_hat / (jnp.sqrt(v_hat) + ADAM_EPS) + ADAM_WD * w)
    return delta, m2, v2


def adamw(name, w, g, m, v):
    *lead, r, c = w.shape
    tr = _pick(r, (256, 128, 64, 32, 16, 8))

    def body(w_ref, g_ref, m_ref, v_ref, d_ref, m2_ref, v2_ref):
        d_ref[...], m2_ref[...], v2_ref[...] = _adamw_math(w_ref[...], g_ref[...], m_ref[...], v_ref[...])

    if lead:
        grid, spec = (lead[0], r // tr), pl.BlockSpec((None, tr, c), lambda l, i: (l, i, 0))
    else:
        grid, spec = (r // tr,), pl.BlockSpec((tr, c), lambda i: (i, 0))
    return pl.pallas_call(body, grid=grid, in_specs=[spec] * 4, out_specs=[spec] * 3,
                          out_shape=[jax.ShapeDtypeStruct(w.shape, F32)] * 3,
                          compiler_params=_params(("parallel",) * len(grid)), name=name)(w, g, m, v)


def _sum_rows(name, xs, out_dtype=F32):
    r, c = xs[0].shape
    tr = _pick(r, tuple(t for t in (512, 256, 128, 64, 32, 16, 8) if t * c * 4 <= 2 << 20))

    def body(*refs):
        acc = refs[0][...].astype(F32)
        for ref in refs[1:-1]:
            acc = acc + ref[...].astype(F32)
        refs[-1][...] = acc.astype(out_dtype)

    spec = pl.BlockSpec((tr, c), lambda i: (i, 0))
    return pl.pallas_call(body, grid=(r // tr,), in_specs=[spec] * len(xs), out_specs=spec,
                          out_shape=jax.ShapeDtypeStruct((r, c), out_dtype),
                          compiler_params=_params(("parallel",)), name=name)(*xs)


def _place():
    return lax.axis_index("x"), lax.axis_index("y"), lax.axis_index("c")


def _chip_peers(x, y, c):
    return [(1 - x, y, c), (x, 1 - y, c), (1 - x, 1 - y, c)]


def all_gather8(name, block):
    m_per, n = block.shape

    def body(x_ref, out_ref, send_sems, recv_sems, local_sem):
        x, y, c = _place()
        me, sibling = (x, y, c), (x, y, 1 - c)
        chips = [(1 - x, y), (x, 1 - y), (1 - x, 1 - y)]

        def rows(px, py, pc):
            return out_ref.at[pl.ds((4 * px + 2 * py + pc) * m_per, m_per), :]

        def copy(k, blk, to, src=None):
            return pltpu.make_async_remote_copy(
                src_ref=rows(*blk) if src is None else src, dst_ref=rows(*blk),
                send_sem=send_sems.at[k], recv_sem=recv_sems.at[k], device_id=to, device_id_type=MESH_ID)

        mine = pltpu.make_async_copy(x_ref, rows(*me), local_sem)
        mine.start()
        first = [copy(0, me, sibling, src=x_ref)]
        first += [copy(1 + j, me, (*chip, c), src=x_ref) for j, chip in enumerate(chips)]
        for cp in first:
            cp.start()
        passed = [copy(4 + j, (*chip, c), sibling) for j, chip in enumerate(chips)]
        for j, chip in enumerate(chips):
            copy(1 + j, (*chip, c), me).wait_recv()
            passed[j].start()
        copy(0, sibling, me).wait_recv()
        for j, chip in enumerate(chips):
            copy(4 + j, (*chip, 1 - c), me).wait_recv()
        for cp in first + passed:
            cp.wait_send()
        mine.wait()

    return pl.pallas_call(
        body, out_shape=jax.ShapeDtypeStruct((8 * m_per, n), block.dtype),
        in_specs=[pl.BlockSpec(memory_space=pltpu.VMEM)], out_specs=pl.BlockSpec(memory_space=pltpu.VMEM),
        scratch_shapes=[pltpu.SemaphoreType.DMA((7,)), pltpu.SemaphoreType.DMA((7,)), pltpu.SemaphoreType.DMA],
        compiler_params=_params(), name=name,
    )(block)


N_PEERS = 3


def gather_chips(name, shards):
    n = len(shards)

    def body(*refs):
        xs, outs, (send_sems, recv_sems, local_sems) = refs[:n], refs[n:2 * n], refs[2 * n:]
        x, y, c = _place()
        local, remote = [], []
        for i in range(n):
            local.append(pltpu.make_async_copy(xs[i].at[c], outs[i].at[2 * x + y], local_sems.at[i]))
            local[-1].start()
            for p, peer in enumerate(_chip_peers(x, y, c)):
                remote.append(pltpu.make_async_remote_copy(
                    src_ref=xs[i].at[c], dst_ref=outs[i].at[2 * x + y], send_sem=send_sems.at[i * N_PEERS + p],
                    recv_sem=recv_sems.at[i * N_PEERS + p], device_id=peer, device_id_type=MESH_ID))
                remote[-1].start()
        for i in range(n):
            for p, (px, py, _) in enumerate(_chip_peers(x, y, c)):
                pltpu.make_async_remote_copy(
                    src_ref=xs[i].at[c], dst_ref=outs[i].at[2 * px + py], send_sem=send_sems.at[i * N_PEERS + p],
                    recv_sem=recv_sems.at[i * N_PEERS + p], device_id=(px, py, c), device_id_type=MESH_ID).wait_recv()
        for cp in remote:
            cp.wait_send()
        for cp in local:
            cp.wait()

    hbm = pl.BlockSpec(memory_space=pltpu.HBM)
    return pl.pallas_call(
        body, out_shape=[jax.ShapeDtypeStruct((N_CHIPS,) + s.shape[1:], s.dtype) for s in shards],
        in_specs=[hbm] * n, out_specs=[hbm] * n,
        scratch_shapes=[pltpu.SemaphoreType.DMA((n * N_PEERS,)), pltpu.SemaphoreType.DMA((n * N_PEERS,)),
                        pltpu.SemaphoreType.DMA((n,))],
        compiler_params=_params(), name=name,
    )(*shards)


def swap_sibling(name, blocks):
    n = len(blocks)
    pairs = [tuple(b) if isinstance(b, (tuple, list)) else (b,) for b in blocks]
    flat = [a for p in pairs for a in p]
    first = [sum(len(q) for q in pairs[:i]) for i in range(n)]

    def body(*refs):
        xs, outs, (send_sems, recv_sems) = refs[:len(flat)], refs[len(flat):len(flat) + n], refs[len(flat) + n:]
        x, y, c = _place()

        def copy(i, src):
            return pltpu.make_async_remote_copy(src_ref=src, dst_ref=outs[i], send_sem=send_sems.at[i], recv_sem=recv_sems.at[i],
                                                device_id=(x, y, 1 - c), device_id_type=MESH_ID)

        for i, p in enumerate(pairs):
            if len(p) == 1:
                copy(i, xs[first[i]]).start()
            else:
                for half in range(2):
                    pl.when(c == 1 - half)(copy(i, xs[first[i] + half]).start)
        for i in range(n):
            copy(i, xs[first[i]]).wait()

    hbm = pl.BlockSpec(memory_space=pltpu.HBM)
    return pl.pallas_call(
        body, out_shape=[jax.ShapeDtypeStruct(p[0].shape, p[0].dtype) for p in pairs],
        in_specs=[hbm] * len(flat), out_specs=[hbm] * n,
        scratch_shapes=[pltpu.SemaphoreType.DMA((n,)), pltpu.SemaphoreType.DMA((n,))],
        compiler_params=_params(), name=name,
    )(*flat)


def scatter_chips(name, pieces):
    n = len(pieces)

    def body(*refs):
        xs, outs, (send_sems, recv_sems) = refs[:n], refs[n:2 * n], refs[2 * n:]
        x, y, c = _place()
        copies = []
        for i in range(n):
            for p, (px, py, pc) in enumerate(_chip_peers(x, y, c)):
                copies.append(pltpu.make_async_remote_copy(
                    src_ref=xs[i].at[2 * px + py], dst_ref=outs[i].at[p], send_sem=send_sems.at[i * N_PEERS + p],
                    recv_sem=recv_sems.at[i * N_PEERS + p], device_id=(px, py, pc), device_id_type=MESH_ID))
                copies[-1].start()
        for cp in copies:
            cp.wait()

    hbm = pl.BlockSpec(memory_space=pltpu.HBM)
    return pl.pallas_call(
        body, out_shape=[jax.ShapeDtypeStruct((N_PEERS,) + p.shape[1:], p.dtype) for p in pieces],
        in_specs=[hbm] * n, out_specs=[hbm] * n,
        scratch_shapes=[pltpu.SemaphoreType.DMA((n * N_PEERS,)), pltpu.SemaphoreType.DMA((n * N_PEERS,))],
        compiler_params=_params(), name=name,
    )(*pieces)


def pair_sum(name, half0, half1, got, out_dtype):
    r, w = got.shape
    tr = _pick(r, tuple(t for t in (512, 256, 128, 64, 32, 16, 8) if t * w * 4 <= 2 << 20))

    def body(a0, a1, g, o):
        mine = jnp.where(lax.axis_index("c") == 0, a0[...], a1[...])
        o[...] = (mine.astype(F32) + g[...].astype(F32)).astype(out_dtype)

    spec = pl.BlockSpec((tr, w), lambda i: (i, 0))
    return pl.pallas_call(body, grid=(r // tr,), in_specs=[spec] * 3, out_specs=spec,
                          out_shape=jax.ShapeDtypeStruct((r, w), out_dtype),
                          compiler_params=_params(("parallel",)), name=name)(half0, half1, got)


def _sum_devices(name, got, fold=False):
    def body(a_ref, *o_refs):
        acc = a_ref[0]
        for i in range(1, N_DEV):
            acc = acc + a_ref[i]
        o_refs[0][...] = acc
        if fold:
            o_refs[1][...] = acc + pltpu.roll(acc, SUBLANES // 2, 0)

    shape = jax.ShapeDtypeStruct(got.shape[1:], F32)
    out = pl.pallas_call(body, out_shape=[shape] * (2 if fold else 1), compiler_params=_params(), name=name)(got)
    return out if fold else out[0]


def _w_in_bounds(d, pw, scw):
    off_z = 3 * DN_WIDTH
    off_a = off_z + DN_WIDTH
    off_pool = off_a + 2 * N_DIRHEAD
    off_sc = off_pool + pw
    off_gate = off_sc + 3 * scw
    return (0, off_z, off_a, off_pool, off_sc, off_gate, off_gate + 3 * d)


def _misc_widths(pw, scw):
    return (DN_WIDTH, pw, 3 * scw, LANES)


def split_matrices(full, pw, scw):
    depth, d = len(full["w_in"]), full["w_in"][0].shape[0]
    b = _w_in_bounds(d, pw, scw)
    out = {k: [] for k in ("qkv", "gate", "misc", "gu", "br_a", "br_b", "br_c", "o", "down")}
    for l in range(depth):
        w = full["w_in"][l]
        out["qkv"].append(w[:, b[0]:b[1]])
        out["gate"].append(w[:, b[5]:b[6]])
        out["misc"].append(jnp.concatenate([w[:, b[1]:b[2]], w[:, b[3]:b[5]], _pad_lanes(w[:, b[2]:b[3]])], axis=1))
        out["gu"].append(full["w_gu"][l])
        for k in ("br_a", "br_b", "br_c", "o", "down"):
            out[k].append(full["w_" + k][l])
    return out


def join_matrix_grads(g, pw, scw):
    depth = len(g["qkv"])
    n_z, n_rest, n_ab = DN_WIDTH, pw + 3 * scw, 2 * N_DIRHEAD
    w_in = []
    for l in range(depth):
        m = g["misc"][l]
        w_in.append(jnp.concatenate([g["qkv"][l], m[:, :n_z], m[:, n_z + n_rest:n_z + n_rest + n_ab],
                                     m[:, n_z:n_z + n_rest], g["gate"][l]], axis=1))
    out = {"w_in": w_in, "w_gu": g["gu"]}
    for k in ("br_a", "br_b", "br_c", "o", "down"):
        out["w_" + k] = g[k]
    return {k: [a.astype(MXU_DTYPE) for a in v] for k, v in out.items()}


def split_cols(x, widths):
    edges = np.cumsum((0,) + tuple(widths))

    def cut(x):
        return tuple(x[:, a:b] for a, b in zip(edges[:-1], edges[1:]))

    f = jax.custom_vjp(cut)
    f.defvjp(lambda x: (cut(x), None), lambda _, cts: (jnp.concatenate(cts, axis=1),))
    return f(x)


def _row(v):
    return v.reshape(1, -1)


def _pad_lanes(v, width=LANES):
    return jnp.pad(v, ((0, 0), (0, width - v.shape[1])))


def _block_diag(blocks):
    g, n, _ = blocks.shape
    out = jnp.zeros((g * n, g * n), blocks.dtype)
    for i in range(g):
        out = out.at[i * n:(i + 1) * n, i * n:(i + 1) * n].set(blocks[i])
    return out


def local_loss(p, carriers, x, mod_lat, mod_ctx, wts, ctx, target):
    ctx_rows, d = ctx.shape
    depth = len(wts["qkv"])
    pw, scw = p["pool_scale"].shape[1], p["sc_conv_w"].shape[2]
    ff = wts["down"][0].shape[0]
    ct = ctx_rows // ROW_TILE
    xs = jnp.concatenate([ctx, x], axis=0)
    seg = lambda l, k: jnp.stack([mod_ctx[l, k], mod_lat[l, k]]).reshape(2, 1, d)
    dn = gate2 = None
    for l in range(depth):
        sh1, sc1, g1, sh2, sc2, g2 = (seg(l, k) for k in range(6))
        tag = f"l{l}_"
        lin = lambda key, a: mm(tag + key, a, wts[key][l], carriers[key][l])
        if l == 0:
            (h1,) = rowwise(tag + "mod", _fn_modulate, [xs], [_row(p["norm1_g"][l]), sh1, sc1], seg=(False, True, True),
                            outs=[(d, MXU_DTYPE, 1)], ctx_tiles=ct)
        else:
            xs, h1 = rowwise(tag + "resmod1", _fn_resmod, [xs, dn], [gate2, _row(p["norm1_g"][l]), sh1, sc1],
                             seg=(True, False, True, True), outs=[(d, F32, 1), (d, MXU_DTYPE, 1)], ctx_tiles=ct)
        fan = ("qkv", "gate", "misc")
        p_qkv, p_gate, p_misc = mm_fanout(tag + "in", h1, [wts[k][l] for k in fan], [carriers[k][l] for k in fan])
        p_z, p_pool, p_sc, p_ab = split_cols(p_misc, _misc_widths(pw, scw))
        cw = p["dn_conv_w"][l]
        qkv_conv = conv3(tag + "dnconv", p_qkv, cw[0:1], cw[1:2], cw[2:3], ct, F32)
        q, k, v, gb = rowwise(
            tag + "dnprep", _fn_dnprep, [qkv_conv, p_ab],
            [_pad_lanes(p["dn_a_log"][l].reshape(1, -1)), _pad_lanes(p["dn_dt_bias"][l].reshape(1, -1))],
            parts_r=(3 * DN_HEADS, 1), outs=[(DN_WIDTH, F32, DN_HEADS)] * 3 + [(LANES, F32, 1)], ctx_tiles=ct)
        o_f, o_b = deltanet(tag + "dn", q, k, v, gb, ctx_rows)
        (oz,) = rowwise(tag + "dnpost", _fn_dnpost, [o_f, o_b, p_z], [_row(p["dn_norm_g"][l])], parts_r=(DN_HEADS,) * 3,
                        outs=[(DN_WIDTH, MXU_DTYPE, DN_HEADS)], ctx_tiles=ct)
        y_a = lin("br_a", oz)
        means = pool_means(tag + "box", p_pool, ctx_rows)
        (dpool,) = rowwise(tag + "poolsub", _fn_sub, [means, p_pool], [], outs=[(pw, MXU_DTYPE, 1)], long_tiles=True)
        pool_mat = _block_diag(p["pool_w"][l])
        mixed = mm(tag + "poolw", dpool, pool_mat.astype(MXU_DTYPE), pool_mat)
        (yb_in,) = rowwise(tag + "poolscale", _fn_scale, [mixed], [_row(p["pool_scale"][l])], outs=[(pw, MXU_DTYPE, 1)],
                           long_tiles=True)
        y_b = lin("br_b", yb_in)
        sw = p["sc_conv_w"][l]
        (yc_in,) = rowwise(tag + "sconv", _fn_shortconv, [p_sc], [sw[0:1], sw[1:2], sw[2:3]], halo=(True,), parts_r=(3,),
                           outs=[(scw, MXU_DTYPE, 1)], ctx_tiles=ct)
        y_c = lin("br_c", yc_in)
        (y,) = rowwise(tag + "merge", _fn_merge, [p_gate, y_a, y_b, y_c], [], parts_r=(3, 1, 1, 1),
                       outs=[(d, MXU_DTYPE, 1)], ctx_tiles=ct)
        mix = lin("o", y)
        xs, h2 = rowwise(tag + "resmod2", _fn_resmod, [xs, mix], [g1, _row(p["norm2_g"][l]), sh2, sc2],
                         seg=(True, False, True, True), outs=[(d, F32, 1), (d, MXU_DTYPE, 1)], ctx_tiles=ct)
        (act,) = rowwise(tag + "swiglu", _fn_swiglu, [lin("gu", h2)], [], parts_r=(2,),
                         outs=[(ff, MXU_DTYPE, 1)], ctx_tiles=ct)
        dn = lin("down", act)
        gate2 = g2
    counts = jnp.concatenate([jnp.zeros((1, 1, LANES), F32), jnp.ones((1, 1, LANES), F32)])
    (total,) = rowwise("final", _fn_final, [xs, dn, target], [gate2, _row(p["final_norm_g"]), counts],
                       seg=(True, False, True), reds=[LANES], ctx_tiles=ct, skip=(0, 0, ct))
    return total[0, 0]


BIG = ("w_in", "w_br_a", "w_br_b", "w_br_c", "w_o", "w_gu", "w_down")
ROW_SHARDED = ("w_o", "w_down")
SMALL_REPL = ("norm1_g", "norm2_g", "dn_a_log", "dn_dt_bias", "dn_norm_g", "pool_w", "pool_scale", "final_norm_g")
SMALL_SHARD = ("dn_conv_w", "sc_conv_w")
WEIGHTS = ("c_ctx", "w_ada", "b_ada", "norm1_g", "norm2_g", "w_in", "dn_conv_w", "dn_a_log", "dn_dt_bias", "dn_norm_g", "pool_w",
           "pool_scale", "sc_conv_w", "w_br_a", "w_br_b", "w_br_c", "w_o", "w_gu", "w_down", "final_norm_g")
N_CHIPS, N_DEV = 4, 8
COMM_COLS = 1024


def _pack(arrays, rows_multiple, cols=COMM_COLS, dtype=F32):
    size = sum(int(np.prod(a.shape)) for a in arrays)
    rows = -(-size // (cols * rows_multiple)) * rows_multiple
    tail = [jnp.zeros((rows * cols - size,), dtype)] if rows * cols > size else []
    return jnp.concatenate([a.astype(dtype).reshape(-1) for a in arrays] + tail).reshape(rows, cols)


def _unpack(flat, shapes):
    out, pos = [], 0
    for s in shapes:
        n = int(np.prod(s))
        out.append(flat[pos:pos + n].reshape(s))
        pos += n
    return out


def _join_shards(parts, name):
    _, a, b = parts.shape
    if name in ROW_SHARDED:
        return parts.reshape(N_CHIPS * a, b)
    return jnp.moveaxis(parts, 0, 1).reshape(a, N_CHIPS * b)


def _cut_shards(full, name):
    k, n = full.shape
    if name in ROW_SHARDED:
        return full.reshape(N_CHIPS, k // N_CHIPS, n)
    return jnp.moveaxis(full.reshape(k, N_CHIPS, n // N_CHIPS), 1, 0)


def _dsilu(x):
    s = _sigmoid(x)
    return s + x * s * (1.0 - s)


def kernel(x, c, ctx, c_ctx, w_ada, b_ada, norm1_g, norm2_g, w_in, dn_conv_w, dn_a_log, dn_dt_bias, dn_norm_g, pool_w, pool_scale, sc_conv_w, w_br_a, w_br_b, w_br_c, w_o, w_gu, w_down, final_norm_g, loss_target, m_c_ctx, m_w_ada, m_b_ada, m_norm1_g, m_norm2_g, m_w_in, m_dn_conv_w, m_dn_a_log, m_dn_dt_bias, m_dn_norm_g, m_pool_w, m_pool_scale, m_sc_conv_w, m_w_br_a, m_w_br_b, m_w_br_c, m_w_o, m_w_gu, m_w_down, m_final_norm_g, v_c_ctx, v_w_ada, v_b_ada, v_norm1_g, v_norm2_g, v_w_in, v_dn_conv_w, v_dn_a_log, v_dn_dt_bias, v_dn_norm_g, v_pool_w, v_pool_scale, v_sc_conv_w, v_w_br_a, v_w_br_b, v_w_br_c, v_w_o, v_w_gu, v_w_down, v_final_norm_g):
    given = dict(locals())
    ix, iy, ic = _place()
    chip, dev = 2 * ix + iy, 4 * ix + 2 * iy + ic
    d = x.shape[-1]
    depth = w_in.shape[0]
    ada_cols = w_ada.shape[2]
    spare = 2 * SUBLANES - N_DEV - 1

    got0 = all_gather8("gather_cond", _pack([c, dn_conv_w, sc_conv_w], SUBLANES)).reshape(N_DEV, -1)
    c_all = got0[:, :d]
    taps = [_unpack(got0[2 * j, d:], [dn_conv_w.shape, sc_conv_w.shape]) for j in range(N_CHIPS)]
    full = {"dn_conv_w": jnp.concatenate([t[0] for t in taps], axis=-1),
            "sc_conv_w": jnp.concatenate([t[1] for t in taps], axis=-1)}
    assert depth == 2, "the two cores of a chip split the layers between them"
    got_layer = gather_chips("gather_weights", [given[n].astype(MXU_DTYPE) for n in BIG])
    other_layer = swap_sibling("swap_weights", got_layer)
    mats = {n: [_join_shards(jnp.where(ic == l, got_layer[i], other_layer[i]), n) for l in range(depth)]
            for i, n in enumerate(BIG)}
    for n in SMALL_REPL:
        full[n] = given[n]
    pw, scw = pool_scale.shape[1], full["sc_conv_w"].shape[2]
    wts = split_matrices(mats, pw, scw)
    carriers = {k: [jnp.zeros(w.shape, F32) for w in ws] for k, ws in wts.items()}

    cond = jnp.concatenate([c_all, c_ctx[None], jnp.zeros((spare, d), F32)])
    s_cond = _ew("silu_cond", _silu, cond)
    mod_cols = jnp.concatenate([_mm(f"ada{l}_f", s_cond, w_ada[l], NN) for l in range(depth)], axis=1)
    mod_got = all_gather8("gather_mod", mod_cols).reshape(N_DEV, 2 * SUBLANES, depth, ada_cols)
    mod_all = jnp.concatenate([mod_got[2 * j] for j in range(N_CHIPS)], axis=-1) + b_ada[None]
    mod_lat = lax.dynamic_index_in_dim(mod_all, dev, 0, keepdims=False).reshape(depth, 6, d)
    mod_ctx = mod_all[N_DEV].reshape(depth, 6, d)

    loss_local, (g_full, g_mats, grad_x, g_mod_lat, g_mod_ctx) = jax.value_and_grad(local_loss, argnums=(0, 1, 2, 3, 4))(
        full, carriers, x[0], mod_lat, mod_ctx, wts, ctx[0], loss_target[0])
    g_mats_full = join_matrix_grads(g_mats, pw, scw)
    loss = lax.psum(loss_local, ("x", "y", "c"))

    dmod_cols = (2 * depth * 6 * d) // SUBLANES
    dmod = all_gather8("gather_dmod", _pack([g_mod_lat, g_mod_ctx], SUBLANES, cols=dmod_cols))
    dmod = dmod.reshape(N_DEV, SUBLANES, dmod_cols)
    dmod_sum, dmod_fold = _sum_devices("reduce_dmod", dmod, fold=True)
    half_rows = SUBLANES // 2
    grad_b_ada = dmod_fold[:half_rows].reshape(depth, 6 * d)
    dctx_sum = dmod_sum[half_rows:].reshape(1, depth, 6 * d)
    d9 = jnp.concatenate([dmod[:, :half_rows].reshape(N_DEV, depth, 6 * d), dctx_sum, jnp.zeros((spare, depth, 6 * d), F32)])
    d9 = lax.dynamic_slice_in_dim(d9, chip * ada_cols, ada_cols, axis=2)
    grad_w_ada = jnp.stack([_mm(f"ada{l}_dw", s_cond, d9[:, l], TN) for l in range(depth)])
    ds_cond = [_mm(f"ada{l}_da", d9[:, l], w_ada[l], NT) for l in range(depth)]
    dsilu_part = _sum_rows("sum_dcond", ds_cond)[N_DEV]

    small_names = SMALL_REPL + SMALL_SHARD
    small_grads = [dsilu_part] + [g_full[n] for n in small_names]
    small_shapes = [a.shape for a in small_grads]
    n_small = sum(int(np.prod(s)) for s in small_shapes)
    cols = -(-n_small // (SUBLANES * LANES)) * LANES
    got2 = all_gather8("gather_small", _pack(small_grads, SUBLANES, cols=cols)).reshape(N_DEV, SUBLANES, cols)
    small_sum = _unpack(_sum_devices("reduce_small", got2).reshape(-1), small_shapes)
    grads = dict(zip(small_names, small_sum[1:]))
    grads["c_ctx"] = _ew("dsilu", lambda g, z: 0.5 * g * _dsilu(z), _row(small_sum[0]), _row(c_ctx)).reshape(-1)
    grads["w_ada"], grads["b_ada"] = grad_w_ada, grad_b_ada
    for n in SMALL_SHARD:
        width = given[n].shape[-1]
        grads[n] = lax.dynamic_slice_in_dim(grads[n], chip * width, width, axis=-1)

    from_sibling = swap_sibling("swap_layers", [tuple(g_mats_full[n]) for n in BIG])
    pieces = [_cut_shards(pair_sum("sum_pair_" + n, *g_mats_full[n], from_sibling[i], MXU_DTYPE), n) for i, n in enumerate(BIG)]
    from_chips = scatter_chips("scatter_pieces", pieces)
    reduced = []
    for i, n in enumerate(BIG):
        own = lax.dynamic_index_in_dim(pieces[i], chip, 0, keepdims=False)
        a, b = own.shape
        view = lambda t: t.reshape(-1, b)
        reduced.append(_sum_rows("sum_chips_" + n, [view(own)] + [view(from_chips[i][p]) for p in range(N_PEERS)]).reshape(a, b))
    other_reduced = swap_sibling("swap_reduced", reduced)
    for i, n in enumerate(BIG):
        grads[n] = jnp.stack([jnp.where(ic == l, reduced[i], other_reduced[i]) for l in range(depth)])

    delta, new_m, new_v = {}, {}, {}
    large = BIG + ("w_ada",)
    for n in large:
        delta[n], new_m[n], new_v[n] = adamw("adamw_" + n, given[n], grads[n], given["m_" + n], given["v_" + n])
    rest = [n for n in WEIGHTS if n not in large]
    rest_shapes = [given[n].shape for n in rest]
    packed = [_pack([src[pre + n] for n in rest], SUBLANES, cols=LANES)
              for src, pre in ((given, ""), (grads, ""), (given, "m_"), (given, "v_"))]
    for res, o in zip((delta, new_m, new_v), adamw("adamw_small", *packed)):
        for n, a in zip(rest, _unpack(o.reshape(-1), rest_shapes)):
            res[n] = a
    return (loss, grad_x[None], *[grads[n] for n in WEIGHTS], *[delta[n] for n in WEIGHTS],
            *[new_m[n] for n in WEIGHTS], *[new_v[n] for n in WEIGHTS])
```

```python
import functools
import math

import numpy as np
import jax
import jax.numpy as jnp
from jax import lax
from jax.experimental import pallas as pl
from jax.experimental.pallas import tpu as pltpu

F32 = jnp.float32
MXU_DTYPE = jnp.bfloat16
HIGHEST = lax.Precision.HIGHEST

DN_HEADS = 4
HEAD_DIM = 128
DN_WIDTH = DN_HEADS * HEAD_DIM
DN_CHUNK = 64
GRID_W = 64
EPS = 1e-6
POOL_WINDOWS = (2, 4, 8, 16)
N_DIRHEAD = 2 * DN_HEADS
ADAM_LR, ADAM_B1, ADAM_B2, ADAM_EPS, ADAM_WD, ADAM_STEP = 0.001, 0.9, 0.999, 1e-08, 0.01, 10

LANES = 128
SUBLANES = 8
ROW_TILE = 256
VMEM_LIMIT = 56 * 1024 * 1024

MESH_ID = pl.DeviceIdType.MESH
NN, NT, TN = ((1,), (0,)), ((1,), (1,)), ((0,), (0,))


def _params(sem=None):
    return pltpu.CompilerParams(dimension_semantics=sem, vmem_limit_bytes=VMEM_LIMIT)


def _pick(n, cands):
    for c in cands:
        if c <= n and n % c == 0:
            return c
    return n


def _mm(name, a, b, dims, out_dtype=F32):
    if dims == NN:
        (m, kk), (_, n) = a.shape, b.shape
    elif dims == NT:
        (m, kk), (n, _) = a.shape, b.shape
    else:
        (kk, m), (_, n) = a.shape, b.shape
    tm = _pick(m, (768, 1024, 1408, 512, 256, 128, 64, 32, 16, 8))
    tn = _pick(n, ((3072, 2816) if dims == NN else ()) + (1536, 1024, 1408, 1664, 768, 896, 512, 256, 128))
    tk = kk if dims == NN and kk <= 2816 else _pick(kk, (1024, 1408, 768, 512, 256, 128))
    gi, gj, gl = m // tm, n // tn, kk // tk
    if dims == NN:
        a_spec = pl.BlockSpec((tm, tk), lambda i, j, l: (i, l))
        b_spec = pl.BlockSpec((tk, tn), lambda i, j, l: (l, j))
    elif dims == NT:
        a_spec = pl.BlockSpec((tm, tk), lambda i, j, l: (i, l))
        b_spec = pl.BlockSpec((tn, tk), lambda i, j, l: (j, l))
    else:
        a_spec = pl.BlockSpec((tk, tm), lambda i, j, l: (l, i))
        b_spec = pl.BlockSpec((tk, tn), lambda i, j, l: (l, j))
    direct = gl == 1
    use_acc = (not direct) and out_dtype != F32

    def body(a_ref, b_ref, o_ref, *scratch):
        part = lax.dot_general(a_ref[...].astype(MXU_DTYPE), b_ref[...].astype(MXU_DTYPE), (dims, ((), ())),
                               preferred_element_type=F32)
        if direct:
            o_ref[...] = part.astype(out_dtype)
            return
        acc = scratch[0] if use_acc else o_ref
        l = pl.program_id(2)

        @pl.when(l == 0)
        def _():
            acc[...] = part

        @pl.when(l > 0)
        def _():
            acc[...] += part

        if use_acc:
            @pl.when(l == gl - 1)
            def _():
                o_ref[...] = acc[...].astype(out_dtype)

    return pl.pallas_call(
        body, grid=(gi, gj, gl), in_specs=[a_spec, b_spec],
        out_specs=pl.BlockSpec((tm, tn), lambda i, j, l: (i, j)),
        out_shape=jax.ShapeDtypeStruct((m, n), out_dtype),
        scratch_shapes=[pltpu.VMEM((tm, tn), F32)] if use_acc else [],
        compiler_params=_params(("parallel", "parallel", "arbitrary")), name=name,
    )(a, b)


def mm(name, a, w, carrier):
    @jax.custom_vjp
    def f(a, w, carrier):
        return _mm(name + "_f", a, w, NN, out_dtype=MXU_DTYPE)

    def fwd(a, w, carrier):
        return f(a, w, carrier), (a, w)

    def bwd(res, dc):
        a, w = res
        da = _mm_nt_sum(name + "_da", [dc], [w], a.dtype)
        dw = _mm(name + "_dw", a, dc, TN)
        return da, None, dw

    f.defvjp(fwd, bwd)
    return f(a, w, carrier)


def _mm_nt_sum(name, dcs, ws, out_dtype):
    m, kk = dcs[0].shape[0], ws[0].shape[0]
    tm = _pick(m, (768, 1024, 512, 256, 128, 64, 32, 16, 8))
    tns = [_pick(w.shape[1], (2816, 1664, 1536, 1408, 1024, 768, 896, 512, 256, 128)) for w in ws]
    counts = [w.shape[1] // tn for w, tn in zip(ws, tns)]
    starts = [sum(counts[:g]) for g in range(len(ws))]
    steps = sum(counts)

    def col(g):
        return lambda i, t: jnp.clip(t - starts[g], 0, counts[g] - 1)

    def body(*refs):
        dc_refs, w_refs, o_ref, acc = refs[:len(ws)], refs[len(ws):2 * len(ws)], refs[-2], refs[-1]
        t = pl.program_id(1)

        @pl.when(t == 0)
        def _():
            acc[...] = jnp.zeros_like(acc)

        for g in range(len(ws)):
            @pl.when(jnp.logical_and(t >= starts[g], t < starts[g] + counts[g]))
            def _():
                acc[...] += lax.dot_general(dc_refs[g][...].astype(MXU_DTYPE), w_refs[g][...].astype(MXU_DTYPE),
                                            (NT, ((), ())), preferred_element_type=F32)

        @pl.when(t == steps - 1)
        def _():
            o_ref[...] = acc[...].astype(out_dtype)

    dc_specs = [pl.BlockSpec((tm, tns[g]), (lambda c: lambda i, t: (i, c(i, t)))(col(g))) for g in range(len(ws))]
    w_specs = [pl.BlockSpec((kk, tns[g]), (lambda c: lambda i, t: (0, c(i, t)))(col(g))) for g in range(len(ws))]
    return pl.pallas_call(
        body, grid=(m // tm, steps), in_specs=dc_specs + w_specs,
        out_specs=pl.BlockSpec((tm, kk), lambda i, t: (i, 0)),
        out_shape=jax.ShapeDtypeStruct((m, kk), out_dtype),
        scratch_shapes=[pltpu.VMEM((tm, kk), F32)],
        compiler_params=_params(("parallel", "arbitrary")), name=name,
    )(*dcs, *ws)


def mm_fanout(name, a, ws, carriers):
    n = len(ws)

    @jax.custom_vjp
    def f(a, ws, carriers):
        return tuple(_mm(f"{name}{g}_f", a, ws[g], NN, out_dtype=MXU_DTYPE) for g in range(n))

    def fwd(a, ws, carriers):
        return f(a, ws, carriers), (a, ws)

    def bwd(res, dcs):
        a, ws = res
        da = _mm_nt_sum(name + "_da", list(dcs), list(ws), a.dtype)
        dws = tuple(_mm(f"{name}{g}_dw", a, dcs[g], TN) for g in range(n))
        return da, None, dws

    f.defvjp(fwd, bwd)
    return f(a, tuple(ws), tuple(carriers))


def _split(x, parts):
    w = x.shape[-1] // parts
    return [x[:, k * w:(k + 1) * w] for k in range(parts)]


def _cat(xs):
    return xs[0] if len(xs) == 1 else jnp.concatenate(xs, axis=-1)


def _shift_rows(x, prev_ref, next_ref, i, ctx_tiles, nt):
    tr = x.shape[0]
    rid = lax.broadcasted_iota(jnp.int32, x.shape, 0)
    first = jnp.logical_or(i == 0, i == ctx_tiles)
    last = jnp.logical_or(i == ctx_tiles - 1, i == nt - 1)
    prow = jnp.where(first, 0.0, prev_ref[SUBLANES - 1:SUBLANES, :].astype(F32))
    nrow = jnp.where(last, 0.0, next_ref[0:1, :].astype(F32))
    xm = jnp.where(rid == 0, prow, pltpu.roll(x, 1, 0))
    xp = jnp.where(rid == tr - 1, nrow, pltpu.roll(x, tr - 1, 0))
    return xm, xp


def rowwise(name, fn, rows, vecs, *, halo=(), seg=(), parts_r=None, parts_v=None, outs=(), reds=(), ctx_tiles=1, skip=None,
            long_tiles=0):
    nr, nv = len(rows), len(vecs)
    halo = tuple(halo) or (False,) * nr
    seg = tuple(seg) or (False,) * nv
    skip = tuple(skip or (0,) * nr)
    parts_r = tuple(parts_r or (1,) * nr)
    parts_v = tuple(parts_v or (1,) * nv)
    r_total = rows[0].shape[0]
    tr = _pick(r_total, (long_tiles, 768)) if long_tiles else ROW_TILE
    assert not long_tiles or not (any(halo) or any(seg) or any(skip))
    nt = r_total // tr
    assert r_total % tr == 0 and (ctx_tiles > 0 or not any(seg)) and not any(h and s for h, s in zip(halo, skip))

    def tile_map(i):
        return lambda t: (jnp.maximum(t - skip[i], 0), 0)

    def row_specs():
        sp = []
        for i, r in enumerate(rows):
            w = r.shape[1]
            sp.append(pl.BlockSpec((tr, w), tile_map(i)))
            if halo[i]:
                k = tr // SUBLANES
                sp.append(pl.BlockSpec((SUBLANES, w), lambda t: (jnp.maximum(t * k - 1, 0), 0)))
                sp.append(pl.BlockSpec((SUBLANES, w), lambda t: (jnp.minimum((t + 1) * k, nt * k - 1), 0)))
        return sp

    def vec_spec(j):
        w = vecs[j].shape[-1]
        if seg[j]:
            return pl.BlockSpec((None, 1, w), lambda t: (jnp.where(t >= ctx_tiles, 1, 0), 0, 0))
        return pl.BlockSpec((1, w), lambda t: (0, 0))

    def row_args(rv):
        a = []
        for i in range(nr):
            a += [rv[i]] * 3 if halo[i] else [rv[i]]
        return a

    def load(refs, t):
        pos, rp = 0, []
        for i in range(nr):
            x = refs[pos][...].astype(F32)
            if halo[i]:
                xm, xp = _shift_rows(x, refs[pos + 1], refs[pos + 2], t, ctx_tiles, nt)
                rp.append(list(zip(_split(x, parts_r[i]), _split(xm, parts_r[i]), _split(xp, parts_r[i]))))
                pos += 3
            else:
                rp.append(_split(x, parts_r[i]))
                pos += 1
        vp = []
        for j in range(nv):
            vp.append(_split(refs[pos][...].astype(F32), parts_v[j]))
            pos += 1
        return rp, vp, refs[pos:]

    n_out, n_red = len(outs), len(reds)

    def fwd_call(*rv):
        def body(*refs):
            t = pl.program_id(0)
            rp, vp, rest = load(refs, t)
            o_parts, r_parts = fn(rp, vp)
            for k in range(n_out):
                rest[k][...] = _cat(o_parts[k]).astype(outs[k][1])
            for k in range(n_red):
                ref = rest[n_out + k]

                @pl.when(t == 0)
                def _():
                    ref[...] = r_parts[k]

                @pl.when(t > 0)
                def _():
                    ref[...] += r_parts[k]

        res = pl.pallas_call(
            body, grid=(nt,),
            in_specs=row_specs() + [vec_spec(j) for j in range(nv)],
            out_specs=[pl.BlockSpec((tr, o[0]), lambda t: (t, 0)) for o in outs]
            + [pl.BlockSpec((1, w), lambda t: (0, 0)) for w in reds],
            out_shape=[jax.ShapeDtypeStruct((r_total, o[0]), o[1]) for o in outs]
            + [jax.ShapeDtypeStruct((1, w), F32) for w in reds],
            compiler_params=_params(("arbitrary",)), name=name + "_f",
        )(*row_args(rv), *rv[nr:])
        return tuple(res)

    def bwd_call(rv, cts):
        def body(*refs):
            t = pl.program_id(0)
            rp, vp, rest = load(refs, t)
            ct_o = [_split(rest[k][...].astype(F32), outs[k][2]) for k in range(n_out)]
            ct_r = [rest[n_out + k][...] for k in range(n_red)]
            rest = rest[n_out + n_red:]
            _, vjp = jax.vjp(fn, rp, vp)
            d_rp, d_vp = vjp((ct_o, ct_r))
            pos = 0
            for i in range(nr):
                if halo[i]:
                    for c in range(3):
                        rest[pos + c][...] = _cat([p[c] for p in d_rp[i]])
                    pos += 3
                else:
                    rest[pos][...] = _cat(d_rp[i]).astype(rows[i].dtype)
                    pos += 1
            for j in range(nv):
                ref, val = rest[pos + j], _cat(d_vp[j])
                start = jnp.logical_or(t == 0, t == ctx_tiles) if seg[j] else t == 0

                @pl.when(start)
                def _():
                    ref[...] = val

                @pl.when(jnp.logical_not(start))
                def _():
                    ref[...] += val

        d_row_specs, d_row_shapes = [], []
        for i, r in enumerate(rows):
            w = r.shape[1]
            for _ in range(3 if halo[i] else 1):
                d_row_specs.append(pl.BlockSpec((tr, w), tile_map(i)))
                d_row_shapes.append(jax.ShapeDtypeStruct(r.shape, F32 if halo[i] else r.dtype))
        res = pl.pallas_call(
            body, grid=(nt,),
            in_specs=row_specs() + [vec_spec(j) for j in range(nv)]
            + [pl.BlockSpec((tr, o[0]), lambda t: (t, 0)) for o in outs]
            + [pl.BlockSpec((1, w), lambda t: (0, 0)) for w in reds],
            out_specs=d_row_specs + [vec_spec(j) for j in range(nv)],
            out_shape=d_row_shapes + [jax.ShapeDtypeStruct(v.shape, F32) for v in vecs],
            compiler_params=_params(("arbitrary",)), name=name + "_b",
        )(*row_args(rv), *rv[nr:], *cts)
        d_rows, pos = [], 0
        for i in range(nr):
            if halo[i]:
                d_rows.append(_unshift(res[pos], res[pos + 1], res[pos + 2], ctx_tiles * tr).astype(rows[i].dtype))
                pos += 3
            else:
                d_rows.append(res[pos])
                pos += 1
        return tuple(d_rows) + tuple(res[pos:])

    @jax.custom_vjp
    def f(*rv):
        return fwd_call(*rv)

    f.defvjp(lambda *rv: (fwd_call(*rv), rv), lambda rv, cts: bwd_call(rv, cts))
    return f(*rows, *vecs)


def _unshift(d, dm, dp, ctx_rows):
    r = d.shape[0]
    t = lax.broadcasted_iota(jnp.int32, (r, 1), 0)
    zero = jnp.zeros((1, d.shape[1]), d.dtype)
    from_m = jnp.concatenate([dm[1:], zero], axis=0)
    from_p = jnp.concatenate([zero, dp[:-1]], axis=0)
    from_m = jnp.where(t == ctx_rows - 1, 0.0, from_m)
    from_p = jnp.where(t == ctx_rows, 0.0, from_p)
    return d + from_m + from_p


def _sigmoid(x):
    return 0.5 * (jnp.tanh(0.5 * x) + 1.0)


def _silu(x):
    return x * _sigmoid(x)


def _softplus(x):
    return jnp.maximum(x, 0.0) + jnp.log(1.0 + jnp.exp(-jnp.abs(x)))


def _rms(x, g):
    return x * lax.rsqrt(jnp.mean(x * x, axis=-1, keepdims=True) + EPS) * g


def _fn_modulate(r, v):
    (x,), (g,), (sh,), (sc,) = r[0], v[0], v[1], v[2]
    return [[_rms(x, g) * (1.0 + sc) + sh]], []


def _fn_resmod(r, v):
    (x,), (y,) = r
    (gate,), (g,), (sh,), (sc,) = v
    xn = x + gate * y
    return [[xn], [_rms(xn, g) * (1.0 + sc) + sh]], []


def _fn_final(r, v):
    (x,), (y,), (tgt,) = r
    (gate,), (g,), (counts,) = v
    err = _rms(x + gate * y, g) - tgt
    row_loss = jnp.mean(err * err, axis=-1, keepdims=True)
    total = 0.5 * jnp.sum(row_loss, axis=0, keepdims=True)
    return [], [total * counts]


def _fn_conv(r, v):
    (x, xm, xp), = r[0]
    (w0,), (w1,), (w2,) = v
    return [[xm * w0 + x * w1 + xp * w2]], []


def _fn_conv_taps(r, v):
    (dy,), ((x, xm, xp),) = r
    col = lambda a: jnp.sum(dy * a, axis=0, keepdims=True)
    return [], [col(xm), col(x), col(xp)]


def conv3(name, x, w0, w1, w2, ctx_tiles, out_dtype):
    width = x.shape[1]

    def conv(tag, x, taps, dtype):
        return rowwise(name + tag, _fn_conv, [x], list(taps), halo=(True,), outs=[(width, dtype, 1)], ctx_tiles=ctx_tiles)[0]

    @jax.custom_vjp
    def f(x, w0, w1, w2):
        return conv("_y", x, (w0, w1, w2), out_dtype)

    def bwd(res, dy):
        x, w0, w1, w2 = res
        taps = rowwise(name + "_dw", _fn_conv_taps, [dy, x], [], halo=(False, True), reds=[width] * 3, ctx_tiles=ctx_tiles)
        return (conv("_dx", dy, (w2, w1, w0), x.dtype),) + tuple(taps)

    f.defvjp(lambda x, w0, w1, w2: (conv("_y", x, (w0, w1, w2), out_dtype), (x, w0, w1, w2)), bwd)
    return f(x, w0, w1, w2)


def _fn_dnprep(r, v):
    qkv, (ab,) = r
    (alog,), (dtb,) = v
    out = [[], [], []]
    for n, x in enumerate(qkv):
        which = n // DN_HEADS
        y = _silu(x)
        if which < 2:
            y = y * lax.rsqrt(jnp.sum(y * y, axis=-1, keepdims=True) + EPS)
        if which == 0:
            y = y * (HEAD_DIM ** -0.5)
        out[which].append(y)
    lane = lax.broadcasted_iota(jnp.int32, ab.shape, 1)
    g = -jnp.exp(alog) * _softplus(ab + dtb)
    gb = jnp.where(lane < N_DIRHEAD, g, jnp.where(lane < 2 * N_DIRHEAD, _sigmoid(ab), 0.0))
    return out + [[gb]], []


def _fn_dnpost(r, v):
    of, ob, z = r
    (g,) = v[0]
    return [[_rms(a + b, g) * _silu(c) for a, b, c in zip(of, ob, z)]], []


def _fn_sub(r, v):
    return [[r[0][0] - r[1][0]]], []


def _fn_scale(r, v):
    return [[r[0][0] * v[0][0]]], []


def _fn_shortconv(r, v):
    (xin, gb, gc), = r
    (w0,), (w1,), (w2,) = v
    u, um, up = (gc[k] * xin[k] for k in range(3))
    return [[gb[0] * (um * w0 + u * w1 + up * w2)]], []


def _fn_merge(r, v):
    gates, (ya,), (yb,), (yc,) = r
    return [[_sigmoid(gates[0]) * ya + _sigmoid(gates[1]) * yb + _sigmoid(gates[2]) * yc]], []


def _fn_swiglu(r, v):
    gate, up = r[0]
    return [[_silu(gate) * up]], []


def _dot3(a, b):
    (ah, al), (bh, bl) = _hi_lo(a), _hi_lo(b)
    dot = lambda x, y: lax.dot_general(x, y, (NN, ((), ())), preferred_element_type=F32)
    return dot(ah, bh) + (dot(ah, bl) + dot(al, bh))


def _bdot(a, b, dims):
    (ca,), (cb,) = dims
    return lax.dot_general(a.astype(MXU_DTYPE), b.astype(MXU_DTYPE), (((ca + 1,), (cb + 1,)), ((0,), (0,))),
                           preferred_element_type=F32)


def _hi_lo(a):
    hi = a.astype(MXU_DTYPE)
    return hi, (a - hi.astype(F32)).astype(MXU_DTYPE)


def _bdot3_raw(a, b, dims):
    (ah, al), (bh, bl) = _hi_lo(a), _hi_lo(b)
    return _bdot(ah, bh, dims) + (_bdot(ah, bl, dims) + _bdot(al, bh, dims))


@jax.custom_vjp
def _bdot3(a, b):
    return _bdot3_raw(a, b, NN)


_bdot3.defvjp(lambda a, b: (_bdot3_raw(a, b, NN), (a, b)),
              lambda res, ct: (_bdot3_raw(ct, res[1], NT), _bdot3_raw(res[0], ct, TN)))


def _inv_doubling(a):
    c = a.shape[-1]
    ii, jj = (lax.broadcasted_iota(jnp.int32, a.shape, d) for d in (1, 2))
    t = jnp.where(ii == jj, 1.0, 0.0) - a
    p = _bdot3_raw(a, a, NN)
    for _ in range(int(math.log2(c)) - 2):
        both = _bdot3_raw(jnp.concatenate([p, t], axis=1), p, NN)
        t = t + both[:, c:]
        p = both[:, :c]
    return t + _bdot3_raw(t, p, NN)


def _dn_gates(k4, gb):
    nb, c = N_DIRHEAD, k4.shape[1]
    k = jnp.concatenate([k4, k4], axis=0)
    lane = lax.broadcasted_iota(jnp.int32, gb.shape, 1)
    col = lambda j: jnp.sum(jnp.where(lane == j, gb, 0.0), axis=1, keepdims=True)
    g_col = jnp.concatenate([col(j)[None] for j in range(nb)], axis=0)
    b_col = jnp.concatenate([col(nb + j)[None] for j in range(nb)], axis=0)
    bi, ii, jj = (lax.broadcasted_iota(jnp.int32, (nb, c, c), a) for a in range(3))
    ahead = jnp.where(bi >= DN_HEADS, jj - ii, ii - jj)
    incl = ahead >= 0
    g_row = jnp.sum(jnp.where(ahead == 0, g_col, 0.0), axis=1, keepdims=True)
    gc_col = jnp.sum(jnp.where(incl, g_row, 0.0), axis=2, keepdims=True)
    gc_row = jnp.sum(jnp.where(ahead <= 0, g_col, 0.0), axis=1, keepdims=True)
    decay = jnp.where(incl, jnp.exp(jnp.where(incl, gc_col - gc_row, 0.0)), 0.0)
    return k, g_col, b_col, gc_col, decay, ahead > 0


def _dn_a(k4, gb):
    k, _, b_col, _, decay, strict = _dn_gates(k4, gb)
    return _bdot(k * b_col, k, NT) * jnp.where(strict, decay, 0.0)


def _dn_operands(q4, k4, v4, gb, t):
    k, g_col, b_col, gc_col, decay, _ = _dn_gates(k4, gb)
    q, v = (jnp.concatenate([a, a], axis=0) for a in (q4, v4))
    e_gc = jnp.exp(gc_col)
    d = k.shape[-1]
    uw = _bdot3(t, jnp.concatenate([v * b_col, k * b_col * e_gc], axis=2))
    u, w = uw[:, :, :d], uw[:, :, d:]
    g_last = jnp.sum(g_col, axis=1, keepdims=True)
    k_state = k * jnp.exp(g_last - gc_col)
    a_qk = _bdot(q, k, NT) * decay
    return u, w, q * e_gc, k_state, a_qk, jnp.broadcast_to(jnp.exp(g_last), (N_DIRHEAD, 1, LANES))


def _dn_step(s, u, w, qd, ks, aqk, gl):
    c = u.shape[1]
    on_state = _bdot(jnp.concatenate([w, qd], axis=1), s, NN)
    v_new = u - on_state[:, :c]
    o = on_state[:, c:] + _bdot(aqk, v_new, NN)
    return s * gl[:, :, :1] + _bdot(ks, v_new, TN), o


def _heads(x):
    return jnp.concatenate([x[None, :, h * HEAD_DIM:(h + 1) * HEAD_DIM] for h in range(DN_HEADS)], axis=0)


def _unheads(x):
    return jnp.concatenate([x[h] for h in range(x.shape[0])], axis=-1)


def _dn_rev(t, nc, n):
    return jnp.where(t < nc, nc - 1 - t, n - 1 - (t - nc))


PRE_CHUNKS = 2


def _pre_shapes(n):
    c, d, h = DN_CHUNK, HEAD_DIM, DN_HEADS
    shapes = [(n, h, c, d)] * 8 + [(n, h, c, c)] * 2 + [(n, h, 1, LANES)] * 2
    return shapes, [pl.BlockSpec((PRE_CHUNKS,) + s[1:], lambda t: (t, 0, 0, 0)) for s in shapes]


def _inverse_spec(n):
    shape = (n, N_DIRHEAD, DN_CHUNK, DN_CHUNK)
    return shape, pl.BlockSpec((PRE_CHUNKS,) + shape[1:], lambda t: (t, 0, 0, 0))


def _pre_row_specs():
    rows = PRE_CHUNKS * DN_CHUNK
    return pl.BlockSpec((rows, DN_HEADS * HEAD_DIM), lambda t: (t, 0)), pl.BlockSpec((rows, LANES), lambda t: (t, 0))


def _dn_pre_fwd(name, q, k, v, gb):
    n = q.shape[0] // DN_CHUNK
    shapes, specs = _pre_shapes(n)
    t_shape, t_spec = _inverse_spec(n)

    def body(q_ref, k_ref, v_ref, g_ref, *o_refs):
        for s in range(PRE_CHUNKS):
            rows = pl.ds(s * DN_CHUNK, DN_CHUNK)
            q4, k4, v4, gb_ = _heads(q_ref[rows, :]), _heads(k_ref[rows, :]), _heads(v_ref[rows, :]), g_ref[rows, :]
            t = _inv_doubling(_dn_a(k4, gb_))
            for i, r in enumerate(_dn_operands(q4, k4, v4, gb_, t)):
                o_refs[2 * i][s] = r[:DN_HEADS]
                o_refs[2 * i + 1][s] = r[DN_HEADS:]
            o_refs[-1][s] = t

    wide, narrow = _pre_row_specs()
    return pl.pallas_call(body, grid=(n // PRE_CHUNKS,), in_specs=[wide] * 3 + [narrow], out_specs=specs + [t_spec],
                          out_shape=[jax.ShapeDtypeStruct(s, F32) for s in shapes + [t_shape]],
                          compiler_params=_params(("parallel",)), name=name + "_pre_f")(q, k, v, gb)


def _dn_pre_bwd(name, q, k, v, gb, inv, cts):
    n = q.shape[0] // DN_CHUNK
    _, specs = _pre_shapes(n)
    _, t_spec = _inverse_spec(n)
    n_ct = len(specs)

    def body(q_ref, k_ref, v_ref, g_ref, t_ref, *refs):
        for s in range(PRE_CHUNKS):
            rows = pl.ds(s * DN_CHUNK, DN_CHUNK)
            ct = tuple(jnp.concatenate([refs[i][s], refs[i + 1][s]], axis=0) for i in range(0, n_ct, 2))
            q4, k4, v4, gb_, t = _heads(q_ref[rows, :]), _heads(k_ref[rows, :]), _heads(v_ref[rows, :]), g_ref[rows, :], t_ref[s]
            _, vjp = jax.vjp(_dn_operands, q4, k4, v4, gb_, t)
            dq, dk, dv, dg, dt = vjp(ct)
            da = -_bdot3_raw(_bdot3_raw(t, dt, TN), t, NT)
            _, vjp_a = jax.vjp(_dn_a, k4, gb_)
            dk_a, dg_a = vjp_a(da)
            for r, val in zip(refs[n_ct:], (_unheads(dq), _unheads(dk + dk_a), _unheads(dv), dg + dg_a)):
                r[rows, :] = val

    wide, narrow = _pre_row_specs()
    return pl.pallas_call(body, grid=(n // PRE_CHUNKS,), in_specs=[wide] * 3 + [narrow, t_spec] + specs,
                          out_specs=[wide] * 3 + [narrow],
                          out_shape=[jax.ShapeDtypeStruct(q.shape, F32)] * 3 + [jax.ShapeDtypeStruct(gb.shape, F32)],
                          compiler_params=_params(("parallel",)), name=name + "_pre_b")(q, k, v, gb, inv, *cts)


SCAN_CHUNKS = 4


def _scan_specs(pre_shapes, fw, bw):
    maps = (lambda t: (fw(t), 0, 0, 0), lambda t: (bw(t), 0, 0, 0))
    return [pl.BlockSpec((SCAN_CHUNKS,) + s[1:], maps[i % 2]) for i, s in enumerate(pre_shapes)]


def _dn_scan_fwd(name, pre, nc):
    n = pre[0].shape[0]
    r = n * DN_CHUNK
    hd, nh, nb, sc = HEAD_DIM, DN_HEADS, N_DIRHEAD, SCAN_CHUNKS
    assert n % sc == 0 and nc % sc == 0
    shapes, _ = _pre_shapes(n)
    n_in = len(shapes)

    def body(*refs):
        ins, (of_ref, ob_ref, sf_ref, sb_ref, s_scr) = refs[:n_in], refs[n_in:]
        t = pl.program_id(0)

        @pl.when(t == 0)
        def _():
            s_scr[...] = jnp.zeros_like(s_scr)

        s = s_scr[...]
        for j in range(sc):
            jf, jb = j, sc - 1 - j
            sf_ref[jf] = s[:nh]
            sb_ref[jb] = s[nh:]
            args = [jnp.concatenate([ins[i][jf], ins[i + 1][jb]], axis=0) for i in range(0, n_in, 2)]
            s, o = _dn_step(s, *args)
            of_ref[pl.ds(jf * DN_CHUNK, DN_CHUNK), :] = _unheads(o[:nh])
            ob_ref[pl.ds(jb * DN_CHUNK, DN_CHUNK), :] = _unheads(o[nh:])
        s_scr[...] = s

    fw = lambda t: t
    bw = lambda t: _dn_rev(sc * t + sc - 1, nc, n) // sc
    wide, st = (sc * DN_CHUNK, nh * hd), (sc, nh, hd, hd)
    return pl.pallas_call(
        body, grid=(n // sc,), in_specs=_scan_specs(shapes, fw, bw),
        out_specs=[pl.BlockSpec(wide, lambda t: (fw(t), 0)), pl.BlockSpec(wide, lambda t: (bw(t), 0)),
                   pl.BlockSpec(st, lambda t: (fw(t), 0, 0, 0)), pl.BlockSpec(st, lambda t: (bw(t), 0, 0, 0))],
        out_shape=[jax.ShapeDtypeStruct((r, nh * hd), F32)] * 2 + [jax.ShapeDtypeStruct((n, nh, hd, hd), F32)] * 2,
        scratch_shapes=[pltpu.VMEM((nb, hd, hd), F32)],
        compiler_params=_params(("arbitrary",)), name=name + "_scan_f",
    )(*pre)


def _dn_scan_bwd(name, pre, sall_f, sall_b, do_f, do_b, nc):
    n = pre[0].shape[0]
    hd, nh, nb, sc = HEAD_DIM, DN_HEADS, N_DIRHEAD, SCAN_CHUNKS
    shapes, _ = _pre_shapes(n)
    n_in = len(shapes)

    def body(*refs):
        ins, (sf_ref, sb_ref, dof_ref, dob_ref) = refs[:n_in], refs[n_in:n_in + 4]
        outs, ds_scr = refs[n_in + 4:2 * n_in + 4], refs[2 * n_in + 4]
        t = pl.program_id(0)

        @pl.when(t == 0)
        def _():
            ds_scr[...] = jnp.zeros_like(ds_scr)

        ds = ds_scr[...]
        for j in range(sc):
            jf, jb = sc - 1 - j, j
            args = [jnp.concatenate([ins[i][jf], ins[i + 1][jb]], axis=0) for i in range(0, n_in, 2)]
            s = jnp.concatenate([sf_ref[jf], sb_ref[jb]], axis=0)
            do = jnp.concatenate([_heads(dof_ref[pl.ds(jf * DN_CHUNK, DN_CHUNK), :]),
                                  _heads(dob_ref[pl.ds(jb * DN_CHUNK, DN_CHUNK), :])], axis=0)
            _, vjp = jax.vjp(_dn_step, s, *args)
            cts = vjp((ds, do))
            ds = cts[0]
            for i, ct in enumerate(cts[1:]):
                outs[2 * i][jf] = ct[:nh]
                outs[2 * i + 1][jb] = ct[nh:]
        ds_scr[...] = ds

    fw = lambda t: n // sc - 1 - t
    bw = lambda t: _dn_rev(n - 1 - sc * t, nc, n) // sc
    wide, st = (sc * DN_CHUNK, nh * hd), (sc, nh, hd, hd)
    return tuple(pl.pallas_call(
        body, grid=(n // sc,),
        in_specs=_scan_specs(shapes, fw, bw)
        + [pl.BlockSpec(st, lambda t: (fw(t), 0, 0, 0)), pl.BlockSpec(st, lambda t: (bw(t), 0, 0, 0)),
           pl.BlockSpec(wide, lambda t: (fw(t), 0)), pl.BlockSpec(wide, lambda t: (bw(t), 0))],
        out_specs=_scan_specs(shapes, fw, bw), out_shape=[jax.ShapeDtypeStruct(s, F32) for s in shapes],
        scratch_shapes=[pltpu.VMEM((nb, hd, hd), F32)],
        compiler_params=_params(("arbitrary",)), name=name + "_scan_b",
    )(*pre, sall_f, sall_b, do_f, do_b))


def deltanet(name, q, k, v, gb, ctx_rows):
    nc = ctx_rows // DN_CHUNK

    @jax.custom_vjp
    def pre(q, k, v, gb):
        return tuple(_dn_pre_fwd(name, q, k, v, gb)[:-1])

    def pre_fwd(*a):
        *ops, inv = _dn_pre_fwd(name, *a)
        return tuple(ops), (a, inv)

    pre.defvjp(pre_fwd, lambda res, cts: tuple(_dn_pre_bwd(name, *res[0], res[1], cts)))

    @jax.custom_vjp
    def scan(*ops):
        return tuple(_dn_scan_fwd(name, ops, nc)[:2])

    def scan_fwd(*ops):
        of, ob, sf, sb = _dn_scan_fwd(name, ops, nc)
        return (of, ob), (ops, sf, sb)

    scan.defvjp(scan_fwd, lambda res, cts: _dn_scan_bwd(name, res[0], res[1], res[2], cts[0], cts[1], nc))
    return scan(*pre(q, k, v, gb))


def _box_matrix(l, w):
    lo, hi = w // 2, w - 1 - w // 2
    pos = np.arange(l)
    start, end = np.clip(pos - lo, 0, l), np.clip(pos + hi + 1, 0, l)
    col = np.arange(l)[None, :]
    return ((col >= start[:, None]) & (col < end[:, None])) / (end - start)[:, None].astype(np.float64)


def _pool_matrices(ctx_rows, grid_rows):
    assert ctx_rows == ROW_TILE and ROW_TILE % GRID_W == 0
    ctx = np.stack([_box_matrix(ctx_rows, w) for w in POOL_WINDOWS])
    cols = np.stack([np.kron(np.eye(ROW_TILE // GRID_W), _box_matrix(GRID_W, w)) for w in POOL_WINDOWS])
    rows = np.stack([_box_matrix(grid_rows, w) for w in POOL_WINDOWS])[None]
    return np.stack([ctx, cols]).astype(np.float32), rows.astype(np.float32)


def _pool_apply(name, x, mats, group_w, seg_tiles, lane_tile):
    r, w = x.shape
    b = mats.shape[-1]
    ng = mats.shape[1]
    period = ng * group_w

    def body(x_ref, m_ref, o_ref):
        xv = x_ref[...].astype(F32)
        lane = lax.broadcasted_iota(jnp.int32, xv.shape, 1)
        grp = (lane % period) // group_w
        acc = jnp.zeros_like(xv)
        for g in range(ng):
            acc = acc + jnp.where(grp == g, _dot3(m_ref[g], xv), 0.0)
        o_ref[...] = acc

    return pl.pallas_call(
        body, grid=(r // b, w // lane_tile),
        in_specs=[pl.BlockSpec((b, lane_tile), lambda i, j: (i, j)),
                  pl.BlockSpec((None, ng, b, b), lambda i, j: (jnp.where(i >= seg_tiles, 1, 0) if mats.shape[0] > 1 else 0, 0, 0, 0))],
        out_specs=pl.BlockSpec((b, lane_tile), lambda i, j: (i, j)),
        out_shape=jax.ShapeDtypeStruct(x.shape, F32),
        compiler_params=_params(("parallel", "parallel")), name=name,
    )(x, mats)


def pool_means(name, u, ctx_rows):
    r, pw = u.shape
    grid_rows = (r - ctx_rows) // GRID_W
    gw = pw // len(POOL_WINDOWS)
    m1, m2 = _pool_matrices(ctx_rows, grid_rows)
    lane_tile = min(2048, GRID_W * pw)

    def apply(x, a1, a2, tag):
        y = _pool_apply(name + tag + "1", x, jnp.asarray(a1), gw, ctx_rows // ROW_TILE, pw)
        lat = y[ctx_rows:].reshape(grid_rows, GRID_W * pw)
        lat = _pool_apply(name + tag + "2", lat, jnp.asarray(a2), gw, 0, lane_tile)
        return jnp.concatenate([y[:ctx_rows], lat.reshape(r - ctx_rows, pw)], axis=0)

    @jax.custom_vjp
    def f(u):
        return apply(u, m1, m2, "_f")

    tr = lambda m: np.ascontiguousarray(np.swapaxes(m, -1, -2))
    f.defvjp(lambda u: (apply(u, m1, m2, "_f"), None), lambda _, ct: (apply(ct, tr(m1), tr(m2), "_b").astype(u.dtype),))
    return f(u)


def _ew(name, fn, *xs, n_out=1):
    shapes = jax.eval_shape(lambda *a: fn(*a), *xs)
    shapes = shapes if isinstance(shapes, (tuple, list)) else (shapes,)

    def body(*refs):
        res = fn(*[r[...] for r in refs[:len(xs)]])
        res = res if isinstance(res, (tuple, list)) else (res,)
        for r, o in zip(res, refs[len(xs):]):
            o[...] = r

    out = pl.pallas_call(body, out_shape=[jax.ShapeDtypeStruct(s.shape, s.dtype) for s in shapes],
                         compiler_params=_params(), name=name)(*xs)
    return out[0] if len(shapes) == 1 else tuple(out)


def _adamw_math(w, g, m, v):
    m2 = ADAM_B1 * m + (1.0 - ADAM_B1) * g
    v2 = ADAM_B2 * v + (1.0 - ADAM_B2) * (g * g)
    m_hat = m2 / (1.0 - ADAM_B1 ** ADAM_STEP)
    v_hat = v2 / (1.0 - ADAM_B2 ** ADAM_STEP)
    delta = -ADAM_LR * (m_hat / (jnp.sqrt(v_hat) + ADAM_EPS) + ADAM_WD * w)
    return delta, m2, v2


def adamw(name, w, g, m, v):
    *lead, r, c = w.shape
    tr = _pick(r, (256, 128, 64, 32, 16, 8))

    def body(w_ref, g_ref, m_ref, v_ref, d_ref, m2_ref, v2_ref):
        d_ref[...], m2_ref[...], v2_ref[...] = _adamw_math(w_ref[...], g_ref[...], m_ref[...], v_ref[...])

    if lead:
        grid, spec = (lead[0], r // tr), pl.BlockSpec((None, tr, c), lambda l, i: (l, i, 0))
    else:
        grid, spec = (r // tr,), pl.BlockSpec((tr, c), lambda i: (i, 0))
    return pl.pallas_call(body, grid=grid, in_specs=[spec] * 4, out_specs=[spec] * 3,
                          out_shape=[jax.ShapeDtypeStruct(w.shape, F32)] * 3,
                          compiler_params=_params(("parallel",) * len(grid)), name=name)(w, g, m, v)


def _sum_rows(name, xs, out_dtype=F32):
    r, c = xs[0].shape
    tr = _pick(r, tuple(t for t in (512, 256, 128, 64, 32, 16, 8) if t * c * 4 <= 2 << 20))

    def body(*refs):
        acc = refs[0][...].astype(F32)
        for ref in refs[1:-1]:
            acc = acc + ref[...].astype(F32)
        refs[-1][...] = acc.astype(out_dtype)

    spec = pl.BlockSpec((tr, c), lambda i: (i, 0))
    return pl.pallas_call(body, grid=(r // tr,), in_specs=[spec] * len(xs), out_specs=spec,
                          out_shape=jax.ShapeDtypeStruct((r, c), out_dtype),
                          compiler_params=_params(("parallel",)), name=name)(*xs)


def _place():
    return lax.axis_index("x"), lax.axis_index("y"), lax.axis_index("c")


def _chip_peers(x, y, c):
    return [(1 - x, y, c), (x, 1 - y, c), (1 - x, 1 - y, c)]


def all_gather8(name, block):
    m_per, n = block.shape

    def body(x_ref, out_ref, send_sems, recv_sems, local_sem):
        x, y, c = _place()
        me, sibling = (x, y, c), (x, y, 1 - c)
        chips = [(1 - x, y), (x, 1 - y), (1 - x, 1 - y)]

        def rows(px, py, pc):
            return out_ref.at[pl.ds((4 * px + 2 * py + pc) * m_per, m_per), :]

        def copy(k, blk, to, src=None):
            return pltpu.make_async_remote_copy(
                src_ref=rows(*blk) if src is None else src, dst_ref=rows(*blk),
                send_sem=send_sems.at[k], recv_sem=recv_sems.at[k], device_id=to, device_id_type=MESH_ID)

        mine = pltpu.make_async_copy(x_ref, rows(*me), local_sem)
        mine.start()
        first = [copy(0, me, sibling, src=x_ref)]
        first += [copy(1 + j, me, (*chip, c), src=x_ref) for j, chip in enumerate(chips)]
        for cp in first:
            cp.start()
        passed = [copy(4 + j, (*chip, c), sibling) for j, chip in enumerate(chips)]
        for j, chip in enumerate(chips):
            copy(1 + j, (*chip, c), me).wait_recv()
            passed[j].start()
        copy(0, sibling, me).wait_recv()
        for j, chip in enumerate(chips):
            copy(4 + j, (*chip, 1 - c), me).wait_recv()
        for cp in first + passed:
            cp.wait_send()
        mine.wait()

    return pl.pallas_call(
        body, out_shape=jax.ShapeDtypeStruct((8 * m_per, n), block.dtype),
        in_specs=[pl.BlockSpec(memory_space=pltpu.VMEM)], out_specs=pl.BlockSpec(memory_space=pltpu.VMEM),
        scratch_shapes=[pltpu.SemaphoreType.DMA((7,)), pltpu.SemaphoreType.DMA((7,)), pltpu.SemaphoreType.DMA],
        compiler_params=_params(), name=name,
    )(block)


N_PEERS = 3


def gather_chips(name, shards):
    n = len(shards)

    def body(*refs):
        xs, outs, (send_sems, recv_sems, local_sems) = refs[:n], refs[n:2 * n], refs[2 * n:]
        x, y, c = _place()
        local, remote = [], []
        for i in range(n):
            local.append(pltpu.make_async_copy(xs[i].at[c], outs[i].at[2 * x + y], local_sems.at[i]))
            local[-1].start()
            for p, peer in enumerate(_chip_peers(x, y, c)):
                remote.append(pltpu.make_async_remote_copy(
                    src_ref=xs[i].at[c], dst_ref=outs[i].at[2 * x + y], send_sem=send_sems.at[i * N_PEERS + p],
                    recv_sem=recv_sems.at[i * N_PEERS + p], device_id=peer, device_id_type=MESH_ID))
                remote[-1].start()
        for i in range(n):
            for p, (px, py, _) in enumerate(_chip_peers(x, y, c)):
                pltpu.make_async_remote_copy(
                    src_ref=xs[i].at[c], dst_ref=outs[i].at[2 * px + py], send_sem=send_sems.at[i * N_PEERS + p],
                    recv_sem=recv_sems.at[i * N_PEERS + p], device_id=(px, py, c), device_id_type=MESH_ID).wait_recv()
        for cp in remote:
            cp.wait_send()
        for cp in local:
            cp.wait()

    hbm = pl.BlockSpec(memory_space=pltpu.HBM)
    return pl.pallas_call(
        body, out_shape=[jax.ShapeDtypeStruct((N_CHIPS,) + s.shape[1:], s.dtype) for s in shards],
        in_specs=[hbm] * n, out_specs=[hbm] * n,
        scratch_shapes=[pltpu.SemaphoreType.DMA((n * N_PEERS,)), pltpu.SemaphoreType.DMA((n * N_PEERS,)),
                        pltpu.SemaphoreType.DMA((n,))],
        compiler_params=_params(), name=name,
    )(*shards)


def swap_sibling(name, blocks):
    n = len(blocks)
    pairs = [tuple(b) if isinstance(b, (tuple, list)) else (b,) for b in blocks]
    flat = [a for p in pairs for a in p]
    first = [sum(len(q) for q in pairs[:i]) for i in range(n)]

    def body(*refs):
        xs, outs, (send_sems, recv_sems) = refs[:len(flat)], refs[len(flat):len(flat) + n], refs[len(flat) + n:]
        x, y, c = _place()

        def copy(i, src):
            return pltpu.make_async_remote_copy(src_ref=src, dst_ref=outs[i], send_sem=send_sems.at[i], recv_sem=recv_sems.at[i],
                                                device_id=(x, y, 1 - c), device_id_type=MESH_ID)

        for i, p in enumerate(pairs):
            if len(p) == 1:
                copy(i, xs[first[i]]).start()
            else:
                for half in range(2):
                    pl.when(c == 1 - half)(copy(i, xs[first[i] + half]).start)
        for i in range(n):
            copy(i, xs[first[i]]).wait()

    hbm = pl.BlockSpec(memory_space=pltpu.HBM)
    return pl.pallas_call(
        body, out_shape=[jax.ShapeDtypeStruct(p[0].shape, p[0].dtype) for p in pairs],
        in_specs=[hbm] * len(flat), out_specs=[hbm] * n,
        scratch_shapes=[pltpu.SemaphoreType.DMA((n,)), pltpu.SemaphoreType.DMA((n,))],
        compiler_params=_params(), name=name,
    )(*flat)


def scatter_chips(name, pieces):
    n = len(pieces)

    def body(*refs):
        xs, outs, (send_sems, recv_sems) = refs[:n], refs[n:2 * n], refs[2 * n:]
        x, y, c = _place()
        copies = []
        for i in range(n):
            for p, (px, py, pc) in enumerate(_chip_peers(x, y, c)):
                copies.append(pltpu.make_async_remote_copy(
                    src_ref=xs[i].at[2 * px + py], dst_ref=outs[i].at[p], send_sem=send_sems.at[i * N_PEERS + p],
                    recv_sem=recv_sems.at[i * N_PEERS + p], device_id=(px, py, pc), device_id_type=MESH_ID))
                copies[-1].start()
        for cp in copies:
            cp.wait()

    hbm = pl.BlockSpec(memory_space=pltpu.HBM)
    return pl.pallas_call(
        body, out_shape=[jax.ShapeDtypeStruct((N_PEERS,) + p.shape[1:], p.dtype) for p in pieces],
        in_specs=[hbm] * n, out_specs=[hbm] * n,
        scratch_shapes=[pltpu.SemaphoreType.DMA((n * N_PEERS,)), pltpu.SemaphoreType.DMA((n * N_PEERS,))],
        compiler_params=_params(), name=name,
    )(*pieces)


def pair_sum(name, half0, half1, got, out_dtype):
    r, w = got.shape
    tr = _pick(r, tuple(t for t in (512, 256, 128, 64, 32, 16, 8) if t * w * 4 <= 2 << 20))

    def body(a0, a1, g, o):
        mine = jnp.where(lax.axis_index("c") == 0, a0[...], a1[...])
        o[...] = (mine.astype(F32) + g[...].astype(F32)).astype(out_dtype)

    spec = pl.BlockSpec((tr, w), lambda i: (i, 0))
    return pl.pallas_call(body, grid=(r // tr,), in_specs=[spec] * 3, out_specs=spec,
                          out_shape=jax.ShapeDtypeStruct((r, w), out_dtype),
                          compiler_params=_params(("parallel",)), name=name)(half0, half1, got)


def _sum_devices(name, got, fold=False):
    def body(a_ref, *o_refs):
        acc = a_ref[0]
        for i in range(1, N_DEV):
            acc = acc + a_ref[i]
        o_refs[0][...] = acc
        if fold:
            o_refs[1][...] = acc + pltpu.roll(acc, SUBLANES // 2, 0)

    shape = jax.ShapeDtypeStruct(got.shape[1:], F32)
    out = pl.pallas_call(body, out_shape=[shape] * (2 if fold else 1), compiler_params=_params(), name=name)(got)
    return out if fold else out[0]


def _w_in_bounds(d, pw, scw):
    off_z = 3 * DN_WIDTH
    off_a = off_z + DN_WIDTH
    off_pool = off_a + 2 * N_DIRHEAD
    off_sc = off_pool + pw
    off_gate = off_sc + 3 * scw
    return (0, off_z, off_a, off_pool, off_sc, off_gate, off_gate + 3 * d)


def _misc_widths(pw, scw):
    return (DN_WIDTH, pw, 3 * scw, LANES)


def split_matrices(full, pw, scw):
    depth, d = len(full["w_in"]), full["w_in"][0].shape[0]
    b = _w_in_bounds(d, pw, scw)
    out = {k: [] for k in ("qkv", "gate", "misc", "gu", "br_a", "br_b", "br_c", "o", "down")}
    for l in range(depth):
        w = full["w_in"][l]
        out["qkv"].append(w[:, b[0]:b[1]])
        out["gate"].append(w[:, b[5]:b[6]])
        out["misc"].append(jnp.concatenate([w[:, b[1]:b[2]], w[:, b[3]:b[5]], _pad_lanes(w[:, b[2]:b[3]])], axis=1))
        out["gu"].append(full["w_gu"][l])
        for k in ("br_a", "br_b", "br_c", "o", "down"):
            out[k].append(full["w_" + k][l])
    return out


def join_matrix_grads(g, pw, scw):
    depth = len(g["qkv"])
    n_z, n_rest, n_ab = DN_WIDTH, pw + 3 * scw, 2 * N_DIRHEAD
    w_in = []
    for l in range(depth):
        m = g["misc"][l]
        w_in.append(jnp.concatenate([g["qkv"][l], m[:, :n_z], m[:, n_z + n_rest:n_z + n_rest + n_ab],
                                     m[:, n_z:n_z + n_rest], g["gate"][l]], axis=1))
    out = {"w_in": w_in, "w_gu": g["gu"]}
    for k in ("br_a", "br_b", "br_c", "o", "down"):
        out["w_" + k] = g[k]
    return {k: [a.astype(MXU_DTYPE) for a in v] for k, v in out.items()}


def split_cols(x, widths):
    edges = np.cumsum((0,) + tuple(widths))

    def cut(x):
        return tuple(x[:, a:b] for a, b in zip(edges[:-1], edges[1:]))

    f = jax.custom_vjp(cut)
    f.defvjp(lambda x: (cut(x), None), lambda _, cts: (jnp.concatenate(cts, axis=1),))
    return f(x)


def _row(v):
    return v.reshape(1, -1)


def _pad_lanes(v, width=LANES):
    return jnp.pad(v, ((0, 0), (0, width - v.shape[1])))


def _block_diag(blocks):
    g, n, _ = blocks.shape
    out = jnp.zeros((g * n, g * n), blocks.dtype)
    for i in range(g):
        out = out.at[i * n:(i + 1) * n, i * n:(i + 1) * n].set(blocks[i])
    return out


def local_loss(p, carriers, x, mod_lat, mod_ctx, wts, ctx, target):
    ctx_rows, d = ctx.shape
    depth = len(wts["qkv"])
    pw, scw = p["pool_scale"].shape[1], p["sc_conv_w"].shape[2]
    ff = wts["down"][0].shape[0]
    ct = ctx_rows // ROW_TILE
    xs = jnp.concatenate([ctx, x], axis=0)
    seg = lambda l, k: jnp.stack([mod_ctx[l, k], mod_lat[l, k]]).reshape(2, 1, d)
    dn = gate2 = None
    for l in range(depth):
        sh1, sc1, g1, sh2, sc2, g2 = (seg(l, k) for k in range(6))
        tag = f"l{l}_"
        lin = lambda key, a: mm(tag + key, a, wts[key][l], carriers[key][l])
        if l == 0:
            (h1,) = rowwise(tag + "mod", _fn_modulate, [xs], [_row(p["norm1_g"][l]), sh1, sc1], seg=(False, True, True),
                            outs=[(d, MXU_DTYPE, 1)], ctx_tiles=ct)
        else:
            xs, h1 = rowwise(tag + "resmod1", _fn_resmod, [xs, dn], [gate2, _row(p["norm1_g"][l]), sh1, sc1],
                             seg=(True, False, True, True), outs=[(d, F32, 1), (d, MXU_DTYPE, 1)], ctx_tiles=ct)
        fan = ("qkv", "gate", "misc")
        p_qkv, p_gate, p_misc = mm_fanout(tag + "in", h1, [wts[k][l] for k in fan], [carriers[k][l] for k in fan])
        p_z, p_pool, p_sc, p_ab = split_cols(p_misc, _misc_widths(pw, scw))
        cw = p["dn_conv_w"][l]
        qkv_conv = conv3(tag + "dnconv", p_qkv, cw[0:1], cw[1:2], cw[2:3], ct, F32)
        q, k, v, gb = rowwise(
            tag + "dnprep", _fn_dnprep, [qkv_conv, p_ab],
            [_pad_lanes(p["dn_a_log"][l].reshape(1, -1)), _pad_lanes(p["dn_dt_bias"][l].reshape(1, -1))],
            parts_r=(3 * DN_HEADS, 1), outs=[(DN_WIDTH, F32, DN_HEADS)] * 3 + [(LANES, F32, 1)], ctx_tiles=ct)
        o_f, o_b = deltanet(tag + "dn", q, k, v, gb, ctx_rows)
        (oz,) = rowwise(tag + "dnpost", _fn_dnpost, [o_f, o_b, p_z], [_row(p["dn_norm_g"][l])], parts_r=(DN_HEADS,) * 3,
                        outs=[(DN_WIDTH, MXU_DTYPE, DN_HEADS)], long_tiles=768)
        y_a = lin("br_a", oz)
        means = pool_means(tag + "box", p_pool, ctx_rows)
        (dpool,) = rowwise(tag + "poolsub", _fn_sub, [means, p_pool], [], outs=[(pw, MXU_DTYPE, 1)], long_tiles=2816)
        pool_mat = _block_diag(p["pool_w"][l])
        mixed = mm(tag + "poolw", dpool, pool_mat.astype(MXU_DTYPE), pool_mat)
        (yb_in,) = rowwise(tag + "poolscale", _fn_scale, [mixed], [_row(p["pool_scale"][l])], outs=[(pw, MXU_DTYPE, 1)],
                           long_tiles=2816)
        y_b = lin("br_b", yb_in)
        sw = p["sc_conv_w"][l]
        (yc_in,) = rowwise(tag + "sconv", _fn_shortconv, [p_sc], [sw[0:1], sw[1:2], sw[2:3]], halo=(True,), parts_r=(3,),
                           outs=[(scw, MXU_DTYPE, 1)], ctx_tiles=ct)
        y_c = lin("br_c", yc_in)
        (y,) = rowwise(tag + "merge", _fn_merge, [p_gate, y_a, y_b, y_c], [], parts_r=(3, 1, 1, 1),
                       outs=[(d, MXU_DTYPE, 1)], ctx_tiles=ct)
        mix = lin("o", y)
        xs, h2 = rowwise(tag + "resmod2", _fn_resmod, [xs, mix], [g1, _row(p["norm2_g"][l]), sh2, sc2],
                         seg=(True, False, True, True), outs=[(d, F32, 1), (d, MXU_DTYPE, 1)], ctx_tiles=ct)
        (act,) = rowwise(tag + "swiglu", _fn_swiglu, [lin("gu", h2)], [], parts_r=(2,),
                         outs=[(ff, MXU_DTYPE, 1)], ctx_tiles=ct)
        dn = lin("down", act)
        gate2 = g2
    counts = jnp.concatenate([jnp.zeros((1, 1, LANES), F32), jnp.ones((1, 1, LANES), F32)])
    (total,) = rowwise("final", _fn_final, [xs, dn, target], [gate2, _row(p["final_norm_g"]), counts],
                       seg=(True, False, True), reds=[LANES], ctx_tiles=ct, skip=(0, 0, ct))
    return total[0, 0]


BIG = ("w_in", "w_br_a", "w_br_b", "w_br_c", "w_o", "w_gu", "w_down")
ROW_SHARDED = ("w_o", "w_down")
SMALL_REPL = ("norm1_g", "norm2_g", "dn_a_log", "dn_dt_bias", "dn_norm_g", "pool_w", "pool_scale", "final_norm_g")
SMALL_SHARD = ("dn_conv_w", "sc_conv_w")
WEIGHTS = ("c_ctx", "w_ada", "b_ada", "norm1_g", "norm2_g", "w_in", "dn_conv_w", "dn_a_log", "dn_dt_bias", "dn_norm_g", "pool_w",
           "pool_scale", "sc_conv_w", "w_br_a", "w_br_b", "w_br_c", "w_o", "w_gu", "w_down", "final_norm_g")
N_CHIPS, N_DEV = 4, 8
COMM_COLS = 1024


def _pack(arrays, rows_multiple, cols=COMM_COLS, dtype=F32):
    size = sum(int(np.prod(a.shape)) for a in arrays)
    rows = -(-size // (cols * rows_multiple)) * rows_multiple
    tail = [jnp.zeros((rows * cols - size,), dtype)] if rows * cols > size else []
    return jnp.concatenate([a.astype(dtype).reshape(-1) for a in arrays] + tail).reshape(rows, cols)


def _unpack(flat, shapes):
    out, pos = [], 0
    for s in shapes:
        n = int(np.prod(s))
        out.append(flat[pos:pos + n].reshape(s))
        pos += n
    return out


def _join_shards(parts, name):
    _, a, b = parts.shape
    if name in ROW_SHARDED:
        return parts.reshape(N_CHIPS * a, b)
    return jnp.moveaxis(parts, 0, 1).reshape(a, N_CHIPS * b)


def _cut_shards(full, name):
    k, n = full.shape
    if name in ROW_SHARDED:
        return full.reshape(N_CHIPS, k // N_CHIPS, n)
    return jnp.moveaxis(full.reshape(k, N_CHIPS, n // N_CHIPS), 1, 0)


def _dsilu(x):
    s = _sigmoid(x)
    return s + x * s * (1.0 - s)


def kernel(x, c, ctx, c_ctx, w_ada, b_ada, norm1_g, norm2_g, w_in, dn_conv_w, dn_a_log, dn_dt_bias, dn_norm_g, pool_w, pool_scale, sc_conv_w, w_br_a, w_br_b, w_br_c, w_o, w_gu, w_down, final_norm_g, loss_target, m_c_ctx, m_w_ada, m_b_ada, m_norm1_g, m_norm2_g, m_w_in, m_dn_conv_w, m_dn_a_log, m_dn_dt_bias, m_dn_norm_g, m_pool_w, m_pool_scale, m_sc_conv_w, m_w_br_a, m_w_br_b, m_w_br_c, m_w_o, m_w_gu, m_w_down, m_final_norm_g, v_c_ctx, v_w_ada, v_b_ada, v_norm1_g, v_norm2_g, v_w_in, v_dn_conv_w, v_dn_a_log, v_dn_dt_bias, v_dn_norm_g, v_pool_w, v_pool_scale, v_sc_conv_w, v_w_br_a, v_w_br_b, v_w_br_c, v_w_o, v_w_gu, v_w_down, v_final_norm_g):
    given = dict(locals())
    ix, iy, ic = _place()
    chip, dev = 2 * ix + iy, 4 * ix + 2 * iy + ic
    d = x.shape[-1]
    depth = w_in.shape[0]
    ada_cols = w_ada.shape[2]
    spare = 2 * SUBLANES - N_DEV - 1

    got0 = all_gather8("gather_cond", _pack([c, dn_conv_w, sc_conv_w], SUBLANES)).reshape(N_DEV, -1)
    c_all = got0[:, :d]
    taps = [_unpack(got0[2 * j, d:], [dn_conv_w.shape, sc_conv_w.shape]) for j in range(N_CHIPS)]
    full = {"dn_conv_w": jnp.concatenate([t[0] for t in taps], axis=-1),
            "sc_conv_w": jnp.concatenate([t[1] for t in taps], axis=-1)}
    assert depth == 2, "the two cores of a chip split the layers between them"
    got_layer = gather_chips("gather_weights", [given[n].astype(MXU_DTYPE) for n in BIG])
    other_layer = swap_sibling("swap_weights", got_layer)
    mats = {n: [_join_shards(jnp.where(ic == l, got_layer[i], other_layer[i]), n) for l in range(depth)]
            for i, n in enumerate(BIG)}
    for n in SMALL_REPL:
        full[n] = given[n]
    pw, scw = pool_scale.shape[1], full["sc_conv_w"].shape[2]
    wts = split_matrices(mats, pw, scw)
    carriers = {k: [jnp.zeros(w.shape, F32) for w in ws] for k, ws in wts.items()}

    cond = jnp.concatenate([c_all, c_ctx[None], jnp.zeros((spare, d), F32)])
    s_cond = _ew("silu_cond", _silu, cond)
    mod_cols = jnp.concatenate([_mm(f"ada{l}_f", s_cond, w_ada[l], NN) for l in range(depth)], axis=1)
    mod_got = all_gather8("gather_mod", mod_cols).reshape(N_DEV, 2 * SUBLANES, depth, ada_cols)
    mod_all = jnp.concatenate([mod_got[2 * j] for j in range(N_CHIPS)], axis=-1) + b_ada[None]
    mod_lat = lax.dynamic_index_in_dim(mod_all, dev, 0, keepdims=False).reshape(depth, 6, d)
    mod_ctx = mod_all[N_DEV].reshape(depth, 6, d)

    loss_local, (g_full, g_mats, grad_x, g_mod_lat, g_mod_ctx) = jax.value_and_grad(local_loss, argnums=(0, 1, 2, 3, 4))(
        full, carriers, x[0], mod_lat, mod_ctx, wts, ctx[0], loss_target[0])
    g_mats_full = join_matrix_grads(g_mats, pw, scw)
    loss = lax.psum(loss_local, ("x", "y", "c"))

    dmod_cols = (2 * depth * 6 * d) // SUBLANES
    dmod = all_gather8("gather_dmod", _pack([g_mod_lat, g_mod_ctx], SUBLANES, cols=dmod_cols))
    dmod = dmod.reshape(N_DEV, SUBLANES, dmod_cols)
    dmod_sum, dmod_fold = _sum_devices("reduce_dmod", dmod, fold=True)
    half_rows = SUBLANES // 2
    grad_b_ada = dmod_fold[:half_rows].reshape(depth, 6 * d)
    dctx_sum = dmod_sum[half_rows:].reshape(1, depth, 6 * d)
    d9 = jnp.concatenate([dmod[:, :half_rows].reshape(N_DEV, depth, 6 * d), dctx_sum, jnp.zeros((spare, depth, 6 * d), F32)])
    d9 = lax.dynamic_slice_in_dim(d9, chip * ada_cols, ada_cols, axis=2)
    grad_w_ada = jnp.stack([_mm(f"ada{l}_dw", s_cond, d9[:, l], TN) for l in range(depth)])
    ds_cond = [_mm(f"ada{l}_da", d9[:, l], w_ada[l], NT) for l in range(depth)]
    dsilu_part = _sum_rows("sum_dcond", ds_cond)[N_DEV]

    small_names = SMALL_REPL + SMALL_SHARD
    small_grads = [dsilu_part] + [g_full[n] for n in small_names]
    small_shapes = [a.shape for a in small_grads]
    n_small = sum(int(np.prod(s)) for s in small_shapes)
    cols = -(-n_small // (SUBLANES * LANES)) * LANES
    got2 = all_gather8("gather_small", _pack(small_grads, SUBLANES, cols=cols)).reshape(N_DEV, SUBLANES, cols)
    small_sum = _unpack(_sum_devices("reduce_small", got2).reshape(-1), small_shapes)
    grads = dict(zip(small_names, small_sum[1:]))
    grads["c_ctx"] = _ew("dsilu", lambda g, z: 0.5 * g * _dsilu(z), _row(small_sum[0]), _row(c_ctx)).reshape(-1)
    grads["w_ada"], grads["b_ada"] = grad_w_ada, grad_b_ada
    for n in SMALL_SHARD:
        width = given[n].shape[-1]
        grads[n] = lax.dynamic_slice_in_dim(grads[n], chip * width, width, axis=-1)

    from_sibling = swap_sibling("swap_layers", [tuple(g_mats_full[n]) for n in BIG])
    pieces = [_cut_shards(pair_sum("sum_pair_" + n, *g_mats_full[n], from_sibling[i], MXU_DTYPE), n) for i, n in enumerate(BIG)]
    from_chips = scatter_chips("scatter_pieces", pieces)
    reduced = []
    for i, n in enumerate(BIG):
        own = lax.dynamic_index_in_dim(pieces[i], chip, 0, keepdims=False)
        a, b = own.shape
        view = lambda t: t.reshape(-1, b)
        reduced.append(_sum_rows("sum_chips_" + n, [view(own)] + [view(from_chips[i][p]) for p in range(N_PEERS)]).reshape(a, b))
    other_reduced = swap_sibling("swap_reduced", reduced)
    for i, n in enumerate(BIG):
        grads[n] = jnp.stack([jnp.where(ic == l, reduced[i], other_reduced[i]) for l in range(depth)])

    delta, new_m, new_v = {}, {}, {}
    large = BIG + ("w_ada",)
    for n in large:
        delta[n], new_m[n], new_v[n] = adamw("adamw_" + n, given[n], grads[n], given["m_" + n], given["v_" + n])
    rest = [n for n in WEIGHTS if n not in large]
    rest_shapes = [given[n].shape for n in rest]
    packed = [_pack([src[pre + n] for n in rest], SUBLANES, cols=LANES)
              for src, pre in ((given, ""), (grads, ""), (given, "m_"), (given, "v_"))]
    for res, o in zip((delta, new_m, new_v), adamw("adamw_small", *packed)):
        for n, a in zip(rest, _unpack(o.reshape(-1), rest_shapes)):
            res[n] = a
    return (loss, grad_x[None], *[grads[n] for n in WEIGHTS], *[delta[n] for n in WEIGHTS],
            *[new_m[n] for n in WEIGHTS], *[new_v[n] for n in WEIGHTS])
```

```python
import math

import numpy as np
import jax
import jax.numpy as jnp
from jax import lax
from jax.experimental import pallas as pl
from jax.experimental.pallas import tpu as pltpu

F32 = jnp.float32
MXU_DTYPE = jnp.bfloat16

DN_HEADS = 4
HEAD_DIM = 128
DN_WIDTH = DN_HEADS * HEAD_DIM
DN_CHUNK = 64
GRID_W = 64
EPS = 1e-6
POOL_WINDOWS = (2, 4, 8, 16)
N_DIRHEAD = 2 * DN_HEADS
ADAM_LR, ADAM_B1, ADAM_B2, ADAM_EPS, ADAM_WD, ADAM_STEP = 0.001, 0.9, 0.999, 1e-08, 0.01, 10

LANES = 128
SUBLANES = 8
ROW_TILE = 256
VMEM_LIMIT = 56 * 1024 * 1024

MESH_ID = pl.DeviceIdType.MESH
NN, NT, TN = ((1,), (0,)), ((1,), (1,)), ((0,), (0,))


def _params(sem=None):
    return pltpu.CompilerParams(dimension_semantics=sem, vmem_limit_bytes=VMEM_LIMIT)


def _pick(n, cands):
    for c in cands:
        if c <= n and n % c == 0:
            return c
    return n


def _mm(name, a, b, dims, out_dtype=F32):
    if dims == NN:
        (m, kk), (_, n) = a.shape, b.shape
    elif dims == NT:
        (m, kk), (n, _) = a.shape, b.shape
    else:
        (kk, m), (_, n) = a.shape, b.shape
    tm = _pick(m, (768, 1024, 1408, 512, 256, 128, 64, 32, 16, 8))
    tn = _pick(n, ((3072, 2816) if dims == NN else ()) + (1536, 1024, 1408, 1664, 768, 896, 512, 256, 128))
    tk = kk if dims == NN and kk <= 2816 else _pick(kk, (1024, 1408, 768, 512, 256, 128))
    gi, gj, gl = m // tm, n // tn, kk // tk
    if dims == NN:
        a_spec = pl.BlockSpec((tm, tk), lambda i, j, l: (i, l))
        b_spec = pl.BlockSpec((tk, tn), lambda i, j, l: (l, j))
    elif dims == NT:
        a_spec = pl.BlockSpec((tm, tk), lambda i, j, l: (i, l))
        b_spec = pl.BlockSpec((tn, tk), lambda i, j, l: (j, l))
    else:
        a_spec = pl.BlockSpec((tk, tm), lambda i, j, l: (l, i))
        b_spec = pl.BlockSpec((tk, tn), lambda i, j, l: (l, j))
    direct = gl == 1
    use_acc = (not direct) and out_dtype != F32

    def body(a_ref, b_ref, o_ref, *scratch):
        part = lax.dot_general(a_ref[...].astype(MXU_DTYPE), b_ref[...].astype(MXU_DTYPE), (dims, ((), ())),
                               preferred_element_type=F32)
        if direct:
            o_ref[...] = part.astype(out_dtype)
            return
        acc = scratch[0] if use_acc else o_ref
        l = pl.program_id(2)

        @pl.when(l == 0)
        def _():
            acc[...] = part

        @pl.when(l > 0)
        def _():
            acc[...] += part

        if use_acc:
            @pl.when(l == gl - 1)
            def _():
                o_ref[...] = acc[...].astype(out_dtype)

    return pl.pallas_call(
        body, grid=(gi, gj, gl), in_specs=[a_spec, b_spec],
        out_specs=pl.BlockSpec((tm, tn), lambda i, j, l: (i, j)),
        out_shape=jax.ShapeDtypeStruct((m, n), out_dtype),
        scratch_shapes=[pltpu.VMEM((tm, tn), F32)] if use_acc else [],
        compiler_params=_params(("parallel", "parallel", "arbitrary")), name=name,
    )(a, b)


def mm(name, a, w, carrier):
    @jax.custom_vjp
    def f(a, w, carrier):
        return _mm(name + "_f", a, w, NN, out_dtype=MXU_DTYPE)

    def fwd(a, w, carrier):
        return f(a, w, carrier), (a, w)

    def bwd(res, dc):
        a, w = res
        da = _mm_nt_sum(name + "_da", [dc], [w], a.dtype)
        dw = _mm(name + "_dw", a, dc, TN)
        return da, None, dw

    f.defvjp(fwd, bwd)
    return f(a, w, carrier)


def _mm_nt_sum(name, dcs, ws, out_dtype):
    m, kk = dcs[0].shape[0], ws[0].shape[0]
    tm = _pick(m, (768, 1024, 512, 256, 128, 64, 32, 16, 8))
    tns = [_pick(w.shape[1], (2816, 1664, 1536, 1408, 1024, 768, 896, 512, 256, 128)) for w in ws]
    counts = [w.shape[1] // tn for w, tn in zip(ws, tns)]
    starts = [sum(counts[:g]) for g in range(len(ws))]
    steps = sum(counts)

    def col(g):
        return lambda i, t: jnp.clip(t - starts[g], 0, counts[g] - 1)

    def body(*refs):
        dc_refs, w_refs, o_ref, acc = refs[:len(ws)], refs[len(ws):2 * len(ws)], refs[-2], refs[-1]
        t = pl.program_id(1)

        @pl.when(t == 0)
        def _():
            acc[...] = jnp.zeros_like(acc)

        for g in range(len(ws)):
            @pl.when(jnp.logical_and(t >= starts[g], t < starts[g] + counts[g]))
            def _():
                acc[...] += lax.dot_general(dc_refs[g][...].astype(MXU_DTYPE), w_refs[g][...].astype(MXU_DTYPE),
                                            (NT, ((), ())), preferred_element_type=F32)

        @pl.when(t == steps - 1)
        def _():
            o_ref[...] = acc[...].astype(out_dtype)

    dc_specs = [pl.BlockSpec((tm, tns[g]), (lambda c: lambda i, t: (i, c(i, t)))(col(g))) for g in range(len(ws))]
    w_specs = [pl.BlockSpec((kk, tns[g]), (lambda c: lambda i, t: (0, c(i, t)))(col(g))) for g in range(len(ws))]
    return pl.pallas_call(
        body, grid=(m // tm, steps), in_specs=dc_specs + w_specs,
        out_specs=pl.BlockSpec((tm, kk), lambda i, t: (i, 0)),
        out_shape=jax.ShapeDtypeStruct((m, kk), out_dtype),
        scratch_shapes=[pltpu.VMEM((tm, kk), F32)],
        compiler_params=_params(("parallel", "arbitrary")), name=name,
    )(*dcs, *ws)


def mm_fanout(name, a, ws, carriers):
    n = len(ws)

    @jax.custom_vjp
    def f(a, ws, carriers):
        return tuple(_mm(f"{name}{g}_f", a, ws[g], NN, out_dtype=MXU_DTYPE) for g in range(n))

    def fwd(a, ws, carriers):
        return f(a, ws, carriers), (a, ws)

    def bwd(res, dcs):
        a, ws = res
        da = _mm_nt_sum(name + "_da", list(dcs), list(ws), a.dtype)
        dws = tuple(_mm(f"{name}{g}_dw", a, dcs[g], TN) for g in range(n))
        return da, None, dws

    f.defvjp(fwd, bwd)
    return f(a, tuple(ws), tuple(carriers))


def _split(x, parts):
    w = x.shape[-1] // parts
    return [x[:, k * w:(k + 1) * w] for k in range(parts)]


def _cat(xs):
    return xs[0] if len(xs) == 1 else jnp.concatenate(xs, axis=-1)


def _shift_rows(x, prev_ref, next_ref, i, ctx_tiles, nt):
    tr = x.shape[0]
    rid = lax.broadcasted_iota(jnp.int32, x.shape, 0)
    first = jnp.logical_or(i == 0, i == ctx_tiles)
    last = jnp.logical_or(i == ctx_tiles - 1, i == nt - 1)
    prow = jnp.where(first, 0.0, prev_ref[SUBLANES - 1:SUBLANES, :].astype(F32))
    nrow = jnp.where(last, 0.0, next_ref[0:1, :].astype(F32))
    xm = jnp.where(rid == 0, prow, pltpu.roll(x, 1, 0))
    xp = jnp.where(rid == tr - 1, nrow, pltpu.roll(x, tr - 1, 0))
    return xm, xp


def rowwise(name, fn, rows, vecs, *, halo=(), seg=(), parts_r=None, parts_v=None, outs=(), reds=(), ctx_tiles=1, skip=None,
            long_tiles=0):
    nr, nv = len(rows), len(vecs)
    halo = tuple(halo) or (False,) * nr
    seg = tuple(seg) or (False,) * nv
    skip = tuple(skip or (0,) * nr)
    parts_r = tuple(parts_r or (1,) * nr)
    parts_v = tuple(parts_v or (1,) * nv)
    r_total = rows[0].shape[0]
    tr = _pick(r_total, (long_tiles, 768)) if long_tiles else ROW_TILE
    assert not long_tiles or not (any(halo) or any(seg) or any(skip))
    nt = r_total // tr
    assert r_total % tr == 0 and (ctx_tiles > 0 or not any(seg)) and not any(h and s for h, s in zip(halo, skip))

    def tile_map(i):
        return lambda t: (jnp.maximum(t - skip[i], 0), 0)

    def row_specs():
        sp = []
        for i, r in enumerate(rows):
            w = r.shape[1]
            sp.append(pl.BlockSpec((tr, w), tile_map(i)))
            if halo[i]:
                k = tr // SUBLANES
                sp.append(pl.BlockSpec((SUBLANES, w), lambda t: (jnp.maximum(t * k - 1, 0), 0)))
                sp.append(pl.BlockSpec((SUBLANES, w), lambda t: (jnp.minimum((t + 1) * k, nt * k - 1), 0)))
        return sp

    def vec_spec(j):
        w = vecs[j].shape[-1]
        if seg[j]:
            return pl.BlockSpec((None, 1, w), lambda t: (jnp.where(t >= ctx_tiles, 1, 0), 0, 0))
        return pl.BlockSpec((1, w), lambda t: (0, 0))

    def row_args(rv):
        a = []
        for i in range(nr):
            a += [rv[i]] * 3 if halo[i] else [rv[i]]
        return a

    def load(refs, t):
        pos, rp = 0, []
        for i in range(nr):
            x = refs[pos][...].astype(F32)
            if halo[i]:
                xm, xp = _shift_rows(x, refs[pos + 1], refs[pos + 2], t, ctx_tiles, nt)
                rp.append(list(zip(_split(x, parts_r[i]), _split(xm, parts_r[i]), _split(xp, parts_r[i]))))
                pos += 3
            else:
                rp.append(_split(x, parts_r[i]))
                pos += 1
        vp = []
        for j in range(nv):
            vp.append(_split(refs[pos][...].astype(F32), parts_v[j]))
            pos += 1
        return rp, vp, refs[pos:]

    n_out, n_red = len(outs), len(reds)

    def fwd_call(*rv):
        def body(*refs):
            t = pl.program_id(0)
            rp, vp, rest = load(refs, t)
            o_parts, r_parts = fn(rp, vp)
            for k in range(n_out):
                rest[k][...] = _cat(o_parts[k]).astype(outs[k][1])
            for k in range(n_red):
                ref = rest[n_out + k]

                @pl.when(t == 0)
                def _():
                    ref[...] = r_parts[k]

                @pl.when(t > 0)
                def _():
                    ref[...] += r_parts[k]

        res = pl.pallas_call(
            body, grid=(nt,),
            in_specs=row_specs() + [vec_spec(j) for j in range(nv)],
            out_specs=[pl.BlockSpec((tr, o[0]), lambda t: (t, 0)) for o in outs]
            + [pl.BlockSpec((1, w), lambda t: (0, 0)) for w in reds],
            out_shape=[jax.ShapeDtypeStruct((r_total, o[0]), o[1]) for o in outs]
            + [jax.ShapeDtypeStruct((1, w), F32) for w in reds],
            compiler_params=_params(("arbitrary",)), name=name + "_f",
        )(*row_args(rv), *rv[nr:])
        return tuple(res)

    def bwd_call(rv, cts):
        def body(*refs):
            t = pl.program_id(0)
            rp, vp, rest = load(refs, t)
            ct_o = [_split(rest[k][...].astype(F32), outs[k][2]) for k in range(n_out)]
            ct_r = [rest[n_out + k][...] for k in range(n_red)]
            rest = rest[n_out + n_red:]
            _, vjp = jax.vjp(fn, rp, vp)
            d_rp, d_vp = vjp((ct_o, ct_r))
            pos = 0
            for i in range(nr):
                if halo[i]:
                    for c in range(3):
                        rest[pos + c][...] = _cat([p[c] for p in d_rp[i]])
                    pos += 3
                else:
                    rest[pos][...] = _cat(d_rp[i]).astype(rows[i].dtype)
                    pos += 1
            for j in range(nv):
                ref, val = rest[pos + j], _cat(d_vp[j])
                start = jnp.logical_or(t == 0, t == ctx_tiles) if seg[j] else t == 0

                @pl.when(start)
                def _():
                    ref[...] = val

                @pl.when(jnp.logical_not(start))
                def _():
                    ref[...] += val

        d_row_specs, d_row_shapes = [], []
        for i, r in enumerate(rows):
            w = r.shape[1]
            for _ in range(3 if halo[i] else 1):
                d_row_specs.append(pl.BlockSpec((tr, w), tile_map(i)))
                d_row_shapes.append(jax.ShapeDtypeStruct(r.shape, F32 if halo[i] else r.dtype))
        res = pl.pallas_call(
            body, grid=(nt,),
            in_specs=row_specs() + [vec_spec(j) for j in range(nv)]
            + [pl.BlockSpec((tr, o[0]), lambda t: (t, 0)) for o in outs]
            + [pl.BlockSpec((1, w), lambda t: (0, 0)) for w in reds],
            out_specs=d_row_specs + [vec_spec(j) for j in range(nv)],
            out_shape=d_row_shapes + [jax.ShapeDtypeStruct(v.shape, F32) for v in vecs],
            compiler_params=_params(("arbitrary",)), name=name + "_b",
        )(*row_args(rv), *rv[nr:], *cts)
        d_rows, pos = [], 0
        for i in range(nr):
            if halo[i]:
                d_rows.append(_unshift(res[pos], res[pos + 1], res[pos + 2], ctx_tiles * tr).astype(rows[i].dtype))
                pos += 3
            else:
                d_rows.append(res[pos])
                pos += 1
        return tuple(d_rows) + tuple(res[pos:])

    @jax.custom_vjp
    def f(*rv):
        return fwd_call(*rv)

    f.defvjp(lambda *rv: (fwd_call(*rv), rv), lambda rv, cts: bwd_call(rv, cts))
    return f(*rows, *vecs)


def _unshift(d, dm, dp, ctx_rows):
    r = d.shape[0]
    t = lax.broadcasted_iota(jnp.int32, (r, 1), 0)
    zero = jnp.zeros((1, d.shape[1]), d.dtype)
    from_m = jnp.concatenate([dm[1:], zero], axis=0)
    from_p = jnp.concatenate([zero, dp[:-1]], axis=0)
    from_m = jnp.where(t == ctx_rows - 1, 0.0, from_m)
    from_p = jnp.where(t == ctx_rows, 0.0, from_p)
    return d + from_m + from_p


def _sigmoid(x):
    return 0.5 * (jnp.tanh(0.5 * x) + 1.0)


def _silu(x):
    return x * _sigmoid(x)


def _softplus(x):
    return jnp.maximum(x, 0.0) + jnp.log(1.0 + jnp.exp(-jnp.abs(x)))


def _rms(x, g):
    return x * lax.rsqrt(jnp.mean(x * x, axis=-1, keepdims=True) + EPS) * g


def _fn_modulate(r, v):
    (x,), (g,), (sh,), (sc,) = r[0], v[0], v[1], v[2]
    return [[_rms(x, g) * (1.0 + sc) + sh]], []


def _fn_resmod(r, v):
    (x,), (y,) = r
    (gate,), (g,), (sh,), (sc,) = v
    xn = x + gate * y
    return [[xn], [_rms(xn, g) * (1.0 + sc) + sh]], []


def _fn_final(r, v):
    (x,), (y,), (tgt,) = r
    (gate,), (g,), (counts,) = v
    err = _rms(x + gate * y, g) - tgt
    row_loss = jnp.mean(err * err, axis=-1, keepdims=True)
    total = 0.5 * jnp.sum(row_loss, axis=0, keepdims=True)
    return [], [total * counts]


def _fn_conv(r, v):
    (x, xm, xp), = r[0]
    (w0,), (w1,), (w2,) = v
    return [[xm * w0 + x * w1 + xp * w2]], []


def _fn_conv_taps(r, v):
    (dy,), ((x, xm, xp),) = r
    col = lambda a: jnp.sum(dy * a, axis=0, keepdims=True)
    return [], [col(xm), col(x), col(xp)]


def conv3(name, x, w0, w1, w2, ctx_tiles, out_dtype):
    width = x.shape[1]

    def conv(tag, x, taps, dtype):
        return rowwise(name + tag, _fn_conv, [x], list(taps), halo=(True,), outs=[(width, dtype, 1)], ctx_tiles=ctx_tiles)[0]

    @jax.custom_vjp
    def f(x, w0, w1, w2):
        return conv("_y", x, (w0, w1, w2), out_dtype)

    def bwd(res, dy):
        x, w0, w1, w2 = res
        taps = rowwise(name + "_dw", _fn_conv_taps, [dy, x], [], halo=(False, True), reds=[width] * 3, ctx_tiles=ctx_tiles)
        return (conv("_dx", dy, (w2, w1, w0), x.dtype),) + tuple(taps)

    f.defvjp(lambda x, w0, w1, w2: (conv("_y", x, (w0, w1, w2), out_dtype), (x, w0, w1, w2)), bwd)
    return f(x, w0, w1, w2)


def _fn_dnprep(r, v):
    qkv, (ab,) = r
    (alog,), (dtb,) = v
    out = [[], [], []]
    for n, x in enumerate(qkv):
        which = n // DN_HEADS
        y = _silu(x)
        if which < 2:
            y = y * lax.rsqrt(jnp.sum(y * y, axis=-1, keepdims=True) + EPS)
        if which == 0:
            y = y * (HEAD_DIM ** -0.5)
        out[which].append(y)
    lane = lax.broadcasted_iota(jnp.int32, ab.shape, 1)
    g = -jnp.exp(alog) * _softplus(ab + dtb)
    gb = jnp.where(lane < N_DIRHEAD, g, jnp.where(lane < 2 * N_DIRHEAD, _sigmoid(ab), 0.0))
    return out + [[gb]], []


def _fn_dnpost(r, v):
    of, ob, z = r
    (g,) = v[0]
    return [[_rms(a + b, g) * _silu(c) for a, b, c in zip(of, ob, z)]], []


def _fn_sub(r, v):
    return [[r[0][0] - r[1][0]]], []


def _fn_scale(r, v):
    return [[r[0][0] * v[0][0]]], []


def _fn_shortconv(r, v):
    (xin, gb, gc), = r
    (w0,), (w1,), (w2,) = v
    u, um, up = (gc[k] * xin[k] for k in range(3))
    return [[gb[0] * (um * w0 + u * w1 + up * w2)]], []


def _fn_merge(r, v):
    gates, (ya,), (yb,), (yc,) = r
    return [[_sigmoid(gates[0]) * ya + _sigmoid(gates[1]) * yb + _sigmoid(gates[2]) * yc]], []


def _fn_swiglu(r, v):
    gate, up = r[0]
    return [[_silu(gate) * up]], []


def _dot3(a, b):
    (ah, al), (bh, bl) = _hi_lo(a), _hi_lo(b)
    dot = lambda x, y: lax.dot_general(x, y, (NN, ((), ())), preferred_element_type=F32)
    return dot(ah, bh) + (dot(ah, bl) + dot(al, bh))


def _bdot(a, b, dims):
    (ca,), (cb,) = dims
    return lax.dot_general(a.astype(MXU_DTYPE), b.astype(MXU_DTYPE), (((ca + 1,), (cb + 1,)), ((0,), (0,))),
                           preferred_element_type=F32)


def _hi_lo(a):
    hi = a.astype(MXU_DTYPE)
    return hi, (a - hi.astype(F32)).astype(MXU_DTYPE)


def _bdot3_raw(a, b, dims):
    (ah, al), (bh, bl) = _hi_lo(a), _hi_lo(b)
    return _bdot(ah, bh, dims) + (_bdot(ah, bl, dims) + _bdot(al, bh, dims))


@jax.custom_vjp
def _bdot3(a, b):
    return _bdot3_raw(a, b, NN)


_bdot3.defvjp(lambda a, b: (_bdot3_raw(a, b, NN), (a, b)),
              lambda res, ct: (_bdot3_raw(ct, res[1], NT), _bdot3_raw(res[0], ct, TN)))


def _inv_doubling(a):
    c = a.shape[-1]
    ii, jj = (lax.broadcasted_iota(jnp.int32, a.shape, d) for d in (1, 2))
    t = jnp.where(ii == jj, 1.0, 0.0) - a
    p = _bdot3_raw(a, a, NN)
    for _ in range(int(math.log2(c)) - 2):
        both = _bdot3_raw(jnp.concatenate([p, t], axis=1), p, NN)
        t = t + both[:, c:]
        p = both[:, :c]
    return t + _bdot3_raw(t, p, NN)


def _dn_gates(k4, gb):
    nb, c = N_DIRHEAD, k4.shape[1]
    k = jnp.concatenate([k4, k4], axis=0)
    lane = lax.broadcasted_iota(jnp.int32, gb.shape, 1)
    col = lambda j: jnp.sum(jnp.where(lane == j, gb, 0.0), axis=1, keepdims=True)
    g_col = jnp.concatenate([col(j)[None] for j in range(nb)], axis=0)
    b_col = jnp.concatenate([col(nb + j)[None] for j in range(nb)], axis=0)
    bi, ii, jj = (lax.broadcasted_iota(jnp.int32, (nb, c, c), a) for a in range(3))
    ahead = jnp.where(bi >= DN_HEADS, jj - ii, ii - jj)
    incl = ahead >= 0
    g_row = jnp.sum(jnp.where(ahead == 0, g_col, 0.0), axis=1, keepdims=True)
    gc_col = jnp.sum(jnp.where(incl, g_row, 0.0), axis=2, keepdims=True)
    gc_row = jnp.sum(jnp.where(ahead <= 0, g_col, 0.0), axis=1, keepdims=True)
    decay = jnp.where(incl, jnp.exp(jnp.where(incl, gc_col - gc_row, 0.0)), 0.0)
    return k, g_col, b_col, gc_col, decay, ahead > 0


def _dn_a(k4, gb):
    k, _, b_col, _, decay, strict = _dn_gates(k4, gb)
    return _bdot(k * b_col, k, NT) * jnp.where(strict, decay, 0.0)


def _dn_operands(q4, k4, v4, gb, t):
    k, g_col, b_col, gc_col, decay, _ = _dn_gates(k4, gb)
    q, v = (jnp.concatenate([a, a], axis=0) for a in (q4, v4))
    e_gc = jnp.exp(gc_col)
    d = k.shape[-1]
    uw = _bdot3(t, jnp.concatenate([v * b_col, k * b_col * e_gc], axis=2))
    u, w = uw[:, :, :d], uw[:, :, d:]
    g_last = jnp.sum(g_col, axis=1, keepdims=True)
    k_state = k * jnp.exp(g_last - gc_col)
    a_qk = _bdot(q, k, NT) * decay
    return u, w, q * e_gc, k_state, a_qk, jnp.broadcast_to(jnp.exp(g_last), (N_DIRHEAD, 1, LANES))


def _dn_step(s, u, w, qd, ks, aqk, gl):
    c = u.shape[1]
    on_state = _bdot(jnp.concatenate([w, qd], axis=1), s, NN)
    v_new = u - on_state[:, :c]
    o = on_state[:, c:] + _bdot(aqk, v_new, NN)
    return s * gl[:, :, :1] + _bdot(ks, v_new, TN), o


def _heads(x):
    return jnp.concatenate([x[None, :, h * HEAD_DIM:(h + 1) * HEAD_DIM] for h in range(DN_HEADS)], axis=0)


def _unheads(x):
    return jnp.concatenate([x[h] for h in range(x.shape[0])], axis=-1)


def _dn_rev(t, nc, n):
    return jnp.where(t < nc, nc - 1 - t, n - 1 - (t - nc))


PRE_CHUNKS = 4


def _pre_shapes(n):
    c, d, h = DN_CHUNK, HEAD_DIM, DN_HEADS
    shapes = [(n, h, c, d)] * 8 + [(n, h, c, c)] * 2 + [(n, h, 1, LANES)] * 2
    return shapes, [pl.BlockSpec((PRE_CHUNKS,) + s[1:], lambda t: (t, 0, 0, 0)) for s in shapes]


def _inverse_spec(n):
    shape = (n, N_DIRHEAD, DN_CHUNK, DN_CHUNK)
    return shape, pl.BlockSpec((PRE_CHUNKS,) + shape[1:], lambda t: (t, 0, 0, 0))


def _pre_row_specs():
    rows = PRE_CHUNKS * DN_CHUNK
    return pl.BlockSpec((rows, DN_HEADS * HEAD_DIM), lambda t: (t, 0)), pl.BlockSpec((rows, LANES), lambda t: (t, 0))


def _dn_pre_fwd(name, q, k, v, gb):
    n = q.shape[0] // DN_CHUNK
    shapes, specs = _pre_shapes(n)
    t_shape, t_spec = _inverse_spec(n)

    def body(q_ref, k_ref, v_ref, g_ref, *o_refs):
        for s in range(PRE_CHUNKS):
            rows = pl.ds(s * DN_CHUNK, DN_CHUNK)
            q4, k4, v4, gb_ = _heads(q_ref[rows, :]), _heads(k_ref[rows, :]), _heads(v_ref[rows, :]), g_ref[rows, :]
            t = _inv_doubling(_dn_a(k4, gb_))
            for i, r in enumerate(_dn_operands(q4, k4, v4, gb_, t)):
                o_refs[2 * i][s] = r[:DN_HEADS]
                o_refs[2 * i + 1][s] = r[DN_HEADS:]
            o_refs[-1][s] = t

    wide, narrow = _pre_row_specs()
    return pl.pallas_call(body, grid=(n // PRE_CHUNKS,), in_specs=[wide] * 3 + [narrow], out_specs=specs + [t_spec],
                          out_shape=[jax.ShapeDtypeStruct(s, F32) for s in shapes + [t_shape]],
                          compiler_params=_params(("parallel",)), name=name + "_pre_f")(q, k, v, gb)


def _dn_pre_bwd(name, q, k, v, gb, inv, cts):
    n = q.shape[0] // DN_CHUNK
    _, specs = _pre_shapes(n)
    _, t_spec = _inverse_spec(n)
    n_ct = len(specs)

    def body(q_ref, k_ref, v_ref, g_ref, t_ref, *refs):
        for s in range(PRE_CHUNKS):
            rows = pl.ds(s * DN_CHUNK, DN_CHUNK)
            ct = tuple(jnp.concatenate([refs[i][s], refs[i + 1][s]], axis=0) for i in range(0, n_ct, 2))
            q4, k4, v4, gb_, t = _heads(q_ref[rows, :]), _heads(k_ref[rows, :]), _heads(v_ref[rows, :]), g_ref[rows, :], t_ref[s]
            _, vjp = jax.vjp(_dn_operands, q4, k4, v4, gb_, t)
            dq, dk, dv, dg, dt = vjp(ct)
            da = -_bdot3_raw(_bdot3_raw(t, dt, TN), t, NT)
            _, vjp_a = jax.vjp(_dn_a, k4, gb_)
            dk_a, dg_a = vjp_a(da)
            for r, val in zip(refs[n_ct:], (_unheads(dq), _unheads(dk + dk_a), _unheads(dv), dg + dg_a)):
                r[rows, :] = val

    wide, narrow = _pre_row_specs()
    return pl.pallas_call(body, grid=(n // PRE_CHUNKS,), in_specs=[wide] * 3 + [narrow, t_spec] + specs,
                          out_specs=[wide] * 3 + [narrow],
                          out_shape=[jax.ShapeDtypeStruct(q.shape, F32)] * 3 + [jax.ShapeDtypeStruct(gb.shape, F32)],
                          compiler_params=_params(("parallel",)), name=name + "_pre_b")(q, k, v, gb, inv, *cts)


SCAN_CHUNKS = 4


def _scan_specs(pre_shapes, fw, bw):
    maps = (lambda t: (fw(t), 0, 0, 0), lambda t: (bw(t), 0, 0, 0))
    return [pl.BlockSpec((SCAN_CHUNKS,) + s[1:], maps[i % 2]) for i, s in enumerate(pre_shapes)]


def _dn_scan_fwd(name, pre, nc):
    n = pre[0].shape[0]
    r = n * DN_CHUNK
    hd, nh, nb, sc = HEAD_DIM, DN_HEADS, N_DIRHEAD, SCAN_CHUNKS
    assert n % sc == 0 and nc % sc == 0
    shapes, _ = _pre_shapes(n)
    n_in = len(shapes)

    def body(*refs):
        ins, (of_ref, ob_ref, sf_ref, sb_ref, s_scr) = refs[:n_in], refs[n_in:]
        t = pl.program_id(0)

        @pl.when(t == 0)
        def _():
            s_scr[...] = jnp.zeros_like(s_scr)

        s = s_scr[...]
        for j in range(sc):
            jf, jb = j, sc - 1 - j
            sf_ref[jf] = s[:nh]
            sb_ref[jb] = s[nh:]
            args = [jnp.concatenate([ins[i][jf], ins[i + 1][jb]], axis=0) for i in range(0, n_in, 2)]
            s, o = _dn_step(s, *args)
            of_ref[pl.ds(jf * DN_CHUNK, DN_CHUNK), :] = _unheads(o[:nh])
            ob_ref[pl.ds(jb * DN_CHUNK, DN_CHUNK), :] = _unheads(o[nh:])
        s_scr[...] = s

    fw = lambda t: t
    bw = lambda t: _dn_rev(sc * t + sc - 1, nc, n) // sc
    wide, st = (sc * DN_CHUNK, nh * hd), (sc, nh, hd, hd)
    return pl.pallas_call(
        body, grid=(n // sc,), in_specs=_scan_specs(shapes, fw, bw),
        out_specs=[pl.BlockSpec(wide, lambda t: (fw(t), 0)), pl.BlockSpec(wide, lambda t: (bw(t), 0)),
                   pl.BlockSpec(st, lambda t: (fw(t), 0, 0, 0)), pl.BlockSpec(st, lambda t: (bw(t), 0, 0, 0))],
        out_shape=[jax.ShapeDtypeStruct((r, nh * hd), F32)] * 2 + [jax.ShapeDtypeStruct((n, nh, hd, hd), F32)] * 2,
        scratch_shapes=[pltpu.VMEM((nb, hd, hd), F32)],
        compiler_params=_params(("arbitrary",)), name=name + "_scan_f",
    )(*pre)


def _dn_scan_bwd(name, pre, sall_f, sall_b, do_f, do_b, nc):
    n = pre[0].shape[0]
    hd, nh, nb, sc = HEAD_DIM, DN_HEADS, N_DIRHEAD, SCAN_CHUNKS
    shapes, _ = _pre_shapes(n)
    n_in = len(shapes)

    def body(*refs):
        ins, (sf_ref, sb_ref, dof_ref, dob_ref) = refs[:n_in], refs[n_in:n_in + 4]
        outs, ds_scr = refs[n_in + 4:2 * n_in + 4], refs[2 * n_in + 4]
        t = pl.program_id(0)

        @pl.when(t == 0)
        def _():
            ds_scr[...] = jnp.zeros_like(ds_scr)

        ds = ds_scr[...]
        for j in range(sc):
            jf, jb = sc - 1 - j, j
            args = [jnp.concatenate([ins[i][jf], ins[i + 1][jb]], axis=0) for i in range(0, n_in, 2)]
            s = jnp.concatenate([sf_ref[jf], sb_ref[jb]], axis=0)
            do = jnp.concatenate([_heads(dof_ref[pl.ds(jf * DN_CHUNK, DN_CHUNK), :]),
                                  _heads(dob_ref[pl.ds(jb * DN_CHUNK, DN_CHUNK), :])], axis=0)
            _, vjp = jax.vjp(_dn_step, s, *args)
            cts = vjp((ds, do))
            ds = cts[0]
            for i, ct in enumerate(cts[1:]):
                outs[2 * i][jf] = ct[:nh]
                outs[2 * i + 1][jb] = ct[nh:]
        ds_scr[...] = ds

    fw = lambda t: n // sc - 1 - t
    bw = lambda t: _dn_rev(n - 1 - sc * t, nc, n) // sc
    wide, st = (sc * DN_CHUNK, nh * hd), (sc, nh, hd, hd)
    return tuple(pl.pallas_call(
        body, grid=(n // sc,),
        in_specs=_scan_specs(shapes, fw, bw)
        + [pl.BlockSpec(st, lambda t: (fw(t), 0, 0, 0)), pl.BlockSpec(st, lambda t: (bw(t), 0, 0, 0)),
           pl.BlockSpec(wide, lambda t: (fw(t), 0)), pl.BlockSpec(wide, lambda t: (bw(t), 0))],
        out_specs=_scan_specs(shapes, fw, bw), out_shape=[jax.ShapeDtypeStruct(s, F32) for s in shapes],
        scratch_shapes=[pltpu.VMEM((nb, hd, hd), F32)],
        compiler_params=_params(("arbitrary",)), name=name + "_scan_b",
    )(*pre, sall_f, sall_b, do_f, do_b))


def deltanet(name, q, k, v, gb, ctx_rows):
    nc = ctx_rows // DN_CHUNK

    @jax.custom_vjp
    def pre(q, k, v, gb):
        return tuple(_dn_pre_fwd(name, q, k, v, gb)[:-1])

    def pre_fwd(*a):
        *ops, inv = _dn_pre_fwd(name, *a)
        return tuple(ops), (a, inv)

    pre.defvjp(pre_fwd, lambda res, cts: tuple(_dn_pre_bwd(name, *res[0], res[1], cts)))

    @jax.custom_vjp
    def scan(*ops):
        return tuple(_dn_scan_fwd(name, ops, nc)[:2])

    def scan_fwd(*ops):
        of, ob, sf, sb = _dn_scan_fwd(name, ops, nc)
        return (of, ob), (ops, sf, sb)

    scan.defvjp(scan_fwd, lambda res, cts: _dn_scan_bwd(name, res[0], res[1], res[2], cts[0], cts[1], nc))
    return scan(*pre(q, k, v, gb))


def _box_matrix(l, w):
    lo, hi = w // 2, w - 1 - w // 2
    pos = np.arange(l)
    start, end = np.clip(pos - lo, 0, l), np.clip(pos + hi + 1, 0, l)
    col = np.arange(l)[None, :]
    return ((col >= start[:, None]) & (col < end[:, None])) / (end - start)[:, None].astype(np.float64)


def _pool_matrices(ctx_rows, grid_rows):
    assert ctx_rows == ROW_TILE and ROW_TILE % GRID_W == 0
    ctx = np.stack([_box_matrix(ctx_rows, w) for w in POOL_WINDOWS])
    cols = np.stack([np.kron(np.eye(ROW_TILE // GRID_W), _box_matrix(GRID_W, w)) for w in POOL_WINDOWS])
    rows = np.stack([_box_matrix(grid_rows, w) for w in POOL_WINDOWS])[None]
    return np.stack([ctx, cols]).astype(np.float32), rows.astype(np.float32)


def _pool_apply(name, x, mats, group_w, seg_tiles, lane_tile):
    r, w = x.shape
    b = mats.shape[-1]
    ng = mats.shape[1]
    period = ng * group_w

    def body(x_ref, m_ref, o_ref):
        xv = x_ref[...].astype(F32)
        lane = lax.broadcasted_iota(jnp.int32, xv.shape, 1)
        grp = (lane % period) // group_w
        acc = jnp.zeros_like(xv)
        for g in range(ng):
            acc = acc + jnp.where(grp == g, _dot3(m_ref[g], xv), 0.0)
        o_ref[...] = acc

    return pl.pallas_call(
        body, grid=(r // b, w // lane_tile),
        in_specs=[pl.BlockSpec((b, lane_tile), lambda i, j: (i, j)),
                  pl.BlockSpec((None, ng, b, b), lambda i, j: (jnp.where(i >= seg_tiles, 1, 0) if mats.shape[0] > 1 else 0, 0, 0, 0))],
        out_specs=pl.BlockSpec((b, lane_tile), lambda i, j: (i, j)),
        out_shape=jax.ShapeDtypeStruct(x.shape, F32),
        compiler_params=_params(("parallel", "parallel")), name=name,
    )(x, mats)


def pool_means(name, u, ctx_rows):
    r, pw = u.shape
    grid_rows = (r - ctx_rows) // GRID_W
    gw = pw // len(POOL_WINDOWS)
    m1, m2 = _pool_matrices(ctx_rows, grid_rows)
    lane_tile = min(2048, GRID_W * pw)

    def apply(x, a1, a2, tag):
        y = _pool_apply(name + tag + "1", x, jnp.asarray(a1), gw, ctx_rows // ROW_TILE, pw)
        lat = y[ctx_rows:].reshape(grid_rows, GRID_W * pw)
        lat = _pool_apply(name + tag + "2", lat, jnp.asarray(a2), gw, 0, lane_tile)
        return jnp.concatenate([y[:ctx_rows], lat.reshape(r - ctx_rows, pw)], axis=0)

    @jax.custom_vjp
    def f(u):
        return apply(u, m1, m2, "_f")

    tr = lambda m: np.ascontiguousarray(np.swapaxes(m, -1, -2))
    f.defvjp(lambda u: (apply(u, m1, m2, "_f"), None), lambda _, ct: (apply(ct, tr(m1), tr(m2), "_b").astype(u.dtype),))
    return f(u)


def _ew(name, fn, *xs):
    shapes = jax.eval_shape(lambda *a: fn(*a), *xs)
    shapes = shapes if isinstance(shapes, (tuple, list)) else (shapes,)

    def body(*refs):
        res = fn(*[r[...] for r in refs[:len(xs)]])
        res = res if isinstance(res, (tuple, list)) else (res,)
        for r, o in zip(res, refs[len(xs):]):
            o[...] = r

    out = pl.pallas_call(body, out_shape=[jax.ShapeDtypeStruct(s.shape, s.dtype) for s in shapes],
                         compiler_params=_params(), name=name)(*xs)
    return out[0] if len(shapes) == 1 else tuple(out)


def _adamw_math(w, g, m, v):
    m2 = ADAM_B1 * m + (1.0 - ADAM_B1) * g
    v2 = ADAM_B2 * v + (1.0 - ADAM_B2) * (g * g)
    m_hat = m2 / (1.0 - ADAM_B1 ** ADAM_STEP)
    v_hat = v2 / (1.0 - ADAM_B2 ** ADAM_STEP)
    delta = -ADAM_LR * (m_hat / (jnp.sqrt(v_hat) + ADAM_EPS) + ADAM_WD * w)
    return delta, m2, v2


def adamw(name, w, g, m, v):
    *lead, r, c = w.shape
    tr = _pick(r, (256, 128, 64, 32, 16, 8))

    def body(w_ref, g_ref, m_ref, v_ref, d_ref, m2_ref, v2_ref):
        d_ref[...], m2_ref[...], v2_ref[...] = _adamw_math(w_ref[...], g_ref[...], m_ref[...], v_ref[...])

    if lead:
        grid, spec = (lead[0], r // tr), pl.BlockSpec((None, tr, c), lambda l, i: (l, i, 0))
    else:
        grid, spec = (r // tr,), pl.BlockSpec((tr, c), lambda i: (i, 0))
    return pl.pallas_call(body, grid=grid, in_specs=[spec] * 4, out_specs=[spec] * 3,
                          out_shape=[jax.ShapeDtypeStruct(w.shape, F32)] * 3,
                          compiler_params=_params(("parallel",) * len(grid)), name=name)(w, g, m, v)


def _sum_rows(name, xs, out_dtype=F32):
    r, c = xs[0].shape
    tr = _pick(r, tuple(t for t in (512, 256, 128, 64, 32, 16, 8) if t * c * 4 <= 2 << 20))

    def body(*refs):
        acc = refs[0][...].astype(F32)
        for ref in refs[1:-1]:
            acc = acc + ref[...].astype(F32)
        refs[-1][...] = acc.astype(out_dtype)

    spec = pl.BlockSpec((tr, c), lambda i: (i, 0))
    return pl.pallas_call(body, grid=(r // tr,), in_specs=[spec] * len(xs), out_specs=spec,
                          out_shape=jax.ShapeDtypeStruct((r, c), out_dtype),
                          compiler_params=_params(("parallel",)), name=name)(*xs)


def _place():
    return lax.axis_index("x"), lax.axis_index("y"), lax.axis_index("c")


def _chip_peers(x, y, c):
    return [(1 - x, y, c), (x, 1 - y, c), (1 - x, 1 - y, c)]


def all_gather8(name, block):
    m_per, n = block.shape

    def body(x_ref, out_ref, send_sems, recv_sems, local_sem):
        x, y, c = _place()
        me, sibling = (x, y, c), (x, y, 1 - c)
        chips = [(1 - x, y), (x, 1 - y), (1 - x, 1 - y)]

        def rows(px, py, pc):
            return out_ref.at[pl.ds((4 * px + 2 * py + pc) * m_per, m_per), :]

        def copy(k, blk, to, src=None):
            return pltpu.make_async_remote_copy(
                src_ref=rows(*blk) if src is None else src, dst_ref=rows(*blk),
                send_sem=send_sems.at[k], recv_sem=recv_sems.at[k], device_id=to, device_id_type=MESH_ID)

        mine = pltpu.make_async_copy(x_ref, rows(*me), local_sem)
        mine.start()
        first = [copy(0, me, sibling, src=x_ref)]
        first += [copy(1 + j, me, (*chip, c), src=x_ref) for j, chip in enumerate(chips)]
        for cp in first:
            cp.start()
        passed = [copy(4 + j, (*chip, c), sibling) for j, chip in enumerate(chips)]
        for j, chip in enumerate(chips):
            copy(1 + j, (*chip, c), me).wait_recv()
            passed[j].start()
        copy(0, sibling, me).wait_recv()
        for j, chip in enumerate(chips):
            copy(4 + j, (*chip, 1 - c), me).wait_recv()
        for cp in first + passed:
            cp.wait_send()
        mine.wait()

    return pl.pallas_call(
        body, out_shape=jax.ShapeDtypeStruct((8 * m_per, n), block.dtype),
        in_specs=[pl.BlockSpec(memory_space=pltpu.VMEM)], out_specs=pl.BlockSpec(memory_space=pltpu.VMEM),
        scratch_shapes=[pltpu.SemaphoreType.DMA((7,)), pltpu.SemaphoreType.DMA((7,)), pltpu.SemaphoreType.DMA],
        compiler_params=_params(), name=name,
    )(block)


N_PEERS = 3


def gather_chips(name, shards):
    n = len(shards)

    def body(*refs):
        xs, outs, (send_sems, recv_sems, local_sems) = refs[:n], refs[n:2 * n], refs[2 * n:]
        x, y, c = _place()
        local, remote = [], []
        for i in range(n):
            local.append(pltpu.make_async_copy(xs[i].at[c], outs[i].at[2 * x + y], local_sems.at[i]))
            local[-1].start()
            for p, peer in enumerate(_chip_peers(x, y, c)):
                remote.append(pltpu.make_async_remote_copy(
                    src_ref=xs[i].at[c], dst_ref=outs[i].at[2 * x + y], send_sem=send_sems.at[i * N_PEERS + p],
                    recv_sem=recv_sems.at[i * N_PEERS + p], device_id=peer, device_id_type=MESH_ID))
                remote[-1].start()
        for i in range(n):
            for p, (px, py, _) in enumerate(_chip_peers(x, y, c)):
                pltpu.make_async_remote_copy(
                    src_ref=xs[i].at[c], dst_ref=outs[i].at[2 * px + py], send_sem=send_sems.at[i * N_PEERS + p],
                    recv_sem=recv_sems.at[i * N_PEERS + p], device_id=(px, py, c), device_id_type=MESH_ID).wait_recv()
        for cp in remote:
            cp.wait_send()
        for cp in local:
            cp.wait()

    hbm = pl.BlockSpec(memory_space=pltpu.HBM)
    return pl.pallas_call(
        body, out_shape=[jax.ShapeDtypeStruct((N_CHIPS,) + s.shape[1:], s.dtype) for s in shards],
        in_specs=[hbm] * n, out_specs=[hbm] * n,
        scratch_shapes=[pltpu.SemaphoreType.DMA((n * N_PEERS,)), pltpu.SemaphoreType.DMA((n * N_PEERS,)),
                        pltpu.SemaphoreType.DMA((n,))],
        compiler_params=_params(), name=name,
    )(*shards)


def swap_sibling(name, blocks):
    n = len(blocks)
    pairs = [tuple(b) if isinstance(b, (tuple, list)) else (b,) for b in blocks]
    flat = [a for p in pairs for a in p]
    first = [sum(len(q) for q in pairs[:i]) for i in range(n)]

    def body(*refs):
        xs, outs, (send_sems, recv_sems) = refs[:len(flat)], refs[len(flat):len(flat) + n], refs[len(flat) + n:]
        x, y, c = _place()

        def copy(i, src):
            return pltpu.make_async_remote_copy(src_ref=src, dst_ref=outs[i], send_sem=send_sems.at[i], recv_sem=recv_sems.at[i],
                                                device_id=(x, y, 1 - c), device_id_type=MESH_ID)

        for i, p in enumerate(pairs):
            if len(p) == 1:
                copy(i, xs[first[i]]).start()
            else:
                for half in range(2):
                    pl.when(c == 1 - half)(copy(i, xs[first[i] + half]).start)
        for i in range(n):
            copy(i, xs[first[i]]).wait()

    hbm = pl.BlockSpec(memory_space=pltpu.HBM)
    return pl.pallas_call(
        body, out_shape=[jax.ShapeDtypeStruct(p[0].shape, p[0].dtype) for p in pairs],
        in_specs=[hbm] * len(flat), out_specs=[hbm] * n,
        scratch_shapes=[pltpu.SemaphoreType.DMA((n,)), pltpu.SemaphoreType.DMA((n,))],
        compiler_params=_params(), name=name,
    )(*flat)


def scatter_chips(name, pieces):
    n = len(pieces)

    def body(*refs):
        xs, outs, (send_sems, recv_sems) = refs[:n], refs[n:2 * n], refs[2 * n:]
        x, y, c = _place()
        copies = []
        for i in range(n):
            for p, (px, py, pc) in enumerate(_chip_peers(x, y, c)):
                copies.append(pltpu.make_async_remote_copy(
                    src_ref=xs[i].at[2 * px + py], dst_ref=outs[i].at[p], send_sem=send_sems.at[i * N_PEERS + p],
                    recv_sem=recv_sems.at[i * N_PEERS + p], device_id=(px, py, pc), device_id_type=MESH_ID))
                copies[-1].start()
        for cp in copies:
            cp.wait()

    hbm = pl.BlockSpec(memory_space=pltpu.HBM)
    return pl.pallas_call(
        body, out_shape=[jax.ShapeDtypeStruct((N_PEERS,) + p.shape[1:], p.dtype) for p in pieces],
        in_specs=[hbm] * n, out_specs=[hbm] * n,
        scratch_shapes=[pltpu.SemaphoreType.DMA((n * N_PEERS,)), pltpu.SemaphoreType.DMA((n * N_PEERS,))],
        compiler_params=_params(), name=name,
    )(*pieces)


def pair_sum(name, half0, half1, got, out_dtype):
    r, w = got.shape
    tr = _pick(r, tuple(t for t in (512, 256, 128, 64, 32, 16, 8) if t * w * 4 <= 2 << 20))

    def body(a0, a1, g, o):
        mine = jnp.where(lax.axis_index("c") == 0, a0[...], a1[...])
        o[...] = (mine.astype(F32) + g[...].astype(F32)).astype(out_dtype)

    spec = pl.BlockSpec((tr, w), lambda i: (i, 0))
    return pl.pallas_call(body, grid=(r // tr,), in_specs=[spec] * 3, out_specs=spec,
                          out_shape=jax.ShapeDtypeStruct((r, w), out_dtype),
                          compiler_params=_params(("parallel",)), name=name)(half0, half1, got)


def _sum_devices(name, got, fold=False):
    def body(a_ref, *o_refs):
        acc = a_ref[0]
        for i in range(1, N_DEV):
            acc = acc + a_ref[i]
        o_refs[0][...] = acc
        if fold:
            o_refs[1][...] = acc + pltpu.roll(acc, SUBLANES // 2, 0)

    shape = jax.ShapeDtypeStruct(got.shape[1:], F32)
    out = pl.pallas_call(body, out_shape=[shape] * (2 if fold else 1), compiler_params=_params(), name=name)(got)
    return out if fold else out[0]


def _w_in_bounds(d, pw, scw):
    off_z = 3 * DN_WIDTH
    off_a = off_z + DN_WIDTH
    off_pool = off_a + 2 * N_DIRHEAD
    off_sc = off_pool + pw
    off_gate = off_sc + 3 * scw
    return (0, off_z, off_a, off_pool, off_sc, off_gate, off_gate + 3 * d)


def _misc_widths(pw, scw):
    return (DN_WIDTH, pw, 3 * scw, LANES)


def split_matrices(full, pw, scw):
    depth, d = len(full["w_in"]), full["w_in"][0].shape[0]
    b = _w_in_bounds(d, pw, scw)
    out = {k: [] for k in ("qkv", "gate", "misc", "gu", "br_a", "br_b", "br_c", "o", "down")}
    for l in range(depth):
        w = full["w_in"][l]
        out["qkv"].append(w[:, b[0]:b[1]])
        out["gate"].append(w[:, b[5]:b[6]])
        out["misc"].append(jnp.concatenate([w[:, b[1]:b[2]], w[:, b[3]:b[5]], _pad_lanes(w[:, b[2]:b[3]])], axis=1))
        out["gu"].append(full["w_gu"][l])
        for k in ("br_a", "br_b", "br_c", "o", "down"):
            out[k].append(full["w_" + k][l])
    return out


def join_matrix_grads(g, pw, scw):
    depth = len(g["qkv"])
    n_z, n_rest, n_ab = DN_WIDTH, pw + 3 * scw, 2 * N_DIRHEAD
    w_in = []
    for l in range(depth):
        m = g["misc"][l]
        w_in.append(jnp.concatenate([g["qkv"][l], m[:, :n_z], m[:, n_z + n_rest:n_z + n_rest + n_ab],
                                     m[:, n_z:n_z + n_rest], g["gate"][l]], axis=1))
    out = {"w_in": w_in, "w_gu": g["gu"]}
    for k in ("br_a", "br_b", "br_c", "o", "down"):
        out["w_" + k] = g[k]
    return {k: [a.astype(MXU_DTYPE) for a in v] for k, v in out.items()}


def split_cols(x, widths):
    edges = np.cumsum((0,) + tuple(widths))

    def cut(x):
        return tuple(x[:, a:b] for a, b in zip(edges[:-1], edges[1:]))

    f = jax.custom_vjp(cut)
    f.defvjp(lambda x: (cut(x), None), lambda _, cts: (jnp.concatenate(cts, axis=1),))
    return f(x)


def _row(v):
    return v.reshape(1, -1)


def _pad_lanes(v, width=LANES):
    return jnp.pad(v, ((0, 0), (0, width - v.shape[1])))


def _block_diag(blocks):
    g, n, _ = blocks.shape
    out = jnp.zeros((g * n, g * n), blocks.dtype)
    for i in range(g):
        out = out.at[i * n:(i + 1) * n, i * n:(i + 1) * n].set(blocks[i])
    return out


def local_loss(p, carriers, x, mod_lat, mod_ctx, wts, ctx, target):
    ctx_rows, d = ctx.shape
    depth = len(wts["qkv"])
    pw, scw = p["pool_scale"].shape[1], p["sc_conv_w"].shape[2]
    ff = wts["down"][0].shape[0]
    ct = ctx_rows // ROW_TILE
    xs = jnp.concatenate([ctx, x], axis=0)
    seg = lambda l, k: jnp.stack([mod_ctx[l, k], mod_lat[l, k]]).reshape(2, 1, d)
    dn = gate2 = None
    for l in range(depth):
        sh1, sc1, g1, sh2, sc2, g2 = (seg(l, k) for k in range(6))
        tag = f"l{l}_"
        lin = lambda key, a: mm(tag + key, a, wts[key][l], carriers[key][l])
        if l == 0:
            (h1,) = rowwise(tag + "mod", _fn_modulate, [xs], [_row(p["norm1_g"][l]), sh1, sc1], seg=(False, True, True),
                            outs=[(d, MXU_DTYPE, 1)], ctx_tiles=ct)
        else:
            xs, h1 = rowwise(tag + "resmod1", _fn_resmod, [xs, dn], [gate2, _row(p["norm1_g"][l]), sh1, sc1],
                             seg=(True, False, True, True), outs=[(d, F32, 1), (d, MXU_DTYPE, 1)], ctx_tiles=ct)
        fan = ("qkv", "gate", "misc")
        p_qkv, p_gate, p_misc = mm_fanout(tag + "in", h1, [wts[k][l] for k in fan], [carriers[k][l] for k in fan])
        p_z, p_pool, p_sc, p_ab = split_cols(p_misc, _misc_widths(pw, scw))
        cw = p["dn_conv_w"][l]
        qkv_conv = conv3(tag + "dnconv", p_qkv, cw[0:1], cw[1:2], cw[2:3], ct, F32)
        q, k, v, gb = rowwise(
            tag + "dnprep", _fn_dnprep, [qkv_conv, p_ab],
            [_pad_lanes(p["dn_a_log"][l].reshape(1, -1)), _pad_lanes(p["dn_dt_bias"][l].reshape(1, -1))],
            parts_r=(3 * DN_HEADS, 1), outs=[(DN_WIDTH, F32, DN_HEADS)] * 3 + [(LANES, F32, 1)], ctx_tiles=ct)
        o_f, o_b = deltanet(tag + "dn", q, k, v, gb, ctx_rows)
        (oz,) = rowwise(tag + "dnpost", _fn_dnpost, [o_f, o_b, p_z], [_row(p["dn_norm_g"][l])], parts_r=(DN_HEADS,) * 3,
                        outs=[(DN_WIDTH, MXU_DTYPE, DN_HEADS)], long_tiles=768)
        y_a = lin("br_a", oz)
        means = pool_means(tag + "box", p_pool, ctx_rows)
        (dpool,) = rowwise(tag + "poolsub", _fn_sub, [means, p_pool], [], outs=[(pw, MXU_DTYPE, 1)], long_tiles=2816)
        pool_mat = _block_diag(p["pool_w"][l])
        mixed = mm(tag + "poolw", dpool, pool_mat.astype(MXU_DTYPE), pool_mat)
        (yb_in,) = rowwise(tag + "poolscale", _fn_scale, [mixed], [_row(p["pool_scale"][l])], outs=[(pw, MXU_DTYPE, 1)],
                           long_tiles=2816)
        y_b = lin("br_b", yb_in)
        sw = p["sc_conv_w"][l]
        (yc_in,) = rowwise(tag + "sconv", _fn_shortconv, [p_sc], [sw[0:1], sw[1:2], sw[2:3]], halo=(True,), parts_r=(3,),
                           outs=[(scw, MXU_DTYPE, 1)], ctx_tiles=ct)
        y_c = lin("br_c", yc_in)
        (y,) = rowwise(tag + "merge", _fn_merge, [p_gate, y_a, y_b, y_c], [], parts_r=(3, 1, 1, 1),
                       outs=[(d, MXU_DTYPE, 1)], ctx_tiles=ct)
        mix = lin("o", y)
        xs, h2 = rowwise(tag + "resmod2", _fn_resmod, [xs, mix], [g1, _row(p["norm2_g"][l]), sh2, sc2],
                         seg=(True, False, True, True), outs=[(d, F32, 1), (d, MXU_DTYPE, 1)], ctx_tiles=ct)
        (act,) = rowwise(tag + "swiglu", _fn_swiglu, [lin("gu", h2)], [], parts_r=(2,),
                         outs=[(ff, MXU_DTYPE, 1)], ctx_tiles=ct)
        dn = lin("down", act)
        gate2 = g2
    counts = jnp.concatenate([jnp.zeros((1, 1, LANES), F32), jnp.ones((1, 1, LANES), F32)])
    (total,) = rowwise("final", _fn_final, [xs, dn, target], [gate2, _row(p["final_norm_g"]), counts],
                       seg=(True, False, True), reds=[LANES], ctx_tiles=ct, skip=(0, 0, ct))
    return total[0, 0]


BIG = ("w_in", "w_br_a", "w_br_b", "w_br_c", "w_o", "w_gu", "w_down")
ROW_SHARDED = ("w_o", "w_down")
SMALL_REPL = ("norm1_g", "norm2_g", "dn_a_log", "dn_dt_bias", "dn_norm_g", "pool_w", "pool_scale", "final_norm_g")
SMALL_SHARD = ("dn_conv_w", "sc_conv_w")
WEIGHTS = ("c_ctx", "w_ada", "b_ada", "norm1_g", "norm2_g", "w_in", "dn_conv_w", "dn_a_log", "dn_dt_bias", "dn_norm_g", "pool_w",
           "pool_scale", "sc_conv_w", "w_br_a", "w_br_b", "w_br_c", "w_o", "w_gu", "w_down", "final_norm_g")
N_CHIPS, N_DEV = 4, 8
COMM_COLS = 1024


def _pack(arrays, rows_multiple, cols=COMM_COLS, dtype=F32):
    size = sum(int(np.prod(a.shape)) for a in arrays)
    rows = -(-size // (cols * rows_multiple)) * rows_multiple
    tail = [jnp.zeros((rows * cols - size,), dtype)] if rows * cols > size else []
    return jnp.concatenate([a.astype(dtype).reshape(-1) for a in arrays] + tail).reshape(rows, cols)


def _unpack(flat, shapes):
    out, pos = [], 0
    for s in shapes:
        n = int(np.prod(s))
        out.append(flat[pos:pos + n].reshape(s))
        pos += n
    return out


def _join_shards(parts, name):
    _, a, b = parts.shape
    if name in ROW_SHARDED:
        return parts.reshape(N_CHIPS * a, b)
    return jnp.moveaxis(parts, 0, 1).reshape(a, N_CHIPS * b)


def _cut_shards(full, name):
    k, n = full.shape
    if name in ROW_SHARDED:
        return full.reshape(N_CHIPS, k // N_CHIPS, n)
    return jnp.moveaxis(full.reshape(k, N_CHIPS, n // N_CHIPS), 1, 0)


def _dsilu(x):
    s = _sigmoid(x)
    return s + x * s * (1.0 - s)


def kernel(x, c, ctx, c_ctx, w_ada, b_ada, norm1_g, norm2_g, w_in, dn_conv_w, dn_a_log, dn_dt_bias, dn_norm_g, pool_w, pool_scale, sc_conv_w, w_br_a, w_br_b, w_br_c, w_o, w_gu, w_down, final_norm_g, loss_target, m_c_ctx, m_w_ada, m_b_ada, m_norm1_g, m_norm2_g, m_w_in, m_dn_conv_w, m_dn_a_log, m_dn_dt_bias, m_dn_norm_g, m_pool_w, m_pool_scale, m_sc_conv_w, m_w_br_a, m_w_br_b, m_w_br_c, m_w_o, m_w_gu, m_w_down, m_final_norm_g, v_c_ctx, v_w_ada, v_b_ada, v_norm1_g, v_norm2_g, v_w_in, v_dn_conv_w, v_dn_a_log, v_dn_dt_bias, v_dn_norm_g, v_pool_w, v_pool_scale, v_sc_conv_w, v_w_br_a, v_w_br_b, v_w_br_c, v_w_o, v_w_gu, v_w_down, v_final_norm_g):
    given = dict(locals())
    ix, iy, ic = _place()
    chip, dev = 2 * ix + iy, 4 * ix + 2 * iy + ic
    d = x.shape[-1]
    depth = w_in.shape[0]
    ada_cols = w_ada.shape[2]
    spare = 2 * SUBLANES - N_DEV - 1

    got0 = all_gather8("gather_cond", _pack([c, dn_conv_w, sc_conv_w], SUBLANES)).reshape(N_DEV, -1)
    c_all = got0[:, :d]
    taps = [_unpack(got0[2 * j, d:], [dn_conv_w.shape, sc_conv_w.shape]) for j in range(N_CHIPS)]
    full = {"dn_conv_w": jnp.concatenate([t[0] for t in taps], axis=-1),
            "sc_conv_w": jnp.concatenate([t[1] for t in taps], axis=-1)}
    assert depth == 2, "the two cores of a chip split the layers between them"
    got_layer = gather_chips("gather_weights", [given[n].astype(MXU_DTYPE) for n in BIG])
    other_layer = swap_sibling("swap_weights", got_layer)
    mats = {n: [_join_shards(jnp.where(ic == l, got_layer[i], other_layer[i]), n) for l in range(depth)]
            for i, n in enumerate(BIG)}
    for n in SMALL_REPL:
        full[n] = given[n]
    pw, scw = pool_scale.shape[1], full["sc_conv_w"].shape[2]
    wts = split_matrices(mats, pw, scw)
    carriers = {k: [jnp.zeros(w.shape, F32) for w in ws] for k, ws in wts.items()}

    cond = jnp.concatenate([c_all, c_ctx[None], jnp.zeros((spare, d), F32)])
    s_cond = _ew("silu_cond", _silu, cond)
    mod_cols = jnp.concatenate([_mm(f"ada{l}_f", s_cond, w_ada[l], NN) for l in range(depth)], axis=1)
    mod_got = all_gather8("gather_mod", mod_cols).reshape(N_DEV, 2 * SUBLANES, depth, ada_cols)
    mod_all = jnp.concatenate([mod_got[2 * j] for j in range(N_CHIPS)], axis=-1) + b_ada[None]
    mod_lat = lax.dynamic_index_in_dim(mod_all, dev, 0, keepdims=False).reshape(depth, 6, d)
    mod_ctx = mod_all[N_DEV].reshape(depth, 6, d)

    loss_local, (g_full, g_mats, grad_x, g_mod_lat, g_mod_ctx) = jax.value_and_grad(local_loss, argnums=(0, 1, 2, 3, 4))(
        full, carriers, x[0], mod_lat, mod_ctx, wts, ctx[0], loss_target[0])
    g_mats_full = join_matrix_grads(g_mats, pw, scw)
    loss = lax.psum(loss_local, ("x", "y", "c"))

    dmod_cols = (2 * depth * 6 * d) // SUBLANES
    dmod = all_gather8("gather_dmod", _pack([g_mod_lat, g_mod_ctx], SUBLANES, cols=dmod_cols))
    dmod = dmod.reshape(N_DEV, SUBLANES, dmod_cols)
    dmod_sum, dmod_fold = _sum_devices("reduce_dmod", dmod, fold=True)
    half_rows = SUBLANES // 2
    grad_b_ada = dmod_fold[:half_rows].reshape(depth, 6 * d)
    dctx_sum = dmod_sum[half_rows:].reshape(1, depth, 6 * d)
    d9 = jnp.concatenate([dmod[:, :half_rows].reshape(N_DEV, depth, 6 * d), dctx_sum, jnp.zeros((spare, depth, 6 * d), F32)])
    d9 = lax.dynamic_slice_in_dim(d9, chip * ada_cols, ada_cols, axis=2)
    grad_w_ada = jnp.stack([_mm(f"ada{l}_dw", s_cond, d9[:, l], TN) for l in range(depth)])
    ds_cond = [_mm(f"ada{l}_da", d9[:, l], w_ada[l], NT) for l in range(depth)]
    dsilu_part = _sum_rows("sum_dcond", ds_cond)[N_DEV]

    small_names = SMALL_REPL + SMALL_SHARD
    small_grads = [dsilu_part] + [g_full[n] for n in small_names]
    small_shapes = [a.shape for a in small_grads]
    n_small = sum(int(np.prod(s)) for s in small_shapes)
    cols = -(-n_small // (SUBLANES * LANES)) * LANES
    got2 = all_gather8("gather_small", _pack(small_grads, SUBLANES, cols=cols)).reshape(N_DEV, SUBLANES, cols)
    small_sum = _unpack(_sum_devices("reduce_small", got2).reshape(-1), small_shapes)
    grads = dict(zip(small_names, small_sum[1:]))
    grads["c_ctx"] = _ew("dsilu", lambda g, z: 0.5 * g * _dsilu(z), _row(small_sum[0]), _row(c_ctx)).reshape(-1)
    grads["w_ada"], grads["b_ada"] = grad_w_ada, grad_b_ada
    for n in SMALL_SHARD:
        width = given[n].shape[-1]
        grads[n] = lax.dynamic_slice_in_dim(grads[n], chip * width, width, axis=-1)

    from_sibling = swap_sibling("swap_layers", [tuple(g_mats_full[n]) for n in BIG])
    pieces = [_cut_shards(pair_sum("sum_pair_" + n, *g_mats_full[n], from_sibling[i], MXU_DTYPE), n) for i, n in enumerate(BIG)]
    from_chips = scatter_chips("scatter_pieces", pieces)
    reduced = []
    for i, n in enumerate(BIG):
        own = lax.dynamic_index_in_dim(pieces[i], chip, 0, keepdims=False)
        a, b = own.shape
        view = lambda t: t.reshape(-1, b)
        reduced.append(_sum_rows("sum_chips_" + n, [view(own)] + [view(from_chips[i][p]) for p in range(N_PEERS)]).reshape(a, b))
    other_reduced = swap_sibling("swap_reduced", reduced)
    for i, n in enumerate(BIG):
        grads[n] = jnp.stack([jnp.where(ic == l, reduced[i], other_reduced[i]) for l in range(depth)])

    delta, new_m, new_v = {}, {}, {}
    large = BIG + ("w_ada",)
    for n in large:
        delta[n], new_m[n], new_v[n] = adamw("adamw_" + n, given[n], grads[n], given["m_" + n], given["v_" + n])
    rest = [n for n in WEIGHTS if n not in large]
    rest_shapes = [given[n].shape for n in rest]
    packed = [_pack([src[pre + n] for n in rest], SUBLANES, cols=LANES)
              for src, pre in ((given, ""), (grads, ""), (given, "m_"), (given, "v_"))]
    for res, o in zip((delta, new_m, new_v), adamw("adamw_small", *packed)):
        for n, a in zip(rest, _unpack(o.reshape(-1), rest_shapes)):
            res[n] = a
    return (loss, grad_x[None], *[grads[n] for n in WEIGHTS], *[delta[n] for n in WEIGHTS],
            *[new_m[n] for n in WEIGHTS], *[new_v[n] for n in WEIGHTS])
```

```python
import math

import numpy as np
import jax
import jax.numpy as jnp
from jax import lax
from jax.experimental import pallas as pl
from jax.experimental.pallas import tpu as pltpu

F32 = jnp.float32
MXU_DTYPE = jnp.bfloat16

DN_HEADS = 4
HEAD_DIM = 128
DN_WIDTH = DN_HEADS * HEAD_DIM
DN_CHUNK = 64
GRID_W = 64
EPS = 1e-6
POOL_WINDOWS = (2, 4, 8, 16)
N_DIRHEAD = 2 * DN_HEADS
ADAM_LR, ADAM_B1, ADAM_B2, ADAM_EPS, ADAM_WD, ADAM_STEP = 0.001, 0.9, 0.999, 1e-08, 0.01, 10

LANES = 128
SUBLANES = 8
ROW_TILE = 256
VMEM_LIMIT = 56 * 1024 * 1024

MESH_ID = pl.DeviceIdType.MESH
NN, NT, TN = ((1,), (0,)), ((1,), (1,)), ((0,), (0,))


def _params(sem=None):
    return pltpu.CompilerParams(dimension_semantics=sem, vmem_limit_bytes=VMEM_LIMIT)


def _pick(n, cands):
    for c in cands:
        if c <= n and n % c == 0:
            return c
    return n


def _mm(name, a, b, dims, out_dtype=F32):
    if dims == NN:
        (m, kk), (_, n) = a.shape, b.shape
    elif dims == NT:
        (m, kk), (n, _) = a.shape, b.shape
    else:
        (kk, m), (_, n) = a.shape, b.shape
    tm = _pick(m, (768, 1024, 1408, 512, 256, 128, 64, 32, 16, 8))
    tn = _pick(n, ((3072, 2816) if dims == NN else ()) + (1536, 1024, 1408, 1664, 768, 896, 512, 256, 128))
    tk = kk if dims == NN and kk <= 2816 else _pick(kk, (1024, 1408, 768, 512, 256, 128))
    gi, gj, gl = m // tm, n // tn, kk // tk
    if dims == NN:
        a_spec = pl.BlockSpec((tm, tk), lambda i, j, l: (i, l))
        b_spec = pl.BlockSpec((tk, tn), lambda i, j, l: (l, j))
    elif dims == NT:
        a_spec = pl.BlockSpec((tm, tk), lambda i, j, l: (i, l))
        b_spec = pl.BlockSpec((tn, tk), lambda i, j, l: (j, l))
    else:
        a_spec = pl.BlockSpec((tk, tm), lambda i, j, l: (l, i))
        b_spec = pl.BlockSpec((tk, tn), lambda i, j, l: (l, j))
    direct = gl == 1
    use_acc = (not direct) and out_dtype != F32

    def body(a_ref, b_ref, o_ref, *scratch):
        part = lax.dot_general(a_ref[...].astype(MXU_DTYPE), b_ref[...].astype(MXU_DTYPE), (dims, ((), ())),
                               preferred_element_type=F32)
        if direct:
            o_ref[...] = part.astype(out_dtype)
            return
        acc = scratch[0] if use_acc else o_ref
        l = pl.program_id(2)

        @pl.when(l == 0)
        def _():
            acc[...] = part

        @pl.when(l > 0)
        def _():
            acc[...] += part

        if use_acc:
            @pl.when(l == gl - 1)
            def _():
                o_ref[...] = acc[...].astype(out_dtype)

    return pl.pallas_call(
        body, grid=(gi, gj, gl), in_specs=[a_spec, b_spec],
        out_specs=pl.BlockSpec((tm, tn), lambda i, j, l: (i, j)),
        out_shape=jax.ShapeDtypeStruct((m, n), out_dtype),
        scratch_shapes=[pltpu.VMEM((tm, tn), F32)] if use_acc else [],
        compiler_params=_params(("parallel", "parallel", "arbitrary")), name=name,
    )(a, b)


def mm(name, a, w, carrier):
    @jax.custom_vjp
    def f(a, w, carrier):
        return _mm(name + "_f", a, w, NN, out_dtype=MXU_DTYPE)

    def fwd(a, w, carrier):
        return f(a, w, carrier), (a, w)

    def bwd(res, dc):
        a, w = res
        da = _mm_nt_sum(name + "_da", [dc], [w], a.dtype)
        dw = _mm(name + "_dw", a, dc, TN)
        return da, None, dw

    f.defvjp(fwd, bwd)
    return f(a, w, carrier)


def _mm_nt_sum(name, dcs, ws, out_dtype):
    m, kk = dcs[0].shape[0], ws[0].shape[0]
    tm = _pick(m, (768, 1024, 512, 256, 128, 64, 32, 16, 8))
    tns = [_pick(w.shape[1], (2816, 1664, 1536, 1408, 1024, 768, 896, 512, 256, 128)) for w in ws]
    counts = [w.shape[1] // tn for w, tn in zip(ws, tns)]
    starts = [sum(counts[:g]) for g in range(len(ws))]
    steps = sum(counts)

    def col(g):
        return lambda i, t: jnp.clip(t - starts[g], 0, counts[g] - 1)

    def body(*refs):
        dc_refs, w_refs, o_ref, acc = refs[:len(ws)], refs[len(ws):2 * len(ws)], refs[-2], refs[-1]
        t = pl.program_id(1)

        @pl.when(t == 0)
        def _():
            acc[...] = jnp.zeros_like(acc)

        for g in range(len(ws)):
            @pl.when(jnp.logical_and(t >= starts[g], t < starts[g] + counts[g]))
            def _():
                acc[...] += lax.dot_general(dc_refs[g][...].astype(MXU_DTYPE), w_refs[g][...].astype(MXU_DTYPE),
                                            (NT, ((), ())), preferred_element_type=F32)

        @pl.when(t == steps - 1)
        def _():
            o_ref[...] = acc[...].astype(out_dtype)

    dc_specs = [pl.BlockSpec((tm, tns[g]), (lambda c: lambda i, t: (i, c(i, t)))(col(g))) for g in range(len(ws))]
    w_specs = [pl.BlockSpec((kk, tns[g]), (lambda c: lambda i, t: (0, c(i, t)))(col(g))) for g in range(len(ws))]
    return pl.pallas_call(
        body, grid=(m // tm, steps), in_specs=dc_specs + w_specs,
        out_specs=pl.BlockSpec((tm, kk), lambda i, t: (i, 0)),
        out_shape=jax.ShapeDtypeStruct((m, kk), out_dtype),
        scratch_shapes=[pltpu.VMEM((tm, kk), F32)],
        compiler_params=_params(("parallel", "arbitrary")), name=name,
    )(*dcs, *ws)


def mm_fanout(name, a, ws, carriers):
    n = len(ws)

    @jax.custom_vjp
    def f(a, ws, carriers):
        return tuple(_mm(f"{name}{g}_f", a, ws[g], NN, out_dtype=MXU_DTYPE) for g in range(n))

    def fwd(a, ws, carriers):
        return f(a, ws, carriers), (a, ws)

    def bwd(res, dcs):
        a, ws = res
        da = _mm_nt_sum(name + "_da", list(dcs), list(ws), a.dtype)
        dws = tuple(_mm(f"{name}{g}_dw", a, dcs[g], TN) for g in range(n))
        return da, None, dws

    f.defvjp(fwd, bwd)
    return f(a, tuple(ws), tuple(carriers))


def _split(x, parts):
    w = x.shape[-1] // parts
    return [x[:, k * w:(k + 1) * w] for k in range(parts)]


def _cat(xs):
    return xs[0] if len(xs) == 1 else jnp.concatenate(xs, axis=-1)


def _shift_rows(x, prev_ref, next_ref, i, ctx_tiles, nt):
    tr = x.shape[0]
    rid = lax.broadcasted_iota(jnp.int32, x.shape, 0)
    first = jnp.logical_or(i == 0, i == ctx_tiles)
    last = jnp.logical_or(i == ctx_tiles - 1, i == nt - 1)
    prow = jnp.where(first, 0.0, prev_ref[SUBLANES - 1:SUBLANES, :].astype(F32))
    nrow = jnp.where(last, 0.0, next_ref[0:1, :].astype(F32))
    xm = jnp.where(rid == 0, prow, pltpu.roll(x, 1, 0))
    xp = jnp.where(rid == tr - 1, nrow, pltpu.roll(x, tr - 1, 0))
    return xm, xp


def rowwise(name, fn, rows, vecs, *, halo=(), seg=(), parts_r=None, parts_v=None, outs=(), reds=(), ctx_tiles=1, skip=None,
            long_tiles=0):
    nr, nv = len(rows), len(vecs)
    halo = tuple(halo) or (False,) * nr
    seg = tuple(seg) or (False,) * nv
    skip = tuple(skip or (0,) * nr)
    parts_r = tuple(parts_r or (1,) * nr)
    parts_v = tuple(parts_v or (1,) * nv)
    r_total = rows[0].shape[0]
    tr = _pick(r_total, (long_tiles, 768)) if long_tiles else ROW_TILE
    assert not long_tiles or not (any(halo) or any(seg) or any(skip))
    nt = r_total // tr
    assert r_total % tr == 0 and (ctx_tiles > 0 or not any(seg)) and not any(h and s for h, s in zip(halo, skip))

    def tile_map(i):
        return lambda t: (jnp.maximum(t - skip[i], 0), 0)

    def row_specs():
        sp = []
        for i, r in enumerate(rows):
            w = r.shape[1]
            sp.append(pl.BlockSpec((tr, w), tile_map(i)))
            if halo[i]:
                k = tr // SUBLANES
                sp.append(pl.BlockSpec((SUBLANES, w), lambda t: (jnp.maximum(t * k - 1, 0), 0)))
                sp.append(pl.BlockSpec((SUBLANES, w), lambda t: (jnp.minimum((t + 1) * k, nt * k - 1), 0)))
        return sp

    def vec_spec(j):
        w = vecs[j].shape[-1]
        if seg[j]:
            return pl.BlockSpec((None, 1, w), lambda t: (jnp.where(t >= ctx_tiles, 1, 0), 0, 0))
        return pl.BlockSpec((1, w), lambda t: (0, 0))

    def row_args(rv):
        a = []
        for i in range(nr):
            a += [rv[i]] * 3 if halo[i] else [rv[i]]
        return a

    def load(refs, t):
        pos, rp = 0, []
        for i in range(nr):
            x = refs[pos][...].astype(F32)
            if halo[i]:
                xm, xp = _shift_rows(x, refs[pos + 1], refs[pos + 2], t, ctx_tiles, nt)
                rp.append(list(zip(_split(x, parts_r[i]), _split(xm, parts_r[i]), _split(xp, parts_r[i]))))
                pos += 3
            else:
                rp.append(_split(x, parts_r[i]))
                pos += 1
        vp = []
        for j in range(nv):
            vp.append(_split(refs[pos][...].astype(F32), parts_v[j]))
            pos += 1
        return rp, vp, refs[pos:]

    n_out, n_red = len(outs), len(reds)

    def fwd_call(*rv):
        def body(*refs):
            t = pl.program_id(0)
            rp, vp, rest = load(refs, t)
            o_parts, r_parts = fn(rp, vp)
            for k in range(n_out):
                rest[k][...] = _cat(o_parts[k]).astype(outs[k][1])
            for k in range(n_red):
                ref = rest[n_out + k]

                @pl.when(t == 0)
                def _():
                    ref[...] = r_parts[k]

                @pl.when(t > 0)
                def _():
                    ref[...] += r_parts[k]

        res = pl.pallas_call(
            body, grid=(nt,),
            in_specs=row_specs() + [vec_spec(j) for j in range(nv)],
            out_specs=[pl.BlockSpec((tr, o[0]), lambda t: (t, 0)) for o in outs]
            + [pl.BlockSpec((1, w), lambda t: (0, 0)) for w in reds],
            out_shape=[jax.ShapeDtypeStruct((r_total, o[0]), o[1]) for o in outs]
            + [jax.ShapeDtypeStruct((1, w), F32) for w in reds],
            compiler_params=_params(("arbitrary",)), name=name + "_f",
        )(*row_args(rv), *rv[nr:])
        return tuple(res)

    def bwd_call(rv, cts):
        def body(*refs):
            t = pl.program_id(0)
            rp, vp, rest = load(refs, t)
            ct_o = [_split(rest[k][...].astype(F32), outs[k][2]) for k in range(n_out)]
            ct_r = [rest[n_out + k][...] for k in range(n_red)]
            rest = rest[n_out + n_red:]
            _, vjp = jax.vjp(fn, rp, vp)
            d_rp, d_vp = vjp((ct_o, ct_r))
            pos = 0
            for i in range(nr):
                if halo[i]:
                    for c in range(3):
                        rest[pos + c][...] = _cat([p[c] for p in d_rp[i]])
                    pos += 3
                else:
                    rest[pos][...] = _cat(d_rp[i]).astype(rows[i].dtype)
                    pos += 1
            for j in range(nv):
                ref, val = rest[pos + j], _cat(d_vp[j])
                start = jnp.logical_or(t == 0, t == ctx_tiles) if seg[j] else t == 0

                @pl.when(start)
                def _():
                    ref[...] = val

                @pl.when(jnp.logical_not(start))
                def _():
                    ref[...] += val

        d_row_specs, d_row_shapes = [], []
        for i, r in enumerate(rows):
            w = r.shape[1]
            for _ in range(3 if halo[i] else 1):
                d_row_specs.append(pl.BlockSpec((tr, w), tile_map(i)))
                d_row_shapes.append(jax.ShapeDtypeStruct(r.shape, F32 if halo[i] else r.dtype))
        res = pl.pallas_call(
            body, grid=(nt,),
            in_specs=row_specs() + [vec_spec(j) for j in range(nv)]
            + [pl.BlockSpec((tr, o[0]), lambda t: (t, 0)) for o in outs]
            + [pl.BlockSpec((1, w), lambda t: (0, 0)) for w in reds],
            out_specs=d_row_specs + [vec_spec(j) for j in range(nv)],
            out_shape=d_row_shapes + [jax.ShapeDtypeStruct(v.shape, F32) for v in vecs],
            compiler_params=_params(("arbitrary",)), name=name + "_b",
        )(*row_args(rv), *rv[nr:], *cts)
        d_rows, pos = [], 0
        for i in range(nr):
            if halo[i]:
                d_rows.append(_unshift(res[pos], res[pos + 1], res[pos + 2], ctx_tiles * tr).astype(rows[i].dtype))
                pos += 3
            else:
                d_rows.append(res[pos])
                pos += 1
        return tuple(d_rows) + tuple(res[pos:])

    @jax.custom_vjp
    def f(*rv):
        return fwd_call(*rv)

    f.defvjp(lambda *rv: (fwd_call(*rv), rv), lambda rv, cts: bwd_call(rv, cts))
    return f(*rows, *vecs)


def _unshift(d, dm, dp, ctx_rows):
    r = d.shape[0]
    t = lax.broadcasted_iota(jnp.int32, (r, 1), 0)
    zero = jnp.zeros((1, d.shape[1]), d.dtype)
    from_m = jnp.concatenate([dm[1:], zero], axis=0)
    from_p = jnp.concatenate([zero, dp[:-1]], axis=0)
    from_m = jnp.where(t == ctx_rows - 1, 0.0, from_m)
    from_p = jnp.where(t == ctx_rows, 0.0, from_p)
    return d + from_m + from_p


def _sigmoid(x):
    return 0.5 * (jnp.tanh(0.5 * x) + 1.0)


def _silu(x):
    return x * _sigmoid(x)


def _softplus(x):
    return jnp.maximum(x, 0.0) + jnp.log(1.0 + jnp.exp(-jnp.abs(x)))


def _rms(x, g):
    return x * lax.rsqrt(jnp.mean(x * x, axis=-1, keepdims=True) + EPS) * g


def _fn_modulate(r, v):
    (x,), (g,), (sh,), (sc,) = r[0], v[0], v[1], v[2]
    return [[_rms(x, g) * (1.0 + sc) + sh]], []


def _fn_resmod(r, v):
    (x,), (y,) = r
    (gate,), (g,), (sh,), (sc,) = v
    xn = x + gate * y
    return [[xn], [_rms(xn, g) * (1.0 + sc) + sh]], []


def _fn_final(r, v):
    (x,), (y,), (tgt,) = r
    (gate,), (g,), (counts,) = v
    err = _rms(x + gate * y, g) - tgt
    row_loss = jnp.mean(err * err, axis=-1, keepdims=True)
    total = 0.5 * jnp.sum(row_loss, axis=0, keepdims=True)
    return [], [total * counts]


def _fn_conv(r, v):
    (x, xm, xp), = r[0]
    (w0,), (w1,), (w2,) = v
    return [[xm * w0 + x * w1 + xp * w2]], []


def _fn_conv_taps(r, v):
    (dy,), ((x, xm, xp),) = r
    col = lambda a: jnp.sum(dy * a, axis=0, keepdims=True)
    return [], [col(xm), col(x), col(xp)]


def conv3(name, x, w0, w1, w2, ctx_tiles, out_dtype):
    width = x.shape[1]

    def conv(tag, x, taps, dtype):
        return rowwise(name + tag, _fn_conv, [x], list(taps), halo=(True,), outs=[(width, dtype, 1)], ctx_tiles=ctx_tiles)[0]

    @jax.custom_vjp
    def f(x, w0, w1, w2):
        return conv("_y", x, (w0, w1, w2), out_dtype)

    def bwd(res, dy):
        x, w0, w1, w2 = res
        taps = rowwise(name + "_dw", _fn_conv_taps, [dy, x], [], halo=(False, True), reds=[width] * 3, ctx_tiles=ctx_tiles)
        return (conv("_dx", dy, (w2, w1, w0), x.dtype),) + tuple(taps)

    f.defvjp(lambda x, w0, w1, w2: (conv("_y", x, (w0, w1, w2), out_dtype), (x, w0, w1, w2)), bwd)
    return f(x, w0, w1, w2)


def _fn_dnprep(r, v):
    qkv, (ab,) = r
    (alog,), (dtb,) = v
    out = [[], [], []]
    for n, x in enumerate(qkv):
        which = n // DN_HEADS
        y = _silu(x)
        if which < 2:
            y = y * lax.rsqrt(jnp.sum(y * y, axis=-1, keepdims=True) + EPS)
        if which == 0:
            y = y * (HEAD_DIM ** -0.5)
        out[which].append(y)
    lane = lax.broadcasted_iota(jnp.int32, ab.shape, 1)
    g = -jnp.exp(alog) * _softplus(ab + dtb)
    gb = jnp.where(lane < N_DIRHEAD, g, jnp.where(lane < 2 * N_DIRHEAD, _sigmoid(ab), 0.0))
    return out + [[gb]], []


def _fn_dnpost(r, v):
    of, ob, z = r
    (g,) = v[0]
    return [[_rms(a + b, g) * _silu(c) for a, b, c in zip(of, ob, z)]], []


def _fn_sub(r, v):
    return [[r[0][0] - r[1][0]]], []


def _fn_scale(r, v):
    return [[r[0][0] * v[0][0]]], []


def _fn_shortconv(r, v):
    (xin, gb, gc), = r
    (w0,), (w1,), (w2,) = v
    u, um, up = (gc[k] * xin[k] for k in range(3))
    return [[gb[0] * (um * w0 + u * w1 + up * w2)]], []


def _fn_merge(r, v):
    gates, (ya,), (yb,), (yc,) = r
    return [[_sigmoid(gates[0]) * ya + _sigmoid(gates[1]) * yb + _sigmoid(gates[2]) * yc]], []


def _fn_swiglu(r, v):
    gate, up = r[0]
    return [[_silu(gate) * up]], []


def _dot3(a, b):
    (ah, al), (bh, bl) = _hi_lo(a), _hi_lo(b)
    dot = lambda x, y: lax.dot_general(x, y, (NN, ((), ())), preferred_element_type=F32)
    return dot(ah, bh) + (dot(ah, bl) + dot(al, bh))


def _bdot(a, b, dims):
    (ca,), (cb,) = dims
    return lax.dot_general(a.astype(MXU_DTYPE), b.astype(MXU_DTYPE), (((ca + 1,), (cb + 1,)), ((0,), (0,))),
                           preferred_element_type=F32)


def _hi_lo(a):
    hi = a.astype(MXU_DTYPE)
    return hi, (a - hi.astype(F32)).astype(MXU_DTYPE)


def _bdot3_raw(a, b, dims):
    (ah, al), (bh, bl) = _hi_lo(a), _hi_lo(b)
    return _bdot(ah, bh, dims) + (_bdot(ah, bl, dims) + _bdot(al, bh, dims))


@jax.custom_vjp
def _bdot3(a, b):
    return _bdot3_raw(a, b, NN)


_bdot3.defvjp(lambda a, b: (_bdot3_raw(a, b, NN), (a, b)),
              lambda res, ct: (_bdot3_raw(ct, res[1], NT), _bdot3_raw(res[0], ct, TN)))


def _inv_doubling(a):
    c = a.shape[-1]
    ii, jj = (lax.broadcasted_iota(jnp.int32, a.shape, d) for d in (1, 2))
    t = jnp.where(ii == jj, 1.0, 0.0) - a
    p = _bdot3_raw(a, a, NN)
    for _ in range(int(math.log2(c)) - 2):
        both = _bdot3_raw(jnp.concatenate([p, t], axis=1), p, NN)
        t = t + both[:, c:]
        p = both[:, :c]
    return t + _bdot3_raw(t, p, NN)


def _dn_gates(k4, gb):
    nb, c = N_DIRHEAD, k4.shape[1]
    k = jnp.concatenate([k4, k4], axis=0)
    lane = lax.broadcasted_iota(jnp.int32, gb.shape, 1)
    col = lambda j: jnp.sum(jnp.where(lane == j, gb, 0.0), axis=1, keepdims=True)
    g_col = jnp.concatenate([col(j)[None] for j in range(nb)], axis=0)
    b_col = jnp.concatenate([col(nb + j)[None] for j in range(nb)], axis=0)
    bi, ii, jj = (lax.broadcasted_iota(jnp.int32, (nb, c, c), a) for a in range(3))
    ahead = jnp.where(bi >= DN_HEADS, jj - ii, ii - jj)
    incl = ahead >= 0
    g_row = jnp.sum(jnp.where(ahead == 0, g_col, 0.0), axis=1, keepdims=True)
    gc_col = jnp.sum(jnp.where(incl, g_row, 0.0), axis=2, keepdims=True)
    gc_row = jnp.sum(jnp.where(ahead <= 0, g_col, 0.0), axis=1, keepdims=True)
    decay = jnp.where(incl, jnp.exp(jnp.where(incl, gc_col - gc_row, 0.0)), 0.0)
    return k, g_col, b_col, gc_col, decay, ahead > 0


def _dn_a(k4, gb):
    k, _, b_col, _, decay, strict = _dn_gates(k4, gb)
    return _bdot(k * b_col, k, NT) * jnp.where(strict, decay, 0.0)


def _dn_operands(q4, k4, v4, gb, t):
    k, g_col, b_col, gc_col, decay, _ = _dn_gates(k4, gb)
    q, v = (jnp.concatenate([a, a], axis=0) for a in (q4, v4))
    e_gc = jnp.exp(gc_col)
    d = k.shape[-1]
    uw = _bdot3(t, jnp.concatenate([v * b_col, k * b_col * e_gc], axis=2))
    u, w = uw[:, :, :d], uw[:, :, d:]
    g_last = jnp.sum(g_col, axis=1, keepdims=True)
    k_state = k * jnp.exp(g_last - gc_col)
    a_qk = _bdot(q, k, NT) * decay
    return u, w, q * e_gc, k_state, a_qk, jnp.broadcast_to(jnp.exp(g_last), (N_DIRHEAD, 1, LANES))


def _dn_step(s, u, w, qd, ks, aqk, gl):
    c = u.shape[1]
    on_state = _bdot(jnp.concatenate([w, qd], axis=1), s, NN)
    v_new = u - on_state[:, :c]
    o = on_state[:, c:] + _bdot(aqk, v_new, NN)
    return s * gl[:, :, :1] + _bdot(ks, v_new, TN), o


def _heads(x):
    return jnp.concatenate([x[None, :, h * HEAD_DIM:(h + 1) * HEAD_DIM] for h in range(DN_HEADS)], axis=0)


def _unheads(x):
    return jnp.concatenate([x[h] for h in range(x.shape[0])], axis=-1)


def _dn_rev(t, nc, n):
    return jnp.where(t < nc, nc - 1 - t, n - 1 - (t - nc))


PRE_CHUNKS = 4


def _pre_shapes(n):
    c, d, h = DN_CHUNK, HEAD_DIM, DN_HEADS
    shapes = [(n, h, c, d)] * 8 + [(n, h, c, c)] * 2 + [(n, h, 1, LANES)] * 2
    return shapes, [pl.BlockSpec((PRE_CHUNKS,) + s[1:], lambda t: (t, 0, 0, 0)) for s in shapes]


def _inverse_spec(n):
    shape = (n, N_DIRHEAD, DN_CHUNK, DN_CHUNK)
    return shape, pl.BlockSpec((PRE_CHUNKS,) + shape[1:], lambda t: (t, 0, 0, 0))


def _pre_row_specs():
    rows = PRE_CHUNKS * DN_CHUNK
    return pl.BlockSpec((rows, DN_HEADS * HEAD_DIM), lambda t: (t, 0)), pl.BlockSpec((rows, LANES), lambda t: (t, 0))


def _dn_pre_fwd(name, q, k, v, gb):
    n = q.shape[0] // DN_CHUNK
    shapes, specs = _pre_shapes(n)
    t_shape, t_spec = _inverse_spec(n)

    def body(q_ref, k_ref, v_ref, g_ref, *o_refs):
        for s in range(PRE_CHUNKS):
            rows = pl.ds(s * DN_CHUNK, DN_CHUNK)
            q4, k4, v4, gb_ = _heads(q_ref[rows, :]), _heads(k_ref[rows, :]), _heads(v_ref[rows, :]), g_ref[rows, :]
            t = _inv_doubling(_dn_a(k4, gb_))
            for i, r in enumerate(_dn_operands(q4, k4, v4, gb_, t)):
                o_refs[2 * i][s] = r[:DN_HEADS]
                o_refs[2 * i + 1][s] = r[DN_HEADS:]
            o_refs[-1][s] = t

    wide, narrow = _pre_row_specs()
    return pl.pallas_call(body, grid=(n // PRE_CHUNKS,), in_specs=[wide] * 3 + [narrow], out_specs=specs + [t_spec],
                          out_shape=[jax.ShapeDtypeStruct(s, F32) for s in shapes + [t_shape]],
                          compiler_params=_params(("parallel",)), name=name + "_pre_f")(q, k, v, gb)


def _dn_pre_bwd(name, q, k, v, gb, inv, cts):
    n = q.shape[0] // DN_CHUNK
    _, specs = _pre_shapes(n)
    _, t_spec = _inverse_spec(n)
    n_ct = len(specs)

    def body(q_ref, k_ref, v_ref, g_ref, t_ref, *refs):
        for s in range(PRE_CHUNKS):
            rows = pl.ds(s * DN_CHUNK, DN_CHUNK)
            ct = tuple(jnp.concatenate([refs[i][s], refs[i + 1][s]], axis=0) for i in range(0, n_ct, 2))
            q4, k4, v4, gb_, t = _heads(q_ref[rows, :]), _heads(k_ref[rows, :]), _heads(v_ref[rows, :]), g_ref[rows, :], t_ref[s]
            _, vjp = jax.vjp(_dn_operands, q4, k4, v4, gb_, t)
            dq, dk, dv, dg, dt = vjp(ct)
            da = -_bdot3_raw(_bdot3_raw(t, dt, TN), t, NT)
            _, vjp_a = jax.vjp(_dn_a, k4, gb_)
            dk_a, dg_a = vjp_a(da)
            for r, val in zip(refs[n_ct:], (_unheads(dq), _unheads(dk + dk_a), _unheads(dv), dg + dg_a)):
                r[rows, :] = val

    wide, narrow = _pre_row_specs()
    return pl.pallas_call(body, grid=(n // PRE_CHUNKS,), in_specs=[wide] * 3 + [narrow, t_spec] + specs,
                          out_specs=[wide] * 3 + [narrow],
                          out_shape=[jax.ShapeDtypeStruct(q.shape, F32)] * 3 + [jax.ShapeDtypeStruct(gb.shape, F32)],
                          compiler_params=_params(("parallel",)), name=name + "_pre_b")(q, k, v, gb, inv, *cts)


SCAN_CHUNKS = 4


def _scan_specs(pre_shapes, fw, bw):
    maps = (lambda t: (fw(t), 0, 0, 0), lambda t: (bw(t), 0, 0, 0))
    return [pl.BlockSpec((SCAN_CHUNKS,) + s[1:], maps[i % 2]) for i, s in enumerate(pre_shapes)]


def _dn_scan_fwd(name, pre, nc):
    n = pre[0].shape[0]
    r = n * DN_CHUNK
    hd, nh, nb, sc = HEAD_DIM, DN_HEADS, N_DIRHEAD, SCAN_CHUNKS
    assert n % sc == 0 and nc % sc == 0
    shapes, _ = _pre_shapes(n)
    n_in = len(shapes)

    def body(*refs):
        ins, (of_ref, ob_ref, sf_ref, sb_ref, s_scr) = refs[:n_in], refs[n_in:]
        t = pl.program_id(0)

        @pl.when(t == 0)
        def _():
            s_scr[...] = jnp.zeros_like(s_scr)

        s = s_scr[...]
        for j in range(sc):
            jf, jb = j, sc - 1 - j
            sf_ref[jf] = s[:nh]
            sb_ref[jb] = s[nh:]
            args = [jnp.concatenate([ins[i][jf], ins[i + 1][jb]], axis=0) for i in range(0, n_in, 2)]
            s, o = _dn_step(s, *args)
            of_ref[pl.ds(jf * DN_CHUNK, DN_CHUNK), :] = _unheads(o[:nh])
            ob_ref[pl.ds(jb * DN_CHUNK, DN_CHUNK), :] = _unheads(o[nh:])
        s_scr[...] = s

    fw = lambda t: t
    bw = lambda t: _dn_rev(sc * t + sc - 1, nc, n) // sc
    wide, st = (sc * DN_CHUNK, nh * hd), (sc, nh, hd, hd)
    return pl.pallas_call(
        body, grid=(n // sc,), in_specs=_scan_specs(shapes, fw, bw),
        out_specs=[pl.BlockSpec(wide, lambda t: (fw(t), 0)), pl.BlockSpec(wide, lambda t: (bw(t), 0)),
                   pl.BlockSpec(st, lambda t: (fw(t), 0, 0, 0)), pl.BlockSpec(st, lambda t: (bw(t), 0, 0, 0))],
        out_shape=[jax.ShapeDtypeStruct((r, nh * hd), F32)] * 2 + [jax.ShapeDtypeStruct((n, nh, hd, hd), F32)] * 2,
        scratch_shapes=[pltpu.VMEM((nb, hd, hd), F32)],
        compiler_params=_params(("arbitrary",)), name=name + "_scan_f",
    )(*pre)


def _dn_scan_bwd(name, pre, sall_f, sall_b, do_f, do_b, nc):
    n = pre[0].shape[0]
    hd, nh, nb, sc = HEAD_DIM, DN_HEADS, N_DIRHEAD, SCAN_CHUNKS
    shapes, _ = _pre_shapes(n)
    n_in = len(shapes)

    def body(*refs):
        ins, (sf_ref, sb_ref, dof_ref, dob_ref) = refs[:n_in], refs[n_in:n_in + 4]
        outs, ds_scr = refs[n_in + 4:2 * n_in + 4], refs[2 * n_in + 4]
        t = pl.program_id(0)

        @pl.when(t == 0)
        def _():
            ds_scr[...] = jnp.zeros_like(ds_scr)

        ds = ds_scr[...]
        for j in range(sc):
            jf, jb = sc - 1 - j, j
            args = [jnp.concatenate([ins[i][jf], ins[i + 1][jb]], axis=0) for i in range(0, n_in, 2)]
            s = jnp.concatenate([sf_ref[jf], sb_ref[jb]], axis=0)
            do = jnp.concatenate([_heads(dof_ref[pl.ds(jf * DN_CHUNK, DN_CHUNK), :]),
                                  _heads(dob_ref[pl.ds(jb * DN_CHUNK, DN_CHUNK), :])], axis=0)
            _, vjp = jax.vjp(_dn_step, s, *args)
            cts = vjp((ds, do))
            ds = cts[0]
            for i, ct in enumerate(cts[1:]):
                outs[2 * i][jf] = ct[:nh]
                outs[2 * i + 1][jb] = ct[nh:]
        ds_scr[...] = ds

    fw = lambda t: n // sc - 1 - t
    bw = lambda t: _dn_rev(n - 1 - sc * t, nc, n) // sc
    wide, st = (sc * DN_CHUNK, nh * hd), (sc, nh, hd, hd)
    return tuple(pl.pallas_call(
        body, grid=(n // sc,),
        in_specs=_scan_specs(shapes, fw, bw)
        + [pl.BlockSpec(st, lambda t: (fw(t), 0, 0, 0)), pl.BlockSpec(st, lambda t: (bw(t), 0, 0, 0)),
           pl.BlockSpec(wide, lambda t: (fw(t), 0)), pl.BlockSpec(wide, lambda t: (bw(t), 0))],
        out_specs=_scan_specs(shapes, fw, bw), out_shape=[jax.ShapeDtypeStruct(s, F32) for s in shapes],
        scratch_shapes=[pltpu.VMEM((nb, hd, hd), F32)],
        compiler_params=_params(("arbitrary",)), name=name + "_scan_b",
    )(*pre, sall_f, sall_b, do_f, do_b))


def deltanet(name, q, k, v, gb, ctx_rows):
    nc = ctx_rows // DN_CHUNK

    @jax.custom_vjp
    def pre(q, k, v, gb):
        return tuple(_dn_pre_fwd(name, q, k, v, gb)[:-1])

    def pre_fwd(*a):
        *ops, inv = _dn_pre_fwd(name, *a)
        return tuple(ops), (a, inv)

    pre.defvjp(pre_fwd, lambda res, cts: tuple(_dn_pre_bwd(name, *res[0], res[1], cts)))

    @jax.custom_vjp
    def scan(*ops):
        return tuple(_dn_scan_fwd(name, ops, nc)[:2])

    def scan_fwd(*ops):
        of, ob, sf, sb = _dn_scan_fwd(name, ops, nc)
        return (of, ob), (ops, sf, sb)

    scan.defvjp(scan_fwd, lambda res, cts: _dn_scan_bwd(name, res[0], res[1], res[2], cts[0], cts[1], nc))
    return scan(*pre(q, k, v, gb))


def _box_matrix(l, w):
    lo, hi = w // 2, w - 1 - w // 2
    pos = np.arange(l)
    start, end = np.clip(pos - lo, 0, l), np.clip(pos + hi + 1, 0, l)
    col = np.arange(l)[None, :]
    return ((col >= start[:, None]) & (col < end[:, None])) / (end - start)[:, None].astype(np.float64)


def _pool_matrices(ctx_rows, grid_rows):
    assert ctx_rows == ROW_TILE and ROW_TILE % GRID_W == 0
    ctx = np.stack([_box_matrix(ctx_rows, w) for w in POOL_WINDOWS])
    cols = np.stack([np.kron(np.eye(ROW_TILE // GRID_W), _box_matrix(GRID_W, w)) for w in POOL_WINDOWS])
    rows = np.stack([_box_matrix(grid_rows, w) for w in POOL_WINDOWS])[None]
    return np.stack([ctx, cols]).astype(np.float32), rows.astype(np.float32)


def _pool_apply(name, x, mats, group_w, seg_tiles, lane_tile):
    r, w = x.shape
    b = mats.shape[-1]
    ng = mats.shape[1]
    period = ng * group_w

    def body(x_ref, m_ref, o_ref):
        xv = x_ref[...].astype(F32)
        lane = lax.broadcasted_iota(jnp.int32, xv.shape, 1)
        grp = (lane % period) // group_w
        acc = jnp.zeros_like(xv)
        for g in range(ng):
            acc = acc + jnp.where(grp == g, _dot3(m_ref[g], xv), 0.0)
        o_ref[...] = acc

    return pl.pallas_call(
        body, grid=(r // b, w // lane_tile),
        in_specs=[pl.BlockSpec((b, lane_tile), lambda i, j: (i, j)),
                  pl.BlockSpec((None, ng, b, b), lambda i, j: (jnp.where(i >= seg_tiles, 1, 0) if mats.shape[0] > 1 else 0, 0, 0, 0))],
        out_specs=pl.BlockSpec((b, lane_tile), lambda i, j: (i, j)),
        out_shape=jax.ShapeDtypeStruct(x.shape, F32),
        compiler_params=_params(("parallel", "parallel")), name=name,
    )(x, mats)


def pool_means(name, u, ctx_rows):
    r, pw = u.shape
    grid_rows = (r - ctx_rows) // GRID_W
    gw = pw // len(POOL_WINDOWS)
    m1, m2 = _pool_matrices(ctx_rows, grid_rows)
    lane_tile = min(2048, GRID_W * pw)

    def apply(x, a1, a2, tag):
        y = _pool_apply(name + tag + "1", x, jnp.asarray(a1), gw, ctx_rows // ROW_TILE, pw)
        lat = y[ctx_rows:].reshape(grid_rows, GRID_W * pw)
        lat = _pool_apply(name + tag + "2", lat, jnp.asarray(a2), gw, 0, lane_tile)
        return jnp.concatenate([y[:ctx_rows], lat.reshape(r - ctx_rows, pw)], axis=0)

    @jax.custom_vjp
    def f(u):
        return apply(u, m1, m2, "_f")

    tr = lambda m: np.ascontiguousarray(np.swapaxes(m, -1, -2))
    f.defvjp(lambda u: (apply(u, m1, m2, "_f"), None), lambda _, ct: (apply(ct, tr(m1), tr(m2), "_b").astype(u.dtype),))
    return f(u)


def _ew(name, fn, *xs):
    shapes = jax.eval_shape(lambda *a: fn(*a), *xs)
    shapes = shapes if isinstance(shapes, (tuple, list)) else (shapes,)

    def body(*refs):
        res = fn(*[r[...] for r in refs[:len(xs)]])
        res = res if isinstance(res, (tuple, list)) else (res,)
        for r, o in zip(res, refs[len(xs):]):
            o[...] = r

    out = pl.pallas_call(body, out_shape=[jax.ShapeDtypeStruct(s.shape, s.dtype) for s in shapes],
                         compiler_params=_params(), name=name)(*xs)
    return out[0] if len(shapes) == 1 else tuple(out)


def _adamw_math(w, g, m, v):
    m2 = ADAM_B1 * m + (1.0 - ADAM_B1) * g
    v2 = ADAM_B2 * v + (1.0 - ADAM_B2) * (g * g)
    m_hat = m2 / (1.0 - ADAM_B1 ** ADAM_STEP)
    v_hat = v2 / (1.0 - ADAM_B2 ** ADAM_STEP)
    delta = -ADAM_LR * (m_hat / (jnp.sqrt(v_hat) + ADAM_EPS) + ADAM_WD * w)
    return delta, m2, v2


def adamw(name, w, g, m, v):
    *lead, r, c = w.shape
    tr = _pick(r, (256, 128, 64, 32, 16, 8))

    def body(w_ref, g_ref, m_ref, v_ref, d_ref, m2_ref, v2_ref):
        d_ref[...], m2_ref[...], v2_ref[...] = _adamw_math(w_ref[...], g_ref[...], m_ref[...], v_ref[...])

    if lead:
        grid, spec = (lead[0], r // tr), pl.BlockSpec((None, tr, c), lambda l, i: (l, i, 0))
    else:
        grid, spec = (r // tr,), pl.BlockSpec((tr, c), lambda i: (i, 0))
    return pl.pallas_call(body, grid=grid, in_specs=[spec] * 4, out_specs=[spec] * 3,
                          out_shape=[jax.ShapeDtypeStruct(w.shape, F32)] * 3,
                          compiler_params=_params(("parallel",) * len(grid)), name=name)(w, g, m, v)


def _sum_rows(name, xs, out_dtype=F32):
    r, c = xs[0].shape
    tr = _pick(r, tuple(t for t in (512, 256, 128, 64, 32, 16, 8) if t * c * 4 <= 2 << 20))

    def body(*refs):
        acc = refs[0][...].astype(F32)
        for ref in refs[1:-1]:
            acc = acc + ref[...].astype(F32)
        refs[-1][...] = acc.astype(out_dtype)

    spec = pl.BlockSpec((tr, c), lambda i: (i, 0))
    return pl.pallas_call(body, grid=(r // tr,), in_specs=[spec] * len(xs), out_specs=spec,
                          out_shape=jax.ShapeDtypeStruct((r, c), out_dtype),
                          compiler_params=_params(("parallel",)), name=name)(*xs)


def _place():
    return lax.axis_index("x"), lax.axis_index("y"), lax.axis_index("c")


def _chip_peers(x, y, c):
    return [(1 - x, y, c), (x, 1 - y, c), (1 - x, 1 - y, c)]


def all_gather8(name, block):
    m_per, n = block.shape

    def body(x_ref, out_ref, send_sems, recv_sems, local_sem):
        x, y, c = _place()
        me, sibling = (x, y, c), (x, y, 1 - c)
        chips = [(1 - x, y), (x, 1 - y), (1 - x, 1 - y)]

        def rows(px, py, pc):
            return out_ref.at[pl.ds((4 * px + 2 * py + pc) * m_per, m_per), :]

        def copy(k, blk, to, src=None):
            return pltpu.make_async_remote_copy(
                src_ref=rows(*blk) if src is None else src, dst_ref=rows(*blk),
                send_sem=send_sems.at[k], recv_sem=recv_sems.at[k], device_id=to, device_id_type=MESH_ID)

        mine = pltpu.make_async_copy(x_ref, rows(*me), local_sem)
        mine.start()
        first = [copy(0, me, sibling, src=x_ref)]
        first += [copy(1 + j, me, (*chip, c), src=x_ref) for j, chip in enumerate(chips)]
        for cp in first:
            cp.start()
        passed = [copy(4 + j, (*chip, c), sibling) for j, chip in enumerate(chips)]
        for j, chip in enumerate(chips):
            copy(1 + j, (*chip, c), me).wait_recv()
            passed[j].start()
        copy(0, sibling, me).wait_recv()
        for j, chip in enumerate(chips):
            copy(4 + j, (*chip, 1 - c), me).wait_recv()
        for cp in first + passed:
            cp.wait_send()
        mine.wait()

    return pl.pallas_call(
        body, out_shape=jax.ShapeDtypeStruct((8 * m_per, n), block.dtype),
        in_specs=[pl.BlockSpec(memory_space=pltpu.VMEM)], out_specs=pl.BlockSpec(memory_space=pltpu.VMEM),
        scratch_shapes=[pltpu.SemaphoreType.DMA((7,)), pltpu.SemaphoreType.DMA((7,)), pltpu.SemaphoreType.DMA],
        compiler_params=_params(), name=name,
    )(block)


N_PEERS = 3


def gather_chips(name, shards):
    n = len(shards)

    def body(*refs):
        xs, outs, (send_sems, recv_sems, local_sems) = refs[:n], refs[n:2 * n], refs[2 * n:]
        x, y, c = _place()
        local, remote = [], []
        for i in range(n):
            local.append(pltpu.make_async_copy(xs[i].at[c], outs[i].at[2 * x + y], local_sems.at[i]))
            local[-1].start()
            for p, peer in enumerate(_chip_peers(x, y, c)):
                remote.append(pltpu.make_async_remote_copy(
                    src_ref=xs[i].at[c], dst_ref=outs[i].at[2 * x + y], send_sem=send_sems.at[i * N_PEERS + p],
                    recv_sem=recv_sems.at[i * N_PEERS + p], device_id=peer, device_id_type=MESH_ID))
                remote[-1].start()
        for i in range(n):
            for p, (px, py, _) in enumerate(_chip_peers(x, y, c)):
                pltpu.make_async_remote_copy(
                    src_ref=xs[i].at[c], dst_ref=outs[i].at[2 * px + py], send_sem=send_sems.at[i * N_PEERS + p],
                    recv_sem=recv_sems.at[i * N_PEERS + p], device_id=(px, py, c), device_id_type=MESH_ID).wait_recv()
        for cp in remote:
            cp.wait_send()
        for cp in local:
            cp.wait()

    hbm = pl.BlockSpec(memory_space=pltpu.HBM)
    return pl.pallas_call(
        body, out_shape=[jax.ShapeDtypeStruct((N_CHIPS,) + s.shape[1:], s.dtype) for s in shards],
        in_specs=[hbm] * n, out_specs=[hbm] * n,
        scratch_shapes=[pltpu.SemaphoreType.DMA((n * N_PEERS,)), pltpu.SemaphoreType.DMA((n * N_PEERS,)),
                        pltpu.SemaphoreType.DMA((n,))],
        compiler_params=_params(), name=name,
    )(*shards)


def swap_sibling(name, blocks):
    n = len(blocks)
    pairs = [tuple(b) if isinstance(b, (tuple, list)) else (b,) for b in blocks]
    flat = [a for p in pairs for a in p]
    first = [sum(len(q) for q in pairs[:i]) for i in range(n)]

    def body(*refs):
        xs, outs, (send_sems, recv_sems) = refs[:len(flat)], refs[len(flat):len(flat) + n], refs[len(flat) + n:]
        x, y, c = _place()

        def copy(i, src):
            return pltpu.make_async_remote_copy(src_ref=src, dst_ref=outs[i], send_sem=send_sems.at[i], recv_sem=recv_sems.at[i],
                                                device_id=(x, y, 1 - c), device_id_type=MESH_ID)

        for i, p in enumerate(pairs):
            if len(p) == 1:
                copy(i, xs[first[i]]).start()
            else:
                for half in range(2):
                    pl.when(c == 1 - half)(copy(i, xs[first[i] + half]).start)
        for i in range(n):
            copy(i, xs[first[i]]).wait()

    hbm = pl.BlockSpec(memory_space=pltpu.HBM)
    return pl.pallas_call(
        body, out_shape=[jax.ShapeDtypeStruct(p[0].shape, p[0].dtype) for p in pairs],
        in_specs=[hbm] * len(flat), out_specs=[hbm] * n,
        scratch_shapes=[pltpu.SemaphoreType.DMA((n,)), pltpu.SemaphoreType.DMA((n,))],
        compiler_params=_params(), name=name,
    )(*flat)


def scatter_chips(name, pieces):
    n = len(pieces)

    def body(*refs):
        xs, outs, (send_sems, recv_sems) = refs[:n], refs[n:2 * n], refs[2 * n:]
        x, y, c = _place()
        copies = []
        for i in range(n):
            for p, (px, py, pc) in enumerate(_chip_peers(x, y, c)):
                copies.append(pltpu.make_async_remote_copy(
                    src_ref=xs[i].at[2 * px + py], dst_ref=outs[i].at[p], send_sem=send_sems.at[i * N_PEERS + p],
                    recv_sem=recv_sems.at[i * N_PEERS + p], device_id=(px, py, pc), device_id_type=MESH_ID))
                copies[-1].start()
        for cp in copies:
            cp.wait()

    hbm = pl.BlockSpec(memory_space=pltpu.HBM)
    return pl.pallas_call(
        body, out_shape=[jax.ShapeDtypeStruct((N_PEERS,) + p.shape[1:], p.dtype) for p in pieces],
        in_specs=[hbm] * n, out_specs=[hbm] * n,
        scratch_shapes=[pltpu.SemaphoreType.DMA((n * N_PEERS,)), pltpu.SemaphoreType.DMA((n * N_PEERS,))],
        compiler_params=_params(), name=name,
    )(*pieces)


def pair_sum(name, half0, half1, got, out_dtype):
    r, w = got.shape
    tr = _pick(r, tuple(t for t in (512, 256, 128, 64, 32, 16, 8) if t * w * 4 <= 2 << 20))

    def body(a0, a1, g, o):
        mine = jnp.where(lax.axis_index("c") == 0, a0[...], a1[...])
        o[...] = (mine.astype(F32) + g[...].astype(F32)).astype(out_dtype)

    spec = pl.BlockSpec((tr, w), lambda i: (i, 0))
    return pl.pallas_call(body, grid=(r // tr,), in_specs=[spec] * 3, out_specs=spec,
                          out_shape=jax.ShapeDtypeStruct((r, w), out_dtype),
                          compiler_params=_params(("parallel",)), name=name)(half0, half1, got)


def _sum_devices(name, got, fold=False):
    def body(a_ref, *o_refs):
        acc = a_ref[0]
        for i in range(1, N_DEV):
            acc = acc + a_ref[i]
        o_refs[0][...] = acc
        if fold:
            o_refs[1][...] = acc + pltpu.roll(acc, SUBLANES // 2, 0)

    shape = jax.ShapeDtypeStruct(got.shape[1:], F32)
    out = pl.pallas_call(body, out_shape=[shape] * (2 if fold else 1), compiler_params=_params(), name=name)(got)
    return out if fold else out[0]


def _w_in_bounds(d, pw, scw):
    off_z = 3 * DN_WIDTH
    off_a = off_z + DN_WIDTH
    off_pool = off_a + 2 * N_DIRHEAD
    off_sc = off_pool + pw
    off_gate = off_sc + 3 * scw
    return (0, off_z, off_a, off_pool, off_sc, off_gate, off_gate + 3 * d)


def _misc_widths(pw, scw):
    return (DN_WIDTH, pw, 3 * scw, LANES)


def split_matrices(full, pw, scw):
    depth, d = len(full["w_in"]), full["w_in"][0].shape[0]
    b = _w_in_bounds(d, pw, scw)
    out = {k: [] for k in ("qkv", "gate", "misc", "gu", "br_a", "br_b", "br_c", "o", "down")}
    for l in range(depth):
        w = full["w_in"][l]
        out["qkv"].append(w[:, b[0]:b[1]])
        out["gate"].append(w[:, b[5]:b[6]])
        out["misc"].append(jnp.concatenate([w[:, b[1]:b[2]], w[:, b[3]:b[5]], _pad_lanes(w[:, b[2]:b[3]])], axis=1))
        out["gu"].append(full["w_gu"][l])
        for k in ("br_a", "br_b", "br_c", "o", "down"):
            out[k].append(full["w_" + k][l])
    return out


def join_matrix_grads(g, pw, scw):
    depth = len(g["qkv"])
    n_z, n_rest, n_ab = DN_WIDTH, pw + 3 * scw, 2 * N_DIRHEAD
    w_in = []
    for l in range(depth):
        m = g["misc"][l]
        w_in.append(jnp.concatenate([g["qkv"][l], m[:, :n_z], m[:, n_z + n_rest:n_z + n_rest + n_ab],
                                     m[:, n_z:n_z + n_rest], g["gate"][l]], axis=1))
    out = {"w_in": w_in, "w_gu": g["gu"]}
    for k in ("br_a", "br_b", "br_c", "o", "down"):
        out["w_" + k] = g[k]
    return {k: [a.astype(MXU_DTYPE) for a in v] for k, v in out.items()}


def split_cols(x, widths):
    edges = np.cumsum((0,) + tuple(widths))

    def cut(x):
        return tuple(x[:, a:b] for a, b in zip(edges[:-1], edges[1:]))

    f = jax.custom_vjp(cut)
    f.defvjp(lambda x: (cut(x), None), lambda _, cts: (jnp.concatenate(cts, axis=1),))
    return f(x)


def _row(v):
    return v.reshape(1, -1)


def _pad_lanes(v, width=LANES):
    return jnp.pad(v, ((0, 0), (0, width - v.shape[1])))


def _block_diag(blocks):
    g, n, _ = blocks.shape
    out = jnp.zeros((g * n, g * n), blocks.dtype)
    for i in range(g):
        out = out.at[i * n:(i + 1) * n, i * n:(i + 1) * n].set(blocks[i])
    return out


def local_loss(p, carriers, x, mod_lat, mod_ctx, wts, ctx, target):
    ctx_rows, d = ctx.shape
    depth = len(wts["qkv"])
    pw, scw = p["pool_scale"].shape[1], p["sc_conv_w"].shape[2]
    ff = wts["down"][0].shape[0]
    ct = ctx_rows // ROW_TILE
    xs = jnp.concatenate([ctx, x], axis=0)
    seg = lambda l, k: jnp.stack([mod_ctx[l, k], mod_lat[l, k]]).reshape(2, 1, d)
    dn = gate2 = None
    for l in range(depth):
        sh1, sc1, g1, sh2, sc2, g2 = (seg(l, k) for k in range(6))
        tag = f"l{l}_"
        lin = lambda key, a: mm(tag + key, a, wts[key][l], carriers[key][l])
        if l == 0:
            (h1,) = rowwise(tag + "mod", _fn_modulate, [xs], [_row(p["norm1_g"][l]), sh1, sc1], seg=(False, True, True),
                            outs=[(d, MXU_DTYPE, 1)], ctx_tiles=ct)
        else:
            xs, h1 = rowwise(tag + "resmod1", _fn_resmod, [xs, dn], [gate2, _row(p["norm1_g"][l]), sh1, sc1],
                             seg=(True, False, True, True), outs=[(d, F32, 1), (d, MXU_DTYPE, 1)], ctx_tiles=ct)
        fan = ("qkv", "gate", "misc")
        p_qkv, p_gate, p_misc = mm_fanout(tag + "in", h1, [wts[k][l] for k in fan], [carriers[k][l] for k in fan])
        p_z, p_pool, p_sc, p_ab = split_cols(p_misc, _misc_widths(pw, scw))
        cw = p["dn_conv_w"][l]
        qkv_conv = conv3(tag + "dnconv", p_qkv, cw[0:1], cw[1:2], cw[2:3], ct, F32)
        q, k, v, gb = rowwise(
            tag + "dnprep", _fn_dnprep, [qkv_conv, p_ab],
            [_pad_lanes(p["dn_a_log"][l].reshape(1, -1)), _pad_lanes(p["dn_dt_bias"][l].reshape(1, -1))],
            parts_r=(3 * DN_HEADS, 1), outs=[(DN_WIDTH, F32, DN_HEADS)] * 3 + [(LANES, F32, 1)], long_tiles=768)
        o_f, o_b = deltanet(tag + "dn", q, k, v, gb, ctx_rows)
        (oz,) = rowwise(tag + "dnpost", _fn_dnpost, [o_f, o_b, p_z], [_row(p["dn_norm_g"][l])], parts_r=(DN_HEADS,) * 3,
                        outs=[(DN_WIDTH, MXU_DTYPE, DN_HEADS)], long_tiles=768)
        y_a = lin("br_a", oz)
        means = pool_means(tag + "box", p_pool, ctx_rows)
        (dpool,) = rowwise(tag + "poolsub", _fn_sub, [means, p_pool], [], outs=[(pw, MXU_DTYPE, 1)], long_tiles=2816)
        pool_mat = _block_diag(p["pool_w"][l])
        mixed = mm(tag + "poolw", dpool, pool_mat.astype(MXU_DTYPE), pool_mat)
        (yb_in,) = rowwise(tag + "poolscale", _fn_scale, [mixed], [_row(p["pool_scale"][l])], outs=[(pw, MXU_DTYPE, 1)],
                           long_tiles=2816)
        y_b = lin("br_b", yb_in)
        sw = p["sc_conv_w"][l]
        (yc_in,) = rowwise(tag + "sconv", _fn_shortconv, [p_sc], [sw[0:1], sw[1:2], sw[2:3]], halo=(True,), parts_r=(3,),
                           outs=[(scw, MXU_DTYPE, 1)], ctx_tiles=ct)
        y_c = lin("br_c", yc_in)
        (y,) = rowwise(tag + "merge", _fn_merge, [p_gate, y_a, y_b, y_c], [], parts_r=(3, 1, 1, 1),
                       outs=[(d, MXU_DTYPE, 1)], long_tiles=768)
        mix = lin("o", y)
        xs, h2 = rowwise(tag + "resmod2", _fn_resmod, [xs, mix], [g1, _row(p["norm2_g"][l]), sh2, sc2],
                         seg=(True, False, True, True), outs=[(d, F32, 1), (d, MXU_DTYPE, 1)], ctx_tiles=ct)
        (act,) = rowwise(tag + "swiglu", _fn_swiglu, [lin("gu", h2)], [], parts_r=(2,),
                         outs=[(ff, MXU_DTYPE, 1)], long_tiles=768)
        dn = lin("down", act)
        gate2 = g2
    counts = jnp.concatenate([jnp.zeros((1, 1, LANES), F32), jnp.ones((1, 1, LANES), F32)])
    (total,) = rowwise("final", _fn_final, [xs, dn, target], [gate2, _row(p["final_norm_g"]), counts],
                       seg=(True, False, True), reds=[LANES], ctx_tiles=ct, skip=(0, 0, ct))
    return total[0, 0]


BIG = ("w_in", "w_br_a", "w_br_b", "w_br_c", "w_o", "w_gu", "w_down")
ROW_SHARDED = ("w_o", "w_down")
SMALL_REPL = ("norm1_g", "norm2_g", "dn_a_log", "dn_dt_bias", "dn_norm_g", "pool_w", "pool_scale", "final_norm_g")
SMALL_SHARD = ("dn_conv_w", "sc_conv_w")
WEIGHTS = ("c_ctx", "w_ada", "b_ada", "norm1_g", "norm2_g", "w_in", "dn_conv_w", "dn_a_log", "dn_dt_bias", "dn_norm_g", "pool_w",
           "pool_scale", "sc_conv_w", "w_br_a", "w_br_b", "w_br_c", "w_o", "w_gu", "w_down", "final_norm_g")
N_CHIPS, N_DEV = 4, 8
COMM_COLS = 1024


def _pack(arrays, rows_multiple, cols=COMM_COLS, dtype=F32):
    size = sum(int(np.prod(a.shape)) for a in arrays)
    rows = -(-size // (cols * rows_multiple)) * rows_multiple
    tail = [jnp.zeros((rows * cols - size,), dtype)] if rows * cols > size else []
    return jnp.concatenate([a.astype(dtype).reshape(-1) for a in arrays] + tail).reshape(rows, cols)


def _unpack(flat, shapes):
    out, pos = [], 0
    for s in shapes:
        n = int(np.prod(s))
        out.append(flat[pos:pos + n].reshape(s))
        pos += n
    return out


def _join_shards(parts, name):
    _, a, b = parts.shape
    if name in ROW_SHARDED:
        return parts.reshape(N_CHIPS * a, b)
    return jnp.moveaxis(parts, 0, 1).reshape(a, N_CHIPS * b)


def _cut_shards(full, name):
    k, n = full.shape
    if name in ROW_SHARDED:
        return full.reshape(N_CHIPS, k // N_CHIPS, n)
    return jnp.moveaxis(full.reshape(k, N_CHIPS, n // N_CHIPS), 1, 0)


def _dsilu(x):
    s = _sigmoid(x)
    return s + x * s * (1.0 - s)


def kernel(x, c, ctx, c_ctx, w_ada, b_ada, norm1_g, norm2_g, w_in, dn_conv_w, dn_a_log, dn_dt_bias, dn_norm_g, pool_w, pool_scale, sc_conv_w, w_br_a, w_br_b, w_br_c, w_o, w_gu, w_down, final_norm_g, loss_target, m_c_ctx, m_w_ada, m_b_ada, m_norm1_g, m_norm2_g, m_w_in, m_dn_conv_w, m_dn_a_log, m_dn_dt_bias, m_dn_norm_g, m_pool_w, m_pool_scale, m_sc_conv_w, m_w_br_a, m_w_br_b, m_w_br_c, m_w_o, m_w_gu, m_w_down, m_final_norm_g, v_c_ctx, v_w_ada, v_b_ada, v_norm1_g, v_norm2_g, v_w_in, v_dn_conv_w, v_dn_a_log, v_dn_dt_bias, v_dn_norm_g, v_pool_w, v_pool_scale, v_sc_conv_w, v_w_br_a, v_w_br_b, v_w_br_c, v_w_o, v_w_gu, v_w_down, v_final_norm_g):
    given = dict(locals())
    ix, iy, ic = _place()
    chip, dev = 2 * ix + iy, 4 * ix + 2 * iy + ic
    d = x.shape[-1]
    depth = w_in.shape[0]
    ada_cols = w_ada.shape[2]
    spare = 2 * SUBLANES - N_DEV - 1

    got0 = all_gather8("gather_cond", _pack([c, dn_conv_w, sc_conv_w], SUBLANES)).reshape(N_DEV, -1)
    c_all = got0[:, :d]
    taps = [_unpack(got0[2 * j, d:], [dn_conv_w.shape, sc_conv_w.shape]) for j in range(N_CHIPS)]
    full = {"dn_conv_w": jnp.concatenate([t[0] for t in taps], axis=-1),
            "sc_conv_w": jnp.concatenate([t[1] for t in taps], axis=-1)}
    assert depth == 2, "the two cores of a chip split the layers between them"
    got_layer = gather_chips("gather_weights", [given[n].astype(MXU_DTYPE) for n in BIG])
    other_layer = swap_sibling("swap_weights", got_layer)
    mats = {n: [_join_shards(jnp.where(ic == l, got_layer[i], other_layer[i]), n) for l in range(depth)]
            for i, n in enumerate(BIG)}
    for n in SMALL_REPL:
        full[n] = given[n]
    pw, scw = pool_scale.shape[1], full["sc_conv_w"].shape[2]
    wts = split_matrices(mats, pw, scw)
    carriers = {k: [jnp.zeros(w.shape, F32) for w in ws] for k, ws in wts.items()}

    cond = jnp.concatenate([c_all, c_ctx[None], jnp.zeros((spare, d), F32)])
    s_cond = _ew("silu_cond", _silu, cond)
    mod_cols = jnp.concatenate([_mm(f"ada{l}_f", s_cond, w_ada[l], NN) for l in range(depth)], axis=1)
    mod_got = all_gather8("gather_mod", mod_cols).reshape(N_DEV, 2 * SUBLANES, depth, ada_cols)
    mod_all = jnp.concatenate([mod_got[2 * j] for j in range(N_CHIPS)], axis=-1) + b_ada[None]
    mod_lat = lax.dynamic_index_in_dim(mod_all, dev, 0, keepdims=False).reshape(depth, 6, d)
    mod_ctx = mod_all[N_DEV].reshape(depth, 6, d)

    loss_local, (g_full, g_mats, grad_x, g_mod_lat, g_mod_ctx) = jax.value_and_grad(local_loss, argnums=(0, 1, 2, 3, 4))(
        full, carriers, x[0], mod_lat, mod_ctx, wts, ctx[0], loss_target[0])
    g_mats_full = join_matrix_grads(g_mats, pw, scw)
    loss = lax.psum(loss_local, ("x", "y", "c"))

    dmod_cols = (2 * depth * 6 * d) // SUBLANES
    dmod = all_gather8("gather_dmod", _pack([g_mod_lat, g_mod_ctx], SUBLANES, cols=dmod_cols))
    dmod = dmod.reshape(N_DEV, SUBLANES, dmod_cols)
    dmod_sum, dmod_fold = _sum_devices("reduce_dmod", dmod, fold=True)
    half_rows = SUBLANES // 2
    grad_b_ada = dmod_fold[:half_rows].reshape(depth, 6 * d)
    dctx_sum = dmod_sum[half_rows:].reshape(1, depth, 6 * d)
    d9 = jnp.concatenate([dmod[:, :half_rows].reshape(N_DEV, depth, 6 * d), dctx_sum, jnp.zeros((spare, depth, 6 * d), F32)])
    d9 = lax.dynamic_slice_in_dim(d9, chip * ada_cols, ada_cols, axis=2)
    grad_w_ada = jnp.stack([_mm(f"ada{l}_dw", s_cond, d9[:, l], TN) for l in range(depth)])
    ds_cond = [_mm(f"ada{l}_da", d9[:, l], w_ada[l], NT) for l in range(depth)]
    dsilu_part = _sum_rows("sum_dcond", ds_cond)[N_DEV]

    small_names = SMALL_REPL + SMALL_SHARD
    small_grads = [dsilu_part] + [g_full[n] for n in small_names]
    small_shapes = [a.shape for a in small_grads]
    n_small = sum(int(np.prod(s)) for s in small_shapes)
    cols = -(-n_small // (SUBLANES * LANES)) * LANES
    got2 = all_gather8("gather_small", _pack(small_grads, SUBLANES, cols=cols)).reshape(N_DEV, SUBLANES, cols)
    small_sum = _unpack(_sum_devices("reduce_small", got2).reshape(-1), small_shapes)
    grads = dict(zip(small_names, small_sum[1:]))
    grads["c_ctx"] = _ew("dsilu", lambda g, z: 0.5 * g * _dsilu(z), _row(small_sum[0]), _row(c_ctx)).reshape(-1)
    grads["w_ada"], grads["b_ada"] = grad_w_ada, grad_b_ada
    for n in SMALL_SHARD:
        width = given[n].shape[-1]
        grads[n] = lax.dynamic_slice_in_dim(grads[n], chip * width, width, axis=-1)

    from_sibling = swap_sibling("swap_layers", [tuple(g_mats_full[n]) for n in BIG])
    pieces = [_cut_shards(pair_sum("sum_pair_" + n, *g_mats_full[n], from_sibling[i], MXU_DTYPE), n) for i, n in enumerate(BIG)]
    from_chips = scatter_chips("scatter_pieces", pieces)
    reduced = []
    for i, n in enumerate(BIG):
        own = lax.dynamic_index_in_dim(pieces[i], chip, 0, keepdims=False)
        a, b = own.shape
        view = lambda t: t.reshape(-1, b)
        reduced.append(_sum_rows("sum_chips_" + n, [view(own)] + [view(from_chips[i][p]) for p in range(N_PEERS)]).reshape(a, b))
    other_reduced = swap_sibling("swap_reduced", reduced)
    for i, n in enumerate(BIG):
        grads[n] = jnp.stack([jnp.where(ic == l, reduced[i], other_reduced[i]) for l in range(depth)])

    delta, new_m, new_v = {}, {}, {}
    large = BIG + ("w_ada",)
    for n in large:
        delta[n], new_m[n], new_v[n] = adamw("adamw_" + n, given[n], grads[n], given["m_" + n], given["v_" + n])
    rest = [n for n in WEIGHTS if n not in large]
    rest_shapes = [given[n].shape for n in rest]
    packed = [_pack([src[pre + n] for n in rest], SUBLANES, cols=LANES)
              for src, pre in ((given, ""), (grads, ""), (given, "m_"), (given, "v_"))]
    for res, o in zip((delta, new_m, new_v), adamw("adamw_small", *packed)):
        for n, a in zip(rest, _unpack(o.reshape(-1), rest_shapes)):
            res[n] = a
    return (loss, grad_x[None], *[grads[n] for n in WEIGHTS], *[delta[n] for n in WEIGHTS],
            *[new_m[n] for n in WEIGHTS], *[new_v[n] for n in WEIGHTS])
```

```python
import math

import numpy as np
import jax
import jax.numpy as jnp
from jax import lax
from jax.experimental import pallas as pl
from jax.experimental.pallas import tpu as pltpu

F32 = jnp.float32
MXU_DTYPE = jnp.bfloat16

DN_HEADS = 4
HEAD_DIM = 128
DN_WIDTH = DN_HEADS * HEAD_DIM
DN_CHUNK = 64
GRID_W = 64
EPS = 1e-6
POOL_WINDOWS = (2, 4, 8, 16)
N_DIRHEAD = 2 * DN_HEADS
ADAM_LR, ADAM_B1, ADAM_B2, ADAM_EPS, ADAM_WD, ADAM_STEP = 0.001, 0.9, 0.999, 1e-08, 0.01, 10

LANES = 128
SUBLANES = 8
ROW_TILE = 256
VMEM_LIMIT = 56 * 1024 * 1024

MESH_ID = pl.DeviceIdType.MESH
NN, NT, TN = ((1,), (0,)), ((1,), (1,)), ((0,), (0,))


def _params(sem=None):
    return pltpu.CompilerParams(dimension_semantics=sem, vmem_limit_bytes=VMEM_LIMIT)


def _pick(n, cands):
    for c in cands:
        if c <= n and n % c == 0:
            return c
    return n


def _mm(name, a, b, dims, out_dtype=F32):
    if dims == NN:
        (m, kk), (_, n) = a.shape, b.shape
    elif dims == NT:
        (m, kk), (n, _) = a.shape, b.shape
    else:
        (kk, m), (_, n) = a.shape, b.shape
    tm = _pick(m, (768, 1024, 1408, 512, 256, 128, 64, 32, 16, 8))
    tn = _pick(n, ((3072, 2816) if dims == NN else ()) + (1536, 1024, 1408, 1664, 768, 896, 512, 256, 128))
    tk = kk if dims == NN and kk <= 2816 else _pick(kk, (1024, 1408, 768, 512, 256, 128))
    gi, gj, gl = m // tm, n // tn, kk // tk
    if dims == NN:
        a_spec = pl.BlockSpec((tm, tk), lambda i, j, l: (i, l))
        b_spec = pl.BlockSpec((tk, tn), lambda i, j, l: (l, j))
    elif dims == NT:
        a_spec = pl.BlockSpec((tm, tk), lambda i, j, l: (i, l))
        b_spec = pl.BlockSpec((tn, tk), lambda i, j, l: (j, l))
    else:
        a_spec = pl.BlockSpec((tk, tm), lambda i, j, l: (l, i))
        b_spec = pl.BlockSpec((tk, tn), lambda i, j, l: (l, j))
    direct = gl == 1
    use_acc = (not direct) and out_dtype != F32

    def body(a_ref, b_ref, o_ref, *scratch):
        part = lax.dot_general(a_ref[...].astype(MXU_DTYPE), b_ref[...].astype(MXU_DTYPE), (dims, ((), ())),
                               preferred_element_type=F32)
        if direct:
            o_ref[...] = part.astype(out_dtype)
            return
        acc = scratch[0] if use_acc else o_ref
        l = pl.program_id(2)

        @pl.when(l == 0)
        def _():
            acc[...] = part

        @pl.when(l > 0)
        def _():
            acc[...] += part

        if use_acc:
            @pl.when(l == gl - 1)
            def _():
                o_ref[...] = acc[...].astype(out_dtype)

    return pl.pallas_call(
        body, grid=(gi, gj, gl), in_specs=[a_spec, b_spec],
        out_specs=pl.BlockSpec((tm, tn), lambda i, j, l: (i, j)),
        out_shape=jax.ShapeDtypeStruct((m, n), out_dtype),
        scratch_shapes=[pltpu.VMEM((tm, tn), F32)] if use_acc else [],
        compiler_params=_params(("parallel", "parallel", "arbitrary")), name=name,
    )(a, b)


def mm(name, a, w, carrier):
    @jax.custom_vjp
    def f(a, w, carrier):
        return _mm(name + "_f", a, w, NN, out_dtype=MXU_DTYPE)

    def fwd(a, w, carrier):
        return f(a, w, carrier), (a, w)

    def bwd(res, dc):
        a, w = res
        da = _mm_nt_sum(name + "_da", [dc], [w], a.dtype)
        dw = _mm(name + "_dw", a, dc, TN)
        return da, None, dw

    f.defvjp(fwd, bwd)
    return f(a, w, carrier)


def _mm_nt_sum(name, dcs, ws, out_dtype):
    m, kk = dcs[0].shape[0], ws[0].shape[0]
    tm = _pick(m, (768, 1024, 512, 256, 128, 64, 32, 16, 8))
    tns = [_pick(w.shape[1], (2816, 1664, 1536, 1408, 1024, 768, 896, 512, 256, 128)) for w in ws]
    counts = [w.shape[1] // tn for w, tn in zip(ws, tns)]
    starts = [sum(counts[:g]) for g in range(len(ws))]
    steps = sum(counts)

    def col(g):
        return lambda i, t: jnp.clip(t - starts[g], 0, counts[g] - 1)

    def body(*refs):
        dc_refs, w_refs, o_ref, acc = refs[:len(ws)], refs[len(ws):2 * len(ws)], refs[-2], refs[-1]
        t = pl.program_id(1)

        @pl.when(t == 0)
        def _():
            acc[...] = jnp.zeros_like(acc)

        for g in range(len(ws)):
            @pl.when(jnp.logical_and(t >= starts[g], t < starts[g] + counts[g]))
            def _():
                acc[...] += lax.dot_general(dc_refs[g][...].astype(MXU_DTYPE), w_refs[g][...].astype(MXU_DTYPE),
                                            (NT, ((), ())), preferred_element_type=F32)

        @pl.when(t == steps - 1)
        def _():
            o_ref[...] = acc[...].astype(out_dtype)

    dc_specs = [pl.BlockSpec((tm, tns[g]), (lambda c: lambda i, t: (i, c(i, t)))(col(g))) for g in range(len(ws))]
    w_specs = [pl.BlockSpec((kk, tns[g]), (lambda c: lambda i, t: (0, c(i, t)))(col(g))) for g in range(len(ws))]
    return pl.pallas_call(
        body, grid=(m // tm, steps), in_specs=dc_specs + w_specs,
        out_specs=pl.BlockSpec((tm, kk), lambda i, t: (i, 0)),
        out_shape=jax.ShapeDtypeStruct((m, kk), out_dtype),
        scratch_shapes=[pltpu.VMEM((tm, kk), F32)],
        compiler_params=_params(("parallel", "arbitrary")), name=name,
    )(*dcs, *ws)


def mm_fanout(name, a, ws, carriers):
    n = len(ws)

    @jax.custom_vjp
    def f(a, ws, carriers):
        return tuple(_mm(f"{name}{g}_f", a, ws[g], NN, out_dtype=MXU_DTYPE) for g in range(n))

    def fwd(a, ws, carriers):
        return f(a, ws, carriers), (a, ws)

    def bwd(res, dcs):
        a, ws = res
        da = _mm_nt_sum(name + "_da", list(dcs), list(ws), a.dtype)
        dws = tuple(_mm(f"{name}{g}_dw", a, dcs[g], TN) for g in range(n))
        return da, None, dws

    f.defvjp(fwd, bwd)
    return f(a, tuple(ws), tuple(carriers))


def _split(x, parts):
    w = x.shape[-1] // parts
    return [x[:, k * w:(k + 1) * w] for k in range(parts)]


def _cat(xs):
    return xs[0] if len(xs) == 1 else jnp.concatenate(xs, axis=-1)


def _shift_rows(x, prev_ref, next_ref, i, ctx_tiles, nt):
    tr = x.shape[0]
    rid = lax.broadcasted_iota(jnp.int32, x.shape, 0)
    first = jnp.logical_or(i == 0, i == ctx_tiles)
    last = jnp.logical_or(i == ctx_tiles - 1, i == nt - 1)
    prow = jnp.where(first, 0.0, prev_ref[SUBLANES - 1:SUBLANES, :].astype(F32))
    nrow = jnp.where(last, 0.0, next_ref[0:1, :].astype(F32))
    xm = jnp.where(rid == 0, prow, pltpu.roll(x, 1, 0))
    xp = jnp.where(rid == tr - 1, nrow, pltpu.roll(x, tr - 1, 0))
    return xm, xp


def rowwise(name, fn, rows, vecs, *, halo=(), seg=(), parts_r=None, parts_v=None, outs=(), reds=(), ctx_tiles=1, skip=None,
            long_tiles=0):
    nr, nv = len(rows), len(vecs)
    halo = tuple(halo) or (False,) * nr
    seg = tuple(seg) or (False,) * nv
    skip = tuple(skip or (0,) * nr)
    parts_r = tuple(parts_r or (1,) * nr)
    parts_v = tuple(parts_v or (1,) * nv)
    r_total = rows[0].shape[0]
    tr = _pick(r_total, (long_tiles, 768)) if long_tiles else ROW_TILE
    assert not long_tiles or not (any(halo) or any(seg) or any(skip))
    nt = r_total // tr
    assert r_total % tr == 0 and (ctx_tiles > 0 or not any(seg)) and not any(h and s for h, s in zip(halo, skip))

    def tile_map(i):
        return lambda t: (jnp.maximum(t - skip[i], 0), 0)

    def row_specs():
        sp = []
        for i, r in enumerate(rows):
            w = r.shape[1]
            sp.append(pl.BlockSpec((tr, w), tile_map(i)))
            if halo[i]:
                k = tr // SUBLANES
                sp.append(pl.BlockSpec((SUBLANES, w), lambda t: (jnp.maximum(t * k - 1, 0), 0)))
                sp.append(pl.BlockSpec((SUBLANES, w), lambda t: (jnp.minimum((t + 1) * k, nt * k - 1), 0)))
        return sp

    def vec_spec(j):
        w = vecs[j].shape[-1]
        if seg[j]:
            return pl.BlockSpec((None, 1, w), lambda t: (jnp.where(t >= ctx_tiles, 1, 0), 0, 0))
        return pl.BlockSpec((1, w), lambda t: (0, 0))

    def row_args(rv):
        a = []
        for i in range(nr):
            a += [rv[i]] * 3 if halo[i] else [rv[i]]
        return a

    def load(refs, t):
        pos, rp = 0, []
        for i in range(nr):
            x = refs[pos][...].astype(F32)
            if halo[i]:
                xm, xp = _shift_rows(x, refs[pos + 1], refs[pos + 2], t, ctx_tiles, nt)
                rp.append(list(zip(_split(x, parts_r[i]), _split(xm, parts_r[i]), _split(xp, parts_r[i]))))
                pos += 3
            else:
                rp.append(_split(x, parts_r[i]))
                pos += 1
        vp = []
        for j in range(nv):
            vp.append(_split(refs[pos][...].astype(F32), parts_v[j]))
            pos += 1
        return rp, vp, refs[pos:]

    n_out, n_red = len(outs), len(reds)

    def fwd_call(*rv):
        def body(*refs):
            t = pl.program_id(0)
            rp, vp, rest = load(refs, t)
            o_parts, r_parts = fn(rp, vp)
            for k in range(n_out):
                rest[k][...] = _cat(o_parts[k]).astype(outs[k][1])
            for k in range(n_red):
                ref = rest[n_out + k]

                @pl.when(t == 0)
                def _():
                    ref[...] = r_parts[k]

                @pl.when(t > 0)
                def _():
                    ref[...] += r_parts[k]

        res = pl.pallas_call(
            body, grid=(nt,),
            in_specs=row_specs() + [vec_spec(j) for j in range(nv)],
            out_specs=[pl.BlockSpec((tr, o[0]), lambda t: (t, 0)) for o in outs]
            + [pl.BlockSpec((1, w), lambda t: (0, 0)) for w in reds],
            out_shape=[jax.ShapeDtypeStruct((r_total, o[0]), o[1]) for o in outs]
            + [jax.ShapeDtypeStruct((1, w), F32) for w in reds],
            compiler_params=_params(("arbitrary",)), name=name + "_f",
        )(*row_args(rv), *rv[nr:])
        return tuple(res)

    def bwd_call(rv, cts):
        def body(*refs):
            t = pl.program_id(0)
            rp, vp, rest = load(refs, t)
            ct_o = [_split(rest[k][...].astype(F32), outs[k][2]) for k in range(n_out)]
            ct_r = [rest[n_out + k][...] for k in range(n_red)]
            rest = rest[n_out + n_red:]
            _, vjp = jax.vjp(fn, rp, vp)
            d_rp, d_vp = vjp((ct_o, ct_r))
            pos = 0
            for i in range(nr):
                if halo[i]:
                    for c in range(3):
                        rest[pos + c][...] = _cat([p[c] for p in d_rp[i]])
                    pos += 3
                else:
                    rest[pos][...] = _cat(d_rp[i]).astype(rows[i].dtype)
                    pos += 1
            for j in range(nv):
                ref, val = rest[pos + j], _cat(d_vp[j])
                start = jnp.logical_or(t == 0, t == ctx_tiles) if seg[j] else t == 0

                @pl.when(start)
                def _():
                    ref[...] = val

                @pl.when(jnp.logical_not(start))
                def _():
                    ref[...] += val

        d_row_specs, d_row_shapes = [], []
        for i, r in enumerate(rows):
            w = r.shape[1]
            for _ in range(3 if halo[i] else 1):
                d_row_specs.append(pl.BlockSpec((tr, w), tile_map(i)))
                d_row_shapes.append(jax.ShapeDtypeStruct(r.shape, F32 if halo[i] else r.dtype))
        res = pl.pallas_call(
            body, grid=(nt,),
            in_specs=row_specs() + [vec_spec(j) for j in range(nv)]
            + [pl.BlockSpec((tr, o[0]), lambda t: (t, 0)) for o in outs]
            + [pl.BlockSpec((1, w), lambda t: (0, 0)) for w in reds],
            out_specs=d_row_specs + [vec_spec(j) for j in range(nv)],
            out_shape=d_row_shapes + [jax.ShapeDtypeStruct(v.shape, F32) for v in vecs],
            compiler_params=_params(("arbitrary",)), name=name + "_b",
        )(*row_args(rv), *rv[nr:], *cts)
        d_rows, pos = [], 0
        for i in range(nr):
            if halo[i]:
                d_rows.append(_unshift(res[pos], res[pos + 1], res[pos + 2], ctx_tiles * tr).astype(rows[i].dtype))
                pos += 3
            else:
                d_rows.append(res[pos])
                pos += 1
        return tuple(d_rows) + tuple(res[pos:])

    @jax.custom_vjp
    def f(*rv):
        return fwd_call(*rv)

    f.defvjp(lambda *rv: (fwd_call(*rv), rv), lambda rv, cts: bwd_call(rv, cts))
    return f(*rows, *vecs)


def _unshift(d, dm, dp, ctx_rows):
    r = d.shape[0]
    t = lax.broadcasted_iota(jnp.int32, (r, 1), 0)
    zero = jnp.zeros((1, d.shape[1]), d.dtype)
    from_m = jnp.concatenate([dm[1:], zero], axis=0)
    from_p = jnp.concatenate([zero, dp[:-1]], axis=0)
    from_m = jnp.where(t == ctx_rows - 1, 0.0, from_m)
    from_p = jnp.where(t == ctx_rows, 0.0, from_p)
    return d + from_m + from_p


def _sigmoid(x):
    return 0.5 * (jnp.tanh(0.5 * x) + 1.0)


def _silu(x):
    return x * _sigmoid(x)


def _softplus(x):
    return jnp.maximum(x, 0.0) + jnp.log(1.0 + jnp.exp(-jnp.abs(x)))


def _rms(x, g):
    return x * lax.rsqrt(jnp.mean(x * x, axis=-1, keepdims=True) + EPS) * g


def _fn_modulate(r, v):
    (x,), (g,), (sh,), (sc,) = r[0], v[0], v[1], v[2]
    return [[_rms(x, g) * (1.0 + sc) + sh]], []


def _fn_resmod(r, v):
    (x,), (y,) = r
    (gate,), (g,), (sh,), (sc,) = v
    xn = x + gate * y
    return [[xn], [_rms(xn, g) * (1.0 + sc) + sh]], []


def _fn_final(r, v):
    (x,), (y,), (tgt,) = r
    (gate,), (g,), (counts,) = v
    err = _rms(x + gate * y, g) - tgt
    row_loss = jnp.mean(err * err, axis=-1, keepdims=True)
    total = 0.5 * jnp.sum(row_loss, axis=0, keepdims=True)
    return [], [total * counts]


def _fn_conv(r, v):
    (x, xm, xp), = r[0]
    (w0,), (w1,), (w2,) = v
    return [[xm * w0 + x * w1 + xp * w2]], []


def _fn_conv_taps(r, v):
    (dy,), ((x, xm, xp),) = r
    col = lambda a: jnp.sum(dy * a, axis=0, keepdims=True)
    return [], [col(xm), col(x), col(xp)]


def conv3(name, x, w0, w1, w2, ctx_tiles, out_dtype):
    width = x.shape[1]

    def conv(tag, x, taps, dtype):
        return rowwise(name + tag, _fn_conv, [x], list(taps), halo=(True,), outs=[(width, dtype, 1)], ctx_tiles=ctx_tiles)[0]

    @jax.custom_vjp
    def f(x, w0, w1, w2):
        return conv("_y", x, (w0, w1, w2), out_dtype)

    def bwd(res, dy):
        x, w0, w1, w2 = res
        taps = rowwise(name + "_dw", _fn_conv_taps, [dy, x], [], halo=(False, True), reds=[width] * 3, ctx_tiles=ctx_tiles)
        return (conv("_dx", dy, (w2, w1, w0), x.dtype),) + tuple(taps)

    f.defvjp(lambda x, w0, w1, w2: (conv("_y", x, (w0, w1, w2), out_dtype), (x, w0, w1, w2)), bwd)
    return f(x, w0, w1, w2)


def _fn_dnprep(r, v):
    qkv, (ab,) = r
    (alog,), (dtb,) = v
    out = [[], [], []]
    for n, x in enumerate(qkv):
        which = n // DN_HEADS
        y = _silu(x)
        if which < 2:
            y = y * lax.rsqrt(jnp.sum(y * y, axis=-1, keepdims=True) + EPS)
        if which == 0:
            y = y * (HEAD_DIM ** -0.5)
        out[which].append(y)
    lane = lax.broadcasted_iota(jnp.int32, ab.shape, 1)
    g = -jnp.exp(alog) * _softplus(ab + dtb)
    gb = jnp.where(lane < N_DIRHEAD, g, jnp.where(lane < 2 * N_DIRHEAD, _sigmoid(ab), 0.0))
    return out + [[gb]], []


def _fn_dnpost(r, v):
    of, ob, z = r
    (g,) = v[0]
    return [[_rms(a + b, g) * _silu(c) for a, b, c in zip(of, ob, z)]], []


def _fn_sub(r, v):
    return [[r[0][0] - r[1][0]]], []


def _fn_scale(r, v):
    return [[r[0][0] * v[0][0]]], []


def _fn_shortconv(r, v):
    (xin, gb, gc), = r
    (w0,), (w1,), (w2,) = v
    u, um, up = (gc[k] * xin[k] for k in range(3))
    return [[gb[0] * (um * w0 + u * w1 + up * w2)]], []


def _fn_merge(r, v):
    gates, (ya,), (yb,), (yc,) = r
    return [[_sigmoid(gates[0]) * ya + _sigmoid(gates[1]) * yb + _sigmoid(gates[2]) * yc]], []


def _fn_swiglu(r, v):
    gate, up = r[0]
    return [[_silu(gate) * up]], []


def _dot3(a, b):
    (ah, al), (bh, bl) = _hi_lo(a), _hi_lo(b)
    dot = lambda x, y: lax.dot_general(x, y, (NN, ((), ())), preferred_element_type=F32)
    return dot(ah, bh) + (dot(ah, bl) + dot(al, bh))


def _bdot(a, b, dims):
    (ca,), (cb,) = dims
    return lax.dot_general(a.astype(MXU_DTYPE), b.astype(MXU_DTYPE), (((ca + 1,), (cb + 1,)), ((0,), (0,))),
                           preferred_element_type=F32)


def _hi_lo(a):
    hi = a.astype(MXU_DTYPE)
    return hi, (a - hi.astype(F32)).astype(MXU_DTYPE)


def _bdot3_raw(a, b, dims):
    (ah, al), (bh, bl) = _hi_lo(a), _hi_lo(b)
    return _bdot(ah, bh, dims) + (_bdot(ah, bl, dims) + _bdot(al, bh, dims))


@jax.custom_vjp
def _bdot3(a, b):
    return _bdot3_raw(a, b, NN)


_bdot3.defvjp(lambda a, b: (_bdot3_raw(a, b, NN), (a, b)),
              lambda res, ct: (_bdot3_raw(ct, res[1], NT), _bdot3_raw(res[0], ct, TN)))


def _inv_doubling(a):
    c = a.shape[-1]
    ii, jj = (lax.broadcasted_iota(jnp.int32, a.shape, d) for d in (1, 2))
    t = jnp.where(ii == jj, 1.0, 0.0) - a
    p = _bdot3_raw(a, a, NN)
    for _ in range(int(math.log2(c)) - 2):
        both = _bdot3_raw(jnp.concatenate([p, t], axis=1), p, NN)
        t = t + both[:, c:]
        p = both[:, :c]
    return t + _bdot3_raw(t, p, NN)


def _dn_gates(k4, gb):
    nb, c = N_DIRHEAD, k4.shape[1]
    k = jnp.concatenate([k4, k4], axis=0)
    lane = lax.broadcasted_iota(jnp.int32, gb.shape, 1)
    col = lambda j: jnp.sum(jnp.where(lane == j, gb, 0.0), axis=1, keepdims=True)
    g_col = jnp.concatenate([col(j)[None] for j in range(nb)], axis=0)
    b_col = jnp.concatenate([col(nb + j)[None] for j in range(nb)], axis=0)
    bi, ii, jj = (lax.broadcasted_iota(jnp.int32, (nb, c, c), a) for a in range(3))
    ahead = jnp.where(bi >= DN_HEADS, jj - ii, ii - jj)
    incl = ahead >= 0
    g_row = jnp.sum(jnp.where(ahead == 0, g_col, 0.0), axis=1, keepdims=True)
    gc_col = jnp.sum(jnp.where(incl, g_row, 0.0), axis=2, keepdims=True)
    gc_row = jnp.sum(jnp.where(ahead <= 0, g_col, 0.0), axis=1, keepdims=True)
    decay = jnp.where(incl, jnp.exp(jnp.where(incl, gc_col - gc_row, 0.0)), 0.0)
    return k, g_col, b_col, gc_col, decay, ahead > 0


def _dn_a(k4, gb):
    k, _, b_col, _, decay, strict = _dn_gates(k4, gb)
    return _bdot(k * b_col, k, NT) * jnp.where(strict, decay, 0.0)


def _dn_operands(q4, k4, v4, gb, t):
    k, g_col, b_col, gc_col, decay, _ = _dn_gates(k4, gb)
    q, v = (jnp.concatenate([a, a], axis=0) for a in (q4, v4))
    e_gc = jnp.exp(gc_col)
    d = k.shape[-1]
    uw = _bdot3(t, jnp.concatenate([v * b_col, k * b_col * e_gc], axis=2))
    u, w = uw[:, :, :d], uw[:, :, d:]
    g_last = jnp.sum(g_col, axis=1, keepdims=True)
    k_state = k * jnp.exp(g_last - gc_col)
    a_qk = _bdot(q, k, NT) * decay
    return u, w, q * e_gc, k_state, a_qk, jnp.broadcast_to(jnp.exp(g_last), (N_DIRHEAD, 1, LANES))


def _dn_step(s, u, w, qd, ks, aqk, gl):
    c = u.shape[1]
    on_state = _bdot(jnp.concatenate([w, qd], axis=1), s, NN)
    v_new = u - on_state[:, :c]
    o = on_state[:, c:] + _bdot(aqk, v_new, NN)
    return s * gl[:, :, :1] + _bdot(ks, v_new, TN), o


def _heads(x):
    return jnp.concatenate([x[None, :, h * HEAD_DIM:(h + 1) * HEAD_DIM] for h in range(DN_HEADS)], axis=0)


def _unheads(x):
    return jnp.concatenate([x[h] for h in range(x.shape[0])], axis=-1)


def _dn_rev(t, nc, n):
    return jnp.where(t < nc, nc - 1 - t, n - 1 - (t - nc))


PRE_CHUNKS = 4


def _pre_shapes(n):
    c, d, h = DN_CHUNK, HEAD_DIM, DN_HEADS
    shapes = [(n, h, c, d)] * 8 + [(n, h, c, c)] * 2 + [(n, h, 1, LANES)] * 2
    return shapes, [pl.BlockSpec((PRE_CHUNKS,) + s[1:], lambda t: (t, 0, 0, 0)) for s in shapes]


def _inverse_spec(n):
    shape = (n, N_DIRHEAD, DN_CHUNK, DN_CHUNK)
    return shape, pl.BlockSpec((PRE_CHUNKS,) + shape[1:], lambda t: (t, 0, 0, 0))


def _pre_row_specs():
    rows = PRE_CHUNKS * DN_CHUNK
    return pl.BlockSpec((rows, DN_HEADS * HEAD_DIM), lambda t: (t, 0)), pl.BlockSpec((rows, LANES), lambda t: (t, 0))


def _dn_pre_fwd(name, q, k, v, gb):
    n = q.shape[0] // DN_CHUNK
    shapes, specs = _pre_shapes(n)
    t_shape, t_spec = _inverse_spec(n)

    def body(q_ref, k_ref, v_ref, g_ref, *o_refs):
        for s in range(PRE_CHUNKS):
            rows = pl.ds(s * DN_CHUNK, DN_CHUNK)
            q4, k4, v4, gb_ = _heads(q_ref[rows, :]), _heads(k_ref[rows, :]), _heads(v_ref[rows, :]), g_ref[rows, :]
            t = _inv_doubling(_dn_a(k4, gb_))
            for i, r in enumerate(_dn_operands(q4, k4, v4, gb_, t)):
                o_refs[2 * i][s] = r[:DN_HEADS]
                o_refs[2 * i + 1][s] = r[DN_HEADS:]
            o_refs[-1][s] = t

    wide, narrow = _pre_row_specs()
    return pl.pallas_call(body, grid=(n // PRE_CHUNKS,), in_specs=[wide] * 3 + [narrow], out_specs=specs + [t_spec],
                          out_shape=[jax.ShapeDtypeStruct(s, F32) for s in shapes + [t_shape]],
                          compiler_params=_params(("parallel",)), name=name + "_pre_f")(q, k, v, gb)


def _dn_pre_bwd(name, q, k, v, gb, inv, cts):
    n = q.shape[0] // DN_CHUNK
    _, specs = _pre_shapes(n)
    _, t_spec = _inverse_spec(n)
    n_ct = len(specs)

    def body(q_ref, k_ref, v_ref, g_ref, t_ref, *refs):
        for s in range(PRE_CHUNKS):
            rows = pl.ds(s * DN_CHUNK, DN_CHUNK)
            ct = tuple(jnp.concatenate([refs[i][s], refs[i + 1][s]], axis=0) for i in range(0, n_ct, 2))
            q4, k4, v4, gb_, t = _heads(q_ref[rows, :]), _heads(k_ref[rows, :]), _heads(v_ref[rows, :]), g_ref[rows, :], t_ref[s]
            _, vjp = jax.vjp(_dn_operands, q4, k4, v4, gb_, t)
            dq, dk, dv, dg, dt = vjp(ct)
            da = -_bdot3_raw(_bdot3_raw(t, dt, TN), t, NT)
            _, vjp_a = jax.vjp(_dn_a, k4, gb_)
            dk_a, dg_a = vjp_a(da)
            for r, val in zip(refs[n_ct:], (_unheads(dq), _unheads(dk + dk_a), _unheads(dv), dg + dg_a)):
                r[rows, :] = val

    wide, narrow = _pre_row_specs()
    return pl.pallas_call(body, grid=(n // PRE_CHUNKS,), in_specs=[wide] * 3 + [narrow, t_spec] + specs,
                          out_specs=[wide] * 3 + [narrow],
                          out_shape=[jax.ShapeDtypeStruct(q.shape, F32)] * 3 + [jax.ShapeDtypeStruct(gb.shape, F32)],
                          compiler_params=_params(("parallel",)), name=name + "_pre_b")(q, k, v, gb, inv, *cts)


SCAN_CHUNKS = 4


def _scan_specs(pre_shapes, fw, bw):
    maps = (lambda t: (fw(t), 0, 0, 0), lambda t: (bw(t), 0, 0, 0))
    return [pl.BlockSpec((SCAN_CHUNKS,) + s[1:], maps[i % 2]) for i, s in enumerate(pre_shapes)]


def _dn_scan_fwd(name, pre, nc):
    n = pre[0].shape[0]
    r = n * DN_CHUNK
    hd, nh, nb, sc = HEAD_DIM, DN_HEADS, N_DIRHEAD, SCAN_CHUNKS
    assert n % sc == 0 and nc % sc == 0
    shapes, _ = _pre_shapes(n)
    n_in = len(shapes)

    def body(*refs):
        ins, (of_ref, ob_ref, sf_ref, sb_ref, s_scr) = refs[:n_in], refs[n_in:]
        t = pl.program_id(0)

        @pl.when(t == 0)
        def _():
            s_scr[...] = jnp.zeros_like(s_scr)

        s = s_scr[...]
        for j in range(sc):
            jf, jb = j, sc - 1 - j
            sf_ref[jf] = s[:nh]
            sb_ref[jb] = s[nh:]
            args = [jnp.concatenate([ins[i][jf], ins[i + 1][jb]], axis=0) for i in range(0, n_in, 2)]
            s, o = _dn_step(s, *args)
            of_ref[pl.ds(jf * DN_CHUNK, DN_CHUNK), :] = _unheads(o[:nh])
            ob_ref[pl.ds(jb * DN_CHUNK, DN_CHUNK), :] = _unheads(o[nh:])
        s_scr[...] = s

    fw = lambda t: t
    bw = lambda t: _dn_rev(sc * t + sc - 1, nc, n) // sc
    wide, st = (sc * DN_CHUNK, nh * hd), (sc, nh, hd, hd)
    return pl.pallas_call(
        body, grid=(n // sc,), in_specs=_scan_specs(shapes, fw, bw),
        out_specs=[pl.BlockSpec(wide, lambda t: (fw(t), 0)), pl.BlockSpec(wide, lambda t: (bw(t), 0)),
                   pl.BlockSpec(st, lambda t: (fw(t), 0, 0, 0)), pl.BlockSpec(st, lambda t: (bw(t), 0, 0, 0))],
        out_shape=[jax.ShapeDtypeStruct((r, nh * hd), F32)] * 2 + [jax.ShapeDtypeStruct((n, nh, hd, hd), F32)] * 2,
        scratch_shapes=[pltpu.VMEM((nb, hd, hd), F32)],
        compiler_params=_params(("arbitrary",)), name=name + "_scan_f",
    )(*pre)


def _dn_scan_bwd(name, pre, sall_f, sall_b, do_f, do_b, nc):
    n = pre[0].shape[0]
    hd, nh, nb, sc = HEAD_DIM, DN_HEADS, N_DIRHEAD, SCAN_CHUNKS
    shapes, _ = _pre_shapes(n)
    n_in = len(shapes)

    def body(*refs):
        ins, (sf_ref, sb_ref, dof_ref, dob_ref) = refs[:n_in], refs[n_in:n_in + 4]
        outs, ds_scr = refs[n_in + 4:2 * n_in + 4], refs[2 * n_in + 4]
        t = pl.program_id(0)

        @pl.when(t == 0)
        def _():
            ds_scr[...] = jnp.zeros_like(ds_scr)

        ds = ds_scr[...]
        for j in range(sc):
            jf, jb = sc - 1 - j, j
            args = [jnp.concatenate([ins[i][jf], ins[i + 1][jb]], axis=0) for i in range(0, n_in, 2)]
            s = jnp.concatenate([sf_ref[jf], sb_ref[jb]], axis=0)
            do = jnp.concatenate([_heads(dof_ref[pl.ds(jf * DN_CHUNK, DN_CHUNK), :]),
                                  _heads(dob_ref[pl.ds(jb * DN_CHUNK, DN_CHUNK), :])], axis=0)
            _, vjp = jax.vjp(_dn_step, s, *args)
            cts = vjp((ds, do))
            ds = cts[0]
            for i, ct in enumerate(cts[1:]):
                outs[2 * i][jf] = ct[:nh]
                outs[2 * i + 1][jb] = ct[nh:]
        ds_scr[...] = ds

    fw = lambda t: n // sc - 1 - t
    bw = lambda t: _dn_rev(n - 1 - sc * t, nc, n) // sc
    wide, st = (sc * DN_CHUNK, nh * hd), (sc, nh, hd, hd)
    return tuple(pl.pallas_call(
        body, grid=(n // sc,),
        in_specs=_scan_specs(shapes, fw, bw)
        + [pl.BlockSpec(st, lambda t: (fw(t), 0, 0, 0)), pl.BlockSpec(st, lambda t: (bw(t), 0, 0, 0)),
           pl.BlockSpec(wide, lambda t: (fw(t), 0)), pl.BlockSpec(wide, lambda t: (bw(t), 0))],
        out_specs=_scan_specs(shapes, fw, bw), out_shape=[jax.ShapeDtypeStruct(s, F32) for s in shapes],
        scratch_shapes=[pltpu.VMEM((nb, hd, hd), F32)],
        compiler_params=_params(("arbitrary",)), name=name + "_scan_b",
    )(*pre, sall_f, sall_b, do_f, do_b))


def deltanet(name, q, k, v, gb, ctx_rows):
    nc = ctx_rows // DN_CHUNK

    @jax.custom_vjp
    def pre(q, k, v, gb):
        return tuple(_dn_pre_fwd(name, q, k, v, gb)[:-1])

    def pre_fwd(*a):
        *ops, inv = _dn_pre_fwd(name, *a)
        return tuple(ops), (a, inv)

    pre.defvjp(pre_fwd, lambda res, cts: tuple(_dn_pre_bwd(name, *res[0], res[1], cts)))

    @jax.custom_vjp
    def scan(*ops):
        return tuple(_dn_scan_fwd(name, ops, nc)[:2])

    def scan_fwd(*ops):
        of, ob, sf, sb = _dn_scan_fwd(name, ops, nc)
        return (of, ob), (ops, sf, sb)

    scan.defvjp(scan_fwd, lambda res, cts: _dn_scan_bwd(name, res[0], res[1], res[2], cts[0], cts[1], nc))
    return scan(*pre(q, k, v, gb))


def _box_matrix(l, w):
    lo, hi = w // 2, w - 1 - w // 2
    pos = np.arange(l)
    start, end = np.clip(pos - lo, 0, l), np.clip(pos + hi + 1, 0, l)
    col = np.arange(l)[None, :]
    return ((col >= start[:, None]) & (col < end[:, None])) / (end - start)[:, None].astype(np.float64)


def _pool_matrices(ctx_rows, grid_rows):
    assert ctx_rows == ROW_TILE and ROW_TILE % GRID_W == 0
    ctx = np.stack([_box_matrix(ctx_rows, w) for w in POOL_WINDOWS])
    cols = np.stack([np.kron(np.eye(ROW_TILE // GRID_W), _box_matrix(GRID_W, w)) for w in POOL_WINDOWS])
    rows = np.stack([_box_matrix(grid_rows, w) for w in POOL_WINDOWS])[None]
    return np.stack([ctx, cols]).astype(np.float32), rows.astype(np.float32)


def _pool_apply(name, x, mats, group_w, seg_tiles, lane_tile):
    r, w = x.shape
    nseg, ng, b, _ = mats.shape
    period = ng * group_w
    per = _pick(r // b, (3, 1))

    def body(x_ref, m_ref, o_ref):
        i = pl.program_id(0)
        for s in range(per):
            rows = pl.ds(s * b, b)
            xv = x_ref[rows, :].astype(F32)
            seg = jnp.where(i * per + s >= seg_tiles, 1, 0) if nseg > 1 else 0
            lane = lax.broadcasted_iota(jnp.int32, xv.shape, 1)
            grp = (lane % period) // group_w
            acc = jnp.zeros_like(xv)
            for g in range(ng):
                acc = acc + jnp.where(grp == g, _dot3(m_ref[seg, g], xv), 0.0)
            o_ref[rows, :] = acc

    return pl.pallas_call(
        body, grid=(r // (per * b), w // lane_tile),
        in_specs=[pl.BlockSpec((per * b, lane_tile), lambda i, j: (i, j)),
                  pl.BlockSpec((nseg, ng, b, b), lambda i, j: (0, 0, 0, 0))],
        out_specs=pl.BlockSpec((per * b, lane_tile), lambda i, j: (i, j)),
        out_shape=jax.ShapeDtypeStruct(x.shape, F32),
        compiler_params=_params(("parallel", "parallel")), name=name,
    )(x, mats)


def pool_means(name, u, ctx_rows):
    r, pw = u.shape
    grid_rows = (r - ctx_rows) // GRID_W
    gw = pw // len(POOL_WINDOWS)
    m1, m2 = _pool_matrices(ctx_rows, grid_rows)
    lane_tile = min(2048, GRID_W * pw)

    def apply(x, a1, a2, tag):
        y = _pool_apply(name + tag + "1", x, jnp.asarray(a1), gw, ctx_rows // ROW_TILE, pw)
        lat = y[ctx_rows:].reshape(grid_rows, GRID_W * pw)
        lat = _pool_apply(name + tag + "2", lat, jnp.asarray(a2), gw, 0, lane_tile)
        return jnp.concatenate([y[:ctx_rows], lat.reshape(r - ctx_rows, pw)], axis=0)

    @jax.custom_vjp
    def f(u):
        return apply(u, m1, m2, "_f")

    tr = lambda m: np.ascontiguousarray(np.swapaxes(m, -1, -2))
    f.defvjp(lambda u: (apply(u, m1, m2, "_f"), None), lambda _, ct: (apply(ct, tr(m1), tr(m2), "_b").astype(u.dtype),))
    return f(u)


def _ew(name, fn, *xs):
    shapes = jax.eval_shape(lambda *a: fn(*a), *xs)
    shapes = shapes if isinstance(shapes, (tuple, list)) else (shapes,)

    def body(*refs):
        res = fn(*[r[...] for r in refs[:len(xs)]])
        res = res if isinstance(res, (tuple, list)) else (res,)
        for r, o in zip(res, refs[len(xs):]):
            o[...] = r

    out = pl.pallas_call(body, out_shape=[jax.ShapeDtypeStruct(s.shape, s.dtype) for s in shapes],
                         compiler_params=_params(), name=name)(*xs)
    return out[0] if len(shapes) == 1 else tuple(out)


def _adamw_math(w, g, m, v):
    m2 = ADAM_B1 * m + (1.0 - ADAM_B1) * g
    v2 = ADAM_B2 * v + (1.0 - ADAM_B2) * (g * g)
    m_hat = m2 / (1.0 - ADAM_B1 ** ADAM_STEP)
    v_hat = v2 / (1.0 - ADAM_B2 ** ADAM_STEP)
    delta = -ADAM_LR * (m_hat / (jnp.sqrt(v_hat) + ADAM_EPS) + ADAM_WD * w)
    return delta, m2, v2


def adamw(name, w, g, m, v):
    *lead, r, c = w.shape
    tr = _pick(r, (256, 128, 64, 32, 16, 8))

    def body(w_ref, g_ref, m_ref, v_ref, d_ref, m2_ref, v2_ref):
        d_ref[...], m2_ref[...], v2_ref[...] = _adamw_math(w_ref[...], g_ref[...], m_ref[...], v_ref[...])

    if lead:
        grid, spec = (lead[0], r // tr), pl.BlockSpec((None, tr, c), lambda l, i: (l, i, 0))
    else:
        grid, spec = (r // tr,), pl.BlockSpec((tr, c), lambda i: (i, 0))
    return pl.pallas_call(body, grid=grid, in_specs=[spec] * 4, out_specs=[spec] * 3,
                          out_shape=[jax.ShapeDtypeStruct(w.shape, F32)] * 3,
                          compiler_params=_params(("parallel",) * len(grid)), name=name)(w, g, m, v)


def _sum_rows(name, xs, out_dtype=F32):
    r, c = xs[0].shape
    tr = _pick(r, tuple(t for t in (512, 256, 128, 64, 32, 16, 8) if t * c * 4 <= 2 << 20))

    def body(*refs):
        acc = refs[0][...].astype(F32)
        for ref in refs[1:-1]:
            acc = acc + ref[...].astype(F32)
        refs[-1][...] = acc.astype(out_dtype)

    spec = pl.BlockSpec((tr, c), lambda i: (i, 0))
    return pl.pallas_call(body, grid=(r // tr,), in_specs=[spec] * len(xs), out_specs=spec,
                          out_shape=jax.ShapeDtypeStruct((r, c), out_dtype),
                          compiler_params=_params(("parallel",)), name=name)(*xs)


def _place():
    return lax.axis_index("x"), lax.axis_index("y"), lax.axis_index("c")


def _chip_peers(x, y, c):
    return [(1 - x, y, c), (x, 1 - y, c), (1 - x, 1 - y, c)]


def all_gather8(name, block):
    m_per, n = block.shape

    def body(x_ref, out_ref, send_sems, recv_sems, local_sem):
        x, y, c = _place()
        me, sibling = (x, y, c), (x, y, 1 - c)
        chips = [(1 - x, y), (x, 1 - y), (1 - x, 1 - y)]

        def rows(px, py, pc):
            return out_ref.at[pl.ds((4 * px + 2 * py + pc) * m_per, m_per), :]

        def copy(k, blk, to, src=None):
            return pltpu.make_async_remote_copy(
                src_ref=rows(*blk) if src is None else src, dst_ref=rows(*blk),
                send_sem=send_sems.at[k], recv_sem=recv_sems.at[k], device_id=to, device_id_type=MESH_ID)

        mine = pltpu.make_async_copy(x_ref, rows(*me), local_sem)
        mine.start()
        first = [copy(0, me, sibling, src=x_ref)]
        first += [copy(1 + j, me, (*chip, c), src=x_ref) for j, chip in enumerate(chips)]
        for cp in first:
            cp.start()
        passed = [copy(4 + j, (*chip, c), sibling) for j, chip in enumerate(chips)]
        for j, chip in enumerate(chips):
            copy(1 + j, (*chip, c), me).wait_recv()
            passed[j].start()
        copy(0, sibling, me).wait_recv()
        for j, chip in enumerate(chips):
            copy(4 + j, (*chip, 1 - c), me).wait_recv()
        for cp in first + passed:
            cp.wait_send()
        mine.wait()

    return pl.pallas_call(
        body, out_shape=jax.ShapeDtypeStruct((8 * m_per, n), block.dtype),
        in_specs=[pl.BlockSpec(memory_space=pltpu.VMEM)], out_specs=pl.BlockSpec(memory_space=pltpu.VMEM),
        scratch_shapes=[pltpu.SemaphoreType.DMA((7,)), pltpu.SemaphoreType.DMA((7,)), pltpu.SemaphoreType.DMA],
        compiler_params=_params(), name=name,
    )(block)


N_PEERS = 3


def gather_chips(name, shards):
    n = len(shards)

    def body(*refs):
        xs, outs, (send_sems, recv_sems, local_sems) = refs[:n], refs[n:2 * n], refs[2 * n:]
        x, y, c = _place()
        local, remote = [], []
        for i in range(n):
            local.append(pltpu.make_async_copy(xs[i].at[c], outs[i].at[2 * x + y], local_sems.at[i]))
            local[-1].start()
            for p, peer in enumerate(_chip_peers(x, y, c)):
                remote.append(pltpu.make_async_remote_copy(
                    src_ref=xs[i].at[c], dst_ref=outs[i].at[2 * x + y], send_sem=send_sems.at[i * N_PEERS + p],
                    recv_sem=recv_sems.at[i * N_PEERS + p], device_id=peer, device_id_type=MESH_ID))
                remote[-1].start()
        for i in range(n):
            for p, (px, py, _) in enumerate(_chip_peers(x, y, c)):
                pltpu.make_async_remote_copy(
                    src_ref=xs[i].at[c], dst_ref=outs[i].at[2 * px + py], send_sem=send_sems.at[i * N_PEERS + p],
                    recv_sem=recv_sems.at[i * N_PEERS + p], device_id=(px, py, c), device_id_type=MESH_ID).wait_recv()
        for cp in remote:
            cp.wait_send()
        for cp in local:
            cp.wait()

    hbm = pl.BlockSpec(memory_space=pltpu.HBM)
    return pl.pallas_call(
        body, out_shape=[jax.ShapeDtypeStruct((N_CHIPS,) + s.shape[1:], s.dtype) for s in shards],
        in_specs=[hbm] * n, out_specs=[hbm] * n,
        scratch_shapes=[pltpu.SemaphoreType.DMA((n * N_PEERS,)), pltpu.SemaphoreType.DMA((n * N_PEERS,)),
                        pltpu.SemaphoreType.DMA((n,))],
        compiler_params=_params(), name=name,
    )(*shards)


def swap_sibling(name, blocks):
    n = len(blocks)
    pairs = [tuple(b) if isinstance(b, (tuple, list)) else (b,) for b in blocks]
    flat = [a for p in pairs for a in p]
    first = [sum(len(q) for q in pairs[:i]) for i in range(n)]

    def body(*refs):
        xs, outs, (send_sems, recv_sems) = refs[:len(flat)], refs[len(flat):len(flat) + n], refs[len(flat) + n:]
        x, y, c = _place()

        def copy(i, src):
            return pltpu.make_async_remote_copy(src_ref=src, dst_ref=outs[i], send_sem=send_sems.at[i], recv_sem=recv_sems.at[i],
                                                device_id=(x, y, 1 - c), device_id_type=MESH_ID)

        for i, p in enumerate(pairs):
            if len(p) == 1:
                copy(i, xs[first[i]]).start()
            else:
                for half in range(2):
                    pl.when(c == 1 - half)(copy(i, xs[first[i] + half]).start)
        for i in range(n):
            copy(i, xs[first[i]]).wait()

    hbm = pl.BlockSpec(memory_space=pltpu.HBM)
    return pl.pallas_call(
        body, out_shape=[jax.ShapeDtypeStruct(p[0].shape, p[0].dtype) for p in pairs],
        in_specs=[hbm] * len(flat), out_specs=[hbm] * n,
        scratch_shapes=[pltpu.SemaphoreType.DMA((n,)), pltpu.SemaphoreType.DMA((n,))],
        compiler_params=_params(), name=name,
    )(*flat)


def scatter_chips(name, pieces):
    n = len(pieces)

    def body(*refs):
        xs, outs, (send_sems, recv_sems) = refs[:n], refs[n:2 * n], refs[2 * n:]
        x, y, c = _place()
        copies = []
        for i in range(n):
            for p, (px, py, pc) in enumerate(_chip_peers(x, y, c)):
                copies.append(pltpu.make_async_remote_copy(
                    src_ref=xs[i].at[2 * px + py], dst_ref=outs[i].at[p], send_sem=send_sems.at[i * N_PEERS + p],
                    recv_sem=recv_sems.at[i * N_PEERS + p], device_id=(px, py, pc), device_id_type=MESH_ID))
                copies[-1].start()
        for cp in copies:
            cp.wait()

    hbm = pl.BlockSpec(memory_space=pltpu.HBM)
    return pl.pallas_call(
        body, out_shape=[jax.ShapeDtypeStruct((N_PEERS,) + p.shape[1:], p.dtype) for p in pieces],
        in_specs=[hbm] * n, out_specs=[hbm] * n,
        scratch_shapes=[pltpu.SemaphoreType.DMA((n * N_PEERS,)), pltpu.SemaphoreType.DMA((n * N_PEERS,))],
        compiler_params=_params(), name=name,
    )(*pieces)


def pair_sum(name, half0, half1, got, out_dtype):
    r, w = got.shape
    tr = _pick(r, tuple(t for t in (512, 256, 128, 64, 32, 16, 8) if t * w * 4 <= 2 << 20))

    def body(a0, a1, g, o):
        mine = jnp.where(lax.axis_index("c") == 0, a0[...], a1[...])
        o[...] = (mine.astype(F32) + g[...].astype(F32)).astype(out_dtype)

    spec = pl.BlockSpec((tr, w), lambda i: (i, 0))
    return pl.pallas_call(body, grid=(r // tr,), in_specs=[spec] * 3, out_specs=spec,
                          out_shape=jax.ShapeDtypeStruct((r, w), out_dtype),
                          compiler_params=_params(("parallel",)), name=name)(half0, half1, got)


def _sum_devices(name, got, fold=False):
    def body(a_ref, *o_refs):
        acc = a_ref[0]
        for i in range(1, N_DEV):
            acc = acc + a_ref[i]
        o_refs[0][...] = acc
        if fold:
            o_refs[1][...] = acc + pltpu.roll(acc, SUBLANES // 2, 0)

    shape = jax.ShapeDtypeStruct(got.shape[1:], F32)
    out = pl.pallas_call(body, out_shape=[shape] * (2 if fold else 1), compiler_params=_params(), name=name)(got)
    return out if fold else out[0]


def _w_in_bounds(d, pw, scw):
    off_z = 3 * DN_WIDTH
    off_a = off_z + DN_WIDTH
    off_pool = off_a + 2 * N_DIRHEAD
    off_sc = off_pool + pw
    off_gate = off_sc + 3 * scw
    return (0, off_z, off_a, off_pool, off_sc, off_gate, off_gate + 3 * d)


def _misc_widths(pw, scw):
    return (DN_WIDTH, pw, 3 * scw, LANES)


def split_matrices(full, pw, scw):
    depth, d = len(full["w_in"]), full["w_in"][0].shape[0]
    b = _w_in_bounds(d, pw, scw)
    out = {k: [] for k in ("qkv", "gate", "misc", "gu", "br_a", "br_b", "br_c", "o", "down")}
    for l in range(depth):
        w = full["w_in"][l]
        out["qkv"].append(w[:, b[0]:b[1]])
        out["gate"].append(w[:, b[5]:b[6]])
        out["misc"].append(jnp.concatenate([w[:, b[1]:b[2]], w[:, b[3]:b[5]], _pad_lanes(w[:, b[2]:b[3]])], axis=1))
        out["gu"].append(full["w_gu"][l])
        for k in ("br_a", "br_b", "br_c", "o", "down"):
            out[k].append(full["w_" + k][l])
    return out


def join_matrix_grads(g, pw, scw):
    depth = len(g["qkv"])
    n_z, n_rest, n_ab = DN_WIDTH, pw + 3 * scw, 2 * N_DIRHEAD
    w_in = []
    for l in range(depth):
        m = g["misc"][l]
        w_in.append(jnp.concatenate([g["qkv"][l], m[:, :n_z], m[:, n_z + n_rest:n_z + n_rest + n_ab],
                                     m[:, n_z:n_z + n_rest], g["gate"][l]], axis=1))
    out = {"w_in": w_in, "w_gu": g["gu"]}
    for k in ("br_a", "br_b", "br_c", "o", "down"):
        out["w_" + k] = g[k]
    return {k: [a.astype(MXU_DTYPE) for a in v] for k, v in out.items()}


def split_cols(x, widths):
    edges = np.cumsum((0,) + tuple(widths))

    def cut(x):
        return tuple(x[:, a:b] for a, b in zip(edges[:-1], edges[1:]))

    f = jax.custom_vjp(cut)
    f.defvjp(lambda x: (cut(x), None), lambda _, cts: (jnp.concatenate(cts, axis=1),))
    return f(x)


def _row(v):
    return v.reshape(1, -1)


def _pad_lanes(v, width=LANES):
    return jnp.pad(v, ((0, 0), (0, width - v.shape[1])))


def _block_diag(blocks):
    g, n, _ = blocks.shape
    out = jnp.zeros((g * n, g * n), blocks.dtype)
    for i in range(g):
        out = out.at[i * n:(i + 1) * n, i * n:(i + 1) * n].set(blocks[i])
    return out


def local_loss(p, carriers, x, mod_lat, mod_ctx, wts, ctx, target):
    ctx_rows, d = ctx.shape
    depth = len(wts["qkv"])
    pw, scw = p["pool_scale"].shape[1], p["sc_conv_w"].shape[2]
    ff = wts["down"][0].shape[0]
    ct = ctx_rows // ROW_TILE
    xs = jnp.concatenate([ctx, x], axis=0)
    seg = lambda l, k: jnp.stack([mod_ctx[l, k], mod_lat[l, k]]).reshape(2, 1, d)
    dn = gate2 = None
    for l in range(depth):
        sh1, sc1, g1, sh2, sc2, g2 = (seg(l, k) for k in range(6))
        tag = f"l{l}_"
        lin = lambda key, a: mm(tag + key, a, wts[key][l], carriers[key][l])
        if l == 0:
            (h1,) = rowwise(tag + "mod", _fn_modulate, [xs], [_row(p["norm1_g"][l]), sh1, sc1], seg=(False, True, True),
                            outs=[(d, MXU_DTYPE, 1)], ctx_tiles=ct)
        else:
            xs, h1 = rowwise(tag + "resmod1", _fn_resmod, [xs, dn], [gate2, _row(p["norm1_g"][l]), sh1, sc1],
                             seg=(True, False, True, True), outs=[(d, F32, 1), (d, MXU_DTYPE, 1)], ctx_tiles=ct)
        fan = ("qkv", "gate", "misc")
        p_qkv, p_gate, p_misc = mm_fanout(tag + "in", h1, [wts[k][l] for k in fan], [carriers[k][l] for k in fan])
        p_z, p_pool, p_sc, p_ab = split_cols(p_misc, _misc_widths(pw, scw))
        cw = p["dn_conv_w"][l]
        qkv_conv = conv3(tag + "dnconv", p_qkv, cw[0:1], cw[1:2], cw[2:3], ct, F32)
        q, k, v, gb = rowwise(
            tag + "dnprep", _fn_dnprep, [qkv_conv, p_ab],
            [_pad_lanes(p["dn_a_log"][l].reshape(1, -1)), _pad_lanes(p["dn_dt_bias"][l].reshape(1, -1))],
            parts_r=(3 * DN_HEADS, 1), outs=[(DN_WIDTH, F32, DN_HEADS)] * 3 + [(LANES, F32, 1)], long_tiles=768)
        o_f, o_b = deltanet(tag + "dn", q, k, v, gb, ctx_rows)
        (oz,) = rowwise(tag + "dnpost", _fn_dnpost, [o_f, o_b, p_z], [_row(p["dn_norm_g"][l])], parts_r=(DN_HEADS,) * 3,
                        outs=[(DN_WIDTH, MXU_DTYPE, DN_HEADS)], long_tiles=768)
        y_a = lin("br_a", oz)
        means = pool_means(tag + "box", p_pool, ctx_rows)
        (dpool,) = rowwise(tag + "poolsub", _fn_sub, [means, p_pool], [], outs=[(pw, MXU_DTYPE, 1)], long_tiles=2816)
        pool_mat = _block_diag(p["pool_w"][l])
        mixed = mm(tag + "poolw", dpool, pool_mat.astype(MXU_DTYPE), pool_mat)
        (yb_in,) = rowwise(tag + "poolscale", _fn_scale, [mixed], [_row(p["pool_scale"][l])], outs=[(pw, MXU_DTYPE, 1)],
                           long_tiles=2816)
        y_b = lin("br_b", yb_in)
        sw = p["sc_conv_w"][l]
        (yc_in,) = rowwise(tag + "sconv", _fn_shortconv, [p_sc], [sw[0:1], sw[1:2], sw[2:3]], halo=(True,), parts_r=(3,),
                           outs=[(scw, MXU_DTYPE, 1)], ctx_tiles=ct)
        y_c = lin("br_c", yc_in)
        (y,) = rowwise(tag + "merge", _fn_merge, [p_gate, y_a, y_b, y_c], [], parts_r=(3, 1, 1, 1),
                       outs=[(d, MXU_DTYPE, 1)], long_tiles=768)
        mix = lin("o", y)
        xs, h2 = rowwise(tag + "resmod2", _fn_resmod, [xs, mix], [g1, _row(p["norm2_g"][l]), sh2, sc2],
                         seg=(True, False, True, True), outs=[(d, F32, 1), (d, MXU_DTYPE, 1)], ctx_tiles=ct)
        (act,) = rowwise(tag + "swiglu", _fn_swiglu, [lin("gu", h2)], [], parts_r=(2,),
                         outs=[(ff, MXU_DTYPE, 1)], long_tiles=768)
        dn = lin("down", act)
        gate2 = g2
    counts = jnp.concatenate([jnp.zeros((1, 1, LANES), F32), jnp.ones((1, 1, LANES), F32)])
    (total,) = rowwise("final", _fn_final, [xs, dn, target], [gate2, _row(p["final_norm_g"]), counts],
                       seg=(True, False, True), reds=[LANES], ctx_tiles=ct, skip=(0, 0, ct))
    return total[0, 0]


BIG = ("w_in", "w_br_a", "w_br_b", "w_br_c", "w_o", "w_gu", "w_down")
ROW_SHARDED = ("w_o", "w_down")
SMALL_REPL = ("norm1_g", "norm2_g", "dn_a_log", "dn_dt_bias", "dn_norm_g", "pool_w", "pool_scale", "final_norm_g")
SMALL_SHARD = ("dn_conv_w", "sc_conv_w")
WEIGHTS = ("c_ctx", "w_ada", "b_ada", "norm1_g", "norm2_g", "w_in", "dn_conv_w", "dn_a_log", "dn_dt_bias", "dn_norm_g", "pool_w",
           "pool_scale", "sc_conv_w", "w_br_a", "w_br_b", "w_br_c", "w_o", "w_gu", "w_down", "final_norm_g")
N_CHIPS, N_DEV = 4, 8
COMM_COLS = 1024


def _pack(arrays, rows_multiple, cols=COMM_COLS, dtype=F32):
    size = sum(int(np.prod(a.shape)) for a in arrays)
    rows = -(-size // (cols * rows_multiple)) * rows_multiple
    tail = [jnp.zeros((rows * cols - size,), dtype)] if rows * cols > size else []
    return jnp.concatenate([a.astype(dtype).reshape(-1) for a in arrays] + tail).reshape(rows, cols)


def _unpack(flat, shapes):
    out, pos = [], 0
    for s in shapes:
        n = int(np.prod(s))
        out.append(flat[pos:pos + n].reshape(s))
        pos += n
    return out


def _join_shards(parts, name):
    _, a, b = parts.shape
    if name in ROW_SHARDED:
        return parts.reshape(N_CHIPS * a, b)
    return jnp.moveaxis(parts, 0, 1).reshape(a, N_CHIPS * b)


def _cut_shards(full, name):
    k, n = full.shape
    if name in ROW_SHARDED:
        return full.reshape(N_CHIPS, k // N_CHIPS, n)
    return jnp.moveaxis(full.reshape(k, N_CHIPS, n // N_CHIPS), 1, 0)


def _dsilu(x):
    s = _sigmoid(x)
    return s + x * s * (1.0 - s)


def kernel(x, c, ctx, c_ctx, w_ada, b_ada, norm1_g, norm2_g, w_in, dn_conv_w, dn_a_log, dn_dt_bias, dn_norm_g, pool_w, pool_scale, sc_conv_w, w_br_a, w_br_b, w_br_c, w_o, w_gu, w_down, final_norm_g, loss_target, m_c_ctx, m_w_ada, m_b_ada, m_norm1_g, m_norm2_g, m_w_in, m_dn_conv_w, m_dn_a_log, m_dn_dt_bias, m_dn_norm_g, m_pool_w, m_pool_scale, m_sc_conv_w, m_w_br_a, m_w_br_b, m_w_br_c, m_w_o, m_w_gu, m_w_down, m_final_norm_g, v_c_ctx, v_w_ada, v_b_ada, v_norm1_g, v_norm2_g, v_w_in, v_dn_conv_w, v_dn_a_log, v_dn_dt_bias, v_dn_norm_g, v_pool_w, v_pool_scale, v_sc_conv_w, v_w_br_a, v_w_br_b, v_w_br_c, v_w_o, v_w_gu, v_w_down, v_final_norm_g):
    given = dict(locals())
    ix, iy, ic = _place()
    chip, dev = 2 * ix + iy, 4 * ix + 2 * iy + ic
    d = x.shape[-1]
    depth = w_in.shape[0]
    ada_cols = w_ada.shape[2]
    spare = 2 * SUBLANES - N_DEV - 1

    got0 = all_gather8("gather_cond", _pack([c, dn_conv_w, sc_conv_w], SUBLANES)).reshape(N_DEV, -1)
    c_all = got0[:, :d]
    taps = [_unpack(got0[2 * j, d:], [dn_conv_w.shape, sc_conv_w.shape]) for j in range(N_CHIPS)]
    full = {"dn_conv_w": jnp.concatenate([t[0] for t in taps], axis=-1),
            "sc_conv_w": jnp.concatenate([t[1] for t in taps], axis=-1)}
    assert depth == 2, "the two cores of a chip split the layers between them"
    got_layer = gather_chips("gather_weights", [given[n].astype(MXU_DTYPE) for n in BIG])
    other_layer = swap_sibling("swap_weights", got_layer)
    mats = {n: [_join_shards(jnp.where(ic == l, got_layer[i], other_layer[i]), n) for l in range(depth)]
            for i, n in enumerate(BIG)}
    for n in SMALL_REPL:
        full[n] = given[n]
    pw, scw = pool_scale.shape[1], full["sc_conv_w"].shape[2]
    wts = split_matrices(mats, pw, scw)
    carriers = {k: [jnp.zeros(w.shape, F32) for w in ws] for k, ws in wts.items()}

    cond = jnp.concatenate([c_all, c_ctx[None], jnp.zeros((spare, d), F32)])
    s_cond = _ew("silu_cond", _silu, cond)
    mod_cols = jnp.concatenate([_mm(f"ada{l}_f", s_cond, w_ada[l], NN) for l in range(depth)], axis=1)
    mod_got = all_gather8("gather_mod", mod_cols).reshape(N_DEV, 2 * SUBLANES, depth, ada_cols)
    mod_all = jnp.concatenate([mod_got[2 * j] for j in range(N_CHIPS)], axis=-1) + b_ada[None]
    mod_lat = lax.dynamic_index_in_dim(mod_all, dev, 0, keepdims=False).reshape(depth, 6, d)
    mod_ctx = mod_all[N_DEV].reshape(depth, 6, d)

    loss_local, (g_full, g_mats, grad_x, g_mod_lat, g_mod_ctx) = jax.value_and_grad(local_loss, argnums=(0, 1, 2, 3, 4))(
        full, carriers, x[0], mod_lat, mod_ctx, wts, ctx[0], loss_target[0])
    g_mats_full = join_matrix_grads(g_mats, pw, scw)
    loss = lax.psum(loss_local, ("x", "y", "c"))

    dmod_cols = (2 * depth * 6 * d) // SUBLANES
    dmod = all_gather8("gather_dmod", _pack([g_mod_lat, g_mod_ctx], SUBLANES, cols=dmod_cols))
    dmod = dmod.reshape(N_DEV, SUBLANES, dmod_cols)
    dmod_sum, dmod_fold = _sum_devices("reduce_dmod", dmod, fold=True)
    half_rows = SUBLANES // 2
    grad_b_ada = dmod_fold[:half_rows].reshape(depth, 6 * d)
    dctx_sum = dmod_sum[half_rows:].reshape(1, depth, 6 * d)
    d9 = jnp.concatenate([dmod[:, :half_rows].reshape(N_DEV, depth, 6 * d), dctx_sum, jnp.zeros((spare, depth, 6 * d), F32)])
    d9 = lax.dynamic_slice_in_dim(d9, chip * ada_cols, ada_cols, axis=2)
    grad_w_ada = jnp.stack([_mm(f"ada{l}_dw", s_cond, d9[:, l], TN) for l in range(depth)])
    ds_cond = [_mm(f"ada{l}_da", d9[:, l], w_ada[l], NT) for l in range(depth)]
    dsilu_part = _sum_rows("sum_dcond", ds_cond)[N_DEV]

    small_names = SMALL_REPL + SMALL_SHARD
    small_grads = [dsilu_part] + [g_full[n] for n in small_names]
    small_shapes = [a.shape for a in small_grads]
    n_small = sum(int(np.prod(s)) for s in small_shapes)
    cols = -(-n_small // (SUBLANES * LANES)) * LANES
    got2 = all_gather8("gather_small", _pack(small_grads, SUBLANES, cols=cols)).reshape(N_DEV, SUBLANES, cols)
    small_sum = _unpack(_sum_devices("reduce_small", got2).reshape(-1), small_shapes)
    grads = dict(zip(small_names, small_sum[1:]))
    grads["c_ctx"] = _ew("dsilu", lambda g, z: 0.5 * g * _dsilu(z), _row(small_sum[0]), _row(c_ctx)).reshape(-1)
    grads["w_ada"], grads["b_ada"] = grad_w_ada, grad_b_ada
    for n in SMALL_SHARD:
        width = given[n].shape[-1]
        grads[n] = lax.dynamic_slice_in_dim(grads[n], chip * width, width, axis=-1)

    from_sibling = swap_sibling("swap_layers", [tuple(g_mats_full[n]) for n in BIG])
    pieces = [_cut_shards(pair_sum("sum_pair_" + n, *g_mats_full[n], from_sibling[i], MXU_DTYPE), n) for i, n in enumerate(BIG)]
    from_chips = scatter_chips("scatter_pieces", pieces)
    reduced = []
    for i, n in enumerate(BIG):
        own = lax.dynamic_index_in_dim(pieces[i], chip, 0, keepdims=False)
        a, b = own.shape
        view = lambda t: t.reshape(-1, b)
        reduced.append(_sum_rows("sum_chips_" + n, [view(own)] + [view(from_chips[i][p]) for p in range(N_PEERS)]).reshape(a, b))
    other_reduced = swap_sibling("swap_reduced", reduced)
    for i, n in enumerate(BIG):
        grads[n] = jnp.stack([jnp.where(ic == l, reduced[i], other_reduced[i]) for l in range(depth)])

    delta, new_m, new_v = {}, {}, {}
    large = BIG + ("w_ada",)
    for n in large:
        delta[n], new_m[n], new_v[n] = adamw("adamw_" + n, given[n], grads[n], given["m_" + n], given["v_" + n])
    rest = [n for n in WEIGHTS if n not in large]
    rest_shapes = [given[n].shape for n in rest]
    packed = [_pack([src[pre + n] for n in rest], SUBLANES, cols=LANES)
              for src, pre in ((given, ""), (grads, ""), (given, "m_"), (given, "v_"))]
    for res, o in zip((delta, new_m, new_v), adamw("adamw_small", *packed)):
        for n, a in zip(rest, _unpack(o.reshape(-1), rest_shapes)):
            res[n] = a
    return (loss, grad_x[None], *[grads[n] for n in WEIGHTS], *[delta[n] for n in WEIGHTS],
            *[new_m[n] for n in WEIGHTS], *[new_v[n] for n in WEIGHTS])
```

```python
import math

import numpy as np
import jax
import jax.numpy as jnp
from jax import lax
from jax.experimental import pallas as pl
from jax.experimental.pallas import tpu as pltpu

F32 = jnp.float32
MXU_DTYPE = jnp.bfloat16

DN_HEADS = 4
HEAD_DIM = 128
DN_WIDTH = DN_HEADS * HEAD_DIM
DN_CHUNK = 64
GRID_W = 64
EPS = 1e-6
POOL_WINDOWS = (2, 4, 8, 16)
N_DIRHEAD = 2 * DN_HEADS
ADAM_LR, ADAM_B1, ADAM_B2, ADAM_EPS, ADAM_WD, ADAM_STEP = 0.001, 0.9, 0.999, 1e-08, 0.01, 10

LANES = 128
SUBLANES = 8
ROW_TILE = 256
VMEM_LIMIT = 56 * 1024 * 1024

MESH_ID = pl.DeviceIdType.MESH
NN, NT, TN = ((1,), (0,)), ((1,), (1,)), ((0,), (0,))


def _params(sem=None):
    return pltpu.CompilerParams(dimension_semantics=sem, vmem_limit_bytes=VMEM_LIMIT)


def _pick(n, cands):
    for c in cands:
        if c <= n and n % c == 0:
            return c
    return n


def _mm(name, a, b, dims, out_dtype=F32):
    if dims == NN:
        (m, kk), (_, n) = a.shape, b.shape
    elif dims == NT:
        (m, kk), (n, _) = a.shape, b.shape
    else:
        (kk, m), (_, n) = a.shape, b.shape
    tm = _pick(m, (768, 1024, 1408, 512, 256, 128, 64, 32, 16, 8))
    tn = _pick(n, ((3072, 2816) if dims == NN else ()) + (1536, 1024, 1408, 1664, 768, 896, 512, 256, 128))
    tk = kk if dims == NN and kk <= 2816 else _pick(kk, ((2112,) if dims == TN else ()) + (1024, 1408, 768, 512, 256, 128))
    gi, gj, gl = m // tm, n // tn, kk // tk
    if dims == NN:
        a_spec = pl.BlockSpec((tm, tk), lambda i, j, l: (i, l))
        b_spec = pl.BlockSpec((tk, tn), lambda i, j, l: (l, j))
    elif dims == NT:
        a_spec = pl.BlockSpec((tm, tk), lambda i, j, l: (i, l))
        b_spec = pl.BlockSpec((tn, tk), lambda i, j, l: (j, l))
    else:
        a_spec = pl.BlockSpec((tk, tm), lambda i, j, l: (l, i))
        b_spec = pl.BlockSpec((tk, tn), lambda i, j, l: (l, j))
    direct = gl == 1
    use_acc = (not direct) and out_dtype != F32

    def body(a_ref, b_ref, o_ref, *scratch):
        part = lax.dot_general(a_ref[...].astype(MXU_DTYPE), b_ref[...].astype(MXU_DTYPE), (dims, ((), ())),
                               preferred_element_type=F32)
        if direct:
            o_ref[...] = part.astype(out_dtype)
            return
        acc = scratch[0] if use_acc else o_ref
        l = pl.program_id(2)

        @pl.when(l == 0)
        def _():
            acc[...] = part

        @pl.when(l > 0)
        def _():
            acc[...] += part

        if use_acc:
            @pl.when(l == gl - 1)
            def _():
                o_ref[...] = acc[...].astype(out_dtype)

    return pl.pallas_call(
        body, grid=(gi, gj, gl), in_specs=[a_spec, b_spec],
        out_specs=pl.BlockSpec((tm, tn), lambda i, j, l: (i, j)),
        out_shape=jax.ShapeDtypeStruct((m, n), out_dtype),
        scratch_shapes=[pltpu.VMEM((tm, tn), F32)] if use_acc else [],
        compiler_params=_params(("parallel", "parallel", "arbitrary")), name=name,
    )(a, b)


def mm(name, a, w, carrier):
    @jax.custom_vjp
    def f(a, w, carrier):
        return _mm(name + "_f", a, w, NN, out_dtype=MXU_DTYPE)

    def fwd(a, w, carrier):
        return f(a, w, carrier), (a, w)

    def bwd(res, dc):
        a, w = res
        da = _mm_nt_sum(name + "_da", [dc], [w], a.dtype)
        dw = _mm(name + "_dw", a, dc, TN)
        return da, None, dw

    f.defvjp(fwd, bwd)
    return f(a, w, carrier)


def _mm_nt_sum(name, dcs, ws, out_dtype):
    m, kk = dcs[0].shape[0], ws[0].shape[0]
    tm = _pick(m, (768, 1024, 512, 256, 128, 64, 32, 16, 8))
    tns = [_pick(w.shape[1], (2816, 1664, 1536, 1408, 1024, 768, 896, 512, 256, 128)) for w in ws]
    counts = [w.shape[1] // tn for w, tn in zip(ws, tns)]
    starts = [sum(counts[:g]) for g in range(len(ws))]
    steps = sum(counts)

    def col(g):
        return lambda i, t: jnp.clip(t - starts[g], 0, counts[g] - 1)

    def body(*refs):
        dc_refs, w_refs, o_ref, acc = refs[:len(ws)], refs[len(ws):2 * len(ws)], refs[-2], refs[-1]
        t = pl.program_id(1)

        @pl.when(t == 0)
        def _():
            acc[...] = jnp.zeros_like(acc)

        for g in range(len(ws)):
            @pl.when(jnp.logical_and(t >= starts[g], t < starts[g] + counts[g]))
            def _():
                acc[...] += lax.dot_general(dc_refs[g][...].astype(MXU_DTYPE), w_refs[g][...].astype(MXU_DTYPE),
                                            (NT, ((), ())), preferred_element_type=F32)

        @pl.when(t == steps - 1)
        def _():
            o_ref[...] = acc[...].astype(out_dtype)

    dc_specs = [pl.BlockSpec((tm, tns[g]), (lambda c: lambda i, t: (i, c(i, t)))(col(g))) for g in range(len(ws))]
    w_specs = [pl.BlockSpec((kk, tns[g]), (lambda c: lambda i, t: (0, c(i, t)))(col(g))) for g in range(len(ws))]
    return pl.pallas_call(
        body, grid=(m // tm, steps), in_specs=dc_specs + w_specs,
        out_specs=pl.BlockSpec((tm, kk), lambda i, t: (i, 0)),
        out_shape=jax.ShapeDtypeStruct((m, kk), out_dtype),
        scratch_shapes=[pltpu.VMEM((tm, kk), F32)],
        compiler_params=_params(("parallel", "arbitrary")), name=name,
    )(*dcs, *ws)


def mm_fanout(name, a, ws, carriers):
    n = len(ws)

    @jax.custom_vjp
    def f(a, ws, carriers):
        return tuple(_mm(f"{name}{g}_f", a, ws[g], NN, out_dtype=MXU_DTYPE) for g in range(n))

    def fwd(a, ws, carriers):
        return f(a, ws, carriers), (a, ws)

    def bwd(res, dcs):
        a, ws = res
        da = _mm_nt_sum(name + "_da", list(dcs), list(ws), a.dtype)
        dws = tuple(_mm(f"{name}{g}_dw", a, dcs[g], TN) for g in range(n))
        return da, None, dws

    f.defvjp(fwd, bwd)
    return f(a, tuple(ws), tuple(carriers))


def _split(x, parts):
    w = x.shape[-1] // parts
    return [x[:, k * w:(k + 1) * w] for k in range(parts)]


def _cat(xs):
    return xs[0] if len(xs) == 1 else jnp.concatenate(xs, axis=-1)


def _shift_rows(x, prev_ref, next_ref, i, ctx_tiles, nt):
    tr = x.shape[0]
    rid = lax.broadcasted_iota(jnp.int32, x.shape, 0)
    first = jnp.logical_or(i == 0, i == ctx_tiles)
    last = jnp.logical_or(i == ctx_tiles - 1, i == nt - 1)
    prow = jnp.where(first, 0.0, prev_ref[SUBLANES - 1:SUBLANES, :].astype(F32))
    nrow = jnp.where(last, 0.0, next_ref[0:1, :].astype(F32))
    xm = jnp.where(rid == 0, prow, pltpu.roll(x, 1, 0))
    xp = jnp.where(rid == tr - 1, nrow, pltpu.roll(x, tr - 1, 0))
    return xm, xp


def rowwise(name, fn, rows, vecs, *, halo=(), seg=(), parts_r=None, parts_v=None, outs=(), reds=(), ctx_tiles=1, skip=None,
            long_tiles=0):
    nr, nv = len(rows), len(vecs)
    halo = tuple(halo) or (False,) * nr
    seg = tuple(seg) or (False,) * nv
    skip = tuple(skip or (0,) * nr)
    parts_r = tuple(parts_r or (1,) * nr)
    parts_v = tuple(parts_v or (1,) * nv)
    r_total = rows[0].shape[0]
    tr = _pick(r_total, (long_tiles, 768)) if long_tiles else ROW_TILE
    assert not long_tiles or not (any(halo) or any(seg) or any(skip))
    nt = r_total // tr
    assert r_total % tr == 0 and (ctx_tiles > 0 or not any(seg)) and not any(h and s for h, s in zip(halo, skip))

    def tile_map(i):
        return lambda t: (jnp.maximum(t - skip[i], 0), 0)

    def row_specs():
        sp = []
        for i, r in enumerate(rows):
            w = r.shape[1]
            sp.append(pl.BlockSpec((tr, w), tile_map(i)))
            if halo[i]:
                k = tr // SUBLANES
                sp.append(pl.BlockSpec((SUBLANES, w), lambda t: (jnp.maximum(t * k - 1, 0), 0)))
                sp.append(pl.BlockSpec((SUBLANES, w), lambda t: (jnp.minimum((t + 1) * k, nt * k - 1), 0)))
        return sp

    def vec_spec(j):
        w = vecs[j].shape[-1]
        if seg[j]:
            return pl.BlockSpec((None, 1, w), lambda t: (jnp.where(t >= ctx_tiles, 1, 0), 0, 0))
        return pl.BlockSpec((1, w), lambda t: (0, 0))

    def row_args(rv):
        a = []
        for i in range(nr):
            a += [rv[i]] * 3 if halo[i] else [rv[i]]
        return a

    def load(refs, t):
        pos, rp = 0, []
        for i in range(nr):
            x = refs[pos][...].astype(F32)
            if halo[i]:
                xm, xp = _shift_rows(x, refs[pos + 1], refs[pos + 2], t, ctx_tiles, nt)
                rp.append(list(zip(_split(x, parts_r[i]), _split(xm, parts_r[i]), _split(xp, parts_r[i]))))
                pos += 3
            else:
                rp.append(_split(x, parts_r[i]))
                pos += 1
        vp = []
        for j in range(nv):
            vp.append(_split(refs[pos][...].astype(F32), parts_v[j]))
            pos += 1
        return rp, vp, refs[pos:]

    n_out, n_red = len(outs), len(reds)

    def fwd_call(*rv):
        def body(*refs):
            t = pl.program_id(0)
            rp, vp, rest = load(refs, t)
            o_parts, r_parts = fn(rp, vp)
            for k in range(n_out):
                rest[k][...] = _cat(o_parts[k]).astype(outs[k][1])
            for k in range(n_red):
                ref = rest[n_out + k]

                @pl.when(t == 0)
                def _():
                    ref[...] = r_parts[k]

                @pl.when(t > 0)
                def _():
                    ref[...] += r_parts[k]

        res = pl.pallas_call(
            body, grid=(nt,),
            in_specs=row_specs() + [vec_spec(j) for j in range(nv)],
            out_specs=[pl.BlockSpec((tr, o[0]), lambda t: (t, 0)) for o in outs]
            + [pl.BlockSpec((1, w), lambda t: (0, 0)) for w in reds],
            out_shape=[jax.ShapeDtypeStruct((r_total, o[0]), o[1]) for o in outs]
            + [jax.ShapeDtypeStruct((1, w), F32) for w in reds],
            compiler_params=_params(("arbitrary",)), name=name + "_f",
        )(*row_args(rv), *rv[nr:])
        return tuple(res)

    def bwd_call(rv, cts):
        def body(*refs):
            t = pl.program_id(0)
            rp, vp, rest = load(refs, t)
            ct_o = [_split(rest[k][...].astype(F32), outs[k][2]) for k in range(n_out)]
            ct_r = [rest[n_out + k][...] for k in range(n_red)]
            rest = rest[n_out + n_red:]
            _, vjp = jax.vjp(fn, rp, vp)
            d_rp, d_vp = vjp((ct_o, ct_r))
            pos = 0
            for i in range(nr):
                if halo[i]:
                    for c in range(3):
                        rest[pos + c][...] = _cat([p[c] for p in d_rp[i]])
                    pos += 3
                else:
                    rest[pos][...] = _cat(d_rp[i]).astype(rows[i].dtype)
                    pos += 1
            for j in range(nv):
                ref, val = rest[pos + j], _cat(d_vp[j])
                start = jnp.logical_or(t == 0, t == ctx_tiles) if seg[j] else t == 0

                @pl.when(start)
                def _():
                    ref[...] = val

                @pl.when(jnp.logical_not(start))
                def _():
                    ref[...] += val

        d_row_specs, d_row_shapes = [], []
        for i, r in enumerate(rows):
            w = r.shape[1]
            for _ in range(3 if halo[i] else 1):
                d_row_specs.append(pl.BlockSpec((tr, w), tile_map(i)))
                d_row_shapes.append(jax.ShapeDtypeStruct(r.shape, F32 if halo[i] else r.dtype))
        res = pl.pallas_call(
            body, grid=(nt,),
            in_specs=row_specs() + [vec_spec(j) for j in range(nv)]
            + [pl.BlockSpec((tr, o[0]), lambda t: (t, 0)) for o in outs]
            + [pl.BlockSpec((1, w), lambda t: (0, 0)) for w in reds],
            out_specs=d_row_specs + [vec_spec(j) for j in range(nv)],
            out_shape=d_row_shapes + [jax.ShapeDtypeStruct(v.shape, F32) for v in vecs],
            compiler_params=_params(("arbitrary",)), name=name + "_b",
        )(*row_args(rv), *rv[nr:], *cts)
        d_rows, pos = [], 0
        for i in range(nr):
            if halo[i]:
                d_rows.append(_unshift(res[pos], res[pos + 1], res[pos + 2], ctx_tiles * tr).astype(rows[i].dtype))
                pos += 3
            else:
                d_rows.append(res[pos])
                pos += 1
        return tuple(d_rows) + tuple(res[pos:])

    @jax.custom_vjp
    def f(*rv):
        return fwd_call(*rv)

    f.defvjp(lambda *rv: (fwd_call(*rv), rv), lambda rv, cts: bwd_call(rv, cts))
    return f(*rows, *vecs)


def _unshift(d, dm, dp, ctx_rows):
    r = d.shape[0]
    t = lax.broadcasted_iota(jnp.int32, (r, 1), 0)
    zero = jnp.zeros((1, d.shape[1]), d.dtype)
    from_m = jnp.concatenate([dm[1:], zero], axis=0)
    from_p = jnp.concatenate([zero, dp[:-1]], axis=0)
    from_m = jnp.where(t == ctx_rows - 1, 0.0, from_m)
    from_p = jnp.where(t == ctx_rows, 0.0, from_p)
    return d + from_m + from_p


def _sigmoid(x):
    return 0.5 * (jnp.tanh(0.5 * x) + 1.0)


def _silu(x):
    return x * _sigmoid(x)


def _softplus(x):
    return jnp.maximum(x, 0.0) + jnp.log(1.0 + jnp.exp(-jnp.abs(x)))


def _rms(x, g):
    return x * lax.rsqrt(jnp.mean(x * x, axis=-1, keepdims=True) + EPS) * g


def _fn_modulate(r, v):
    (x,), (g,), (sh,), (sc,) = r[0], v[0], v[1], v[2]
    return [[_rms(x, g) * (1.0 + sc) + sh]], []


def _fn_resmod(r, v):
    (x,), (y,) = r
    (gate,), (g,), (sh,), (sc,) = v
    xn = x + gate * y
    return [[xn], [_rms(xn, g) * (1.0 + sc) + sh]], []


def _fn_final(r, v):
    (x,), (y,), (tgt,) = r
    (gate,), (g,), (counts,) = v
    err = _rms(x + gate * y, g) - tgt
    row_loss = jnp.mean(err * err, axis=-1, keepdims=True)
    total = 0.5 * jnp.sum(row_loss, axis=0, keepdims=True)
    return [], [total * counts]


def _fn_conv(r, v):
    (x, xm, xp), = r[0]
    (w0,), (w1,), (w2,) = v
    return [[xm * w0 + x * w1 + xp * w2]], []


def _fn_conv_taps(r, v):
    (dy,), ((x, xm, xp),) = r
    col = lambda a: jnp.sum(dy * a, axis=0, keepdims=True)
    return [], [col(xm), col(x), col(xp)]


def conv3(name, x, w0, w1, w2, ctx_tiles, out_dtype):
    width = x.shape[1]

    def conv(tag, x, taps, dtype):
        return rowwise(name + tag, _fn_conv, [x], list(taps), halo=(True,), outs=[(width, dtype, 1)], ctx_tiles=ctx_tiles)[0]

    @jax.custom_vjp
    def f(x, w0, w1, w2):
        return conv("_y", x, (w0, w1, w2), out_dtype)

    def bwd(res, dy):
        x, w0, w1, w2 = res
        taps = rowwise(name + "_dw", _fn_conv_taps, [dy, x], [], halo=(False, True), reds=[width] * 3, ctx_tiles=ctx_tiles)
        return (conv("_dx", dy, (w2, w1, w0), x.dtype),) + tuple(taps)

    f.defvjp(lambda x, w0, w1, w2: (conv("_y", x, (w0, w1, w2), out_dtype), (x, w0, w1, w2)), bwd)
    return f(x, w0, w1, w2)


def _fn_dnprep(r, v):
    qkv, (ab,) = r
    (alog,), (dtb,) = v
    out = [[], [], []]
    for n, x in enumerate(qkv):
        which = n // DN_HEADS
        y = _silu(x)
        if which < 2:
            y = y * lax.rsqrt(jnp.sum(y * y, axis=-1, keepdims=True) + EPS)
        if which == 0:
            y = y * (HEAD_DIM ** -0.5)
        out[which].append(y)
    lane = lax.broadcasted_iota(jnp.int32, ab.shape, 1)
    g = -jnp.exp(alog) * _softplus(ab + dtb)
    gb = jnp.where(lane < N_DIRHEAD, g, jnp.where(lane < 2 * N_DIRHEAD, _sigmoid(ab), 0.0))
    return out + [[gb]], []


def _fn_dnpost(r, v):
    of, ob, z = r
    (g,) = v[0]
    return [[_rms(a + b, g) * _silu(c) for a, b, c in zip(of, ob, z)]], []


def _fn_sub(r, v):
    return [[r[0][0] - r[1][0]]], []


def _fn_scale(r, v):
    return [[r[0][0] * v[0][0]]], []


def _fn_shortconv(r, v):
    (xin, gb, gc), = r
    (w0,), (w1,), (w2,) = v
    u, um, up = (gc[k] * xin[k] for k in range(3))
    return [[gb[0] * (um * w0 + u * w1 + up * w2)]], []


def _fn_merge(r, v):
    gates, (ya,), (yb,), (yc,) = r
    return [[_sigmoid(gates[0]) * ya + _sigmoid(gates[1]) * yb + _sigmoid(gates[2]) * yc]], []


def _fn_swiglu(r, v):
    gate, up = r[0]
    return [[_silu(gate) * up]], []


def _dot3(a, b):
    (ah, al), (bh, bl) = _hi_lo(a), _hi_lo(b)
    dot = lambda x, y: lax.dot_general(x, y, (NN, ((), ())), preferred_element_type=F32)
    return dot(ah, bh) + (dot(ah, bl) + dot(al, bh))


def _bdot(a, b, dims):
    (ca,), (cb,) = dims
    return lax.dot_general(a.astype(MXU_DTYPE), b.astype(MXU_DTYPE), (((ca + 1,), (cb + 1,)), ((0,), (0,))),
                           preferred_element_type=F32)


def _hi_lo(a):
    hi = a.astype(MXU_DTYPE)
    return hi, (a - hi.astype(F32)).astype(MXU_DTYPE)


def _bdot3_raw(a, b, dims):
    (ah, al), (bh, bl) = _hi_lo(a), _hi_lo(b)
    return _bdot(ah, bh, dims) + (_bdot(ah, bl, dims) + _bdot(al, bh, dims))


@jax.custom_vjp
def _bdot3(a, b):
    return _bdot3_raw(a, b, NN)


_bdot3.defvjp(lambda a, b: (_bdot3_raw(a, b, NN), (a, b)),
              lambda res, ct: (_bdot3_raw(ct, res[1], NT), _bdot3_raw(res[0], ct, TN)))


def _inv_doubling(a):
    c = a.shape[-1]
    ii, jj = (lax.broadcasted_iota(jnp.int32, a.shape, d) for d in (1, 2))
    t = jnp.where(ii == jj, 1.0, 0.0) - a
    p = _bdot3_raw(a, a, NN)
    for _ in range(int(math.log2(c)) - 2):
        both = _bdot3_raw(jnp.concatenate([p, t], axis=1), p, NN)
        t = t + both[:, c:]
        p = both[:, :c]
    return t + _bdot3_raw(t, p, NN)


def _dn_gates(k4, gb):
    nb, c = N_DIRHEAD, k4.shape[1]
    k = jnp.concatenate([k4, k4], axis=0)
    lane = lax.broadcasted_iota(jnp.int32, gb.shape, 1)
    col = lambda j: jnp.sum(jnp.where(lane == j, gb, 0.0), axis=1, keepdims=True)
    g_col = jnp.concatenate([col(j)[None] for j in range(nb)], axis=0)
    b_col = jnp.concatenate([col(nb + j)[None] for j in range(nb)], axis=0)
    bi, ii, jj = (lax.broadcasted_iota(jnp.int32, (nb, c, c), a) for a in range(3))
    ahead = jnp.where(bi >= DN_HEADS, jj - ii, ii - jj)
    incl = ahead >= 0
    g_row = jnp.sum(jnp.where(ahead == 0, g_col, 0.0), axis=1, keepdims=True)
    gc_col = jnp.sum(jnp.where(incl, g_row, 0.0), axis=2, keepdims=True)
    gc_row = jnp.sum(jnp.where(ahead <= 0, g_col, 0.0), axis=1, keepdims=True)
    decay = jnp.where(incl, jnp.exp(jnp.where(incl, gc_col - gc_row, 0.0)), 0.0)
    return k, g_col, b_col, gc_col, decay, ahead > 0


def _dn_a(k4, gb):
    k, _, b_col, _, decay, strict = _dn_gates(k4, gb)
    return _bdot(k * b_col, k, NT) * jnp.where(strict, decay, 0.0)


def _dn_operands(q4, k4, v4, gb, t):
    k, g_col, b_col, gc_col, decay, _ = _dn_gates(k4, gb)
    q, v = (jnp.concatenate([a, a], axis=0) for a in (q4, v4))
    e_gc = jnp.exp(gc_col)
    d = k.shape[-1]
    uw = _bdot3(t, jnp.concatenate([v * b_col, k * b_col * e_gc], axis=2))
    u, w = uw[:, :, :d], uw[:, :, d:]
    g_last = jnp.sum(g_col, axis=1, keepdims=True)
    k_state = k * jnp.exp(g_last - gc_col)
    a_qk = _bdot(q, k, NT) * decay
    return u, w, q * e_gc, k_state, a_qk, jnp.broadcast_to(jnp.exp(g_last), (N_DIRHEAD, 1, LANES))


def _dn_step(s, u, w, qd, ks, aqk, gl):
    c = u.shape[1]
    on_state = _bdot(jnp.concatenate([w, qd], axis=1), s, NN)
    v_new = u - on_state[:, :c]
    o = on_state[:, c:] + _bdot(aqk, v_new, NN)
    return s * gl[:, :, :1] + _bdot(ks, v_new, TN), o


def _heads(x):
    return jnp.concatenate([x[None, :, h * HEAD_DIM:(h + 1) * HEAD_DIM] for h in range(DN_HEADS)], axis=0)


def _unheads(x):
    return jnp.concatenate([x[h] for h in range(x.shape[0])], axis=-1)


def _dn_rev(t, nc, n):
    return jnp.where(t < nc, nc - 1 - t, n - 1 - (t - nc))


PRE_CHUNKS = 4


def _pre_shapes(n):
    c, d, h = DN_CHUNK, HEAD_DIM, DN_HEADS
    shapes = [(n, h, c, d)] * 8 + [(n, h, c, c)] * 2 + [(n, h, 1, LANES)] * 2
    return shapes, [pl.BlockSpec((PRE_CHUNKS,) + s[1:], lambda t: (t, 0, 0, 0)) for s in shapes]


def _inverse_spec(n):
    shape = (n, N_DIRHEAD, DN_CHUNK, DN_CHUNK)
    return shape, pl.BlockSpec((PRE_CHUNKS,) + shape[1:], lambda t: (t, 0, 0, 0))


def _pre_row_specs():
    rows = PRE_CHUNKS * DN_CHUNK
    return pl.BlockSpec((rows, DN_HEADS * HEAD_DIM), lambda t: (t, 0)), pl.BlockSpec((rows, LANES), lambda t: (t, 0))


def _dn_pre_fwd(name, q, k, v, gb):
    n = q.shape[0] // DN_CHUNK
    shapes, specs = _pre_shapes(n)
    t_shape, t_spec = _inverse_spec(n)

    def body(q_ref, k_ref, v_ref, g_ref, *o_refs):
        for s in range(PRE_CHUNKS):
            rows = pl.ds(s * DN_CHUNK, DN_CHUNK)
            q4, k4, v4, gb_ = _heads(q_ref[rows, :]), _heads(k_ref[rows, :]), _heads(v_ref[rows, :]), g_ref[rows, :]
            t = _inv_doubling(_dn_a(k4, gb_))
            for i, r in enumerate(_dn_operands(q4, k4, v4, gb_, t)):
                o_refs[2 * i][s] = r[:DN_HEADS]
                o_refs[2 * i + 1][s] = r[DN_HEADS:]
            o_refs[-1][s] = t

    wide, narrow = _pre_row_specs()
    return pl.pallas_call(body, grid=(n // PRE_CHUNKS,), in_specs=[wide] * 3 + [narrow], out_specs=specs + [t_spec],
                          out_shape=[jax.ShapeDtypeStruct(s, F32) for s in shapes + [t_shape]],
                          compiler_params=_params(("parallel",)), name=name + "_pre_f")(q, k, v, gb)


def _dn_pre_bwd(name, q, k, v, gb, inv, cts):
    n = q.shape[0] // DN_CHUNK
    _, specs = _pre_shapes(n)
    _, t_spec = _inverse_spec(n)
    n_ct = len(specs)

    def body(q_ref, k_ref, v_ref, g_ref, t_ref, *refs):
        for s in range(PRE_CHUNKS):
            rows = pl.ds(s * DN_CHUNK, DN_CHUNK)
            ct = tuple(jnp.concatenate([refs[i][s], refs[i + 1][s]], axis=0) for i in range(0, n_ct, 2))
            q4, k4, v4, gb_, t = _heads(q_ref[rows, :]), _heads(k_ref[rows, :]), _heads(v_ref[rows, :]), g_ref[rows, :], t_ref[s]
            _, vjp = jax.vjp(_dn_operands, q4, k4, v4, gb_, t)
            dq, dk, dv, dg, dt = vjp(ct)
            da = -_bdot3_raw(_bdot3_raw(t, dt, TN), t, NT)
            _, vjp_a = jax.vjp(_dn_a, k4, gb_)
            dk_a, dg_a = vjp_a(da)
            for r, val in zip(refs[n_ct:], (_unheads(dq), _unheads(dk + dk_a), _unheads(dv), dg + dg_a)):
                r[rows, :] = val

    wide, narrow = _pre_row_specs()
    return pl.pallas_call(body, grid=(n // PRE_CHUNKS,), in_specs=[wide] * 3 + [narrow, t_spec] + specs,
                          out_specs=[wide] * 3 + [narrow],
                          out_shape=[jax.ShapeDtypeStruct(q.shape, F32)] * 3 + [jax.ShapeDtypeStruct(gb.shape, F32)],
                          compiler_params=_params(("parallel",)), name=name + "_pre_b")(q, k, v, gb, inv, *cts)


SCAN_CHUNKS = 4


def _scan_specs(pre_shapes, fw, bw):
    maps = (lambda t: (fw(t), 0, 0, 0), lambda t: (bw(t), 0, 0, 0))
    return [pl.BlockSpec((SCAN_CHUNKS,) + s[1:], maps[i % 2]) for i, s in enumerate(pre_shapes)]


def _dn_scan_fwd(name, pre, nc):
    n = pre[0].shape[0]
    r = n * DN_CHUNK
    hd, nh, nb, sc = HEAD_DIM, DN_HEADS, N_DIRHEAD, SCAN_CHUNKS
    assert n % sc == 0 and nc % sc == 0
    shapes, _ = _pre_shapes(n)
    n_in = len(shapes)

    def body(*refs):
        ins, (of_ref, ob_ref, sf_ref, sb_ref, s_scr) = refs[:n_in], refs[n_in:]
        t = pl.program_id(0)

        @pl.when(t == 0)
        def _():
            s_scr[...] = jnp.zeros_like(s_scr)

        s = s_scr[...]
        for j in range(sc):
            jf, jb = j, sc - 1 - j
            sf_ref[jf] = s[:nh]
            sb_ref[jb] = s[nh:]
            args = [jnp.concatenate([ins[i][jf], ins[i + 1][jb]], axis=0) for i in range(0, n_in, 2)]
            s, o = _dn_step(s, *args)
            of_ref[pl.ds(jf * DN_CHUNK, DN_CHUNK), :] = _unheads(o[:nh])
            ob_ref[pl.ds(jb * DN_CHUNK, DN_CHUNK), :] = _unheads(o[nh:])
        s_scr[...] = s

    fw = lambda t: t
    bw = lambda t: _dn_rev(sc * t + sc - 1, nc, n) // sc
    wide, st = (sc * DN_CHUNK, nh * hd), (sc, nh, hd, hd)
    return pl.pallas_call(
        body, grid=(n // sc,), in_specs=_scan_specs(shapes, fw, bw),
        out_specs=[pl.BlockSpec(wide, lambda t: (fw(t), 0)), pl.BlockSpec(wide, lambda t: (bw(t), 0)),
                   pl.BlockSpec(st, lambda t: (fw(t), 0, 0, 0)), pl.BlockSpec(st, lambda t: (bw(t), 0, 0, 0))],
        out_shape=[jax.ShapeDtypeStruct((r, nh * hd), F32)] * 2 + [jax.ShapeDtypeStruct((n, nh, hd, hd), F32)] * 2,
        scratch_shapes=[pltpu.VMEM((nb, hd, hd), F32)],
        compiler_params=_params(("arbitrary",)), name=name + "_scan_f",
    )(*pre)


def _dn_scan_bwd(name, pre, sall_f, sall_b, do_f, do_b, nc):
    n = pre[0].shape[0]
    hd, nh, nb, sc = HEAD_DIM, DN_HEADS, N_DIRHEAD, SCAN_CHUNKS
    shapes, _ = _pre_shapes(n)
    n_in = len(shapes)

    def body(*refs):
        ins, (sf_ref, sb_ref, dof_ref, dob_ref) = refs[:n_in], refs[n_in:n_in + 4]
        outs, ds_scr = refs[n_in + 4:2 * n_in + 4], refs[2 * n_in + 4]
        t = pl.program_id(0)

        @pl.when(t == 0)
        def _():
            ds_scr[...] = jnp.zeros_like(ds_scr)

        ds = ds_scr[...]
        for j in range(sc):
            jf, jb = sc - 1 - j, j
            args = [jnp.concatenate([ins[i][jf], ins[i + 1][jb]], axis=0) for i in range(0, n_in, 2)]
            s = jnp.concatenate([sf_ref[jf], sb_ref[jb]], axis=0)
            do = jnp.concatenate([_heads(dof_ref[pl.ds(jf * DN_CHUNK, DN_CHUNK), :]),
                                  _heads(dob_ref[pl.ds(jb * DN_CHUNK, DN_CHUNK), :])], axis=0)
            _, vjp = jax.vjp(_dn_step, s, *args)
            cts = vjp((ds, do))
            ds = cts[0]
            for i, ct in enumerate(cts[1:]):
                outs[2 * i][jf] = ct[:nh]
                outs[2 * i + 1][jb] = ct[nh:]
        ds_scr[...] = ds

    fw = lambda t: n // sc - 1 - t
    bw = lambda t: _dn_rev(n - 1 - sc * t, nc, n) // sc
    wide, st = (sc * DN_CHUNK, nh * hd), (sc, nh, hd, hd)
    return tuple(pl.pallas_call(
        body, grid=(n // sc,),
        in_specs=_scan_specs(shapes, fw, bw)
        + [pl.BlockSpec(st, lambda t: (fw(t), 0, 0, 0)), pl.BlockSpec(st, lambda t: (bw(t), 0, 0, 0)),
           pl.BlockSpec(wide, lambda t: (fw(t), 0)), pl.BlockSpec(wide, lambda t: (bw(t), 0))],
        out_specs=_scan_specs(shapes, fw, bw), out_shape=[jax.ShapeDtypeStruct(s, F32) for s in shapes],
        scratch_shapes=[pltpu.VMEM((nb, hd, hd), F32)],
        compiler_params=_params(("arbitrary",)), name=name + "_scan_b",
    )(*pre, sall_f, sall_b, do_f, do_b))


def deltanet(name, q, k, v, gb, ctx_rows):
    nc = ctx_rows // DN_CHUNK

    @jax.custom_vjp
    def pre(q, k, v, gb):
        return tuple(_dn_pre_fwd(name, q, k, v, gb)[:-1])

    def pre_fwd(*a):
        *ops, inv = _dn_pre_fwd(name, *a)
        return tuple(ops), (a, inv)

    pre.defvjp(pre_fwd, lambda res, cts: tuple(_dn_pre_bwd(name, *res[0], res[1], cts)))

    @jax.custom_vjp
    def scan(*ops):
        return tuple(_dn_scan_fwd(name, ops, nc)[:2])

    def scan_fwd(*ops):
        of, ob, sf, sb = _dn_scan_fwd(name, ops, nc)
        return (of, ob), (ops, sf, sb)

    scan.defvjp(scan_fwd, lambda res, cts: _dn_scan_bwd(name, res[0], res[1], res[2], cts[0], cts[1], nc))
    return scan(*pre(q, k, v, gb))


def _box_matrix(l, w):
    lo, hi = w // 2, w - 1 - w // 2
    pos = np.arange(l)
    start, end = np.clip(pos - lo, 0, l), np.clip(pos + hi + 1, 0, l)
    col = np.arange(l)[None, :]
    return ((col >= start[:, None]) & (col < end[:, None])) / (end - start)[:, None].astype(np.float64)


def _pool_matrices(ctx_rows, grid_rows):
    assert ctx_rows == ROW_TILE and ROW_TILE % GRID_W == 0
    ctx = np.stack([_box_matrix(ctx_rows, w) for w in POOL_WINDOWS])
    cols = np.stack([np.kron(np.eye(ROW_TILE // GRID_W), _box_matrix(GRID_W, w)) for w in POOL_WINDOWS])
    rows = np.stack([_box_matrix(grid_rows, w) for w in POOL_WINDOWS])[None]
    return np.stack([ctx, cols]).astype(np.float32), rows.astype(np.float32)


def _pool_apply(name, x, mats, group_w, seg_tiles, lane_tile):
    r, w = x.shape
    nseg, ng, b, _ = mats.shape
    period = ng * group_w
    per = _pick(r // b, (3, 1))

    def body(x_ref, m_ref, o_ref):
        i = pl.program_id(0)
        for s in range(per):
            rows = pl.ds(s * b, b)
            xv = x_ref[rows, :].astype(F32)
            seg = jnp.where(i * per + s >= seg_tiles, 1, 0) if nseg > 1 else 0
            lane = lax.broadcasted_iota(jnp.int32, xv.shape, 1)
            grp = (lane % period) // group_w
            acc = jnp.zeros_like(xv)
            for g in range(ng):
                acc = acc + jnp.where(grp == g, _dot3(m_ref[seg, g], xv), 0.0)
            o_ref[rows, :] = acc

    return pl.pallas_call(
        body, grid=(r // (per * b), w // lane_tile),
        in_specs=[pl.BlockSpec((per * b, lane_tile), lambda i, j: (i, j)),
                  pl.BlockSpec((nseg, ng, b, b), lambda i, j: (0, 0, 0, 0))],
        out_specs=pl.BlockSpec((per * b, lane_tile), lambda i, j: (i, j)),
        out_shape=jax.ShapeDtypeStruct(x.shape, F32),
        compiler_params=_params(("parallel", "parallel")), name=name,
    )(x, mats)


def pool_means(name, u, ctx_rows):
    r, pw = u.shape
    grid_rows = (r - ctx_rows) // GRID_W
    gw = pw // len(POOL_WINDOWS)
    m1, m2 = _pool_matrices(ctx_rows, grid_rows)
    lane_tile = min(2048, GRID_W * pw)

    def apply(x, a1, a2, tag):
        y = _pool_apply(name + tag + "1", x, jnp.asarray(a1), gw, ctx_rows // ROW_TILE, pw)
        lat = y[ctx_rows:].reshape(grid_rows, GRID_W * pw)
        lat = _pool_apply(name + tag + "2", lat, jnp.asarray(a2), gw, 0, lane_tile)
        return jnp.concatenate([y[:ctx_rows], lat.reshape(r - ctx_rows, pw)], axis=0)

    @jax.custom_vjp
    def f(u):
        return apply(u, m1, m2, "_f")

    tr = lambda m: np.ascontiguousarray(np.swapaxes(m, -1, -2))
    f.defvjp(lambda u: (apply(u, m1, m2, "_f"), None), lambda _, ct: (apply(ct, tr(m1), tr(m2), "_b").astype(u.dtype),))
    return f(u)


def _ew(name, fn, *xs):
    shapes = jax.eval_shape(lambda *a: fn(*a), *xs)
    shapes = shapes if isinstance(shapes, (tuple, list)) else (shapes,)

    def body(*refs):
        res = fn(*[r[...] for r in refs[:len(xs)]])
        res = res if isinstance(res, (tuple, list)) else (res,)
        for r, o in zip(res, refs[len(xs):]):
            o[...] = r

    out = pl.pallas_call(body, out_shape=[jax.ShapeDtypeStruct(s.shape, s.dtype) for s in shapes],
                         compiler_params=_params(), name=name)(*xs)
    return out[0] if len(shapes) == 1 else tuple(out)


def _adamw_math(w, g, m, v):
    m2 = ADAM_B1 * m + (1.0 - ADAM_B1) * g
    v2 = ADAM_B2 * v + (1.0 - ADAM_B2) * (g * g)
    m_hat = m2 / (1.0 - ADAM_B1 ** ADAM_STEP)
    v_hat = v2 / (1.0 - ADAM_B2 ** ADAM_STEP)
    delta = -ADAM_LR * (m_hat / (jnp.sqrt(v_hat) + ADAM_EPS) + ADAM_WD * w)
    return delta, m2, v2


def adamw(name, w, g, m, v):
    *lead, r, c = w.shape
    tr = _pick(r, (256, 128, 64, 32, 16, 8))

    def body(w_ref, g_ref, m_ref, v_ref, d_ref, m2_ref, v2_ref):
        d_ref[...], m2_ref[...], v2_ref[...] = _adamw_math(w_ref[...], g_ref[...], m_ref[...], v_ref[...])

    if lead:
        grid, spec = (lead[0], r // tr), pl.BlockSpec((None, tr, c), lambda l, i: (l, i, 0))
    else:
        grid, spec = (r // tr,), pl.BlockSpec((tr, c), lambda i: (i, 0))
    return pl.pallas_call(body, grid=grid, in_specs=[spec] * 4, out_specs=[spec] * 3,
                          out_shape=[jax.ShapeDtypeStruct(w.shape, F32)] * 3,
                          compiler_params=_params(("parallel",) * len(grid)), name=name)(w, g, m, v)


def _sum_rows(name, xs, out_dtype=F32):
    r, c = xs[0].shape
    tr = _pick(r, tuple(t for t in (512, 256, 128, 64, 32, 16, 8) if t * c * 4 <= 2 << 20))

    def body(*refs):
        acc = refs[0][...].astype(F32)
        for ref in refs[1:-1]:
            acc = acc + ref[...].astype(F32)
        refs[-1][...] = acc.astype(out_dtype)

    spec = pl.BlockSpec((tr, c), lambda i: (i, 0))
    return pl.pallas_call(body, grid=(r // tr,), in_specs=[spec] * len(xs), out_specs=spec,
                          out_shape=jax.ShapeDtypeStruct((r, c), out_dtype),
                          compiler_params=_params(("parallel",)), name=name)(*xs)


def _place():
    return lax.axis_index("x"), lax.axis_index("y"), lax.axis_index("c")


def _chip_peers(x, y, c):
    return [(1 - x, y, c), (x, 1 - y, c), (1 - x, 1 - y, c)]


def all_gather8(name, block):
    m_per, n = block.shape

    def body(x_ref, out_ref, send_sems, recv_sems, local_sem):
        x, y, c = _place()
        me, sibling = (x, y, c), (x, y, 1 - c)
        chips = [(1 - x, y), (x, 1 - y), (1 - x, 1 - y)]

        def rows(px, py, pc):
            return out_ref.at[pl.ds((4 * px + 2 * py + pc) * m_per, m_per), :]

        def copy(k, blk, to, src=None):
            return pltpu.make_async_remote_copy(
                src_ref=rows(*blk) if src is None else src, dst_ref=rows(*blk),
                send_sem=send_sems.at[k], recv_sem=recv_sems.at[k], device_id=to, device_id_type=MESH_ID)

        mine = pltpu.make_async_copy(x_ref, rows(*me), local_sem)
        mine.start()
        first = [copy(0, me, sibling, src=x_ref)]
        first += [copy(1 + j, me, (*chip, c), src=x_ref) for j, chip in enumerate(chips)]
        for cp in first:
            cp.start()
        passed = [copy(4 + j, (*chip, c), sibling) for j, chip in enumerate(chips)]
        for j, chip in enumerate(chips):
            copy(1 + j, (*chip, c), me).wait_recv()
            passed[j].start()
        copy(0, sibling, me).wait_recv()
        for j, chip in enumerate(chips):
            copy(4 + j, (*chip, 1 - c), me).wait_recv()
        for cp in first + passed:
            cp.wait_send()
        mine.wait()

    return pl.pallas_call(
        body, out_shape=jax.ShapeDtypeStruct((8 * m_per, n), block.dtype),
        in_specs=[pl.BlockSpec(memory_space=pltpu.VMEM)], out_specs=pl.BlockSpec(memory_space=pltpu.VMEM),
        scratch_shapes=[pltpu.SemaphoreType.DMA((7,)), pltpu.SemaphoreType.DMA((7,)), pltpu.SemaphoreType.DMA],
        compiler_params=_params(), name=name,
    )(block)


N_PEERS = 3


def gather_chips(name, shards):
    n = len(shards)

    def body(*refs):
        xs, outs, (send_sems, recv_sems, local_sems) = refs[:n], refs[n:2 * n], refs[2 * n:]
        x, y, c = _place()
        local, remote = [], []
        for i in range(n):
            local.append(pltpu.make_async_copy(xs[i].at[c], outs[i].at[2 * x + y], local_sems.at[i]))
            local[-1].start()
            for p, peer in enumerate(_chip_peers(x, y, c)):
                remote.append(pltpu.make_async_remote_copy(
                    src_ref=xs[i].at[c], dst_ref=outs[i].at[2 * x + y], send_sem=send_sems.at[i * N_PEERS + p],
                    recv_sem=recv_sems.at[i * N_PEERS + p], device_id=peer, device_id_type=MESH_ID))
                remote[-1].start()
        for i in range(n):
            for p, (px, py, _) in enumerate(_chip_peers(x, y, c)):
                pltpu.make_async_remote_copy(
                    src_ref=xs[i].at[c], dst_ref=outs[i].at[2 * px + py], send_sem=send_sems.at[i * N_PEERS + p],
                    recv_sem=recv_sems.at[i * N_PEERS + p], device_id=(px, py, c), device_id_type=MESH_ID).wait_recv()
        for cp in remote:
            cp.wait_send()
        for cp in local:
            cp.wait()

    hbm = pl.BlockSpec(memory_space=pltpu.HBM)
    return pl.pallas_call(
        body, out_shape=[jax.ShapeDtypeStruct((N_CHIPS,) + s.shape[1:], s.dtype) for s in shards],
        in_specs=[hbm] * n, out_specs=[hbm] * n,
        scratch_shapes=[pltpu.SemaphoreType.DMA((n * N_PEERS,)), pltpu.SemaphoreType.DMA((n * N_PEERS,)),
                        pltpu.SemaphoreType.DMA((n,))],
        compiler_params=_params(), name=name,
    )(*shards)


def swap_sibling(name, blocks):
    n = len(blocks)
    pairs = [tuple(b) if isinstance(b, (tuple, list)) else (b,) for b in blocks]
    flat = [a for p in pairs for a in p]
    first = [sum(len(q) for q in pairs[:i]) for i in range(n)]

    def body(*refs):
        xs, outs, (send_sems, recv_sems) = refs[:len(flat)], refs[len(flat):len(flat) + n], refs[len(flat) + n:]
        x, y, c = _place()

        def copy(i, src):
            return pltpu.make_async_remote_copy(src_ref=src, dst_ref=outs[i], send_sem=send_sems.at[i], recv_sem=recv_sems.at[i],
                                                device_id=(x, y, 1 - c), device_id_type=MESH_ID)

        for i, p in enumerate(pairs):
            if len(p) == 1:
                copy(i, xs[first[i]]).start()
            else:
                for half in range(2):
                    pl.when(c == 1 - half)(copy(i, xs[first[i] + half]).start)
        for i in range(n):
            copy(i, xs[first[i]]).wait()

    hbm = pl.BlockSpec(memory_space=pltpu.HBM)
    return pl.pallas_call(
        body, out_shape=[jax.ShapeDtypeStruct(p[0].shape, p[0].dtype) for p in pairs],
        in_specs=[hbm] * len(flat), out_specs=[hbm] * n,
        scratch_shapes=[pltpu.SemaphoreType.DMA((n,)), pltpu.SemaphoreType.DMA((n,))],
        compiler_params=_params(), name=name,
    )(*flat)


def scatter_chips(name, pieces):
    n = len(pieces)

    def body(*refs):
        xs, outs, (send_sems, recv_sems) = refs[:n], refs[n:2 * n], refs[2 * n:]
        x, y, c = _place()
        copies = []
        for i in range(n):
            for p, (px, py, pc) in enumerate(_chip_peers(x, y, c)):
                copies.append(pltpu.make_async_remote_copy(
                    src_ref=xs[i].at[2 * px + py], dst_ref=outs[i].at[p], send_sem=send_sems.at[i * N_PEERS + p],
                    recv_sem=recv_sems.at[i * N_PEERS + p], device_id=(px, py, pc), device_id_type=MESH_ID))
                copies[-1].start()
        for cp in copies:
            cp.wait()

    hbm = pl.BlockSpec(memory_space=pltpu.HBM)
    return pl.pallas_call(
        body, out_shape=[jax.ShapeDtypeStruct((N_PEERS,) + p.shape[1:], p.dtype) for p in pieces],
        in_specs=[hbm] * n, out_specs=[hbm] * n,
        scratch_shapes=[pltpu.SemaphoreType.DMA((n * N_PEERS,)), pltpu.SemaphoreType.DMA((n * N_PEERS,))],
        compiler_params=_params(), name=name,
    )(*pieces)


def pair_sum(name, half0, half1, got, out_dtype):
    r, w = got.shape
    tr = _pick(r, tuple(t for t in (512, 256, 128, 64, 32, 16, 8) if t * w * 4 <= 2 << 20))

    def body(a0, a1, g, o):
        mine = jnp.where(lax.axis_index("c") == 0, a0[...], a1[...])
        o[...] = (mine.astype(F32) + g[...].astype(F32)).astype(out_dtype)

    spec = pl.BlockSpec((tr, w), lambda i: (i, 0))
    return pl.pallas_call(body, grid=(r // tr,), in_specs=[spec] * 3, out_specs=spec,
                          out_shape=jax.ShapeDtypeStruct((r, w), out_dtype),
                          compiler_params=_params(("parallel",)), name=name)(half0, half1, got)


def _sum_devices(name, got, fold=False):
    def body(a_ref, *o_refs):
        acc = a_ref[0]
        for i in range(1, N_DEV):
            acc = acc + a_ref[i]
        o_refs[0][...] = acc
        if fold:
            o_refs[1][...] = acc + pltpu.roll(acc, SUBLANES // 2, 0)

    shape = jax.ShapeDtypeStruct(got.shape[1:], F32)
    out = pl.pallas_call(body, out_shape=[shape] * (2 if fold else 1), compiler_params=_params(), name=name)(got)
    return out if fold else out[0]


def _w_in_bounds(d, pw, scw):
    off_z = 3 * DN_WIDTH
    off_a = off_z + DN_WIDTH
    off_pool = off_a + 2 * N_DIRHEAD
    off_sc = off_pool + pw
    off_gate = off_sc + 3 * scw
    return (0, off_z, off_a, off_pool, off_sc, off_gate, off_gate + 3 * d)


def _misc_widths(pw, scw):
    return (DN_WIDTH, pw, 3 * scw, LANES)


def split_matrices(full, pw, scw):
    depth, d = len(full["w_in"]), full["w_in"][0].shape[0]
    b = _w_in_bounds(d, pw, scw)
    out = {k: [] for k in ("qkv", "gate", "misc", "gu", "br_a", "br_b", "br_c", "o", "down")}
    for l in range(depth):
        w = full["w_in"][l]
        out["qkv"].append(w[:, b[0]:b[1]])
        out["gate"].append(w[:, b[5]:b[6]])
        out["misc"].append(jnp.concatenate([w[:, b[1]:b[2]], w[:, b[3]:b[5]], _pad_lanes(w[:, b[2]:b[3]])], axis=1))
        out["gu"].append(full["w_gu"][l])
        for k in ("br_a", "br_b", "br_c", "o", "down"):
            out[k].append(full["w_" + k][l])
    return out


def join_matrix_grads(g, pw, scw):
    depth = len(g["qkv"])
    n_z, n_rest, n_ab = DN_WIDTH, pw + 3 * scw, 2 * N_DIRHEAD
    w_in = []
    for l in range(depth):
        m = g["misc"][l]
        w_in.append(jnp.concatenate([g["qkv"][l], m[:, :n_z], m[:, n_z + n_rest:n_z + n_rest + n_ab],
                                     m[:, n_z:n_z + n_rest], g["gate"][l]], axis=1))
    out = {"w_in": w_in, "w_gu": g["gu"]}
    for k in ("br_a", "br_b", "br_c", "o", "down"):
        out["w_" + k] = g[k]
    return {k: [a.astype(MXU_DTYPE) for a in v] for k, v in out.items()}


def split_cols(x, widths):
    edges = np.cumsum((0,) + tuple(widths))

    def cut(x):
        return tuple(x[:, a:b] for a, b in zip(edges[:-1], edges[1:]))

    f = jax.custom_vjp(cut)
    f.defvjp(lambda x: (cut(x), None), lambda _, cts: (jnp.concatenate(cts, axis=1),))
    return f(x)


def _row(v):
    return v.reshape(1, -1)


def _pad_lanes(v, width=LANES):
    return jnp.pad(v, ((0, 0), (0, width - v.shape[1])))


def _block_diag(blocks):
    g, n, _ = blocks.shape
    out = jnp.zeros((g * n, g * n), blocks.dtype)
    for i in range(g):
        out = out.at[i * n:(i + 1) * n, i * n:(i + 1) * n].set(blocks[i])
    return out


def local_loss(p, carriers, x, mod_lat, mod_ctx, wts, ctx, target):
    ctx_rows, d = ctx.shape
    depth = len(wts["qkv"])
    pw, scw = p["pool_scale"].shape[1], p["sc_conv_w"].shape[2]
    ff = wts["down"][0].shape[0]
    ct = ctx_rows // ROW_TILE
    xs = jnp.concatenate([ctx, x], axis=0)
    seg = lambda l, k: jnp.stack([mod_ctx[l, k], mod_lat[l, k]]).reshape(2, 1, d)
    dn = gate2 = None
    for l in range(depth):
        sh1, sc1, g1, sh2, sc2, g2 = (seg(l, k) for k in range(6))
        tag = f"l{l}_"
        lin = lambda key, a: mm(tag + key, a, wts[key][l], carriers[key][l])
        if l == 0:
            (h1,) = rowwise(tag + "mod", _fn_modulate, [xs], [_row(p["norm1_g"][l]), sh1, sc1], seg=(False, True, True),
                            outs=[(d, MXU_DTYPE, 1)], ctx_tiles=ct)
        else:
            xs, h1 = rowwise(tag + "resmod1", _fn_resmod, [xs, dn], [gate2, _row(p["norm1_g"][l]), sh1, sc1],
                             seg=(True, False, True, True), outs=[(d, F32, 1), (d, MXU_DTYPE, 1)], ctx_tiles=ct)
        fan = ("qkv", "gate", "misc")
        p_qkv, p_gate, p_misc = mm_fanout(tag + "in", h1, [wts[k][l] for k in fan], [carriers[k][l] for k in fan])
        p_z, p_pool, p_sc, p_ab = split_cols(p_misc, _misc_widths(pw, scw))
        cw = p["dn_conv_w"][l]
        qkv_conv = conv3(tag + "dnconv", p_qkv, cw[0:1], cw[1:2], cw[2:3], ct, F32)
        q, k, v, gb = rowwise(
            tag + "dnprep", _fn_dnprep, [qkv_conv, p_ab],
            [_pad_lanes(p["dn_a_log"][l].reshape(1, -1)), _pad_lanes(p["dn_dt_bias"][l].reshape(1, -1))],
            parts_r=(3 * DN_HEADS, 1), outs=[(DN_WIDTH, F32, DN_HEADS)] * 3 + [(LANES, F32, 1)], long_tiles=768)
        o_f, o_b = deltanet(tag + "dn", q, k, v, gb, ctx_rows)
        (oz,) = rowwise(tag + "dnpost", _fn_dnpost, [o_f, o_b, p_z], [_row(p["dn_norm_g"][l])], parts_r=(DN_HEADS,) * 3,
                        outs=[(DN_WIDTH, MXU_DTYPE, DN_HEADS)], long_tiles=768)
        y_a = lin("br_a", oz)
        means = pool_means(tag + "box", p_pool, ctx_rows)
        (dpool,) = rowwise(tag + "poolsub", _fn_sub, [means, p_pool], [], outs=[(pw, MXU_DTYPE, 1)], long_tiles=2816)
        pool_mat = _block_diag(p["pool_w"][l])
        mixed = mm(tag + "poolw", dpool, pool_mat.astype(MXU_DTYPE), pool_mat)
        (yb_in,) = rowwise(tag + "poolscale", _fn_scale, [mixed], [_row(p["pool_scale"][l])], outs=[(pw, MXU_DTYPE, 1)],
                           long_tiles=2816)
        y_b = lin("br_b", yb_in)
        sw = p["sc_conv_w"][l]
        (yc_in,) = rowwise(tag + "sconv", _fn_shortconv, [p_sc], [sw[0:1], sw[1:2], sw[2:3]], halo=(True,), parts_r=(3,),
                           outs=[(scw, MXU_DTYPE, 1)], ctx_tiles=ct)
        y_c = lin("br_c", yc_in)
        (y,) = rowwise(tag + "merge", _fn_merge, [p_gate, y_a, y_b, y_c], [], parts_r=(3, 1, 1, 1),
                       outs=[(d, MXU_DTYPE, 1)], long_tiles=768)
        mix = lin("o", y)
        xs, h2 = rowwise(tag + "resmod2", _fn_resmod, [xs, mix], [g1, _row(p["norm2_g"][l]), sh2, sc2],
                         seg=(True, False, True, True), outs=[(d, F32, 1), (d, MXU_DTYPE, 1)], ctx_tiles=ct)
        (act,) = rowwise(tag + "swiglu", _fn_swiglu, [lin("gu", h2)], [], parts_r=(2,),
                         outs=[(ff, MXU_DTYPE, 1)], long_tiles=768)
        dn = lin("down", act)
        gate2 = g2
    counts = jnp.concatenate([jnp.zeros((1, 1, LANES), F32), jnp.ones((1, 1, LANES), F32)])
    (total,) = rowwise("final", _fn_final, [xs, dn, target], [gate2, _row(p["final_norm_g"]), counts],
                       seg=(True, False, True), reds=[LANES], ctx_tiles=ct, skip=(0, 0, ct))
    return total[0, 0]


BIG = ("w_in", "w_br_a", "w_br_b", "w_br_c", "w_o", "w_gu", "w_down")
ROW_SHARDED = ("w_o", "w_down")
SMALL_REPL = ("norm1_g", "norm2_g", "dn_a_log", "dn_dt_bias", "dn_norm_g", "pool_w", "pool_scale", "final_norm_g")
SMALL_SHARD = ("dn_conv_w", "sc_conv_w")
WEIGHTS = ("c_ctx", "w_ada", "b_ada", "norm1_g", "norm2_g", "w_in", "dn_conv_w", "dn_a_log", "dn_dt_bias", "dn_norm_g", "pool_w",
           "pool_scale", "sc_conv_w", "w_br_a", "w_br_b", "w_br_c", "w_o", "w_gu", "w_down", "final_norm_g")
N_CHIPS, N_DEV = 4, 8
COMM_COLS = 1024


def _pack(arrays, rows_multiple, cols=COMM_COLS, dtype=F32):
    size = sum(int(np.prod(a.shape)) for a in arrays)
    rows = -(-size // (cols * rows_multiple)) * rows_multiple
    tail = [jnp.zeros((rows * cols - size,), dtype)] if rows * cols > size else []
    return jnp.concatenate([a.astype(dtype).reshape(-1) for a in arrays] + tail).reshape(rows, cols)


def _unpack(flat, shapes):
    out, pos = [], 0
    for s in shapes:
        n = int(np.prod(s))
        out.append(flat[pos:pos + n].reshape(s))
        pos += n
    return out


def _join_shards(parts, name):
    _, a, b = parts.shape
    if name in ROW_SHARDED:
        return parts.reshape(N_CHIPS * a, b)
    return jnp.moveaxis(parts, 0, 1).reshape(a, N_CHIPS * b)


def _cut_shards(full, name):
    k, n = full.shape
    if name in ROW_SHARDED:
        return full.reshape(N_CHIPS, k // N_CHIPS, n)
    return jnp.moveaxis(full.reshape(k, N_CHIPS, n // N_CHIPS), 1, 0)


def _dsilu(x):
    s = _sigmoid(x)
    return s + x * s * (1.0 - s)


def kernel(x, c, ctx, c_ctx, w_ada, b_ada, norm1_g, norm2_g, w_in, dn_conv_w, dn_a_log, dn_dt_bias, dn_norm_g, pool_w, pool_scale, sc_conv_w, w_br_a, w_br_b, w_br_c, w_o, w_gu, w_down, final_norm_g, loss_target, m_c_ctx, m_w_ada, m_b_ada, m_norm1_g, m_norm2_g, m_w_in, m_dn_conv_w, m_dn_a_log, m_dn_dt_bias, m_dn_norm_g, m_pool_w, m_pool_scale, m_sc_conv_w, m_w_br_a, m_w_br_b, m_w_br_c, m_w_o, m_w_gu, m_w_down, m_final_norm_g, v_c_ctx, v_w_ada, v_b_ada, v_norm1_g, v_norm2_g, v_w_in, v_dn_conv_w, v_dn_a_log, v_dn_dt_bias, v_dn_norm_g, v_pool_w, v_pool_scale, v_sc_conv_w, v_w_br_a, v_w_br_b, v_w_br_c, v_w_o, v_w_gu, v_w_down, v_final_norm_g):
    given = dict(locals())
    ix, iy, ic = _place()
    chip, dev = 2 * ix + iy, 4 * ix + 2 * iy + ic
    d = x.shape[-1]
    depth = w_in.shape[0]
    ada_cols = w_ada.shape[2]
    spare = 2 * SUBLANES - N_DEV - 1

    got0 = all_gather8("gather_cond", _pack([c, dn_conv_w, sc_conv_w], SUBLANES)).reshape(N_DEV, -1)
    c_all = got0[:, :d]
    taps = [_unpack(got0[2 * j, d:], [dn_conv_w.shape, sc_conv_w.shape]) for j in range(N_CHIPS)]
    full = {"dn_conv_w": jnp.concatenate([t[0] for t in taps], axis=-1),
            "sc_conv_w": jnp.concatenate([t[1] for t in taps], axis=-1)}
    assert depth == 2, "the two cores of a chip split the layers between them"
    got_layer = gather_chips("gather_weights", [given[n].astype(MXU_DTYPE) for n in BIG])
    other_layer = swap_sibling("swap_weights", got_layer)
    mats = {n: [_join_shards(jnp.where(ic == l, got_layer[i], other_layer[i]), n) for l in range(depth)]
            for i, n in enumerate(BIG)}
    for n in SMALL_REPL:
        full[n] = given[n]
    pw, scw = pool_scale.shape[1], full["sc_conv_w"].shape[2]
    wts = split_matrices(mats, pw, scw)
    carriers = {k: [jnp.zeros(w.shape, F32) for w in ws] for k, ws in wts.items()}

    cond = jnp.concatenate([c_all, c_ctx[None], jnp.zeros((spare, d), F32)])
    s_cond = _ew("silu_cond", _silu, cond)
    mod_cols = jnp.concatenate([_mm(f"ada{l}_f", s_cond, w_ada[l], NN) for l in range(depth)], axis=1)
    mod_got = all_gather8("gather_mod", mod_cols).reshape(N_DEV, 2 * SUBLANES, depth, ada_cols)
    mod_all = jnp.concatenate([mod_got[2 * j] for j in range(N_CHIPS)], axis=-1) + b_ada[None]
    mod_lat = lax.dynamic_index_in_dim(mod_all, dev, 0, keepdims=False).reshape(depth, 6, d)
    mod_ctx = mod_all[N_DEV].reshape(depth, 6, d)

    loss_local, (g_full, g_mats, grad_x, g_mod_lat, g_mod_ctx) = jax.value_and_grad(local_loss, argnums=(0, 1, 2, 3, 4))(
        full, carriers, x[0], mod_lat, mod_ctx, wts, ctx[0], loss_target[0])
    g_mats_full = join_matrix_grads(g_mats, pw, scw)
    loss = lax.psum(loss_local, ("x", "y", "c"))

    dmod_cols = (2 * depth * 6 * d) // SUBLANES
    dmod = all_gather8("gather_dmod", _pack([g_mod_lat, g_mod_ctx], SUBLANES, cols=dmod_cols))
    dmod = dmod.reshape(N_DEV, SUBLANES, dmod_cols)
    dmod_sum, dmod_fold = _sum_devices("reduce_dmod", dmod, fold=True)
    half_rows = SUBLANES // 2
    grad_b_ada = dmod_fold[:half_rows].reshape(depth, 6 * d)
    dctx_sum = dmod_sum[half_rows:].reshape(1, depth, 6 * d)
    d9 = jnp.concatenate([dmod[:, :half_rows].reshape(N_DEV, depth, 6 * d), dctx_sum, jnp.zeros((spare, depth, 6 * d), F32)])
    d9 = lax.dynamic_slice_in_dim(d9, chip * ada_cols, ada_cols, axis=2)
    grad_w_ada = jnp.stack([_mm(f"ada{l}_dw", s_cond, d9[:, l], TN) for l in range(depth)])
    ds_cond = [_mm(f"ada{l}_da", d9[:, l], w_ada[l], NT) for l in range(depth)]
    dsilu_part = _sum_rows("sum_dcond", ds_cond)[N_DEV]

    small_names = SMALL_REPL + SMALL_SHARD
    small_grads = [dsilu_part] + [g_full[n] for n in small_names]
    small_shapes = [a.shape for a in small_grads]
    n_small = sum(int(np.prod(s)) for s in small_shapes)
    cols = -(-n_small // (SUBLANES * LANES)) * LANES
    got2 = all_gather8("gather_small", _pack(small_grads, SUBLANES, cols=cols)).reshape(N_DEV, SUBLANES, cols)
    small_sum = _unpack(_sum_devices("reduce_small", got2).reshape(-1), small_shapes)
    grads = dict(zip(small_names, small_sum[1:]))
    grads["c_ctx"] = _ew("dsilu", lambda g, z: 0.5 * g * _dsilu(z), _row(small_sum[0]), _row(c_ctx)).reshape(-1)
    grads["w_ada"], grads["b_ada"] = grad_w_ada, grad_b_ada
    for n in SMALL_SHARD:
        width = given[n].shape[-1]
        grads[n] = lax.dynamic_slice_in_dim(grads[n], chip * width, width, axis=-1)

    from_sibling = swap_sibling("swap_layers", [tuple(g_mats_full[n]) for n in BIG])
    pieces = [_cut_shards(pair_sum("sum_pair_" + n, *g_mats_full[n], from_sibling[i], MXU_DTYPE), n) for i, n in enumerate(BIG)]
    from_chips = scatter_chips("scatter_pieces", pieces)
    reduced = []
    for i, n in enumerate(BIG):
        own = lax.dynamic_index_in_dim(pieces[i], chip, 0, keepdims=False)
        a, b = own.shape
        view = lambda t: t.reshape(-1, b)
        reduced.append(_sum_rows("sum_chips_" + n, [view(own)] + [view(from_chips[i][p]) for p in range(N_PEERS)]).reshape(a, b))
    other_reduced = swap_sibling("swap_reduced", reduced)
    for i, n in enumerate(BIG):
        grads[n] = jnp.stack([jnp.where(ic == l, reduced[i], other_reduced[i]) for l in range(depth)])

    delta, new_m, new_v = {}, {}, {}
    large = BIG + ("w_ada",)
    for n in large:
        delta[n], new_m[n], new_v[n] = adamw("adamw_" + n, given[n], grads[n], given["m_" + n], given["v_" + n])
    rest = [n for n in WEIGHTS if n not in large]
    rest_shapes = [given[n].shape for n in rest]
    packed = [_pack([src[pre + n] for n in rest], SUBLANES, cols=LANES)
              for src, pre in ((given, ""), (grads, ""), (given, "m_"), (given, "v_"))]
    for res, o in zip((delta, new_m, new_v), adamw("adamw_small", *packed)):
        for n, a in zip(rest, _unpack(o.reshape(-1), rest_shapes)):
            res[n] = a
    return (loss, grad_x[None], *[grads[n] for n in WEIGHTS], *[delta[n] for n in WEIGHTS],
            *[new_m[n] for n in WEIGHTS], *[new_v[n] for n in WEIGHTS])
```
